```python
import math
import jax, jax.numpy as jnp
from jax import lax
import numpy as np

D_MODEL = 1024
BATCH = 8
SEQ = 8192
DEPTH = 1

CHUNK = 64
EPS = 1e-6
PLE_DIM = 256
D_S5 = D_MODEL
S5_GROUP_CH = 16
S5_GROUPS = D_S5 // S5_GROUP_CH
S5_STATE = 64
D_SSD = D_MODEL
SSD_HEAD_DIM = 64
SSD_HEADS = D_SSD // SSD_HEAD_DIM
SSD_GROUPS = 4
SSD_HEADS_PER_GROUP = SSD_HEADS // SSD_GROUPS
SSD_STATE = 128
CONV_WIDTH = 4
SSD_CONV_DIM = D_SSD + 2 * SSD_GROUPS * SSD_STATE
NORM_GROUP = D_SSD // SSD_GROUPS
MIX_WIDTH = D_S5 + D_SSD
D_IN_PROJ = 2 * D_S5 + D_SSD + SSD_CONV_DIM + SSD_HEADS
SPLITS = (D_S5, 2 * D_S5, 2 * D_S5 + D_SSD, 2 * D_S5 + D_SSD + SSD_CONV_DIM)

kernel_name = "hybrid_s5_ssd_parallel_heads"


def rmsnorm(x, w):
    xf = x.astype(jnp.float32)
    y = xf * lax.rsqrt(jnp.mean(xf * xf, axis=-1, keepdims=True) + EPS)
    return (y * w.astype(jnp.float32)).astype(x.dtype)


def gated_group_rmsnorm(y, z, w):
    g = y.astype(jnp.float32) * jax.nn.silu(z.astype(jnp.float32))
    b, l, d = g.shape
    g = g.reshape(b, l, SSD_GROUPS, NORM_GROUP)
    g = g * lax.rsqrt(jnp.mean(g * g, axis=-1, keepdims=True) + EPS)
    return (g.reshape(b, l, d) * w.astype(jnp.float32)).astype(y.dtype)


def causal_depthwise_conv(x, w, bias):
    k_w = w.shape[0]
    L = x.shape[1]
    xp = jnp.pad(x, ((0, 0), (k_w - 1, 0), (0, 0)))
    return sum(xp[:, k:k + L] * w[k] for k in range(k_w)) + bias


def segsum(a):
    T = a.shape[-1]
    cs = jnp.cumsum(a, axis=-1)
    diff = cs[..., :, None] - cs[..., None, :]
    mask = jnp.tril(jnp.ones((T, T), dtype=bool))
    return jnp.where(mask, diff, -jnp.inf)


def s5_branch(u, A_re, A_im, log_dt, B_re, B_im, C_re, C_im, D, w_glu, b_glu):
    f32 = jnp.float32
    b, L, _ = u.shape
    n_chunks = L // CHUNK
    uf = u.astype(f32)
    u_c = uf.reshape(b, n_chunks, CHUNK, S5_GROUPS, S5_GROUP_CH).transpose(1, 0, 2, 3, 4)
    A = lax.complex(A_re.astype(f32), A_im.astype(f32))
    dt = jnp.exp(log_dt.astype(f32))[:, None]
    A_bar = jnp.exp(A * dt)
    Bc = lax.complex(B_re.astype(f32), B_im.astype(f32))
    B_bar = ((A_bar - 1.0) / A)[..., None] * Bc
    Cc = lax.complex(C_re.astype(f32), C_im.astype(f32))
    steps = jnp.arange(1, CHUNK + 1, dtype=f32)[:, None, None]
    pows = jnp.exp(A[None] * dt[None] * steps)

    def combine(e1, e2):
        a1, b1 = e1
        a2, b2 = e2
        return a1 * a2, a2 * b1 + b2

    def step(h0, uc):
        bu = jnp.einsum('btgh,gnh->btgn', uc, B_bar)
        a = jnp.broadcast_to(A_bar, bu.shape)
        _, local = lax.associative_scan(combine, (a, bu), axis=1)
        states = local + pows * h0[:, None]
        y = jnp.einsum('btgn,ghn->btgh', states, Cc).real
        return states[:, -1], y

    h0 = jnp.zeros((b, S5_GROUPS, S5_STATE), jnp.complex64)
    _, ys = lax.scan(step, h0, u_c)
    y = ys.transpose(1, 0, 2, 3, 4).reshape(b, L, D_S5) + D.astype(f32) * uf
    y = jax.nn.gelu(y)
    y = y * jax.nn.sigmoid(y @ w_glu.astype(f32) + b_glu.astype(f32))
    return y.astype(u.dtype)


def ssd_chunked(xs, dt, A, Bm, Cm):
    b, L, H, P = xs.shape
    c = L // CHUNK
    Bh = jnp.repeat(Bm, SSD_HEADS_PER_GROUP, axis=2).reshape(b, c, CHUNK, H, SSD_STATE)
    Ch = jnp.repeat(Cm, SSD_HEADS_PER_GROUP, axis=2).reshape(b, c, CHUNK, H, SSD_STATE)
    xd = (xs * dt[..., None]).reshape(b, c, CHUNK, H, P)
    a = (dt * A).reshape(b, c, CHUNK, H).transpose(0, 3, 1, 2)
    a_cs = jnp.cumsum(a, axis=-1)
    L_intra = jnp.exp(segsum(a))
    scores = jnp.einsum('bclhn,bcshn->bhcls', Ch, Bh) * L_intra
    y_diag = jnp.einsum('bhcls,bcshp->bclhp', scores, xd)
    decay_states = jnp.exp(a_cs[..., -1:] - a_cs)
    states = jnp.einsum('bclhn,bhcl,bclhp->bchpn', Bh, decay_states, xd)
    states = jnp.concatenate([jnp.zeros_like(states[:, :1]), states], axis=1)
    chunk_a = jnp.pad(a_cs[..., -1], ((0, 0), (0, 0), (1, 0)))
    decay_chunk = jnp.exp(segsum(chunk_a))
    new_states = jnp.einsum('bhzc,bchpn->bzhpn', decay_chunk, states)
    prev_states = new_states[:, :-1]
    y_off = jnp.einsum('bclhn,bchpn,bhcl->bclhp', Ch, prev_states, jnp.exp(a_cs))
    return (y_diag + y_off).reshape(b, L, H, P)


def ssd_branch(xBC, dt_raw, z, conv_w, conv_b, dt_bias, A_log, Dh, norm_w):
    f32 = jnp.float32
    b, L, _ = xBC.shape
    xBC = jax.nn.silu(causal_depthwise_conv(xBC, conv_w, conv_b))
    xs = xBC[..., :D_SSD].astype(f32).reshape(b, L, SSD_HEADS, SSD_HEAD_DIM)
    Bm = xBC[..., D_SSD:D_SSD + SSD_GROUPS * SSD_STATE].astype(f32).reshape(b, L, SSD_GROUPS, SSD_STATE)
    Cm = xBC[..., D_SSD + SSD_GROUPS * SSD_STATE:].astype(f32).reshape(b, L, SSD_GROUPS, SSD_STATE)
    dt = jax.nn.softplus(dt_raw.astype(f32) + dt_bias.astype(f32))
    A = -jnp.exp(A_log.astype(f32))
    y = ssd_chunked(xs, dt, A, Bm, Cm) + Dh.astype(f32)[:, None] * xs
    y = y.reshape(b, L, D_SSD).astype(z.dtype)
    return gated_group_rmsnorm(y, z, norm_w)


def _fwd_setup_inputs(seed: int = 0) -> dict:
    key = jax.random.key(seed)
    ks = jax.random.split(key, 26)
    f32 = jnp.float32
    nrm = lambda k, s, sc: jax.random.normal(k, s, f32) * sc
    x = jax.random.normal(ks[0], (BATCH, SEQ, D_MODEL), f32)
    p = jax.random.normal(ks[1], (DEPTH, BATCH, SEQ, PLE_DIM), f32)
    norm_w = 1.0 + nrm(ks[2], (DEPTH, D_MODEL), 0.02)
    w_in = nrm(ks[3], (DEPTH, D_MODEL, D_IN_PROJ), D_MODEL ** -0.5)
    n_idx = jnp.arange(S5_STATE, dtype=f32)
    s5_A_re = -0.5 + nrm(ks[4], (DEPTH, S5_GROUPS, S5_STATE), 0.01)
    s5_A_im = math.pi * n_idx + nrm(ks[5], (DEPTH, S5_GROUPS, S5_STATE), 0.01)
    s5_log_dt = jax.random.uniform(ks[6], (DEPTH, S5_GROUPS), f32, math.log(1e-3), math.log(1e-1))
    s5_B_re = nrm(ks[7], (DEPTH, S5_GROUPS, S5_STATE, S5_GROUP_CH), (2 * S5_GROUP_CH) ** -0.5)
    s5_B_im = nrm(ks[8], (DEPTH, S5_GROUPS, S5_STATE, S5_GROUP_CH), (2 * S5_GROUP_CH) ** -0.5)
    s5_C_re = nrm(ks[9], (DEPTH, S5_GROUPS, S5_GROUP_CH, S5_STATE), (2 * S5_STATE) ** -0.5)
    s5_C_im = nrm(ks[10], (DEPTH, S5_GROUPS, S5_GROUP_CH, S5_STATE), (2 * S5_STATE) ** -0.5)
    s5_D = nrm(ks[11], (DEPTH, D_S5), 1.0)
    s5_w_glu = nrm(ks[12], (DEPTH, D_S5, D_S5), D_S5 ** -0.5)
    s5_b_glu = nrm(ks[13], (DEPTH, D_S5), 0.01)
    conv_w = nrm(ks[14], (DEPTH, CONV_WIDTH, SSD_CONV_DIM), CONV_WIDTH ** -0.5)
    conv_b = nrm(ks[15], (DEPTH, SSD_CONV_DIM), 0.01)
    dt0 = jnp.exp(jax.random.uniform(ks[16], (DEPTH, SSD_HEADS), f32, math.log(1e-3), math.log(1e-1)))
    dt_bias = dt0 + jnp.log(-jnp.expm1(-dt0))
    A_log = jnp.log(jax.random.uniform(ks[17], (DEPTH, SSD_HEADS), f32, 1.0, 16.0))
    ssd_D = 1.0 + nrm(ks[18], (DEPTH, SSD_HEADS), 0.1)
    ssd_norm_w = 1.0 + nrm(ks[19], (DEPTH, D_SSD), 0.02)
    w_out = nrm(ks[20], (DEPTH, MIX_WIDTH, D_MODEL), MIX_WIDTH ** -0.5)
    ple_norm_w = 1.0 + nrm(ks[21], (DEPTH, D_MODEL), 0.02)
    w_ple_gate = nrm(ks[22], (DEPTH, D_MODEL, D_MODEL), D_MODEL ** -0.5)
    w_ple_proj = nrm(ks[23], (DEPTH, PLE_DIM, D_MODEL), PLE_DIM ** -0.5)
    final_norm_w = 1.0 + nrm(ks[24], (D_MODEL,), 0.02)
    return {"x": x, "p": p, "norm_w": norm_w, "w_in": w_in,
            "s5_A_re": s5_A_re, "s5_A_im": s5_A_im, "s5_log_dt": s5_log_dt,
            "s5_B_re": s5_B_re, "s5_B_im": s5_B_im, "s5_C_re": s5_C_re, "s5_C_im": s5_C_im,
            "s5_D": s5_D, "s5_w_glu": s5_w_glu, "s5_b_glu": s5_b_glu,
            "conv_w": conv_w, "conv_b": conv_b, "dt_bias": dt_bias, "A_log": A_log,
            "ssd_D": ssd_D, "ssd_norm_w": ssd_norm_w, "w_out": w_out,
            "ple_norm_w": ple_norm_w, "w_ple_gate": w_ple_gate, "w_ple_proj": w_ple_proj,
            "final_norm_w": final_norm_w}


def _fwd_reference(x, p, norm_w, w_in, s5_A_re, s5_A_im, s5_log_dt, s5_B_re, s5_B_im,
              s5_C_re, s5_C_im, s5_D, s5_w_glu, s5_b_glu, conv_w, conv_b, dt_bias,
              A_log, ssd_D, ssd_norm_w, w_out, ple_norm_w, w_ple_gate, w_ple_proj,
              final_norm_w):
    h = x
    for i in range(DEPTH):
        hn = rmsnorm(h, norm_w[i])
        proj = hn @ w_in[i]
        u_s5, z_s5, z_ssd, xBC, dt_raw = jnp.split(proj, SPLITS, axis=-1)
        y_s5 = s5_branch(u_s5, s5_A_re[i], s5_A_im[i], s5_log_dt[i], s5_B_re[i], s5_B_im[i],
                         s5_C_re[i], s5_C_im[i], s5_D[i], s5_w_glu[i], s5_b_glu[i])
        y_s5 = y_s5 * jax.nn.silu(z_s5)
        y_ssd = ssd_branch(xBC, dt_raw, z_ssd, conv_w[i], conv_b[i], dt_bias[i], A_log[i],
                           ssd_D[i], ssd_norm_w[i])
        h = h + jnp.concatenate([y_s5, y_ssd], axis=-1) @ w_out[i]
        gate = jax.nn.sigmoid(rmsnorm(h, ple_norm_w[i]) @ w_ple_gate[i])
        h = h + (p[i] @ w_ple_proj[i]) * gate
    return rmsnorm(h, final_norm_w)


import jax as _jax
import jax.numpy as _jnp

TWIN_FORMAT = 'train_step'
FWD_PARAMS = ['x', 'p', 'norm_w', 'w_in', 's5_A_re', 's5_A_im', 's5_log_dt', 's5_B_re', 's5_B_im', 's5_C_re', 's5_C_im', 's5_D', 's5_w_glu', 's5_b_glu', 'conv_w', 'conv_b', 'dt_bias', 'A_log', 'ssd_D', 'ssd_norm_w', 'w_out', 'ple_norm_w', 'w_ple_gate', 'w_ple_proj', 'final_norm_w']
TWIN_WEIGHTS = ['norm_w', 'w_in', 's5_A_re', 's5_A_im', 's5_log_dt', 's5_B_re', 's5_B_im', 's5_C_re', 's5_C_im', 's5_D', 's5_w_glu', 's5_b_glu', 'conv_w', 'conv_b', 'dt_bias', 'A_log', 'ssd_D', 'ssd_norm_w', 'w_out', 'ple_norm_w', 'w_ple_gate', 'w_ple_proj', 'final_norm_w']
TWIN_DIFF_INPUT = 'x'
TWIN_INPUTS = ['x', 'p', 'norm_w', 'w_in', 's5_A_re', 's5_A_im', 's5_log_dt', 's5_B_re', 's5_B_im', 's5_C_re', 's5_C_im', 's5_D', 's5_w_glu', 's5_b_glu', 'conv_w', 'conv_b', 'dt_bias', 'A_log', 'ssd_D', 'ssd_norm_w', 'w_out', 'ple_norm_w', 'w_ple_gate', 'w_ple_proj', 'final_norm_w', 'loss_target', 'm_norm_w', 'm_w_in', 'm_s5_A_re', 'm_s5_A_im', 'm_s5_log_dt', 'm_s5_B_re', 'm_s5_B_im', 'm_s5_C_re', 'm_s5_C_im', 'm_s5_D', 'm_s5_w_glu', 'm_s5_b_glu', 'm_conv_w', 'm_conv_b', 'm_dt_bias', 'm_A_log', 'm_ssd_D', 'm_ssd_norm_w', 'm_w_out', 'm_ple_norm_w', 'm_w_ple_gate', 'm_w_ple_proj', 'm_final_norm_w', 'v_norm_w', 'v_w_in', 'v_s5_A_re', 'v_s5_A_im', 'v_s5_log_dt', 'v_s5_B_re', 'v_s5_B_im', 'v_s5_C_re', 'v_s5_C_im', 'v_s5_D', 'v_s5_w_glu', 'v_s5_b_glu', 'v_conv_w', 'v_conv_b', 'v_dt_bias', 'v_A_log', 'v_ssd_D', 'v_ssd_norm_w', 'v_w_out', 'v_ple_norm_w', 'v_w_ple_gate', 'v_w_ple_proj', 'v_final_norm_w']
TWIN_OUTPUTS = ['loss', 'grad_x', 'grad_norm_w', 'grad_w_in', 'grad_s5_A_re', 'grad_s5_A_im', 'grad_s5_log_dt', 'grad_s5_B_re', 'grad_s5_B_im', 'grad_s5_C_re', 'grad_s5_C_im', 'grad_s5_D', 'grad_s5_w_glu', 'grad_s5_b_glu', 'grad_conv_w', 'grad_conv_b', 'grad_dt_bias', 'grad_A_log', 'grad_ssd_D', 'grad_ssd_norm_w', 'grad_w_out', 'grad_ple_norm_w', 'grad_w_ple_gate', 'grad_w_ple_proj', 'grad_final_norm_w', 'delta_norm_w', 'delta_w_in', 'delta_s5_A_re', 'delta_s5_A_im', 'delta_s5_log_dt', 'delta_s5_B_re', 'delta_s5_B_im', 'delta_s5_C_re', 'delta_s5_C_im', 'delta_s5_D', 'delta_s5_w_glu', 'delta_s5_b_glu', 'delta_conv_w', 'delta_conv_b', 'delta_dt_bias', 'delta_A_log', 'delta_ssd_D', 'delta_ssd_norm_w', 'delta_w_out', 'delta_ple_norm_w', 'delta_w_ple_gate', 'delta_w_ple_proj', 'delta_final_norm_w', 'new_m_norm_w', 'new_m_w_in', 'new_m_s5_A_re', 'new_m_s5_A_im', 'new_m_s5_log_dt', 'new_m_s5_B_re', 'new_m_s5_B_im', 'new_m_s5_C_re', 'new_m_s5_C_im', 'new_m_s5_D', 'new_m_s5_w_glu', 'new_m_s5_b_glu', 'new_m_conv_w', 'new_m_conv_b', 'new_m_dt_bias', 'new_m_A_log', 'new_m_ssd_D', 'new_m_ssd_norm_w', 'new_m_w_out', 'new_m_ple_norm_w', 'new_m_w_ple_gate', 'new_m_w_ple_proj', 'new_m_final_norm_w', 'new_v_norm_w', 'new_v_w_in', 'new_v_s5_A_re', 'new_v_s5_A_im', 'new_v_s5_log_dt', 'new_v_s5_B_re', 'new_v_s5_B_im', 'new_v_s5_C_re', 'new_v_s5_C_im', 'new_v_s5_D', 'new_v_s5_w_glu', 'new_v_s5_b_glu', 'new_v_conv_w', 'new_v_conv_b', 'new_v_dt_bias', 'new_v_A_log', 'new_v_ssd_D', 'new_v_ssd_norm_w', 'new_v_w_out', 'new_v_ple_norm_w', 'new_v_w_ple_gate', 'new_v_w_ple_proj', 'new_v_final_norm_w']
TWIN_LEAF_KINDS = {'loss': 'loss', 'grad_x': 'grad_x', 'grad_norm_w': 'grad_w', 'grad_w_in': 'grad_w', 'grad_s5_A_re': 'grad_w', 'grad_s5_A_im': 'grad_w', 'grad_s5_log_dt': 'grad_w', 'grad_s5_B_re': 'grad_w', 'grad_s5_B_im': 'grad_w', 'grad_s5_C_re': 'grad_w', 'grad_s5_C_im': 'grad_w', 'grad_s5_D': 'grad_w', 'grad_s5_w_glu': 'grad_w', 'grad_s5_b_glu': 'grad_w', 'grad_conv_w': 'grad_w', 'grad_conv_b': 'grad_w', 'grad_dt_bias': 'grad_w', 'grad_A_log': 'grad_w', 'grad_ssd_D': 'grad_w', 'grad_ssd_norm_w': 'grad_w', 'grad_w_out': 'grad_w', 'grad_ple_norm_w': 'grad_w', 'grad_w_ple_gate': 'grad_w', 'grad_w_ple_proj': 'grad_w', 'grad_final_norm_w': 'grad_w', 'delta_norm_w': 'delta_w', 'delta_w_in': 'delta_w', 'delta_s5_A_re': 'delta_w', 'delta_s5_A_im': 'delta_w', 'delta_s5_log_dt': 'delta_w', 'delta_s5_B_re': 'delta_w', 'delta_s5_B_im': 'delta_w', 'delta_s5_C_re': 'delta_w', 'delta_s5_C_im': 'delta_w', 'delta_s5_D': 'delta_w', 'delta_s5_w_glu': 'delta_w', 'delta_s5_b_glu': 'delta_w', 'delta_conv_w': 'delta_w', 'delta_conv_b': 'delta_w', 'delta_dt_bias': 'delta_w', 'delta_A_log': 'delta_w', 'delta_ssd_D': 'delta_w', 'delta_ssd_norm_w': 'delta_w', 'delta_w_out': 'delta_w', 'delta_ple_norm_w': 'delta_w', 'delta_w_ple_gate': 'delta_w', 'delta_w_ple_proj': 'delta_w', 'delta_final_norm_w': 'delta_w', 'new_m_norm_w': 'new_m', 'new_m_w_in': 'new_m', 'new_m_s5_A_re': 'new_m', 'new_m_s5_A_im': 'new_m', 'new_m_s5_log_dt': 'new_m', 'new_m_s5_B_re': 'new_m', 'new_m_s5_B_im': 'new_m', 'new_m_s5_C_re': 'new_m', 'new_m_s5_C_im': 'new_m', 'new_m_s5_D': 'new_m', 'new_m_s5_w_glu': 'new_m', 'new_m_s5_b_glu': 'new_m', 'new_m_conv_w': 'new_m', 'new_m_conv_b': 'new_m', 'new_m_dt_bias': 'new_m', 'new_m_A_log': 'new_m', 'new_m_ssd_D': 'new_m', 'new_m_ssd_norm_w': 'new_m', 'new_m_w_out': 'new_m', 'new_m_ple_norm_w': 'new_m', 'new_m_w_ple_gate': 'new_m', 'new_m_w_ple_proj': 'new_m', 'new_m_final_norm_w': 'new_m', 'new_v_norm_w': 'new_v', 'new_v_w_in': 'new_v', 'new_v_s5_A_re': 'new_v', 'new_v_s5_A_im': 'new_v', 'new_v_s5_log_dt': 'new_v', 'new_v_s5_B_re': 'new_v', 'new_v_s5_B_im': 'new_v', 'new_v_s5_C_re': 'new_v', 'new_v_s5_C_im': 'new_v', 'new_v_s5_D': 'new_v', 'new_v_s5_w_glu': 'new_v', 'new_v_s5_b_glu': 'new_v', 'new_v_conv_w': 'new_v', 'new_v_conv_b': 'new_v', 'new_v_dt_bias': 'new_v', 'new_v_A_log': 'new_v', 'new_v_ssd_D': 'new_v', 'new_v_ssd_norm_w': 'new_v', 'new_v_w_out': 'new_v', 'new_v_ple_norm_w': 'new_v', 'new_v_w_ple_gate': 'new_v', 'new_v_w_ple_proj': 'new_v', 'new_v_final_norm_w': 'new_v'}


def _forward(args):
    return _fwd_reference(*[args[k] for k in FWD_PARAMS])


def _output_shape():
    def fwd():
        inp = _fwd_setup_inputs(0)
        return _fwd_reference(*[inp[k] for k in FWD_PARAMS])
    out = _jax.eval_shape(fwd)
    return out.shape, out.dtype

N_MICROBATCH = 1
ADAM_LR = 0.001
ADAM_B1 = 0.9
ADAM_B2 = 0.999
ADAM_EPS = 1e-08
ADAM_WD = 0.01
ADAM_STEP = 10
PER_EXAMPLE_BATCH_AXIS = {'x': 0, 'p': 1, 'loss_target': 0}
SHARED_INPUTS = []
_WEIGHT_DTYPES = {'norm_w': _jnp.float32, 'w_in': _jnp.float32, 's5_A_re': _jnp.float32, 's5_A_im': _jnp.float32, 's5_log_dt': _jnp.float32, 's5_B_re': _jnp.float32, 's5_B_im': _jnp.float32, 's5_C_re': _jnp.float32, 's5_C_im': _jnp.float32, 's5_D': _jnp.float32, 's5_w_glu': _jnp.float32, 's5_b_glu': _jnp.float32, 'conv_w': _jnp.float32, 'conv_b': _jnp.float32, 'dt_bias': _jnp.float32, 'A_log': _jnp.float32, 'ssd_D': _jnp.float32, 'ssd_norm_w': _jnp.float32, 'w_out': _jnp.float32, 'ple_norm_w': _jnp.float32, 'w_ple_gate': _jnp.float32, 'w_ple_proj': _jnp.float32, 'final_norm_w': _jnp.float32}
MOMENT_SCALE = {'norm_w': 2.019570e-01, 'w_in': 9.195206e-02, 's5_A_re': 1.718580e-03, 's5_A_im': 1.581488e-03, 's5_log_dt': 1.119625e+00, 's5_B_re': 9.792866e-04, 's5_B_im': 9.763013e-04, 's5_C_re': 1.977005e-03, 's5_C_im': 1.932996e-03, 's5_D': 3.107526e-02, 's5_w_glu': 8.031723e-03, 's5_b_glu': 1.167016e-02, 'conv_w': 1.066824e-01, 'conv_b': 1.468077e-01, 'dt_bias': 2.771583e-01, 'A_log': 5.622242e-01, 'ssd_D': 8.010597e-01, 'ssd_norm_w': 1.420162e-01, 'w_out': 1.386873e-01, 'ple_norm_w': 4.223028e-02, 'w_ple_gate': 4.030854e-02, 'w_ple_proj': 1.016953e-01, 'final_norm_w': 6.402471e+01}


def _to_microbatches(a, axis):
    t = _jnp.moveaxis(a, axis, 0)
    t = t.reshape((N_MICROBATCH, t.shape[0] // N_MICROBATCH) + t.shape[1:])
    return _jnp.moveaxis(t, 1, axis + 1)


def setup_inputs(seed: int = 0) -> dict:
    inp = _fwd_setup_inputs(seed)
    key = _jax.random.fold_in(_jax.random.key(seed), 7919)
    shape, _ = _output_shape()
    out = dict(inp)
    out["loss_target"] = _jax.random.normal(_jax.random.fold_in(key, 0), shape, _jnp.float32)
    for i, name in enumerate(TWIN_WEIGHTS):
        w = inp[name].astype(_jnp.float32)
        if MOMENT_SCALE is None:
            s = _jnp.sqrt(_jnp.mean(_jnp.square(w)) + 1e-30)
        else:
            s = MOMENT_SCALE[name]
        km, kv = _jax.random.split(_jax.random.fold_in(key, i + 1))
        out[name] = w
        out["m_" + name] = s * _jax.random.normal(km, w.shape, _jnp.float32)
        out["v_" + name] = (s * s) * _jax.random.uniform(kv, w.shape, _jnp.float32, 0.5, 1.5)
    if N_MICROBATCH > 1:
        for name, axis in PER_EXAMPLE_BATCH_AXIS.items():
            out[name] = _to_microbatches(out[name], axis)
    return {'x': out['x'], 'p': out['p'], 'norm_w': out['norm_w'], 'w_in': out['w_in'], 's5_A_re': out['s5_A_re'], 's5_A_im': out['s5_A_im'], 's5_log_dt': out['s5_log_dt'], 's5_B_re': out['s5_B_re'], 's5_B_im': out['s5_B_im'], 's5_C_re': out['s5_C_re'], 's5_C_im': out['s5_C_im'], 's5_D': out['s5_D'], 's5_w_glu': out['s5_w_glu'], 's5_b_glu': out['s5_b_glu'], 'conv_w': out['conv_w'], 'conv_b': out['conv_b'], 'dt_bias': out['dt_bias'], 'A_log': out['A_log'], 'ssd_D': out['ssd_D'], 'ssd_norm_w': out['ssd_norm_w'], 'w_out': out['w_out'], 'ple_norm_w': out['ple_norm_w'], 'w_ple_gate': out['w_ple_gate'], 'w_ple_proj': out['w_ple_proj'], 'final_norm_w': out['final_norm_w'], 'loss_target': out['loss_target'], 'm_norm_w': out['m_norm_w'], 'm_w_in': out['m_w_in'], 'm_s5_A_re': out['m_s5_A_re'], 'm_s5_A_im': out['m_s5_A_im'], 'm_s5_log_dt': out['m_s5_log_dt'], 'm_s5_B_re': out['m_s5_B_re'], 'm_s5_B_im': out['m_s5_B_im'], 'm_s5_C_re': out['m_s5_C_re'], 'm_s5_C_im': out['m_s5_C_im'], 'm_s5_D': out['m_s5_D'], 'm_s5_w_glu': out['m_s5_w_glu'], 'm_s5_b_glu': out['m_s5_b_glu'], 'm_conv_w': out['m_conv_w'], 'm_conv_b': out['m_conv_b'], 'm_dt_bias': out['m_dt_bias'], 'm_A_log': out['m_A_log'], 'm_ssd_D': out['m_ssd_D'], 'm_ssd_norm_w': out['m_ssd_norm_w'], 'm_w_out': out['m_w_out'], 'm_ple_norm_w': out['m_ple_norm_w'], 'm_w_ple_gate': out['m_w_ple_gate'], 'm_w_ple_proj': out['m_w_ple_proj'], 'm_final_norm_w': out['m_final_norm_w'], 'v_norm_w': out['v_norm_w'], 'v_w_in': out['v_w_in'], 'v_s5_A_re': out['v_s5_A_re'], 'v_s5_A_im': out['v_s5_A_im'], 'v_s5_log_dt': out['v_s5_log_dt'], 'v_s5_B_re': out['v_s5_B_re'], 'v_s5_B_im': out['v_s5_B_im'], 'v_s5_C_re': out['v_s5_C_re'], 'v_s5_C_im': out['v_s5_C_im'], 'v_s5_D': out['v_s5_D'], 'v_s5_w_glu': out['v_s5_w_glu'], 'v_s5_b_glu': out['v_s5_b_glu'], 'v_conv_w': out['v_conv_w'], 'v_conv_b': out['v_conv_b'], 'v_dt_bias': out['v_dt_bias'], 'v_A_log': out['v_A_log'], 'v_ssd_D': out['v_ssd_D'], 'v_ssd_norm_w': out['v_ssd_norm_w'], 'v_w_out': out['v_w_out'], 'v_ple_norm_w': out['v_ple_norm_w'], 'v_w_ple_gate': out['v_w_ple_gate'], 'v_w_ple_proj': out['v_w_ple_proj'], 'v_final_norm_w': out['v_final_norm_w']}


def _loss(weights, diff, rest, loss_target):
    with _jax.named_scope("forward"):
        args = {**rest, TWIN_DIFF_INPUT: diff, **{k: w.astype(_WEIGHT_DTYPES[k]) for k, w in weights.items()}}
        y = _forward(args)
    with _jax.named_scope("loss_head"):
        err = _jnp.square(y.astype(_jnp.float32) - loss_target)
        return 0.5 * _jnp.sum(_jnp.mean(err, axis=-1)) if err.ndim else 0.5 * err


def _adamw(w, g, m, v):
    m = ADAM_B1 * m + (1.0 - ADAM_B1) * g
    v = ADAM_B2 * v + (1.0 - ADAM_B2) * _jnp.square(g)
    m_hat = m / (1.0 - ADAM_B1 ** ADAM_STEP)
    v_hat = v / (1.0 - ADAM_B2 ** ADAM_STEP)
    delta = -ADAM_LR * (m_hat / (_jnp.sqrt(v_hat) + ADAM_EPS) + ADAM_WD * w)
    return delta, m, v


def reference(x, p, norm_w, w_in, s5_A_re, s5_A_im, s5_log_dt, s5_B_re, s5_B_im, s5_C_re, s5_C_im, s5_D, s5_w_glu, s5_b_glu, conv_w, conv_b, dt_bias, A_log, ssd_D, ssd_norm_w, w_out, ple_norm_w, w_ple_gate, w_ple_proj, final_norm_w, loss_target, m_norm_w, m_w_in, m_s5_A_re, m_s5_A_im, m_s5_log_dt, m_s5_B_re, m_s5_B_im, m_s5_C_re, m_s5_C_im, m_s5_D, m_s5_w_glu, m_s5_b_glu, m_conv_w, m_conv_b, m_dt_bias, m_A_log, m_ssd_D, m_ssd_norm_w, m_w_out, m_ple_norm_w, m_w_ple_gate, m_w_ple_proj, m_final_norm_w, v_norm_w, v_w_in, v_s5_A_re, v_s5_A_im, v_s5_log_dt, v_s5_B_re, v_s5_B_im, v_s5_C_re, v_s5_C_im, v_s5_D, v_s5_w_glu, v_s5_b_glu, v_conv_w, v_conv_b, v_dt_bias, v_A_log, v_ssd_D, v_ssd_norm_w, v_w_out, v_ple_norm_w, v_w_ple_gate, v_w_ple_proj, v_final_norm_w):
    given = dict(x=x, p=p, norm_w=norm_w, w_in=w_in, s5_A_re=s5_A_re, s5_A_im=s5_A_im, s5_log_dt=s5_log_dt, s5_B_re=s5_B_re, s5_B_im=s5_B_im, s5_C_re=s5_C_re, s5_C_im=s5_C_im, s5_D=s5_D, s5_w_glu=s5_w_glu, s5_b_glu=s5_b_glu, conv_w=conv_w, conv_b=conv_b, dt_bias=dt_bias, A_log=A_log, ssd_D=ssd_D, ssd_norm_w=ssd_norm_w, w_out=w_out, ple_norm_w=ple_norm_w, w_ple_gate=w_ple_gate, w_ple_proj=w_ple_proj, final_norm_w=final_norm_w, loss_target=loss_target, m_norm_w=m_norm_w, m_w_in=m_w_in, m_s5_A_re=m_s5_A_re, m_s5_A_im=m_s5_A_im, m_s5_log_dt=m_s5_log_dt, m_s5_B_re=m_s5_B_re, m_s5_B_im=m_s5_B_im, m_s5_C_re=m_s5_C_re, m_s5_C_im=m_s5_C_im, m_s5_D=m_s5_D, m_s5_w_glu=m_s5_w_glu, m_s5_b_glu=m_s5_b_glu, m_conv_w=m_conv_w, m_conv_b=m_conv_b, m_dt_bias=m_dt_bias, m_A_log=m_A_log, m_ssd_D=m_ssd_D, m_ssd_norm_w=m_ssd_norm_w, m_w_out=m_w_out, m_ple_norm_w=m_ple_norm_w, m_w_ple_gate=m_w_ple_gate, m_w_ple_proj=m_w_ple_proj, m_final_norm_w=m_final_norm_w, v_norm_w=v_norm_w, v_w_in=v_w_in, v_s5_A_re=v_s5_A_re, v_s5_A_im=v_s5_A_im, v_s5_log_dt=v_s5_log_dt, v_s5_B_re=v_s5_B_re, v_s5_B_im=v_s5_B_im, v_s5_C_re=v_s5_C_re, v_s5_C_im=v_s5_C_im, v_s5_D=v_s5_D, v_s5_w_glu=v_s5_w_glu, v_s5_b_glu=v_s5_b_glu, v_conv_w=v_conv_w, v_conv_b=v_conv_b, v_dt_bias=v_dt_bias, v_A_log=v_A_log, v_ssd_D=v_ssd_D, v_ssd_norm_w=v_ssd_norm_w, v_w_out=v_w_out, v_ple_norm_w=v_ple_norm_w, v_w_ple_gate=v_w_ple_gate, v_w_ple_proj=v_w_ple_proj, v_final_norm_w=v_final_norm_w)
    weights = {n: given[n] for n in TWIN_WEIGHTS}
    shared = {n: given[n] for n in SHARED_INPUTS}
    per_example = {n: given[n] for n in ['x', 'p']}
    grad_fn = _jax.value_and_grad(_loss, argnums=(0, 1))

    def one_microbatch(ex, loss_target):
        ex = dict(ex)
        diff = ex.pop(TWIN_DIFF_INPUT)
        return grad_fn(weights, diff, {**shared, **ex}, loss_target)

    if N_MICROBATCH == 1:
        loss, (grad_w, grad_x) = one_microbatch(per_example, given["loss_target"])
    else:
        def body(carry, xs):
            loss_sum, grad_sum = carry
            l_k, (gw_k, gx_k) = one_microbatch(xs[0], xs[1])
            with _jax.named_scope("update"):
                return (loss_sum + l_k, _jax.tree.map(_jnp.add, grad_sum, gw_k)), gx_k

        init = (_jnp.zeros((), _jnp.float32), _jax.tree.map(_jnp.zeros_like, weights))
        (loss, grad_w), grad_x = _jax.lax.scan(body, init, (per_example, given["loss_target"]))
    with _jax.named_scope("update"):
        delta_w, new_m, new_v = {}, {}, {}
        for n in TWIN_WEIGHTS:
            delta_w[n], new_m[n], new_v[n] = _adamw(weights[n], grad_w[n], given["m_" + n], given["v_" + n])
    return (loss, grad_x, *[grad_w[n] for n in TWIN_WEIGHTS], *[delta_w[n] for n in TWIN_WEIGHTS],
            *[new_m[n] for n in TWIN_WEIGHTS], *[new_v[n] for n in TWIN_WEIGHTS])
```

```python
import functools
import math

import jax
import jax.numpy as jnp
from jax import lax
from jax.experimental import pallas as pl
from jax.experimental.pallas import tpu as pltpu

F32 = jnp.float32
BF = jnp.bfloat16
EPS = 1e-6
CHUNK = 64
D_MODEL = 1024
S5_GROUPS = 64
S5_CH = 16
S5_STATE = 64
SSD_HEADS = 16
SSD_HEAD_DIM = 64
SSD_GROUPS = 4
SSD_STATE = 128
D_MAIN = 5120
LANES = 128
TOKEN_TILE = 256
VMEM_LIMIT = 56 * 1024 * 1024
MESH_AXES = ("x", "y", "c")
N_CHIPS = 4
ADAM_LR, ADAM_B1, ADAM_B2, ADAM_EPS, ADAM_WD, ADAM_STEP = 0.001, 0.9, 0.999, 1e-08, 0.01, 10
MESH = pl.DeviceIdType.MESH
ANY = pl.BlockSpec(memory_space=pl.ANY)


def _dot(a, b):
    return jnp.dot(a, b, preferred_element_type=F32)


def _dot_nt(a, b):
    return lax.dot_general(a, b, (((1,), (1,)), ((), ())), preferred_element_type=F32)


def _dot_tn(a, b):
    return lax.dot_general(a, b, (((0,), (0,)), ((), ())), preferred_element_type=F32)


def _sigmoid(x):
    return 1.0 / (1.0 + jnp.exp(-x))


def _softplus(x):
    return jnp.maximum(x, 0.0) + jnp.log(1.0 + jnp.exp(-jnp.abs(x)))


_GELU_C = math.sqrt(2.0 / math.pi)


def _gelu(x):
    return 0.5 * x * (1.0 + jnp.tanh(_GELU_C * (x + 0.044715 * x * x * x)))


def _gelu_grad(x):
    th = jnp.tanh(_GELU_C * (x + 0.044715 * x * x * x))
    return 0.5 * (1.0 + th) + 0.5 * x * (1.0 - th * th) * _GELU_C * (1.0 + 3.0 * 0.044715 * x * x)


def _params(sem=None):
    return pltpu.CompilerParams(dimension_semantics=sem, vmem_limit_bytes=VMEM_LIMIT)


def _row_spec(tl, width, col=0):
    return pl.BlockSpec((tl, width), lambda i, col=col: (i, col))


def _const_spec(shape):
    nd = len(shape)
    return pl.BlockSpec(shape, lambda *_: (0,) * nd)


def _in_proj_fwd(x, norm_w, w_main, w_dt):
    L = x.shape[0]
    tl = min(TOKEN_TILE, L)

    def body(x_ref, nw_ref, wm_ref, wd_ref, hn_ref, pm_ref, pd_ref):
        xv = x_ref[...]
        r = lax.rsqrt(jnp.mean(xv * xv, axis=-1, keepdims=True) + EPS)
        hn = (xv * r * nw_ref[...]).astype(BF)
        hn_ref[...] = hn
        for j in range(D_MAIN // 1024):
            pm_ref[:, j * 1024:(j + 1) * 1024] = _dot(hn, wm_ref[:, j * 1024:(j + 1) * 1024]).astype(BF)
        pd_ref[...] = _dot(hn, wd_ref[...])

    return pl.pallas_call(
        body, name="in_proj_fwd", grid=(L // tl,),
        in_specs=[_row_spec(tl, D_MODEL), _const_spec((1, D_MODEL)), _const_spec((D_MODEL, D_MAIN)), _const_spec((D_MODEL, LANES))],
        out_specs=[_row_spec(tl, D_MODEL), _row_spec(tl, D_MAIN), _row_spec(tl, LANES)],
        out_shape=[jax.ShapeDtypeStruct((L, D_MODEL), BF), jax.ShapeDtypeStruct((L, D_MAIN), BF), jax.ShapeDtypeStruct((L, LANES), F32)],
        compiler_params=_params(("arbitrary",)),
    )(x, norm_w, w_main, w_dt)


def _in_proj_bwd(x, norm_w, dh1, du_flat, dyssm, s5_d, dzs, dzd, dxbc, ddt, w_main, w_dt):
    L = x.shape[0]
    tl = min(TOKEN_TILE, L)

    def body(x_ref, nw_ref, dh1_ref, duf_ref, dys_ref, d_ref, dzs_ref, dzd_ref, dxbc_ref, ddt_ref, wm_ref, wd_ref,
             gx_ref, du_ref, gnw_ref):
        @pl.when(pl.program_id(0) == 0)
        def _():
            gnw_ref[...] = jnp.zeros_like(gnw_ref)

        du = (duf_ref[...].astype(F32) + dys_ref[...].astype(F32) * d_ref[...]).astype(BF)
        du_ref[...] = du
        dhn = _dot_nt(du, wm_ref[:, 0:1024])
        dhn += _dot_nt(dzs_ref[...], wm_ref[:, 1024:2048])
        dhn += _dot_nt(dzd_ref[...], wm_ref[:, 2048:3072])
        dhn += _dot_nt(dxbc_ref[...], wm_ref[:, 3072:5120])
        dhn += _dot_nt(ddt_ref[...].astype(BF), wd_ref[...])
        xv = x_ref[...]
        r = lax.rsqrt(jnp.mean(xv * xv, axis=-1, keepdims=True) + EPS)
        xh = xv * r
        gnw_ref[...] += jnp.sum(dhn * xh, axis=0, keepdims=True)
        g = dhn * nw_ref[...]
        gx_ref[...] = dh1_ref[...] + r * (g - xh * jnp.mean(g * xh, axis=-1, keepdims=True))

    return pl.pallas_call(
        body, name="in_proj_bwd", grid=(L // tl,),
        in_specs=[_row_spec(tl, D_MODEL), _const_spec((1, D_MODEL)), _row_spec(tl, D_MODEL), _row_spec(tl, D_MODEL),
                  _row_spec(tl, D_MODEL), _const_spec((1, D_MODEL)), _row_spec(tl, D_MODEL), _row_spec(tl, D_MODEL),
                  _row_spec(tl, 2048), _row_spec(tl, LANES), _const_spec((D_MODEL, D_MAIN)), _const_spec((D_MODEL, LANES))],
        out_specs=[_row_spec(tl, D_MODEL), _row_spec(tl, D_MODEL), _const_spec((1, D_MODEL))],
        out_shape=[jax.ShapeDtypeStruct((L, D_MODEL), F32), jax.ShapeDtypeStruct((L, D_MODEL), BF), jax.ShapeDtypeStruct((1, D_MODEL), F32)],
        compiler_params=_params(("arbitrary",)),
    )(x, norm_w, dh1, du_flat, dyssm, s5_d, dzs, dzd, dxbc, ddt, w_main, w_dt)


def _matmul_tn(a, b, name):
    L, M = a.shape
    N = b.shape[1]
    tm, tn, tk = min(M, 1024), min(N, 1024), min(L, 512)

    def body(a_ref, b_ref, o_ref):
        @pl.when(pl.program_id(2) == 0)
        def _():
            o_ref[...] = jnp.zeros_like(o_ref)

        o_ref[...] += _dot_tn(a_ref[...].astype(BF), b_ref[...].astype(BF))

    return pl.pallas_call(
        body, name=name, grid=(M // tm, N // tn, L // tk),
        in_specs=[pl.BlockSpec((tk, tm), lambda i, j, k: (k, i)), pl.BlockSpec((tk, tn), lambda i, j, k: (k, j))],
        out_specs=pl.BlockSpec((tm, tn), lambda i, j, k: (i, j)),
        out_shape=jax.ShapeDtypeStruct((M, N), F32),
        compiler_params=_params(("parallel", "parallel", "arbitrary")),
    )(a, b)


def _s5_tables(a_re, a_im, log_dt, b_re, b_im, c_re, c_im):
    hi = lax.Precision.HIGHEST
    dt = jnp.exp(log_dt)[:, None]
    lam_re, lam_im = a_re * dt, a_im * dt
    tau = jnp.arange(CHUNK + 1, dtype=F32)
    mag = jnp.exp(lam_re[:, :, None] * tau)
    ang = lam_im[:, :, None] * tau
    pw = lax.complex(mag * jnp.cos(ang), mag * jnp.sin(ang))
    beta = (pw[:, :, 1] - 1.0) / lax.complex(a_re, a_im)
    bb = beta[:, :, None] * lax.complex(b_re, b_im)
    cc = lax.complex(c_re, c_im)
    cp = cc.transpose(0, 2, 1)[:, :, None, :] * pw[:, :, :CHUNK, None]
    cpf = cp.reshape(S5_GROUPS, S5_STATE, CHUNK * S5_CH)
    bbt = bb.transpose(0, 2, 1)
    kmat = (jnp.einsum("ghn,gnj->ghj", bbt.real, cpf.real, precision=hi)
            - jnp.einsum("ghn,gnj->ghj", bbt.imag, cpf.imag, precision=hi))
    w = bbt[:, :, None, :] * pw[:, :, CHUNK - 1::-1][:, :, :CHUNK].transpose(0, 2, 1)[:, None, :, :]
    wst = jnp.concatenate([w.real, w.imag], axis=-1).reshape(S5_GROUPS, S5_CH * CHUNK, 2 * S5_STATE)
    v = cc.transpose(0, 2, 1)[:, :, None, :] * pw[:, :, 1:CHUNK + 1, None]
    woff = jnp.concatenate([v.real, -v.imag], axis=1).reshape(S5_GROUPS, 2 * S5_STATE, CHUNK * S5_CH)
    a64 = jnp.concatenate([pw[:, :, CHUNK].real, pw[:, :, CHUNK].imag], axis=-1)
    return kmat, wst, woff, a64


def _s5_scan_powers(a_re, a_im, log_dt, nsteps):
    dt = jnp.exp(log_dt)[:, None]
    steps = (CHUNK * (2.0 ** jnp.arange(8, dtype=F32)))[None, :, None]
    mag = jnp.exp((a_re * dt)[:, None, :] * steps)
    ang = (a_im * dt)[:, None, :] * steps
    re, im = mag * jnp.cos(ang), mag * jnp.sin(ang)
    del nsteps
    return jnp.concatenate([re, re], -1), jnp.concatenate([-im, im], -1)


def _build_toeplitz(kmat_ref, tg_ref):
    lane = lax.broadcasted_iota(jnp.int32, (CHUNK, CHUNK * S5_CH), 1)
    srow = lax.broadcasted_iota(jnp.int32, (CHUNK, CHUNK * S5_CH), 0)
    keep = lane >= S5_CH * srow
    for h in range(S5_CH):
        row = jnp.broadcast_to(kmat_ref[0, h:h + 1, :], (CHUNK, CHUNK * S5_CH))
        rolled = pltpu.roll(row, 0, 1, stride=S5_CH, stride_axis=0)
        tg_ref[h * CHUNK:(h + 1) * CHUNK, :] = jnp.where(keep, rolled, 0.0).astype(BF)


def _swap_halves(x):
    return pltpu.roll(x, S5_STATE, 1)


def _s5_core_fwd(uflat, kmat, wst, woff, p1, p2):
    G, nc, W = uflat.shape
    nsteps = max(1, (nc - 1).bit_length())

    def body(u_ref, k_ref, wst_ref, woff_ref, p1_ref, p2_ref, y_ref, h_ref, tg_ref):
        _build_toeplitz(k_ref, tg_ref)
        u = u_ref[0]
        x = _dot(u, wst_ref[0])
        row = lax.broadcasted_iota(jnp.int32, x.shape, 0)
        d = 1
        for k in range(nsteps):
            sh = jnp.where(row >= d, pltpu.roll(x, d, 0), 0.0)
            x = x + p1_ref[0, k:k + 1, :] * sh + p2_ref[0, k:k + 1, :] * _swap_halves(sh)
            d *= 2
        h = jnp.where(row >= 1, pltpu.roll(x, 1, 0), 0.0)
        h_ref[0] = h
        y = _dot(u, tg_ref[...]) + _dot(h.astype(BF), woff_ref[0])
        y_ref[0] = y.astype(BF)

    return pl.pallas_call(
        body, name="s5_core_fwd", grid=(G,),
        in_specs=[pl.BlockSpec((1, nc, W), lambda g: (g, 0, 0)), pl.BlockSpec((1, S5_CH, W), lambda g: (g, 0, 0)),
                  pl.BlockSpec((1, W, 2 * S5_STATE), lambda g: (g, 0, 0)), pl.BlockSpec((1, 2 * S5_STATE, W), lambda g: (g, 0, 0)),
                  pl.BlockSpec((1, 8, 2 * S5_STATE), lambda g: (g, 0, 0)), pl.BlockSpec((1, 8, 2 * S5_STATE), lambda g: (g, 0, 0))],
        out_specs=[pl.BlockSpec((1, nc, W), lambda g: (g, 0, 0)), pl.BlockSpec((1, nc, 2 * S5_STATE), lambda g: (g, 0, 0))],
        out_shape=[jax.ShapeDtypeStruct((G, nc, W), BF), jax.ShapeDtypeStruct((G, nc, 2 * S5_STATE), F32)],
        scratch_shapes=[pltpu.VMEM((W, W), BF)],
        compiler_params=_params(("arbitrary",)),
    )(uflat, kmat, wst, woff, p1, p2)


def _s5_core_bwd(uflat, urev, dyflat, hsave, kmat, wst, woff, p1, p2):
    G, nc, W = uflat.shape
    nsteps = max(1, (nc - 1).bit_length())

    def body(u_ref, ur_ref, dy_ref, h_ref, k_ref, wst_ref, woff_ref, p1_ref, p2_ref,
             du_ref, dk_ref, dwst_ref, dwoff_ref, da_ref, tg_ref):
        _build_toeplitz(k_ref, tg_ref)
        u = u_ref[0]
        dy = dy_ref[0]
        h = h_ref[0]
        gh = _dot_nt(dy, woff_ref[0])
        row = lax.broadcasted_iota(jnp.int32, gh.shape, 0)
        x = jnp.where(row < nc - 1, pltpu.roll(gh, nc - 1, 0), 0.0)
        d = 1
        for k in range(nsteps):
            sh = jnp.where(row < nc - d, pltpu.roll(x, nc - d, 0), 0.0)
            x = x + p1_ref[0, k:k + 1, :] * sh - p2_ref[0, k:k + 1, :] * _swap_halves(sh)
            d *= 2
        gs = x.astype(BF)
        du_ref[0] = (_dot_nt(dy, tg_ref[...]) + _dot_nt(gs, wst_ref[0])).astype(BF)
        dwst_ref[0] = _dot_tn(u, gs)
        dwoff_ref[0] = _dot_tn(h.astype(BF), dy)
        r1 = jnp.sum(x * h, axis=0, keepdims=True)
        r2 = jnp.sum(x * _swap_halves(h), axis=0, keepdims=True)
        da_ref[0] = jnp.concatenate([r1, r2, jnp.zeros((6, 2 * S5_STATE), F32)], axis=0)
        lane = lax.broadcasted_iota(jnp.int32, (CHUNK, W), 1)
        srow = lax.broadcasted_iota(jnp.int32, (CHUNK, W), 0)
        keep = lane < S5_CH * (srow + 1)
        for hh in range(S5_CH):
            dt_h = _dot_tn(ur_ref[0, :, hh * CHUNK:(hh + 1) * CHUNK], dy)
            back = pltpu.roll(dt_h, S5_CH, 1, stride=S5_CH, stride_axis=0)
            dk_ref[0, hh:hh + 1, :] = jnp.sum(jnp.where(keep, back, 0.0), axis=0, keepdims=True)

    spec_g = lambda a, b: pl.BlockSpec((1, a, b), lambda g: (g, 0, 0))
    return pl.pallas_call(
        body, name="s5_core_bwd", grid=(G,),
        in_specs=[spec_g(nc, W), spec_g(nc, W), spec_g(nc, W), spec_g(nc, 2 * S5_STATE), spec_g(S5_CH, W), spec_g(W, 2 * S5_STATE),
                  spec_g(2 * S5_STATE, W), spec_g(8, 2 * S5_STATE), spec_g(8, 2 * S5_STATE)],
        out_specs=[spec_g(nc, W), spec_g(S5_CH, W), spec_g(W, 2 * S5_STATE), spec_g(2 * S5_STATE, W), spec_g(8, 2 * S5_STATE)],
        out_shape=[jax.ShapeDtypeStruct((G, nc, W), BF), jax.ShapeDtypeStruct((G, S5_CH, W), F32),
                   jax.ShapeDtypeStruct((G, W, 2 * S5_STATE), F32), jax.ShapeDtypeStruct((G, 2 * S5_STATE, W), F32),
                   jax.ShapeDtypeStruct((G, 8, 2 * S5_STATE), F32)],
        scratch_shapes=[pltpu.VMEM((W, W), BF)],
        compiler_params=_params(("arbitrary",)),
    )(uflat, urev, dyflat, hsave, kmat, wst, woff, p1, p2)


def _flat_hs(a, nc, reverse_time=False):
    a = a.reshape(nc, CHUNK, S5_GROUPS, S5_CH)
    if reverse_time:
        a = a[:, ::-1]
    return a.transpose(2, 0, 3, 1).reshape(S5_GROUPS, nc, CHUNK * S5_CH)


def _unflat_hs(a, nc):
    return a.reshape(S5_GROUPS, nc, S5_CH, CHUNK).transpose(1, 3, 0, 2).reshape(nc * CHUNK, D_MODEL)


def _flat_tk(a, nc):
    return a.reshape(nc, CHUNK, S5_GROUPS, S5_CH).transpose(2, 0, 1, 3).reshape(S5_GROUPS, nc, CHUNK * S5_CH)


def _unflat_tk(a, nc):
    return a.reshape(S5_GROUPS, nc, CHUNK, S5_CH).transpose(1, 2, 0, 3).reshape(nc * CHUNK, D_MODEL)


def _s5_post_fwd(yssm, proj, s5_d, w_glu, b_glu):
    L = yssm.shape[0]
    tl = min(TOKEN_TILE, L)

    def body(ys_ref, u_ref, z_ref, d_ref, wg_ref, bg_ref, o_ref):
        u = u_ref[...].astype(F32)
        a = _gelu(ys_ref[...].astype(F32) + d_ref[...] * u)
        y = a * _sigmoid(_dot(a.astype(BF), wg_ref[...]) + bg_ref[...])
        z = z_ref[...].astype(F32)
        o_ref[...] = (y * z * _sigmoid(z)).astype(BF)

    return pl.pallas_call(
        body, name="s5_post_fwd", grid=(L // tl,),
        in_specs=[_row_spec(tl, D_MODEL), _row_spec(tl, D_MODEL, 0), _row_spec(tl, D_MODEL, 1), _const_spec((1, D_MODEL)),
                  _const_spec((D_MODEL, D_MODEL)), _const_spec((1, D_MODEL))],
        out_specs=_row_spec(tl, D_MODEL),
        out_shape=jax.ShapeDtypeStruct((L, D_MODEL), BF),
        compiler_params=_params(("arbitrary",)),
    )(yssm, proj, proj, s5_d, w_glu, b_glu)


def _s5_post_bwd(dys5, yssm, proj, s5_d, w_glu, b_glu):
    L = yssm.shape[0]
    tl = min(TOKEN_TILE, L)

    def body(dy_ref, ys_ref, u_ref, z_ref, d_ref, wg_ref, bg_ref, dz_ref, dys_ref, a_ref, dgl_ref, dbg_ref, dd_ref):
        @pl.when(pl.program_id(0) == 0)
        def _():
            dbg_ref[...] = jnp.zeros_like(dbg_ref)
            dd_ref[...] = jnp.zeros_like(dd_ref)

        u = u_ref[...].astype(F32)
        y0 = ys_ref[...].astype(F32) + d_ref[...] * u
        a = _gelu(y0)
        a_bf = a.astype(BF)
        sg = _sigmoid(_dot(a_bf, wg_ref[...]) + bg_ref[...])
        y = a * sg
        z = z_ref[...].astype(F32)
        sz = _sigmoid(z)
        dout = dy_ref[...].astype(F32)
        dz_ref[...] = (dout * y * sz * (1.0 + z * (1.0 - sz))).astype(BF)
        dyv = dout * z * sz
        dgl = dyv * a * sg * (1.0 - sg)
        dgl_bf = dgl.astype(BF)
        da = dyv * sg + _dot_nt(dgl_bf, wg_ref[...])
        dy0 = da * _gelu_grad(y0)
        dbg_ref[...] += jnp.sum(dgl, axis=0, keepdims=True)
        dd_ref[...] += jnp.sum(dy0 * u, axis=0, keepdims=True)
        dys_ref[...] = dy0.astype(BF)
        a_ref[...] = a_bf
        dgl_ref[...] = dgl_bf

    big = jax.ShapeDtypeStruct((L, D_MODEL), BF)
    vec = jax.ShapeDtypeStruct((1, D_MODEL), F32)
    return pl.pallas_call(
        body, name="s5_post_bwd", grid=(L // tl,),
        in_specs=[_row_spec(tl, D_MODEL), _row_spec(tl, D_MODEL), _row_spec(tl, D_MODEL, 0), _row_spec(tl, D_MODEL, 1),
                  _const_spec((1, D_MODEL)), _const_spec((D_MODEL, D_MODEL)), _const_spec((1, D_MODEL))],
        out_specs=[_row_spec(tl, D_MODEL)] * 4 + [_const_spec((1, D_MODEL))] * 2,
        out_shape=[big, big, big, big, vec, vec],
        compiler_params=_params(("arbitrary",)),
    )(dys5, yssm, proj, proj, s5_d, w_glu, b_glu)


def _cumsum_rows(a):
    row = lax.broadcasted_iota(jnp.int32, a.shape, 0)
    d = 1
    while d < a.shape[0]:
        a = a + jnp.where(row >= d, pltpu.roll(a, d, 0), 0.0)
        d *= 2
    return a


def _rev_cumsum_rows(a):
    n = a.shape[0]
    row = lax.broadcasted_iota(jnp.int32, a.shape, 0)
    d = 1
    while d < n:
        a = a + jnp.where(row < n - d, pltpu.roll(a, n - d, 0), 0.0)
        d *= 2
    return a


def _ssd_conv_fwd(first, xs_ref, bc_ref, hx_ref, hb_ref, cw_ref, cb_ref, xp_ref, tl):
    hal = jnp.concatenate([hx_ref[...], hb_ref[...]], axis=1).astype(F32)
    xp_ref[0:8, :] = jnp.where(first, 0.0, hal)
    xp_ref[8:8 + tl, 0:1024] = xs_ref[...].astype(F32)
    xp_ref[8:8 + tl, 1024:2048] = bc_ref[...].astype(F32)
    pre = cb_ref[...] + cw_ref[0:1, :] * xp_ref[5:5 + tl, :]
    for k in range(1, 4):
        pre = pre + cw_ref[k:k + 1, :] * xp_ref[5 + k:5 + k + tl, :]
    return pre


def _onehot_lane(h):
    return (lax.broadcasted_iota(jnp.int32, (1, LANES), 1) == h).astype(F32)


def _ssd_specs_in(tl, nt, rev):
    t_of = (lambda i: nt - 1 - i) if rev else (lambda i: i)
    rows = lambda w, col: pl.BlockSpec((tl, w), lambda i: (t_of(i), col))
    halo = lambda col: pl.BlockSpec((8, 1024), lambda i: (jnp.maximum(t_of(i) * (tl // 8) - 1, 0), col))
    return t_of, rows, halo


def _ssd_fwd(proj, pdt, conv_w, conv_b, dt_bias, a_log, ssd_d, norm_w):
    L = proj.shape[0]
    tl = min(TOKEN_TILE, L)
    nt, ncl = L // tl, tl // CHUNK
    _, rows, halo = _ssd_specs_in(tl, nt, False)

    def body(xs_ref, bc_ref, hx_ref, hb_ref, dt_ref, z_ref, cw_ref, cb_ref, dtb_ref, al_ref, dd_ref, nw_ref,
             y_ref, ypre_ref, st_ref, xp_ref, xbc_ref, dts_ref, yp_ref, hst_ref):
        i = pl.program_id(0)

        @pl.when(i == 0)
        def _():
            hst_ref[...] = jnp.zeros_like(hst_ref)

        pre = _ssd_conv_fwd(i == 0, xs_ref, bc_ref, hx_ref, hb_ref, cw_ref, cb_ref, xp_ref, tl)
        xbc_ref[...] = pre * _sigmoid(pre)
        dts_ref[...] = _softplus(dt_ref[...] + dtb_ref[...])
        a_neg = -jnp.exp(al_ref[...])
        li = lax.broadcasted_iota(jnp.int32, (CHUNK, CHUNK), 0)
        si = lax.broadcasted_iota(jnp.int32, (CHUNK, CHUNK), 1)
        causal = li >= si

        def chunk(c, carry):
            r0 = pl.multiple_of(c * CHUNK, CHUNK)
            xbc = xbc_ref[pl.ds(r0, CHUNK), :]
            dtc = dts_ref[pl.ds(r0, CHUNK), :]
            acs = _cumsum_rows(dtc * a_neg)
            acs_t = acs.T
            for j in range(SSD_GROUPS):
                bj = xbc[:, 1024 + 128 * j:1024 + 128 * (j + 1)].astype(BF)
                cj = xbc[:, 1536 + 128 * j:1536 + 128 * (j + 1)].astype(BF)
                g = _dot_nt(cj, bj)
                hj = hst_ref[256 * j:256 * (j + 1), :]
                zj = _dot_nt(cj, hj.astype(BF))
                xdd, ecol = [], []
                for hh in range(4):
                    h = 4 * j + hh
                    col = acs[:, h:h + 1]
                    lm = jnp.where(causal, jnp.exp(col - acs_t[h:h + 1, :]), 0.0)
                    xs_h = xbc[:, 64 * h:64 * (h + 1)]
                    xd = xs_h * dtc[:, h:h + 1]
                    y = _dot((g * lm).astype(BF), xd.astype(BF))
                    y = y + jnp.exp(col) * zj[:, 64 * hh:64 * (hh + 1)] + dd_ref[0:1, h:h + 1] * xs_h
                    yp_ref[pl.ds(r0, CHUNK), 64 * h:64 * (h + 1)] = y
                    last = acs[CHUNK - 1:CHUNK, h:h + 1]
                    xdd.append(xd * jnp.exp(last - col))
                    ecol.append(jnp.broadcast_to(jnp.exp(last), (SSD_HEAD_DIM, 1)))
                st = _dot_tn(jnp.concatenate(xdd, axis=1).astype(BF), bj)
                st_ref[c, 256 * j:256 * (j + 1), :] = hj
                hst_ref[256 * j:256 * (j + 1), :] = jnp.concatenate(ecol, axis=0) * hj + st
            return carry

        lax.fori_loop(0, ncl, chunk, 0)
        ypre = yp_ref[...]
        z = z_ref[...].astype(F32)
        gg = ypre * z * _sigmoid(z)
        for j in range(SSD_GROUPS):
            seg = gg[:, 256 * j:256 * (j + 1)]
            r = lax.rsqrt(jnp.mean(seg * seg, axis=-1, keepdims=True) + EPS)
            y_ref[:, 256 * j:256 * (j + 1)] = (seg * r * nw_ref[:, 256 * j:256 * (j + 1)]).astype(BF)
        ypre_ref[...] = ypre.astype(BF)

    nc = L // CHUNK
    return pl.pallas_call(
        body, name="ssd_fwd", grid=(nt,),
        in_specs=[rows(1024, 3), rows(1024, 4), halo(3), halo(4), rows(LANES, 0), rows(1024, 2),
                  _const_spec((4, 2048)), _const_spec((1, 2048)), _const_spec((1, LANES)), _const_spec((1, LANES)),
                  _const_spec((1, LANES)), _const_spec((1, D_MODEL))],
        out_specs=[_row_spec(tl, D_MODEL), _row_spec(tl, D_MODEL), pl.BlockSpec((ncl, 1024, SSD_STATE), lambda i: (i, 0, 0))],
        out_shape=[jax.ShapeDtypeStruct((L, D_MODEL), BF), jax.ShapeDtypeStruct((L, D_MODEL), BF),
                   jax.ShapeDtypeStruct((nc, 1024, SSD_STATE), F32)],
        scratch_shapes=[pltpu.VMEM((tl + 8, 2048), F32), pltpu.VMEM((tl, 2048), F32), pltpu.VMEM((tl, LANES), F32),
                        pltpu.VMEM((tl, D_MODEL), F32), pltpu.VMEM((1024, SSD_STATE), F32)],
        compiler_params=_params(("arbitrary",)),
    )(proj, proj, proj, proj, pdt, proj, conv_w, conv_b, dt_bias, a_log, ssd_d, norm_w)


def _ssd_bwd(dyssd, ypre, proj, pdt, states, conv_w, conv_b, dt_bias, a_log, ssd_d, norm_w):
    L = proj.shape[0]
    tl = min(TOKEN_TILE, L)
    nt, ncl = L // tl, tl // CHUNK
    t_of, rows, halo = _ssd_specs_in(tl, nt, True)

    def body(dy_ref, ypre_ref, z_ref, xs_ref, bc_ref, hx_ref, hb_ref, dt_ref, st_ref, cw_ref, cb_ref, dtb_ref, al_ref,
             dd_ref, nw_ref,
             dxbc_ref, ddt_ref, dz_ref, dcw_ref, dcb_ref, ddtb_ref, dal_ref, ddd_ref, dnw_ref,
             xp_ref, xbc_ref, pre_ref, dts_ref, dyp_ref, dxs_ref, ddts_ref, dp_ref, dh_ref):
        i = pl.program_id(0)

        @pl.when(i == 0)
        def _():
            for r in (dcw_ref, dcb_ref, ddtb_ref, dal_ref, ddd_ref, dnw_ref, dh_ref):
                r[...] = jnp.zeros_like(r)
            dp_ref[tl:tl + 8, :] = jnp.zeros((8, 2048), F32)

        pre = _ssd_conv_fwd(t_of(i) == 0, xs_ref, bc_ref, hx_ref, hb_ref, cw_ref, cb_ref, xp_ref, tl)
        pre_ref[...] = pre
        xbc_ref[...] = pre * _sigmoid(pre)
        dts_ref[...] = _softplus(dt_ref[...] + dtb_ref[...])
        a_neg = -jnp.exp(al_ref[...])

        ypre = ypre_ref[...].astype(F32)
        z = z_ref[...].astype(F32)
        sz = _sigmoid(z)
        gg = ypre * z * sz
        dout = dy_ref[...].astype(F32)
        for j in range(SSD_GROUPS):
            sl = slice(256 * j, 256 * (j + 1))
            seg = gg[:, sl]
            r = lax.rsqrt(jnp.mean(seg * seg, axis=-1, keepdims=True) + EPS)
            gh = seg * r
            dnw_ref[:, sl] += jnp.sum(dout[:, sl] * gh, axis=0, keepdims=True)
            gw = dout[:, sl] * nw_ref[:, sl]
            dgg = r * (gw - gh * jnp.mean(gw * gh, axis=-1, keepdims=True))
            dyp_ref[:, sl] = dgg * z[:, sl] * sz[:, sl]
            dz_ref[:, sl] = (dgg * ypre[:, sl] * sz[:, sl] * (1.0 + z[:, sl] * (1.0 - sz[:, sl]))).astype(BF)

        li = lax.broadcasted_iota(jnp.int32, (CHUNK, CHUNK), 0)
        si = lax.broadcasted_iota(jnp.int32, (CHUNK, CHUNK), 1)
        causal = li >= si
        last_row = (lax.broadcasted_iota(jnp.int32, (CHUNK, 1), 0) == CHUNK - 1).astype(F32)

        def chunk(k, carry):
            dal_acc, ddd_acc = carry
            c = ncl - 1 - k
            r0 = pl.multiple_of(c * CHUNK, CHUNK)
            xbc = xbc_ref[pl.ds(r0, CHUNK), :]
            dtc = dts_ref[pl.ds(r0, CHUNK), :]
            dyp = dyp_ref[pl.ds(r0, CHUNK), :]
            acs = _cumsum_rows(dtc * a_neg)
            acs_t = acs.T
            dacs = jnp.zeros((CHUNK, LANES), F32)
            ddtc = jnp.zeros((CHUNK, LANES), F32)
            for j in range(SSD_GROUPS):
                bj = xbc[:, 1024 + 128 * j:1024 + 128 * (j + 1)].astype(BF)
                cj = xbc[:, 1536 + 128 * j:1536 + 128 * (j + 1)].astype(BF)
                g = _dot_nt(cj, bj)
                hj = st_ref[c, 256 * j:256 * (j + 1), :]
                hj_bf = hj.astype(BF)
                dhj = dh_ref[256 * j:256 * (j + 1), :]
                dhj_bf = dhj.astype(BF)
                zj = _dot_nt(cj, hj_bf)
                qj = _dot_nt(bj, dhj_bf)
                dg = jnp.zeros((CHUNK, CHUNK), F32)
                dz_l, xdd, ecol = [], [], []
                for hh in range(4):
                    h = 4 * j + hh
                    oh = _onehot_lane(h)
                    col = acs[:, h:h + 1]
                    lm = jnp.where(causal, jnp.exp(col - acs_t[h:h + 1, :]), 0.0)
                    sc = g * lm
                    xs_h = xbc[:, 64 * h:64 * (h + 1)]
                    dt_h = dtc[:, h:h + 1]
                    xd = xs_h * dt_h
                    dy_h = dyp[:, 64 * h:64 * (h + 1)]
                    dy_bf = dy_h.astype(BF)
                    e_col = jnp.exp(col)
                    last = acs[CHUNK - 1:CHUNK, h:h + 1]
                    dec = jnp.exp(last - col)
                    elast = jnp.exp(last)
                    ddd_acc = ddd_acc + oh * jnp.sum(dy_h * xs_h)
                    dxs = dd_ref[0:1, h:h + 1] * dy_h
                    dsc = _dot_nt(dy_bf, xd.astype(BF))
                    dxd = _dot_tn(sc.astype(BF), dy_bf)
                    m = dsc * sc
                    dg = dg + dsc * lm
                    dacs_h = jnp.sum(m, axis=1, keepdims=True) - jnp.sum(m.T, axis=1, keepdims=True)
                    z_h = zj[:, 64 * hh:64 * (hh + 1)]
                    dz_h = dy_h * e_col
                    dacs_h = dacs_h + jnp.sum(dz_h * z_h, axis=1, keepdims=True)
                    q_h = qj[:, 64 * hh:64 * (hh + 1)]
                    dxd = dxd + q_h * dec
                    wdec = jnp.sum(q_h * xd, axis=1, keepdims=True) * dec
                    tot = jnp.sum(wdec) + elast * jnp.sum(dhj[64 * hh:64 * (hh + 1), :] * hj[64 * hh:64 * (hh + 1), :])
                    dacs_h = dacs_h - wdec + last_row * tot
                    dxs = dxs + dxd * dt_h
                    dxs_ref[pl.ds(r0, CHUNK), 64 * h:64 * (h + 1)] = dxs
                    dacs = dacs + oh * dacs_h
                    ddtc = ddtc + oh * jnp.sum(dxd * xs_h, axis=1, keepdims=True)
                    dz_l.append(dz_h)
                    xdd.append(xd * dec)
                    ecol.append(jnp.broadcast_to(elast, (SSD_HEAD_DIM, 1)))
                dzj = jnp.concatenate(dz_l, axis=1).astype(BF)
                xdj = jnp.concatenate(xdd, axis=1).astype(BF)
                dg_bf = dg.astype(BF)
                dxs_ref[pl.ds(r0, CHUNK), 1536 + 128 * j:1536 + 128 * (j + 1)] = _dot(dg_bf, bj) + _dot(dzj, hj_bf)
                dxs_ref[pl.ds(r0, CHUNK), 1024 + 128 * j:1024 + 128 * (j + 1)] = _dot_tn(dg_bf, cj) + _dot(xdj, dhj_bf)
                dh_ref[256 * j:256 * (j + 1), :] = jnp.concatenate(ecol, axis=0) * dhj + _dot_tn(dzj, cj)
            da = _rev_cumsum_rows(dacs)
            ddts_ref[pl.ds(r0, CHUNK), :] = ddtc + da * a_neg
            dal_acc = dal_acc + jnp.sum(da * dtc, axis=0, keepdims=True)
            return dal_acc, ddd_acc

        zero_row = jnp.zeros((1, LANES), F32)
        dal_acc, ddd_acc = lax.fori_loop(0, ncl, chunk, (zero_row, zero_row))
        dal_ref[...] += dal_acc * a_neg
        ddd_ref[...] += ddd_acc
        ddt_raw = ddts_ref[...] * _sigmoid(dt_ref[...] + dtb_ref[...])
        ddt_ref[...] = ddt_raw
        ddtb_ref[...] += jnp.sum(ddt_raw, axis=0, keepdims=True)

        pre = pre_ref[...]
        sp = _sigmoid(pre)
        dpre = dxs_ref[...] * sp * (1.0 + pre * (1.0 - sp))
        dp_ref[0:tl, :] = dpre
        dcb_ref[...] += jnp.sum(dpre, axis=0, keepdims=True)
        dx = jnp.zeros((tl, 2048), F32)
        for k in range(4):
            dcw_ref[k:k + 1, :] += jnp.sum(dpre * xp_ref[5 + k:5 + k + tl, :], axis=0, keepdims=True)
            dx = dx + cw_ref[k:k + 1, :] * dp_ref[3 - k:3 - k + tl, :]
        dxbc_ref[...] = dx.astype(BF)
        dp_ref[tl:tl + 8, :] = dp_ref[0:8, :]

    vec = lambda w: jax.ShapeDtypeStruct((1, w), F32)
    rrow = lambda w: pl.BlockSpec((tl, w), lambda i: (t_of(i), 0))
    return pl.pallas_call(
        body, name="ssd_bwd", grid=(nt,),
        in_specs=[rrow(D_MODEL), rrow(D_MODEL), rows(1024, 2), rows(1024, 3), rows(1024, 4), halo(3), halo(4), rows(LANES, 0),
                  pl.BlockSpec((ncl, 1024, SSD_STATE), lambda i: (t_of(i), 0, 0)),
                  _const_spec((4, 2048)), _const_spec((1, 2048)), _const_spec((1, LANES)), _const_spec((1, LANES)),
                  _const_spec((1, LANES)), _const_spec((1, D_MODEL))],
        out_specs=[rrow(2048), rrow(LANES), rrow(D_MODEL), _const_spec((8, 2048)), _const_spec((1, 2048)),
                   _const_spec((1, LANES)), _const_spec((1, LANES)), _const_spec((1, LANES)), _const_spec((1, D_MODEL))],
        out_shape=[jax.ShapeDtypeStruct((L, 2048), BF), jax.ShapeDtypeStruct((L, LANES), F32), jax.ShapeDtypeStruct((L, D_MODEL), BF),
                   jax.ShapeDtypeStruct((8, 2048), F32), vec(2048), vec(LANES), vec(LANES), vec(LANES), vec(D_MODEL)],
        scratch_shapes=[pltpu.VMEM((tl + 8, 2048), F32), pltpu.VMEM((tl, 2048), F32), pltpu.VMEM((tl, 2048), F32),
                        pltpu.VMEM((tl, LANES), F32), pltpu.VMEM((tl, D_MODEL), F32), pltpu.VMEM((tl, 2048), F32),
                        pltpu.VMEM((tl, LANES), F32), pltpu.VMEM((tl + 8, 2048), F32), pltpu.VMEM((1024, SSD_STATE), F32)],
        compiler_params=_params(("arbitrary",)),
    )(dyssd, ypre, proj, proj, proj, proj, proj, pdt, states, conv_w, conv_b, dt_bias, a_log, ssd_d, norm_w)


def _head_fwd_bwd(x, ys5, yssd, p, target, w_out, w_gate, w_proj, ple_nw, fin_nw):
    L = x.shape[0]
    tl = min(TOKEN_TILE, L)
    inv_d = 1.0 / D_MODEL

    def body(x_ref, ys_ref, yd_ref, p_ref, t_ref, wo_ref, wg_ref, wp_ref, pnw_ref, fnw_ref,
             loss_ref, dys_ref, dyd_ref, dh1_ref, n2_ref, dgl_ref, dpp_ref, dpnw_ref, dfnw_ref):
        @pl.when(pl.program_id(0) == 0)
        def _():
            loss_ref[...] = jnp.zeros_like(loss_ref)
            dpnw_ref[...] = jnp.zeros_like(dpnw_ref)
            dfnw_ref[...] = jnp.zeros_like(dfnw_ref)

        h1 = x_ref[...] + _dot(ys_ref[...], wo_ref[0:1024, :]) + _dot(yd_ref[...], wo_ref[1024:2048, :])
        r1 = lax.rsqrt(jnp.mean(h1 * h1, axis=-1, keepdims=True) + EPS)
        hh1 = h1 * r1
        n2 = (hh1 * pnw_ref[...]).astype(BF)
        gate = _sigmoid(_dot(n2, wg_ref[...]))
        pp = _dot(p_ref[...].astype(BF), wp_ref[...])
        h2 = h1 + pp * gate
        r2 = lax.rsqrt(jnp.mean(h2 * h2, axis=-1, keepdims=True) + EPS)
        hh2 = h2 * r2
        err = hh2 * fnw_ref[...] - t_ref[...]
        loss_ref[...] += 0.5 * inv_d * jnp.sum(err * err)
        dyo = err * inv_d
        dfnw_ref[...] += jnp.sum(dyo * hh2, axis=0, keepdims=True)
        g2 = dyo * fnw_ref[...]
        dh2 = r2 * (g2 - hh2 * jnp.mean(g2 * hh2, axis=-1, keepdims=True))
        dpp_ref[...] = (dh2 * gate).astype(BF)
        dgl = (dh2 * pp * gate * (1.0 - gate)).astype(BF)
        dgl_ref[...] = dgl
        n2_ref[...] = n2
        dn2 = _dot_nt(dgl, wg_ref[...])
        dpnw_ref[...] += jnp.sum(dn2 * hh1, axis=0, keepdims=True)
        g1 = dn2 * pnw_ref[...]
        dh1 = dh2 + r1 * (g1 - hh1 * jnp.mean(g1 * hh1, axis=-1, keepdims=True))
        dh1_ref[...] = dh1
        dh1_bf = dh1.astype(BF)
        dys_ref[...] = _dot_nt(dh1_bf, wo_ref[0:1024, :]).astype(BF)
        dyd_ref[...] = _dot_nt(dh1_bf, wo_ref[1024:2048, :]).astype(BF)

    big = jax.ShapeDtypeStruct((L, D_MODEL), BF)
    vec = jax.ShapeDtypeStruct((1, D_MODEL), F32)
    return pl.pallas_call(
        body, name="head_fwd_bwd", grid=(L // tl,),
        in_specs=[_row_spec(tl, D_MODEL), _row_spec(tl, D_MODEL), _row_spec(tl, D_MODEL), _row_spec(tl, 256), _row_spec(tl, D_MODEL),
                  _const_spec((2048, D_MODEL)), _const_spec((D_MODEL, D_MODEL)), _const_spec((256, D_MODEL)),
                  _const_spec((1, D_MODEL)), _const_spec((1, D_MODEL))],
        out_specs=[_const_spec((8, LANES)), _row_spec(tl, D_MODEL), _row_spec(tl, D_MODEL), _row_spec(tl, D_MODEL),
                   _row_spec(tl, D_MODEL), _row_spec(tl, D_MODEL), _row_spec(tl, D_MODEL), _const_spec((1, D_MODEL)), _const_spec((1, D_MODEL))],
        out_shape=[jax.ShapeDtypeStruct((8, LANES), F32), big, big, jax.ShapeDtypeStruct((L, D_MODEL), F32), big, big, big, vec, vec],
        compiler_params=_params(("arbitrary",)),
    )(x, ys5, yssd, p, target, w_out, w_gate, w_proj, ple_nw, fin_nw)


def _pad_lanes(v):
    return jnp.pad(v.reshape(1, -1), ((0, 0), (0, LANES - v.size)))


def _local_step(x, p, target, w):
    L = x.shape[0]
    nc = L // CHUNK
    nsteps = max(1, (nc - 1).bit_length())
    w_in = w["w_in"]
    w_main = w_in[:, :D_MAIN]
    w_dt = jnp.pad(w_in[:, D_MAIN:], ((0, 0), (0, LANES - SSD_HEADS)))
    norm_w = w["norm_w"].reshape(1, -1)
    s5_d = w["s5_D"].reshape(1, -1)
    b_glu = w["s5_b_glu"].reshape(1, -1)
    conv_b = w["conv_b"].reshape(1, -1)
    dtb, alog, ssd_d = _pad_lanes(w["dt_bias"]), _pad_lanes(w["A_log"]), _pad_lanes(w["ssd_D"])
    ssd_nw = w["ssd_norm_w"].reshape(1, -1)
    ple_nw = w["ple_norm_w"].reshape(1, -1)
    fin_nw = w["final_norm_w"].reshape(1, -1)

    s5_args = (w["s5_A_re"], w["s5_A_im"], w["s5_log_dt"], w["s5_B_re"], w["s5_B_im"], w["s5_C_re"], w["s5_C_im"])
    (kmat, wst, woff, _a64), tables_vjp = jax.vjp(_s5_tables, *s5_args)
    p1, p2 = _s5_scan_powers(w["s5_A_re"], w["s5_A_im"], w["s5_log_dt"], nsteps)
    wst_bf, woff_bf = wst.astype(BF), woff.astype(BF)

    hn, proj, pdt = _in_proj_fwd(x, norm_w, w_main, w_dt)
    uflat = _flat_hs(proj[:, :D_MODEL], nc)
    yflat, hsave = _s5_core_fwd(uflat, kmat, wst_bf, woff_bf, p1, p2)
    yssm = _unflat_tk(yflat, nc)
    ys5 = _s5_post_fwd(yssm, proj, s5_d, w["s5_w_glu"], b_glu)
    yssd, ypre, states = _ssd_fwd(proj, pdt, w["conv_w"], conv_b, dtb, alog, ssd_d, ssd_nw)
    (loss8, dys5, dyssd, dh1, n2, dgl2, dpp, g_ple_nw, g_fin_nw) = _head_fwd_bwd(
        x, ys5, yssd, p, target, w["w_out"], w["w_ple_gate"], w["w_ple_proj"], ple_nw, fin_nw)

    (dxbc, ddt, dzd, g_cw, g_cb, g_dtb, g_alog, g_ssd_d, g_ssd_nw) = _ssd_bwd(
        dyssd, ypre, proj, pdt, states, w["conv_w"], conv_b, dtb, alog, ssd_d, ssd_nw)
    dzs, dyssm, a_glu, dgl1, g_bglu, g_s5d = _s5_post_bwd(dys5, yssm, proj, s5_d, w["s5_w_glu"], b_glu)
    urev = _flat_hs(proj[:, :D_MODEL], nc, reverse_time=True)
    duflat, dkmat, dwst, dwoff, da8 = _s5_core_bwd(uflat, urev, _flat_tk(dyssm, nc), hsave, kmat, wst_bf, woff_bf, p1, p2)
    da64 = jnp.concatenate([da8[:, 0, :S5_STATE] + da8[:, 0, S5_STATE:], da8[:, 1, S5_STATE:] - da8[:, 1, :S5_STATE]], axis=-1)
    g_s5 = tables_vjp((dkmat, dwst, dwoff, da64))
    gx, du, g_norm_w = _in_proj_bwd(x, norm_w, dh1, _unflat_hs(duflat, nc), dyssm, s5_d, dzs, dzd, dxbc, ddt, w_main, w_dt)

    g_w_in = jnp.concatenate([
        _matmul_tn(hn, du, "dw_in_u"), _matmul_tn(hn, dzs, "dw_in_zs"), _matmul_tn(hn, dzd, "dw_in_zd"),
        _matmul_tn(hn, dxbc, "dw_in_xbc"), _matmul_tn(hn, ddt, "dw_in_dt")[:, :SSD_HEADS]], axis=1)
    grads = {
        "norm_w": g_norm_w, "w_in": g_w_in,
        "s5_A_re": g_s5[0], "s5_A_im": g_s5[1], "s5_log_dt": g_s5[2], "s5_B_re": g_s5[3], "s5_B_im": g_s5[4],
        "s5_C_re": g_s5[5], "s5_C_im": g_s5[6], "s5_D": g_s5d, "s5_w_glu": _matmul_tn(a_glu, dgl1, "dw_glu"), "s5_b_glu": g_bglu,
        "conv_w": g_cw[:4], "conv_b": g_cb, "dt_bias": g_dtb[:, :SSD_HEADS], "A_log": g_alog[:, :SSD_HEADS],
        "ssd_D": g_ssd_d[:, :SSD_HEADS], "ssd_norm_w": g_ssd_nw,
        "w_out": jnp.concatenate([_matmul_tn(ys5, dh1, "dw_out_s5"), _matmul_tn(yssd, dh1, "dw_out_ssd")], axis=0),
        "ple_norm_w": g_ple_nw, "w_ple_gate": _matmul_tn(n2, dgl2, "dw_gate"), "w_ple_proj": _matmul_tn(p, dpp, "dw_proj"),
        "final_norm_w": g_fin_nw,
    }
    return loss8[0, 0], gx, grads


WEIGHTS = ("norm_w", "w_in", "s5_A_re", "s5_A_im", "s5_log_dt", "s5_B_re", "s5_B_im", "s5_C_re", "s5_C_im", "s5_D", "s5_w_glu",
           "s5_b_glu", "conv_w", "conv_b", "dt_bias", "A_log", "ssd_D", "ssd_norm_w", "w_out", "ple_norm_w", "w_ple_gate",
           "w_ple_proj", "final_norm_w")
BIG = {"w_in": ((1024, 1284), 1), "s5_w_glu": ((256, 1024), 0), "w_out": ((512, 1024), 0), "w_ple_gate": ((256, 1024), 0),
       "w_ple_proj": ((256, 256), 1)}
SMALL = {"norm_w": (1024,), "s5_A_re": (64, 64), "s5_A_im": (64, 64), "s5_log_dt": (64,), "s5_B_re": (64, 64, 16),
         "s5_B_im": (64, 64, 16), "s5_C_re": (64, 16, 64), "s5_C_im": (64, 16, 64), "s5_D": (1024,), "s5_b_glu": (1024,),
         "conv_w": (4, 2048), "conv_b": (2048,), "dt_bias": (16,), "A_log": (16,), "ssd_D": (16,), "ssd_norm_w": (1024,),
         "ple_norm_w": (1024,), "final_norm_w": (1024,)}
BIG_ROWS = {n: s[0] * s[1] // LANES for n, (s, _) in BIG.items()}
BIG_ROWS_TOTAL = sum(BIG_ROWS.values())
SMALL_TOTAL = sum(math.prod(s) for s in SMALL.values())
SMALL_PIECE_ROWS = -(-SMALL_TOTAL // (N_CHIPS * 16 * LANES)) * 16
HALF_ROWS = (BIG_ROWS_TOTAL + SMALL_PIECE_ROWS) // 2
SMALL_ROW0 = BIG_ROWS_TOTAL - HALF_ROWS


def _mesh_pos():
    return lax.axis_index("x"), lax.axis_index("y"), lax.axis_index("c")


def _other_chips(x, y):
    return [(1 - x, y), (x, 1 - y), (1 - x, 1 - y)]


def _comm_params():
    return pltpu.CompilerParams(has_side_effects=True)


def _all_gather_chips(wpack, cw):
    def body(w_ref, c_ref, wo_ref, co_ref, send_sems, recv_sems, loc_sems):
        x, y, c = _mesh_pos()
        me = 2 * x + y
        loc = [pltpu.make_async_copy(w_ref, wo_ref.at[me], loc_sems.at[0]),
               pltpu.make_async_copy(c_ref, co_ref.at[me], loc_sems.at[1])]
        for cp in loc:
            cp.start()
        cps = []
        for k, (px, py) in enumerate(_other_chips(x, y)):
            cps.append(pltpu.make_async_remote_copy(w_ref, wo_ref.at[me], send_sems.at[2 * k], recv_sems.at[2 * k],
                                                    device_id=(px, py, c), device_id_type=MESH))
            cps.append(pltpu.make_async_remote_copy(c_ref, co_ref.at[me], send_sems.at[2 * k + 1], recv_sems.at[2 * k + 1],
                                                    device_id=(px, py, c), device_id_type=MESH))
        for cp in cps:
            cp.start()
        for cp in cps + loc:
            cp.wait()

    return pl.pallas_call(
        body, name="all_gather_weights", in_specs=[ANY, ANY], out_specs=[ANY, ANY],
        out_shape=[jax.ShapeDtypeStruct((N_CHIPS,) + wpack.shape, wpack.dtype), jax.ShapeDtypeStruct((N_CHIPS,) + cw.shape, cw.dtype)],
        scratch_shapes=[pltpu.SemaphoreType.DMA((6,)), pltpu.SemaphoreType.DMA((6,)), pltpu.SemaphoreType.DMA((2,))],
        compiler_params=_comm_params(),
    )(wpack, cw)


def _exchange_pair(gp):
    def body(g_ref, r_ref, send_sems, recv_sems):
        x, y, c = _mesh_pos()
        cps = [pltpu.make_async_remote_copy(g_ref.at[s, 1 - c], r_ref.at[s], send_sems.at[s], recv_sems.at[s],
                                            device_id=(x, y, 1 - c), device_id_type=MESH) for s in range(N_CHIPS)]
        for cp in cps:
            cp.start()
        for cp in cps:
            cp.wait()

    return pl.pallas_call(
        body, name="grad_exchange_pair", in_specs=[ANY], out_specs=ANY,
        out_shape=jax.ShapeDtypeStruct((N_CHIPS,) + gp.shape[2:], gp.dtype),
        scratch_shapes=[pltpu.SemaphoreType.DMA((N_CHIPS,)), pltpu.SemaphoreType.DMA((N_CHIPS,))],
        compiler_params=_comm_params(),
    )(gp)


def _exchange_chips(ps):
    def body(p_ref, r_ref, send_sems, recv_sems, loc_sem):
        x, y, c = _mesh_pos()
        me = 2 * x + y
        loc = pltpu.make_async_copy(p_ref.at[me], r_ref.at[me], loc_sem)
        loc.start()
        cps = [pltpu.make_async_remote_copy(p_ref.at[2 * px + py], r_ref.at[me], send_sems.at[k], recv_sems.at[k],
                                            device_id=(px, py, c), device_id_type=MESH)
               for k, (px, py) in enumerate(_other_chips(x, y))]
        for cp in cps:
            cp.start()
        for cp in cps:
            cp.wait()
        loc.wait()

    return pl.pallas_call(
        body, name="grad_exchange_chips", in_specs=[ANY], out_specs=ANY,
        out_shape=jax.ShapeDtypeStruct(ps.shape, ps.dtype),
        scratch_shapes=[pltpu.SemaphoreType.DMA((3,)), pltpu.SemaphoreType.DMA((3,)), pltpu.SemaphoreType.DMA],
        compiler_params=_comm_params(),
    )(ps)


def _gather_reduced(gh):
    def body(g_ref, gs_ref, sm_ref, send_sems, recv_sems, sm_send, sm_recv, loc_sems):
        x, y, c = _mesh_pos()
        me = 2 * x + y
        small = g_ref.at[pl.ds(SMALL_ROW0, SMALL_PIECE_ROWS)]
        loc = pltpu.make_async_copy(g_ref, gs_ref.at[c], loc_sems.at[0])
        loc.start()
        half = pltpu.make_async_remote_copy(g_ref, gs_ref.at[c], send_sems.at[0], recv_sems.at[0],
                                            device_id=(x, y, 1 - c), device_id_type=MESH)
        half.start()
        others = _other_chips(x, y)
        dests = [((x, y, 0), 0)] + [((px, py, pc), 1 + k) for k, (px, py) in enumerate(others) for pc in (0, 1)]

        @pl.when(c == 1)
        def _():
            own = pltpu.make_async_copy(small, sm_ref.at[me], loc_sems.at[1])
            own.start()
            cps = [pltpu.make_async_remote_copy(small, sm_ref.at[me], sm_send.at[k], sm_recv.at[rs], device_id=d, device_id_type=MESH)
                   for k, (d, rs) in enumerate(dests)]
            for cp in cps:
                cp.start()
            for cp in cps:
                cp.wait_send()
            own.wait()

        def arrival(chip, sem):
            return pltpu.make_async_remote_copy(small, sm_ref.at[chip], sm_send.at[0], sm_recv.at[sem],
                                                device_id=(x, y, c), device_id_type=MESH)

        @pl.when(c == 0)
        def _():
            arrival(me, 0).wait_recv()

        for k, (px, py) in enumerate(others):
            arrival(2 * px + py, 1 + k).wait_recv()
        half.wait()
        loc.wait()

    return pl.pallas_call(
        body, name="grad_gather_reduced", in_specs=[ANY], out_specs=[ANY, ANY],
        out_shape=[jax.ShapeDtypeStruct((2,) + gh.shape, gh.dtype), jax.ShapeDtypeStruct((N_CHIPS, SMALL_PIECE_ROWS, LANES), gh.dtype)],
        scratch_shapes=[pltpu.SemaphoreType.DMA((1,)), pltpu.SemaphoreType.DMA((1,)), pltpu.SemaphoreType.DMA((7,)),
                        pltpu.SemaphoreType.DMA((4,)), pltpu.SemaphoreType.DMA((2,))],
        compiler_params=_comm_params(),
    )(gh)


def _add_rows(parts, name):
    shape = parts[0].shape
    flat = [a.reshape(-1, LANES) for a in parts]
    rows = flat[0].shape[0]
    nblk = next(k for k in (16, 13, 8, 4, 2, 1) if rows % (k * 8) == 0 or k == 1)
    rb = rows // nblk

    def body(*refs):
        acc = refs[0][...]
        for r in refs[1:-1]:
            acc = acc + r[...]
        refs[-1][...] = acc

    out = pl.pallas_call(
        body, name=name, grid=(nblk,), in_specs=[_row_spec(rb, LANES)] * len(flat), out_specs=_row_spec(rb, LANES),
        out_shape=jax.ShapeDtypeStruct((rows, LANES), parts[0].dtype), compiler_params=_params(("parallel",)),
    )(*flat)
    return out.reshape(shape)


def _pack_grads(grads):
    small = jnp.concatenate([grads[n].reshape(-1) for n in SMALL])
    small = jnp.pad(small, (0, N_CHIPS * SMALL_PIECE_ROWS * LANES - SMALL_TOTAL)).reshape(N_CHIPS, SMALL_PIECE_ROWS, LANES)
    pieces = []
    for s in range(N_CHIPS):
        rows = []
        for n, (shp, axis) in BIG.items():
            g = grads[n]
            blk = g[s * shp[0]:(s + 1) * shp[0], :] if axis == 0 else g[:, s * shp[1]:(s + 1) * shp[1]]
            rows.append(blk.reshape(-1, LANES))
        rows.append(small[s])
        pieces.append(jnp.concatenate(rows, axis=0).reshape(2, HALF_ROWS, LANES))
    return jnp.stack(pieces)


def _unpack_shard(gs):
    rows = gs.reshape(2 * HALF_ROWS, LANES)
    out, r0 = {}, 0
    for n, (shp, _) in BIG.items():
        out[n] = rows[r0:r0 + BIG_ROWS[n]].reshape(shp)
        r0 += BIG_ROWS[n]
    return out


def _unpack_small(sm):
    flat = sm.reshape(-1)
    out, o = {}, 0
    for n, shp in SMALL.items():
        k = math.prod(shp)
        out[n] = flat[o:o + k].reshape(shp)
        o += k
    return out


def _as_2d(a):
    n = a.size
    if a.ndim >= 2 and a.shape[-1] > 1024:
        return a.reshape(-1, a.shape[-1])
    if n % 1024 == 0:
        return a.reshape(n // 1024, 1024)
    return a.reshape(1, n)


def _adamw(w, g, m, v, name):
    shape = w.shape
    w2, g2, m2, v2 = (_as_2d(a) for a in (w, g, m, v))
    rows, cols = w2.shape
    rb = 256 if rows >= 512 else rows

    def body(w_ref, g_ref, m_ref, v_ref, d_ref, mo_ref, vo_ref):
        gv = g_ref[...]
        mn = ADAM_B1 * m_ref[...] + (1.0 - ADAM_B1) * gv
        vn = ADAM_B2 * v_ref[...] + (1.0 - ADAM_B2) * (gv * gv)
        m_hat = mn / (1.0 - ADAM_B1 ** ADAM_STEP)
        v_hat = vn / (1.0 - ADAM_B2 ** ADAM_STEP)
        d_ref[...] = -ADAM_LR * (m_hat / (jnp.sqrt(v_hat) + ADAM_EPS) + ADAM_WD * w_ref[...])
        mo_ref[...] = mn
        vo_ref[...] = vn

    spec = _row_spec(rb, cols)
    sds = jax.ShapeDtypeStruct((rows, cols), F32)
    d, mo, vo = pl.pallas_call(
        body, name=name, grid=(rows // rb,), in_specs=[spec] * 4, out_specs=[spec] * 3, out_shape=[sds] * 3,
        compiler_params=_params(("parallel",)),
    )(w2, g2, m2, v2)
    return d.reshape(shape), mo.reshape(shape), vo.reshape(shape)


def kernel(x, p, norm_w, w_in, s5_A_re, s5_A_im, s5_log_dt, s5_B_re, s5_B_im, s5_C_re, s5_C_im, s5_D, s5_w_glu, s5_b_glu, conv_w, conv_b, dt_bias, A_log, ssd_D, ssd_norm_w, w_out, ple_norm_w, w_ple_gate, w_ple_proj, final_norm_w, loss_target, m_norm_w, m_w_in, m_s5_A_re, m_s5_A_im, m_s5_log_dt, m_s5_B_re, m_s5_B_im, m_s5_C_re, m_s5_C_im, m_s5_D, m_s5_w_glu, m_s5_b_glu, m_conv_w, m_conv_b, m_dt_bias, m_A_log, m_ssd_D, m_ssd_norm_w, m_w_out, m_ple_norm_w, m_w_ple_gate, m_w_ple_proj, m_final_norm_w, v_norm_w, v_w_in, v_s5_A_re, v_s5_A_im, v_s5_log_dt, v_s5_B_re, v_s5_B_im, v_s5_C_re, v_s5_C_im, v_s5_D, v_s5_w_glu, v_s5_b_glu, v_conv_w, v_conv_b, v_dt_bias, v_A_log, v_ssd_D, v_ssd_norm_w, v_w_out, v_ple_norm_w, v_w_ple_gate, v_w_ple_proj, v_final_norm_w):
    given = (norm_w, w_in, s5_A_re, s5_A_im, s5_log_dt, s5_B_re, s5_B_im, s5_C_re, s5_C_im, s5_D, s5_w_glu, s5_b_glu, conv_w, conv_b,
             dt_bias, A_log, ssd_D, ssd_norm_w, w_out, ple_norm_w, w_ple_gate, w_ple_proj, final_norm_w)
    given_m = (m_norm_w, m_w_in, m_s5_A_re, m_s5_A_im, m_s5_log_dt, m_s5_B_re, m_s5_B_im, m_s5_C_re, m_s5_C_im, m_s5_D, m_s5_w_glu,
               m_s5_b_glu, m_conv_w, m_conv_b, m_dt_bias, m_A_log, m_ssd_D, m_ssd_norm_w, m_w_out, m_ple_norm_w, m_w_ple_gate,
               m_w_ple_proj, m_final_norm_w)
    given_v = (v_norm_w, v_w_in, v_s5_A_re, v_s5_A_im, v_s5_log_dt, v_s5_B_re, v_s5_B_im, v_s5_C_re, v_s5_C_im, v_s5_D, v_s5_w_glu,
               v_s5_b_glu, v_conv_w, v_conv_b, v_dt_bias, v_A_log, v_ssd_D, v_ssd_norm_w, v_w_out, v_ple_norm_w, v_w_ple_gate,
               v_w_ple_proj, v_final_norm_w)
    wts, mom, var = dict(zip(WEIGHTS, given)), dict(zip(WEIGHTS, given_m)), dict(zip(WEIGHTS, given_v))
    drop = lambda n, a: a if n == "final_norm_w" else a[0]

    wpack = jnp.concatenate([drop(n, wts[n]).astype(BF).reshape(-1, LANES) for n in BIG], axis=0)
    wall, cwall = _all_gather_chips(wpack, drop("conv_w", wts["conv_w"]))
    full, r0 = {}, 0
    for n, (shp, axis) in BIG.items():
        blk = wall[:, r0:r0 + BIG_ROWS[n]].reshape((N_CHIPS,) + shp)
        full[n] = blk.reshape(N_CHIPS * shp[0], shp[1]) if axis == 0 else blk.transpose(1, 0, 2).reshape(shp[0], N_CHIPS * shp[1])
        r0 += BIG_ROWS[n]
    for n in SMALL:
        full[n] = drop(n, wts[n])
    full["conv_w"] = cwall.transpose(1, 0, 2).reshape(4, 2048)

    loss, gx, grads = _local_step(x[0], p[0, 0], loss_target[0], full)
    loss = lax.psum(loss, MESH_AXES)

    gp = _pack_grads({n: grads[n].reshape(SMALL[n]) if n in SMALL else grads[n] for n in WEIGHTS})
    c = lax.axis_index("c")
    from_sibling = _exchange_pair(gp)
    mine = lax.dynamic_index_in_dim(gp, c, axis=1, keepdims=False)
    pair_sum = _add_rows([mine, from_sibling], "grad_pair_sum")
    by_chip = _exchange_chips(pair_sum)
    reduced_half = _add_rows([by_chip[k] for k in range(N_CHIPS)], "grad_chip_sum")
    gs, sm = _gather_reduced(reduced_half)
    g_final = {**_unpack_small(sm), **_unpack_shard(gs)}
    chip = 2 * lax.axis_index("x") + lax.axis_index("y")
    g_final["conv_w"] = lax.dynamic_slice_in_dim(g_final["conv_w"], chip * 512, 512, axis=1)

    outs_g, outs_d, outs_m, outs_v = [], [], [], []
    for n in WEIGHTS:
        g = g_final[n].reshape(wts[n].shape)
        d, mo, vo = _adamw(wts[n], g, mom[n], var[n], "adamw_" + n)
        outs_g.append(g)
        outs_d.append(d)
        outs_m.append(mo)
        outs_v.append(vo)
    return (loss, gx[None], *outs_g, *outs_d, *outs_m, *outs_v)
```

```python
import functools
import math

import jax
import jax.numpy as jnp
from jax import lax
from jax.experimental import pallas as pl
from jax.experimental.pallas import tpu as pltpu

F32 = jnp.float32
BF = jnp.bfloat16
EPS = 1e-6
CHUNK = 64
D_MODEL = 1024
S5_GROUPS = 64
S5_CH = 16
S5_STATE = 64
SSD_HEADS = 16
SSD_HEAD_DIM = 64
SSD_GROUPS = 4
SSD_STATE = 128
D_MAIN = 5120
LANES = 128
TOKEN_TILE = 256
VMEM_LIMIT = 56 * 1024 * 1024
MESH_AXES = ("x", "y", "c")
N_CHIPS = 4
ADAM_LR, ADAM_B1, ADAM_B2, ADAM_EPS, ADAM_WD, ADAM_STEP = 0.001, 0.9, 0.999, 1e-08, 0.01, 10
MESH = pl.DeviceIdType.MESH
ANY = pl.BlockSpec(memory_space=pl.ANY)


def _dot(a, b):
    return jnp.dot(a, b, preferred_element_type=F32)


def _dot_nt(a, b):
    return lax.dot_general(a, b, (((1,), (1,)), ((), ())), preferred_element_type=F32)


def _dot_tn(a, b):
    return lax.dot_general(a, b, (((0,), (0,)), ((), ())), preferred_element_type=F32)


def _sigmoid(x):
    return 1.0 / (1.0 + jnp.exp(-x))


def _softplus(x):
    return jnp.maximum(x, 0.0) + jnp.log(1.0 + jnp.exp(-jnp.abs(x)))


_GELU_C = math.sqrt(2.0 / math.pi)


def _gelu(x):
    return 0.5 * x * (1.0 + jnp.tanh(_GELU_C * (x + 0.044715 * x * x * x)))


def _gelu_grad(x):
    th = jnp.tanh(_GELU_C * (x + 0.044715 * x * x * x))
    return 0.5 * (1.0 + th) + 0.5 * x * (1.0 - th * th) * _GELU_C * (1.0 + 3.0 * 0.044715 * x * x)


def _params(sem=None):
    return pltpu.CompilerParams(dimension_semantics=sem, vmem_limit_bytes=VMEM_LIMIT)


def _row_spec(tl, width, col=0):
    return pl.BlockSpec((tl, width), lambda i, col=col: (i, col))


def _const_spec(shape):
    nd = len(shape)
    return pl.BlockSpec(shape, lambda *_: (0,) * nd)


def _in_proj_fwd(x, norm_w, w_main, w_dt):
    L = x.shape[0]
    tl = min(TOKEN_TILE, L)

    def body(x_ref, nw_ref, wm_ref, wd_ref, hn_ref, ps5_ref, pssd_ref, pd_ref):
        xv = x_ref[...]
        r = lax.rsqrt(jnp.mean(xv * xv, axis=-1, keepdims=True) + EPS)
        hn = (xv * r * nw_ref[...]).astype(BF)
        hn_ref[...] = hn
        for j in range(2):
            ps5_ref[:, j * 1024:(j + 1) * 1024] = _dot(hn, wm_ref[:, j * 1024:(j + 1) * 1024]).astype(BF)
        for j in range(3):
            pssd_ref[:, j * 1024:(j + 1) * 1024] = _dot(hn, wm_ref[:, (j + 2) * 1024:(j + 3) * 1024])
        pd_ref[...] = _dot(hn, wd_ref[...])

    return pl.pallas_call(
        body, name="in_proj_fwd", grid=(L // tl,),
        in_specs=[_row_spec(tl, D_MODEL), _const_spec((1, D_MODEL)), _const_spec((D_MODEL, D_MAIN)), _const_spec((D_MODEL, LANES))],
        out_specs=[_row_spec(tl, D_MODEL), _row_spec(tl, 2048), _row_spec(tl, 3072), _row_spec(tl, LANES)],
        out_shape=[jax.ShapeDtypeStruct((L, D_MODEL), BF), jax.ShapeDtypeStruct((L, 2048), BF), jax.ShapeDtypeStruct((L, 3072), F32),
                   jax.ShapeDtypeStruct((L, LANES), F32)],
        compiler_params=_params(("arbitrary",)),
    )(x, norm_w, w_main, w_dt)


def _in_proj_bwd(x, norm_w, dh1, du_flat, dyssm, s5_d, dzs, dzd, dxbc, ddt, w_main, w_dt):
    L = x.shape[0]
    tl = min(TOKEN_TILE, L)

    def body(x_ref, nw_ref, dh1_ref, duf_ref, dys_ref, d_ref, dzs_ref, dzd_ref, dxbc_ref, ddt_ref, wm_ref, wd_ref,
             gx_ref, du_ref, gnw_ref):
        @pl.when(pl.program_id(0) == 0)
        def _():
            gnw_ref[...] = jnp.zeros_like(gnw_ref)

        du = (duf_ref[...].astype(F32) + dys_ref[...].astype(F32) * d_ref[...]).astype(BF)
        du_ref[...] = du
        dhn = _dot_nt(du, wm_ref[:, 0:1024])
        dhn += _dot_nt(dzs_ref[...], wm_ref[:, 1024:2048])
        dhn += _dot_nt(dzd_ref[...], wm_ref[:, 2048:3072])
        dhn += _dot_nt(dxbc_ref[...], wm_ref[:, 3072:5120])
        dhn += _dot_nt(ddt_ref[...].astype(BF), wd_ref[...])
        xv = x_ref[...]
        r = lax.rsqrt(jnp.mean(xv * xv, axis=-1, keepdims=True) + EPS)
        xh = xv * r
        gnw_ref[...] += jnp.sum(dhn * xh, axis=0, keepdims=True)
        g = dhn * nw_ref[...]
        gx_ref[...] = dh1_ref[...] + r * (g - xh * jnp.mean(g * xh, axis=-1, keepdims=True))

    return pl.pallas_call(
        body, name="in_proj_bwd", grid=(L // tl,),
        in_specs=[_row_spec(tl, D_MODEL), _const_spec((1, D_MODEL)), _row_spec(tl, D_MODEL), _row_spec(tl, D_MODEL),
                  _row_spec(tl, D_MODEL), _const_spec((1, D_MODEL)), _row_spec(tl, D_MODEL), _row_spec(tl, D_MODEL),
                  _row_spec(tl, 2048), _row_spec(tl, LANES), _const_spec((D_MODEL, D_MAIN)), _const_spec((D_MODEL, LANES))],
        out_specs=[_row_spec(tl, D_MODEL), _row_spec(tl, D_MODEL), _const_spec((1, D_MODEL))],
        out_shape=[jax.ShapeDtypeStruct((L, D_MODEL), F32), jax.ShapeDtypeStruct((L, D_MODEL), BF), jax.ShapeDtypeStruct((1, D_MODEL), F32)],
        compiler_params=_params(("arbitrary",)),
    )(x, norm_w, dh1, du_flat, dyssm, s5_d, dzs, dzd, dxbc, ddt, w_main, w_dt)


def _matmul_tn(a, b, name):
    L, M = a.shape
    N = b.shape[1]
    tm, tn, tk = min(M, 1024), min(N, 1024), min(L, 512)

    def body(a_ref, b_ref, o_ref):
        @pl.when(pl.program_id(2) == 0)
        def _():
            o_ref[...] = jnp.zeros_like(o_ref)

        o_ref[...] += _dot_tn(a_ref[...].astype(BF), b_ref[...].astype(BF))

    return pl.pallas_call(
        body, name=name, grid=(M // tm, N // tn, L // tk),
        in_specs=[pl.BlockSpec((tk, tm), lambda i, j, k: (k, i)), pl.BlockSpec((tk, tn), lambda i, j, k: (k, j))],
        out_specs=pl.BlockSpec((tm, tn), lambda i, j, k: (i, j)),
        out_shape=jax.ShapeDtypeStruct((M, N), F32),
        compiler_params=_params(("parallel", "parallel", "arbitrary")),
    )(a, b)


def _s5_tables(a_re, a_im, log_dt, b_re, b_im, c_re, c_im):
    hi = lax.Precision.HIGHEST
    dt = jnp.exp(log_dt)[:, None]
    lam_re, lam_im = a_re * dt, a_im * dt
    tau = jnp.arange(CHUNK + 1, dtype=F32)
    mag = jnp.exp(lam_re[:, :, None] * tau)
    ang = lam_im[:, :, None] * tau
    pw = lax.complex(mag * jnp.cos(ang), mag * jnp.sin(ang))
    beta = (pw[:, :, 1] - 1.0) / lax.complex(a_re, a_im)
    bb = beta[:, :, None] * lax.complex(b_re, b_im)
    cc = lax.complex(c_re, c_im)
    cp = cc.transpose(0, 2, 1)[:, :, None, :] * pw[:, :, :CHUNK, None]
    cpf = cp.reshape(S5_GROUPS, S5_STATE, CHUNK * S5_CH)
    bbt = bb.transpose(0, 2, 1)
    kmat = (jnp.einsum("ghn,gnj->ghj", bbt.real, cpf.real, precision=hi)
            - jnp.einsum("ghn,gnj->ghj", bbt.imag, cpf.imag, precision=hi))
    w = bbt[:, :, None, :] * pw[:, :, CHUNK - 1::-1][:, :, :CHUNK].transpose(0, 2, 1)[:, None, :, :]
    wst = jnp.concatenate([w.real, w.imag], axis=-1).reshape(S5_GROUPS, S5_CH * CHUNK, 2 * S5_STATE)
    v = cc.transpose(0, 2, 1)[:, :, None, :] * pw[:, :, 1:CHUNK + 1, None]
    woff = jnp.concatenate([v.real, -v.imag], axis=1).reshape(S5_GROUPS, 2 * S5_STATE, CHUNK * S5_CH)
    a64 = jnp.concatenate([pw[:, :, CHUNK].real, pw[:, :, CHUNK].imag], axis=-1)
    return kmat, wst, woff, a64


def _s5_scan_powers(a_re, a_im, log_dt, nsteps):
    dt = jnp.exp(log_dt)[:, None]
    steps = (CHUNK * (2.0 ** jnp.arange(8, dtype=F32)))[None, :, None]
    mag = jnp.exp((a_re * dt)[:, None, :] * steps)
    ang = (a_im * dt)[:, None, :] * steps
    re, im = mag * jnp.cos(ang), mag * jnp.sin(ang)
    del nsteps
    return jnp.concatenate([re, re], -1), jnp.concatenate([-im, im], -1)


def _build_toeplitz(kmat_ref, tg_ref):
    lane = lax.broadcasted_iota(jnp.int32, (CHUNK, CHUNK * S5_CH), 1)
    srow = lax.broadcasted_iota(jnp.int32, (CHUNK, CHUNK * S5_CH), 0)
    keep = lane >= S5_CH * srow
    for h in range(S5_CH):
        row = jnp.broadcast_to(kmat_ref[0, h:h + 1, :], (CHUNK, CHUNK * S5_CH))
        rolled = pltpu.roll(row, 0, 1, stride=S5_CH, stride_axis=0)
        tg_ref[h * CHUNK:(h + 1) * CHUNK, :] = jnp.where(keep, rolled, 0.0).astype(BF)


def _swap_halves(x):
    return pltpu.roll(x, S5_STATE, 1)


def _s5_core_fwd(uflat, kmat, wst, woff, p1, p2):
    G, nc, W = uflat.shape
    nsteps = max(1, (nc - 1).bit_length())

    def body(u_ref, k_ref, wst_ref, woff_ref, p1_ref, p2_ref, y_ref, h_ref, tg_ref):
        _build_toeplitz(k_ref, tg_ref)
        u = u_ref[0]
        x = _dot(u, wst_ref[0])
        row = lax.broadcasted_iota(jnp.int32, x.shape, 0)
        d = 1
        for k in range(nsteps):
            sh = jnp.where(row >= d, pltpu.roll(x, d, 0), 0.0)
            x = x + p1_ref[0, k:k + 1, :] * sh + p2_ref[0, k:k + 1, :] * _swap_halves(sh)
            d *= 2
        h = jnp.where(row >= 1, pltpu.roll(x, 1, 0), 0.0)
        h_ref[0] = h
        y = _dot(u, tg_ref[...]) + _dot(h.astype(BF), woff_ref[0])
        y_ref[0] = y.astype(BF)

    return pl.pallas_call(
        body, name="s5_core_fwd", grid=(G,),
        in_specs=[pl.BlockSpec((1, nc, W), lambda g: (g, 0, 0)), pl.BlockSpec((1, S5_CH, W), lambda g: (g, 0, 0)),
                  pl.BlockSpec((1, W, 2 * S5_STATE), lambda g: (g, 0, 0)), pl.BlockSpec((1, 2 * S5_STATE, W), lambda g: (g, 0, 0)),
                  pl.BlockSpec((1, 8, 2 * S5_STATE), lambda g: (g, 0, 0)), pl.BlockSpec((1, 8, 2 * S5_STATE), lambda g: (g, 0, 0))],
        out_specs=[pl.BlockSpec((1, nc, W), lambda g: (g, 0, 0)), pl.BlockSpec((1, nc, 2 * S5_STATE), lambda g: (g, 0, 0))],
        out_shape=[jax.ShapeDtypeStruct((G, nc, W), BF), jax.ShapeDtypeStruct((G, nc, 2 * S5_STATE), F32)],
        scratch_shapes=[pltpu.VMEM((W, W), BF)],
        compiler_params=_params(("arbitrary",)),
    )(uflat, kmat, wst, woff, p1, p2)


def _s5_core_bwd(uflat, dyflat, hsave, kmat, wst, woff, p1, p2):
    G, nc, W = uflat.shape
    nsteps = max(1, (nc - 1).bit_length())

    def body(u_ref, dy_ref, h_ref, k_ref, wst_ref, woff_ref, p1_ref, p2_ref,
             du_ref, dk_ref, dwst_ref, dwoff_ref, da_ref, tg_ref):
        _build_toeplitz(k_ref, tg_ref)
        u = u_ref[0]
        dy = dy_ref[0]
        h = h_ref[0]
        gh = _dot_nt(dy, woff_ref[0])
        row = lax.broadcasted_iota(jnp.int32, gh.shape, 0)
        x = jnp.where(row < nc - 1, pltpu.roll(gh, nc - 1, 0), 0.0)
        d = 1
        for k in range(nsteps):
            sh = jnp.where(row < nc - d, pltpu.roll(x, nc - d, 0), 0.0)
            x = x + p1_ref[0, k:k + 1, :] * sh - p2_ref[0, k:k + 1, :] * _swap_halves(sh)
            d *= 2
        gs = x.astype(BF)
        du_ref[0] = (_dot_nt(dy, tg_ref[...]) + _dot_nt(gs, wst_ref[0])).astype(BF)
        dwst_ref[0] = _dot_tn(u, gs)
        dwoff_ref[0] = _dot_tn(h.astype(BF), dy)
        r1 = jnp.sum(x * h, axis=0, keepdims=True)
        r2 = jnp.sum(x * _swap_halves(h), axis=0, keepdims=True)
        da_ref[0] = jnp.concatenate([r1, r2, jnp.zeros((6, 2 * S5_STATE), F32)], axis=0)
        lane = lax.broadcasted_iota(jnp.int32, (CHUNK, W), 1)
        srow = lax.broadcasted_iota(jnp.int32, (CHUNK, W), 0)
        keep = lane < S5_CH * (srow + 1)
        flip = (lax.broadcasted_iota(jnp.int32, (CHUNK, CHUNK), 0) + lax.broadcasted_iota(jnp.int32, (CHUNK, CHUNK), 1)
                == CHUNK - 1).astype(BF)
        for hh in range(S5_CH):
            ur = _dot(u[:, hh * CHUNK:(hh + 1) * CHUNK], flip).astype(BF)
            dt_h = _dot_tn(ur, dy)
            back = pltpu.roll(dt_h, S5_CH, 1, stride=S5_CH, stride_axis=0)
            dk_ref[0, hh:hh + 1, :] = jnp.sum(jnp.where(keep, back, 0.0), axis=0, keepdims=True)

    spec_g = lambda a, b: pl.BlockSpec((1, a, b), lambda g: (g, 0, 0))
    return pl.pallas_call(
        body, name="s5_core_bwd", grid=(G,),
        in_specs=[spec_g(nc, W), spec_g(nc, W), spec_g(nc, 2 * S5_STATE), spec_g(S5_CH, W), spec_g(W, 2 * S5_STATE),
                  spec_g(2 * S5_STATE, W), spec_g(8, 2 * S5_STATE), spec_g(8, 2 * S5_STATE)],
        out_specs=[spec_g(nc, W), spec_g(S5_CH, W), spec_g(W, 2 * S5_STATE), spec_g(2 * S5_STATE, W), spec_g(8, 2 * S5_STATE)],
        out_shape=[jax.ShapeDtypeStruct((G, nc, W), BF), jax.ShapeDtypeStruct((G, S5_CH, W), F32),
                   jax.ShapeDtypeStruct((G, W, 2 * S5_STATE), F32), jax.ShapeDtypeStruct((G, 2 * S5_STATE, W), F32),
                   jax.ShapeDtypeStruct((G, 8, 2 * S5_STATE), F32)],
        scratch_shapes=[pltpu.VMEM((W, W), BF)],
        compiler_params=_params(("arbitrary",)),
    )(uflat, dyflat, hsave, kmat, wst, woff, p1, p2)


def _flat_hs(a, nc):
    return a.reshape(nc, CHUNK, S5_GROUPS, S5_CH).transpose(2, 0, 3, 1).reshape(S5_GROUPS, nc, CHUNK * S5_CH)


def _unflat_hs(a, nc):
    return a.reshape(S5_GROUPS, nc, S5_CH, CHUNK).transpose(1, 3, 0, 2).reshape(nc * CHUNK, D_MODEL)


def _flat_tk(a, nc):
    return a.reshape(nc, CHUNK, S5_GROUPS, S5_CH).transpose(2, 0, 1, 3).reshape(S5_GROUPS, nc, CHUNK * S5_CH)


def _unflat_tk(a, nc):
    return a.reshape(S5_GROUPS, nc, CHUNK, S5_CH).transpose(1, 2, 0, 3).reshape(nc * CHUNK, D_MODEL)


def _s5_post_fwd(yssm, proj, s5_d, w_glu, b_glu):
    L = yssm.shape[0]
    tl = min(TOKEN_TILE, L)

    def body(ys_ref, u_ref, z_ref, d_ref, wg_ref, bg_ref, o_ref):
        u = u_ref[...].astype(F32)
        a = _gelu(ys_ref[...].astype(F32) + d_ref[...] * u)
        y = a * _sigmoid(_dot(a.astype(BF), wg_ref[...]) + bg_ref[...])
        z = z_ref[...].astype(F32)
        o_ref[...] = (y * z * _sigmoid(z)).astype(BF)

    return pl.pallas_call(
        body, name="s5_post_fwd", grid=(L // tl,),
        in_specs=[_row_spec(tl, D_MODEL), _row_spec(tl, D_MODEL, 0), _row_spec(tl, D_MODEL, 1), _const_spec((1, D_MODEL)),
                  _const_spec((D_MODEL, D_MODEL)), _const_spec((1, D_MODEL))],
        out_specs=_row_spec(tl, D_MODEL),
        out_shape=jax.ShapeDtypeStruct((L, D_MODEL), BF),
        compiler_params=_params(("arbitrary",)),
    )(yssm, proj, proj, s5_d, w_glu, b_glu)


def _s5_post_bwd(dys5, yssm, proj, s5_d, w_glu, b_glu):
    L = yssm.shape[0]
    tl = min(TOKEN_TILE, L)

    def body(dy_ref, ys_ref, u_ref, z_ref, d_ref, wg_ref, bg_ref, dz_ref, dys_ref, a_ref, dgl_ref, dbg_ref, dd_ref):
        @pl.when(pl.program_id(0) == 0)
        def _():
            dbg_ref[...] = jnp.zeros_like(dbg_ref)
            dd_ref[...] = jnp.zeros_like(dd_ref)

        u = u_ref[...].astype(F32)
        y0 = ys_ref[...].astype(F32) + d_ref[...] * u
        a = _gelu(y0)
        a_bf = a.astype(BF)
        sg = _sigmoid(_dot(a_bf, wg_ref[...]) + bg_ref[...])
        y = a * sg
        z = z_ref[...].astype(F32)
        sz = _sigmoid(z)
        dout = dy_ref[...].astype(F32)
        dz_ref[...] = (dout * y * sz * (1.0 + z * (1.0 - sz))).astype(BF)
        dyv = dout * z * sz
        dgl = dyv * a * sg * (1.0 - sg)
        dgl_bf = dgl.astype(BF)
        da = dyv * sg + _dot_nt(dgl_bf, wg_ref[...])
        dy0 = da * _gelu_grad(y0)
        dbg_ref[...] += jnp.sum(dgl, axis=0, keepdims=True)
        dd_ref[...] += jnp.sum(dy0 * u, axis=0, keepdims=True)
        dys_ref[...] = dy0.astype(BF)
        a_ref[...] = a_bf
        dgl_ref[...] = dgl_bf

    big = jax.ShapeDtypeStruct((L, D_MODEL), BF)
    vec = jax.ShapeDtypeStruct((1, D_MODEL), F32)
    return pl.pallas_call(
        body, name="s5_post_bwd", grid=(L // tl,),
        in_specs=[_row_spec(tl, D_MODEL), _row_spec(tl, D_MODEL), _row_spec(tl, D_MODEL, 0), _row_spec(tl, D_MODEL, 1),
                  _const_spec((1, D_MODEL)), _const_spec((D_MODEL, D_MODEL)), _const_spec((1, D_MODEL))],
        out_specs=[_row_spec(tl, D_MODEL)] * 4 + [_const_spec((1, D_MODEL))] * 2,
        out_shape=[big, big, big, big, vec, vec],
        compiler_params=_params(("arbitrary",)),
    )(dys5, yssm, proj, proj, s5_d, w_glu, b_glu)


def _cumsum_rows(a):
    row = lax.broadcasted_iota(jnp.int32, a.shape, 0)
    d = 1
    while d < a.shape[0]:
        a = a + jnp.where(row >= d, pltpu.roll(a, d, 0), 0.0)
        d *= 2
    return a


def _rev_cumsum_rows(a):
    n = a.shape[0]
    row = lax.broadcasted_iota(jnp.int32, a.shape, 0)
    d = 1
    while d < n:
        a = a + jnp.where(row < n - d, pltpu.roll(a, n - d, 0), 0.0)
        d *= 2
    return a


def _ssd_conv_fwd(first, xs_ref, bc_ref, hx_ref, hb_ref, cw_ref, cb_ref, xp_ref, tl):
    hal = jnp.concatenate([hx_ref[...], hb_ref[...]], axis=1).astype(F32)
    xp_ref[0:8, :] = jnp.where(first, 0.0, hal)
    xp_ref[8:8 + tl, 0:1024] = xs_ref[...].astype(F32)
    xp_ref[8:8 + tl, 1024:2048] = bc_ref[...].astype(F32)
    pre = cb_ref[...] + cw_ref[0:1, :] * xp_ref[5:5 + tl, :]
    for k in range(1, 4):
        pre = pre + cw_ref[k:k + 1, :] * xp_ref[5 + k:5 + k + tl, :]
    return pre


def _onehot_lane(h):
    return (lax.broadcasted_iota(jnp.int32, (1, LANES), 1) == h).astype(F32)


def _dot_exact(x, e):
    hi = x.astype(BF)
    r = x - hi.astype(F32)
    mid = r.astype(BF)
    lo = (r - mid.astype(F32)).astype(BF)
    return _dot(hi, e) + _dot(mid, e) + _dot(lo, e)


def _head_expand_matrices():
    e = lax.broadcasted_iota(jnp.int32, (LANES, D_MODEL), 0) == (lax.broadcasted_iota(jnp.int32, (LANES, D_MODEL), 1) >> 6)
    et = (lax.broadcasted_iota(jnp.int32, (D_MODEL, LANES), 0) >> 6) == lax.broadcasted_iota(jnp.int32, (D_MODEL, LANES), 1)
    return e.astype(BF), et.astype(BF)


def _group_masks():
    r64 = lax.broadcasted_iota(jnp.int32, (4 * CHUNK, CHUNK), 0)
    causal4 = (r64 & (CHUNK - 1)) >= lax.broadcasted_iota(jnp.int32, (4 * CHUNK, CHUNK), 1)
    r256 = lax.broadcasted_iota(jnp.int32, (4 * CHUNK, 4 * SSD_HEAD_DIM), 0)
    same = (r256 >> 6) == (lax.broadcasted_iota(jnp.int32, (4 * CHUNK, 4 * SSD_HEAD_DIM), 1) >> 6)
    return causal4, same


def _group_decay(acs, acs_t, j, causal4):
    col = jnp.concatenate([acs[:, 4 * j + hh:4 * j + hh + 1] for hh in range(4)], axis=0)
    rowv = jnp.concatenate([jnp.broadcast_to(acs_t[4 * j + hh:4 * j + hh + 1, :], (CHUNK, CHUNK)) for hh in range(4)], axis=0)
    return jnp.where(causal4, jnp.exp(col - rowv), 0.0)


def _group_last_decay(acs_t, j):
    return jnp.concatenate([jnp.broadcast_to(jnp.exp(acs_t[4 * j + hh:4 * j + hh + 1, CHUNK - 1:CHUNK]), (SSD_HEAD_DIM, 1))
                            for hh in range(4)], axis=0)


def _fold_heads(r):
    return r[0:CHUNK] + r[CHUNK:2 * CHUNK] + r[2 * CHUNK:3 * CHUNK] + r[3 * CHUNK:4 * CHUNK]


def _ssd_specs_in(tl, nt, rev):
    t_of = (lambda i: nt - 1 - i) if rev else (lambda i: i)
    rows = lambda w, col: pl.BlockSpec((tl, w), lambda i: (t_of(i), col))
    halo = lambda col: pl.BlockSpec((8, 1024), lambda i: (jnp.maximum(t_of(i) * (tl // 8) - 1, 0), col))
    return t_of, rows, halo


def _ssd_fwd(proj, pdt, conv_w, conv_b, dt_bias, a_log, ssd_d, norm_w):
    L = proj.shape[0]
    tl = min(TOKEN_TILE, L)
    nt, ncl = L // tl, tl // CHUNK
    _, rows, halo = _ssd_specs_in(tl, nt, False)

    def body(xs_ref, bc_ref, hx_ref, hb_ref, dt_ref, z_ref, cw_ref, cb_ref, dtb_ref, al_ref, dd_ref, nw_ref,
             y_ref, ypre_ref, st_ref, xp_ref, xbc_ref, dts_ref, yp_ref, hst_ref):
        i = pl.program_id(0)

        @pl.when(i == 0)
        def _():
            hst_ref[...] = jnp.zeros_like(hst_ref)

        pre = _ssd_conv_fwd(i == 0, xs_ref, bc_ref, hx_ref, hb_ref, cw_ref, cb_ref, xp_ref, tl)
        xbc_ref[...] = pre * _sigmoid(pre)
        dts_ref[...] = _softplus(dt_ref[...] + dtb_ref[...])
        a_neg = -jnp.exp(al_ref[...])
        e16, _ = _head_expand_matrices()
        causal4, same = _group_masks()
        dd_x = _dot_exact(jnp.broadcast_to(dd_ref[...], (8, LANES)), e16)[0:1, :]

        def chunk(c, carry):
            r0 = pl.multiple_of(c * CHUNK, CHUNK)
            xbc = xbc_ref[pl.ds(r0, CHUNK), :]
            dtc = dts_ref[pl.ds(r0, CHUNK), :]
            acs = _cumsum_rows(dtc * a_neg)
            acs_t = acs.T
            acs_x = _dot_exact(acs, e16)
            xs = xbc[:, 0:1024]
            xd = xs * _dot_exact(dtc, e16)
            xd_bf = xd.astype(BF)
            xdd = (xd * jnp.exp(acs_x[CHUNK - 1:CHUNK, :] - acs_x)).astype(BF)
            e_x = jnp.exp(acs_x)
            for j in range(SSD_GROUPS):
                sl = slice(256 * j, 256 * (j + 1))
                bj = xbc[:, 1024 + 128 * j:1024 + 128 * (j + 1)].astype(BF)
                cj = xbc[:, 1536 + 128 * j:1536 + 128 * (j + 1)].astype(BF)
                g = _dot_nt(cj, bj)
                hj = hst_ref[sl, :]
                zj = _dot_nt(cj, hj.astype(BF))
                sc = (jnp.concatenate([g] * 4, axis=0) * _group_decay(acs, acs_t, j, causal4)).astype(BF)
                yd = _fold_heads(jnp.where(same, _dot(sc, xd_bf[:, sl]), 0.0))
                yp_ref[pl.ds(r0, CHUNK), sl] = yd + e_x[:, sl] * zj + dd_x[:, sl] * xs[:, sl]
                st_ref[c, sl, :] = hj
                hst_ref[sl, :] = _group_last_decay(acs_t, j) * hj + _dot_tn(xdd[:, sl], bj)
            return carry

        lax.fori_loop(0, ncl, chunk, 0)
        ypre = yp_ref[...]
        z = z_ref[...].astype(F32)
        gg = ypre * z * _sigmoid(z)
        for j in range(SSD_GROUPS):
            seg = gg[:, 256 * j:256 * (j + 1)]
            r = lax.rsqrt(jnp.mean(seg * seg, axis=-1, keepdims=True) + EPS)
            y_ref[:, 256 * j:256 * (j + 1)] = (seg * r * nw_ref[:, 256 * j:256 * (j + 1)]).astype(BF)
        ypre_ref[...] = ypre

    nc = L // CHUNK
    return pl.pallas_call(
        body, name="ssd_fwd", grid=(nt,),
        in_specs=[rows(1024, 1), rows(1024, 2), halo(1), halo(2), rows(LANES, 0), rows(1024, 0),
                  _const_spec((4, 2048)), _const_spec((1, 2048)), _const_spec((1, LANES)), _const_spec((1, LANES)),
                  _const_spec((1, LANES)), _const_spec((1, D_MODEL))],
        out_specs=[_row_spec(tl, D_MODEL), _row_spec(tl, D_MODEL), pl.BlockSpec((ncl, 1024, SSD_STATE), lambda i: (i, 0, 0))],
        out_shape=[jax.ShapeDtypeStruct((L, D_MODEL), BF), jax.ShapeDtypeStruct((L, D_MODEL), F32),
                   jax.ShapeDtypeStruct((nc, 1024, SSD_STATE), F32)],
        scratch_shapes=[pltpu.VMEM((tl + 8, 2048), F32), pltpu.VMEM((tl, 2048), F32), pltpu.VMEM((tl, LANES), F32),
                        pltpu.VMEM((tl, D_MODEL), F32), pltpu.VMEM((1024, SSD_STATE), F32)],
        compiler_params=_params(("arbitrary",)),
    )(proj, proj, proj, proj, pdt, proj, conv_w, conv_b, dt_bias, a_log, ssd_d, norm_w)


def _ssd_bwd(dyssd, ypre, proj, pdt, states, conv_w, conv_b, dt_bias, a_log, ssd_d, norm_w):
    L = proj.shape[0]
    tl = min(TOKEN_TILE, L)
    nt, ncl = L // tl, tl // CHUNK
    t_of, rows, halo = _ssd_specs_in(tl, nt, True)

    def body(dy_ref, ypre_ref, z_ref, xs_ref, bc_ref, hx_ref, hb_ref, dt_ref, st_ref, cw_ref, cb_ref, dtb_ref, al_ref,
             dd_ref, nw_ref,
             dxbc_ref, ddt_ref, dz_ref, dcw_ref, dcb_ref, ddtb_ref, dal_ref, ddd_ref, dnw_ref,
             xp_ref, xbc_ref, pre_ref, dts_ref, dyp_ref, dxs_ref, ddts_ref, dp_ref, dh_ref):
        i = pl.program_id(0)

        @pl.when(i == 0)
        def _():
            for r in (dcw_ref, dcb_ref, ddtb_ref, dal_ref, ddd_ref, dnw_ref, dh_ref):
                r[...] = jnp.zeros_like(r)
            dp_ref[tl:tl + 8, :] = jnp.zeros((8, 2048), F32)

        pre = _ssd_conv_fwd(t_of(i) == 0, xs_ref, bc_ref, hx_ref, hb_ref, cw_ref, cb_ref, xp_ref, tl)
        pre_ref[...] = pre
        xbc_ref[...] = pre * _sigmoid(pre)
        dts_ref[...] = _softplus(dt_ref[...] + dtb_ref[...])
        a_neg = -jnp.exp(al_ref[...])

        ypre = ypre_ref[...].astype(F32)
        z = z_ref[...].astype(F32)
        sz = _sigmoid(z)
        gg = ypre * z * sz
        dout = dy_ref[...].astype(F32)
        for j in range(SSD_GROUPS):
            sl = slice(256 * j, 256 * (j + 1))
            seg = gg[:, sl]
            r = lax.rsqrt(jnp.mean(seg * seg, axis=-1, keepdims=True) + EPS)
            gh = seg * r
            dnw_ref[:, sl] += jnp.sum(dout[:, sl] * gh, axis=0, keepdims=True)
            gw = dout[:, sl] * nw_ref[:, sl]
            dgg = r * (gw - gh * jnp.mean(gw * gh, axis=-1, keepdims=True))
            dyp_ref[:, sl] = dgg * z[:, sl] * sz[:, sl]
            dz_ref[:, sl] = (dgg * ypre[:, sl] * sz[:, sl] * (1.0 + z[:, sl] * (1.0 - sz[:, sl]))).astype(BF)

        e16, e16t = _head_expand_matrices()
        causal4, same = _group_masks()
        dd_x = _dot_exact(jnp.broadcast_to(dd_ref[...], (8, LANES)), e16)[0:1, :]
        last_row = (lax.broadcasted_iota(jnp.int32, (CHUNK, 1), 0) == CHUNK - 1).astype(F32)
        sel_rows = lax.broadcasted_iota(jnp.int32, (4 * CHUNK, LANES), 0) >> 6
        sel_lanes = lax.broadcasted_iota(jnp.int32, (4 * CHUNK, LANES), 1)

        def chunk(k, carry):
            dal_acc, ddx_acc = carry
            c = ncl - 1 - k
            r0 = pl.multiple_of(c * CHUNK, CHUNK)
            xbc = xbc_ref[pl.ds(r0, CHUNK), :]
            dtc = dts_ref[pl.ds(r0, CHUNK), :]
            dyp = dyp_ref[pl.ds(r0, CHUNK), :]
            acs = _cumsum_rows(dtc * a_neg)
            acs_t = acs.T
            acs_x = _dot_exact(acs, e16)
            dt_x = _dot_exact(dtc, e16)
            xs = xbc[:, 0:1024]
            xd = xs * dt_x
            xd_bf = xd.astype(BF)
            dec_x = jnp.exp(acs_x[CHUNK - 1:CHUNK, :] - acs_x)
            xdd = xd * dec_x
            xdd_bf = xdd.astype(BF)
            dz = dyp * jnp.exp(acs_x)
            dz_bf = dz.astype(BF)
            ddx_acc = ddx_acc + jnp.sum(dyp * xs, axis=0, keepdims=True)
            dacs = jnp.zeros((CHUNK, LANES), F32)
            hsum = jnp.zeros((1, LANES), F32)
            p1_l, p2_l, p3_l = [], [], []
            for j in range(SSD_GROUPS):
                sl = slice(256 * j, 256 * (j + 1))
                bj = xbc[:, 1024 + 128 * j:1024 + 128 * (j + 1)].astype(BF)
                cj = xbc[:, 1536 + 128 * j:1536 + 128 * (j + 1)].astype(BF)
                g = _dot_nt(cj, bj)
                hj = st_ref[c, sl, :]
                hj_bf = hj.astype(BF)
                dhj = dh_ref[sl, :]
                dhj_bf = dhj.astype(BF)
                zj = _dot_nt(cj, hj_bf)
                qj = _dot_nt(bj, dhj_bf)
                lm = _group_decay(acs, acs_t, j, causal4)
                sc = jnp.concatenate([g] * 4, axis=0) * lm
                sc_bf = sc.astype(BF)
                dym = jnp.where(same, jnp.concatenate([dyp[:, sl]] * 4, axis=0), 0.0).astype(BF)
                dsc = _dot_nt(dym, xd_bf[:, sl])
                dxd = _dot_tn(sc_bf, dym) + qj * dec_x[:, sl]
                m = dsc * sc
                dg_bf = _fold_heads(dsc * lm).astype(BF)
                rs = jnp.sum(m, axis=1, keepdims=True)
                e2 = dhj * hj
                for hh in range(4):
                    oh = _onehot_lane(4 * j + hh)
                    dacs = dacs + oh * rs[CHUNK * hh:CHUNK * (hh + 1)]
                    hsum = hsum + oh * jnp.sum(jnp.sum(e2[64 * hh:64 * (hh + 1)], axis=0, keepdims=True), axis=1, keepdims=True)
                sel = (sel_rows + 4 * j == sel_lanes).astype(BF)
                hi = m.astype(BF)
                rem = m - hi.astype(F32)
                mid = rem.astype(BF)
                lo = (rem - mid.astype(F32)).astype(BF)
                dacs = dacs - (_dot_tn(hi, sel) + _dot_tn(mid, sel) + _dot_tn(lo, sel))
                p1_l.append(dz[:, sl] * zj)
                p2_l.append(qj * xdd[:, sl])
                p3_l.append(dxd * xs[:, sl])
                dxs_ref[pl.ds(r0, CHUNK), sl] = dd_x[:, sl] * dyp[:, sl] + dxd * dt_x[:, sl]
                dxs_ref[pl.ds(r0, CHUNK), 1536 + 128 * j:1536 + 128 * (j + 1)] = _dot(dg_bf, bj) + _dot(dz_bf[:, sl], hj_bf)
                dxs_ref[pl.ds(r0, CHUNK), 1024 + 128 * j:1024 + 128 * (j + 1)] = _dot_tn(dg_bf, cj) + _dot(xdd_bf[:, sl], dhj_bf)
                dh_ref[sl, :] = _group_last_decay(acs_t, j) * dhj + _dot_tn(dz_bf[:, sl], cj)
            stacked = jnp.concatenate([jnp.concatenate(p1_l, axis=1), jnp.concatenate(p2_l, axis=1), jnp.concatenate(p3_l, axis=1)], axis=0)
            red = _dot_exact(stacked, e16t)
            r1, r2, ddtc = red[0:CHUNK], red[CHUNK:2 * CHUNK], red[2 * CHUNK:3 * CHUNK]
            tot = jnp.sum(r2, axis=0, keepdims=True) + jnp.exp(acs[CHUNK - 1:CHUNK, :]) * hsum
            da = _rev_cumsum_rows(dacs + r1 - r2 + last_row * tot)
            ddts_ref[pl.ds(r0, CHUNK), :] = ddtc + da * a_neg
            dal_acc = dal_acc + jnp.sum(da * dtc, axis=0, keepdims=True)
            return dal_acc, ddx_acc

        dal_acc, ddx_acc = lax.fori_loop(0, ncl, chunk, (jnp.zeros((1, LANES), F32), jnp.zeros((1, D_MODEL), F32)))
        dal_ref[...] += dal_acc * a_neg
        ddd_ref[...] += _dot_exact(jnp.broadcast_to(ddx_acc, (8, D_MODEL)), e16t)[0:1, :]
        ddt_raw = ddts_ref[...] * _sigmoid(dt_ref[...] + dtb_ref[...])
        ddt_ref[...] = ddt_raw
        ddtb_ref[...] += jnp.sum(ddt_raw, axis=0, keepdims=True)

        pre = pre_ref[...]
        sp = _sigmoid(pre)
        dpre = dxs_ref[...] * sp * (1.0 + pre * (1.0 - sp))
        dp_ref[0:tl, :] = dpre
        dcb_ref[...] += jnp.sum(dpre, axis=0, keepdims=True)
        dx = jnp.zeros((tl, 2048), F32)
        for k in range(4):
            dcw_ref[k:k + 1, :] += jnp.sum(dpre * xp_ref[5 + k:5 + k + tl, :], axis=0, keepdims=True)
            dx = dx + cw_ref[k:k + 1, :] * dp_ref[3 - k:3 - k + tl, :]
        dxbc_ref[...] = dx.astype(BF)
        dp_ref[tl:tl + 8, :] = dp_ref[0:8, :]

    vec = lambda w: jax.ShapeDtypeStruct((1, w), F32)
    rrow = lambda w: pl.BlockSpec((tl, w), lambda i: (t_of(i), 0))
    return pl.pallas_call(
        body, name="ssd_bwd", grid=(nt,),
        in_specs=[rrow(D_MODEL), rrow(D_MODEL), rows(1024, 0), rows(1024, 1), rows(1024, 2), halo(1), halo(2), rows(LANES, 0),
                  pl.BlockSpec((ncl, 1024, SSD_STATE), lambda i: (t_of(i), 0, 0)),
                  _const_spec((4, 2048)), _const_spec((1, 2048)), _const_spec((1, LANES)), _const_spec((1, LANES)),
                  _const_spec((1, LANES)), _const_spec((1, D_MODEL))],
        out_specs=[rrow(2048), rrow(LANES), rrow(D_MODEL), _const_spec((8, 2048)), _const_spec((1, 2048)),
                   _const_spec((1, LANES)), _const_spec((1, LANES)), _const_spec((1, LANES)), _const_spec((1, D_MODEL))],
        out_shape=[jax.ShapeDtypeStruct((L, 2048), BF), jax.ShapeDtypeStruct((L, LANES), F32), jax.ShapeDtypeStruct((L, D_MODEL), BF),
                   jax.ShapeDtypeStruct((8, 2048), F32), vec(2048), vec(LANES), vec(LANES), vec(LANES), vec(D_MODEL)],
        scratch_shapes=[pltpu.VMEM((tl + 8, 2048), F32), pltpu.VMEM((tl, 2048), F32), pltpu.VMEM((tl, 2048), F32),
                        pltpu.VMEM((tl, LANES), F32), pltpu.VMEM((tl, D_MODEL), F32), pltpu.VMEM((tl, 2048), F32),
                        pltpu.VMEM((tl, LANES), F32), pltpu.VMEM((tl + 8, 2048), F32), pltpu.VMEM((1024, SSD_STATE), F32)],
        compiler_params=_params(("arbitrary",)),
    )(dyssd, ypre, proj, proj, proj, proj, proj, pdt, states, conv_w, conv_b, dt_bias, a_log, ssd_d, norm_w)


def _head_fwd_bwd(x, ys5, yssd, p, target, w_out, w_gate, w_proj, ple_nw, fin_nw):
    L = x.shape[0]
    tl = min(TOKEN_TILE, L)
    inv_d = 1.0 / D_MODEL

    def body(x_ref, ys_ref, yd_ref, p_ref, t_ref, wo_ref, wg_ref, wp_ref, pnw_ref, fnw_ref,
             loss_ref, dys_ref, dyd_ref, dh1_ref, n2_ref, dgl_ref, dpp_ref, dpnw_ref, dfnw_ref):
        @pl.when(pl.program_id(0) == 0)
        def _():
            loss_ref[...] = jnp.zeros_like(loss_ref)
            dpnw_ref[...] = jnp.zeros_like(dpnw_ref)
            dfnw_ref[...] = jnp.zeros_like(dfnw_ref)

        h1 = x_ref[...] + _dot(ys_ref[...], wo_ref[0:1024, :]) + _dot(yd_ref[...], wo_ref[1024:2048, :])
        r1 = lax.rsqrt(jnp.mean(h1 * h1, axis=-1, keepdims=True) + EPS)
        hh1 = h1 * r1
        n2 = (hh1 * pnw_ref[...]).astype(BF)
        gate = _sigmoid(_dot(n2, wg_ref[...]))
        pp = _dot(p_ref[...].astype(BF), wp_ref[...])
        h2 = h1 + pp * gate
        r2 = lax.rsqrt(jnp.mean(h2 * h2, axis=-1, keepdims=True) + EPS)
        hh2 = h2 * r2
        err = hh2 * fnw_ref[...] - t_ref[...]
        loss_ref[...] += 0.5 * inv_d * jnp.sum(err * err)
        dyo = err * inv_d
        dfnw_ref[...] += jnp.sum(dyo * hh2, axis=0, keepdims=True)
        g2 = dyo * fnw_ref[...]
        dh2 = r2 * (g2 - hh2 * jnp.mean(g2 * hh2, axis=-1, keepdims=True))
        dpp_ref[...] = (dh2 * gate).astype(BF)
        dgl = (dh2 * pp * gate * (1.0 - gate)).astype(BF)
        dgl_ref[...] = dgl
        n2_ref[...] = n2
        dn2 = _dot_nt(dgl, wg_ref[...])
        dpnw_ref[...] += jnp.sum(dn2 * hh1, axis=0, keepdims=True)
        g1 = dn2 * pnw_ref[...]
        dh1 = dh2 + r1 * (g1 - hh1 * jnp.mean(g1 * hh1, axis=-1, keepdims=True))
        dh1_ref[...] = dh1
        dh1_bf = dh1.astype(BF)
        dys_ref[...] = _dot_nt(dh1_bf, wo_ref[0:1024, :]).astype(BF)
        dyd_ref[...] = _dot_nt(dh1_bf, wo_ref[1024:2048, :])

    big = jax.ShapeDtypeStruct((L, D_MODEL), BF)
    vec = jax.ShapeDtypeStruct((1, D_MODEL), F32)
    return pl.pallas_call(
        body, name="head_fwd_bwd", grid=(L // tl,),
        in_specs=[_row_spec(tl, D_MODEL), _row_spec(tl, D_MODEL), _row_spec(tl, D_MODEL), _row_spec(tl, 256), _row_spec(tl, D_MODEL),
                  _const_spec((2048, D_MODEL)), _const_spec((D_MODEL, D_MODEL)), _const_spec((256, D_MODEL)),
                  _const_spec((1, D_MODEL)), _const_spec((1, D_MODEL))],
        out_specs=[_const_spec((8, LANES)), _row_spec(tl, D_MODEL), _row_spec(tl, D_MODEL), _row_spec(tl, D_MODEL),
                   _row_spec(tl, D_MODEL), _row_spec(tl, D_MODEL), _row_spec(tl, D_MODEL), _const_spec((1, D_MODEL)), _const_spec((1, D_MODEL))],
        out_shape=[jax.ShapeDtypeStruct((8, LANES), F32), big, jax.ShapeDtypeStruct((L, D_MODEL), F32),
                   jax.ShapeDtypeStruct((L, D_MODEL), F32), big, big, big, vec, vec],
        compiler_params=_params(("arbitrary",)),
    )(x, ys5, yssd, p, target, w_out, w_gate, w_proj, ple_nw, fin_nw)


def _pad_lanes(v):
    return jnp.pad(v.reshape(1, -1), ((0, 0), (0, LANES - v.size)))


def _local_step(x, p, target, w):
    L = x.shape[0]
    nc = L // CHUNK
    nsteps = max(1, (nc - 1).bit_length())
    w_in = w["w_in"]
    w_main = w_in[:, :D_MAIN]
    w_dt = jnp.pad(w_in[:, D_MAIN:], ((0, 0), (0, LANES - SSD_HEADS)))
    norm_w = w["norm_w"].reshape(1, -1)
    s5_d = w["s5_D"].reshape(1, -1)
    b_glu = w["s5_b_glu"].reshape(1, -1)
    conv_b = w["conv_b"].reshape(1, -1)
    dtb, alog, ssd_d = _pad_lanes(w["dt_bias"]), _pad_lanes(w["A_log"]), _pad_lanes(w["ssd_D"])
    ssd_nw = w["ssd_norm_w"].reshape(1, -1)
    ple_nw = w["ple_norm_w"].reshape(1, -1)
    fin_nw = w["final_norm_w"].reshape(1, -1)

    s5_args = (w["s5_A_re"], w["s5_A_im"], w["s5_log_dt"], w["s5_B_re"], w["s5_B_im"], w["s5_C_re"], w["s5_C_im"])
    (kmat, wst, woff, _a64), tables_vjp = jax.vjp(_s5_tables, *s5_args)
    p1, p2 = _s5_scan_powers(w["s5_A_re"], w["s5_A_im"], w["s5_log_dt"], nsteps)
    wst_bf, woff_bf = wst.astype(BF), woff.astype(BF)

    hn, proj, pssd, pdt = _in_proj_fwd(x, norm_w, w_main, w_dt)
    uflat = _flat_hs(proj[:, :D_MODEL], nc)
    yflat, hsave = _s5_core_fwd(uflat, kmat, wst_bf, woff_bf, p1, p2)
    yssm = _unflat_tk(yflat, nc)
    ys5 = _s5_post_fwd(yssm, proj, s5_d, w["s5_w_glu"], b_glu)
    yssd, ypre, states = _ssd_fwd(pssd, pdt, w["conv_w"], conv_b, dtb, alog, ssd_d, ssd_nw)
    (loss8, dys5, dyssd, dh1, n2, dgl2, dpp, g_ple_nw, g_fin_nw) = _head_fwd_bwd(
        x, ys5, yssd, p, target, w["w_out"], w["w_ple_gate"], w["w_ple_proj"], ple_nw, fin_nw)

    (dxbc, ddt, dzd, g_cw, g_cb, g_dtb, g_alog, g_ssd_d, g_ssd_nw) = _ssd_bwd(
        dyssd, ypre, pssd, pdt, states, w["conv_w"], conv_b, dtb, alog, ssd_d, ssd_nw)
    dzs, dyssm, a_glu, dgl1, g_bglu, g_s5d = _s5_post_bwd(dys5, yssm, proj, s5_d, w["s5_w_glu"], b_glu)
    duflat, dkmat, dwst, dwoff, da8 = _s5_core_bwd(uflat, _flat_tk(dyssm, nc), hsave, kmat, wst_bf, woff_bf, p1, p2)
    da64 = jnp.concatenate([da8[:, 0, :S5_STATE] + da8[:, 0, S5_STATE:], da8[:, 1, S5_STATE:] - da8[:, 1, :S5_STATE]], axis=-1)
    g_s5 = tables_vjp((dkmat, dwst, dwoff, da64))
    gx, du, g_norm_w = _in_proj_bwd(x, norm_w, dh1, _unflat_hs(duflat, nc), dyssm, s5_d, dzs, dzd, dxbc, ddt, w_main, w_dt)

    g_w_in = jnp.concatenate([
        _matmul_tn(hn, du, "dw_in_u"), _matmul_tn(hn, dzs, "dw_in_zs"), _matmul_tn(hn, dzd, "dw_in_zd"),
        _matmul_tn(hn, dxbc, "dw_in_xbc"), _matmul_tn(hn, ddt, "dw_in_dt")[:, :SSD_HEADS]], axis=1)
    grads = {
        "norm_w": g_norm_w, "w_in": g_w_in,
        "s5_A_re": g_s5[0], "s5_A_im": g_s5[1], "s5_log_dt": g_s5[2], "s5_B_re": g_s5[3], "s5_B_im": g_s5[4],
        "s5_C_re": g_s5[5], "s5_C_im": g_s5[6], "s5_D": g_s5d, "s5_w_glu": _matmul_tn(a_glu, dgl1, "dw_glu"), "s5_b_glu": g_bglu,
        "conv_w": g_cw[:4], "conv_b": g_cb, "dt_bias": g_dtb[:, :SSD_HEADS], "A_log": g_alog[:, :SSD_HEADS],
        "ssd_D": g_ssd_d[:, :SSD_HEADS], "ssd_norm_w": g_ssd_nw,
        "w_out": jnp.concatenate([_matmul_tn(ys5, dh1, "dw_out_s5"), _matmul_tn(yssd, dh1, "dw_out_ssd")], axis=0),
        "ple_norm_w": g_ple_nw, "w_ple_gate": _matmul_tn(n2, dgl2, "dw_gate"), "w_ple_proj": _matmul_tn(p, dpp, "dw_proj"),
        "final_norm_w": g_fin_nw,
    }
    return loss8[0, 0], gx, grads


WEIGHTS = ("norm_w", "w_in", "s5_A_re", "s5_A_im", "s5_log_dt", "s5_B_re", "s5_B_im", "s5_C_re", "s5_C_im", "s5_D", "s5_w_glu",
           "s5_b_glu", "conv_w", "conv_b", "dt_bias", "A_log", "ssd_D", "ssd_norm_w", "w_out", "ple_norm_w", "w_ple_gate",
           "w_ple_proj", "final_norm_w")
BIG = {"w_in": ((1024, 1284), 1), "s5_w_glu": ((256, 1024), 0), "w_out": ((512, 1024), 0), "w_ple_gate": ((256, 1024), 0),
       "w_ple_proj": ((256, 256), 1)}
SMALL = {"norm_w": (1024,), "s5_A_re": (64, 64), "s5_A_im": (64, 64), "s5_log_dt": (64,), "s5_B_re": (64, 64, 16),
         "s5_B_im": (64, 64, 16), "s5_C_re": (64, 16, 64), "s5_C_im": (64, 16, 64), "s5_D": (1024,), "s5_b_glu": (1024,),
         "conv_w": (4, 2048), "conv_b": (2048,), "dt_bias": (16,), "A_log": (16,), "ssd_D": (16,), "ssd_norm_w": (1024,),
         "ple_norm_w": (1024,), "final_norm_w": (1024,)}
BIG_ROWS = {n: s[0] * s[1] // LANES for n, (s, _) in BIG.items()}
BIG_ROWS_TOTAL = sum(BIG_ROWS.values())
SMALL_TOTAL = sum(math.prod(s) for s in SMALL.values())
SMALL_PIECE_ROWS = -(-SMALL_TOTAL // (N_CHIPS * 16 * LANES)) * 16
HALF_ROWS = (BIG_ROWS_TOTAL + SMALL_PIECE_ROWS) // 2
SMALL_ROW0 = BIG_ROWS_TOTAL - HALF_ROWS


def _mesh_pos():
    return lax.axis_index("x"), lax.axis_index("y"), lax.axis_index("c")


def _other_chips(x, y):
    return [(1 - x, y), (x, 1 - y), (1 - x, 1 - y)]


def _comm_params():
    return pltpu.CompilerParams(has_side_effects=True)


def _all_gather_chips(wpack, cw):
    def body(w_ref, c_ref, wo_ref, co_ref, send_sems, recv_sems, loc_sems):
        x, y, c = _mesh_pos()
        me = 2 * x + y
        loc = [pltpu.make_async_copy(w_ref, wo_ref.at[me], loc_sems.at[0]),
               pltpu.make_async_copy(c_ref, co_ref.at[me], loc_sems.at[1])]
        for cp in loc:
            cp.start()
        cps = []
        for k, (px, py) in enumerate(_other_chips(x, y)):
            cps.append(pltpu.make_async_remote_copy(w_ref, wo_ref.at[me], send_sems.at[2 * k], recv_sems.at[2 * k],
                                                    device_id=(px, py, c), device_id_type=MESH))
            cps.append(pltpu.make_async_remote_copy(c_ref, co_ref.at[me], send_sems.at[2 * k + 1], recv_sems.at[2 * k + 1],
                                                    device_id=(px, py, c), device_id_type=MESH))
        for cp in cps:
            cp.start()
        for cp in cps + loc:
            cp.wait()

    return pl.pallas_call(
        body, name="all_gather_weights", in_specs=[ANY, ANY], out_specs=[ANY, ANY],
        out_shape=[jax.ShapeDtypeStruct((N_CHIPS,) + wpack.shape, wpack.dtype), jax.ShapeDtypeStruct((N_CHIPS,) + cw.shape, cw.dtype)],
        scratch_shapes=[pltpu.SemaphoreType.DMA((6,)), pltpu.SemaphoreType.DMA((6,)), pltpu.SemaphoreType.DMA((2,))],
        compiler_params=_comm_params(),
    )(wpack, cw)


def _exchange_pair(gp):
    def body(g_ref, r_ref, send_sems, recv_sems):
        x, y, c = _mesh_pos()
        cps = [pltpu.make_async_remote_copy(g_ref.at[s, 1 - c], r_ref.at[s], send_sems.at[s], recv_sems.at[s],
                                            device_id=(x, y, 1 - c), device_id_type=MESH) for s in range(N_CHIPS)]
        for cp in cps:
            cp.start()
        for cp in cps:
            cp.wait()

    return pl.pallas_call(
        body, name="grad_exchange_pair", in_specs=[ANY], out_specs=ANY,
        out_shape=jax.ShapeDtypeStruct((N_CHIPS,) + gp.shape[2:], gp.dtype),
        scratch_shapes=[pltpu.SemaphoreType.DMA((N_CHIPS,)), pltpu.SemaphoreType.DMA((N_CHIPS,))],
        compiler_params=_comm_params(),
    )(gp)


def _exchange_chips(ps):
    def body(p_ref, r_ref, send_sems, recv_sems, loc_sem):
        x, y, c = _mesh_pos()
        me = 2 * x + y
        loc = pltpu.make_async_copy(p_ref.at[me], r_ref.at[me], loc_sem)
        loc.start()
        cps = [pltpu.make_async_remote_copy(p_ref.at[2 * px + py], r_ref.at[me], send_sems.at[k], recv_sems.at[k],
                                            device_id=(px, py, c), device_id_type=MESH)
               for k, (px, py) in enumerate(_other_chips(x, y))]
        for cp in cps:
            cp.start()
        for cp in cps:
            cp.wait()
        loc.wait()

    return pl.pallas_call(
        body, name="grad_exchange_chips", in_specs=[ANY], out_specs=ANY,
        out_shape=jax.ShapeDtypeStruct(ps.shape, ps.dtype),
        scratch_shapes=[pltpu.SemaphoreType.DMA((3,)), pltpu.SemaphoreType.DMA((3,)), pltpu.SemaphoreType.DMA],
        compiler_params=_comm_params(),
    )(ps)


def _gather_reduced(gh):
    def body(g_ref, gs_ref, sm_ref, send_sems, recv_sems, sm_send, sm_recv, loc_sems):
        x, y, c = _mesh_pos()
        me = 2 * x + y
        small = g_ref.at[pl.ds(SMALL_ROW0, SMALL_PIECE_ROWS)]
        loc = pltpu.make_async_copy(g_ref, gs_ref.at[c], loc_sems.at[0])
        loc.start()
        half = pltpu.make_async_remote_copy(g_ref, gs_ref.at[c], send_sems.at[0], recv_sems.at[0],
                                            device_id=(x, y, 1 - c), device_id_type=MESH)
        half.start()
        others = _other_chips(x, y)
        dests = [((x, y, 0), 0)] + [((px, py, pc), 1 + k) for k, (px, py) in enumerate(others) for pc in (0, 1)]

        @pl.when(c == 1)
        def _():
            own = pltpu.make_async_copy(small, sm_ref.at[me], loc_sems.at[1])
            own.start()
            cps = [pltpu.make_async_remote_copy(small, sm_ref.at[me], sm_send.at[k], sm_recv.at[rs], device_id=d, device_id_type=MESH)
                   for k, (d, rs) in enumerate(dests)]
            for cp in cps:
                cp.start()
            for cp in cps:
                cp.wait_send()
            own.wait()

        def arrival(chip, sem):
            return pltpu.make_async_remote_copy(small, sm_ref.at[chip], sm_send.at[0], sm_recv.at[sem],
                                                device_id=(x, y, c), device_id_type=MESH)

        @pl.when(c == 0)
        def _():
            arrival(me, 0).wait_recv()

        for k, (px, py) in enumerate(others):
            arrival(2 * px + py, 1 + k).wait_recv()
        half.wait()
        loc.wait()

    return pl.pallas_call(
        body, name="grad_gather_reduced", in_specs=[ANY], out_specs=[ANY, ANY],
        out_shape=[jax.ShapeDtypeStruct((2,) + gh.shape, gh.dtype), jax.ShapeDtypeStruct((N_CHIPS, SMALL_PIECE_ROWS, LANES), gh.dtype)],
        scratch_shapes=[pltpu.SemaphoreType.DMA((1,)), pltpu.SemaphoreType.DMA((1,)), pltpu.SemaphoreType.DMA((7,)),
                        pltpu.SemaphoreType.DMA((4,)), pltpu.SemaphoreType.DMA((2,))],
        compiler_params=_comm_params(),
    )(gh)


def _add_rows(parts, name):
    shape = parts[0].shape
    flat = [a.reshape(-1, LANES) for a in parts]
    rows = flat[0].shape[0]
    nblk = next(k for k in (16, 13, 8, 4, 2, 1) if rows % (k * 8) == 0 or k == 1)
    rb = rows // nblk

    def body(*refs):
        acc = refs[0][...]
        for r in refs[1:-1]:
            acc = acc + r[...]
        refs[-1][...] = acc

    out = pl.pallas_call(
        body, name=name, grid=(nblk,), in_specs=[_row_spec(rb, LANES)] * len(flat), out_specs=_row_spec(rb, LANES),
        out_shape=jax.ShapeDtypeStruct((rows, LANES), parts[0].dtype), compiler_params=_params(("parallel",)),
    )(*flat)
    return out.reshape(shape)


def _pack_grads(grads):
    small = jnp.concatenate([grads[n].reshape(-1) for n in SMALL])
    small = jnp.pad(small, (0, N_CHIPS * SMALL_PIECE_ROWS * LANES - SMALL_TOTAL)).reshape(N_CHIPS, SMALL_PIECE_ROWS, LANES)
    pieces = []
    for s in range(N_CHIPS):
        rows = []
        for n, (shp, axis) in BIG.items():
            g = grads[n]
            blk = g[s * shp[0]:(s + 1) * shp[0], :] if axis == 0 else g[:, s * shp[1]:(s + 1) * shp[1]]
            rows.append(blk.reshape(-1, LANES))
        rows.append(small[s])
        pieces.append(jnp.concatenate(rows, axis=0).reshape(2, HALF_ROWS, LANES))
    return jnp.stack(pieces)


def _unpack_shard(gs):
    rows = gs.reshape(2 * HALF_ROWS, LANES)
    out, r0 = {}, 0
    for n, (shp, _) in BIG.items():
        out[n] = rows[r0:r0 + BIG_ROWS[n]].reshape(shp)
        r0 += BIG_ROWS[n]
    return out


def _unpack_small(sm):
    flat = sm.reshape(-1)
    out, o = {}, 0
    for n, shp in SMALL.items():
        k = math.prod(shp)
        out[n] = flat[o:o + k].reshape(shp)
        o += k
    return out


def _as_2d(a):
    n = a.size
    if a.ndim >= 2 and a.shape[-1] > 1024:
        return a.reshape(-1, a.shape[-1])
    if n % 1024 == 0:
        return a.reshape(n // 1024, 1024)
    return a.reshape(1, n)


def _adamw(w, g, m, v, name):
    shape = w.shape
    w2, g2, m2, v2 = (_as_2d(a) for a in (w, g, m, v))
    rows, cols = w2.shape
    rb = 256 if rows >= 512 else rows

    def body(w_ref, g_ref, m_ref, v_ref, d_ref, mo_ref, vo_ref):
        gv = g_ref[...]
        mn = ADAM_B1 * m_ref[...] + (1.0 - ADAM_B1) * gv
        vn = ADAM_B2 * v_ref[...] + (1.0 - ADAM_B2) * (gv * gv)
        m_hat = mn / (1.0 - ADAM_B1 ** ADAM_STEP)
        v_hat = vn / (1.0 - ADAM_B2 ** ADAM_STEP)
        d_ref[...] = -ADAM_LR * (m_hat / (jnp.sqrt(v_hat) + ADAM_EPS) + ADAM_WD * w_ref[...])
        mo_ref[...] = mn
        vo_ref[...] = vn

    spec = _row_spec(rb, cols)
    sds = jax.ShapeDtypeStruct((rows, cols), F32)
    d, mo, vo = pl.pallas_call(
        body, name=name, grid=(rows // rb,), in_specs=[spec] * 4, out_specs=[spec] * 3, out_shape=[sds] * 3,
        compiler_params=_params(("parallel",)),
    )(w2, g2, m2, v2)
    return d.reshape(shape), mo.reshape(shape), vo.reshape(shape)


def kernel(x, p, norm_w, w_in, s5_A_re, s5_A_im, s5_log_dt, s5_B_re, s5_B_im, s5_C_re, s5_C_im, s5_D, s5_w_glu, s5_b_glu, conv_w, conv_b, dt_bias, A_log, ssd_D, ssd_norm_w, w_out, ple_norm_w, w_ple_gate, w_ple_proj, final_norm_w, loss_target, m_norm_w, m_w_in, m_s5_A_re, m_s5_A_im, m_s5_log_dt, m_s5_B_re, m_s5_B_im, m_s5_C_re, m_s5_C_im, m_s5_D, m_s5_w_glu, m_s5_b_glu, m_conv_w, m_conv_b, m_dt_bias, m_A_log, m_ssd_D, m_ssd_norm_w, m_w_out, m_ple_norm_w, m_w_ple_gate, m_w_ple_proj, m_final_norm_w, v_norm_w, v_w_in, v_s5_A_re, v_s5_A_im, v_s5_log_dt, v_s5_B_re, v_s5_B_im, v_s5_C_re, v_s5_C_im, v_s5_D, v_s5_w_glu, v_s5_b_glu, v_conv_w, v_conv_b, v_dt_bias, v_A_log, v_ssd_D, v_ssd_norm_w, v_w_out, v_ple_norm_w, v_w_ple_gate, v_w_ple_proj, v_final_norm_w):
    given = (norm_w, w_in, s5_A_re, s5_A_im, s5_log_dt, s5_B_re, s5_B_im, s5_C_re, s5_C_im, s5_D, s5_w_glu, s5_b_glu, conv_w, conv_b,
             dt_bias, A_log, ssd_D, ssd_norm_w, w_out, ple_norm_w, w_ple_gate, w_ple_proj, final_norm_w)
    given_m = (m_norm_w, m_w_in, m_s5_A_re, m_s5_A_im, m_s5_log_dt, m_s5_B_re, m_s5_B_im, m_s5_C_re, m_s5_C_im, m_s5_D, m_s5_w_glu,
               m_s5_b_glu, m_conv_w, m_conv_b, m_dt_bias, m_A_log, m_ssd_D, m_ssd_norm_w, m_w_out, m_ple_norm_w, m_w_ple_gate,
               m_w_ple_proj, m_final_norm_w)
    given_v = (v_norm_w, v_w_in, v_s5_A_re, v_s5_A_im, v_s5_log_dt, v_s5_B_re, v_s5_B_im, v_s5_C_re, v_s5_C_im, v_s5_D, v_s5_w_glu,
               v_s5_b_glu, v_conv_w, v_conv_b, v_dt_bias, v_A_log, v_ssd_D, v_ssd_norm_w, v_w_out, v_ple_norm_w, v_w_ple_gate,
               v_w_ple_proj, v_final_norm_w)
    wts, mom, var = dict(zip(WEIGHTS, given)), dict(zip(WEIGHTS, given_m)), dict(zip(WEIGHTS, given_v))
    drop = lambda n, a: a if n == "final_norm_w" else a[0]

    wpack = jnp.concatenate([drop(n, wts[n]).astype(BF).reshape(-1, LANES) for n in BIG], axis=0)
    wall, cwall = _all_gather_chips(wpack, drop("conv_w", wts["conv_w"]))
    full, r0 = {}, 0
    for n, (shp, axis) in BIG.items():
        blk = wall[:, r0:r0 + BIG_ROWS[n]].reshape((N_CHIPS,) + shp)
        full[n] = blk.reshape(N_CHIPS * shp[0], shp[1]) if axis == 0 else blk.transpose(1, 0, 2).reshape(shp[0], N_CHIPS * shp[1])
        r0 += BIG_ROWS[n]
    for n in SMALL:
        full[n] = drop(n, wts[n])
    full["conv_w"] = cwall.transpose(1, 0, 2).reshape(4, 2048)

    loss, gx, grads = _local_step(x[0], p[0, 0], loss_target[0], full)
    loss = lax.psum(loss, MESH_AXES)

    gp = _pack_grads({n: grads[n].reshape(SMALL[n]) if n in SMALL else grads[n] for n in WEIGHTS})
    c = lax.axis_index("c")
    from_sibling = _exchange_pair(gp)
    mine = lax.dynamic_index_in_dim(gp, c, axis=1, keepdims=False)
    pair_sum = _add_rows([mine, from_sibling], "grad_pair_sum")
    by_chip = _exchange_chips(pair_sum)
    reduced_half = _add_rows([by_chip[k] for k in range(N_CHIPS)], "grad_chip_sum")
    gs, sm = _gather_reduced(reduced_half)
    g_final = {**_unpack_small(sm), **_unpack_shard(gs)}
    chip = 2 * lax.axis_index("x") + lax.axis_index("y")
    g_final["conv_w"] = lax.dynamic_slice_in_dim(g_final["conv_w"], chip * 512, 512, axis=1)

    outs_g, outs_d, outs_m, outs_v = [], [], [], []
    for n in WEIGHTS:
        g = g_final[n].reshape(wts[n].shape)
        d, mo, vo = _adamw(wts[n], g, mom[n], var[n], "adamw_" + n)
        outs_g.append(g)
        outs_d.append(d)
        outs_m.append(mo)
        outs_v.append(vo)
    return (loss, gx[None], *outs_g, *outs_d, *outs_m, *outs_v)
```

```python
import functools
import math

import jax
import jax.numpy as jnp
from jax import lax
from jax.experimental import pallas as pl
from jax.experimental.pallas import tpu as pltpu

F32 = jnp.float32
BF = jnp.bfloat16
EPS = 1e-6
CHUNK = 64
D_MODEL = 1024
S5_GROUPS = 64
S5_CH = 16
S5_STATE = 64
SSD_HEADS = 16
SSD_HEAD_DIM = 64
SSD_GROUPS = 4
SSD_STATE = 128
D_MAIN = 5120
LANES = 128
TOKEN_TILE = 256
VMEM_LIMIT = 56 * 1024 * 1024
MESH_AXES = ("x", "y", "c")
N_CHIPS = 4
ADAM_LR, ADAM_B1, ADAM_B2, ADAM_EPS, ADAM_WD, ADAM_STEP = 0.001, 0.9, 0.999, 1e-08, 0.01, 10
MESH = pl.DeviceIdType.MESH
ANY = pl.BlockSpec(memory_space=pl.ANY)


def _dot(a, b):
    return jnp.dot(a, b, preferred_element_type=F32)


def _dot_nt(a, b):
    return lax.dot_general(a, b, (((1,), (1,)), ((), ())), preferred_element_type=F32)


def _dot_tn(a, b):
    return lax.dot_general(a, b, (((0,), (0,)), ((), ())), preferred_element_type=F32)


def _sigmoid(x):
    return 1.0 / (1.0 + jnp.exp(-x))


def _softplus(x):
    return jnp.maximum(x, 0.0) + jnp.log(1.0 + jnp.exp(-jnp.abs(x)))


_GELU_C = math.sqrt(2.0 / math.pi)


def _gelu(x):
    return 0.5 * x * (1.0 + jnp.tanh(_GELU_C * (x + 0.044715 * x * x * x)))


def _gelu_grad(x):
    th = jnp.tanh(_GELU_C * (x + 0.044715 * x * x * x))
    return 0.5 * (1.0 + th) + 0.5 * x * (1.0 - th * th) * _GELU_C * (1.0 + 3.0 * 0.044715 * x * x)


def _params(sem=None):
    return pltpu.CompilerParams(dimension_semantics=sem, vmem_limit_bytes=VMEM_LIMIT)


def _row_spec(tl, width, col=0):
    return pl.BlockSpec((tl, width), lambda i, col=col: (i, col))


def _const_spec(shape):
    nd = len(shape)
    return pl.BlockSpec(shape, lambda *_: (0,) * nd)


def _in_proj_fwd(x, norm_w, w_main, w_dt):
    L = x.shape[0]
    tl = min(TOKEN_TILE, L)

    def body(x_ref, nw_ref, wm_ref, wd_ref, hn_ref, ps5_ref, pssd_ref, pd_ref):
        xv = x_ref[...]
        r = lax.rsqrt(jnp.mean(xv * xv, axis=-1, keepdims=True) + EPS)
        hn = (xv * r * nw_ref[...]).astype(BF)
        hn_ref[...] = hn
        for j in range(2):
            ps5_ref[:, j * 1024:(j + 1) * 1024] = _dot(hn, wm_ref[:, j * 1024:(j + 1) * 1024]).astype(BF)
        for j in range(3):
            pssd_ref[:, j * 1024:(j + 1) * 1024] = _dot(hn, wm_ref[:, (j + 2) * 1024:(j + 3) * 1024])
        pd_ref[...] = _dot(hn, wd_ref[...])

    return pl.pallas_call(
        body, name="in_proj_fwd", grid=(L // tl,),
        in_specs=[_row_spec(tl, D_MODEL), _const_spec((1, D_MODEL)), _const_spec((D_MODEL, D_MAIN)), _const_spec((D_MODEL, LANES))],
        out_specs=[_row_spec(tl, D_MODEL), _row_spec(tl, 2048), _row_spec(tl, 3072), _row_spec(tl, LANES)],
        out_shape=[jax.ShapeDtypeStruct((L, D_MODEL), BF), jax.ShapeDtypeStruct((L, 2048), BF), jax.ShapeDtypeStruct((L, 3072), F32),
                   jax.ShapeDtypeStruct((L, LANES), F32)],
        compiler_params=_params(("arbitrary",)),
    )(x, norm_w, w_main, w_dt)


def _in_proj_bwd(x, norm_w, dh1, du_flat, dyssm, s5_d, dzs, dzd, dxbc, ddt, w_main, w_dt):
    L = x.shape[0]
    tl = min(TOKEN_TILE, L)

    def body(x_ref, nw_ref, dh1_ref, duf_ref, dys_ref, d_ref, dzs_ref, dzd_ref, dxbc_ref, ddt_ref, wm_ref, wd_ref,
             gx_ref, du_ref, gnw_ref):
        @pl.when(pl.program_id(0) == 0)
        def _():
            gnw_ref[...] = jnp.zeros_like(gnw_ref)

        du = (duf_ref[...].astype(F32) + dys_ref[...].astype(F32) * d_ref[...]).astype(BF)
        du_ref[...] = du
        dhn = _dot_nt(du, wm_ref[:, 0:1024])
        dhn += _dot_nt(dzs_ref[...], wm_ref[:, 1024:2048])
        dhn += _dot_nt(dzd_ref[...], wm_ref[:, 2048:3072])
        dhn += _dot_nt(dxbc_ref[...], wm_ref[:, 3072:5120])
        dhn += _dot_nt(ddt_ref[...].astype(BF), wd_ref[...])
        xv = x_ref[...]
        r = lax.rsqrt(jnp.mean(xv * xv, axis=-1, keepdims=True) + EPS)
        xh = xv * r
        gnw_ref[...] += jnp.sum(dhn * xh, axis=0, keepdims=True)
        g = dhn * nw_ref[...]
        gx_ref[...] = dh1_ref[...] + r * (g - xh * jnp.mean(g * xh, axis=-1, keepdims=True))

    return pl.pallas_call(
        body, name="in_proj_bwd", grid=(L // tl,),
        in_specs=[_row_spec(tl, D_MODEL), _const_spec((1, D_MODEL)), _row_spec(tl, D_MODEL), _row_spec(tl, D_MODEL),
                  _row_spec(tl, D_MODEL), _const_spec((1, D_MODEL)), _row_spec(tl, D_MODEL), _row_spec(tl, D_MODEL),
                  _row_spec(tl, 2048), _row_spec(tl, LANES), _const_spec((D_MODEL, D_MAIN)), _const_spec((D_MODEL, LANES))],
        out_specs=[_row_spec(tl, D_MODEL), _row_spec(tl, D_MODEL), _const_spec((1, D_MODEL))],
        out_shape=[jax.ShapeDtypeStruct((L, D_MODEL), F32), jax.ShapeDtypeStruct((L, D_MODEL), BF), jax.ShapeDtypeStruct((1, D_MODEL), F32)],
        compiler_params=_params(("arbitrary",)),
    )(x, norm_w, dh1, du_flat, dyssm, s5_d, dzs, dzd, dxbc, ddt, w_main, w_dt)


def _matmul_tn(a, b, name):
    L, M = a.shape
    N = b.shape[1]
    tm, tn, tk = min(M, 1024), min(N, 1024), min(L, 512)

    def body(a_ref, b_ref, o_ref):
        @pl.when(pl.program_id(2) == 0)
        def _():
            o_ref[...] = jnp.zeros_like(o_ref)

        o_ref[...] += _dot_tn(a_ref[...].astype(BF), b_ref[...].astype(BF))

    return pl.pallas_call(
        body, name=name, grid=(M // tm, N // tn, L // tk),
        in_specs=[pl.BlockSpec((tk, tm), lambda i, j, k: (k, i)), pl.BlockSpec((tk, tn), lambda i, j, k: (k, j))],
        out_specs=pl.BlockSpec((tm, tn), lambda i, j, k: (i, j)),
        out_shape=jax.ShapeDtypeStruct((M, N), F32),
        compiler_params=_params(("parallel", "parallel", "arbitrary")),
    )(a, b)


def _s5_tables(a_re, a_im, log_dt, b_re, b_im, c_re, c_im):
    hi = lax.Precision.HIGHEST
    dt = jnp.exp(log_dt)[:, None]
    lam_re, lam_im = a_re * dt, a_im * dt
    tau = jnp.arange(CHUNK + 1, dtype=F32)
    mag = jnp.exp(lam_re[:, :, None] * tau)
    ang = lam_im[:, :, None] * tau
    pw = lax.complex(mag * jnp.cos(ang), mag * jnp.sin(ang))
    beta = (pw[:, :, 1] - 1.0) / lax.complex(a_re, a_im)
    bb = beta[:, :, None] * lax.complex(b_re, b_im)
    cc = lax.complex(c_re, c_im)
    cp = cc.transpose(0, 2, 1)[:, :, None, :] * pw[:, :, :CHUNK, None]
    cpf = cp.reshape(S5_GROUPS, S5_STATE, CHUNK * S5_CH)
    bbt = bb.transpose(0, 2, 1)
    kmat = (jnp.einsum("ghn,gnj->ghj", bbt.real, cpf.real, precision=hi)
            - jnp.einsum("ghn,gnj->ghj", bbt.imag, cpf.imag, precision=hi))
    w = bbt[:, :, None, :] * pw[:, :, CHUNK - 1::-1][:, :, :CHUNK].transpose(0, 2, 1)[:, None, :, :]
    wst = jnp.concatenate([w.real, w.imag], axis=-1).reshape(S5_GROUPS, S5_CH * CHUNK, 2 * S5_STATE)
    v = cc.transpose(0, 2, 1)[:, :, None, :] * pw[:, :, 1:CHUNK + 1, None]
    woff = jnp.concatenate([v.real, -v.imag], axis=1).reshape(S5_GROUPS, 2 * S5_STATE, CHUNK * S5_CH)
    a64 = jnp.concatenate([pw[:, :, CHUNK].real, pw[:, :, CHUNK].imag], axis=-1)
    return kmat, wst, woff, a64


def _s5_scan_powers(a_re, a_im, log_dt, nsteps):
    dt = jnp.exp(log_dt)[:, None]
    steps = (CHUNK * (2.0 ** jnp.arange(8, dtype=F32)))[None, :, None]
    mag = jnp.exp((a_re * dt)[:, None, :] * steps)
    ang = (a_im * dt)[:, None, :] * steps
    re, im = mag * jnp.cos(ang), mag * jnp.sin(ang)
    del nsteps
    return jnp.concatenate([re, re], -1), jnp.concatenate([-im, im], -1)


def _build_toeplitz(kmat_ref, tg_ref):
    lane = lax.broadcasted_iota(jnp.int32, (CHUNK, CHUNK * S5_CH), 1)
    srow = lax.broadcasted_iota(jnp.int32, (CHUNK, CHUNK * S5_CH), 0)
    keep = lane >= S5_CH * srow
    for h in range(S5_CH):
        row = jnp.broadcast_to(kmat_ref[0, h:h + 1, :], (CHUNK, CHUNK * S5_CH))
        rolled = pltpu.roll(row, 0, 1, stride=S5_CH, stride_axis=0)
        tg_ref[h * CHUNK:(h + 1) * CHUNK, :] = jnp.where(keep, rolled, 0.0).astype(BF)


def _swap_halves(x):
    return pltpu.roll(x, S5_STATE, 1)


def _s5_core_fwd(uflat, kmat, wst, woff, p1, p2):
    G, nc, W = uflat.shape
    nsteps = max(1, (nc - 1).bit_length())

    def body(u_ref, k_ref, wst_ref, woff_ref, p1_ref, p2_ref, y_ref, h_ref, tg_ref):
        _build_toeplitz(k_ref, tg_ref)
        u = u_ref[0]
        x = _dot(u, wst_ref[0])
        row = lax.broadcasted_iota(jnp.int32, x.shape, 0)
        d = 1
        for k in range(nsteps):
            sh = jnp.where(row >= d, pltpu.roll(x, d, 0), 0.0)
            x = x + p1_ref[0, k:k + 1, :] * sh + p2_ref[0, k:k + 1, :] * _swap_halves(sh)
            d *= 2
        h = jnp.where(row >= 1, pltpu.roll(x, 1, 0), 0.0)
        h_ref[0] = h
        y = _dot(u, tg_ref[...]) + _dot(h.astype(BF), woff_ref[0])
        y_ref[0] = y.astype(BF)

    return pl.pallas_call(
        body, name="s5_core_fwd", grid=(G,),
        in_specs=[pl.BlockSpec((1, nc, W), lambda g: (g, 0, 0)), pl.BlockSpec((1, S5_CH, W), lambda g: (g, 0, 0)),
                  pl.BlockSpec((1, W, 2 * S5_STATE), lambda g: (g, 0, 0)), pl.BlockSpec((1, 2 * S5_STATE, W), lambda g: (g, 0, 0)),
                  pl.BlockSpec((1, 8, 2 * S5_STATE), lambda g: (g, 0, 0)), pl.BlockSpec((1, 8, 2 * S5_STATE), lambda g: (g, 0, 0))],
        out_specs=[pl.BlockSpec((1, nc, W), lambda g: (g, 0, 0)), pl.BlockSpec((1, nc, 2 * S5_STATE), lambda g: (g, 0, 0))],
        out_shape=[jax.ShapeDtypeStruct((G, nc, W), BF), jax.ShapeDtypeStruct((G, nc, 2 * S5_STATE), F32)],
        scratch_shapes=[pltpu.VMEM((W, W), BF)],
        compiler_params=_params(("arbitrary",)),
    )(uflat, kmat, wst, woff, p1, p2)


def _s5_core_bwd(uflat, dyflat, hsave, kmat, wst, woff, p1, p2):
    G, nc, W = uflat.shape
    nsteps = max(1, (nc - 1).bit_length())

    def body(u_ref, dy_ref, h_ref, k_ref, wst_ref, woff_ref, p1_ref, p2_ref,
             du_ref, dk_ref, dwst_ref, dwoff_ref, da_ref, tg_ref, flip_ref):
        @pl.when(pl.program_id(0) == 0)
        def _():
            r = lax.broadcasted_iota(jnp.int32, (W, W), 0)
            c = lax.broadcasted_iota(jnp.int32, (W, W), 1)
            flip_ref[...] = (((r >> 6) == (c >> 6)) & ((r & (CHUNK - 1)) + (c & (CHUNK - 1)) == CHUNK - 1)).astype(BF)

        _build_toeplitz(k_ref, tg_ref)
        u = u_ref[0]
        dy = dy_ref[0]
        h = h_ref[0]
        gh = _dot_nt(dy, woff_ref[0])
        row = lax.broadcasted_iota(jnp.int32, gh.shape, 0)
        x = jnp.where(row < nc - 1, pltpu.roll(gh, nc - 1, 0), 0.0)
        d = 1
        for k in range(nsteps):
            sh = jnp.where(row < nc - d, pltpu.roll(x, nc - d, 0), 0.0)
            x = x + p1_ref[0, k:k + 1, :] * sh - p2_ref[0, k:k + 1, :] * _swap_halves(sh)
            d *= 2
        gs = x.astype(BF)
        du_ref[0] = (_dot_nt(dy, tg_ref[...]) + _dot_nt(gs, wst_ref[0])).astype(BF)
        dwst_ref[0] = _dot_tn(u, gs)
        dwoff_ref[0] = _dot_tn(h.astype(BF), dy)
        r1 = jnp.sum(x * h, axis=0, keepdims=True)
        r2 = jnp.sum(x * _swap_halves(h), axis=0, keepdims=True)
        da_ref[0] = jnp.concatenate([r1, r2, jnp.zeros((6, 2 * S5_STATE), F32)], axis=0)
        lane = lax.broadcasted_iota(jnp.int32, (CHUNK, W), 1)
        srow = lax.broadcasted_iota(jnp.int32, (CHUNK, W), 0)
        keep = lane < S5_CH * (srow + 1)
        ur = _dot(u, flip_ref[...]).astype(BF)
        for hh in range(S5_CH):
            dt_h = _dot_tn(ur[:, hh * CHUNK:(hh + 1) * CHUNK], dy)
            back = pltpu.roll(dt_h, S5_CH, 1, stride=S5_CH, stride_axis=0)
            dk_ref[0, hh:hh + 1, :] = jnp.sum(jnp.where(keep, back, 0.0), axis=0, keepdims=True)

    spec_g = lambda a, b: pl.BlockSpec((1, a, b), lambda g: (g, 0, 0))
    return pl.pallas_call(
        body, name="s5_core_bwd", grid=(G,),
        in_specs=[spec_g(nc, W), spec_g(nc, W), spec_g(nc, 2 * S5_STATE), spec_g(S5_CH, W), spec_g(W, 2 * S5_STATE),
                  spec_g(2 * S5_STATE, W), spec_g(8, 2 * S5_STATE), spec_g(8, 2 * S5_STATE)],
        out_specs=[spec_g(nc, W), spec_g(S5_CH, W), spec_g(W, 2 * S5_STATE), spec_g(2 * S5_STATE, W), spec_g(8, 2 * S5_STATE)],
        out_shape=[jax.ShapeDtypeStruct((G, nc, W), BF), jax.ShapeDtypeStruct((G, S5_CH, W), F32),
                   jax.ShapeDtypeStruct((G, W, 2 * S5_STATE), F32), jax.ShapeDtypeStruct((G, 2 * S5_STATE, W), F32),
                   jax.ShapeDtypeStruct((G, 8, 2 * S5_STATE), F32)],
        scratch_shapes=[pltpu.VMEM((W, W), BF), pltpu.VMEM((W, W), BF)],
        compiler_params=_params(("arbitrary",)),
    )(uflat, dyflat, hsave, kmat, wst, woff, p1, p2)


def _flat_hs(a, nc):
    return a.reshape(nc, CHUNK, S5_GROUPS, S5_CH).transpose(2, 0, 3, 1).reshape(S5_GROUPS, nc, CHUNK * S5_CH)


def _unflat_hs(a, nc):
    return a.reshape(S5_GROUPS, nc, S5_CH, CHUNK).transpose(1, 3, 0, 2).reshape(nc * CHUNK, D_MODEL)


def _flat_tk(a, nc):
    return a.reshape(nc, CHUNK, S5_GROUPS, S5_CH).transpose(2, 0, 1, 3).reshape(S5_GROUPS, nc, CHUNK * S5_CH)


def _unflat_tk(a, nc):
    return a.reshape(S5_GROUPS, nc, CHUNK, S5_CH).transpose(1, 2, 0, 3).reshape(nc * CHUNK, D_MODEL)


def _s5_post_fwd(yssm, proj, s5_d, w_glu, b_glu):
    L = yssm.shape[0]
    tl = min(TOKEN_TILE, L)

    def body(ys_ref, u_ref, z_ref, d_ref, wg_ref, bg_ref, o_ref):
        u = u_ref[...].astype(F32)
        a = _gelu(ys_ref[...].astype(F32) + d_ref[...] * u)
        y = a * _sigmoid(_dot(a.astype(BF), wg_ref[...]) + bg_ref[...])
        z = z_ref[...].astype(F32)
        o_ref[...] = (y * z * _sigmoid(z)).astype(BF)

    return pl.pallas_call(
        body, name="s5_post_fwd", grid=(L // tl,),
        in_specs=[_row_spec(tl, D_MODEL), _row_spec(tl, D_MODEL, 0), _row_spec(tl, D_MODEL, 1), _const_spec((1, D_MODEL)),
                  _const_spec((D_MODEL, D_MODEL)), _const_spec((1, D_MODEL))],
        out_specs=_row_spec(tl, D_MODEL),
        out_shape=jax.ShapeDtypeStruct((L, D_MODEL), BF),
        compiler_params=_params(("arbitrary",)),
    )(yssm, proj, proj, s5_d, w_glu, b_glu)


def _s5_post_bwd(dys5, yssm, proj, s5_d, w_glu, b_glu):
    L = yssm.shape[0]
    tl = min(TOKEN_TILE, L)

    def body(dy_ref, ys_ref, u_ref, z_ref, d_ref, wg_ref, bg_ref, dz_ref, dys_ref, a_ref, dgl_ref, dbg_ref, dd_ref):
        @pl.when(pl.program_id(0) == 0)
        def _():
            dbg_ref[...] = jnp.zeros_like(dbg_ref)
            dd_ref[...] = jnp.zeros_like(dd_ref)

        u = u_ref[...].astype(F32)
        y0 = ys_ref[...].astype(F32) + d_ref[...] * u
        a = _gelu(y0)
        a_bf = a.astype(BF)
        sg = _sigmoid(_dot(a_bf, wg_ref[...]) + bg_ref[...])
        y = a * sg
        z = z_ref[...].astype(F32)
        sz = _sigmoid(z)
        dout = dy_ref[...].astype(F32)
        dz_ref[...] = (dout * y * sz * (1.0 + z * (1.0 - sz))).astype(BF)
        dyv = dout * z * sz
        dgl = dyv * a * sg * (1.0 - sg)
        dgl_bf = dgl.astype(BF)
        da = dyv * sg + _dot_nt(dgl_bf, wg_ref[...])
        dy0 = da * _gelu_grad(y0)
        dbg_ref[...] += jnp.sum(dgl, axis=0, keepdims=True)
        dd_ref[...] += jnp.sum(dy0 * u, axis=0, keepdims=True)
        dys_ref[...] = dy0.astype(BF)
        a_ref[...] = a_bf
        dgl_ref[...] = dgl_bf

    big = jax.ShapeDtypeStruct((L, D_MODEL), BF)
    vec = jax.ShapeDtypeStruct((1, D_MODEL), F32)
    return pl.pallas_call(
        body, name="s5_post_bwd", grid=(L // tl,),
        in_specs=[_row_spec(tl, D_MODEL), _row_spec(tl, D_MODEL), _row_spec(tl, D_MODEL, 0), _row_spec(tl, D_MODEL, 1),
                  _const_spec((1, D_MODEL)), _const_spec((D_MODEL, D_MODEL)), _const_spec((1, D_MODEL))],
        out_specs=[_row_spec(tl, D_MODEL)] * 4 + [_const_spec((1, D_MODEL))] * 2,
        out_shape=[big, big, big, big, vec, vec],
        compiler_params=_params(("arbitrary",)),
    )(dys5, yssm, proj, proj, s5_d, w_glu, b_glu)


def _cumsum_rows(a):
    row = lax.broadcasted_iota(jnp.int32, a.shape, 0)
    d = 1
    while d < a.shape[0]:
        a = a + jnp.where(row >= d, pltpu.roll(a, d, 0), 0.0)
        d *= 2
    return a


def _rev_cumsum_rows(a):
    n = a.shape[0]
    row = lax.broadcasted_iota(jnp.int32, a.shape, 0)
    d = 1
    while d < n:
        a = a + jnp.where(row < n - d, pltpu.roll(a, n - d, 0), 0.0)
        d *= 2
    return a


def _ssd_conv_fwd(first, xs_ref, bc_ref, hx_ref, hb_ref, cw_ref, cb_ref, xp_ref, tl):
    hal = jnp.concatenate([hx_ref[...], hb_ref[...]], axis=1).astype(F32)
    xp_ref[0:8, :] = jnp.where(first, 0.0, hal)
    xp_ref[8:8 + tl, 0:1024] = xs_ref[...].astype(F32)
    xp_ref[8:8 + tl, 1024:2048] = bc_ref[...].astype(F32)
    pre = cb_ref[...] + cw_ref[0:1, :] * xp_ref[5:5 + tl, :]
    for k in range(1, 4):
        pre = pre + cw_ref[k:k + 1, :] * xp_ref[5 + k:5 + k + tl, :]
    return pre


def _onehot_lane(h):
    return (lax.broadcasted_iota(jnp.int32, (1, LANES), 1) == h).astype(F32)


def _dot_exact(x, e):
    hi = x.astype(BF)
    r = x - hi.astype(F32)
    mid = r.astype(BF)
    lo = (r - mid.astype(F32)).astype(BF)
    return _dot(hi, e) + _dot(mid, e) + _dot(lo, e)


def _head_expand_matrices():
    e = lax.broadcasted_iota(jnp.int32, (LANES, D_MODEL), 0) == (lax.broadcasted_iota(jnp.int32, (LANES, D_MODEL), 1) >> 6)
    et = (lax.broadcasted_iota(jnp.int32, (D_MODEL, LANES), 0) >> 6) == lax.broadcasted_iota(jnp.int32, (D_MODEL, LANES), 1)
    return e.astype(BF), et.astype(BF)


def _group_masks():
    r64 = lax.broadcasted_iota(jnp.int32, (4 * CHUNK, CHUNK), 0)
    causal4 = (r64 & (CHUNK - 1)) >= lax.broadcasted_iota(jnp.int32, (4 * CHUNK, CHUNK), 1)
    r256 = lax.broadcasted_iota(jnp.int32, (4 * CHUNK, 4 * SSD_HEAD_DIM), 0)
    same = (r256 >> 6) == (lax.broadcasted_iota(jnp.int32, (4 * CHUNK, 4 * SSD_HEAD_DIM), 1) >> 6)
    return causal4, same


def _group_decay(acs, acs_t, j, causal4):
    col = jnp.concatenate([acs[:, 4 * j + hh:4 * j + hh + 1] for hh in range(4)], axis=0)
    rowv = jnp.concatenate([jnp.broadcast_to(acs_t[4 * j + hh:4 * j + hh + 1, :], (CHUNK, CHUNK)) for hh in range(4)], axis=0)
    return jnp.where(causal4, jnp.exp(col - rowv), 0.0)


def _group_last_decay(acs_t, j):
    return jnp.concatenate([jnp.broadcast_to(jnp.exp(acs_t[4 * j + hh:4 * j + hh + 1, CHUNK - 1:CHUNK]), (SSD_HEAD_DIM, 1))
                            for hh in range(4)], axis=0)


def _fold_heads(r):
    return r[0:CHUNK] + r[CHUNK:2 * CHUNK] + r[2 * CHUNK:3 * CHUNK] + r[3 * CHUNK:4 * CHUNK]


def _ssd_specs_in(tl, nt, rev):
    t_of = (lambda i: nt - 1 - i) if rev else (lambda i: i)
    rows = lambda w, col: pl.BlockSpec((tl, w), lambda i: (t_of(i), col))
    halo = lambda col: pl.BlockSpec((8, 1024), lambda i: (jnp.maximum(t_of(i) * (tl // 8) - 1, 0), col))
    return t_of, rows, halo


def _ssd_fwd(proj, pdt, conv_w, conv_b, dt_bias, a_log, ssd_d, norm_w):
    L = proj.shape[0]
    tl = min(TOKEN_TILE, L)
    nt, ncl = L // tl, tl // CHUNK
    _, rows, halo = _ssd_specs_in(tl, nt, False)

    def body(xs_ref, bc_ref, hx_ref, hb_ref, dt_ref, z_ref, cw_ref, cb_ref, dtb_ref, al_ref, dd_ref, nw_ref,
             y_ref, ypre_ref, st_ref, xp_ref, xbc_ref, dts_ref, yp_ref, hst_ref):
        i = pl.program_id(0)

        @pl.when(i == 0)
        def _():
            hst_ref[...] = jnp.zeros_like(hst_ref)

        pre = _ssd_conv_fwd(i == 0, xs_ref, bc_ref, hx_ref, hb_ref, cw_ref, cb_ref, xp_ref, tl)
        xbc_ref[...] = pre * _sigmoid(pre)
        dts_ref[...] = _softplus(dt_ref[...] + dtb_ref[...])
        a_neg = -jnp.exp(al_ref[...])
        e16, _ = _head_expand_matrices()
        causal4, same = _group_masks()
        dd_x = _dot_exact(jnp.broadcast_to(dd_ref[...], (8, LANES)), e16)[0:1, :]

        def chunk(c, carry):
            r0 = pl.multiple_of(c * CHUNK, CHUNK)
            xbc = xbc_ref[pl.ds(r0, CHUNK), :]
            dtc = dts_ref[pl.ds(r0, CHUNK), :]
            acs = _cumsum_rows(dtc * a_neg)
            acs_t = acs.T
            acs_x = _dot_exact(acs, e16)
            xs = xbc[:, 0:1024]
            xd = xs * _dot_exact(dtc, e16)
            xd_bf = xd.astype(BF)
            xdd = (xd * jnp.exp(acs_x[CHUNK - 1:CHUNK, :] - acs_x)).astype(BF)
            e_x = jnp.exp(acs_x)
            for j in range(SSD_GROUPS):
                sl = slice(256 * j, 256 * (j + 1))
                bj = xbc[:, 1024 + 128 * j:1024 + 128 * (j + 1)].astype(BF)
                cj = xbc[:, 1536 + 128 * j:1536 + 128 * (j + 1)].astype(BF)
                g = _dot_nt(cj, bj)
                hj = hst_ref[sl, :]
                zj = _dot_nt(cj, hj.astype(BF))
                sc = (jnp.concatenate([g] * 4, axis=0) * _group_decay(acs, acs_t, j, causal4)).astype(BF)
                yd = _fold_heads(jnp.where(same, _dot(sc, xd_bf[:, sl]), 0.0))
                yp_ref[pl.ds(r0, CHUNK), sl] = yd + e_x[:, sl] * zj + dd_x[:, sl] * xs[:, sl]
                st_ref[c, sl, :] = hj
                hst_ref[sl, :] = _group_last_decay(acs_t, j) * hj + _dot_tn(xdd[:, sl], bj)
            return carry

        lax.fori_loop(0, ncl, chunk, 0)
        ypre = yp_ref[...]
        z = z_ref[...].astype(F32)
        gg = ypre * z * _sigmoid(z)
        for j in range(SSD_GROUPS):
            seg = gg[:, 256 * j:256 * (j + 1)]
            r = lax.rsqrt(jnp.mean(seg * seg, axis=-1, keepdims=True) + EPS)
            y_ref[:, 256 * j:256 * (j + 1)] = (seg * r * nw_ref[:, 256 * j:256 * (j + 1)]).astype(BF)
        ypre_ref[...] = ypre

    nc = L // CHUNK
    return pl.pallas_call(
        body, name="ssd_fwd", grid=(nt,),
        in_specs=[rows(1024, 1), rows(1024, 2), halo(1), halo(2), rows(LANES, 0), rows(1024, 0),
                  _const_spec((4, 2048)), _const_spec((1, 2048)), _const_spec((1, LANES)), _const_spec((1, LANES)),
                  _const_spec((1, LANES)), _const_spec((1, D_MODEL))],
        out_specs=[_row_spec(tl, D_MODEL), _row_spec(tl, D_MODEL), pl.BlockSpec((ncl, 1024, SSD_STATE), lambda i: (i, 0, 0))],
        out_shape=[jax.ShapeDtypeStruct((L, D_MODEL), BF), jax.ShapeDtypeStruct((L, D_MODEL), F32),
                   jax.ShapeDtypeStruct((nc, 1024, SSD_STATE), F32)],
        scratch_shapes=[pltpu.VMEM((tl + 8, 2048), F32), pltpu.VMEM((tl, 2048), F32), pltpu.VMEM((tl, LANES), F32),
                        pltpu.VMEM((tl, D_MODEL), F32), pltpu.VMEM((1024, SSD_STATE), F32)],
        compiler_params=_params(("arbitrary",)),
    )(proj, proj, proj, proj, pdt, proj, conv_w, conv_b, dt_bias, a_log, ssd_d, norm_w)


def _ssd_bwd(dyssd, ypre, proj, pdt, states, conv_w, conv_b, dt_bias, a_log, ssd_d, norm_w):
    L = proj.shape[0]
    tl = min(TOKEN_TILE, L)
    nt, ncl = L // tl, tl // CHUNK
    t_of, rows, halo = _ssd_specs_in(tl, nt, True)

    def body(dy_ref, ypre_ref, z_ref, xs_ref, bc_ref, hx_ref, hb_ref, dt_ref, st_ref, cw_ref, cb_ref, dtb_ref, al_ref,
             dd_ref, nw_ref,
             dxbc_ref, ddt_ref, dz_ref, dcw_ref, dcb_ref, ddtb_ref, dal_ref, ddd_ref, dnw_ref,
             xp_ref, xbc_ref, pre_ref, dts_ref, dyp_ref, dxs_ref, ddts_ref, dp_ref, dh_ref):
        i = pl.program_id(0)

        @pl.when(i == 0)
        def _():
            for r in (dcw_ref, dcb_ref, ddtb_ref, dal_ref, ddd_ref, dnw_ref, dh_ref):
                r[...] = jnp.zeros_like(r)
            dp_ref[tl:tl + 8, :] = jnp.zeros((8, 2048), F32)

        pre = _ssd_conv_fwd(t_of(i) == 0, xs_ref, bc_ref, hx_ref, hb_ref, cw_ref, cb_ref, xp_ref, tl)
        pre_ref[...] = pre
        xbc_ref[...] = pre * _sigmoid(pre)
        dts_ref[...] = _softplus(dt_ref[...] + dtb_ref[...])
        a_neg = -jnp.exp(al_ref[...])

        ypre = ypre_ref[...].astype(F32)
        z = z_ref[...].astype(F32)
        sz = _sigmoid(z)
        gg = ypre * z * sz
        dout = dy_ref[...].astype(F32)
        for j in range(SSD_GROUPS):
            sl = slice(256 * j, 256 * (j + 1))
            seg = gg[:, sl]
            r = lax.rsqrt(jnp.mean(seg * seg, axis=-1, keepdims=True) + EPS)
            gh = seg * r
            dnw_ref[:, sl] += jnp.sum(dout[:, sl] * gh, axis=0, keepdims=True)
            gw = dout[:, sl] * nw_ref[:, sl]
            dgg = r * (gw - gh * jnp.mean(gw * gh, axis=-1, keepdims=True))
            dyp_ref[:, sl] = dgg * z[:, sl] * sz[:, sl]
            dz_ref[:, sl] = (dgg * ypre[:, sl] * sz[:, sl] * (1.0 + z[:, sl] * (1.0 - sz[:, sl]))).astype(BF)

        e16, e16t = _head_expand_matrices()
        causal4, same = _group_masks()
        dd_x = _dot_exact(jnp.broadcast_to(dd_ref[...], (8, LANES)), e16)[0:1, :]
        last_row = (lax.broadcasted_iota(jnp.int32, (CHUNK, 1), 0) == CHUNK - 1).astype(F32)
        sel_rows = lax.broadcasted_iota(jnp.int32, (4 * CHUNK, LANES), 0) >> 6
        sel_lanes = lax.broadcasted_iota(jnp.int32, (4 * CHUNK, LANES), 1)

        def chunk(k, carry):
            dal_acc, ddx_acc = carry
            c = ncl - 1 - k
            r0 = pl.multiple_of(c * CHUNK, CHUNK)
            xbc = xbc_ref[pl.ds(r0, CHUNK), :]
            dtc = dts_ref[pl.ds(r0, CHUNK), :]
            dyp = dyp_ref[pl.ds(r0, CHUNK), :]
            acs = _cumsum_rows(dtc * a_neg)
            acs_t = acs.T
            acs_x = _dot_exact(acs, e16)
            dt_x = _dot_exact(dtc, e16)
            xs = xbc[:, 0:1024]
            xd = xs * dt_x
            xd_bf = xd.astype(BF)
            dec_x = jnp.exp(acs_x[CHUNK - 1:CHUNK, :] - acs_x)
            xdd = xd * dec_x
            xdd_bf = xdd.astype(BF)
            dz = dyp * jnp.exp(acs_x)
            dz_bf = dz.astype(BF)
            ddx_acc = ddx_acc + jnp.sum(dyp * xs, axis=0, keepdims=True)
            dacs = jnp.zeros((CHUNK, LANES), F32)
            hsum = jnp.zeros((1, LANES), F32)
            p1_l, p2_l, p3_l = [], [], []
            for j in range(SSD_GROUPS):
                sl = slice(256 * j, 256 * (j + 1))
                bj = xbc[:, 1024 + 128 * j:1024 + 128 * (j + 1)].astype(BF)
                cj = xbc[:, 1536 + 128 * j:1536 + 128 * (j + 1)].astype(BF)
                g = _dot_nt(cj, bj)
                hj = st_ref[c, sl, :]
                hj_bf = hj.astype(BF)
                dhj = dh_ref[sl, :]
                dhj_bf = dhj.astype(BF)
                zj = _dot_nt(cj, hj_bf)
                qj = _dot_nt(bj, dhj_bf)
                lm = _group_decay(acs, acs_t, j, causal4)
                sc = jnp.concatenate([g] * 4, axis=0) * lm
                sc_bf = sc.astype(BF)
                dym = jnp.where(same, jnp.concatenate([dyp[:, sl]] * 4, axis=0), 0.0).astype(BF)
                dsc = _dot_nt(dym, xd_bf[:, sl])
                dxd = _dot_tn(sc_bf, dym) + qj * dec_x[:, sl]
                m = dsc * sc
                dg_bf = _fold_heads(dsc * lm).astype(BF)
                rs = jnp.sum(m, axis=1, keepdims=True)
                e2 = dhj * hj
                for hh in range(4):
                    oh = _onehot_lane(4 * j + hh)
                    dacs = dacs + oh * rs[CHUNK * hh:CHUNK * (hh + 1)]
                    hsum = hsum + oh * jnp.sum(jnp.sum(e2[64 * hh:64 * (hh + 1)], axis=0, keepdims=True), axis=1, keepdims=True)
                sel = (sel_rows + 4 * j == sel_lanes).astype(BF)
                hi = m.astype(BF)
                rem = m - hi.astype(F32)
                mid = rem.astype(BF)
                lo = (rem - mid.astype(F32)).astype(BF)
                dacs = dacs - (_dot_tn(hi, sel) + _dot_tn(mid, sel) + _dot_tn(lo, sel))
                p1_l.append(dz[:, sl] * zj)
                p2_l.append(qj * xdd[:, sl])
                p3_l.append(dxd * xs[:, sl])
                dxs_ref[pl.ds(r0, CHUNK), sl] = dd_x[:, sl] * dyp[:, sl] + dxd * dt_x[:, sl]
                dxs_ref[pl.ds(r0, CHUNK), 1536 + 128 * j:1536 + 128 * (j + 1)] = _dot(dg_bf, bj) + _dot(dz_bf[:, sl], hj_bf)
                dxs_ref[pl.ds(r0, CHUNK), 1024 + 128 * j:1024 + 128 * (j + 1)] = _dot_tn(dg_bf, cj) + _dot(xdd_bf[:, sl], dhj_bf)
                dh_ref[sl, :] = _group_last_decay(acs_t, j) * dhj + _dot_tn(dz_bf[:, sl], cj)
            stacked = jnp.concatenate([jnp.concatenate(p1_l, axis=1), jnp.concatenate(p2_l, axis=1), jnp.concatenate(p3_l, axis=1)], axis=0)
            red = _dot_exact(stacked, e16t)
            r1, r2, ddtc = red[0:CHUNK], red[CHUNK:2 * CHUNK], red[2 * CHUNK:3 * CHUNK]
            tot = jnp.sum(r2, axis=0, keepdims=True) + jnp.exp(acs[CHUNK - 1:CHUNK, :]) * hsum
            da = _rev_cumsum_rows(dacs + r1 - r2 + last_row * tot)
            ddts_ref[pl.ds(r0, CHUNK), :] = ddtc + da * a_neg
            dal_acc = dal_acc + jnp.sum(da * dtc, axis=0, keepdims=True)
            return dal_acc, ddx_acc

        dal_acc, ddx_acc = lax.fori_loop(0, ncl, chunk, (jnp.zeros((1, LANES), F32), jnp.zeros((1, D_MODEL), F32)))
        dal_ref[...] += dal_acc * a_neg
        ddd_ref[...] += _dot_exact(jnp.broadcast_to(ddx_acc, (8, D_MODEL)), e16t)[0:1, :]
        ddt_raw = ddts_ref[...] * _sigmoid(dt_ref[...] + dtb_ref[...])
        ddt_ref[...] = ddt_raw
        ddtb_ref[...] += jnp.sum(ddt_raw, axis=0, keepdims=True)

        pre = pre_ref[...]
        sp = _sigmoid(pre)
        dpre = dxs_ref[...] * sp * (1.0 + pre * (1.0 - sp))
        dp_ref[0:tl, :] = dpre
        dcb_ref[...] += jnp.sum(dpre, axis=0, keepdims=True)
        dx = jnp.zeros((tl, 2048), F32)
        for k in range(4):
            dcw_ref[k:k + 1, :] += jnp.sum(dpre * xp_ref[5 + k:5 + k + tl, :], axis=0, keepdims=True)
            dx = dx + cw_ref[k:k + 1, :] * dp_ref[3 - k:3 - k + tl, :]
        dxbc_ref[...] = dx.astype(BF)
        dp_ref[tl:tl + 8, :] = dp_ref[0:8, :]

    vec = lambda w: jax.ShapeDtypeStruct((1, w), F32)
    rrow = lambda w: pl.BlockSpec((tl, w), lambda i: (t_of(i), 0))
    return pl.pallas_call(
        body, name="ssd_bwd", grid=(nt,),
        in_specs=[rrow(D_MODEL), rrow(D_MODEL), rows(1024, 0), rows(1024, 1), rows(1024, 2), halo(1), halo(2), rows(LANES, 0),
                  pl.BlockSpec((ncl, 1024, SSD_STATE), lambda i: (t_of(i), 0, 0)),
                  _const_spec((4, 2048)), _const_spec((1, 2048)), _const_spec((1, LANES)), _const_spec((1, LANES)),
                  _const_spec((1, LANES)), _const_spec((1, D_MODEL))],
        out_specs=[rrow(2048), rrow(LANES), rrow(D_MODEL), _const_spec((8, 2048)), _const_spec((1, 2048)),
                   _const_spec((1, LANES)), _const_spec((1, LANES)), _const_spec((1, LANES)), _const_spec((1, D_MODEL))],
        out_shape=[jax.ShapeDtypeStruct((L, 2048), BF), jax.ShapeDtypeStruct((L, LANES), F32), jax.ShapeDtypeStruct((L, D_MODEL), BF),
                   jax.ShapeDtypeStruct((8, 2048), F32), vec(2048), vec(LANES), vec(LANES), vec(LANES), vec(D_MODEL)],
        scratch_shapes=[pltpu.VMEM((tl + 8, 2048), F32), pltpu.VMEM((tl, 2048), F32), pltpu.VMEM((tl, 2048), F32),
                        pltpu.VMEM((tl, LANES), F32), pltpu.VMEM((tl, D_MODEL), F32), pltpu.VMEM((tl, 2048), F32),
                        pltpu.VMEM((tl, LANES), F32), pltpu.VMEM((tl + 8, 2048), F32), pltpu.VMEM((1024, SSD_STATE), F32)],
        compiler_params=_params(("arbitrary",)),
    )(dyssd, ypre, proj, proj, proj, proj, proj, pdt, states, conv_w, conv_b, dt_bias, a_log, ssd_d, norm_w)


def _head_fwd_bwd(x, ys5, yssd, p, target, w_out, w_gate, w_proj, ple_nw, fin_nw):
    L = x.shape[0]
    tl = min(TOKEN_TILE, L)
    inv_d = 1.0 / D_MODEL

    def body(x_ref, ys_ref, yd_ref, p_ref, t_ref, wo_ref, wg_ref, wp_ref, pnw_ref, fnw_ref,
             loss_ref, dys_ref, dyd_ref, dh1_ref, n2_ref, dgl_ref, dpp_ref, dpnw_ref, dfnw_ref):
        @pl.when(pl.program_id(0) == 0)
        def _():
            loss_ref[...] = jnp.zeros_like(loss_ref)
            dpnw_ref[...] = jnp.zeros_like(dpnw_ref)
            dfnw_ref[...] = jnp.zeros_like(dfnw_ref)

        h1 = x_ref[...] + _dot(ys_ref[...], wo_ref[0:1024, :]) + _dot(yd_ref[...], wo_ref[1024:2048, :])
        r1 = lax.rsqrt(jnp.mean(h1 * h1, axis=-1, keepdims=True) + EPS)
        hh1 = h1 * r1
        n2 = (hh1 * pnw_ref[...]).astype(BF)
        gate = _sigmoid(_dot(n2, wg_ref[...]))
        pp = _dot(p_ref[...].astype(BF), wp_ref[...])
        h2 = h1 + pp * gate
        r2 = lax.rsqrt(jnp.mean(h2 * h2, axis=-1, keepdims=True) + EPS)
        hh2 = h2 * r2
        err = hh2 * fnw_ref[...] - t_ref[...]
        loss_ref[...] += 0.5 * inv_d * jnp.sum(err * err)
        dyo = err * inv_d
        dfnw_ref[...] += jnp.sum(dyo * hh2, axis=0, keepdims=True)
        g2 = dyo * fnw_ref[...]
        dh2 = r2 * (g2 - hh2 * jnp.mean(g2 * hh2, axis=-1, keepdims=True))
        dpp_ref[...] = (dh2 * gate).astype(BF)
        dgl = (dh2 * pp * gate * (1.0 - gate)).astype(BF)
        dgl_ref[...] = dgl
        n2_ref[...] = n2
        dn2 = _dot_nt(dgl, wg_ref[...])
        dpnw_ref[...] += jnp.sum(dn2 * hh1, axis=0, keepdims=True)
        g1 = dn2 * pnw_ref[...]
        dh1 = dh2 + r1 * (g1 - hh1 * jnp.mean(g1 * hh1, axis=-1, keepdims=True))
        dh1_ref[...] = dh1
        dh1_bf = dh1.astype(BF)
        dys_ref[...] = _dot_nt(dh1_bf, wo_ref[0:1024, :]).astype(BF)
        dyd_ref[...] = _dot_nt(dh1_bf, wo_ref[1024:2048, :])

    big = jax.ShapeDtypeStruct((L, D_MODEL), BF)
    vec = jax.ShapeDtypeStruct((1, D_MODEL), F32)
    return pl.pallas_call(
        body, name="head_fwd_bwd", grid=(L // tl,),
        in_specs=[_row_spec(tl, D_MODEL), _row_spec(tl, D_MODEL), _row_spec(tl, D_MODEL), _row_spec(tl, 256), _row_spec(tl, D_MODEL),
                  _const_spec((2048, D_MODEL)), _const_spec((D_MODEL, D_MODEL)), _const_spec((256, D_MODEL)),
                  _const_spec((1, D_MODEL)), _const_spec((1, D_MODEL))],
        out_specs=[_const_spec((8, LANES)), _row_spec(tl, D_MODEL), _row_spec(tl, D_MODEL), _row_spec(tl, D_MODEL),
                   _row_spec(tl, D_MODEL), _row_spec(tl, D_MODEL), _row_spec(tl, D_MODEL), _const_spec((1, D_MODEL)), _const_spec((1, D_MODEL))],
        out_shape=[jax.ShapeDtypeStruct((8, LANES), F32), big, jax.ShapeDtypeStruct((L, D_MODEL), F32),
                   jax.ShapeDtypeStruct((L, D_MODEL), F32), big, big, big, vec, vec],
        compiler_params=_params(("arbitrary",)),
    )(x, ys5, yssd, p, target, w_out, w_gate, w_proj, ple_nw, fin_nw)


def _pad_lanes(v):
    return jnp.pad(v.reshape(1, -1), ((0, 0), (0, LANES - v.size)))


def _local_step(x, p, target, w):
    L = x.shape[0]
    nc = L // CHUNK
    nsteps = max(1, (nc - 1).bit_length())
    w_in = w["w_in"]
    w_main = w_in[:, :D_MAIN]
    w_dt = jnp.pad(w_in[:, D_MAIN:], ((0, 0), (0, LANES - SSD_HEADS)))
    norm_w = w["norm_w"].reshape(1, -1)
    s5_d = w["s5_D"].reshape(1, -1)
    b_glu = w["s5_b_glu"].reshape(1, -1)
    conv_b = w["conv_b"].reshape(1, -1)
    dtb, alog, ssd_d = _pad_lanes(w["dt_bias"]), _pad_lanes(w["A_log"]), _pad_lanes(w["ssd_D"])
    ssd_nw = w["ssd_norm_w"].reshape(1, -1)
    ple_nw = w["ple_norm_w"].reshape(1, -1)
    fin_nw = w["final_norm_w"].reshape(1, -1)

    s5_args = (w["s5_A_re"], w["s5_A_im"], w["s5_log_dt"], w["s5_B_re"], w["s5_B_im"], w["s5_C_re"], w["s5_C_im"])
    (kmat, wst, woff, _a64), tables_vjp = jax.vjp(_s5_tables, *s5_args)
    p1, p2 = _s5_scan_powers(w["s5_A_re"], w["s5_A_im"], w["s5_log_dt"], nsteps)
    wst_bf, woff_bf = wst.astype(BF), woff.astype(BF)

    hn, proj, pssd, pdt = _in_proj_fwd(x, norm_w, w_main, w_dt)
    uflat = _flat_hs(proj[:, :D_MODEL], nc)
    yflat, hsave = _s5_core_fwd(uflat, kmat, wst_bf, woff_bf, p1, p2)
    yssm = _unflat_tk(yflat, nc)
    ys5 = _s5_post_fwd(yssm, proj, s5_d, w["s5_w_glu"], b_glu)
    yssd, ypre, states = _ssd_fwd(pssd, pdt, w["conv_w"], conv_b, dtb, alog, ssd_d, ssd_nw)
    (loss8, dys5, dyssd, dh1, n2, dgl2, dpp, g_ple_nw, g_fin_nw) = _head_fwd_bwd(
        x, ys5, yssd, p, target, w["w_out"], w["w_ple_gate"], w["w_ple_proj"], ple_nw, fin_nw)

    (dxbc, ddt, dzd, g_cw, g_cb, g_dtb, g_alog, g_ssd_d, g_ssd_nw) = _ssd_bwd(
        dyssd, ypre, pssd, pdt, states, w["conv_w"], conv_b, dtb, alog, ssd_d, ssd_nw)
    dzs, dyssm, a_glu, dgl1, g_bglu, g_s5d = _s5_post_bwd(dys5, yssm, proj, s5_d, w["s5_w_glu"], b_glu)
    duflat, dkmat, dwst, dwoff, da8 = _s5_core_bwd(uflat, _flat_tk(dyssm, nc), hsave, kmat, wst_bf, woff_bf, p1, p2)
    da64 = jnp.concatenate([da8[:, 0, :S5_STATE] + da8[:, 0, S5_STATE:], da8[:, 1, S5_STATE:] - da8[:, 1, :S5_STATE]], axis=-1)
    g_s5 = tables_vjp((dkmat, dwst, dwoff, da64))
    gx, du, g_norm_w = _in_proj_bwd(x, norm_w, dh1, _unflat_hs(duflat, nc), dyssm, s5_d, dzs, dzd, dxbc, ddt, w_main, w_dt)

    g_w_in = jnp.concatenate([
        _matmul_tn(hn, du, "dw_in_u"), _matmul_tn(hn, dzs, "dw_in_zs"), _matmul_tn(hn, dzd, "dw_in_zd"),
        _matmul_tn(hn, dxbc, "dw_in_xbc"), _matmul_tn(hn, ddt, "dw_in_dt")[:, :SSD_HEADS]], axis=1)
    grads = {
        "norm_w": g_norm_w, "w_in": g_w_in,
        "s5_A_re": g_s5[0], "s5_A_im": g_s5[1], "s5_log_dt": g_s5[2], "s5_B_re": g_s5[3], "s5_B_im": g_s5[4],
        "s5_C_re": g_s5[5], "s5_C_im": g_s5[6], "s5_D": g_s5d, "s5_w_glu": _matmul_tn(a_glu, dgl1, "dw_glu"), "s5_b_glu": g_bglu,
        "conv_w": g_cw[:4], "conv_b": g_cb, "dt_bias": g_dtb[:, :SSD_HEADS], "A_log": g_alog[:, :SSD_HEADS],
        "ssd_D": g_ssd_d[:, :SSD_HEADS], "ssd_norm_w": g_ssd_nw,
        "w_out": jnp.concatenate([_matmul_tn(ys5, dh1, "dw_out_s5"), _matmul_tn(yssd, dh1, "dw_out_ssd")], axis=0),
        "ple_norm_w": g_ple_nw, "w_ple_gate": _matmul_tn(n2, dgl2, "dw_gate"), "w_ple_proj": _matmul_tn(p, dpp, "dw_proj"),
        "final_norm_w": g_fin_nw,
    }
    return loss8[0, 0], gx, grads


WEIGHTS = ("norm_w", "w_in", "s5_A_re", "s5_A_im", "s5_log_dt", "s5_B_re", "s5_B_im", "s5_C_re", "s5_C_im", "s5_D", "s5_w_glu",
           "s5_b_glu", "conv_w", "conv_b", "dt_bias", "A_log", "ssd_D", "ssd_norm_w", "w_out", "ple_norm_w", "w_ple_gate",
           "w_ple_proj", "final_norm_w")
BIG = {"w_in": ((1024, 1284), 1), "s5_w_glu": ((256, 1024), 0), "w_out": ((512, 1024), 0), "w_ple_gate": ((256, 1024), 0),
       "w_ple_proj": ((256, 256), 1)}
SMALL = {"norm_w": (1024,), "s5_A_re": (64, 64), "s5_A_im": (64, 64), "s5_log_dt": (64,), "s5_B_re": (64, 64, 16),
         "s5_B_im": (64, 64, 16), "s5_C_re": (64, 16, 64), "s5_C_im": (64, 16, 64), "s5_D": (1024,), "s5_b_glu": (1024,),
         "conv_w": (4, 2048), "conv_b": (2048,), "dt_bias": (16,), "A_log": (16,), "ssd_D": (16,), "ssd_norm_w": (1024,),
         "ple_norm_w": (1024,), "final_norm_w": (1024,)}
BIG_ROWS = {n: s[0] * s[1] // LANES for n, (s, _) in BIG.items()}
BIG_ROWS_TOTAL = sum(BIG_ROWS.values())
SMALL_TOTAL = sum(math.prod(s) for s in SMALL.values())
SMALL_PIECE_ROWS = -(-SMALL_TOTAL // (N_CHIPS * 16 * LANES)) * 16
HALF_ROWS = (BIG_ROWS_TOTAL + SMALL_PIECE_ROWS) // 2
SMALL_ROW0 = BIG_ROWS_TOTAL - HALF_ROWS


def _mesh_pos():
    return lax.axis_index("x"), lax.axis_index("y"), lax.axis_index("c")


def _other_chips(x, y):
    return [(1 - x, y), (x, 1 - y), (1 - x, 1 - y)]


def _comm_params():
    return pltpu.CompilerParams(has_side_effects=True)


def _all_gather_chips(wpack, cw):
    half = wpack.shape[0] // 2

    def body(w_ref, c_ref, wo_ref, co_ref, send_sems, recv_sems, fwd_send, fwd_recv, loc_sems):
        x, y, c = _mesh_pos()
        me = 2 * x + y
        sib = (x, y, 1 - c)
        mine = pl.ds(c * half, half)
        theirs = pl.ds((1 - c) * half, half)
        others = _other_chips(x, y)
        loc = [pltpu.make_async_copy(w_ref, wo_ref.at[me], loc_sems.at[0]),
               pltpu.make_async_copy(c_ref, co_ref.at[me], loc_sems.at[1])]
        for cp in loc:
            cp.start()

        def from_chip(k, chip, dev):
            return pltpu.make_async_remote_copy(w_ref.at[mine], wo_ref.at[chip, mine], send_sems.at[2 * k], recv_sems.at[2 * k],
                                                device_id=dev, device_id_type=MESH)

        def conv_from(k, chip, dev):
            return pltpu.make_async_remote_copy(c_ref, co_ref.at[chip], send_sems.at[2 * k + 1], recv_sems.at[2 * k + 1],
                                                device_id=dev, device_id_type=MESH)

        def passed(k, chip, rows):
            return pltpu.make_async_remote_copy(wo_ref.at[chip, rows], wo_ref.at[chip, rows], fwd_send.at[k], fwd_recv.at[k],
                                                device_id=sib, device_id_type=MESH)

        sends = []
        for k, (px, py) in enumerate(others):
            sends += [from_chip(k, me, (px, py, c)), conv_from(k, me, (px, py, c))]
        for cp in sends:
            cp.start()
        fwds = []
        for k, (px, py) in enumerate(others):
            chip = 2 * px + py
            from_chip(k, chip, (px, py, c)).wait_recv()
            fwds.append(passed(k, chip, mine))
            fwds[-1].start()
        for k, (px, py) in enumerate(others):
            chip = 2 * px + py
            passed(k, chip, theirs).wait_recv()
            conv_from(k, chip, (px, py, c)).wait_recv()
        for cp in sends + fwds:
            cp.wait_send()
        for cp in loc:
            cp.wait()

    return pl.pallas_call(
        body, name="all_gather_weights", in_specs=[ANY, ANY], out_specs=[ANY, ANY],
        out_shape=[jax.ShapeDtypeStruct((N_CHIPS,) + wpack.shape, wpack.dtype), jax.ShapeDtypeStruct((N_CHIPS,) + cw.shape, cw.dtype)],
        scratch_shapes=[pltpu.SemaphoreType.DMA((6,)), pltpu.SemaphoreType.DMA((6,)), pltpu.SemaphoreType.DMA((3,)),
                        pltpu.SemaphoreType.DMA((3,)), pltpu.SemaphoreType.DMA((2,))],
        compiler_params=_comm_params(),
    )(wpack, cw)


def _exchange_pair(gp):
    def body(g_ref, r_ref, send_sems, recv_sems):
        x, y, c = _mesh_pos()
        cps = [pltpu.make_async_remote_copy(g_ref.at[s, 1 - c], r_ref.at[s], send_sems.at[s], recv_sems.at[s],
                                            device_id=(x, y, 1 - c), device_id_type=MESH) for s in range(N_CHIPS)]
        for cp in cps:
            cp.start()
        for cp in cps:
            cp.wait()

    return pl.pallas_call(
        body, name="grad_exchange_pair", in_specs=[ANY], out_specs=ANY,
        out_shape=jax.ShapeDtypeStruct((N_CHIPS,) + gp.shape[2:], gp.dtype),
        scratch_shapes=[pltpu.SemaphoreType.DMA((N_CHIPS,)), pltpu.SemaphoreType.DMA((N_CHIPS,))],
        compiler_params=_comm_params(),
    )(gp)


def _pair_sum(mine, from_sibling):
    def body(a_ref, b_ref, bf_ref, tail_ref):
        s = a_ref[0] + b_ref[0]
        bf_ref[0] = s.astype(BF)
        tail_ref[0] = s[SMALL_ROW0:, :]

    piece = pl.BlockSpec((1, HALF_ROWS, LANES), lambda i: (i, 0, 0))
    return pl.pallas_call(
        body, name="grad_pair_sum", grid=(N_CHIPS,), in_specs=[piece, piece],
        out_specs=[piece, pl.BlockSpec((1, SMALL_PIECE_ROWS, LANES), lambda i: (i, 0, 0))],
        out_shape=[jax.ShapeDtypeStruct((N_CHIPS, HALF_ROWS, LANES), BF), jax.ShapeDtypeStruct((N_CHIPS, SMALL_PIECE_ROWS, LANES), F32)],
        compiler_params=_params(("parallel",)),
    )(mine, from_sibling)


def _exchange_chips(ps_bf, ps_tail):
    def body(p_ref, t_ref, r_ref, rt_ref, send_sems, recv_sems, loc_sems):
        x, y, c = _mesh_pos()
        me = 2 * x + y
        loc = [pltpu.make_async_copy(p_ref.at[me], r_ref.at[me], loc_sems.at[0]),
               pltpu.make_async_copy(t_ref.at[me], rt_ref.at[me], loc_sems.at[1])]
        for cp in loc:
            cp.start()
        cps = []
        for k, (px, py) in enumerate(_other_chips(x, y)):
            cps.append(pltpu.make_async_remote_copy(p_ref.at[2 * px + py], r_ref.at[me], send_sems.at[2 * k], recv_sems.at[2 * k],
                                                    device_id=(px, py, c), device_id_type=MESH))
            cps.append(pltpu.make_async_remote_copy(t_ref.at[2 * px + py], rt_ref.at[me], send_sems.at[2 * k + 1],
                                                    recv_sems.at[2 * k + 1], device_id=(px, py, c), device_id_type=MESH))
        for cp in cps:
            cp.start()
        for cp in cps + loc:
            cp.wait()

    return pl.pallas_call(
        body, name="grad_exchange_chips", in_specs=[ANY, ANY], out_specs=[ANY, ANY],
        out_shape=[jax.ShapeDtypeStruct(ps_bf.shape, ps_bf.dtype), jax.ShapeDtypeStruct(ps_tail.shape, ps_tail.dtype)],
        scratch_shapes=[pltpu.SemaphoreType.DMA((6,)), pltpu.SemaphoreType.DMA((6,)), pltpu.SemaphoreType.DMA((2,))],
        compiler_params=_comm_params(),
    )(ps_bf, ps_tail)


def _chip_sum(by_chip_bf, by_chip_tail):
    def body(b_ref, t_ref, o_ref):
        acc = b_ref[0, 0:SMALL_ROW0, :].astype(F32)
        tail = t_ref[0]
        for k in range(1, N_CHIPS):
            acc = acc + b_ref[k, 0:SMALL_ROW0, :].astype(F32)
            tail = tail + t_ref[k]
        o_ref[0:SMALL_ROW0, :] = acc
        o_ref[SMALL_ROW0:, :] = tail

    return pl.pallas_call(
        body, name="grad_chip_sum", out_shape=jax.ShapeDtypeStruct((HALF_ROWS, LANES), F32),
        compiler_params=_params(),
    )(by_chip_bf, by_chip_tail)


def _gather_reduced(gh):
    def body(g_ref, gs_ref, sm_ref, send_sems, recv_sems, sm_send, sm_recv, loc_sems):
        x, y, c = _mesh_pos()
        me = 2 * x + y
        small = g_ref.at[pl.ds(SMALL_ROW0, SMALL_PIECE_ROWS)]
        loc = pltpu.make_async_copy(g_ref, gs_ref.at[c], loc_sems.at[0])
        loc.start()
        half = pltpu.make_async_remote_copy(g_ref, gs_ref.at[c], send_sems.at[0], recv_sems.at[0],
                                            device_id=(x, y, 1 - c), device_id_type=MESH)
        half.start()
        others = _other_chips(x, y)
        dests = [((x, y, 0), 0)] + [((px, py, pc), 1 + k) for k, (px, py) in enumerate(others) for pc in (0, 1)]

        @pl.when(c == 1)
        def _():
            own = pltpu.make_async_copy(small, sm_ref.at[me], loc_sems.at[1])
            own.start()
            cps = [pltpu.make_async_remote_copy(small, sm_ref.at[me], sm_send.at[k], sm_recv.at[rs], device_id=d, device_id_type=MESH)
                   for k, (d, rs) in enumerate(dests)]
            for cp in cps:
                cp.start()
            for cp in cps:
                cp.wait_send()
            own.wait()

        def arrival(chip, sem):
            return pltpu.make_async_remote_copy(small, sm_ref.at[chip], sm_send.at[0], sm_recv.at[sem],
                                                device_id=(x, y, c), device_id_type=MESH)

        @pl.when(c == 0)
        def _():
            arrival(me, 0).wait_recv()

        for k, (px, py) in enumerate(others):
            arrival(2 * px + py, 1 + k).wait_recv()
        half.wait()
        loc.wait()

    return pl.pallas_call(
        body, name="grad_gather_reduced", in_specs=[ANY], out_specs=[ANY, ANY],
        out_shape=[jax.ShapeDtypeStruct((2,) + gh.shape, gh.dtype), jax.ShapeDtypeStruct((N_CHIPS, SMALL_PIECE_ROWS, LANES), gh.dtype)],
        scratch_shapes=[pltpu.SemaphoreType.DMA((1,)), pltpu.SemaphoreType.DMA((1,)), pltpu.SemaphoreType.DMA((7,)),
                        pltpu.SemaphoreType.DMA((4,)), pltpu.SemaphoreType.DMA((2,))],
        compiler_params=_comm_params(),
    )(gh)


def _pack_grads(grads):
    small = jnp.concatenate([grads[n].reshape(-1) for n in SMALL])
    small = jnp.pad(small, (0, N_CHIPS * SMALL_PIECE_ROWS * LANES - SMALL_TOTAL)).reshape(N_CHIPS, SMALL_PIECE_ROWS, LANES)
    pieces = []
    for s in range(N_CHIPS):
        rows = []
        for n, (shp, axis) in BIG.items():
            g = grads[n]
            blk = g[s * shp[0]:(s + 1) * shp[0], :] if axis == 0 else g[:, s * shp[1]:(s + 1) * shp[1]]
            rows.append(blk.reshape(-1, LANES))
        rows.append(small[s])
        pieces.append(jnp.concatenate(rows, axis=0).reshape(2, HALF_ROWS, LANES))
    return jnp.stack(pieces)


def _unpack_shard(gs):
    rows = gs.reshape(2 * HALF_ROWS, LANES)
    out, r0 = {}, 0
    for n, (shp, _) in BIG.items():
        out[n] = rows[r0:r0 + BIG_ROWS[n]].reshape(shp)
        r0 += BIG_ROWS[n]
    return out


def _unpack_small(sm):
    flat = sm.reshape(-1)
    out, o = {}, 0
    for n, shp in SMALL.items():
        k = math.prod(shp)
        out[n] = flat[o:o + k].reshape(shp)
        o += k
    return out


def _as_2d(a):
    n = a.size
    if a.ndim >= 2 and a.shape[-1] > 1024:
        return a.reshape(-1, a.shape[-1])
    if n % 1024 == 0:
        return a.reshape(n // 1024, 1024)
    return a.reshape(1, n)


def _adamw(w, g, m, v, name):
    shape = w.shape
    w2, g2, m2, v2 = (_as_2d(a) for a in (w, g, m, v))
    rows, cols = w2.shape
    rb = 256 if rows >= 512 else rows

    def body(w_ref, g_ref, m_ref, v_ref, d_ref, mo_ref, vo_ref):
        gv = g_ref[...]
        mn = ADAM_B1 * m_ref[...] + (1.0 - ADAM_B1) * gv
        vn = ADAM_B2 * v_ref[...] + (1.0 - ADAM_B2) * (gv * gv)
        m_hat = mn / (1.0 - ADAM_B1 ** ADAM_STEP)
        v_hat = vn / (1.0 - ADAM_B2 ** ADAM_STEP)
        d_ref[...] = -ADAM_LR * (m_hat / (jnp.sqrt(v_hat) + ADAM_EPS) + ADAM_WD * w_ref[...])
        mo_ref[...] = mn
        vo_ref[...] = vn

    spec = _row_spec(rb, cols)
    sds = jax.ShapeDtypeStruct((rows, cols), F32)
    d, mo, vo = pl.pallas_call(
        body, name=name, grid=(rows // rb,), in_specs=[spec] * 4, out_specs=[spec] * 3, out_shape=[sds] * 3,
        compiler_params=_params(("parallel",)),
    )(w2, g2, m2, v2)
    return d.reshape(shape), mo.reshape(shape), vo.reshape(shape)


def kernel(x, p, norm_w, w_in, s5_A_re, s5_A_im, s5_log_dt, s5_B_re, s5_B_im, s5_C_re, s5_C_im, s5_D, s5_w_glu, s5_b_glu, conv_w, conv_b, dt_bias, A_log, ssd_D, ssd_norm_w, w_out, ple_norm_w, w_ple_gate, w_ple_proj, final_norm_w, loss_target, m_norm_w, m_w_in, m_s5_A_re, m_s5_A_im, m_s5_log_dt, m_s5_B_re, m_s5_B_im, m_s5_C_re, m_s5_C_im, m_s5_D, m_s5_w_glu, m_s5_b_glu, m_conv_w, m_conv_b, m_dt_bias, m_A_log, m_ssd_D, m_ssd_norm_w, m_w_out, m_ple_norm_w, m_w_ple_gate, m_w_ple_proj, m_final_norm_w, v_norm_w, v_w_in, v_s5_A_re, v_s5_A_im, v_s5_log_dt, v_s5_B_re, v_s5_B_im, v_s5_C_re, v_s5_C_im, v_s5_D, v_s5_w_glu, v_s5_b_glu, v_conv_w, v_conv_b, v_dt_bias, v_A_log, v_ssd_D, v_ssd_norm_w, v_w_out, v_ple_norm_w, v_w_ple_gate, v_w_ple_proj, v_final_norm_w):
    given = (norm_w, w_in, s5_A_re, s5_A_im, s5_log_dt, s5_B_re, s5_B_im, s5_C_re, s5_C_im, s5_D, s5_w_glu, s5_b_glu, conv_w, conv_b,
             dt_bias, A_log, ssd_D, ssd_norm_w, w_out, ple_norm_w, w_ple_gate, w_ple_proj, final_norm_w)
    given_m = (m_norm_w, m_w_in, m_s5_A_re, m_s5_A_im, m_s5_log_dt, m_s5_B_re, m_s5_B_im, m_s5_C_re, m_s5_C_im, m_s5_D, m_s5_w_glu,
               m_s5_b_glu, m_conv_w, m_conv_b, m_dt_bias, m_A_log, m_ssd_D, m_ssd_norm_w, m_w_out, m_ple_norm_w, m_w_ple_gate,
               m_w_ple_proj, m_final_norm_w)
    given_v = (v_norm_w, v_w_in, v_s5_A_re, v_s5_A_im, v_s5_log_dt, v_s5_B_re, v_s5_B_im, v_s5_C_re, v_s5_C_im, v_s5_D, v_s5_w_glu,
               v_s5_b_glu, v_conv_w, v_conv_b, v_dt_bias, v_A_log, v_ssd_D, v_ssd_norm_w, v_w_out, v_ple_norm_w, v_w_ple_gate,
               v_w_ple_proj, v_final_norm_w)
    wts, mom, var = dict(zip(WEIGHTS, given)), dict(zip(WEIGHTS, given_m)), dict(zip(WEIGHTS, given_v))
    drop = lambda n, a: a if n == "final_norm_w" else a[0]

    wpack = jnp.concatenate([drop(n, wts[n]).astype(BF).reshape(-1, LANES) for n in BIG], axis=0)
    wall, cwall = _all_gather_chips(wpack, drop("conv_w", wts["conv_w"]))
    full, r0 = {}, 0
    for n, (shp, axis) in BIG.items():
        blk = wall[:, r0:r0 + BIG_ROWS[n]].reshape((N_CHIPS,) + shp)
        full[n] = blk.reshape(N_CHIPS * shp[0], shp[1]) if axis == 0 else blk.transpose(1, 0, 2).reshape(shp[0], N_CHIPS * shp[1])
        r0 += BIG_ROWS[n]
    for n in SMALL:
        full[n] = drop(n, wts[n])
    full["conv_w"] = cwall.transpose(1, 0, 2).reshape(4, 2048)

    loss, gx, grads = _local_step(x[0], p[0, 0], loss_target[0], full)
    loss = lax.psum(loss, MESH_AXES)

    gp = _pack_grads({n: grads[n].reshape(SMALL[n]) if n in SMALL else grads[n] for n in WEIGHTS})
    c = lax.axis_index("c")
    from_sibling = _exchange_pair(gp)
    mine = lax.dynamic_index_in_dim(gp, c, axis=1, keepdims=False)
    by_chip_bf, by_chip_tail = _exchange_chips(*_pair_sum(mine, from_sibling))
    reduced_half = _chip_sum(by_chip_bf, by_chip_tail)
    gs, sm = _gather_reduced(reduced_half)
    g_final = {**_unpack_small(sm), **_unpack_shard(gs)}
    chip = 2 * lax.axis_index("x") + lax.axis_index("y")
    g_final["conv_w"] = lax.dynamic_slice_in_dim(g_final["conv_w"], chip * 512, 512, axis=1)

    outs_g, outs_d, outs_m, outs_v = [], [], [], []
    for n in WEIGHTS:
        g = g_final[n].reshape(wts[n].shape)
        d, mo, vo = _adamw(wts[n], g, mom[n], var[n], "adamw_" + n)
        outs_g.append(g)
        outs_d.append(d)
        outs_m.append(mo)
        outs_v.append(vo)
    return (loss, gx[None], *outs_g, *outs_d, *outs_m, *outs_v)
```

```python
import functools
import math

import jax
import jax.numpy as jnp
from jax import lax
from jax.experimental import pallas as pl
from jax.experimental.pallas import tpu as pltpu

F32 = jnp.float32
BF = jnp.bfloat16
EPS = 1e-6
CHUNK = 64
D_MODEL = 1024
S5_GROUPS = 64
S5_CH = 16
S5_STATE = 64
SSD_HEADS = 16
SSD_HEAD_DIM = 64
SSD_GROUPS = 4
SSD_STATE = 128
D_MAIN = 5120
LANES = 128
TOKEN_TILE = 256
VMEM_LIMIT = 56 * 1024 * 1024
MESH_AXES = ("x", "y", "c")
N_CHIPS = 4
ADAM_LR, ADAM_B1, ADAM_B2, ADAM_EPS, ADAM_WD, ADAM_STEP = 0.001, 0.9, 0.999, 1e-08, 0.01, 10
MESH = pl.DeviceIdType.MESH
ANY = pl.BlockSpec(memory_space=pl.ANY)


def _dot(a, b):
    return jnp.dot(a, b, preferred_element_type=F32)


def _dot_nt(a, b):
    return lax.dot_general(a, b, (((1,), (1,)), ((), ())), preferred_element_type=F32)


def _dot_tn(a, b):
    return lax.dot_general(a, b, (((0,), (0,)), ((), ())), preferred_element_type=F32)


def _sigmoid(x):
    return 1.0 / (1.0 + jnp.exp(-x))


def _softplus(x):
    return jnp.maximum(x, 0.0) + jnp.log(1.0 + jnp.exp(-jnp.abs(x)))


_GELU_C = math.sqrt(2.0 / math.pi)


def _gelu(x):
    return 0.5 * x * (1.0 + jnp.tanh(_GELU_C * (x + 0.044715 * x * x * x)))


def _gelu_grad(x):
    th = jnp.tanh(_GELU_C * (x + 0.044715 * x * x * x))
    return 0.5 * (1.0 + th) + 0.5 * x * (1.0 - th * th) * _GELU_C * (1.0 + 3.0 * 0.044715 * x * x)


def _params(sem=None):
    return pltpu.CompilerParams(dimension_semantics=sem, vmem_limit_bytes=VMEM_LIMIT)


def _row_spec(tl, width, col=0):
    return pl.BlockSpec((tl, width), lambda i, col=col: (i, col))


def _const_spec(shape):
    nd = len(shape)
    return pl.BlockSpec(shape, lambda *_: (0,) * nd)


def _in_proj_fwd(x, norm_w, w_main, w_dt):
    L = x.shape[0]
    tl = min(TOKEN_TILE, L)

    def body(x_ref, nw_ref, wm_ref, wd_ref, hn_ref, ps5_ref, pssd_ref, pd_ref):
        xv = x_ref[...]
        r = lax.rsqrt(jnp.mean(xv * xv, axis=-1, keepdims=True) + EPS)
        hn = (xv * r * nw_ref[...]).astype(BF)
        hn_ref[...] = hn
        for j in range(2):
            ps5_ref[:, j * 1024:(j + 1) * 1024] = _dot(hn, wm_ref[:, j * 1024:(j + 1) * 1024]).astype(BF)
        for j in range(3):
            pssd_ref[:, j * 1024:(j + 1) * 1024] = _dot(hn, wm_ref[:, (j + 2) * 1024:(j + 3) * 1024])
        pd_ref[...] = _dot(hn, wd_ref[...])

    return pl.pallas_call(
        body, name="in_proj_fwd", grid=(L // tl,),
        in_specs=[_row_spec(tl, D_MODEL), _const_spec((1, D_MODEL)), _const_spec((D_MODEL, D_MAIN)), _const_spec((D_MODEL, LANES))],
        out_specs=[_row_spec(tl, D_MODEL), _row_spec(tl, 2048), _row_spec(tl, 3072), _row_spec(tl, LANES)],
        out_shape=[jax.ShapeDtypeStruct((L, D_MODEL), BF), jax.ShapeDtypeStruct((L, 2048), BF), jax.ShapeDtypeStruct((L, 3072), F32),
                   jax.ShapeDtypeStruct((L, LANES), F32)],
        compiler_params=_params(("arbitrary",)),
    )(x, norm_w, w_main, w_dt)


def _in_proj_bwd(x, norm_w, dh1, du_flat, dyssm, s5_d, dzs, dzd, dxbc, ddt, w_main, w_dt):
    L = x.shape[0]
    tl = min(TOKEN_TILE, L)

    def body(x_ref, nw_ref, dh1_ref, duf_ref, dys_ref, d_ref, dzs_ref, dzd_ref, dxbc_ref, ddt_ref, wm_ref, wd_ref,
             gx_ref, du_ref, gnw_ref):
        @pl.when(pl.program_id(0) == 0)
        def _():
            gnw_ref[...] = jnp.zeros_like(gnw_ref)

        du = (duf_ref[...].astype(F32) + dys_ref[...].astype(F32) * d_ref[...]).astype(BF)
        du_ref[...] = du
        dhn = _dot_nt(du, wm_ref[:, 0:1024])
        dhn += _dot_nt(dzs_ref[...], wm_ref[:, 1024:2048])
        dhn += _dot_nt(dzd_ref[...], wm_ref[:, 2048:3072])
        dhn += _dot_nt(dxbc_ref[...], wm_ref[:, 3072:5120])
        dhn += _dot_nt(ddt_ref[...].astype(BF), wd_ref[...])
        xv = x_ref[...]
        r = lax.rsqrt(jnp.mean(xv * xv, axis=-1, keepdims=True) + EPS)
        xh = xv * r
        gnw_ref[...] += jnp.sum(dhn * xh, axis=0, keepdims=True)
        g = dhn * nw_ref[...]
        gx_ref[...] = dh1_ref[...] + r * (g - xh * jnp.mean(g * xh, axis=-1, keepdims=True))

    return pl.pallas_call(
        body, name="in_proj_bwd", grid=(L // tl,),
        in_specs=[_row_spec(tl, D_MODEL), _const_spec((1, D_MODEL)), _row_spec(tl, D_MODEL), _row_spec(tl, D_MODEL),
                  _row_spec(tl, D_MODEL), _const_spec((1, D_MODEL)), _row_spec(tl, D_MODEL), _row_spec(tl, D_MODEL),
                  _row_spec(tl, 2048), _row_spec(tl, LANES), _const_spec((D_MODEL, D_MAIN)), _const_spec((D_MODEL, LANES))],
        out_specs=[_row_spec(tl, D_MODEL), _row_spec(tl, D_MODEL), _const_spec((1, D_MODEL))],
        out_shape=[jax.ShapeDtypeStruct((L, D_MODEL), F32), jax.ShapeDtypeStruct((L, D_MODEL), BF), jax.ShapeDtypeStruct((1, D_MODEL), F32)],
        compiler_params=_params(("arbitrary",)),
    )(x, norm_w, dh1, du_flat, dyssm, s5_d, dzs, dzd, dxbc, ddt, w_main, w_dt)


def _matmul_tn(a, b, name):
    L, M = a.shape
    N = b.shape[1]
    tm, tn, tk = min(M, 1024), min(N, 1024), min(L, 512)

    def body(a_ref, b_ref, o_ref):
        @pl.when(pl.program_id(2) == 0)
        def _():
            o_ref[...] = jnp.zeros_like(o_ref)

        o_ref[...] += _dot_tn(a_ref[...].astype(BF), b_ref[...].astype(BF))

    return pl.pallas_call(
        body, name=name, grid=(M // tm, N // tn, L // tk),
        in_specs=[pl.BlockSpec((tk, tm), lambda i, j, k: (k, i)), pl.BlockSpec((tk, tn), lambda i, j, k: (k, j))],
        out_specs=pl.BlockSpec((tm, tn), lambda i, j, k: (i, j)),
        out_shape=jax.ShapeDtypeStruct((M, N), F32),
        compiler_params=_params(("parallel", "parallel", "arbitrary")),
    )(a, b)


def _s5_tables(a_re, a_im, log_dt, b_re, b_im, c_re, c_im):
    hi = lax.Precision.HIGHEST
    dt = jnp.exp(log_dt)[:, None]
    lam_re, lam_im = a_re * dt, a_im * dt
    tau = jnp.arange(CHUNK + 1, dtype=F32)
    mag = jnp.exp(lam_re[:, :, None] * tau)
    ang = lam_im[:, :, None] * tau
    pw = lax.complex(mag * jnp.cos(ang), mag * jnp.sin(ang))
    beta = (pw[:, :, 1] - 1.0) / lax.complex(a_re, a_im)
    bb = beta[:, :, None] * lax.complex(b_re, b_im)
    cc = lax.complex(c_re, c_im)
    cp = cc.transpose(0, 2, 1)[:, :, None, :] * pw[:, :, :CHUNK, None]
    cpf = cp.reshape(S5_GROUPS, S5_STATE, CHUNK * S5_CH)
    bbt = bb.transpose(0, 2, 1)
    kmat = (jnp.einsum("ghn,gnj->ghj", bbt.real, cpf.real, precision=hi)
            - jnp.einsum("ghn,gnj->ghj", bbt.imag, cpf.imag, precision=hi))
    w = bbt[:, :, None, :] * pw[:, :, CHUNK - 1::-1][:, :, :CHUNK].transpose(0, 2, 1)[:, None, :, :]
    wst = jnp.concatenate([w.real, w.imag], axis=-1).reshape(S5_GROUPS, S5_CH * CHUNK, 2 * S5_STATE)
    v = cc.transpose(0, 2, 1)[:, :, None, :] * pw[:, :, 1:CHUNK + 1, None]
    woff = jnp.concatenate([v.real, -v.imag], axis=1).reshape(S5_GROUPS, 2 * S5_STATE, CHUNK * S5_CH)
    a64 = jnp.concatenate([pw[:, :, CHUNK].real, pw[:, :, CHUNK].imag], axis=-1)
    return kmat, wst, woff, a64


def _s5_scan_powers(a_re, a_im, log_dt, nsteps):
    dt = jnp.exp(log_dt)[:, None]
    steps = (CHUNK * (2.0 ** jnp.arange(8, dtype=F32)))[None, :, None]
    mag = jnp.exp((a_re * dt)[:, None, :] * steps)
    ang = (a_im * dt)[:, None, :] * steps
    re, im = mag * jnp.cos(ang), mag * jnp.sin(ang)
    del nsteps
    return jnp.concatenate([re, re], -1), jnp.concatenate([-im, im], -1)


def _build_toeplitz(kmat_ref, tg_ref):
    lane = lax.broadcasted_iota(jnp.int32, (CHUNK, CHUNK * S5_CH), 1)
    srow = lax.broadcasted_iota(jnp.int32, (CHUNK, CHUNK * S5_CH), 0)
    keep = lane >= S5_CH * srow
    for h in range(S5_CH):
        row = jnp.broadcast_to(kmat_ref[0, h:h + 1, :], (CHUNK, CHUNK * S5_CH))
        rolled = pltpu.roll(row, 0, 1, stride=S5_CH, stride_axis=0)
        tg_ref[h * CHUNK:(h + 1) * CHUNK, :] = jnp.where(keep, rolled, 0.0).astype(BF)


def _swap_halves(x):
    return pltpu.roll(x, S5_STATE, 1)


def _s5_core_fwd(uflat, kmat, wst, woff, p1, p2):
    G, nc, W = uflat.shape
    nsteps = max(1, (nc - 1).bit_length())

    def body(u_ref, k_ref, wst_ref, woff_ref, p1_ref, p2_ref, y_ref, h_ref, tg_ref):
        _build_toeplitz(k_ref, tg_ref)
        u = u_ref[0]
        x = _dot(u, wst_ref[0])
        row = lax.broadcasted_iota(jnp.int32, x.shape, 0)
        d = 1
        for k in range(nsteps):
            sh = jnp.where(row >= d, pltpu.roll(x, d, 0), 0.0)
            x = x + p1_ref[0, k:k + 1, :] * sh + p2_ref[0, k:k + 1, :] * _swap_halves(sh)
            d *= 2
        h = jnp.where(row >= 1, pltpu.roll(x, 1, 0), 0.0)
        h_ref[0] = h
        y = _dot(u, tg_ref[...]) + _dot(h.astype(BF), woff_ref[0])
        y_ref[0] = y.astype(BF)

    return pl.pallas_call(
        body, name="s5_core_fwd", grid=(G,),
        in_specs=[pl.BlockSpec((1, nc, W), lambda g: (g, 0, 0)), pl.BlockSpec((1, S5_CH, W), lambda g: (g, 0, 0)),
                  pl.BlockSpec((1, W, 2 * S5_STATE), lambda g: (g, 0, 0)), pl.BlockSpec((1, 2 * S5_STATE, W), lambda g: (g, 0, 0)),
                  pl.BlockSpec((1, 8, 2 * S5_STATE), lambda g: (g, 0, 0)), pl.BlockSpec((1, 8, 2 * S5_STATE), lambda g: (g, 0, 0))],
        out_specs=[pl.BlockSpec((1, nc, W), lambda g: (g, 0, 0)), pl.BlockSpec((1, nc, 2 * S5_STATE), lambda g: (g, 0, 0))],
        out_shape=[jax.ShapeDtypeStruct((G, nc, W), BF), jax.ShapeDtypeStruct((G, nc, 2 * S5_STATE), F32)],
        scratch_shapes=[pltpu.VMEM((W, W), BF)],
        compiler_params=_params(("arbitrary",)),
    )(uflat, kmat, wst, woff, p1, p2)


def _s5_core_bwd(uflat, dyflat, hsave, kmat, wst, woff, p1, p2):
    G, nc, W = uflat.shape
    nsteps = max(1, (nc - 1).bit_length())

    def body(u_ref, dy_ref, h_ref, k_ref, wst_ref, woff_ref, p1_ref, p2_ref,
             du_ref, dk_ref, dwst_ref, dwoff_ref, da_ref, tg_ref, flip_ref):
        @pl.when(pl.program_id(0) == 0)
        def _():
            r = lax.broadcasted_iota(jnp.int32, (W, W), 0)
            c = lax.broadcasted_iota(jnp.int32, (W, W), 1)
            flip_ref[...] = (((r >> 6) == (c >> 6)) & ((r & (CHUNK - 1)) + (c & (CHUNK - 1)) == CHUNK - 1)).astype(BF)

        _build_toeplitz(k_ref, tg_ref)
        u = u_ref[0]
        dy = dy_ref[0]
        h = h_ref[0]
        gh = _dot_nt(dy, woff_ref[0])
        row = lax.broadcasted_iota(jnp.int32, gh.shape, 0)
        x = jnp.where(row < nc - 1, pltpu.roll(gh, nc - 1, 0), 0.0)
        d = 1
        for k in range(nsteps):
            sh = jnp.where(row < nc - d, pltpu.roll(x, nc - d, 0), 0.0)
            x = x + p1_ref[0, k:k + 1, :] * sh - p2_ref[0, k:k + 1, :] * _swap_halves(sh)
            d *= 2
        gs = x.astype(BF)
        du_ref[0] = (_dot_nt(dy, tg_ref[...]) + _dot_nt(gs, wst_ref[0])).astype(BF)
        dwst_ref[0] = _dot_tn(u, gs)
        dwoff_ref[0] = _dot_tn(h.astype(BF), dy)
        r1 = jnp.sum(x * h, axis=0, keepdims=True)
        r2 = jnp.sum(x * _swap_halves(h), axis=0, keepdims=True)
        da_ref[0] = jnp.concatenate([r1, r2, jnp.zeros((6, 2 * S5_STATE), F32)], axis=0)
        lane = lax.broadcasted_iota(jnp.int32, (CHUNK, W), 1)
        srow = lax.broadcasted_iota(jnp.int32, (CHUNK, W), 0)
        keep = lane < S5_CH * (srow + 1)
        ur = _dot(u, flip_ref[...]).astype(BF)
        for hh in range(S5_CH):
            dt_h = _dot_tn(ur[:, hh * CHUNK:(hh + 1) * CHUNK], dy)
            back = pltpu.roll(dt_h, S5_CH, 1, stride=S5_CH, stride_axis=0)
            dk_ref[0, hh:hh + 1, :] = jnp.sum(jnp.where(keep, back, 0.0), axis=0, keepdims=True)

    spec_g = lambda a, b: pl.BlockSpec((1, a, b), lambda g: (g, 0, 0))
    return pl.pallas_call(
        body, name="s5_core_bwd", grid=(G,),
        in_specs=[spec_g(nc, W), spec_g(nc, W), spec_g(nc, 2 * S5_STATE), spec_g(S5_CH, W), spec_g(W, 2 * S5_STATE),
                  spec_g(2 * S5_STATE, W), spec_g(8, 2 * S5_STATE), spec_g(8, 2 * S5_STATE)],
        out_specs=[spec_g(nc, W), spec_g(S5_CH, W), spec_g(W, 2 * S5_STATE), spec_g(2 * S5_STATE, W), spec_g(8, 2 * S5_STATE)],
        out_shape=[jax.ShapeDtypeStruct((G, nc, W), BF), jax.ShapeDtypeStruct((G, S5_CH, W), F32),
                   jax.ShapeDtypeStruct((G, W, 2 * S5_STATE), F32), jax.ShapeDtypeStruct((G, 2 * S5_STATE, W), F32),
                   jax.ShapeDtypeStruct((G, 8, 2 * S5_STATE), F32)],
        scratch_shapes=[pltpu.VMEM((W, W), BF), pltpu.VMEM((W, W), BF)],
        compiler_params=_params(("arbitrary",)),
    )(uflat, dyflat, hsave, kmat, wst, woff, p1, p2)


def _flat_hs(a, nc):
    return a.reshape(nc, CHUNK, S5_GROUPS, S5_CH).transpose(2, 0, 3, 1).reshape(S5_GROUPS, nc, CHUNK * S5_CH)


def _unflat_hs(a, nc):
    return a.reshape(S5_GROUPS, nc, S5_CH, CHUNK).transpose(1, 3, 0, 2).reshape(nc * CHUNK, D_MODEL)


def _flat_tk(a, nc):
    return a.reshape(nc, CHUNK, S5_GROUPS, S5_CH).transpose(2, 0, 1, 3).reshape(S5_GROUPS, nc, CHUNK * S5_CH)


def _unflat_tk(a, nc):
    return a.reshape(S5_GROUPS, nc, CHUNK, S5_CH).transpose(1, 2, 0, 3).reshape(nc * CHUNK, D_MODEL)


def _s5_post_fwd(yssm, proj, s5_d, w_glu, b_glu):
    L = yssm.shape[0]
    tl = min(TOKEN_TILE, L)

    def body(ys_ref, u_ref, z_ref, d_ref, wg_ref, bg_ref, o_ref):
        u = u_ref[...].astype(F32)
        a = _gelu(ys_ref[...].astype(F32) + d_ref[...] * u)
        y = a * _sigmoid(_dot(a.astype(BF), wg_ref[...]) + bg_ref[...])
        z = z_ref[...].astype(F32)
        o_ref[...] = (y * z * _sigmoid(z)).astype(BF)

    return pl.pallas_call(
        body, name="s5_post_fwd", grid=(L // tl,),
        in_specs=[_row_spec(tl, D_MODEL), _row_spec(tl, D_MODEL, 0), _row_spec(tl, D_MODEL, 1), _const_spec((1, D_MODEL)),
                  _const_spec((D_MODEL, D_MODEL)), _const_spec((1, D_MODEL))],
        out_specs=_row_spec(tl, D_MODEL),
        out_shape=jax.ShapeDtypeStruct((L, D_MODEL), BF),
        compiler_params=_params(("arbitrary",)),
    )(yssm, proj, proj, s5_d, w_glu, b_glu)


def _s5_post_bwd(dys5, yssm, proj, s5_d, w_glu, b_glu):
    L = yssm.shape[0]
    tl = min(TOKEN_TILE, L)

    def body(dy_ref, ys_ref, u_ref, z_ref, d_ref, wg_ref, bg_ref, dz_ref, dys_ref, a_ref, dgl_ref, dbg_ref, dd_ref):
        @pl.when(pl.program_id(0) == 0)
        def _():
            dbg_ref[...] = jnp.zeros_like(dbg_ref)
            dd_ref[...] = jnp.zeros_like(dd_ref)

        u = u_ref[...].astype(F32)
        y0 = ys_ref[...].astype(F32) + d_ref[...] * u
        a = _gelu(y0)
        a_bf = a.astype(BF)
        sg = _sigmoid(_dot(a_bf, wg_ref[...]) + bg_ref[...])
        y = a * sg
        z = z_ref[...].astype(F32)
        sz = _sigmoid(z)
        dout = dy_ref[...].astype(F32)
        dz_ref[...] = (dout * y * sz * (1.0 + z * (1.0 - sz))).astype(BF)
        dyv = dout * z * sz
        dgl = dyv * a * sg * (1.0 - sg)
        dgl_bf = dgl.astype(BF)
        da = dyv * sg + _dot_nt(dgl_bf, wg_ref[...])
        dy0 = da * _gelu_grad(y0)
        dbg_ref[...] += jnp.sum(dgl, axis=0, keepdims=True)
        dd_ref[...] += jnp.sum(dy0 * u, axis=0, keepdims=True)
        dys_ref[...] = dy0.astype(BF)
        a_ref[...] = a_bf
        dgl_ref[...] = dgl_bf

    big = jax.ShapeDtypeStruct((L, D_MODEL), BF)
    vec = jax.ShapeDtypeStruct((1, D_MODEL), F32)
    return pl.pallas_call(
        body, name="s5_post_bwd", grid=(L // tl,),
        in_specs=[_row_spec(tl, D_MODEL), _row_spec(tl, D_MODEL), _row_spec(tl, D_MODEL, 0), _row_spec(tl, D_MODEL, 1),
                  _const_spec((1, D_MODEL)), _const_spec((D_MODEL, D_MODEL)), _const_spec((1, D_MODEL))],
        out_specs=[_row_spec(tl, D_MODEL)] * 4 + [_const_spec((1, D_MODEL))] * 2,
        out_shape=[big, big, big, big, vec, vec],
        compiler_params=_params(("arbitrary",)),
    )(dys5, yssm, proj, proj, s5_d, w_glu, b_glu)


def _cumsum_rows(a):
    row = lax.broadcasted_iota(jnp.int32, a.shape, 0)
    d = 1
    while d < a.shape[0]:
        a = a + jnp.where(row >= d, pltpu.roll(a, d, 0), 0.0)
        d *= 2
    return a


def _rev_cumsum_rows(a):
    n = a.shape[0]
    row = lax.broadcasted_iota(jnp.int32, a.shape, 0)
    d = 1
    while d < n:
        a = a + jnp.where(row < n - d, pltpu.roll(a, n - d, 0), 0.0)
        d *= 2
    return a


ROW_BLOCK = 16


def _ssd_conv_fwd(first, xs_ref, bc_ref, hx_ref, hb_ref, cw_ref, cb_ref, xp_ref, xbc_ref, pre_ref, tl):
    hal = jnp.concatenate([hx_ref[...], hb_ref[...]], axis=1)
    xp_ref[0:8, :] = jnp.where(first, 0.0, hal)
    xp_ref[8:8 + tl, 0:1024] = xs_ref[...]
    xp_ref[8:8 + tl, 1024:2048] = bc_ref[...]
    for r0 in range(0, tl, ROW_BLOCK):
        pre = cb_ref[...] + cw_ref[0:1, :] * xp_ref[5 + r0:5 + r0 + ROW_BLOCK, :]
        for k in range(1, 4):
            pre = pre + cw_ref[k:k + 1, :] * xp_ref[5 + k + r0:5 + k + r0 + ROW_BLOCK, :]
        if pre_ref is not None:
            pre_ref[r0:r0 + ROW_BLOCK, :] = pre
        xbc_ref[r0:r0 + ROW_BLOCK, :] = pre * _sigmoid(pre)


def _onehot_lane(h):
    return (lax.broadcasted_iota(jnp.int32, (1, LANES), 1) == h).astype(F32)


def _dot_exact(x, e):
    hi = x.astype(BF)
    r = x - hi.astype(F32)
    mid = r.astype(BF)
    lo = (r - mid.astype(F32)).astype(BF)
    return _dot(hi, e) + _dot(mid, e) + _dot(lo, e)


def _head_expand_matrices():
    e = lax.broadcasted_iota(jnp.int32, (LANES, D_MODEL), 0) == (lax.broadcasted_iota(jnp.int32, (LANES, D_MODEL), 1) >> 6)
    et = (lax.broadcasted_iota(jnp.int32, (D_MODEL, LANES), 0) >> 6) == lax.broadcasted_iota(jnp.int32, (D_MODEL, LANES), 1)
    return e.astype(BF), et.astype(BF)


def _group_masks():
    r64 = lax.broadcasted_iota(jnp.int32, (4 * CHUNK, CHUNK), 0)
    causal4 = (r64 & (CHUNK - 1)) >= lax.broadcasted_iota(jnp.int32, (4 * CHUNK, CHUNK), 1)
    r256 = lax.broadcasted_iota(jnp.int32, (4 * CHUNK, 4 * SSD_HEAD_DIM), 0)
    same = (r256 >> 6) == (lax.broadcasted_iota(jnp.int32, (4 * CHUNK, 4 * SSD_HEAD_DIM), 1) >> 6)
    return causal4, same


def _group_decay(acs, acs_t, j, causal4):
    col = jnp.concatenate([acs[:, 4 * j + hh:4 * j + hh + 1] for hh in range(4)], axis=0)
    rowv = jnp.concatenate([jnp.broadcast_to(acs_t[4 * j + hh:4 * j + hh + 1, :], (CHUNK, CHUNK)) for hh in range(4)], axis=0)
    return jnp.where(causal4, jnp.exp(col - rowv), 0.0)


def _group_last_decay(acs_t, j):
    return jnp.concatenate([jnp.broadcast_to(jnp.exp(acs_t[4 * j + hh:4 * j + hh + 1, CHUNK - 1:CHUNK]), (SSD_HEAD_DIM, 1))
                            for hh in range(4)], axis=0)


def _fold_heads(r):
    return r[0:CHUNK] + r[CHUNK:2 * CHUNK] + r[2 * CHUNK:3 * CHUNK] + r[3 * CHUNK:4 * CHUNK]


def _ssd_specs_in(tl, nt, rev):
    t_of = (lambda i: nt - 1 - i) if rev else (lambda i: i)
    rows = lambda w, col: pl.BlockSpec((tl, w), lambda i: (t_of(i), col))
    halo = lambda col: pl.BlockSpec((8, 1024), lambda i: (jnp.maximum(t_of(i) * (tl // 8) - 1, 0), col))
    return t_of, rows, halo


def _ssd_fwd(proj, pdt, conv_w, conv_b, dt_bias, a_log, ssd_d, norm_w):
    L = proj.shape[0]
    tl = min(TOKEN_TILE, L)
    nt, ncl = L // tl, tl // CHUNK
    _, rows, halo = _ssd_specs_in(tl, nt, False)

    def body(xs_ref, bc_ref, hx_ref, hb_ref, dt_ref, z_ref, cw_ref, cb_ref, dtb_ref, al_ref, dd_ref, nw_ref,
             y_ref, ypre_ref, st_ref, xp_ref, xbc_ref, dts_ref, hst_ref):
        i = pl.program_id(0)

        @pl.when(i == 0)
        def _():
            hst_ref[...] = jnp.zeros_like(hst_ref)

        _ssd_conv_fwd(i == 0, xs_ref, bc_ref, hx_ref, hb_ref, cw_ref, cb_ref, xp_ref, xbc_ref, None, tl)
        dts_ref[...] = _softplus(dt_ref[...] + dtb_ref[...])
        a_neg = -jnp.exp(al_ref[...])
        e16, _ = _head_expand_matrices()
        causal4, same = _group_masks()
        dd_x = _dot_exact(jnp.broadcast_to(dd_ref[...], (8, LANES)), e16)[0:1, :]

        def chunk(c, carry):
            r0 = pl.multiple_of(c * CHUNK, CHUNK)
            xbc = xbc_ref[pl.ds(r0, CHUNK), :]
            dtc = dts_ref[pl.ds(r0, CHUNK), :]
            acs = _cumsum_rows(dtc * a_neg)
            acs_t = acs.T
            acs_x = _dot_exact(acs, e16)
            xs = xbc[:, 0:1024]
            xd = xs * _dot_exact(dtc, e16)
            xd_bf = xd.astype(BF)
            xdd = (xd * jnp.exp(acs_x[CHUNK - 1:CHUNK, :] - acs_x)).astype(BF)
            e_x = jnp.exp(acs_x)
            for j in range(SSD_GROUPS):
                sl = slice(256 * j, 256 * (j + 1))
                bj = xbc[:, 1024 + 128 * j:1024 + 128 * (j + 1)].astype(BF)
                cj = xbc[:, 1536 + 128 * j:1536 + 128 * (j + 1)].astype(BF)
                g = _dot_nt(cj, bj)
                hj = hst_ref[sl, :]
                zj = _dot_nt(cj, hj.astype(BF))
                sc = (jnp.concatenate([g] * 4, axis=0) * _group_decay(acs, acs_t, j, causal4)).astype(BF)
                yd = _fold_heads(jnp.where(same, _dot(sc, xd_bf[:, sl]), 0.0))
                ypre_ref[pl.ds(r0, CHUNK), sl] = yd + e_x[:, sl] * zj + dd_x[:, sl] * xs[:, sl]
                st_ref[c, sl, :] = hj
                hst_ref[sl, :] = _group_last_decay(acs_t, j) * hj + _dot_tn(xdd[:, sl], bj)
            return carry

        lax.fori_loop(0, ncl, chunk, 0)
        for r0 in range(0, tl, ROW_BLOCK):
            rows_ = slice(r0, r0 + ROW_BLOCK)
            z = z_ref[rows_, :]
            gg = ypre_ref[rows_, :] * z * _sigmoid(z)
            for j in range(SSD_GROUPS):
                seg = gg[:, 256 * j:256 * (j + 1)]
                r = lax.rsqrt(jnp.mean(seg * seg, axis=-1, keepdims=True) + EPS)
                y_ref[rows_, 256 * j:256 * (j + 1)] = (seg * r * nw_ref[:, 256 * j:256 * (j + 1)]).astype(BF)

    nc = L // CHUNK
    return pl.pallas_call(
        body, name="ssd_fwd", grid=(nt,),
        in_specs=[rows(1024, 1), rows(1024, 2), halo(1), halo(2), rows(LANES, 0), rows(1024, 0),
                  _const_spec((4, 2048)), _const_spec((1, 2048)), _const_spec((1, LANES)), _const_spec((1, LANES)),
                  _const_spec((1, LANES)), _const_spec((1, D_MODEL))],
        out_specs=[_row_spec(tl, D_MODEL), _row_spec(tl, D_MODEL), pl.BlockSpec((ncl, 1024, SSD_STATE), lambda i: (i, 0, 0))],
        out_shape=[jax.ShapeDtypeStruct((L, D_MODEL), BF), jax.ShapeDtypeStruct((L, D_MODEL), F32),
                   jax.ShapeDtypeStruct((nc, 1024, SSD_STATE), F32)],
        scratch_shapes=[pltpu.VMEM((tl + 8, 2048), F32), pltpu.VMEM((tl, 2048), F32), pltpu.VMEM((tl, LANES), F32),
                        pltpu.VMEM((1024, SSD_STATE), F32)],
        compiler_params=_params(("arbitrary",)),
    )(proj, proj, proj, proj, pdt, proj, conv_w, conv_b, dt_bias, a_log, ssd_d, norm_w)


def _ssd_bwd(dyssd, ypre, proj, pdt, states, conv_w, conv_b, dt_bias, a_log, ssd_d, norm_w):
    L = proj.shape[0]
    tl = min(TOKEN_TILE, L)
    nt, ncl = L // tl, tl // CHUNK
    t_of, rows, halo = _ssd_specs_in(tl, nt, True)

    def body(dy_ref, ypre_ref, z_ref, xs_ref, bc_ref, hx_ref, hb_ref, dt_ref, st_ref, cw_ref, cb_ref, dtb_ref, al_ref,
             dd_ref, nw_ref,
             dxbc_ref, ddt_ref, dz_ref, dcw_ref, dcb_ref, ddtb_ref, dal_ref, ddd_ref, dnw_ref,
             xp_ref, xbc_ref, pre_ref, dts_ref, dyp_ref, dxs_ref, ddts_ref, dp_ref, dh_ref):
        i = pl.program_id(0)

        @pl.when(i == 0)
        def _():
            for r in (dcw_ref, dcb_ref, ddtb_ref, dal_ref, ddd_ref, dnw_ref, dh_ref):
                r[...] = jnp.zeros_like(r)
            dp_ref[tl:tl + 8, :] = jnp.zeros((8, 2048), F32)

        _ssd_conv_fwd(t_of(i) == 0, xs_ref, bc_ref, hx_ref, hb_ref, cw_ref, cb_ref, xp_ref, xbc_ref, pre_ref, tl)
        dts_ref[...] = _softplus(dt_ref[...] + dtb_ref[...])
        a_neg = -jnp.exp(al_ref[...])

        dnw = jnp.zeros((1, D_MODEL), F32)
        for r0 in range(0, tl, ROW_BLOCK):
            rows_ = slice(r0, r0 + ROW_BLOCK)
            ypre = ypre_ref[rows_, :]
            z = z_ref[rows_, :]
            sz = _sigmoid(z)
            silu = z * sz
            gg = ypre * silu
            dout = dy_ref[rows_, :]
            dnw_l = []
            for j in range(SSD_GROUPS):
                sl = slice(256 * j, 256 * (j + 1))
                seg = gg[:, sl]
                r = lax.rsqrt(jnp.mean(seg * seg, axis=-1, keepdims=True) + EPS)
                gh = seg * r
                dnw_l.append(jnp.sum(dout[:, sl] * gh, axis=0, keepdims=True))
                gw = dout[:, sl] * nw_ref[:, sl]
                dgg = r * (gw - gh * jnp.mean(gw * gh, axis=-1, keepdims=True))
                dyp_ref[rows_, sl] = dgg * silu[:, sl]
                dz_ref[rows_, sl] = (dgg * ypre[:, sl] * sz[:, sl] * (1.0 + z[:, sl] * (1.0 - sz[:, sl]))).astype(BF)
            dnw = dnw + jnp.concatenate(dnw_l, axis=1)
        dnw_ref[...] += dnw

        e16, e16t = _head_expand_matrices()
        causal4, same = _group_masks()
        dd_x = _dot_exact(jnp.broadcast_to(dd_ref[...], (8, LANES)), e16)[0:1, :]
        last_row = (lax.broadcasted_iota(jnp.int32, (CHUNK, 1), 0) == CHUNK - 1).astype(F32)
        sel_rows = lax.broadcasted_iota(jnp.int32, (4 * CHUNK, LANES), 0) >> 6
        sel_lanes = lax.broadcasted_iota(jnp.int32, (4 * CHUNK, LANES), 1)

        def chunk(k, carry):
            dal_acc, ddx_acc = carry
            c = ncl - 1 - k
            r0 = pl.multiple_of(c * CHUNK, CHUNK)
            xbc = xbc_ref[pl.ds(r0, CHUNK), :]
            dtc = dts_ref[pl.ds(r0, CHUNK), :]
            dyp = dyp_ref[pl.ds(r0, CHUNK), :]
            acs = _cumsum_rows(dtc * a_neg)
            acs_t = acs.T
            acs_x = _dot_exact(acs, e16)
            dt_x = _dot_exact(dtc, e16)
            xs = xbc[:, 0:1024]
            xd = xs * dt_x
            xd_bf = xd.astype(BF)
            dec_x = jnp.exp(acs_x[CHUNK - 1:CHUNK, :] - acs_x)
            xdd = xd * dec_x
            xdd_bf = xdd.astype(BF)
            dz = dyp * jnp.exp(acs_x)
            dz_bf = dz.astype(BF)
            ddx_acc = ddx_acc + jnp.sum(dyp * xs, axis=0, keepdims=True)
            dacs = jnp.zeros((CHUNK, LANES), F32)
            hsum = jnp.zeros((1, LANES), F32)
            p1_l, p2_l, p3_l = [], [], []
            for j in range(SSD_GROUPS):
                sl = slice(256 * j, 256 * (j + 1))
                bj = xbc[:, 1024 + 128 * j:1024 + 128 * (j + 1)].astype(BF)
                cj = xbc[:, 1536 + 128 * j:1536 + 128 * (j + 1)].astype(BF)
                g = _dot_nt(cj, bj)
                hj = st_ref[c, sl, :]
                hj_bf = hj.astype(BF)
                dhj = dh_ref[sl, :]
                dhj_bf = dhj.astype(BF)
                zj = _dot_nt(cj, hj_bf)
                qj = _dot_nt(bj, dhj_bf)
                lm = _group_decay(acs, acs_t, j, causal4)
                sc = jnp.concatenate([g] * 4, axis=0) * lm
                sc_bf = sc.astype(BF)
                dym = jnp.where(same, jnp.concatenate([dyp[:, sl]] * 4, axis=0), 0.0).astype(BF)
                dsc = _dot_nt(dym, xd_bf[:, sl])
                dxd = _dot_tn(sc_bf, dym) + qj * dec_x[:, sl]
                m = dsc * sc
                dg_bf = _fold_heads(dsc * lm).astype(BF)
                rs = jnp.sum(m, axis=1, keepdims=True)
                e2 = dhj * hj
                for hh in range(4):
                    oh = _onehot_lane(4 * j + hh)
                    dacs = dacs + oh * rs[CHUNK * hh:CHUNK * (hh + 1)]
                    hsum = hsum + oh * jnp.sum(jnp.sum(e2[64 * hh:64 * (hh + 1)], axis=0, keepdims=True), axis=1, keepdims=True)
                sel = (sel_rows + 4 * j == sel_lanes).astype(BF)
                hi = m.astype(BF)
                rem = m - hi.astype(F32)
                mid = rem.astype(BF)
                lo = (rem - mid.astype(F32)).astype(BF)
                dacs = dacs - (_dot_tn(hi, sel) + _dot_tn(mid, sel) + _dot_tn(lo, sel))
                p1_l.append(dz[:, sl] * zj)
                p2_l.append(qj * xdd[:, sl])
                p3_l.append(dxd * xs[:, sl])
                dxs_ref[pl.ds(r0, CHUNK), sl] = dd_x[:, sl] * dyp[:, sl] + dxd * dt_x[:, sl]
                dxs_ref[pl.ds(r0, CHUNK), 1536 + 128 * j:1536 + 128 * (j + 1)] = _dot(dg_bf, bj) + _dot(dz_bf[:, sl], hj_bf)
                dxs_ref[pl.ds(r0, CHUNK), 1024 + 128 * j:1024 + 128 * (j + 1)] = _dot_tn(dg_bf, cj) + _dot(xdd_bf[:, sl], dhj_bf)
                dh_ref[sl, :] = _group_last_decay(acs_t, j) * dhj + _dot_tn(dz_bf[:, sl], cj)
            stacked = jnp.concatenate([jnp.concatenate(p1_l, axis=1), jnp.concatenate(p2_l, axis=1), jnp.concatenate(p3_l, axis=1)], axis=0)
            red = _dot_exact(stacked, e16t)
            r1, r2, ddtc = red[0:CHUNK], red[CHUNK:2 * CHUNK], red[2 * CHUNK:3 * CHUNK]
            tot = jnp.sum(r2, axis=0, keepdims=True) + jnp.exp(acs[CHUNK - 1:CHUNK, :]) * hsum
            da = _rev_cumsum_rows(dacs + r1 - r2 + last_row * tot)
            ddts_ref[pl.ds(r0, CHUNK), :] = ddtc + da * a_neg
            dal_acc = dal_acc + jnp.sum(da * dtc, axis=0, keepdims=True)
            return dal_acc, ddx_acc

        dal_acc, ddx_acc = lax.fori_loop(0, ncl, chunk, (jnp.zeros((1, LANES), F32), jnp.zeros((1, D_MODEL), F32)))
        dal_ref[...] += dal_acc * a_neg
        ddd_ref[...] += _dot_exact(jnp.broadcast_to(ddx_acc, (8, D_MODEL)), e16t)[0:1, :]
        ddt_raw = ddts_ref[...] * _sigmoid(dt_ref[...] + dtb_ref[...])
        ddt_ref[...] = ddt_raw
        ddtb_ref[...] += jnp.sum(ddt_raw, axis=0, keepdims=True)

        dcb = jnp.zeros((1, 2048), F32)
        dcw = [jnp.zeros((1, 2048), F32) for _ in range(4)]
        for r0 in range(0, tl, ROW_BLOCK):
            rows_ = slice(r0, r0 + ROW_BLOCK)
            pre = pre_ref[rows_, :]
            sp = _sigmoid(pre)
            dpre = dxs_ref[rows_, :] * sp * (1.0 + pre * (1.0 - sp))
            dp_ref[rows_, :] = dpre
            dcb = dcb + jnp.sum(dpre, axis=0, keepdims=True)
            for k in range(4):
                dcw[k] = dcw[k] + jnp.sum(dpre * xp_ref[5 + k + r0:5 + k + r0 + ROW_BLOCK, :], axis=0, keepdims=True)
        dcb_ref[...] += dcb
        for k in range(4):
            dcw_ref[k:k + 1, :] += dcw[k]
        for r0 in range(0, tl, ROW_BLOCK):
            dx = cw_ref[0:1, :] * dp_ref[3 + r0:3 + r0 + ROW_BLOCK, :]
            for k in range(1, 4):
                dx = dx + cw_ref[k:k + 1, :] * dp_ref[3 - k + r0:3 - k + r0 + ROW_BLOCK, :]
            dxbc_ref[r0:r0 + ROW_BLOCK, :] = dx.astype(BF)
        dp_ref[tl:tl + 8, :] = dp_ref[0:8, :]

    vec = lambda w: jax.ShapeDtypeStruct((1, w), F32)
    rrow = lambda w: pl.BlockSpec((tl, w), lambda i: (t_of(i), 0))
    return pl.pallas_call(
        body, name="ssd_bwd", grid=(nt,),
        in_specs=[rrow(D_MODEL), rrow(D_MODEL), rows(1024, 0), rows(1024, 1), rows(1024, 2), halo(1), halo(2), rows(LANES, 0),
                  pl.BlockSpec((ncl, 1024, SSD_STATE), lambda i: (t_of(i), 0, 0)),
                  _const_spec((4, 2048)), _const_spec((1, 2048)), _const_spec((1, LANES)), _const_spec((1, LANES)),
                  _const_spec((1, LANES)), _const_spec((1, D_MODEL))],
        out_specs=[rrow(2048), rrow(LANES), rrow(D_MODEL), _const_spec((8, 2048)), _const_spec((1, 2048)),
                   _const_spec((1, LANES)), _const_spec((1, LANES)), _const_spec((1, LANES)), _const_spec((1, D_MODEL))],
        out_shape=[jax.ShapeDtypeStruct((L, 2048), BF), jax.ShapeDtypeStruct((L, LANES), F32), jax.ShapeDtypeStruct((L, D_MODEL), BF),
                   jax.ShapeDtypeStruct((8, 2048), F32), vec(2048), vec(LANES), vec(LANES), vec(LANES), vec(D_MODEL)],
        scratch_shapes=[pltpu.VMEM((tl + 8, 2048), F32), pltpu.VMEM((tl, 2048), F32), pltpu.VMEM((tl, 2048), F32),
                        pltpu.VMEM((tl, LANES), F32), pltpu.VMEM((tl, D_MODEL), F32), pltpu.VMEM((tl, 2048), F32),
                        pltpu.VMEM((tl, LANES), F32), pltpu.VMEM((tl + 8, 2048), F32), pltpu.VMEM((1024, SSD_STATE), F32)],
        compiler_params=_params(("arbitrary",)),
    )(dyssd, ypre, proj, proj, proj, proj, proj, pdt, states, conv_w, conv_b, dt_bias, a_log, ssd_d, norm_w)


def _head_fwd_bwd(x, ys5, yssd, p, target, w_out, w_gate, w_proj, ple_nw, fin_nw):
    L = x.shape[0]
    tl = min(TOKEN_TILE, L)
    inv_d = 1.0 / D_MODEL

    def body(x_ref, ys_ref, yd_ref, p_ref, t_ref, wo_ref, wg_ref, wp_ref, pnw_ref, fnw_ref,
             loss_ref, dys_ref, dyd_ref, dh1_ref, n2_ref, dgl_ref, dpp_ref, dpnw_ref, dfnw_ref):
        @pl.when(pl.program_id(0) == 0)
        def _():
            loss_ref[...] = jnp.zeros_like(loss_ref)
            dpnw_ref[...] = jnp.zeros_like(dpnw_ref)
            dfnw_ref[...] = jnp.zeros_like(dfnw_ref)

        h1 = x_ref[...] + _dot(ys_ref[...], wo_ref[0:1024, :]) + _dot(yd_ref[...], wo_ref[1024:2048, :])
        r1 = lax.rsqrt(jnp.mean(h1 * h1, axis=-1, keepdims=True) + EPS)
        hh1 = h1 * r1
        n2 = (hh1 * pnw_ref[...]).astype(BF)
        gate = _sigmoid(_dot(n2, wg_ref[...]))
        pp = _dot(p_ref[...].astype(BF), wp_ref[...])
        h2 = h1 + pp * gate
        r2 = lax.rsqrt(jnp.mean(h2 * h2, axis=-1, keepdims=True) + EPS)
        hh2 = h2 * r2
        err = hh2 * fnw_ref[...] - t_ref[...]
        loss_ref[...] += 0.5 * inv_d * jnp.sum(err * err)
        dyo = err * inv_d
        dfnw_ref[...] += jnp.sum(dyo * hh2, axis=0, keepdims=True)
        g2 = dyo * fnw_ref[...]
        dh2 = r2 * (g2 - hh2 * jnp.mean(g2 * hh2, axis=-1, keepdims=True))
        dpp_ref[...] = (dh2 * gate).astype(BF)
        dgl = (dh2 * pp * gate * (1.0 - gate)).astype(BF)
        dgl_ref[...] = dgl
        n2_ref[...] = n2
        dn2 = _dot_nt(dgl, wg_ref[...])
        dpnw_ref[...] += jnp.sum(dn2 * hh1, axis=0, keepdims=True)
        g1 = dn2 * pnw_ref[...]
        dh1 = dh2 + r1 * (g1 - hh1 * jnp.mean(g1 * hh1, axis=-1, keepdims=True))
        dh1_ref[...] = dh1
        dh1_bf = dh1.astype(BF)
        dys_ref[...] = _dot_nt(dh1_bf, wo_ref[0:1024, :]).astype(BF)
        dyd_ref[...] = _dot_nt(dh1_bf, wo_ref[1024:2048, :])

    big = jax.ShapeDtypeStruct((L, D_MODEL), BF)
    vec = jax.ShapeDtypeStruct((1, D_MODEL), F32)
    return pl.pallas_call(
        body, name="head_fwd_bwd", grid=(L // tl,),
        in_specs=[_row_spec(tl, D_MODEL), _row_spec(tl, D_MODEL), _row_spec(tl, D_MODEL), _row_spec(tl, 256), _row_spec(tl, D_MODEL),
                  _const_spec((2048, D_MODEL)), _const_spec((D_MODEL, D_MODEL)), _const_spec((256, D_MODEL)),
                  _const_spec((1, D_MODEL)), _const_spec((1, D_MODEL))],
        out_specs=[_const_spec((8, LANES)), _row_spec(tl, D_MODEL), _row_spec(tl, D_MODEL), _row_spec(tl, D_MODEL),
                   _row_spec(tl, D_MODEL), _row_spec(tl, D_MODEL), _row_spec(tl, D_MODEL), _const_spec((1, D_MODEL)), _const_spec((1, D_MODEL))],
        out_shape=[jax.ShapeDtypeStruct((8, LANES), F32), big, jax.ShapeDtypeStruct((L, D_MODEL), F32),
                   jax.ShapeDtypeStruct((L, D_MODEL), F32), big, big, big, vec, vec],
        compiler_params=_params(("arbitrary",)),
    )(x, ys5, yssd, p, target, w_out, w_gate, w_proj, ple_nw, fin_nw)


def _pad_lanes(v):
    return jnp.pad(v.reshape(1, -1), ((0, 0), (0, LANES - v.size)))


def _local_step(x, p, target, w):
    L = x.shape[0]
    nc = L // CHUNK
    nsteps = max(1, (nc - 1).bit_length())
    w_in = w["w_in"]
    w_main = w_in[:, :D_MAIN]
    w_dt = jnp.pad(w_in[:, D_MAIN:], ((0, 0), (0, LANES - SSD_HEADS)))
    norm_w = w["norm_w"].reshape(1, -1)
    s5_d = w["s5_D"].reshape(1, -1)
    b_glu = w["s5_b_glu"].reshape(1, -1)
    conv_b = w["conv_b"].reshape(1, -1)
    dtb, alog, ssd_d = _pad_lanes(w["dt_bias"]), _pad_lanes(w["A_log"]), _pad_lanes(w["ssd_D"])
    ssd_nw = w["ssd_norm_w"].reshape(1, -1)
    ple_nw = w["ple_norm_w"].reshape(1, -1)
    fin_nw = w["final_norm_w"].reshape(1, -1)

    s5_args = (w["s5_A_re"], w["s5_A_im"], w["s5_log_dt"], w["s5_B_re"], w["s5_B_im"], w["s5_C_re"], w["s5_C_im"])
    (kmat, wst, woff, _a64), tables_vjp = jax.vjp(_s5_tables, *s5_args)
    p1, p2 = _s5_scan_powers(w["s5_A_re"], w["s5_A_im"], w["s5_log_dt"], nsteps)
    wst_bf, woff_bf = wst.astype(BF), woff.astype(BF)

    hn, proj, pssd, pdt = _in_proj_fwd(x, norm_w, w_main, w_dt)
    uflat = _flat_hs(proj[:, :D_MODEL], nc)
    yflat, hsave = _s5_core_fwd(uflat, kmat, wst_bf, woff_bf, p1, p2)
    yssm = _unflat_tk(yflat, nc)
    ys5 = _s5_post_fwd(yssm, proj, s5_d, w["s5_w_glu"], b_glu)
    yssd, ypre, states = _ssd_fwd(pssd, pdt, w["conv_w"], conv_b, dtb, alog, ssd_d, ssd_nw)
    (loss8, dys5, dyssd, dh1, n2, dgl2, dpp, g_ple_nw, g_fin_nw) = _head_fwd_bwd(
        x, ys5, yssd, p, target, w["w_out"], w["w_ple_gate"], w["w_ple_proj"], ple_nw, fin_nw)

    (dxbc, ddt, dzd, g_cw, g_cb, g_dtb, g_alog, g_ssd_d, g_ssd_nw) = _ssd_bwd(
        dyssd, ypre, pssd, pdt, states, w["conv_w"], conv_b, dtb, alog, ssd_d, ssd_nw)
    dzs, dyssm, a_glu, dgl1, g_bglu, g_s5d = _s5_post_bwd(dys5, yssm, proj, s5_d, w["s5_w_glu"], b_glu)
    duflat, dkmat, dwst, dwoff, da8 = _s5_core_bwd(uflat, _flat_tk(dyssm, nc), hsave, kmat, wst_bf, woff_bf, p1, p2)
    da64 = jnp.concatenate([da8[:, 0, :S5_STATE] + da8[:, 0, S5_STATE:], da8[:, 1, S5_STATE:] - da8[:, 1, :S5_STATE]], axis=-1)
    g_s5 = tables_vjp((dkmat, dwst, dwoff, da64))
    gx, du, g_norm_w = _in_proj_bwd(x, norm_w, dh1, _unflat_hs(duflat, nc), dyssm, s5_d, dzs, dzd, dxbc, ddt, w_main, w_dt)

    g_w_in = jnp.concatenate([
        _matmul_tn(hn, du, "dw_in_u"), _matmul_tn(hn, dzs, "dw_in_zs"), _matmul_tn(hn, dzd, "dw_in_zd"),
        _matmul_tn(hn, dxbc, "dw_in_xbc"), _matmul_tn(hn, ddt, "dw_in_dt")[:, :SSD_HEADS]], axis=1)
    grads = {
        "norm_w": g_norm_w, "w_in": g_w_in,
        "s5_A_re": g_s5[0], "s5_A_im": g_s5[1], "s5_log_dt": g_s5[2], "s5_B_re": g_s5[3], "s5_B_im": g_s5[4],
        "s5_C_re": g_s5[5], "s5_C_im": g_s5[6], "s5_D": g_s5d, "s5_w_glu": _matmul_tn(a_glu, dgl1, "dw_glu"), "s5_b_glu": g_bglu,
        "conv_w": g_cw[:4], "conv_b": g_cb, "dt_bias": g_dtb[:, :SSD_HEADS], "A_log": g_alog[:, :SSD_HEADS],
        "ssd_D": g_ssd_d[:, :SSD_HEADS], "ssd_norm_w": g_ssd_nw,
        "w_out": jnp.concatenate([_matmul_tn(ys5, dh1, "dw_out_s5"), _matmul_tn(yssd, dh1, "dw_out_ssd")], axis=0),
        "ple_norm_w": g_ple_nw, "w_ple_gate": _matmul_tn(n2, dgl2, "dw_gate"), "w_ple_proj": _matmul_tn(p, dpp, "dw_proj"),
        "final_norm_w": g_fin_nw,
    }
    return loss8[0, 0], gx, grads


WEIGHTS = ("norm_w", "w_in", "s5_A_re", "s5_A_im", "s5_log_dt", "s5_B_re", "s5_B_im", "s5_C_re", "s5_C_im", "s5_D", "s5_w_glu",
           "s5_b_glu", "conv_w", "conv_b", "dt_bias", "A_log", "ssd_D", "ssd_norm_w", "w_out", "ple_norm_w", "w_ple_gate",
           "w_ple_proj", "final_norm_w")
BIG = {"w_in": ((1024, 1284), 1), "s5_w_glu": ((256, 1024), 0), "w_out": ((512, 1024), 0), "w_ple_gate": ((256, 1024), 0),
       "w_ple_proj": ((256, 256), 1)}
SMALL = {"norm_w": (1024,), "s5_A_re": (64, 64), "s5_A_im": (64, 64), "s5_log_dt": (64,), "s5_B_re": (64, 64, 16),
         "s5_B_im": (64, 64, 16), "s5_C_re": (64, 16, 64), "s5_C_im": (64, 16, 64), "s5_D": (1024,), "s5_b_glu": (1024,),
         "conv_w": (4, 2048), "conv_b": (2048,), "dt_bias": (16,), "A_log": (16,), "ssd_D": (16,), "ssd_norm_w": (1024,),
         "ple_norm_w": (1024,), "final_norm_w": (1024,)}
BIG_ROWS = {n: s[0] * s[1] // LANES for n, (s, _) in BIG.items()}
BIG_ROWS_TOTAL = sum(BIG_ROWS.values())
SMALL_TOTAL = sum(math.prod(s) for s in SMALL.values())
SMALL_PIECE_ROWS = -(-SMALL_TOTAL // (N_CHIPS * 16 * LANES)) * 16
HALF_ROWS = (BIG_ROWS_TOTAL + SMALL_PIECE_ROWS) // 2
SMALL_ROW0 = BIG_ROWS_TOTAL - HALF_ROWS


def _mesh_pos():
    return lax.axis_index("x"), lax.axis_index("y"), lax.axis_index("c")


def _other_chips(x, y):
    return [(1 - x, y), (x, 1 - y), (1 - x, 1 - y)]


def _comm_params():
    return pltpu.CompilerParams(has_side_effects=True)


def _all_gather_chips(wpack, cw):
    half = wpack.shape[0] // 2

    def body(w_ref, c_ref, wo_ref, co_ref, send_sems, recv_sems, fwd_send, fwd_recv, loc_sems):
        x, y, c = _mesh_pos()
        me = 2 * x + y
        sib = (x, y, 1 - c)
        mine = pl.ds(c * half, half)
        theirs = pl.ds((1 - c) * half, half)
        others = _other_chips(x, y)
        loc = [pltpu.make_async_copy(w_ref, wo_ref.at[me], loc_sems.at[0]),
               pltpu.make_async_copy(c_ref, co_ref.at[me], loc_sems.at[1])]
        for cp in loc:
            cp.start()

        def from_chip(k, chip, dev):
            return pltpu.make_async_remote_copy(w_ref.at[mine], wo_ref.at[chip, mine], send_sems.at[2 * k], recv_sems.at[2 * k],
                                                device_id=dev, device_id_type=MESH)

        def conv_from(k, chip, dev):
            return pltpu.make_async_remote_copy(c_ref, co_ref.at[chip], send_sems.at[2 * k + 1], recv_sems.at[2 * k + 1],
                                                device_id=dev, device_id_type=MESH)

        def passed(k, chip, rows):
            return pltpu.make_async_remote_copy(wo_ref.at[chip, rows], wo_ref.at[chip, rows], fwd_send.at[k], fwd_recv.at[k],
                                                device_id=sib, device_id_type=MESH)

        sends = []
        for k, (px, py) in enumerate(others):
            sends += [from_chip(k, me, (px, py, c)), conv_from(k, me, (px, py, c))]
        for cp in sends:
            cp.start()
        fwds = []
        for k, (px, py) in enumerate(others):
            chip = 2 * px + py
            from_chip(k, chip, (px, py, c)).wait_recv()
            fwds.append(passed(k, chip, mine))
            fwds[-1].start()
        for k, (px, py) in enumerate(others):
            chip = 2 * px + py
            passed(k, chip, theirs).wait_recv()
            conv_from(k, chip, (px, py, c)).wait_recv()
        for cp in sends + fwds:
            cp.wait_send()
        for cp in loc:
            cp.wait()

    return pl.pallas_call(
        body, name="all_gather_weights", in_specs=[ANY, ANY], out_specs=[ANY, ANY],
        out_shape=[jax.ShapeDtypeStruct((N_CHIPS,) + wpack.shape, wpack.dtype), jax.ShapeDtypeStruct((N_CHIPS,) + cw.shape, cw.dtype)],
        scratch_shapes=[pltpu.SemaphoreType.DMA((6,)), pltpu.SemaphoreType.DMA((6,)), pltpu.SemaphoreType.DMA((3,)),
                        pltpu.SemaphoreType.DMA((3,)), pltpu.SemaphoreType.DMA((2,))],
        compiler_params=_comm_params(),
    )(wpack, cw)


def _exchange_pair(gp):
    def body(g_ref, r_ref, send_sems, recv_sems):
        x, y, c = _mesh_pos()
        cps = [pltpu.make_async_remote_copy(g_ref.at[s, 1 - c], r_ref.at[s], send_sems.at[s], recv_sems.at[s],
                                            device_id=(x, y, 1 - c), device_id_type=MESH) for s in range(N_CHIPS)]
        for cp in cps:
            cp.start()
        for cp in cps:
            cp.wait()

    return pl.pallas_call(
        body, name="grad_exchange_pair", in_specs=[ANY], out_specs=ANY,
        out_shape=jax.ShapeDtypeStruct((N_CHIPS,) + gp.shape[2:], gp.dtype),
        scratch_shapes=[pltpu.SemaphoreType.DMA((N_CHIPS,)), pltpu.SemaphoreType.DMA((N_CHIPS,))],
        compiler_params=_comm_params(),
    )(gp)


def _pair_sum(mine, from_sibling):
    def body(a_ref, b_ref, bf_ref, tail_ref):
        s = a_ref[0] + b_ref[0]
        bf_ref[0] = s.astype(BF)
        tail_ref[0] = s[SMALL_ROW0:, :]

    piece = pl.BlockSpec((1, HALF_ROWS, LANES), lambda i: (i, 0, 0))
    return pl.pallas_call(
        body, name="grad_pair_sum", grid=(N_CHIPS,), in_specs=[piece, piece],
        out_specs=[piece, pl.BlockSpec((1, SMALL_PIECE_ROWS, LANES), lambda i: (i, 0, 0))],
        out_shape=[jax.ShapeDtypeStruct((N_CHIPS, HALF_ROWS, LANES), BF), jax.ShapeDtypeStruct((N_CHIPS, SMALL_PIECE_ROWS, LANES), F32)],
        compiler_params=_params(("parallel",)),
    )(mine, from_sibling)


def _exchange_chips(ps_bf, ps_tail):
    def body(p_ref, t_ref, r_ref, rt_ref, send_sems, recv_sems, loc_sems):
        x, y, c = _mesh_pos()
        me = 2 * x + y
        loc = [pltpu.make_async_copy(p_ref.at[me], r_ref.at[me], loc_sems.at[0]),
               pltpu.make_async_copy(t_ref.at[me], rt_ref.at[me], loc_sems.at[1])]
        for cp in loc:
            cp.start()
        cps = []
        for k, (px, py) in enumerate(_other_chips(x, y)):
            cps.append(pltpu.make_async_remote_copy(p_ref.at[2 * px + py], r_ref.at[me], send_sems.at[2 * k], recv_sems.at[2 * k],
                                                    device_id=(px, py, c), device_id_type=MESH))
            cps.append(pltpu.make_async_remote_copy(t_ref.at[2 * px + py], rt_ref.at[me], send_sems.at[2 * k + 1],
                                                    recv_sems.at[2 * k + 1], device_id=(px, py, c), device_id_type=MESH))
        for cp in cps:
            cp.start()
        for cp in cps + loc:
            cp.wait()

    return pl.pallas_call(
        body, name="grad_exchange_chips", in_specs=[ANY, ANY], out_specs=[ANY, ANY],
        out_shape=[jax.ShapeDtypeStruct(ps_bf.shape, ps_bf.dtype), jax.ShapeDtypeStruct(ps_tail.shape, ps_tail.dtype)],
        scratch_shapes=[pltpu.SemaphoreType.DMA((6,)), pltpu.SemaphoreType.DMA((6,)), pltpu.SemaphoreType.DMA((2,))],
        compiler_params=_comm_params(),
    )(ps_bf, ps_tail)


def _chip_sum(by_chip_bf, by_chip_tail):
    def body(b_ref, t_ref, o_ref):
        acc = b_ref[0, 0:SMALL_ROW0, :].astype(F32)
        tail = t_ref[0]
        for k in range(1, N_CHIPS):
            acc = acc + b_ref[k, 0:SMALL_ROW0, :].astype(F32)
            tail = tail + t_ref[k]
        o_ref[0:SMALL_ROW0, :] = acc
        o_ref[SMALL_ROW0:, :] = tail

    return pl.pallas_call(
        body, name="grad_chip_sum", out_shape=jax.ShapeDtypeStruct((HALF_ROWS, LANES), F32),
        compiler_params=_params(),
    )(by_chip_bf, by_chip_tail)


def _swap_reduced_halves(gh):
    def body(g_ref, gs_ref, send_sem, recv_sem, loc_sem):
        x, y, c = _mesh_pos()
        loc = pltpu.make_async_copy(g_ref, gs_ref.at[c], loc_sem)
        loc.start()
        cp = pltpu.make_async_remote_copy(g_ref, gs_ref.at[c], send_sem, recv_sem, device_id=(x, y, 1 - c), device_id_type=MESH)
        cp.start()
        cp.wait()
        loc.wait()

    return pl.pallas_call(
        body, name="grad_swap_halves", in_specs=[ANY], out_specs=ANY,
        out_shape=jax.ShapeDtypeStruct((2,) + gh.shape, gh.dtype),
        scratch_shapes=[pltpu.SemaphoreType.DMA, pltpu.SemaphoreType.DMA, pltpu.SemaphoreType.DMA],
        compiler_params=_comm_params(),
    )(gh)


def _gather_small(gs):
    def body(gs_ref, sm_ref, send_sems, recv_sems, loc_sem):
        x, y, c = _mesh_pos()
        me = 2 * x + y
        small = gs_ref.at[1, pl.ds(SMALL_ROW0, SMALL_PIECE_ROWS)]
        loc = pltpu.make_async_copy(small, sm_ref.at[me], loc_sem)
        loc.start()
        cps = [pltpu.make_async_remote_copy(small, sm_ref.at[me], send_sems.at[k], recv_sems.at[k],
                                            device_id=(px, py, c), device_id_type=MESH)
               for k, (px, py) in enumerate(_other_chips(x, y))]
        for cp in cps:
            cp.start()
        for cp in cps:
            cp.wait()
        loc.wait()

    return pl.pallas_call(
        body, name="grad_gather_small", in_specs=[ANY], out_specs=ANY,
        out_shape=jax.ShapeDtypeStruct((N_CHIPS, SMALL_PIECE_ROWS, LANES), gs.dtype),
        scratch_shapes=[pltpu.SemaphoreType.DMA((3,)), pltpu.SemaphoreType.DMA((3,)), pltpu.SemaphoreType.DMA],
        compiler_params=_comm_params(),
    )(gs)


def _pack_grads(grads):
    small = jnp.concatenate([grads[n].reshape(-1) for n in SMALL])
    small = jnp.pad(small, (0, N_CHIPS * SMALL_PIECE_ROWS * LANES - SMALL_TOTAL)).reshape(N_CHIPS, SMALL_PIECE_ROWS, LANES)
    pieces = []
    for s in range(N_CHIPS):
        rows = []
        for n, (shp, axis) in BIG.items():
            g = grads[n]
            blk = g[s * shp[0]:(s + 1) * shp[0], :] if axis == 0 else g[:, s * shp[1]:(s + 1) * shp[1]]
            rows.append(blk.reshape(-1, LANES))
        rows.append(small[s])
        pieces.append(jnp.concatenate(rows, axis=0).reshape(2, HALF_ROWS, LANES))
    return jnp.stack(pieces)


def _unpack_shard(gs):
    rows = gs.reshape(2 * HALF_ROWS, LANES)
    out, r0 = {}, 0
    for n, (shp, _) in BIG.items():
        out[n] = rows[r0:r0 + BIG_ROWS[n]].reshape(shp)
        r0 += BIG_ROWS[n]
    return out


def _unpack_small(sm):
    flat = sm.reshape(-1)
    out, o = {}, 0
    for n, shp in SMALL.items():
        k = math.prod(shp)
        out[n] = flat[o:o + k].reshape(shp)
        o += k
    return out


def _as_2d(a):
    n = a.size
    if a.ndim >= 2 and a.shape[-1] > 1024:
        return a.reshape(-1, a.shape[-1])
    if n % 1024 == 0:
        return a.reshape(n // 1024, 1024)
    return a.reshape(1, n)


def _adamw(w, g, m, v, name):
    shape = w.shape
    w2, g2, m2, v2 = (_as_2d(a) for a in (w, g, m, v))
    rows, cols = w2.shape
    rb = 256 if rows >= 512 else rows

    def body(w_ref, g_ref, m_ref, v_ref, d_ref, mo_ref, vo_ref):
        gv = g_ref[...]
        mn = ADAM_B1 * m_ref[...] + (1.0 - ADAM_B1) * gv
        vn = ADAM_B2 * v_ref[...] + (1.0 - ADAM_B2) * (gv * gv)
        m_hat = mn / (1.0 - ADAM_B1 ** ADAM_STEP)
        v_hat = vn / (1.0 - ADAM_B2 ** ADAM_STEP)
        d_ref[...] = -ADAM_LR * (m_hat / (jnp.sqrt(v_hat) + ADAM_EPS) + ADAM_WD * w_ref[...])
        mo_ref[...] = mn
        vo_ref[...] = vn

    spec = _row_spec(rb, cols)
    sds = jax.ShapeDtypeStruct((rows, cols), F32)
    d, mo, vo = pl.pallas_call(
        body, name=name, grid=(rows // rb,), in_specs=[spec] * 4, out_specs=[spec] * 3, out_shape=[sds] * 3,
        compiler_params=_params(("parallel",)),
    )(w2, g2, m2, v2)
    return d.reshape(shape), mo.reshape(shape), vo.reshape(shape)


def kernel(x, p, norm_w, w_in, s5_A_re, s5_A_im, s5_log_dt, s5_B_re, s5_B_im, s5_C_re, s5_C_im, s5_D, s5_w_glu, s5_b_glu, conv_w, conv_b, dt_bias, A_log, ssd_D, ssd_norm_w, w_out, ple_norm_w, w_ple_gate, w_ple_proj, final_norm_w, loss_target, m_norm_w, m_w_in, m_s5_A_re, m_s5_A_im, m_s5_log_dt, m_s5_B_re, m_s5_B_im, m_s5_C_re, m_s5_C_im, m_s5_D, m_s5_w_glu, m_s5_b_glu, m_conv_w, m_conv_b, m_dt_bias, m_A_log, m_ssd_D, m_ssd_norm_w, m_w_out, m_ple_norm_w, m_w_ple_gate, m_w_ple_proj, m_final_norm_w, v_norm_w, v_w_in, v_s5_A_re, v_s5_A_im, v_s5_log_dt, v_s5_B_re, v_s5_B_im, v_s5_C_re, v_s5_C_im, v_s5_D, v_s5_w_glu, v_s5_b_glu, v_conv_w, v_conv_b, v_dt_bias, v_A_log, v_ssd_D, v_ssd_norm_w, v_w_out, v_ple_norm_w, v_w_ple_gate, v_w_ple_proj, v_final_norm_w):
    given = (norm_w, w_in, s5_A_re, s5_A_im, s5_log_dt, s5_B_re, s5_B_im, s5_C_re, s5_C_im, s5_D, s5_w_glu, s5_b_glu, conv_w, conv_b,
             dt_bias, A_log, ssd_D, ssd_norm_w, w_out, ple_norm_w, w_ple_gate, w_ple_proj, final_norm_w)
    given_m = (m_norm_w, m_w_in, m_s5_A_re, m_s5_A_im, m_s5_log_dt, m_s5_B_re, m_s5_B_im, m_s5_C_re, m_s5_C_im, m_s5_D, m_s5_w_glu,
               m_s5_b_glu, m_conv_w, m_conv_b, m_dt_bias, m_A_log, m_ssd_D, m_ssd_norm_w, m_w_out, m_ple_norm_w, m_w_ple_gate,
               m_w_ple_proj, m_final_norm_w)
    given_v = (v_norm_w, v_w_in, v_s5_A_re, v_s5_A_im, v_s5_log_dt, v_s5_B_re, v_s5_B_im, v_s5_C_re, v_s5_C_im, v_s5_D, v_s5_w_glu,
               v_s5_b_glu, v_conv_w, v_conv_b, v_dt_bias, v_A_log, v_ssd_D, v_ssd_norm_w, v_w_out, v_ple_norm_w, v_w_ple_gate,
               v_w_ple_proj, v_final_norm_w)
    wts, mom, var = dict(zip(WEIGHTS, given)), dict(zip(WEIGHTS, given_m)), dict(zip(WEIGHTS, given_v))
    drop = lambda n, a: a if n == "final_norm_w" else a[0]

    wpack = jnp.concatenate([drop(n, wts[n]).astype(BF).reshape(-1, LANES) for n in BIG], axis=0)
    wall, cwall = _all_gather_chips(wpack, drop("conv_w", wts["conv_w"]))
    full, r0 = {}, 0
    for n, (shp, axis) in BIG.items():
        blk = wall[:, r0:r0 + BIG_ROWS[n]].reshape((N_CHIPS,) + shp)
        full[n] = blk.reshape(N_CHIPS * shp[0], shp[1]) if axis == 0 else blk.transpose(1, 0, 2).reshape(shp[0], N_CHIPS * shp[1])
        r0 += BIG_ROWS[n]
    for n in SMALL:
        full[n] = drop(n, wts[n])
    full["conv_w"] = cwall.transpose(1, 0, 2).reshape(4, 2048)

    loss, gx, grads = _local_step(x[0], p[0, 0], loss_target[0], full)
    loss = lax.psum(loss, MESH_AXES)

    gp = _pack_grads({n: grads[n].reshape(SMALL[n]) if n in SMALL else grads[n] for n in WEIGHTS})
    c = lax.axis_index("c")
    from_sibling = _exchange_pair(gp)
    mine = lax.dynamic_index_in_dim(gp, c, axis=1, keepdims=False)
    by_chip_bf, by_chip_tail = _exchange_chips(*_pair_sum(mine, from_sibling))
    reduced_half = _chip_sum(by_chip_bf, by_chip_tail)
    gs = _swap_reduced_halves(reduced_half)
    sm = _gather_small(gs)
    g_final = {**_unpack_small(sm), **_unpack_shard(gs)}
    chip = 2 * lax.axis_index("x") + lax.axis_index("y")
    g_final["conv_w"] = lax.dynamic_slice_in_dim(g_final["conv_w"], chip * 512, 512, axis=1)

    outs_g, outs_d, outs_m, outs_v = [], [], [], []
    for n in WEIGHTS:
        g = g_final[n].reshape(wts[n].shape)
        d, mo, vo = _adamw(wts[n], g, mom[n], var[n], "adamw_" + n)
        outs_g.append(g)
        outs_d.append(d)
        outs_m.append(mo)
        outs_v.append(vo)
    return (loss, gx[None], *outs_g, *outs_d, *outs_m, *outs_v)
```

```python
import functools
import math

import jax
import jax.numpy as jnp
from jax import lax
from jax.experimental import pallas as pl
from jax.experimental.pallas import tpu as pltpu

F32 = jnp.float32
BF = jnp.bfloat16
EPS = 1e-6
CHUNK = 64
D_MODEL = 1024
S5_GROUPS = 64
S5_CH = 16
S5_STATE = 64
SSD_HEADS = 16
SSD_HEAD_DIM = 64
SSD_GROUPS = 4
SSD_STATE = 128
D_MAIN = 5120
LANES = 128
TOKEN_TILE = 256
VMEM_LIMIT = 56 * 1024 * 1024
MESH_AXES = ("x", "y", "c")
N_CHIPS = 4
ADAM_LR, ADAM_B1, ADAM_B2, ADAM_EPS, ADAM_WD, ADAM_STEP = 0.001, 0.9, 0.999, 1e-08, 0.01, 10
MESH = pl.DeviceIdType.MESH
ANY = pl.BlockSpec(memory_space=pl.ANY)


def _dot(a, b):
    return jnp.dot(a, b, preferred_element_type=F32)


def _dot_nt(a, b):
    return lax.dot_general(a, b, (((1,), (1,)), ((), ())), preferred_element_type=F32)


def _dot_tn(a, b):
    return lax.dot_general(a, b, (((0,), (0,)), ((), ())), preferred_element_type=F32)


def _sigmoid(x):
    return 0.5 * jnp.tanh(0.5 * x) + 0.5


def _softplus(x):
    return jnp.maximum(x, 0.0) + jnp.log(1.0 + jnp.exp(-jnp.abs(x)))


_GELU_C = math.sqrt(2.0 / math.pi)


def _gelu(x):
    return 0.5 * x * (1.0 + jnp.tanh(_GELU_C * (x + 0.044715 * x * x * x)))


def _gelu_grad(x):
    th = jnp.tanh(_GELU_C * (x + 0.044715 * x * x * x))
    return 0.5 * (1.0 + th) + 0.5 * x * (1.0 - th * th) * _GELU_C * (1.0 + 3.0 * 0.044715 * x * x)


def _params(sem=None):
    return pltpu.CompilerParams(dimension_semantics=sem, vmem_limit_bytes=VMEM_LIMIT)


def _row_spec(tl, width, col=0):
    return pl.BlockSpec((tl, width), lambda i, col=col: (i, col))


def _const_spec(shape):
    nd = len(shape)
    return pl.BlockSpec(shape, lambda *_: (0,) * nd)


def _in_proj_fwd(x, norm_w, w_main, w_dt):
    L = x.shape[0]
    tl = min(TOKEN_TILE, L)

    def body(x_ref, nw_ref, wm_ref, wd_ref, hn_ref, ps5_ref, pssd_ref, pd_ref):
        xv = x_ref[...]
        r = lax.rsqrt(jnp.mean(xv * xv, axis=-1, keepdims=True) + EPS)
        hn = (xv * r * nw_ref[...]).astype(BF)
        hn_ref[...] = hn
        for j in range(2):
            ps5_ref[:, j * 1024:(j + 1) * 1024] = _dot(hn, wm_ref[:, j * 1024:(j + 1) * 1024]).astype(BF)
        for j in range(3):
            pssd_ref[:, j * 1024:(j + 1) * 1024] = _dot(hn, wm_ref[:, (j + 2) * 1024:(j + 3) * 1024])
        pd_ref[...] = _dot(hn, wd_ref[...])

    return pl.pallas_call(
        body, name="in_proj_fwd", grid=(L // tl,),
        in_specs=[_row_spec(tl, D_MODEL), _const_spec((1, D_MODEL)), _const_spec((D_MODEL, D_MAIN)), _const_spec((D_MODEL, LANES))],
        out_specs=[_row_spec(tl, D_MODEL), _row_spec(tl, 2048), _row_spec(tl, 3072), _row_spec(tl, LANES)],
        out_shape=[jax.ShapeDtypeStruct((L, D_MODEL), BF), jax.ShapeDtypeStruct((L, 2048), BF), jax.ShapeDtypeStruct((L, 3072), F32),
                   jax.ShapeDtypeStruct((L, LANES), F32)],
        compiler_params=_params(("arbitrary",)),
    )(x, norm_w, w_main, w_dt)


def _in_proj_bwd(x, norm_w, dh1, du_flat, dyssm, s5_d, dzs, dzd, dxbc, ddt, w_main, w_dt):
    L = x.shape[0]
    tl = min(TOKEN_TILE, L)

    def body(x_ref, nw_ref, dh1_ref, duf_ref, dys_ref, d_ref, dzs_ref, dzd_ref, dxbc_ref, ddt_ref, wm_ref, wd_ref,
             gx_ref, du_ref, gnw_ref):
        @pl.when(pl.program_id(0) == 0)
        def _():
            gnw_ref[...] = jnp.zeros_like(gnw_ref)

        du = (duf_ref[...].astype(F32) + dys_ref[...].astype(F32) * d_ref[...]).astype(BF)
        du_ref[...] = du
        dhn = _dot_nt(du, wm_ref[:, 0:1024])
        dhn += _dot_nt(dzs_ref[...], wm_ref[:, 1024:2048])
        dhn += _dot_nt(dzd_ref[...], wm_ref[:, 2048:3072])
        dhn += _dot_nt(dxbc_ref[...], wm_ref[:, 3072:5120])
        dhn += _dot_nt(ddt_ref[...].astype(BF), wd_ref[...])
        xv = x_ref[...]
        r = lax.rsqrt(jnp.mean(xv * xv, axis=-1, keepdims=True) + EPS)
        xh = xv * r
        gnw_ref[...] += jnp.sum(dhn * xh, axis=0, keepdims=True)
        g = dhn * nw_ref[...]
        gx_ref[...] = dh1_ref[...] + r * (g - xh * jnp.mean(g * xh, axis=-1, keepdims=True))

    return pl.pallas_call(
        body, name="in_proj_bwd", grid=(L // tl,),
        in_specs=[_row_spec(tl, D_MODEL), _const_spec((1, D_MODEL)), _row_spec(tl, D_MODEL), _row_spec(tl, D_MODEL),
                  _row_spec(tl, D_MODEL), _const_spec((1, D_MODEL)), _row_spec(tl, D_MODEL), _row_spec(tl, D_MODEL),
                  _row_spec(tl, 2048), _row_spec(tl, LANES), _const_spec((D_MODEL, D_MAIN)), _const_spec((D_MODEL, LANES))],
        out_specs=[_row_spec(tl, D_MODEL), _row_spec(tl, D_MODEL), _const_spec((1, D_MODEL))],
        out_shape=[jax.ShapeDtypeStruct((L, D_MODEL), F32), jax.ShapeDtypeStruct((L, D_MODEL), BF), jax.ShapeDtypeStruct((1, D_MODEL), F32)],
        compiler_params=_params(("arbitrary",)),
    )(x, norm_w, dh1, du_flat, dyssm, s5_d, dzs, dzd, dxbc, ddt, w_main, w_dt)


def _matmul_tn(a, b, name):
    L, M = a.shape
    N = b.shape[1]
    tm, tn, tk = min(M, 1024), min(N, 1024), min(L, 512)

    def body(a_ref, b_ref, o_ref):
        @pl.when(pl.program_id(2) == 0)
        def _():
            o_ref[...] = jnp.zeros_like(o_ref)

        o_ref[...] += _dot_tn(a_ref[...].astype(BF), b_ref[...].astype(BF))

    return pl.pallas_call(
        body, name=name, grid=(M // tm, N // tn, L // tk),
        in_specs=[pl.BlockSpec((tk, tm), lambda i, j, k: (k, i)), pl.BlockSpec((tk, tn), lambda i, j, k: (k, j))],
        out_specs=pl.BlockSpec((tm, tn), lambda i, j, k: (i, j)),
        out_shape=jax.ShapeDtypeStruct((M, N), F32),
        compiler_params=_params(("parallel", "parallel", "arbitrary")),
    )(a, b)


def _s5_tables(a_re, a_im, log_dt, b_re, b_im, c_re, c_im):
    hi = lax.Precision.HIGHEST
    dt = jnp.exp(log_dt)[:, None]
    lam_re, lam_im = a_re * dt, a_im * dt
    tau = jnp.arange(CHUNK + 1, dtype=F32)
    mag = jnp.exp(lam_re[:, :, None] * tau)
    ang = lam_im[:, :, None] * tau
    pw = lax.complex(mag * jnp.cos(ang), mag * jnp.sin(ang))
    beta = (pw[:, :, 1] - 1.0) / lax.complex(a_re, a_im)
    bb = beta[:, :, None] * lax.complex(b_re, b_im)
    cc = lax.complex(c_re, c_im)
    cp = cc.transpose(0, 2, 1)[:, :, None, :] * pw[:, :, :CHUNK, None]
    cpf = cp.reshape(S5_GROUPS, S5_STATE, CHUNK * S5_CH)
    bbt = bb.transpose(0, 2, 1)
    kmat = (jnp.einsum("ghn,gnj->ghj", bbt.real, cpf.real, precision=hi)
            - jnp.einsum("ghn,gnj->ghj", bbt.imag, cpf.imag, precision=hi))
    w = bbt[:, :, None, :] * pw[:, :, CHUNK - 1::-1][:, :, :CHUNK].transpose(0, 2, 1)[:, None, :, :]
    wst = jnp.concatenate([w.real, w.imag], axis=-1).reshape(S5_GROUPS, S5_CH * CHUNK, 2 * S5_STATE)
    v = cc.transpose(0, 2, 1)[:, :, None, :] * pw[:, :, 1:CHUNK + 1, None]
    woff = jnp.concatenate([v.real, -v.imag], axis=1).reshape(S5_GROUPS, 2 * S5_STATE, CHUNK * S5_CH)
    a64 = jnp.concatenate([pw[:, :, CHUNK].real, pw[:, :, CHUNK].imag], axis=-1)
    return kmat, wst, woff, a64


def _s5_scan_powers(a_re, a_im, log_dt, nsteps):
    dt = jnp.exp(log_dt)[:, None]
    steps = (CHUNK * (2.0 ** jnp.arange(8, dtype=F32)))[None, :, None]
    mag = jnp.exp((a_re * dt)[:, None, :] * steps)
    ang = (a_im * dt)[:, None, :] * steps
    re, im = mag * jnp.cos(ang), mag * jnp.sin(ang)
    del nsteps
    return jnp.concatenate([re, re], -1), jnp.concatenate([-im, im], -1)


def _build_toeplitz(kmat_ref, tg_ref):
    lane = lax.broadcasted_iota(jnp.int32, (CHUNK, CHUNK * S5_CH), 1)
    srow = lax.broadcasted_iota(jnp.int32, (CHUNK, CHUNK * S5_CH), 0)
    keep = lane >= S5_CH * srow
    for h in range(S5_CH):
        row = jnp.broadcast_to(kmat_ref[0, h:h + 1, :], (CHUNK, CHUNK * S5_CH))
        rolled = pltpu.roll(row, 0, 1, stride=S5_CH, stride_axis=0)
        tg_ref[h * CHUNK:(h + 1) * CHUNK, :] = jnp.where(keep, rolled, 0.0).astype(BF)


def _swap_halves(x):
    return pltpu.roll(x, S5_STATE, 1)


def _s5_core_fwd(uflat, kmat, wst, woff, p1, p2):
    G, nc, W = uflat.shape
    nsteps = max(1, (nc - 1).bit_length())

    def body(u_ref, k_ref, wst_ref, woff_ref, p1_ref, p2_ref, y_ref, h_ref, tg_ref):
        _build_toeplitz(k_ref, tg_ref)
        u = u_ref[0]
        x = _dot(u, wst_ref[0])
        row = lax.broadcasted_iota(jnp.int32, x.shape, 0)
        d = 1
        for k in range(nsteps):
            sh = jnp.where(row >= d, pltpu.roll(x, d, 0), 0.0)
            x = x + p1_ref[0, k:k + 1, :] * sh + p2_ref[0, k:k + 1, :] * _swap_halves(sh)
            d *= 2
        h = jnp.where(row >= 1, pltpu.roll(x, 1, 0), 0.0)
        h_ref[0] = h
        y = _dot(u, tg_ref[...]) + _dot(h.astype(BF), woff_ref[0])
        y_ref[0] = y.astype(BF)

    return pl.pallas_call(
        body, name="s5_core_fwd", grid=(G,),
        in_specs=[pl.BlockSpec((1, nc, W), lambda g: (g, 0, 0)), pl.BlockSpec((1, S5_CH, W), lambda g: (g, 0, 0)),
                  pl.BlockSpec((1, W, 2 * S5_STATE), lambda g: (g, 0, 0)), pl.BlockSpec((1, 2 * S5_STATE, W), lambda g: (g, 0, 0)),
                  pl.BlockSpec((1, 8, 2 * S5_STATE), lambda g: (g, 0, 0)), pl.BlockSpec((1, 8, 2 * S5_STATE), lambda g: (g, 0, 0))],
        out_specs=[pl.BlockSpec((1, nc, W), lambda g: (g, 0, 0)), pl.BlockSpec((1, nc, 2 * S5_STATE), lambda g: (g, 0, 0))],
        out_shape=[jax.ShapeDtypeStruct((G, nc, W), BF), jax.ShapeDtypeStruct((G, nc, 2 * S5_STATE), F32)],
        scratch_shapes=[pltpu.VMEM((W, W), BF)],
        compiler_params=_params(("arbitrary",)),
    )(uflat, kmat, wst, woff, p1, p2)


def _s5_core_bwd(uflat, dyflat, hsave, kmat, wst, woff, p1, p2):
    G, nc, W = uflat.shape
    nsteps = max(1, (nc - 1).bit_length())

    def body(u_ref, dy_ref, h_ref, k_ref, wst_ref, woff_ref, p1_ref, p2_ref,
             du_ref, dk_ref, dwst_ref, dwoff_ref, da_ref, tg_ref, flip_ref):
        @pl.when(pl.program_id(0) == 0)
        def _():
            r = lax.broadcasted_iota(jnp.int32, (W, W), 0)
            c = lax.broadcasted_iota(jnp.int32, (W, W), 1)
            flip_ref[...] = (((r >> 6) == (c >> 6)) & ((r & (CHUNK - 1)) + (c & (CHUNK - 1)) == CHUNK - 1)).astype(BF)

        _build_toeplitz(k_ref, tg_ref)
        u = u_ref[0]
        dy = dy_ref[0]
        h = h_ref[0]
        gh = _dot_nt(dy, woff_ref[0])
        row = lax.broadcasted_iota(jnp.int32, gh.shape, 0)
        x = jnp.where(row < nc - 1, pltpu.roll(gh, nc - 1, 0), 0.0)
        d = 1
        for k in range(nsteps):
            sh = jnp.where(row < nc - d, pltpu.roll(x, nc - d, 0), 0.0)
            x = x + p1_ref[0, k:k + 1, :] * sh - p2_ref[0, k:k + 1, :] * _swap_halves(sh)
            d *= 2
        gs = x.astype(BF)
        du_ref[0] = (_dot_nt(dy, tg_ref[...]) + _dot_nt(gs, wst_ref[0])).astype(BF)
        dwst_ref[0] = _dot_tn(u, gs)
        dwoff_ref[0] = _dot_tn(h.astype(BF), dy)
        r1 = jnp.sum(x * h, axis=0, keepdims=True)
        r2 = jnp.sum(x * _swap_halves(h), axis=0, keepdims=True)
        da_ref[0] = jnp.concatenate([r1, r2, jnp.zeros((6, 2 * S5_STATE), F32)], axis=0)
        lane = lax.broadcasted_iota(jnp.int32, (CHUNK, W), 1)
        srow = lax.broadcasted_iota(jnp.int32, (CHUNK, W), 0)
        keep = lane < S5_CH * (srow + 1)
        ur = _dot(u, flip_ref[...]).astype(BF)
        for hh in range(S5_CH):
            dt_h = _dot_tn(ur[:, hh * CHUNK:(hh + 1) * CHUNK], dy)
            back = pltpu.roll(dt_h, S5_CH, 1, stride=S5_CH, stride_axis=0)
            dk_ref[0, hh:hh + 1, :] = jnp.sum(jnp.where(keep, back, 0.0), axis=0, keepdims=True)

    spec_g = lambda a, b: pl.BlockSpec((1, a, b), lambda g: (g, 0, 0))
    return pl.pallas_call(
        body, name="s5_core_bwd", grid=(G,),
        in_specs=[spec_g(nc, W), spec_g(nc, W), spec_g(nc, 2 * S5_STATE), spec_g(S5_CH, W), spec_g(W, 2 * S5_STATE),
                  spec_g(2 * S5_STATE, W), spec_g(8, 2 * S5_STATE), spec_g(8, 2 * S5_STATE)],
        out_specs=[spec_g(nc, W), spec_g(S5_CH, W), spec_g(W, 2 * S5_STATE), spec_g(2 * S5_STATE, W), spec_g(8, 2 * S5_STATE)],
        out_shape=[jax.ShapeDtypeStruct((G, nc, W), BF), jax.ShapeDtypeStruct((G, S5_CH, W), F32),
                   jax.ShapeDtypeStruct((G, W, 2 * S5_STATE), F32), jax.ShapeDtypeStruct((G, 2 * S5_STATE, W), F32),
                   jax.ShapeDtypeStruct((G, 8, 2 * S5_STATE), F32)],
        scratch_shapes=[pltpu.VMEM((W, W), BF), pltpu.VMEM((W, W), BF)],
        compiler_params=_params(("arbitrary",)),
    )(uflat, dyflat, hsave, kmat, wst, woff, p1, p2)


def _flat_hs(a, nc):
    return a.reshape(nc, CHUNK, S5_GROUPS, S5_CH).transpose(2, 0, 3, 1).reshape(S5_GROUPS, nc, CHUNK * S5_CH)


def _unflat_hs(a, nc):
    return a.reshape(S5_GROUPS, nc, S5_CH, CHUNK).transpose(1, 3, 0, 2).reshape(nc * CHUNK, D_MODEL)


def _flat_tk(a, nc):
    return a.reshape(nc, CHUNK, S5_GROUPS, S5_CH).transpose(2, 0, 1, 3).reshape(S5_GROUPS, nc, CHUNK * S5_CH)


def _unflat_tk(a, nc):
    return a.reshape(S5_GROUPS, nc, CHUNK, S5_CH).transpose(1, 2, 0, 3).reshape(nc * CHUNK, D_MODEL)


def _s5_post_fwd(yssm, proj, s5_d, w_glu, b_glu):
    L = yssm.shape[0]
    tl = min(TOKEN_TILE, L)

    def body(ys_ref, u_ref, z_ref, d_ref, wg_ref, bg_ref, o_ref):
        u = u_ref[...].astype(F32)
        a = _gelu(ys_ref[...].astype(F32) + d_ref[...] * u)
        y = a * _sigmoid(_dot(a.astype(BF), wg_ref[...]) + bg_ref[...])
        z = z_ref[...].astype(F32)
        o_ref[...] = (y * z * _sigmoid(z)).astype(BF)

    return pl.pallas_call(
        body, name="s5_post_fwd", grid=(L // tl,),
        in_specs=[_row_spec(tl, D_MODEL), _row_spec(tl, D_MODEL, 0), _row_spec(tl, D_MODEL, 1), _const_spec((1, D_MODEL)),
                  _const_spec((D_MODEL, D_MODEL)), _const_spec((1, D_MODEL))],
        out_specs=_row_spec(tl, D_MODEL),
        out_shape=jax.ShapeDtypeStruct((L, D_MODEL), BF),
        compiler_params=_params(("arbitrary",)),
    )(yssm, proj, proj, s5_d, w_glu, b_glu)


def _s5_post_bwd(dys5, yssm, proj, s5_d, w_glu, b_glu):
    L = yssm.shape[0]
    tl = min(TOKEN_TILE, L)

    def body(dy_ref, ys_ref, u_ref, z_ref, d_ref, wg_ref, bg_ref, dz_ref, dys_ref, a_ref, dgl_ref, dbg_ref, dd_ref):
        @pl.when(pl.program_id(0) == 0)
        def _():
            dbg_ref[...] = jnp.zeros_like(dbg_ref)
            dd_ref[...] = jnp.zeros_like(dd_ref)

        u = u_ref[...].astype(F32)
        y0 = ys_ref[...].astype(F32) + d_ref[...] * u
        a = _gelu(y0)
        a_bf = a.astype(BF)
        sg = _sigmoid(_dot(a_bf, wg_ref[...]) + bg_ref[...])
        y = a * sg
        z = z_ref[...].astype(F32)
        sz = _sigmoid(z)
        dout = dy_ref[...].astype(F32)
        dz_ref[...] = (dout * y * sz * (1.0 + z * (1.0 - sz))).astype(BF)
        dyv = dout * z * sz
        dgl = dyv * a * sg * (1.0 - sg)
        dgl_bf = dgl.astype(BF)
        da = dyv * sg + _dot_nt(dgl_bf, wg_ref[...])
        dy0 = da * _gelu_grad(y0)
        dbg_ref[...] += jnp.sum(dgl, axis=0, keepdims=True)
        dd_ref[...] += jnp.sum(dy0 * u, axis=0, keepdims=True)
        dys_ref[...] = dy0.astype(BF)
        a_ref[...] = a_bf
        dgl_ref[...] = dgl_bf

    big = jax.ShapeDtypeStruct((L, D_MODEL), BF)
    vec = jax.ShapeDtypeStruct((1, D_MODEL), F32)
    return pl.pallas_call(
        body, name="s5_post_bwd", grid=(L // tl,),
        in_specs=[_row_spec(tl, D_MODEL), _row_spec(tl, D_MODEL), _row_spec(tl, D_MODEL, 0), _row_spec(tl, D_MODEL, 1),
                  _const_spec((1, D_MODEL)), _const_spec((D_MODEL, D_MODEL)), _const_spec((1, D_MODEL))],
        out_specs=[_row_spec(tl, D_MODEL)] * 4 + [_const_spec((1, D_MODEL))] * 2,
        out_shape=[big, big, big, big, vec, vec],
        compiler_params=_params(("arbitrary",)),
    )(dys5, yssm, proj, proj, s5_d, w_glu, b_glu)


def _cumsum_rows(a):
    row = lax.broadcasted_iota(jnp.int32, a.shape, 0)
    d = 1
    while d < a.shape[0]:
        a = a + jnp.where(row >= d, pltpu.roll(a, d, 0), 0.0)
        d *= 2
    return a


def _rev_cumsum_rows(a):
    n = a.shape[0]
    row = lax.broadcasted_iota(jnp.int32, a.shape, 0)
    d = 1
    while d < n:
        a = a + jnp.where(row < n - d, pltpu.roll(a, n - d, 0), 0.0)
        d *= 2
    return a


def _ssd_conv_fwd(first, xs_ref, bc_ref, hx_ref, hb_ref, cw_ref, cb_ref, xp_ref, tl):
    hal = jnp.concatenate([hx_ref[...], hb_ref[...]], axis=1)
    xp_ref[0:8, :] = jnp.where(first, 0.0, hal)
    xp_ref[8:8 + tl, 0:1024] = xs_ref[...]
    xp_ref[8:8 + tl, 1024:2048] = bc_ref[...]
    pre = cb_ref[...] + cw_ref[0:1, :] * xp_ref[5:5 + tl, :]
    for k in range(1, 4):
        pre = pre + cw_ref[k:k + 1, :] * xp_ref[5 + k:5 + k + tl, :]
    return pre


def _onehot_lane(h):
    return (lax.broadcasted_iota(jnp.int32, (1, LANES), 1) == h).astype(F32)


def _dot_exact(x, e):
    hi = x.astype(BF)
    r = x - hi.astype(F32)
    mid = r.astype(BF)
    lo = (r - mid.astype(F32)).astype(BF)
    return _dot(hi, e) + _dot(mid, e) + _dot(lo, e)


def _head_expand_matrices():
    e = lax.broadcasted_iota(jnp.int32, (LANES, D_MODEL), 0) == (lax.broadcasted_iota(jnp.int32, (LANES, D_MODEL), 1) >> 6)
    et = (lax.broadcasted_iota(jnp.int32, (D_MODEL, LANES), 0) >> 6) == lax.broadcasted_iota(jnp.int32, (D_MODEL, LANES), 1)
    return e.astype(BF), et.astype(BF)


def _group_masks():
    r64 = lax.broadcasted_iota(jnp.int32, (4 * CHUNK, CHUNK), 0)
    causal4 = (r64 & (CHUNK - 1)) >= lax.broadcasted_iota(jnp.int32, (4 * CHUNK, CHUNK), 1)
    r256 = lax.broadcasted_iota(jnp.int32, (4 * CHUNK, 4 * SSD_HEAD_DIM), 0)
    same = (r256 >> 6) == (lax.broadcasted_iota(jnp.int32, (4 * CHUNK, 4 * SSD_HEAD_DIM), 1) >> 6)
    return causal4, same


def _group_decay(acs, acs_t, j, causal4):
    col = jnp.concatenate([acs[:, 4 * j + hh:4 * j + hh + 1] for hh in range(4)], axis=0)
    rowv = jnp.concatenate([jnp.broadcast_to(acs_t[4 * j + hh:4 * j + hh + 1, :], (CHUNK, CHUNK)) for hh in range(4)], axis=0)
    return jnp.where(causal4, jnp.exp(col - rowv), 0.0)


def _group_last_decay(acs_t, j):
    return jnp.concatenate([jnp.broadcast_to(jnp.exp(acs_t[4 * j + hh:4 * j + hh + 1, CHUNK - 1:CHUNK]), (SSD_HEAD_DIM, 1))
                            for hh in range(4)], axis=0)


def _fold_heads(r):
    return r[0:CHUNK] + r[CHUNK:2 * CHUNK] + r[2 * CHUNK:3 * CHUNK] + r[3 * CHUNK:4 * CHUNK]


def _ssd_specs_in(tl, nt, rev):
    t_of = (lambda i: nt - 1 - i) if rev else (lambda i: i)
    rows = lambda w, col: pl.BlockSpec((tl, w), lambda i: (t_of(i), col))
    halo = lambda col: pl.BlockSpec((8, 1024), lambda i: (jnp.maximum(t_of(i) * (tl // 8) - 1, 0), col))
    return t_of, rows, halo


def _ssd_fwd(proj, pdt, conv_w, conv_b, dt_bias, a_log, ssd_d, norm_w):
    L = proj.shape[0]
    tl = min(TOKEN_TILE, L)
    nt, ncl = L // tl, tl // CHUNK
    _, rows, halo = _ssd_specs_in(tl, nt, False)

    def body(xs_ref, bc_ref, hx_ref, hb_ref, dt_ref, z_ref, cw_ref, cb_ref, dtb_ref, al_ref, dd_ref, nw_ref,
             y_ref, ypre_ref, st_ref, xp_ref, xbc_ref, dts_ref, hst_ref):
        i = pl.program_id(0)

        @pl.when(i == 0)
        def _():
            hst_ref[...] = jnp.zeros_like(hst_ref)

        pre = _ssd_conv_fwd(i == 0, xs_ref, bc_ref, hx_ref, hb_ref, cw_ref, cb_ref, xp_ref, tl)
        xbc_ref[...] = pre * _sigmoid(pre)
        dts_ref[...] = _softplus(dt_ref[...] + dtb_ref[...])
        a_neg = -jnp.exp(al_ref[...])
        e16, _ = _head_expand_matrices()
        causal4, same = _group_masks()
        dd_x = _dot_exact(jnp.broadcast_to(dd_ref[...], (8, LANES)), e16)[0:1, :]

        def chunk(c, carry):
            r0 = pl.multiple_of(c * CHUNK, CHUNK)
            xbc = xbc_ref[pl.ds(r0, CHUNK), :]
            dtc = dts_ref[pl.ds(r0, CHUNK), :]
            acs = _cumsum_rows(dtc * a_neg)
            acs_t = acs.T
            acs_x = _dot_exact(acs, e16)
            xs = xbc[:, 0:1024]
            xd = xs * _dot_exact(dtc, e16)
            xd_bf = xd.astype(BF)
            xdd = (xd * jnp.exp(acs_x[CHUNK - 1:CHUNK, :] - acs_x)).astype(BF)
            e_x = jnp.exp(acs_x)
            for j in range(SSD_GROUPS):
                sl = slice(256 * j, 256 * (j + 1))
                bj = xbc[:, 1024 + 128 * j:1024 + 128 * (j + 1)].astype(BF)
                cj = xbc[:, 1536 + 128 * j:1536 + 128 * (j + 1)].astype(BF)
                g = _dot_nt(cj, bj)
                hj = hst_ref[sl, :]
                zj = _dot_nt(cj, hj.astype(BF))
                sc = (jnp.concatenate([g] * 4, axis=0) * _group_decay(acs, acs_t, j, causal4)).astype(BF)
                yd = _fold_heads(jnp.where(same, _dot(sc, xd_bf[:, sl]), 0.0))
                ypre_ref[pl.ds(r0, CHUNK), sl] = yd + e_x[:, sl] * zj + dd_x[:, sl] * xs[:, sl]
                st_ref[c, sl, :] = hj
                hst_ref[sl, :] = _group_last_decay(acs_t, j) * hj + _dot_tn(xdd[:, sl], bj)
            return carry

        lax.fori_loop(0, ncl, chunk, 0)
        z = z_ref[...]
        gg = ypre_ref[...] * z * _sigmoid(z)
        for j in range(SSD_GROUPS):
            seg = gg[:, 256 * j:256 * (j + 1)]
            r = lax.rsqrt(jnp.mean(seg * seg, axis=-1, keepdims=True) + EPS)
            y_ref[:, 256 * j:256 * (j + 1)] = (seg * r * nw_ref[:, 256 * j:256 * (j + 1)]).astype(BF)

    nc = L // CHUNK
    return pl.pallas_call(
        body, name="ssd_fwd", grid=(nt,),
        in_specs=[rows(1024, 1), rows(1024, 2), halo(1), halo(2), rows(LANES, 0), rows(1024, 0),
                  _const_spec((4, 2048)), _const_spec((1, 2048)), _const_spec((1, LANES)), _const_spec((1, LANES)),
                  _const_spec((1, LANES)), _const_spec((1, D_MODEL))],
        out_specs=[_row_spec(tl, D_MODEL), _row_spec(tl, D_MODEL), pl.BlockSpec((ncl, 1024, SSD_STATE), lambda i: (i, 0, 0))],
        out_shape=[jax.ShapeDtypeStruct((L, D_MODEL), BF), jax.ShapeDtypeStruct((L, D_MODEL), F32),
                   jax.ShapeDtypeStruct((nc, 1024, SSD_STATE), F32)],
        scratch_shapes=[pltpu.VMEM((tl + 8, 2048), F32), pltpu.VMEM((tl, 2048), F32), pltpu.VMEM((tl, LANES), F32),
                        pltpu.VMEM((1024, SSD_STATE), F32)],
        compiler_params=_params(("arbitrary",)),
    )(proj, proj, proj, proj, pdt, proj, conv_w, conv_b, dt_bias, a_log, ssd_d, norm_w)


def _ssd_bwd(dyssd, ypre, proj, pdt, states, conv_w, conv_b, dt_bias, a_log, ssd_d, norm_w):
    L = proj.shape[0]
    tl = min(TOKEN_TILE, L)
    nt, ncl = L // tl, tl // CHUNK
    t_of, rows, halo = _ssd_specs_in(tl, nt, True)

    def body(dy_ref, ypre_ref, z_ref, xs_ref, bc_ref, hx_ref, hb_ref, dt_ref, st_ref, cw_ref, cb_ref, dtb_ref, al_ref,
             dd_ref, nw_ref,
             dxbc_ref, ddt_ref, dz_ref, dcw_ref, dcb_ref, ddtb_ref, dal_ref, ddd_ref, dnw_ref,
             xp_ref, xbc_ref, pre_ref, dts_ref, dyp_ref, dxs_ref, ddts_ref, dp_ref, dh_ref):
        i = pl.program_id(0)

        @pl.when(i == 0)
        def _():
            for r in (dcw_ref, dcb_ref, ddtb_ref, dal_ref, ddd_ref, dnw_ref, dh_ref):
                r[...] = jnp.zeros_like(r)
            dp_ref[tl:tl + 8, :] = jnp.zeros((8, 2048), F32)

        pre = _ssd_conv_fwd(t_of(i) == 0, xs_ref, bc_ref, hx_ref, hb_ref, cw_ref, cb_ref, xp_ref, tl)
        pre_ref[...] = pre
        xbc_ref[...] = pre * _sigmoid(pre)
        dts_ref[...] = _softplus(dt_ref[...] + dtb_ref[...])
        a_neg = -jnp.exp(al_ref[...])

        ypre = ypre_ref[...]
        z = z_ref[...]
        sz = _sigmoid(z)
        gg = ypre * z * sz
        dout = dy_ref[...]
        for j in range(SSD_GROUPS):
            sl = slice(256 * j, 256 * (j + 1))
            seg = gg[:, sl]
            r = lax.rsqrt(jnp.mean(seg * seg, axis=-1, keepdims=True) + EPS)
            gh = seg * r
            dnw_ref[:, sl] += jnp.sum(dout[:, sl] * gh, axis=0, keepdims=True)
            gw = dout[:, sl] * nw_ref[:, sl]
            dgg = r * (gw - gh * jnp.mean(gw * gh, axis=-1, keepdims=True))
            dyp_ref[:, sl] = dgg * z[:, sl] * sz[:, sl]
            dz_ref[:, sl] = (dgg * ypre[:, sl] * sz[:, sl] * (1.0 + z[:, sl] * (1.0 - sz[:, sl]))).astype(BF)

        e16, e16t = _head_expand_matrices()
        causal4, same = _group_masks()
        dd_x = _dot_exact(jnp.broadcast_to(dd_ref[...], (8, LANES)), e16)[0:1, :]
        last_row = (lax.broadcasted_iota(jnp.int32, (CHUNK, 1), 0) == CHUNK - 1).astype(F32)
        sel_rows = lax.broadcasted_iota(jnp.int32, (4 * CHUNK, LANES), 0) >> 6
        sel_lanes = lax.broadcasted_iota(jnp.int32, (4 * CHUNK, LANES), 1)

        def chunk(k, carry):
            dal_acc, ddx_acc = carry
            c = ncl - 1 - k
            r0 = pl.multiple_of(c * CHUNK, CHUNK)
            xbc = xbc_ref[pl.ds(r0, CHUNK), :]
            dtc = dts_ref[pl.ds(r0, CHUNK), :]
            dyp = dyp_ref[pl.ds(r0, CHUNK), :]
            acs = _cumsum_rows(dtc * a_neg)
            acs_t = acs.T
            acs_x = _dot_exact(acs, e16)
            dt_x = _dot_exact(dtc, e16)
            xs = xbc[:, 0:1024]
            xd = xs * dt_x
            xd_bf = xd.astype(BF)
            dec_x = jnp.exp(acs_x[CHUNK - 1:CHUNK, :] - acs_x)
            xdd = xd * dec_x
            xdd_bf = xdd.astype(BF)
            dz = dyp * jnp.exp(acs_x)
            dz_bf = dz.astype(BF)
            ddx_acc = ddx_acc + jnp.sum(dyp * xs, axis=0, keepdims=True)
            dacs = jnp.zeros((CHUNK, LANES), F32)
            hsum = jnp.zeros((1, LANES), F32)
            p1_l, p2_l, p3_l = [], [], []
            for j in range(SSD_GROUPS):
                sl = slice(256 * j, 256 * (j + 1))
                bj = xbc[:, 1024 + 128 * j:1024 + 128 * (j + 1)].astype(BF)
                cj = xbc[:, 1536 + 128 * j:1536 + 128 * (j + 1)].astype(BF)
                g = _dot_nt(cj, bj)
                hj = st_ref[c, sl, :]
                hj_bf = hj.astype(BF)
                dhj = dh_ref[sl, :]
                dhj_bf = dhj.astype(BF)
                zj = _dot_nt(cj, hj_bf)
                qj = _dot_nt(bj, dhj_bf)
                lm = _group_decay(acs, acs_t, j, causal4)
                sc = jnp.concatenate([g] * 4, axis=0) * lm
                sc_bf = sc.astype(BF)
                dym = jnp.where(same, jnp.concatenate([dyp[:, sl]] * 4, axis=0), 0.0).astype(BF)
                dsc = _dot_nt(dym, xd_bf[:, sl])
                dxd = _dot_tn(sc_bf, dym) + qj * dec_x[:, sl]
                m = dsc * sc
                dg_bf = _fold_heads(dsc * lm).astype(BF)
                rs = jnp.sum(m, axis=1, keepdims=True)
                e2 = dhj * hj
                for hh in range(4):
                    oh = _onehot_lane(4 * j + hh)
                    dacs = dacs + oh * rs[CHUNK * hh:CHUNK * (hh + 1)]
                    hsum = hsum + oh * jnp.sum(jnp.sum(e2[64 * hh:64 * (hh + 1)], axis=0, keepdims=True), axis=1, keepdims=True)
                sel = (sel_rows + 4 * j == sel_lanes).astype(BF)
                hi = m.astype(BF)
                rem = m - hi.astype(F32)
                mid = rem.astype(BF)
                lo = (rem - mid.astype(F32)).astype(BF)
                dacs = dacs - (_dot_tn(hi, sel) + _dot_tn(mid, sel) + _dot_tn(lo, sel))
                p1_l.append(dz[:, sl] * zj)
                p2_l.append(qj * xdd[:, sl])
                p3_l.append(dxd * xs[:, sl])
                dxs_ref[pl.ds(r0, CHUNK), sl] = dd_x[:, sl] * dyp[:, sl] + dxd * dt_x[:, sl]
                dxs_ref[pl.ds(r0, CHUNK), 1536 + 128 * j:1536 + 128 * (j + 1)] = _dot(dg_bf, bj) + _dot(dz_bf[:, sl], hj_bf)
                dxs_ref[pl.ds(r0, CHUNK), 1024 + 128 * j:1024 + 128 * (j + 1)] = _dot_tn(dg_bf, cj) + _dot(xdd_bf[:, sl], dhj_bf)
                dh_ref[sl, :] = _group_last_decay(acs_t, j) * dhj + _dot_tn(dz_bf[:, sl], cj)
            stacked = jnp.concatenate([jnp.concatenate(p1_l, axis=1), jnp.concatenate(p2_l, axis=1), jnp.concatenate(p3_l, axis=1)], axis=0)
            red = _dot_exact(stacked, e16t)
            r1, r2, ddtc = red[0:CHUNK], red[CHUNK:2 * CHUNK], red[2 * CHUNK:3 * CHUNK]
            tot = jnp.sum(r2, axis=0, keepdims=True) + jnp.exp(acs[CHUNK - 1:CHUNK, :]) * hsum
            da = _rev_cumsum_rows(dacs + r1 - r2 + last_row * tot)
            ddts_ref[pl.ds(r0, CHUNK), :] = ddtc + da * a_neg
            dal_acc = dal_acc + jnp.sum(da * dtc, axis=0, keepdims=True)
            return dal_acc, ddx_acc

        dal_acc, ddx_acc = lax.fori_loop(0, ncl, chunk, (jnp.zeros((1, LANES), F32), jnp.zeros((1, D_MODEL), F32)))
        dal_ref[...] += dal_acc * a_neg
        ddd_ref[...] += _dot_exact(jnp.broadcast_to(ddx_acc, (8, D_MODEL)), e16t)[0:1, :]
        ddt_raw = ddts_ref[...] * _sigmoid(dt_ref[...] + dtb_ref[...])
        ddt_ref[...] = ddt_raw
        ddtb_ref[...] += jnp.sum(ddt_raw, axis=0, keepdims=True)

        pre = pre_ref[...]
        sp = _sigmoid(pre)
        dpre = dxs_ref[...] * sp * (1.0 + pre * (1.0 - sp))
        dp_ref[0:tl, :] = dpre
        dcb_ref[...] += jnp.sum(dpre, axis=0, keepdims=True)
        dx = jnp.zeros((tl, 2048), F32)
        for k in range(4):
            dcw_ref[k:k + 1, :] += jnp.sum(dpre * xp_ref[5 + k:5 + k + tl, :], axis=0, keepdims=True)
            dx = dx + cw_ref[k:k + 1, :] * dp_ref[3 - k:3 - k + tl, :]
        dxbc_ref[...] = dx.astype(BF)
        dp_ref[tl:tl + 8, :] = dp_ref[0:8, :]

    vec = lambda w: jax.ShapeDtypeStruct((1, w), F32)
    rrow = lambda w: pl.BlockSpec((tl, w), lambda i: (t_of(i), 0))
    return pl.pallas_call(
        body, name="ssd_bwd", grid=(nt,),
        in_specs=[rrow(D_MODEL), rrow(D_MODEL), rows(1024, 0), rows(1024, 1), rows(1024, 2), halo(1), halo(2), rows(LANES, 0),
                  pl.BlockSpec((ncl, 1024, SSD_STATE), lambda i: (t_of(i), 0, 0)),
                  _const_spec((4, 2048)), _const_spec((1, 2048)), _const_spec((1, LANES)), _const_spec((1, LANES)),
                  _const_spec((1, LANES)), _const_spec((1, D_MODEL))],
        out_specs=[rrow(2048), rrow(LANES), rrow(D_MODEL), _const_spec((8, 2048)), _const_spec((1, 2048)),
                   _const_spec((1, LANES)), _const_spec((1, LANES)), _const_spec((1, LANES)), _const_spec((1, D_MODEL))],
        out_shape=[jax.ShapeDtypeStruct((L, 2048), BF), jax.ShapeDtypeStruct((L, LANES), F32), jax.ShapeDtypeStruct((L, D_MODEL), BF),
                   jax.ShapeDtypeStruct((8, 2048), F32), vec(2048), vec(LANES), vec(LANES), vec(LANES), vec(D_MODEL)],
        scratch_shapes=[pltpu.VMEM((tl + 8, 2048), F32), pltpu.VMEM((tl, 2048), F32), pltpu.VMEM((tl, 2048), F32),
                        pltpu.VMEM((tl, LANES), F32), pltpu.VMEM((tl, D_MODEL), F32), pltpu.VMEM((tl, 2048), F32),
                        pltpu.VMEM((tl, LANES), F32), pltpu.VMEM((tl + 8, 2048), F32), pltpu.VMEM((1024, SSD_STATE), F32)],
        compiler_params=_params(("arbitrary",)),
    )(dyssd, ypre, proj, proj, proj, proj, proj, pdt, states, conv_w, conv_b, dt_bias, a_log, ssd_d, norm_w)


def _head_fwd_bwd(x, ys5, yssd, p, target, w_out, w_gate, w_proj, ple_nw, fin_nw):
    L = x.shape[0]
    tl = min(TOKEN_TILE, L)
    inv_d = 1.0 / D_MODEL

    def body(x_ref, ys_ref, yd_ref, p_ref, t_ref, wo_ref, wg_ref, wp_ref, pnw_ref, fnw_ref,
             loss_ref, dys_ref, dyd_ref, dh1_ref, n2_ref, dgl_ref, dpp_ref, dpnw_ref, dfnw_ref):
        @pl.when(pl.program_id(0) == 0)
        def _():
            loss_ref[...] = jnp.zeros_like(loss_ref)
            dpnw_ref[...] = jnp.zeros_like(dpnw_ref)
            dfnw_ref[...] = jnp.zeros_like(dfnw_ref)

        h1 = x_ref[...] + _dot(ys_ref[...], wo_ref[0:1024, :]) + _dot(yd_ref[...], wo_ref[1024:2048, :])
        r1 = lax.rsqrt(jnp.mean(h1 * h1, axis=-1, keepdims=True) + EPS)
        hh1 = h1 * r1
        n2 = (hh1 * pnw_ref[...]).astype(BF)
        gate = _sigmoid(_dot(n2, wg_ref[...]))
        pp = _dot(p_ref[...].astype(BF), wp_ref[...])
        h2 = h1 + pp * gate
        r2 = lax.rsqrt(jnp.mean(h2 * h2, axis=-1, keepdims=True) + EPS)
        hh2 = h2 * r2
        err = hh2 * fnw_ref[...] - t_ref[...]
        loss_ref[...] += 0.5 * inv_d * jnp.sum(err * err)
        dyo = err * inv_d
        dfnw_ref[...] += jnp.sum(dyo * hh2, axis=0, keepdims=True)
        g2 = dyo * fnw_ref[...]
        dh2 = r2 * (g2 - hh2 * jnp.mean(g2 * hh2, axis=-1, keepdims=True))
        dpp_ref[...] = (dh2 * gate).astype(BF)
        dgl = (dh2 * pp * gate * (1.0 - gate)).astype(BF)
        dgl_ref[...] = dgl
        n2_ref[...] = n2
        dn2 = _dot_nt(dgl, wg_ref[...])
        dpnw_ref[...] += jnp.sum(dn2 * hh1, axis=0, keepdims=True)
        g1 = dn2 * pnw_ref[...]
        dh1 = dh2 + r1 * (g1 - hh1 * jnp.mean(g1 * hh1, axis=-1, keepdims=True))
        dh1_ref[...] = dh1
        dh1_bf = dh1.astype(BF)
        dys_ref[...] = _dot_nt(dh1_bf, wo_ref[0:1024, :]).astype(BF)
        dyd_ref[...] = _dot_nt(dh1_bf, wo_ref[1024:2048, :])

    big = jax.ShapeDtypeStruct((L, D_MODEL), BF)
    vec = jax.ShapeDtypeStruct((1, D_MODEL), F32)
    return pl.pallas_call(
        body, name="head_fwd_bwd", grid=(L // tl,),
        in_specs=[_row_spec(tl, D_MODEL), _row_spec(tl, D_MODEL), _row_spec(tl, D_MODEL), _row_spec(tl, 256), _row_spec(tl, D_MODEL),
                  _const_spec((2048, D_MODEL)), _const_spec((D_MODEL, D_MODEL)), _const_spec((256, D_MODEL)),
                  _const_spec((1, D_MODEL)), _const_spec((1, D_MODEL))],
        out_specs=[_const_spec((8, LANES)), _row_spec(tl, D_MODEL), _row_spec(tl, D_MODEL), _row_spec(tl, D_MODEL),
                   _row_spec(tl, D_MODEL), _row_spec(tl, D_MODEL), _row_spec(tl, D_MODEL), _const_spec((1, D_MODEL)), _const_spec((1, D_MODEL))],
        out_shape=[jax.ShapeDtypeStruct((8, LANES), F32), big, jax.ShapeDtypeStruct((L, D_MODEL), F32),
                   jax.ShapeDtypeStruct((L, D_MODEL), F32), big, big, big, vec, vec],
        compiler_params=_params(("arbitrary",)),
    )(x, ys5, yssd, p, target, w_out, w_gate, w_proj, ple_nw, fin_nw)


def _pad_lanes(v):
    return jnp.pad(v.reshape(1, -1), ((0, 0), (0, LANES - v.size)))


def _local_step(x, p, target, w):
    L = x.shape[0]
    nc = L // CHUNK
    nsteps = max(1, (nc - 1).bit_length())
    w_in = w["w_in"]
    w_main = w_in[:, :D_MAIN]
    w_dt = jnp.pad(w_in[:, D_MAIN:], ((0, 0), (0, LANES - SSD_HEADS)))
    norm_w = w["norm_w"].reshape(1, -1)
    s5_d = w["s5_D"].reshape(1, -1)
    b_glu = w["s5_b_glu"].reshape(1, -1)
    conv_b = w["conv_b"].reshape(1, -1)
    dtb, alog, ssd_d = _pad_lanes(w["dt_bias"]), _pad_lanes(w["A_log"]), _pad_lanes(w["ssd_D"])
    ssd_nw = w["ssd_norm_w"].reshape(1, -1)
    ple_nw = w["ple_norm_w"].reshape(1, -1)
    fin_nw = w["final_norm_w"].reshape(1, -1)

    s5_args = (w["s5_A_re"], w["s5_A_im"], w["s5_log_dt"], w["s5_B_re"], w["s5_B_im"], w["s5_C_re"], w["s5_C_im"])
    (kmat, wst, woff, _a64), tables_vjp = jax.vjp(_s5_tables, *s5_args)
    p1, p2 = _s5_scan_powers(w["s5_A_re"], w["s5_A_im"], w["s5_log_dt"], nsteps)
    wst_bf, woff_bf = wst.astype(BF), woff.astype(BF)

    hn, proj, pssd, pdt = _in_proj_fwd(x, norm_w, w_main, w_dt)
    uflat = _flat_hs(proj[:, :D_MODEL], nc)
    yflat, hsave = _s5_core_fwd(uflat, kmat, wst_bf, woff_bf, p1, p2)
    yssm = _unflat_tk(yflat, nc)
    ys5 = _s5_post_fwd(yssm, proj, s5_d, w["s5_w_glu"], b_glu)
    yssd, ypre, states = _ssd_fwd(pssd, pdt, w["conv_w"], conv_b, dtb, alog, ssd_d, ssd_nw)
    (loss8, dys5, dyssd, dh1, n2, dgl2, dpp, g_ple_nw, g_fin_nw) = _head_fwd_bwd(
        x, ys5, yssd, p, target, w["w_out"], w["w_ple_gate"], w["w_ple_proj"], ple_nw, fin_nw)

    (dxbc, ddt, dzd, g_cw, g_cb, g_dtb, g_alog, g_ssd_d, g_ssd_nw) = _ssd_bwd(
        dyssd, ypre, pssd, pdt, states, w["conv_w"], conv_b, dtb, alog, ssd_d, ssd_nw)
    dzs, dyssm, a_glu, dgl1, g_bglu, g_s5d = _s5_post_bwd(dys5, yssm, proj, s5_d, w["s5_w_glu"], b_glu)
    duflat, dkmat, dwst, dwoff, da8 = _s5_core_bwd(uflat, _flat_tk(dyssm, nc), hsave, kmat, wst_bf, woff_bf, p1, p2)
    da64 = jnp.concatenate([da8[:, 0, :S5_STATE] + da8[:, 0, S5_STATE:], da8[:, 1, S5_STATE:] - da8[:, 1, :S5_STATE]], axis=-1)
    g_s5 = tables_vjp((dkmat, dwst, dwoff, da64))
    gx, du, g_norm_w = _in_proj_bwd(x, norm_w, dh1, _unflat_hs(duflat, nc), dyssm, s5_d, dzs, dzd, dxbc, ddt, w_main, w_dt)

    g_w_in = jnp.concatenate([
        _matmul_tn(hn, du, "dw_in_u"), _matmul_tn(hn, dzs, "dw_in_zs"), _matmul_tn(hn, dzd, "dw_in_zd"),
        _matmul_tn(hn, dxbc, "dw_in_xbc"), _matmul_tn(hn, ddt, "dw_in_dt")[:, :SSD_HEADS]], axis=1)
    grads = {
        "norm_w": g_norm_w, "w_in": g_w_in,
        "s5_A_re": g_s5[0], "s5_A_im": g_s5[1], "s5_log_dt": g_s5[2], "s5_B_re": g_s5[3], "s5_B_im": g_s5[4],
        "s5_C_re": g_s5[5], "s5_C_im": g_s5[6], "s5_D": g_s5d, "s5_w_glu": _matmul_tn(a_glu, dgl1, "dw_glu"), "s5_b_glu": g_bglu,
        "conv_w": g_cw[:4], "conv_b": g_cb, "dt_bias": g_dtb[:, :SSD_HEADS], "A_log": g_alog[:, :SSD_HEADS],
        "ssd_D": g_ssd_d[:, :SSD_HEADS], "ssd_norm_w": g_ssd_nw,
        "w_out": jnp.concatenate([_matmul_tn(ys5, dh1, "dw_out_s5"), _matmul_tn(yssd, dh1, "dw_out_ssd")], axis=0),
        "ple_norm_w": g_ple_nw, "w_ple_gate": _matmul_tn(n2, dgl2, "dw_gate"), "w_ple_proj": _matmul_tn(p, dpp, "dw_proj"),
        "final_norm_w": g_fin_nw,
    }
    return loss8[0, 0], gx, grads


WEIGHTS = ("norm_w", "w_in", "s5_A_re", "s5_A_im", "s5_log_dt", "s5_B_re", "s5_B_im", "s5_C_re", "s5_C_im", "s5_D", "s5_w_glu",
           "s5_b_glu", "conv_w", "conv_b", "dt_bias", "A_log", "ssd_D", "ssd_norm_w", "w_out", "ple_norm_w", "w_ple_gate",
           "w_ple_proj", "final_norm_w")
BIG = {"w_in": ((1024, 1284), 1), "s5_w_glu": ((256, 1024), 0), "w_out": ((512, 1024), 0), "w_ple_gate": ((256, 1024), 0),
       "w_ple_proj": ((256, 256), 1)}
SMALL = {"norm_w": (1024,), "s5_A_re": (64, 64), "s5_A_im": (64, 64), "s5_log_dt": (64,), "s5_B_re": (64, 64, 16),
         "s5_B_im": (64, 64, 16), "s5_C_re": (64, 16, 64), "s5_C_im": (64, 16, 64), "s5_D": (1024,), "s5_b_glu": (1024,),
         "conv_w": (4, 2048), "conv_b": (2048,), "dt_bias": (16,), "A_log": (16,), "ssd_D": (16,), "ssd_norm_w": (1024,),
         "ple_norm_w": (1024,), "final_norm_w": (1024,)}
BIG_ROWS = {n: s[0] * s[1] // LANES for n, (s, _) in BIG.items()}
BIG_ROWS_TOTAL = sum(BIG_ROWS.values())
SMALL_TOTAL = sum(math.prod(s) for s in SMALL.values())
SMALL_PIECE_ROWS = -(-SMALL_TOTAL // (N_CHIPS * 16 * LANES)) * 16
HALF_ROWS = (BIG_ROWS_TOTAL + SMALL_PIECE_ROWS) // 2
SMALL_ROW0 = BIG_ROWS_TOTAL - HALF_ROWS


def _mesh_pos():
    return lax.axis_index("x"), lax.axis_index("y"), lax.axis_index("c")


def _other_chips(x, y):
    return [(1 - x, y), (x, 1 - y), (1 - x, 1 - y)]


def _comm_params():
    return pltpu.CompilerParams(has_side_effects=True)


def _all_gather_chips(wpack, cw):
    half = wpack.shape[0] // 2

    def body(w_ref, c_ref, wo_ref, co_ref, send_sems, recv_sems, fwd_send, fwd_recv, loc_sems):
        x, y, c = _mesh_pos()
        me = 2 * x + y
        sib = (x, y, 1 - c)
        mine = pl.ds(c * half, half)
        theirs = pl.ds((1 - c) * half, half)
        others = _other_chips(x, y)
        loc = [pltpu.make_async_copy(c_ref, co_ref.at[me], loc_sems.at[0])]
        for cp in loc:
            cp.start()

        def from_chip(k, chip, dev):
            return pltpu.make_async_remote_copy(w_ref.at[mine], wo_ref.at[chip, mine], send_sems.at[2 * k], recv_sems.at[2 * k],
                                                device_id=dev, device_id_type=MESH)

        def conv_from(k, chip, dev):
            return pltpu.make_async_remote_copy(c_ref, co_ref.at[chip], send_sems.at[2 * k + 1], recv_sems.at[2 * k + 1],
                                                device_id=dev, device_id_type=MESH)

        def passed(k, chip, rows):
            return pltpu.make_async_remote_copy(wo_ref.at[chip, rows], wo_ref.at[chip, rows], fwd_send.at[k], fwd_recv.at[k],
                                                device_id=sib, device_id_type=MESH)

        sends = []
        for k, (px, py) in enumerate(others):
            sends += [from_chip(k, me, (px, py, c)), conv_from(k, me, (px, py, c))]
        for cp in sends:
            cp.start()
        fwds = []
        for k, (px, py) in enumerate(others):
            chip = 2 * px + py
            from_chip(k, chip, (px, py, c)).wait_recv()
            fwds.append(passed(k, chip, mine))
            fwds[-1].start()
        for k, (px, py) in enumerate(others):
            chip = 2 * px + py
            passed(k, chip, theirs).wait_recv()
            conv_from(k, chip, (px, py, c)).wait_recv()
        for cp in sends + fwds:
            cp.wait_send()
        for cp in loc:
            cp.wait()

    return pl.pallas_call(
        body, name="all_gather_weights", in_specs=[ANY, ANY], out_specs=[ANY, ANY],
        out_shape=[jax.ShapeDtypeStruct((N_CHIPS,) + wpack.shape, wpack.dtype), jax.ShapeDtypeStruct((N_CHIPS,) + cw.shape, cw.dtype)],
        scratch_shapes=[pltpu.SemaphoreType.DMA((6,)), pltpu.SemaphoreType.DMA((6,)), pltpu.SemaphoreType.DMA((3,)),
                        pltpu.SemaphoreType.DMA((3,)), pltpu.SemaphoreType.DMA((1,))],
        compiler_params=_comm_params(),
    )(wpack, cw)


def _exchange_pair(gp):
    def body(g_ref, r_ref, send_sems, recv_sems):
        x, y, c = _mesh_pos()
        cps = [pltpu.make_async_remote_copy(g_ref.at[s, 1 - c], r_ref.at[s], send_sems.at[s], recv_sems.at[s],
                                            device_id=(x, y, 1 - c), device_id_type=MESH) for s in range(N_CHIPS)]
        for cp in cps:
            cp.start()
        for cp in cps:
            cp.wait()

    return pl.pallas_call(
        body, name="grad_exchange_pair", in_specs=[ANY], out_specs=ANY,
        out_shape=jax.ShapeDtypeStruct((N_CHIPS,) + gp.shape[2:], gp.dtype),
        scratch_shapes=[pltpu.SemaphoreType.DMA((N_CHIPS,)), pltpu.SemaphoreType.DMA((N_CHIPS,))],
        compiler_params=_comm_params(),
    )(gp)


def _pair_sum(mine, from_sibling):
    def body(a_ref, b_ref, bf_ref, tail_ref):
        s = a_ref[0] + b_ref[0]
        bf_ref[0] = s.astype(BF)
        tail_ref[0] = s[SMALL_ROW0:, :]

    piece = pl.BlockSpec((1, HALF_ROWS, LANES), lambda i: (i, 0, 0))
    return pl.pallas_call(
        body, name="grad_pair_sum", grid=(N_CHIPS,), in_specs=[piece, piece],
        out_specs=[piece, pl.BlockSpec((1, SMALL_PIECE_ROWS, LANES), lambda i: (i, 0, 0))],
        out_shape=[jax.ShapeDtypeStruct((N_CHIPS, HALF_ROWS, LANES), BF), jax.ShapeDtypeStruct((N_CHIPS, SMALL_PIECE_ROWS, LANES), F32)],
        compiler_params=_params(("parallel",)),
    )(mine, from_sibling)


def _exchange_chips(ps_bf, ps_tail):
    def body(p_ref, t_ref, r_ref, rt_ref, send_sems, recv_sems):
        x, y, c = _mesh_pos()
        cps = []
        for k, (px, py) in enumerate(_other_chips(x, y)):
            cps.append(pltpu.make_async_remote_copy(p_ref.at[2 * px + py], r_ref.at[k], send_sems.at[2 * k], recv_sems.at[2 * k],
                                                    device_id=(px, py, c), device_id_type=MESH))
            cps.append(pltpu.make_async_remote_copy(t_ref.at[2 * px + py], rt_ref.at[k], send_sems.at[2 * k + 1],
                                                    recv_sems.at[2 * k + 1], device_id=(px, py, c), device_id_type=MESH))
        for cp in cps:
            cp.start()
        for cp in cps:
            cp.wait()

    return pl.pallas_call(
        body, name="grad_exchange_chips", in_specs=[ANY, ANY], out_specs=[ANY, ANY],
        out_shape=[jax.ShapeDtypeStruct((N_CHIPS - 1,) + ps_bf.shape[1:], ps_bf.dtype),
                   jax.ShapeDtypeStruct((N_CHIPS - 1,) + ps_tail.shape[1:], ps_tail.dtype)],
        scratch_shapes=[pltpu.SemaphoreType.DMA((6,)), pltpu.SemaphoreType.DMA((6,))],
        compiler_params=_comm_params(),
    )(ps_bf, ps_tail)


def _chip_sum(own_bf, own_tail, others_bf, others_tail):
    def body(ob_ref, ot_ref, b_ref, t_ref, o_ref):
        acc = ob_ref[0:SMALL_ROW0, :].astype(F32)
        tail = ot_ref[...]
        for k in range(N_CHIPS - 1):
            acc = acc + b_ref[k, 0:SMALL_ROW0, :].astype(F32)
            tail = tail + t_ref[k]
        o_ref[0:SMALL_ROW0, :] = acc
        o_ref[SMALL_ROW0:, :] = tail

    return pl.pallas_call(
        body, name="grad_chip_sum", out_shape=jax.ShapeDtypeStruct((HALF_ROWS, LANES), F32),
        compiler_params=_params(),
    )(own_bf, own_tail, others_bf, others_tail)


def _swap_reduced_halves(gh):
    def body(g_ref, o_ref, send_sem, recv_sem):
        x, y, c = _mesh_pos()
        cp = pltpu.make_async_remote_copy(g_ref, o_ref, send_sem, recv_sem, device_id=(x, y, 1 - c), device_id_type=MESH)
        cp.start()
        cp.wait()

    return pl.pallas_call(
        body, name="grad_swap_halves", in_specs=[ANY], out_specs=ANY,
        out_shape=jax.ShapeDtypeStruct(gh.shape, gh.dtype),
        scratch_shapes=[pltpu.SemaphoreType.DMA, pltpu.SemaphoreType.DMA],
        compiler_params=_comm_params(),
    )(gh)


def _gather_small(second_half):
    def body(gs_ref, sm_ref, send_sems, recv_sems, loc_sem):
        x, y, c = _mesh_pos()
        me = 2 * x + y
        small = gs_ref.at[pl.ds(SMALL_ROW0, SMALL_PIECE_ROWS)]
        loc = pltpu.make_async_copy(small, sm_ref.at[me], loc_sem)
        loc.start()
        cps = [pltpu.make_async_remote_copy(small, sm_ref.at[me], send_sems.at[k], recv_sems.at[k],
                                            device_id=(px, py, c), device_id_type=MESH)
               for k, (px, py) in enumerate(_other_chips(x, y))]
        for cp in cps:
            cp.start()
        for cp in cps:
            cp.wait()
        loc.wait()

    return pl.pallas_call(
        body, name="grad_gather_small", in_specs=[ANY], out_specs=ANY,
        out_shape=jax.ShapeDtypeStruct((N_CHIPS, SMALL_PIECE_ROWS, LANES), second_half.dtype),
        scratch_shapes=[pltpu.SemaphoreType.DMA((3,)), pltpu.SemaphoreType.DMA((3,)), pltpu.SemaphoreType.DMA],
        compiler_params=_comm_params(),
    )(second_half)


def _pack_grads(grads):
    small = jnp.concatenate([grads[n].reshape(-1) for n in SMALL])
    small = jnp.pad(small, (0, N_CHIPS * SMALL_PIECE_ROWS * LANES - SMALL_TOTAL)).reshape(N_CHIPS, SMALL_PIECE_ROWS, LANES)
    pieces = []
    for s in range(N_CHIPS):
        rows = []
        for n, (shp, axis) in BIG.items():
            g = grads[n]
            blk = g[s * shp[0]:(s + 1) * shp[0], :] if axis == 0 else g[:, s * shp[1]:(s + 1) * shp[1]]
            rows.append(blk.reshape(-1, LANES))
        rows.append(small[s])
        pieces.append(jnp.concatenate(rows, axis=0).reshape(2, HALF_ROWS, LANES))
    return jnp.stack(pieces)


def _unpack_shard(first_half, second_half):
    rows = jnp.concatenate([first_half, second_half], axis=0)
    out, r0 = {}, 0
    for n, (shp, _) in BIG.items():
        out[n] = rows[r0:r0 + BIG_ROWS[n]].reshape(shp)
        r0 += BIG_ROWS[n]
    return out


def _unpack_small(sm):
    flat = sm.reshape(-1)
    out, o = {}, 0
    for n, shp in SMALL.items():
        k = math.prod(shp)
        out[n] = flat[o:o + k].reshape(shp)
        o += k
    return out


def _as_2d(a):
    n = a.size
    if a.ndim >= 2 and a.shape[-1] > 1024:
        return a.reshape(-1, a.shape[-1])
    if n % 1024 == 0:
        return a.reshape(n // 1024, 1024)
    return a.reshape(1, n)


def _adamw(w, g, m, v, name):
    shape = w.shape
    w2, g2, m2, v2 = (_as_2d(a) for a in (w, g, m, v))
    rows, cols = w2.shape
    rb = 256 if rows >= 512 else rows

    def body(w_ref, g_ref, m_ref, v_ref, d_ref, mo_ref, vo_ref):
        gv = g_ref[...]
        mn = ADAM_B1 * m_ref[...] + (1.0 - ADAM_B1) * gv
        vn = ADAM_B2 * v_ref[...] + (1.0 - ADAM_B2) * (gv * gv)
        m_hat = mn / (1.0 - ADAM_B1 ** ADAM_STEP)
        v_hat = vn / (1.0 - ADAM_B2 ** ADAM_STEP)
        d_ref[...] = -ADAM_LR * (m_hat / (jnp.sqrt(v_hat) + ADAM_EPS) + ADAM_WD * w_ref[...])
        mo_ref[...] = mn
        vo_ref[...] = vn

    spec = _row_spec(rb, cols)
    sds = jax.ShapeDtypeStruct((rows, cols), F32)
    d, mo, vo = pl.pallas_call(
        body, name=name, grid=(rows // rb,), in_specs=[spec] * 4, out_specs=[spec] * 3, out_shape=[sds] * 3,
        compiler_params=_params(("parallel",)),
    )(w2, g2, m2, v2)
    return d.reshape(shape), mo.reshape(shape), vo.reshape(shape)


def kernel(x, p, norm_w, w_in, s5_A_re, s5_A_im, s5_log_dt, s5_B_re, s5_B_im, s5_C_re, s5_C_im, s5_D, s5_w_glu, s5_b_glu, conv_w, conv_b, dt_bias, A_log, ssd_D, ssd_norm_w, w_out, ple_norm_w, w_ple_gate, w_ple_proj, final_norm_w, loss_target, m_norm_w, m_w_in, m_s5_A_re, m_s5_A_im, m_s5_log_dt, m_s5_B_re, m_s5_B_im, m_s5_C_re, m_s5_C_im, m_s5_D, m_s5_w_glu, m_s5_b_glu, m_conv_w, m_conv_b, m_dt_bias, m_A_log, m_ssd_D, m_ssd_norm_w, m_w_out, m_ple_norm_w, m_w_ple_gate, m_w_ple_proj, m_final_norm_w, v_norm_w, v_w_in, v_s5_A_re, v_s5_A_im, v_s5_log_dt, v_s5_B_re, v_s5_B_im, v_s5_C_re, v_s5_C_im, v_s5_D, v_s5_w_glu, v_s5_b_glu, v_conv_w, v_conv_b, v_dt_bias, v_A_log, v_ssd_D, v_ssd_norm_w, v_w_out, v_ple_norm_w, v_w_ple_gate, v_w_ple_proj, v_final_norm_w):
    given = (norm_w, w_in, s5_A_re, s5_A_im, s5_log_dt, s5_B_re, s5_B_im, s5_C_re, s5_C_im, s5_D, s5_w_glu, s5_b_glu, conv_w, conv_b,
             dt_bias, A_log, ssd_D, ssd_norm_w, w_out, ple_norm_w, w_ple_gate, w_ple_proj, final_norm_w)
    given_m = (m_norm_w, m_w_in, m_s5_A_re, m_s5_A_im, m_s5_log_dt, m_s5_B_re, m_s5_B_im, m_s5_C_re, m_s5_C_im, m_s5_D, m_s5_w_glu,
               m_s5_b_glu, m_conv_w, m_conv_b, m_dt_bias, m_A_log, m_ssd_D, m_ssd_norm_w, m_w_out, m_ple_norm_w, m_w_ple_gate,
               m_w_ple_proj, m_final_norm_w)
    given_v = (v_norm_w, v_w_in, v_s5_A_re, v_s5_A_im, v_s5_log_dt, v_s5_B_re, v_s5_B_im, v_s5_C_re, v_s5_C_im, v_s5_D, v_s5_w_glu,
               v_s5_b_glu, v_conv_w, v_conv_b, v_dt_bias, v_A_log, v_ssd_D, v_ssd_norm_w, v_w_out, v_ple_norm_w, v_w_ple_gate,
               v_w_ple_proj, v_final_norm_w)
    wts, mom, var = dict(zip(WEIGHTS, given)), dict(zip(WEIGHTS, given_m)), dict(zip(WEIGHTS, given_v))
    drop = lambda n, a: a if n == "final_norm_w" else a[0]

    wpack = jnp.concatenate([drop(n, wts[n]).astype(BF).reshape(-1, LANES) for n in BIG], axis=0)
    wall, cwall = _all_gather_chips(wpack, drop("conv_w", wts["conv_w"]))
    chip = 2 * lax.axis_index("x") + lax.axis_index("y")
    is_own = (lax.broadcasted_iota(jnp.int32, (N_CHIPS, 1, 1), 0) == chip)
    full, r0 = {}, 0
    for n, (shp, axis) in BIG.items():
        blk = jnp.where(is_own, drop(n, wts[n]).astype(BF)[None], wall[:, r0:r0 + BIG_ROWS[n]].reshape((N_CHIPS,) + shp))
        full[n] = blk.reshape(N_CHIPS * shp[0], shp[1]) if axis == 0 else blk.transpose(1, 0, 2).reshape(shp[0], N_CHIPS * shp[1])
        r0 += BIG_ROWS[n]
    for n in SMALL:
        full[n] = drop(n, wts[n])
    full["conv_w"] = cwall.transpose(1, 0, 2).reshape(4, 2048)

    loss, gx, grads = _local_step(x[0], p[0, 0], loss_target[0], full)
    loss = lax.psum(loss, MESH_AXES)

    gp = _pack_grads({n: grads[n].reshape(SMALL[n]) if n in SMALL else grads[n] for n in WEIGHTS})
    c = lax.axis_index("c")
    from_sibling = _exchange_pair(gp)
    mine = lax.dynamic_index_in_dim(gp, c, axis=1, keepdims=False)
    ps_bf, ps_tail = _pair_sum(mine, from_sibling)
    others_bf, others_tail = _exchange_chips(ps_bf, ps_tail)
    own_bf = lax.dynamic_index_in_dim(ps_bf, chip, axis=0, keepdims=False)
    own_tail = lax.dynamic_index_in_dim(ps_tail, chip, axis=0, keepdims=False)
    reduced_half = _chip_sum(own_bf, own_tail, others_bf, others_tail)
    sibling_half = _swap_reduced_halves(reduced_half)
    first_half = jnp.where(c == 0, reduced_half, sibling_half)
    second_half = jnp.where(c == 0, sibling_half, reduced_half)
    sm = _gather_small(second_half)
    g_final = {**_unpack_small(sm), **_unpack_shard(first_half, second_half)}
    g_final["conv_w"] = lax.dynamic_slice_in_dim(g_final["conv_w"], chip * 512, 512, axis=1)

    outs_g, outs_d, outs_m, outs_v = [], [], [], []
    for n in WEIGHTS:
        g = g_final[n].reshape(wts[n].shape)
        d, mo, vo = _adamw(wts[n], g, mom[n], var[n], "adamw_" + n)
        outs_g.append(g)
        outs_d.append(d)
        outs_m.append(mo)
        outs_v.append(vo)
    return (loss, gx[None], *outs_g, *outs_d, *outs_m, *outs_v)
```

```python
import functools
import math

import jax
import jax.numpy as jnp
from jax import lax
from jax.experimental import pallas as pl
from jax.experimental.pallas import tpu as pltpu

F32 = jnp.float32
BF = jnp.bfloat16
EPS = 1e-6
CHUNK = 64
D_MODEL = 1024
S5_GROUPS = 64
S5_CH = 16
S5_STATE = 64
SSD_HEADS = 16
SSD_HEAD_DIM = 64
SSD_GROUPS = 4
SSD_STATE = 128
D_MAIN = 5120
LANES = 128
TOKEN_TILE = 256
VMEM_LIMIT = 56 * 1024 * 1024
MESH_AXES = ("x", "y", "c")
N_CHIPS = 4
ADAM_LR, ADAM_B1, ADAM_B2, ADAM_EPS, ADAM_WD, ADAM_STEP = 0.001, 0.9, 0.999, 1e-08, 0.01, 10
MESH = pl.DeviceIdType.MESH
ANY = pl.BlockSpec(memory_space=pl.ANY)


def _dot(a, b):
    return jnp.dot(a, b, preferred_element_type=F32)


def _dot_nt(a, b):
    return lax.dot_general(a, b, (((1,), (1,)), ((), ())), preferred_element_type=F32)


def _dot_tn(a, b):
    return lax.dot_general(a, b, (((0,), (0,)), ((), ())), preferred_element_type=F32)


def _sigmoid(x):
    return 1.0 / (1.0 + jnp.exp(-x))


def _softplus(x):
    return jnp.maximum(x, 0.0) + jnp.log(1.0 + jnp.exp(-jnp.abs(x)))


_GELU_C = math.sqrt(2.0 / math.pi)


def _gelu(x):
    return 0.5 * x * (1.0 + jnp.tanh(_GELU_C * (x + 0.044715 * x * x * x)))


def _gelu_grad(x):
    th = jnp.tanh(_GELU_C * (x + 0.044715 * x * x * x))
    return 0.5 * (1.0 + th) + 0.5 * x * (1.0 - th * th) * _GELU_C * (1.0 + 3.0 * 0.044715 * x * x)


def _params(sem=None):
    return pltpu.CompilerParams(dimension_semantics=sem, vmem_limit_bytes=VMEM_LIMIT)


def _row_spec(tl, width, col=0):
    return pl.BlockSpec((tl, width), lambda i, col=col: (i, col))


def _const_spec(shape):
    nd = len(shape)
    return pl.BlockSpec(shape, lambda *_: (0,) * nd)


def _in_proj_fwd(x, norm_w, w_main, w_dt):
    L = x.shape[0]
    tl = min(TOKEN_TILE, L)

    def body(x_ref, nw_ref, wm_ref, wd_ref, hn_ref, ps5_ref, pssd_ref, pd_ref):
        xv = x_ref[...]
        r = lax.rsqrt(jnp.mean(xv * xv, axis=-1, keepdims=True) + EPS)
        hn = (xv * r * nw_ref[...]).astype(BF)
        hn_ref[...] = hn
        for j in range(2):
            ps5_ref[:, j * 1024:(j + 1) * 1024] = _dot(hn, wm_ref[:, j * 1024:(j + 1) * 1024]).astype(BF)
        for j in range(3):
            pssd_ref[:, j * 1024:(j + 1) * 1024] = _dot(hn, wm_ref[:, (j + 2) * 1024:(j + 3) * 1024])
        pd_ref[...] = _dot(hn, wd_ref[...])

    return pl.pallas_call(
        body, name="in_proj_fwd", grid=(L // tl,),
        in_specs=[_row_spec(tl, D_MODEL), _const_spec((1, D_MODEL)), _const_spec((D_MODEL, D_MAIN)), _const_spec((D_MODEL, LANES))],
        out_specs=[_row_spec(tl, D_MODEL), _row_spec(tl, 2048), _row_spec(tl, 3072), _row_spec(tl, LANES)],
        out_shape=[jax.ShapeDtypeStruct((L, D_MODEL), BF), jax.ShapeDtypeStruct((L, 2048), BF), jax.ShapeDtypeStruct((L, 3072), F32),
                   jax.ShapeDtypeStruct((L, LANES), F32)],
        compiler_params=_params(("arbitrary",)),
    )(x, norm_w, w_main, w_dt)


def _in_proj_bwd(x, norm_w, dh1, du_flat, dyssm, s5_d, dzs, dzd, dxbc, ddt, w_main, w_dt):
    L = x.shape[0]
    tl = min(TOKEN_TILE, L)

    def body(x_ref, nw_ref, dh1_ref, duf_ref, dys_ref, d_ref, dzs_ref, dzd_ref, dxbc_ref, ddt_ref, wm_ref, wd_ref,
             gx_ref, du_ref, gnw_ref):
        @pl.when(pl.program_id(0) == 0)
        def _():
            gnw_ref[...] = jnp.zeros_like(gnw_ref)

        du = (duf_ref[...].astype(F32) + dys_ref[...].astype(F32) * d_ref[...]).astype(BF)
        du_ref[...] = du
        dhn = _dot_nt(du, wm_ref[:, 0:1024])
        dhn += _dot_nt(dzs_ref[...], wm_ref[:, 1024:2048])
        dhn += _dot_nt(dzd_ref[...], wm_ref[:, 2048:3072])
        dhn += _dot_nt(dxbc_ref[...], wm_ref[:, 3072:5120])
        dhn += _dot_nt(ddt_ref[...].astype(BF), wd_ref[...])
        xv = x_ref[...]
        r = lax.rsqrt(jnp.mean(xv * xv, axis=-1, keepdims=True) + EPS)
        xh = xv * r
        gnw_ref[...] += jnp.sum(dhn * xh, axis=0, keepdims=True)
        g = dhn * nw_ref[...]
        gx_ref[...] = dh1_ref[...] + r * (g - xh * jnp.mean(g * xh, axis=-1, keepdims=True))

    return pl.pallas_call(
        body, name="in_proj_bwd", grid=(L // tl,),
        in_specs=[_row_spec(tl, D_MODEL), _const_spec((1, D_MODEL)), _row_spec(tl, D_MODEL), _row_spec(tl, D_MODEL),
                  _row_spec(tl, D_MODEL), _const_spec((1, D_MODEL)), _row_spec(tl, D_MODEL), _row_spec(tl, D_MODEL),
                  _row_spec(tl, 2048), _row_spec(tl, LANES), _const_spec((D_MODEL, D_MAIN)), _const_spec((D_MODEL, LANES))],
        out_specs=[_row_spec(tl, D_MODEL), _row_spec(tl, D_MODEL), _const_spec((1, D_MODEL))],
        out_shape=[jax.ShapeDtypeStruct((L, D_MODEL), F32), jax.ShapeDtypeStruct((L, D_MODEL), BF), jax.ShapeDtypeStruct((1, D_MODEL), F32)],
        compiler_params=_params(("arbitrary",)),
    )(x, norm_w, dh1, du_flat, dyssm, s5_d, dzs, dzd, dxbc, ddt, w_main, w_dt)


def _matmul_tn(a, b, name):
    L, M = a.shape
    N = b.shape[1]
    tm, tn, tk = min(M, 1024), min(N, 1024), min(L, 512)

    def body(a_ref, b_ref, o_ref):
        @pl.when(pl.program_id(2) == 0)
        def _():
            o_ref[...] = jnp.zeros_like(o_ref)

        o_ref[...] += _dot_tn(a_ref[...].astype(BF), b_ref[...].astype(BF))

    return pl.pallas_call(
        body, name=name, grid=(M // tm, N // tn, L // tk),
        in_specs=[pl.BlockSpec((tk, tm), lambda i, j, k: (k, i)), pl.BlockSpec((tk, tn), lambda i, j, k: (k, j))],
        out_specs=pl.BlockSpec((tm, tn), lambda i, j, k: (i, j)),
        out_shape=jax.ShapeDtypeStruct((M, N), F32),
        compiler_params=_params(("parallel", "parallel", "arbitrary")),
    )(a, b)


def _s5_discretise(a_re, a_im, log_dt, b_re, b_im, c_re, c_im):
    dt = jnp.exp(log_dt)[:, None]
    tau = jnp.arange(CHUNK + 1, dtype=F32)
    mag = jnp.exp((a_re * dt)[:, :, None] * tau)
    ang = (a_im * dt)[:, :, None] * tau
    pw_re, pw_im = mag * jnp.cos(ang), mag * jnp.sin(ang)
    er, ei = pw_re[:, :, 1] - 1.0, pw_im[:, :, 1]
    den = a_re * a_re + a_im * a_im
    beta_re, beta_im = (er * a_re + ei * a_im) / den, (ei * a_re - er * a_im) / den
    bb_re = (beta_re[:, :, None] * b_re - beta_im[:, :, None] * b_im).transpose(0, 2, 1)
    bb_im = (beta_re[:, :, None] * b_im + beta_im[:, :, None] * b_re).transpose(0, 2, 1)
    return bb_re, bb_im, c_re, c_im, pw_re, pw_im


def _s5_kmat(bb_re, bb_im, c_re, c_im, pw_re, pw_im):
    hi = lax.Precision.HIGHEST
    m_re = bb_re[:, :, None, :] * c_re[:, None, :, :] - bb_im[:, :, None, :] * c_im[:, None, :, :]
    m_im = bb_re[:, :, None, :] * c_im[:, None, :, :] + bb_im[:, :, None, :] * c_re[:, None, :, :]
    k4 = (jnp.einsum("ghkn,gnt->ghtk", m_re, pw_re[:, :, :CHUNK], precision=hi)
          - jnp.einsum("ghkn,gnt->ghtk", m_im, pw_im[:, :, :CHUNK], precision=hi))
    return k4.reshape(S5_GROUPS, S5_CH, CHUNK * S5_CH)


def _s5_big_tables(bb_re, bb_im, c_re, c_im, pw_re, pw_im):
    G = S5_GROUPS
    pr = pw_re[:, :, CHUNK - 1::-1].transpose(0, 2, 1)
    pi = pw_im[:, :, CHUNK - 1::-1].transpose(0, 2, 1)
    br, bi = bb_re[:, :, None, :], bb_im[:, :, None, :]
    wst = jnp.concatenate([br * pr[:, None] - bi * pi[:, None], br * pi[:, None] + bi * pr[:, None]], axis=-1)
    wst = wst.reshape(G, S5_CH * CHUNK, 2 * S5_STATE)
    shape4 = (G, S5_STATE, CHUNK, S5_CH)
    cr = jnp.broadcast_to(c_re.transpose(0, 2, 1)[:, :, None, :], shape4)
    ci = jnp.broadcast_to(c_im.transpose(0, 2, 1)[:, :, None, :], shape4)
    p1r = jnp.broadcast_to(pw_re[:, :, 1:CHUNK + 1, None], shape4)
    p1i = jnp.broadcast_to(pw_im[:, :, 1:CHUNK + 1, None], shape4)
    flat = lambda a, b: jnp.concatenate([a, b], axis=1).reshape(G, 2 * S5_STATE, CHUNK * S5_CH)
    woff = flat(cr * p1r - ci * p1i, -(cr * p1i + ci * p1r))
    bbp = jnp.concatenate([bb_re, bb_im], axis=-1)
    pwr = jnp.concatenate([pr, pi], axis=-1)
    return wst, woff, bbp, pwr, flat(cr, ci), flat(p1r, p1i)


def _s5_small_cotangents(r12, q12, dc, dpx, da64):
    n = S5_STATE
    fold = lambda a: a[..., :n] + a[..., n:]
    fold_m = lambda a: a[..., n:] - a[..., :n]
    dbb_re, dbb_im = fold(r12[:, :S5_CH]), fold_m(r12[:, S5_CH:])
    dpr, dpi = fold(q12[:, :CHUNK]), fold_m(q12[:, CHUNK:])
    dc_re, dc_im = dc[:, :n, :S5_CH].transpose(0, 2, 1), dc[:, n:, :S5_CH].transpose(0, 2, 1)
    dp1 = dpx[:, :, S5_CH - 1::S5_CH]
    zero = jnp.zeros((S5_GROUPS, n, 1), F32)
    dpw_re = (jnp.concatenate([dpr[:, ::-1].transpose(0, 2, 1), zero], axis=-1)
              + jnp.concatenate([zero, dp1[:, :n]], axis=-1)).at[:, :, CHUNK].add(da64[:, :n])
    dpw_im = (jnp.concatenate([dpi[:, ::-1].transpose(0, 2, 1), zero], axis=-1)
              + jnp.concatenate([zero, dp1[:, n:]], axis=-1)).at[:, :, CHUNK].add(da64[:, n:])
    return dbb_re, dbb_im, dc_re, dc_im, dpw_re, dpw_im


def _s5_scan_powers(a_re, a_im, log_dt, nsteps):
    dt = jnp.exp(log_dt)[:, None]
    steps = (CHUNK * (2.0 ** jnp.arange(8, dtype=F32)))[None, :, None]
    mag = jnp.exp((a_re * dt)[:, None, :] * steps)
    ang = (a_im * dt)[:, None, :] * steps
    re, im = mag * jnp.cos(ang), mag * jnp.sin(ang)
    del nsteps
    return jnp.concatenate([re, re], -1), jnp.concatenate([-im, im], -1)


def _build_toeplitz(kmat_ref, tg_ref):
    lane = lax.broadcasted_iota(jnp.int32, (CHUNK, CHUNK * S5_CH), 1)
    srow = lax.broadcasted_iota(jnp.int32, (CHUNK, CHUNK * S5_CH), 0)
    keep = lane >= S5_CH * srow
    for h in range(S5_CH):
        row = jnp.broadcast_to(kmat_ref[0, h:h + 1, :], (CHUNK, CHUNK * S5_CH))
        rolled = pltpu.roll(row, 0, 1, stride=S5_CH, stride_axis=0)
        tg_ref[h * CHUNK:(h + 1) * CHUNK, :] = jnp.where(keep, rolled, 0.0).astype(BF)


def _swap_halves(x):
    return pltpu.roll(x, S5_STATE, 1)


def _s5_core_fwd(uflat, kmat, wst, woff, p1, p2):
    G, nc, W = uflat.shape
    nsteps = max(1, (nc - 1).bit_length())

    def body(u_ref, k_ref, wst_ref, woff_ref, p1_ref, p2_ref, y_ref, h_ref, tg_ref):
        _build_toeplitz(k_ref, tg_ref)
        u = u_ref[0]
        x = _dot(u, wst_ref[0])
        row = lax.broadcasted_iota(jnp.int32, x.shape, 0)
        d = 1
        for k in range(nsteps):
            sh = jnp.where(row >= d, pltpu.roll(x, d, 0), 0.0)
            x = x + p1_ref[0, k:k + 1, :] * sh + p2_ref[0, k:k + 1, :] * _swap_halves(sh)
            d *= 2
        h = jnp.where(row >= 1, pltpu.roll(x, 1, 0), 0.0)
        h_ref[0] = h
        y = _dot(u, tg_ref[...]) + _dot(h.astype(BF), woff_ref[0])
        y_ref[0] = y.astype(BF)

    return pl.pallas_call(
        body, name="s5_core_fwd", grid=(G,),
        in_specs=[pl.BlockSpec((1, nc, W), lambda g: (g, 0, 0)), pl.BlockSpec((1, S5_CH, W), lambda g: (g, 0, 0)),
                  pl.BlockSpec((1, W, 2 * S5_STATE), lambda g: (g, 0, 0)), pl.BlockSpec((1, 2 * S5_STATE, W), lambda g: (g, 0, 0)),
                  pl.BlockSpec((1, 8, 2 * S5_STATE), lambda g: (g, 0, 0)), pl.BlockSpec((1, 8, 2 * S5_STATE), lambda g: (g, 0, 0))],
        out_specs=[pl.BlockSpec((1, nc, W), lambda g: (g, 0, 0)), pl.BlockSpec((1, nc, 2 * S5_STATE), lambda g: (g, 0, 0))],
        out_shape=[jax.ShapeDtypeStruct((G, nc, W), BF), jax.ShapeDtypeStruct((G, nc, 2 * S5_STATE), F32)],
        scratch_shapes=[pltpu.VMEM((W, W), BF)],
        compiler_params=_params(("arbitrary",)),
    )(uflat, kmat, wst, woff, p1, p2)


def _s5_core_bwd(uflat, dyflat, hsave, kmat, wst, woff, p1, p2, bbp, pwr, cx, px):
    G, nc, W = uflat.shape
    nsteps = max(1, (nc - 1).bit_length())

    def body(u_ref, dy_ref, h_ref, k_ref, wst_ref, woff_ref, p1_ref, p2_ref, bbp_ref, pwr_ref, cx_ref, px_ref,
             du_ref, dk_ref, r12_ref, q12_ref, dc_ref, dpx_ref, da_ref, tg_ref, flip_ref):
        @pl.when(pl.program_id(0) == 0)
        def _():
            r = lax.broadcasted_iota(jnp.int32, (W, W), 0)
            c = lax.broadcasted_iota(jnp.int32, (W, W), 1)
            flip_ref[...] = (((r >> 6) == (c >> 6)) & ((r & (CHUNK - 1)) + (c & (CHUNK - 1)) == CHUNK - 1)).astype(BF)

        _build_toeplitz(k_ref, tg_ref)
        u = u_ref[0]
        dy = dy_ref[0]
        h = h_ref[0]
        gh = _dot_nt(dy, woff_ref[0])
        row = lax.broadcasted_iota(jnp.int32, gh.shape, 0)
        x = jnp.where(row < nc - 1, pltpu.roll(gh, nc - 1, 0), 0.0)
        d = 1
        for k in range(nsteps):
            sh = jnp.where(row < nc - d, pltpu.roll(x, nc - d, 0), 0.0)
            x = x + p1_ref[0, k:k + 1, :] * sh - p2_ref[0, k:k + 1, :] * _swap_halves(sh)
            d *= 2
        gs = x.astype(BF)
        du_ref[0] = (_dot_nt(dy, tg_ref[...]) + _dot_nt(gs, wst_ref[0])).astype(BF)

        dwst = _dot_tn(u, gs)
        pwr = pwr_ref[0]
        pwr_sw = _swap_halves(pwr)
        q1 = jnp.zeros((CHUNK, 2 * S5_STATE), F32)
        q2 = jnp.zeros((CHUNK, 2 * S5_STATE), F32)
        r1_rows, r2_rows = [], []
        for hh in range(S5_CH):
            blk = dwst[hh * CHUNK:(hh + 1) * CHUNK, :]
            r1_rows.append(jnp.sum(blk * pwr, axis=0, keepdims=True))
            r2_rows.append(jnp.sum(blk * pwr_sw, axis=0, keepdims=True))
            bb = bbp_ref[0, hh:hh + 1, :]
            q1 = q1 + blk * bb
            q2 = q2 + blk * _swap_halves(bb)
        r12_ref[0] = jnp.concatenate(r1_rows + r2_rows, axis=0)
        q12_ref[0] = jnp.concatenate([q1, q2], axis=0)

        dwoff = _dot_tn(h.astype(BF), dy)
        b1, b2 = dwoff[0:S5_STATE], dwoff[S5_STATE:2 * S5_STATE]
        cr, ci = cx_ref[0, 0:S5_STATE, :], cx_ref[0, S5_STATE:2 * S5_STATE, :]
        pr, pi = px_ref[0, 0:S5_STATE, :], px_ref[0, S5_STATE:2 * S5_STATE, :]

        def sum_over_t(v):
            sh = S5_CH
            while sh < W:
                v = v + pltpu.roll(v, sh, 1)
                sh *= 2
            return v[:, 0:2 * S5_STATE]

        def sum_over_k(v):
            sh = 1
            while sh < S5_CH:
                v = v + pltpu.roll(v, sh, 1)
                sh *= 2
            return v

        dc_ref[0] = jnp.concatenate([sum_over_t(b1 * pr - b2 * pi), sum_over_t(-b1 * pi - b2 * pr)], axis=0)
        dpx_ref[0] = jnp.concatenate([sum_over_k(b1 * cr - b2 * ci), sum_over_k(-b1 * ci - b2 * cr)], axis=0)
        r1 = jnp.sum(x * h, axis=0, keepdims=True)
        r2 = jnp.sum(x * _swap_halves(h), axis=0, keepdims=True)
        da_ref[0] = jnp.concatenate([r1, r2, jnp.zeros((6, 2 * S5_STATE), F32)], axis=0)
        lane = lax.broadcasted_iota(jnp.int32, (CHUNK, W), 1)
        srow = lax.broadcasted_iota(jnp.int32, (CHUNK, W), 0)
        keep = lane < S5_CH * (srow + 1)
        ur = _dot(u, flip_ref[...]).astype(BF)
        for hh in range(S5_CH):
            dt_h = _dot_tn(ur[:, hh * CHUNK:(hh + 1) * CHUNK], dy)
            back = pltpu.roll(dt_h, S5_CH, 1, stride=S5_CH, stride_axis=0)
            dk_ref[0, hh:hh + 1, :] = jnp.sum(jnp.where(keep, back, 0.0), axis=0, keepdims=True)

    spec_g = lambda a, b: pl.BlockSpec((1, a, b), lambda g: (g, 0, 0))
    return pl.pallas_call(
        body, name="s5_core_bwd", grid=(G,),
        in_specs=[spec_g(nc, W), spec_g(nc, W), spec_g(nc, 2 * S5_STATE), spec_g(S5_CH, W), spec_g(W, 2 * S5_STATE),
                  spec_g(2 * S5_STATE, W), spec_g(8, 2 * S5_STATE), spec_g(8, 2 * S5_STATE),
                  spec_g(S5_CH, 2 * S5_STATE), spec_g(CHUNK, 2 * S5_STATE), spec_g(2 * S5_STATE, W), spec_g(2 * S5_STATE, W)],
        out_specs=[spec_g(nc, W), spec_g(S5_CH, W), spec_g(2 * S5_CH, 2 * S5_STATE), spec_g(2 * CHUNK, 2 * S5_STATE),
                   spec_g(2 * S5_STATE, 2 * S5_STATE), spec_g(2 * S5_STATE, W), spec_g(8, 2 * S5_STATE)],
        out_shape=[jax.ShapeDtypeStruct((G, nc, W), BF), jax.ShapeDtypeStruct((G, S5_CH, W), F32),
                   jax.ShapeDtypeStruct((G, 2 * S5_CH, 2 * S5_STATE), F32), jax.ShapeDtypeStruct((G, 2 * CHUNK, 2 * S5_STATE), F32),
                   jax.ShapeDtypeStruct((G, 2 * S5_STATE, 2 * S5_STATE), F32), jax.ShapeDtypeStruct((G, 2 * S5_STATE, W), F32),
                   jax.ShapeDtypeStruct((G, 8, 2 * S5_STATE), F32)],
        scratch_shapes=[pltpu.VMEM((W, W), BF), pltpu.VMEM((W, W), BF)],
        compiler_params=_params(("arbitrary",)),
    )(uflat, dyflat, hsave, kmat, wst, woff, p1, p2, bbp, pwr, cx, px)


def _flat_hs(a, nc):
    return a.reshape(nc, CHUNK, S5_GROUPS, S5_CH).transpose(2, 0, 3, 1).reshape(S5_GROUPS, nc, CHUNK * S5_CH)


def _unflat_hs(a, nc):
    return a.reshape(S5_GROUPS, nc, S5_CH, CHUNK).transpose(1, 3, 0, 2).reshape(nc * CHUNK, D_MODEL)


def _flat_tk(a, nc):
    return a.reshape(nc, CHUNK, S5_GROUPS, S5_CH).transpose(2, 0, 1, 3).reshape(S5_GROUPS, nc, CHUNK * S5_CH)


def _unflat_tk(a, nc):
    return a.reshape(S5_GROUPS, nc, CHUNK, S5_CH).transpose(1, 2, 0, 3).reshape(nc * CHUNK, D_MODEL)


def _s5_post_fwd(yssm, proj, s5_d, w_glu, b_glu):
    L = yssm.shape[0]
    tl = min(TOKEN_TILE, L)

    def body(ys_ref, u_ref, z_ref, d_ref, wg_ref, bg_ref, o_ref):
        u = u_ref[...].astype(F32)
        a = _gelu(ys_ref[...].astype(F32) + d_ref[...] * u)
        y = a * _sigmoid(_dot(a.astype(BF), wg_ref[...]) + bg_ref[...])
        z = z_ref[...].astype(F32)
        o_ref[...] = (y * z * _sigmoid(z)).astype(BF)

    return pl.pallas_call(
        body, name="s5_post_fwd", grid=(L // tl,),
        in_specs=[_row_spec(tl, D_MODEL), _row_spec(tl, D_MODEL, 0), _row_spec(tl, D_MODEL, 1), _const_spec((1, D_MODEL)),
                  _const_spec((D_MODEL, D_MODEL)), _const_spec((1, D_MODEL))],
        out_specs=_row_spec(tl, D_MODEL),
        out_shape=jax.ShapeDtypeStruct((L, D_MODEL), BF),
        compiler_params=_params(("arbitrary",)),
    )(yssm, proj, proj, s5_d, w_glu, b_glu)


def _s5_post_bwd(dys5, yssm, proj, s5_d, w_glu, b_glu):
    L = yssm.shape[0]
    tl = min(TOKEN_TILE, L)

    def body(dy_ref, ys_ref, u_ref, z_ref, d_ref, wg_ref, bg_ref, dz_ref, dys_ref, a_ref, dgl_ref, dbg_ref, dd_ref):
        @pl.when(pl.program_id(0) == 0)
        def _():
            dbg_ref[...] = jnp.zeros_like(dbg_ref)
            dd_ref[...] = jnp.zeros_like(dd_ref)

        u = u_ref[...].astype(F32)
        y0 = ys_ref[...].astype(F32) + d_ref[...] * u
        a = _gelu(y0)
        a_bf = a.astype(BF)
        sg = _sigmoid(_dot(a_bf, wg_ref[...]) + bg_ref[...])
        y = a * sg
        z = z_ref[...].astype(F32)
        sz = _sigmoid(z)
        dout = dy_ref[...].astype(F32)
        dz_ref[...] = (dout * y * sz * (1.0 + z * (1.0 - sz))).astype(BF)
        dyv = dout * z * sz
        dgl = dyv * a * sg * (1.0 - sg)
        dgl_bf = dgl.astype(BF)
        da = dyv * sg + _dot_nt(dgl_bf, wg_ref[...])
        dy0 = da * _gelu_grad(y0)
        dbg_ref[...] += jnp.sum(dgl, axis=0, keepdims=True)
        dd_ref[...] += jnp.sum(dy0 * u, axis=0, keepdims=True)
        dys_ref[...] = dy0.astype(BF)
        a_ref[...] = a_bf
        dgl_ref[...] = dgl_bf

    big = jax.ShapeDtypeStruct((L, D_MODEL), BF)
    vec = jax.ShapeDtypeStruct((1, D_MODEL), F32)
    return pl.pallas_call(
        body, name="s5_post_bwd", grid=(L // tl,),
        in_specs=[_row_spec(tl, D_MODEL), _row_spec(tl, D_MODEL), _row_spec(tl, D_MODEL, 0), _row_spec(tl, D_MODEL, 1),
                  _const_spec((1, D_MODEL)), _const_spec((D_MODEL, D_MODEL)), _const_spec((1, D_MODEL))],
        out_specs=[_row_spec(tl, D_MODEL)] * 4 + [_const_spec((1, D_MODEL))] * 2,
        out_shape=[big, big, big, big, vec, vec],
        compiler_params=_params(("arbitrary",)),
    )(dys5, yssm, proj, proj, s5_d, w_glu, b_glu)


def _cumsum_rows(a):
    row = lax.broadcasted_iota(jnp.int32, a.shape, 0)
    d = 1
    while d < a.shape[0]:
        a = a + jnp.where(row >= d, pltpu.roll(a, d, 0), 0.0)
        d *= 2
    return a


def _rev_cumsum_rows(a):
    n = a.shape[0]
    row = lax.broadcasted_iota(jnp.int32, a.shape, 0)
    d = 1
    while d < n:
        a = a + jnp.where(row < n - d, pltpu.roll(a, n - d, 0), 0.0)
        d *= 2
    return a


def _ssd_conv_fwd(first, xs_ref, bc_ref, hx_ref, hb_ref, cw_ref, cb_ref, xp_ref, tl):
    hal = jnp.concatenate([hx_ref[...], hb_ref[...]], axis=1)
    xp_ref[0:8, :] = jnp.where(first, 0.0, hal)
    xp_ref[8:8 + tl, 0:1024] = xs_ref[...]
    xp_ref[8:8 + tl, 1024:2048] = bc_ref[...]
    pre = cb_ref[...] + cw_ref[0:1, :] * xp_ref[5:5 + tl, :]
    for k in range(1, 4):
        pre = pre + cw_ref[k:k + 1, :] * xp_ref[5 + k:5 + k + tl, :]
    return pre


def _onehot_lane(h):
    return (lax.broadcasted_iota(jnp.int32, (1, LANES), 1) == h).astype(F32)


def _dot_exact(x, e):
    hi = x.astype(BF)
    r = x - hi.astype(F32)
    mid = r.astype(BF)
    lo = (r - mid.astype(F32)).astype(BF)
    return _dot(hi, e) + _dot(mid, e) + _dot(lo, e)


def _head_expand_matrices():
    e = lax.broadcasted_iota(jnp.int32, (LANES, D_MODEL), 0) == (lax.broadcasted_iota(jnp.int32, (LANES, D_MODEL), 1) >> 6)
    et = (lax.broadcasted_iota(jnp.int32, (D_MODEL, LANES), 0) >> 6) == lax.broadcasted_iota(jnp.int32, (D_MODEL, LANES), 1)
    return e.astype(BF), et.astype(BF)


def _group_masks():
    r64 = lax.broadcasted_iota(jnp.int32, (4 * CHUNK, CHUNK), 0)
    causal4 = (r64 & (CHUNK - 1)) >= lax.broadcasted_iota(jnp.int32, (4 * CHUNK, CHUNK), 1)
    r256 = lax.broadcasted_iota(jnp.int32, (4 * CHUNK, 4 * SSD_HEAD_DIM), 0)
    same = (r256 >> 6) == (lax.broadcasted_iota(jnp.int32, (4 * CHUNK, 4 * SSD_HEAD_DIM), 1) >> 6)
    return causal4, same


def _group_decay(acs, acs_t, j, causal4):
    col = jnp.concatenate([acs[:, 4 * j + hh:4 * j + hh + 1] for hh in range(4)], axis=0)
    rowv = jnp.concatenate([jnp.broadcast_to(acs_t[4 * j + hh:4 * j + hh + 1, :], (CHUNK, CHUNK)) for hh in range(4)], axis=0)
    return jnp.where(causal4, jnp.exp(col - rowv), 0.0)


def _group_last_decay(acs_t, j):
    return jnp.concatenate([jnp.broadcast_to(jnp.exp(acs_t[4 * j + hh:4 * j + hh + 1, CHUNK - 1:CHUNK]), (SSD_HEAD_DIM, 1))
                            for hh in range(4)], axis=0)


def _fold_heads(r):
    return r[0:CHUNK] + r[CHUNK:2 * CHUNK] + r[2 * CHUNK:3 * CHUNK] + r[3 * CHUNK:4 * CHUNK]


def _ssd_specs_in(tl, nt, rev):
    t_of = (lambda i: nt - 1 - i) if rev else (lambda i: i)
    rows = lambda w, col: pl.BlockSpec((tl, w), lambda i: (t_of(i), col))
    halo = lambda col: pl.BlockSpec((8, 1024), lambda i: (jnp.maximum(t_of(i) * (tl // 8) - 1, 0), col))
    return t_of, rows, halo


def _ssd_fwd(proj, pdt, conv_w, conv_b, dt_bias, a_log, ssd_d, norm_w):
    L = proj.shape[0]
    tl = min(TOKEN_TILE, L)
    nt, ncl = L // tl, tl // CHUNK
    _, rows, halo = _ssd_specs_in(tl, nt, False)

    def body(xs_ref, bc_ref, hx_ref, hb_ref, dt_ref, z_ref, cw_ref, cb_ref, dtb_ref, al_ref, dd_ref, nw_ref,
             y_ref, ypre_ref, st_ref, xp_ref, xbc_ref, dts_ref, hst_ref):
        i = pl.program_id(0)

        @pl.when(i == 0)
        def _():
            hst_ref[...] = jnp.zeros_like(hst_ref)

        pre = _ssd_conv_fwd(i == 0, xs_ref, bc_ref, hx_ref, hb_ref, cw_ref, cb_ref, xp_ref, tl)
        xbc_ref[...] = pre * _sigmoid(pre)
        dts_ref[...] = _softplus(dt_ref[...] + dtb_ref[...])
        a_neg = -jnp.exp(al_ref[...])
        e16, _ = _head_expand_matrices()
        causal4, same = _group_masks()
        dd_x = _dot_exact(jnp.broadcast_to(dd_ref[...], (8, LANES)), e16)[0:1, :]

        def chunk(c, carry):
            r0 = pl.multiple_of(c * CHUNK, CHUNK)
            xbc = xbc_ref[pl.ds(r0, CHUNK), :]
            dtc = dts_ref[pl.ds(r0, CHUNK), :]
            acs = _cumsum_rows(dtc * a_neg)
            acs_t = acs.T
            acs_x = _dot_exact(acs, e16)
            xs = xbc[:, 0:1024]
            xd = xs * _dot_exact(dtc, e16)
            xd_bf = xd.astype(BF)
            xdd = (xd * jnp.exp(acs_x[CHUNK - 1:CHUNK, :] - acs_x)).astype(BF)
            e_x = jnp.exp(acs_x)
            for j in range(SSD_GROUPS):
                sl = slice(256 * j, 256 * (j + 1))
                bj = xbc[:, 1024 + 128 * j:1024 + 128 * (j + 1)].astype(BF)
                cj = xbc[:, 1536 + 128 * j:1536 + 128 * (j + 1)].astype(BF)
                g = _dot_nt(cj, bj)
                hj = hst_ref[sl, :]
                zj = _dot_nt(cj, hj.astype(BF))
                sc = (jnp.concatenate([g] * 4, axis=0) * _group_decay(acs, acs_t, j, causal4)).astype(BF)
                yd = _fold_heads(jnp.where(same, _dot(sc, xd_bf[:, sl]), 0.0))
                ypre_ref[pl.ds(r0, CHUNK), sl] = yd + e_x[:, sl] * zj + dd_x[:, sl] * xs[:, sl]
                st_ref[c, sl, :] = hj
                hst_ref[sl, :] = _group_last_decay(acs_t, j) * hj + _dot_tn(xdd[:, sl], bj)
            return carry

        lax.fori_loop(0, ncl, chunk, 0)
        z = z_ref[...]
        gg = ypre_ref[...] * z * _sigmoid(z)
        for j in range(SSD_GROUPS):
            seg = gg[:, 256 * j:256 * (j + 1)]
            r = lax.rsqrt(jnp.mean(seg * seg, axis=-1, keepdims=True) + EPS)
            y_ref[:, 256 * j:256 * (j + 1)] = (seg * r * nw_ref[:, 256 * j:256 * (j + 1)]).astype(BF)

    nc = L // CHUNK
    return pl.pallas_call(
        body, name="ssd_fwd", grid=(nt,),
        in_specs=[rows(1024, 1), rows(1024, 2), halo(1), halo(2), rows(LANES, 0), rows(1024, 0),
                  _const_spec((4, 2048)), _const_spec((1, 2048)), _const_spec((1, LANES)), _const_spec((1, LANES)),
                  _const_spec((1, LANES)), _const_spec((1, D_MODEL))],
        out_specs=[_row_spec(tl, D_MODEL), _row_spec(tl, D_MODEL), pl.BlockSpec((ncl, 1024, SSD_STATE), lambda i: (i, 0, 0))],
        out_shape=[jax.ShapeDtypeStruct((L, D_MODEL), BF), jax.ShapeDtypeStruct((L, D_MODEL), F32),
                   jax.ShapeDtypeStruct((nc, 1024, SSD_STATE), F32)],
        scratch_shapes=[pltpu.VMEM((tl + 8, 2048), F32), pltpu.VMEM((tl, 2048), F32), pltpu.VMEM((tl, LANES), F32),
                        pltpu.VMEM((1024, SSD_STATE), F32)],
        compiler_params=_params(("arbitrary",)),
    )(proj, proj, proj, proj, pdt, proj, conv_w, conv_b, dt_bias, a_log, ssd_d, norm_w)


def _ssd_bwd(dyssd, ypre, proj, pdt, states, conv_w, conv_b, dt_bias, a_log, ssd_d, norm_w):
    L = proj.shape[0]
    tl = min(TOKEN_TILE, L)
    nt, ncl = L // tl, tl // CHUNK
    t_of, rows, halo = _ssd_specs_in(tl, nt, True)

    def body(dy_ref, ypre_ref, z_ref, xs_ref, bc_ref, hx_ref, hb_ref, dt_ref, st_ref, cw_ref, cb_ref, dtb_ref, al_ref,
             dd_ref, nw_ref,
             dxbc_ref, ddt_ref, dz_ref, dcw_ref, dcb_ref, ddtb_ref, dal_ref, ddd_ref, dnw_ref,
             xp_ref, xbc_ref, pre_ref, dts_ref, dyp_ref, dxs_ref, ddts_ref, dp_ref, dh_ref):
        i = pl.program_id(0)

        @pl.when(i == 0)
        def _():
            for r in (dcw_ref, dcb_ref, ddtb_ref, dal_ref, ddd_ref, dnw_ref, dh_ref):
                r[...] = jnp.zeros_like(r)
            dp_ref[tl:tl + 8, :] = jnp.zeros((8, 2048), F32)

        pre = _ssd_conv_fwd(t_of(i) == 0, xs_ref, bc_ref, hx_ref, hb_ref, cw_ref, cb_ref, xp_ref, tl)
        pre_ref[...] = pre
        xbc_ref[...] = pre * _sigmoid(pre)
        dts_ref[...] = _softplus(dt_ref[...] + dtb_ref[...])
        a_neg = -jnp.exp(al_ref[...])

        ypre = ypre_ref[...]
        z = z_ref[...]
        sz = _sigmoid(z)
        gg = ypre * z * sz
        dout = dy_ref[...]
        for j in range(SSD_GROUPS):
            sl = slice(256 * j, 256 * (j + 1))
            seg = gg[:, sl]
            r = lax.rsqrt(jnp.mean(seg * seg, axis=-1, keepdims=True) + EPS)
            gh = seg * r
            dnw_ref[:, sl] += jnp.sum(dout[:, sl] * gh, axis=0, keepdims=True)
            gw = dout[:, sl] * nw_ref[:, sl]
            dgg = r * (gw - gh * jnp.mean(gw * gh, axis=-1, keepdims=True))
            dyp_ref[:, sl] = dgg * z[:, sl] * sz[:, sl]
            dz_ref[:, sl] = (dgg * ypre[:, sl] * sz[:, sl] * (1.0 + z[:, sl] * (1.0 - sz[:, sl]))).astype(BF)

        e16, e16t = _head_expand_matrices()
        causal4, same = _group_masks()
        dd_x = _dot_exact(jnp.broadcast_to(dd_ref[...], (8, LANES)), e16)[0:1, :]
        last_row = (lax.broadcasted_iota(jnp.int32, (CHUNK, 1), 0) == CHUNK - 1).astype(F32)
        sel_rows = lax.broadcasted_iota(jnp.int32, (4 * CHUNK, LANES), 0) >> 6
        sel_lanes = lax.broadcasted_iota(jnp.int32, (4 * CHUNK, LANES), 1)

        def chunk(k, carry):
            dal_acc, ddx_acc = carry
            c = ncl - 1 - k
            r0 = pl.multiple_of(c * CHUNK, CHUNK)
            xbc = xbc_ref[pl.ds(r0, CHUNK), :]
            dtc = dts_ref[pl.ds(r0, CHUNK), :]
            dyp = dyp_ref[pl.ds(r0, CHUNK), :]
            acs = _cumsum_rows(dtc * a_neg)
            acs_t = acs.T
            acs_x = _dot_exact(acs, e16)
            dt_x = _dot_exact(dtc, e16)
            xs = xbc[:, 0:1024]
            xd = xs * dt_x
            xd_bf = xd.astype(BF)
            dec_x = jnp.exp(acs_x[CHUNK - 1:CHUNK, :] - acs_x)
            xdd = xd * dec_x
            xdd_bf = xdd.astype(BF)
            dz = dyp * jnp.exp(acs_x)
            dz_bf = dz.astype(BF)
            ddx_acc = ddx_acc + jnp.sum(dyp * xs, axis=0, keepdims=True)
            dacs = jnp.zeros((CHUNK, LANES), F32)
            hsum = jnp.zeros((1, LANES), F32)
            p1_l, p2_l, p3_l = [], [], []
            for j in range(SSD_GROUPS):
                sl = slice(256 * j, 256 * (j + 1))
                bj = xbc[:, 1024 + 128 * j:1024 + 128 * (j + 1)].astype(BF)
                cj = xbc[:, 1536 + 128 * j:1536 + 128 * (j + 1)].astype(BF)
                g = _dot_nt(cj, bj)
                hj = st_ref[c, sl, :]
                hj_bf = hj.astype(BF)
                dhj = dh_ref[sl, :]
                dhj_bf = dhj.astype(BF)
                zj = _dot_nt(cj, hj_bf)
                qj = _dot_nt(bj, dhj_bf)
                lm = _group_decay(acs, acs_t, j, causal4)
                sc = jnp.concatenate([g] * 4, axis=0) * lm
                sc_bf = sc.astype(BF)
                dym = jnp.where(same, jnp.concatenate([dyp[:, sl]] * 4, axis=0), 0.0).astype(BF)
                dsc = _dot_nt(dym, xd_bf[:, sl])
                dxd = _dot_tn(sc_bf, dym) + qj * dec_x[:, sl]
                m = dsc * sc
                dg_bf = _fold_heads(dsc * lm).astype(BF)
                rs = jnp.sum(m, axis=1, keepdims=True)
                e2 = dhj * hj
                for hh in range(4):
                    oh = _onehot_lane(4 * j + hh)
                    dacs = dacs + oh * rs[CHUNK * hh:CHUNK * (hh + 1)]
                    hsum = hsum + oh * jnp.sum(jnp.sum(e2[64 * hh:64 * (hh + 1)], axis=0, keepdims=True), axis=1, keepdims=True)
                sel = (sel_rows + 4 * j == sel_lanes).astype(BF)
                hi = m.astype(BF)
                rem = m - hi.astype(F32)
                mid = rem.astype(BF)
                lo = (rem - mid.astype(F32)).astype(BF)
                dacs = dacs - (_dot_tn(hi, sel) + _dot_tn(mid, sel) + _dot_tn(lo, sel))
                p1_l.append(dz[:, sl] * zj)
                p2_l.append(qj * xdd[:, sl])
                p3_l.append(dxd * xs[:, sl])
                dxs_ref[pl.ds(r0, CHUNK), sl] = dd_x[:, sl] * dyp[:, sl] + dxd * dt_x[:, sl]
                dxs_ref[pl.ds(r0, CHUNK), 1536 + 128 * j:1536 + 128 * (j + 1)] = _dot(dg_bf, bj) + _dot(dz_bf[:, sl], hj_bf)
                dxs_ref[pl.ds(r0, CHUNK), 1024 + 128 * j:1024 + 128 * (j + 1)] = _dot_tn(dg_bf, cj) + _dot(xdd_bf[:, sl], dhj_bf)
                dh_ref[sl, :] = _group_last_decay(acs_t, j) * dhj + _dot_tn(dz_bf[:, sl], cj)
            stacked = jnp.concatenate([jnp.concatenate(p1_l, axis=1), jnp.concatenate(p2_l, axis=1), jnp.concatenate(p3_l, axis=1)], axis=0)
            red = _dot_exact(stacked, e16t)
            r1, r2, ddtc = red[0:CHUNK], red[CHUNK:2 * CHUNK], red[2 * CHUNK:3 * CHUNK]
            tot = jnp.sum(r2, axis=0, keepdims=True) + jnp.exp(acs[CHUNK - 1:CHUNK, :]) * hsum
            da = _rev_cumsum_rows(dacs + r1 - r2 + last_row * tot)
            ddts_ref[pl.ds(r0, CHUNK), :] = ddtc + da * a_neg
            dal_acc = dal_acc + jnp.sum(da * dtc, axis=0, keepdims=True)
            return dal_acc, ddx_acc

        dal_acc, ddx_acc = lax.fori_loop(0, ncl, chunk, (jnp.zeros((1, LANES), F32), jnp.zeros((1, D_MODEL), F32)))
        dal_ref[...] += dal_acc * a_neg
        ddd_ref[...] += _dot_exact(jnp.broadcast_to(ddx_acc, (8, D_MODEL)), e16t)[0:1, :]
        ddt_raw = ddts_ref[...] * _sigmoid(dt_ref[...] + dtb_ref[...])
        ddt_ref[...] = ddt_raw
        ddtb_ref[...] += jnp.sum(ddt_raw, axis=0, keepdims=True)

        pre = pre_ref[...]
        sp = _sigmoid(pre)
        dpre = dxs_ref[...] * sp * (1.0 + pre * (1.0 - sp))
        dp_ref[0:tl, :] = dpre
        dcb_ref[...] += jnp.sum(dpre, axis=0, keepdims=True)
        dx = jnp.zeros((tl, 2048), F32)
        for k in range(4):
            dcw_ref[k:k + 1, :] += jnp.sum(dpre * xp_ref[5 + k:5 + k + tl, :], axis=0, keepdims=True)
            dx = dx + cw_ref[k:k + 1, :] * dp_ref[3 - k:3 - k + tl, :]
        dxbc_ref[...] = dx.astype(BF)
        dp_ref[tl:tl + 8, :] = dp_ref[0:8, :]

    vec = lambda w: jax.ShapeDtypeStruct((1, w), F32)
    rrow = lambda w: pl.BlockSpec((tl, w), lambda i: (t_of(i), 0))
    return pl.pallas_call(
        body, name="ssd_bwd", grid=(nt,),
        in_specs=[rrow(D_MODEL), rrow(D_MODEL), rows(1024, 0), rows(1024, 1), rows(1024, 2), halo(1), halo(2), rows(LANES, 0),
                  pl.BlockSpec((ncl, 1024, SSD_STATE), lambda i: (t_of(i), 0, 0)),
                  _const_spec((4, 2048)), _const_spec((1, 2048)), _const_spec((1, LANES)), _const_spec((1, LANES)),
                  _const_spec((1, LANES)), _const_spec((1, D_MODEL))],
        out_specs=[rrow(2048), rrow(LANES), rrow(D_MODEL), _const_spec((8, 2048)), _const_spec((1, 2048)),
                   _const_spec((1, LANES)), _const_spec((1, LANES)), _const_spec((1, LANES)), _const_spec((1, D_MODEL))],
        out_shape=[jax.ShapeDtypeStruct((L, 2048), BF), jax.ShapeDtypeStruct((L, LANES), F32), jax.ShapeDtypeStruct((L, D_MODEL), BF),
                   jax.ShapeDtypeStruct((8, 2048), F32), vec(2048), vec(LANES), vec(LANES), vec(LANES), vec(D_MODEL)],
        scratch_shapes=[pltpu.VMEM((tl + 8, 2048), F32), pltpu.VMEM((tl, 2048), F32), pltpu.VMEM((tl, 2048), F32),
                        pltpu.VMEM((tl, LANES), F32), pltpu.VMEM((tl, D_MODEL), F32), pltpu.VMEM((tl, 2048), F32),
                        pltpu.VMEM((tl, LANES), F32), pltpu.VMEM((tl + 8, 2048), F32), pltpu.VMEM((1024, SSD_STATE), F32)],
        compiler_params=_params(("arbitrary",)),
    )(dyssd, ypre, proj, proj, proj, proj, proj, pdt, states, conv_w, conv_b, dt_bias, a_log, ssd_d, norm_w)


def _head_fwd_bwd(x, ys5, yssd, p, target, w_out, w_gate, w_proj, ple_nw, fin_nw):
    L = x.shape[0]
    tl = min(TOKEN_TILE, L)
    inv_d = 1.0 / D_MODEL

    def body(x_ref, ys_ref, yd_ref, p_ref, t_ref, wo_ref, wg_ref, wp_ref, pnw_ref, fnw_ref,
             loss_ref, dys_ref, dyd_ref, dh1_ref, n2_ref, dgl_ref, dpp_ref, dpnw_ref, dfnw_ref):
        @pl.when(pl.program_id(0) == 0)
        def _():
            loss_ref[...] = jnp.zeros_like(loss_ref)
            dpnw_ref[...] = jnp.zeros_like(dpnw_ref)
            dfnw_ref[...] = jnp.zeros_like(dfnw_ref)

        h1 = x_ref[...] + _dot(ys_ref[...], wo_ref[0:1024, :]) + _dot(yd_ref[...], wo_ref[1024:2048, :])
        r1 = lax.rsqrt(jnp.mean(h1 * h1, axis=-1, keepdims=True) + EPS)
        hh1 = h1 * r1
        n2 = (hh1 * pnw_ref[...]).astype(BF)
        gate = _sigmoid(_dot(n2, wg_ref[...]))
        pp = _dot(p_ref[...].astype(BF), wp_ref[...])
        h2 = h1 + pp * gate
        r2 = lax.rsqrt(jnp.mean(h2 * h2, axis=-1, keepdims=True) + EPS)
        hh2 = h2 * r2
        err = hh2 * fnw_ref[...] - t_ref[...]
        loss_ref[...] += 0.5 * inv_d * jnp.sum(err * err)
        dyo = err * inv_d
        dfnw_ref[...] += jnp.sum(dyo * hh2, axis=0, keepdims=True)
        g2 = dyo * fnw_ref[...]
        dh2 = r2 * (g2 - hh2 * jnp.mean(g2 * hh2, axis=-1, keepdims=True))
        dpp_ref[...] = (dh2 * gate).astype(BF)
        dgl = (dh2 * pp * gate * (1.0 - gate)).astype(BF)
        dgl_ref[...] = dgl
        n2_ref[...] = n2
        dn2 = _dot_nt(dgl, wg_ref[...])
        dpnw_ref[...] += jnp.sum(dn2 * hh1, axis=0, keepdims=True)
        g1 = dn2 * pnw_ref[...]
        dh1 = dh2 + r1 * (g1 - hh1 * jnp.mean(g1 * hh1, axis=-1, keepdims=True))
        dh1_ref[...] = dh1
        dh1_bf = dh1.astype(BF)
        dys_ref[...] = _dot_nt(dh1_bf, wo_ref[0:1024, :]).astype(BF)
        dyd_ref[...] = _dot_nt(dh1_bf, wo_ref[1024:2048, :])

    big = jax.ShapeDtypeStruct((L, D_MODEL), BF)
    vec = jax.ShapeDtypeStruct((1, D_MODEL), F32)
    return pl.pallas_call(
        body, name="head_fwd_bwd", grid=(L // tl,),
        in_specs=[_row_spec(tl, D_MODEL), _row_spec(tl, D_MODEL), _row_spec(tl, D_MODEL), _row_spec(tl, 256), _row_spec(tl, D_MODEL),
                  _const_spec((2048, D_MODEL)), _const_spec((D_MODEL, D_MODEL)), _const_spec((256, D_MODEL)),
                  _const_spec((1, D_MODEL)), _const_spec((1, D_MODEL))],
        out_specs=[_const_spec((8, LANES)), _row_spec(tl, D_MODEL), _row_spec(tl, D_MODEL), _row_spec(tl, D_MODEL),
                   _row_spec(tl, D_MODEL), _row_spec(tl, D_MODEL), _row_spec(tl, D_MODEL), _const_spec((1, D_MODEL)), _const_spec((1, D_MODEL))],
        out_shape=[jax.ShapeDtypeStruct((8, LANES), F32), big, jax.ShapeDtypeStruct((L, D_MODEL), F32),
                   jax.ShapeDtypeStruct((L, D_MODEL), F32), big, big, big, vec, vec],
        compiler_params=_params(("arbitrary",)),
    )(x, ys5, yssd, p, target, w_out, w_gate, w_proj, ple_nw, fin_nw)


def _pad_lanes(v):
    return jnp.pad(v.reshape(1, -1), ((0, 0), (0, LANES - v.size)))


def _local_step(x, p, target, w):
    L = x.shape[0]
    nc = L // CHUNK
    nsteps = max(1, (nc - 1).bit_length())
    w_in = w["w_in"]
    w_main = w_in[:, :D_MAIN]
    w_dt = jnp.pad(w_in[:, D_MAIN:], ((0, 0), (0, LANES - SSD_HEADS)))
    norm_w = w["norm_w"].reshape(1, -1)
    s5_d = w["s5_D"].reshape(1, -1)
    b_glu = w["s5_b_glu"].reshape(1, -1)
    conv_b = w["conv_b"].reshape(1, -1)
    dtb, alog, ssd_d = _pad_lanes(w["dt_bias"]), _pad_lanes(w["A_log"]), _pad_lanes(w["ssd_D"])
    ssd_nw = w["ssd_norm_w"].reshape(1, -1)
    ple_nw = w["ple_norm_w"].reshape(1, -1)
    fin_nw = w["final_norm_w"].reshape(1, -1)

    s5_args = (w["s5_A_re"], w["s5_A_im"], w["s5_log_dt"], w["s5_B_re"], w["s5_B_im"], w["s5_C_re"], w["s5_C_im"])
    small, small_vjp = jax.vjp(_s5_discretise, *s5_args)
    kmat, kmat_vjp = jax.vjp(_s5_kmat, *small)
    wst, woff, bbp, pwr, cx, px = _s5_big_tables(*small)
    p1, p2 = _s5_scan_powers(w["s5_A_re"], w["s5_A_im"], w["s5_log_dt"], nsteps)
    wst_bf, woff_bf = wst.astype(BF), woff.astype(BF)

    hn, proj, pssd, pdt = _in_proj_fwd(x, norm_w, w_main, w_dt)
    uflat = _flat_hs(proj[:, :D_MODEL], nc)
    yflat, hsave = _s5_core_fwd(uflat, kmat, wst_bf, woff_bf, p1, p2)
    yssm = _unflat_tk(yflat, nc)
    ys5 = _s5_post_fwd(yssm, proj, s5_d, w["s5_w_glu"], b_glu)
    yssd, ypre, states = _ssd_fwd(pssd, pdt, w["conv_w"], conv_b, dtb, alog, ssd_d, ssd_nw)
    (loss8, dys5, dyssd, dh1, n2, dgl2, dpp, g_ple_nw, g_fin_nw) = _head_fwd_bwd(
        x, ys5, yssd, p, target, w["w_out"], w["w_ple_gate"], w["w_ple_proj"], ple_nw, fin_nw)

    (dxbc, ddt, dzd, g_cw, g_cb, g_dtb, g_alog, g_ssd_d, g_ssd_nw) = _ssd_bwd(
        dyssd, ypre, pssd, pdt, states, w["conv_w"], conv_b, dtb, alog, ssd_d, ssd_nw)
    dzs, dyssm, a_glu, dgl1, g_bglu, g_s5d = _s5_post_bwd(dys5, yssm, proj, s5_d, w["s5_w_glu"], b_glu)
    duflat, dkmat, r12, q12, dc, dpx, da8 = _s5_core_bwd(uflat, _flat_tk(dyssm, nc), hsave, kmat, wst_bf, woff_bf, p1, p2,
                                                         bbp, pwr, cx, px)
    da64 = jnp.concatenate([da8[:, 0, :S5_STATE] + da8[:, 0, S5_STATE:], da8[:, 1, S5_STATE:] - da8[:, 1, :S5_STATE]], axis=-1)
    d_small = [a + b for a, b in zip(_s5_small_cotangents(r12, q12, dc, dpx, da64), kmat_vjp(dkmat))]
    g_s5 = small_vjp(tuple(d_small))
    gx, du, g_norm_w = _in_proj_bwd(x, norm_w, dh1, _unflat_hs(duflat, nc), dyssm, s5_d, dzs, dzd, dxbc, ddt, w_main, w_dt)

    g_w_in = jnp.concatenate([
        _matmul_tn(hn, du, "dw_in_u"), _matmul_tn(hn, dzs, "dw_in_zs"), _matmul_tn(hn, dzd, "dw_in_zd"),
        _matmul_tn(hn, dxbc, "dw_in_xbc"), _matmul_tn(hn, ddt, "dw_in_dt")[:, :SSD_HEADS]], axis=1)
    grads = {
        "norm_w": g_norm_w, "w_in": g_w_in,
        "s5_A_re": g_s5[0], "s5_A_im": g_s5[1], "s5_log_dt": g_s5[2], "s5_B_re": g_s5[3], "s5_B_im": g_s5[4],
        "s5_C_re": g_s5[5], "s5_C_im": g_s5[6], "s5_D": g_s5d, "s5_w_glu": _matmul_tn(a_glu, dgl1, "dw_glu"), "s5_b_glu": g_bglu,
        "conv_w": g_cw[:4], "conv_b": g_cb, "dt_bias": g_dtb[:, :SSD_HEADS], "A_log": g_alog[:, :SSD_HEADS],
        "ssd_D": g_ssd_d[:, :SSD_HEADS], "ssd_norm_w": g_ssd_nw,
        "w_out": jnp.concatenate([_matmul_tn(ys5, dh1, "dw_out_s5"), _matmul_tn(yssd, dh1, "dw_out_ssd")], axis=0),
        "ple_norm_w": g_ple_nw, "w_ple_gate": _matmul_tn(n2, dgl2, "dw_gate"), "w_ple_proj": _matmul_tn(p, dpp, "dw_proj"),
        "final_norm_w": g_fin_nw,
    }
    return loss8[0, 0], gx, grads


WEIGHTS = ("norm_w", "w_in", "s5_A_re", "s5_A_im", "s5_log_dt", "s5_B_re", "s5_B_im", "s5_C_re", "s5_C_im", "s5_D", "s5_w_glu",
           "s5_b_glu", "conv_w", "conv_b", "dt_bias", "A_log", "ssd_D", "ssd_norm_w", "w_out", "ple_norm_w", "w_ple_gate",
           "w_ple_proj", "final_norm_w")
BIG = {"w_in": ((1024, 1284), 1), "s5_w_glu": ((256, 1024), 0), "w_out": ((512, 1024), 0), "w_ple_gate": ((256, 1024), 0),
       "w_ple_proj": ((256, 256), 1)}
SMALL = {"norm_w": (1024,), "s5_A_re": (64, 64), "s5_A_im": (64, 64), "s5_log_dt": (64,), "s5_B_re": (64, 64, 16),
         "s5_B_im": (64, 64, 16), "s5_C_re": (64, 16, 64), "s5_C_im": (64, 16, 64), "s5_D": (1024,), "s5_b_glu": (1024,),
         "conv_w": (4, 2048), "conv_b": (2048,), "dt_bias": (16,), "A_log": (16,), "ssd_D": (16,), "ssd_norm_w": (1024,),
         "ple_norm_w": (1024,), "final_norm_w": (1024,)}
BIG_ROWS = {n: s[0] * s[1] // LANES for n, (s, _) in BIG.items()}
BIG_ROWS_TOTAL = sum(BIG_ROWS.values())
SMALL_TOTAL = sum(math.prod(s) for s in SMALL.values())
SMALL_PIECE_ROWS = -(-SMALL_TOTAL // (N_CHIPS * 16 * LANES)) * 16
HALF_ROWS = (BIG_ROWS_TOTAL + SMALL_PIECE_ROWS) // 2
SMALL_ROW0 = BIG_ROWS_TOTAL - HALF_ROWS


def _mesh_pos():
    return lax.axis_index("x"), lax.axis_index("y"), lax.axis_index("c")


def _other_chips(x, y):
    return [(1 - x, y), (x, 1 - y), (1 - x, 1 - y)]


def _comm_params():
    return pltpu.CompilerParams(has_side_effects=True)


def _all_gather_chips(wpack, cw):
    half = wpack.shape[0] // 2

    def body(w_ref, c_ref, wo_ref, co_ref, send_sems, recv_sems, fwd_send, fwd_recv, loc_sems):
        x, y, c = _mesh_pos()
        me = 2 * x + y
        sib = (x, y, 1 - c)
        mine = pl.ds(c * half, half)
        theirs = pl.ds((1 - c) * half, half)
        others = _other_chips(x, y)
        loc = [pltpu.make_async_copy(c_ref, co_ref.at[me], loc_sems.at[0])]
        for cp in loc:
            cp.start()

        def from_chip(k, chip, dev):
            return pltpu.make_async_remote_copy(w_ref.at[mine], wo_ref.at[chip, mine], send_sems.at[2 * k], recv_sems.at[2 * k],
                                                device_id=dev, device_id_type=MESH)

        def conv_from(k, chip, dev):
            return pltpu.make_async_remote_copy(c_ref, co_ref.at[chip], send_sems.at[2 * k + 1], recv_sems.at[2 * k + 1],
                                                device_id=dev, device_id_type=MESH)

        def passed(k, chip, rows):
            return pltpu.make_async_remote_copy(wo_ref.at[chip, rows], wo_ref.at[chip, rows], fwd_send.at[k], fwd_recv.at[k],
                                                device_id=sib, device_id_type=MESH)

        sends = []
        for k, (px, py) in enumerate(others):
            sends += [from_chip(k, me, (px, py, c)), conv_from(k, me, (px, py, c))]
        for cp in sends:
            cp.start()
        fwds = []
        for k, (px, py) in enumerate(others):
            chip = 2 * px + py
            from_chip(k, chip, (px, py, c)).wait_recv()
            fwds.append(passed(k, chip, mine))
            fwds[-1].start()
        for k, (px, py) in enumerate(others):
            chip = 2 * px + py
            passed(k, chip, theirs).wait_recv()
            conv_from(k, chip, (px, py, c)).wait_recv()
        for cp in sends + fwds:
            cp.wait_send()
        for cp in loc:
            cp.wait()

    return pl.pallas_call(
        body, name="all_gather_weights", in_specs=[ANY, ANY], out_specs=[ANY, ANY],
        out_shape=[jax.ShapeDtypeStruct((N_CHIPS,) + wpack.shape, wpack.dtype), jax.ShapeDtypeStruct((N_CHIPS,) + cw.shape, cw.dtype)],
        scratch_shapes=[pltpu.SemaphoreType.DMA((6,)), pltpu.SemaphoreType.DMA((6,)), pltpu.SemaphoreType.DMA((3,)),
                        pltpu.SemaphoreType.DMA((3,)), pltpu.SemaphoreType.DMA((1,))],
        compiler_params=_comm_params(),
    )(wpack, cw)


def _exchange_pair(gp):
    def body(g_ref, r_ref, send_sems, recv_sems):
        x, y, c = _mesh_pos()
        cps = [pltpu.make_async_remote_copy(g_ref.at[s, 1 - c], r_ref.at[s], send_sems.at[s], recv_sems.at[s],
                                            device_id=(x, y, 1 - c), device_id_type=MESH) for s in range(N_CHIPS)]
        for cp in cps:
            cp.start()
        for cp in cps:
            cp.wait()

    return pl.pallas_call(
        body, name="grad_exchange_pair", in_specs=[ANY], out_specs=ANY,
        out_shape=jax.ShapeDtypeStruct((N_CHIPS,) + gp.shape[2:], gp.dtype),
        scratch_shapes=[pltpu.SemaphoreType.DMA((N_CHIPS,)), pltpu.SemaphoreType.DMA((N_CHIPS,))],
        compiler_params=_comm_params(),
    )(gp)


def _pair_sum(mine, from_sibling):
    def body(a_ref, b_ref, bf_ref, tail_ref):
        s = a_ref[0] + b_ref[0]
        bf_ref[0] = s.astype(BF)
        tail_ref[0] = s[SMALL_ROW0:, :]

    piece = pl.BlockSpec((1, HALF_ROWS, LANES), lambda i: (i, 0, 0))
    return pl.pallas_call(
        body, name="grad_pair_sum", grid=(N_CHIPS,), in_specs=[piece, piece],
        out_specs=[piece, pl.BlockSpec((1, SMALL_PIECE_ROWS, LANES), lambda i: (i, 0, 0))],
        out_shape=[jax.ShapeDtypeStruct((N_CHIPS, HALF_ROWS, LANES), BF), jax.ShapeDtypeStruct((N_CHIPS, SMALL_PIECE_ROWS, LANES), F32)],
        compiler_params=_params(("parallel",)),
    )(mine, from_sibling)


def _exchange_chips(ps_bf, ps_tail):
    def body(p_ref, t_ref, r_ref, rt_ref, send_sems, recv_sems):
        x, y, c = _mesh_pos()
        cps = []
        for k, (px, py) in enumerate(_other_chips(x, y)):
            cps.append(pltpu.make_async_remote_copy(p_ref.at[2 * px + py], r_ref.at[k], send_sems.at[2 * k], recv_sems.at[2 * k],
                                                    device_id=(px, py, c), device_id_type=MESH))
            cps.append(pltpu.make_async_remote_copy(t_ref.at[2 * px + py], rt_ref.at[k], send_sems.at[2 * k + 1],
                                                    recv_sems.at[2 * k + 1], device_id=(px, py, c), device_id_type=MESH))
        for cp in cps:
            cp.start()
        for cp in cps:
            cp.wait()

    return pl.pallas_call(
        body, name="grad_exchange_chips", in_specs=[ANY, ANY], out_specs=[ANY, ANY],
        out_shape=[jax.ShapeDtypeStruct((N_CHIPS - 1,) + ps_bf.shape[1:], ps_bf.dtype),
                   jax.ShapeDtypeStruct((N_CHIPS - 1,) + ps_tail.shape[1:], ps_tail.dtype)],
        scratch_shapes=[pltpu.SemaphoreType.DMA((6,)), pltpu.SemaphoreType.DMA((6,))],
        compiler_params=_comm_params(),
    )(ps_bf, ps_tail)


def _chip_sum(own_bf, own_tail, others_bf, others_tail):
    def body(ob_ref, ot_ref, b_ref, t_ref, o_ref):
        acc = ob_ref[0:SMALL_ROW0, :].astype(F32)
        tail = ot_ref[...]
        for k in range(N_CHIPS - 1):
            acc = acc + b_ref[k, 0:SMALL_ROW0, :].astype(F32)
            tail = tail + t_ref[k]
        o_ref[0:SMALL_ROW0, :] = acc
        o_ref[SMALL_ROW0:, :] = tail

    return pl.pallas_call(
        body, name="grad_chip_sum", out_shape=jax.ShapeDtypeStruct((HALF_ROWS, LANES), F32),
        compiler_params=_params(),
    )(own_bf, own_tail, others_bf, others_tail)


def _swap_reduced_halves(gh):
    def body(g_ref, o_ref, send_sem, recv_sem):
        x, y, c = _mesh_pos()
        cp = pltpu.make_async_remote_copy(g_ref, o_ref, send_sem, recv_sem, device_id=(x, y, 1 - c), device_id_type=MESH)
        cp.start()
        cp.wait()

    return pl.pallas_call(
        body, name="grad_swap_halves", in_specs=[ANY], out_specs=ANY,
        out_shape=jax.ShapeDtypeStruct(gh.shape, gh.dtype),
        scratch_shapes=[pltpu.SemaphoreType.DMA, pltpu.SemaphoreType.DMA],
        compiler_params=_comm_params(),
    )(gh)


def _gather_small(second_half):
    def body(gs_ref, sm_ref, send_sems, recv_sems, loc_sem):
        x, y, c = _mesh_pos()
        me = 2 * x + y
        small = gs_ref.at[pl.ds(SMALL_ROW0, SMALL_PIECE_ROWS)]
        loc = pltpu.make_async_copy(small, sm_ref.at[me], loc_sem)
        loc.start()
        cps = [pltpu.make_async_remote_copy(small, sm_ref.at[me], send_sems.at[k], recv_sems.at[k],
                                            device_id=(px, py, c), device_id_type=MESH)
               for k, (px, py) in enumerate(_other_chips(x, y))]
        for cp in cps:
            cp.start()
        for cp in cps:
            cp.wait()
        loc.wait()

    return pl.pallas_call(
        body, name="grad_gather_small", in_specs=[ANY], out_specs=ANY,
        out_shape=jax.ShapeDtypeStruct((N_CHIPS, SMALL_PIECE_ROWS, LANES), second_half.dtype),
        scratch_shapes=[pltpu.SemaphoreType.DMA((3,)), pltpu.SemaphoreType.DMA((3,)), pltpu.SemaphoreType.DMA],
        compiler_params=_comm_params(),
    )(second_half)


def _pack_grads(grads):
    small = jnp.concatenate([grads[n].reshape(-1) for n in SMALL])
    small = jnp.pad(small, (0, N_CHIPS * SMALL_PIECE_ROWS * LANES - SMALL_TOTAL)).reshape(N_CHIPS, SMALL_PIECE_ROWS, LANES)
    pieces = []
    for s in range(N_CHIPS):
        rows = []
        for n, (shp, axis) in BIG.items():
            g = grads[n]
            blk = g[s * shp[0]:(s + 1) * shp[0], :] if axis == 0 else g[:, s * shp[1]:(s + 1) * shp[1]]
            rows.append(blk.reshape(-1, LANES))
        rows.append(small[s])
        pieces.append(jnp.concatenate(rows, axis=0).reshape(2, HALF_ROWS, LANES))
    return jnp.stack(pieces)


def _unpack_shard(first_half, second_half):
    rows = jnp.concatenate([first_half, second_half], axis=0)
    out, r0 = {}, 0
    for n, (shp, _) in BIG.items():
        out[n] = rows[r0:r0 + BIG_ROWS[n]].reshape(shp)
        r0 += BIG_ROWS[n]
    return out


def _unpack_small(sm):
    flat = sm.reshape(-1)
    out, o = {}, 0
    for n, shp in SMALL.items():
        k = math.prod(shp)
        out[n] = flat[o:o + k].reshape(shp)
        o += k
    return out


def _as_2d(a):
    n = a.size
    if a.ndim >= 2 and a.shape[-1] > 1024:
        return a.reshape(-1, a.shape[-1])
    if n % 1024 == 0:
        return a.reshape(n // 1024, 1024)
    return a.reshape(1, n)


def _adamw(w, g, m, v, name):
    shape = w.shape
    w2, g2, m2, v2 = (_as_2d(a) for a in (w, g, m, v))
    rows, cols = w2.shape
    rb = 256 if rows >= 512 else rows

    def body(w_ref, g_ref, m_ref, v_ref, d_ref, mo_ref, vo_ref):
        gv = g_ref[...]
        mn = ADAM_B1 * m_ref[...] + (1.0 - ADAM_B1) * gv
        vn = ADAM_B2 * v_ref[...] + (1.0 - ADAM_B2) * (gv * gv)
        m_hat = mn / (1.0 - ADAM_B1 ** ADAM_STEP)
        v_hat = vn / (1.0 - ADAM_B2 ** ADAM_STEP)
        d_ref[...] = -ADAM_LR * (m_hat / (jnp.sqrt(v_hat) + ADAM_EPS) + ADAM_WD * w_ref[...])
        mo_ref[...] = mn
        vo_ref[...] = vn

    spec = _row_spec(rb, cols)
    sds = jax.ShapeDtypeStruct((rows, cols), F32)
    d, mo, vo = pl.pallas_call(
        body, name=name, grid=(rows // rb,), in_specs=[spec] * 4, out_specs=[spec] * 3, out_shape=[sds] * 3,
        compiler_params=_params(("parallel",)),
    )(w2, g2, m2, v2)
    return d.reshape(shape), mo.reshape(shape), vo.reshape(shape)


def kernel(x, p, norm_w, w_in, s5_A_re, s5_A_im, s5_log_dt, s5_B_re, s5_B_im, s5_C_re, s5_C_im, s5_D, s5_w_glu, s5_b_glu, conv_w, conv_b, dt_bias, A_log, ssd_D, ssd_norm_w, w_out, ple_norm_w, w_ple_gate, w_ple_proj, final_norm_w, loss_target, m_norm_w, m_w_in, m_s5_A_re, m_s5_A_im, m_s5_log_dt, m_s5_B_re, m_s5_B_im, m_s5_C_re, m_s5_C_im, m_s5_D, m_s5_w_glu, m_s5_b_glu, m_conv_w, m_conv_b, m_dt_bias, m_A_log, m_ssd_D, m_ssd_norm_w, m_w_out, m_ple_norm_w, m_w_ple_gate, m_w_ple_proj, m_final_norm_w, v_norm_w, v_w_in, v_s5_A_re, v_s5_A_im, v_s5_log_dt, v_s5_B_re, v_s5_B_im, v_s5_C_re, v_s5_C_im, v_s5_D, v_s5_w_glu, v_s5_b_glu, v_conv_w, v_conv_b, v_dt_bias, v_A_log, v_ssd_D, v_ssd_norm_w, v_w_out, v_ple_norm_w, v_w_ple_gate, v_w_ple_proj, v_final_norm_w):
    given = (norm_w, w_in, s5_A_re, s5_A_im, s5_log_dt, s5_B_re, s5_B_im, s5_C_re, s5_C_im, s5_D, s5_w_glu, s5_b_glu, conv_w, conv_b,
             dt_bias, A_log, ssd_D, ssd_norm_w, w_out, ple_norm_w, w_ple_gate, w_ple_proj, final_norm_w)
    given_m = (m_norm_w, m_w_in, m_s5_A_re, m_s5_A_im, m_s5_log_dt, m_s5_B_re, m_s5_B_im, m_s5_C_re, m_s5_C_im, m_s5_D, m_s5_w_glu,
               m_s5_b_glu, m_conv_w, m_conv_b, m_dt_bias, m_A_log, m_ssd_D, m_ssd_norm_w, m_w_out, m_ple_norm_w, m_w_ple_gate,
               m_w_ple_proj, m_final_norm_w)
    given_v = (v_norm_w, v_w_in, v_s5_A_re, v_s5_A_im, v_s5_log_dt, v_s5_B_re, v_s5_B_im, v_s5_C_re, v_s5_C_im, v_s5_D, v_s5_w_glu,
               v_s5_b_glu, v_conv_w, v_conv_b, v_dt_bias, v_A_log, v_ssd_D, v_ssd_norm_w, v_w_out, v_ple_norm_w, v_w_ple_gate,
               v_w_ple_proj, v_final_norm_w)
    wts, mom, var = dict(zip(WEIGHTS, given)), dict(zip(WEIGHTS, given_m)), dict(zip(WEIGHTS, given_v))
    drop = lambda n, a: a if n == "final_norm_w" else a[0]

    wpack = jnp.concatenate([drop(n, wts[n]).astype(BF).reshape(-1, LANES) for n in BIG], axis=0)
    wall, cwall = _all_gather_chips(wpack, drop("conv_w", wts["conv_w"]))
    chip = 2 * lax.axis_index("x") + lax.axis_index("y")
    is_own = (lax.broadcasted_iota(jnp.int32, (N_CHIPS, 1, 1), 0) == chip)
    full, r0 = {}, 0
    for n, (shp, axis) in BIG.items():
        blk = jnp.where(is_own, drop(n, wts[n]).astype(BF)[None], wall[:, r0:r0 + BIG_ROWS[n]].reshape((N_CHIPS,) + shp))
        full[n] = blk.reshape(N_CHIPS * shp[0], shp[1]) if axis == 0 else blk.transpose(1, 0, 2).reshape(shp[0], N_CHIPS * shp[1])
        r0 += BIG_ROWS[n]
    for n in SMALL:
        full[n] = drop(n, wts[n])
    full["conv_w"] = cwall.transpose(1, 0, 2).reshape(4, 2048)

    loss, gx, grads = _local_step(x[0], p[0, 0], loss_target[0], full)
    loss = lax.psum(loss, MESH_AXES)

    gp = _pack_grads({n: grads[n].reshape(SMALL[n]) if n in SMALL else grads[n] for n in WEIGHTS})
    c = lax.axis_index("c")
    from_sibling = _exchange_pair(gp)
    mine = lax.dynamic_index_in_dim(gp, c, axis=1, keepdims=False)
    ps_bf, ps_tail = _pair_sum(mine, from_sibling)
    others_bf, others_tail = _exchange_chips(ps_bf, ps_tail)
    own_bf = lax.dynamic_index_in_dim(ps_bf, chip, axis=0, keepdims=False)
    own_tail = lax.dynamic_index_in_dim(ps_tail, chip, axis=0, keepdims=False)
    reduced_half = _chip_sum(own_bf, own_tail, others_bf, others_tail)
    sibling_half = _swap_reduced_halves(reduced_half)
    first_half = jnp.where(c == 0, reduced_half, sibling_half)
    second_half = jnp.where(c == 0, sibling_half, reduced_half)
    sm = _gather_small(second_half)
    g_final = {**_unpack_small(sm), **_unpack_shard(first_half, second_half)}
    g_final["conv_w"] = lax.dynamic_slice_in_dim(g_final["conv_w"], chip * 512, 512, axis=1)

    outs_g, outs_d, outs_m, outs_v = [], [], [], []
    for n in WEIGHTS:
        g = g_final[n].reshape(wts[n].shape)
        d, mo, vo = _adamw(wts[n], g, mom[n], var[n], "adamw_" + n)
        outs_g.append(g)
        outs_d.append(d)
        outs_m.append(mo)
        outs_v.append(vo)
    return (loss, gx[None], *outs_g, *outs_d, *outs_m, *outs_v)
```

```python
import functools
import math

import jax
import jax.numpy as jnp
from jax import lax
from jax.experimental import pallas as pl
from jax.experimental.pallas import tpu as pltpu

F32 = jnp.float32
BF = jnp.bfloat16
EPS = 1e-6
CHUNK = 64
D_MODEL = 1024
S5_GROUPS = 64
S5_CH = 16
S5_STATE = 64
SSD_HEADS = 16
SSD_HEAD_DIM = 64
SSD_GROUPS = 4
SSD_STATE = 128
D_MAIN = 5120
LANES = 128
TOKEN_TILE = 256
VMEM_LIMIT = 56 * 1024 * 1024
MESH_AXES = ("x", "y", "c")
N_CHIPS = 4
ADAM_LR, ADAM_B1, ADAM_B2, ADAM_EPS, ADAM_WD, ADAM_STEP = 0.001, 0.9, 0.999, 1e-08, 0.01, 10
MESH = pl.DeviceIdType.MESH
ANY = pl.BlockSpec(memory_space=pl.ANY)


def _dot(a, b):
    return jnp.dot(a, b, preferred_element_type=F32)


def _dot_nt(a, b):
    return lax.dot_general(a, b, (((1,), (1,)), ((), ())), preferred_element_type=F32)


def _dot_tn(a, b):
    return lax.dot_general(a, b, (((0,), (0,)), ((), ())), preferred_element_type=F32)


def _sigmoid(x):
    return 1.0 / (1.0 + jnp.exp(-x))


def _softplus(x):
    return jnp.maximum(x, 0.0) + jnp.log(1.0 + jnp.exp(-jnp.abs(x)))


_GELU_C = math.sqrt(2.0 / math.pi)


def _gelu(x):
    return 0.5 * x * (1.0 + jnp.tanh(_GELU_C * (x + 0.044715 * x * x * x)))


def _gelu_grad(x):
    th = jnp.tanh(_GELU_C * (x + 0.044715 * x * x * x))
    return 0.5 * (1.0 + th) + 0.5 * x * (1.0 - th * th) * _GELU_C * (1.0 + 3.0 * 0.044715 * x * x)


def _params(sem=None):
    return pltpu.CompilerParams(dimension_semantics=sem, vmem_limit_bytes=VMEM_LIMIT)


def _row_spec(tl, width, col=0):
    return pl.BlockSpec((tl, width), lambda i, col=col: (i, col))


def _const_spec(shape):
    nd = len(shape)
    return pl.BlockSpec(shape, lambda *_: (0,) * nd)


def _in_proj_fwd(x, norm_w, w_main, w_dt):
    L = x.shape[0]
    tl = min(TOKEN_TILE, L)

    def body(x_ref, nw_ref, wm_ref, wd_ref, hn_ref, ps5_ref, pssd_ref, pd_ref):
        xv = x_ref[...]
        r = lax.rsqrt(jnp.mean(xv * xv, axis=-1, keepdims=True) + EPS)
        hn = (xv * r * nw_ref[...]).astype(BF)
        hn_ref[...] = hn
        for j in range(2):
            ps5_ref[:, j * 1024:(j + 1) * 1024] = _dot(hn, wm_ref[:, j * 1024:(j + 1) * 1024]).astype(BF)
        for j in range(3):
            pssd_ref[:, j * 1024:(j + 1) * 1024] = _dot(hn, wm_ref[:, (j + 2) * 1024:(j + 3) * 1024])
        pd_ref[...] = _dot(hn, wd_ref[...])

    return pl.pallas_call(
        body, name="in_proj_fwd", grid=(L // tl,),
        in_specs=[_row_spec(tl, D_MODEL), _const_spec((1, D_MODEL)), _const_spec((D_MODEL, D_MAIN)), _const_spec((D_MODEL, LANES))],
        out_specs=[_row_spec(tl, D_MODEL), _row_spec(tl, 2048), _row_spec(tl, 3072), _row_spec(tl, LANES)],
        out_shape=[jax.ShapeDtypeStruct((L, D_MODEL), BF), jax.ShapeDtypeStruct((L, 2048), BF), jax.ShapeDtypeStruct((L, 3072), F32),
                   jax.ShapeDtypeStruct((L, LANES), F32)],
        compiler_params=_params(("arbitrary",)),
    )(x, norm_w, w_main, w_dt)


def _in_proj_bwd(x, norm_w, dh1, du_flat, dyssm, s5_d, dzs, dzd, dxbc, ddt, w_main, w_dt):
    L = x.shape[0]
    tl = min(TOKEN_TILE, L)

    def body(x_ref, nw_ref, dh1_ref, duf_ref, dys_ref, d_ref, dzs_ref, dzd_ref, dxbc_ref, ddt_ref, wm_ref, wd_ref,
             gx_ref, du_ref, gnw_ref):
        @pl.when(pl.program_id(0) == 0)
        def _():
            gnw_ref[...] = jnp.zeros_like(gnw_ref)

        du = (duf_ref[...].astype(F32) + dys_ref[...].astype(F32) * d_ref[...]).astype(BF)
        du_ref[...] = du
        dhn = _dot_nt(du, wm_ref[:, 0:1024])
        dhn += _dot_nt(dzs_ref[...], wm_ref[:, 1024:2048])
        dhn += _dot_nt(dzd_ref[...], wm_ref[:, 2048:3072])
        dhn += _dot_nt(dxbc_ref[...], wm_ref[:, 3072:5120])
        dhn += _dot_nt(ddt_ref[...].astype(BF), wd_ref[...])
        xv = x_ref[...]
        r = lax.rsqrt(jnp.mean(xv * xv, axis=-1, keepdims=True) + EPS)
        xh = xv * r
        gnw_ref[...] += jnp.sum(dhn * xh, axis=0, keepdims=True)
        g = dhn * nw_ref[...]
        gx_ref[...] = dh1_ref[...] + r * (g - xh * jnp.mean(g * xh, axis=-1, keepdims=True))

    return pl.pallas_call(
        body, name="in_proj_bwd", grid=(L // tl,),
        in_specs=[_row_spec(tl, D_MODEL), _const_spec((1, D_MODEL)), _row_spec(tl, D_MODEL), _row_spec(tl, D_MODEL),
                  _row_spec(tl, D_MODEL), _const_spec((1, D_MODEL)), _row_spec(tl, D_MODEL), _row_spec(tl, D_MODEL),
                  _row_spec(tl, 2048), _row_spec(tl, LANES), _const_spec((D_MODEL, D_MAIN)), _const_spec((D_MODEL, LANES))],
        out_specs=[_row_spec(tl, D_MODEL), _row_spec(tl, D_MODEL), _const_spec((1, D_MODEL))],
        out_shape=[jax.ShapeDtypeStruct((L, D_MODEL), F32), jax.ShapeDtypeStruct((L, D_MODEL), BF), jax.ShapeDtypeStruct((1, D_MODEL), F32)],
        compiler_params=_params(("arbitrary",)),
    )(x, norm_w, dh1, du_flat, dyssm, s5_d, dzs, dzd, dxbc, ddt, w_main, w_dt)


def _matmul_tn(a, b, name):
    L, M = a.shape
    N = b.shape[1]
    tm, tn, tk = min(M, 1024), min(N, 1024), min(L, 512)

    def body(a_ref, b_ref, o_ref):
        @pl.when(pl.program_id(2) == 0)
        def _():
            o_ref[...] = jnp.zeros_like(o_ref)

        o_ref[...] += _dot_tn(a_ref[...].astype(BF), b_ref[...].astype(BF))

    return pl.pallas_call(
        body, name=name, grid=(M // tm, N // tn, L // tk),
        in_specs=[pl.BlockSpec((tk, tm), lambda i, j, k: (k, i)), pl.BlockSpec((tk, tn), lambda i, j, k: (k, j))],
        out_specs=pl.BlockSpec((tm, tn), lambda i, j, k: (i, j)),
        out_shape=jax.ShapeDtypeStruct((M, N), F32),
        compiler_params=_params(("parallel", "parallel", "arbitrary")),
    )(a, b)


def _s5_discretise(a_re, a_im, log_dt, b_re, b_im, c_re, c_im):
    dt = jnp.exp(log_dt)[:, None]
    tau = jnp.arange(CHUNK + 1, dtype=F32)
    mag = jnp.exp((a_re * dt)[:, :, None] * tau)
    ang = (a_im * dt)[:, :, None] * tau
    pw_re, pw_im = mag * jnp.cos(ang), mag * jnp.sin(ang)
    er, ei = pw_re[:, :, 1] - 1.0, pw_im[:, :, 1]
    den = a_re * a_re + a_im * a_im
    beta_re, beta_im = (er * a_re + ei * a_im) / den, (ei * a_re - er * a_im) / den
    bb_re = (beta_re[:, :, None] * b_re - beta_im[:, :, None] * b_im).transpose(0, 2, 1)
    bb_im = (beta_re[:, :, None] * b_im + beta_im[:, :, None] * b_re).transpose(0, 2, 1)
    return bb_re, bb_im, c_re, c_im, pw_re, pw_im


def _s5_kmat(bb_re, bb_im, c_re, c_im, pw_re, pw_im):
    hi = lax.Precision.HIGHEST
    m_re = bb_re[:, :, None, :] * c_re[:, None, :, :] - bb_im[:, :, None, :] * c_im[:, None, :, :]
    m_im = bb_re[:, :, None, :] * c_im[:, None, :, :] + bb_im[:, :, None, :] * c_re[:, None, :, :]
    k4 = (jnp.einsum("ghkn,gnt->ghtk", m_re, pw_re[:, :, :CHUNK], precision=hi)
          - jnp.einsum("ghkn,gnt->ghtk", m_im, pw_im[:, :, :CHUNK], precision=hi))
    return k4.reshape(S5_GROUPS, S5_CH, CHUNK * S5_CH)


def _s5_table_factors(bb_re, bb_im, c_re, c_im, pw_re, pw_im):
    pr = pw_re[:, :, CHUNK - 1::-1].transpose(0, 2, 1)
    pi = pw_im[:, :, CHUNK - 1::-1].transpose(0, 2, 1)
    bbp = jnp.concatenate([bb_re, bb_im], axis=-1)
    pwr = jnp.concatenate([pr, pi], axis=-1)
    pwn = jnp.concatenate([pw_re[:, :, 1:CHUNK + 1], pw_im[:, :, 1:CHUNK + 1]], axis=1)
    cn = jnp.concatenate([c_re.transpose(0, 2, 1), c_im.transpose(0, 2, 1)], axis=1)
    return bbp, pwr, pwn, cn


def _s5_small_cotangents(r12, q12, dc, dpx, da64):
    n = S5_STATE
    fold = lambda a: a[..., :n] + a[..., n:]
    fold_m = lambda a: a[..., n:] - a[..., :n]
    dbb_re, dbb_im = fold(r12[:, :S5_CH]), fold_m(r12[:, S5_CH:])
    dpr, dpi = fold(q12[:, :CHUNK]), fold_m(q12[:, CHUNK:])
    dc_re, dc_im = dc[:, :n, :S5_CH].transpose(0, 2, 1), dc[:, n:, :S5_CH].transpose(0, 2, 1)
    dp1 = dpx[:, :, S5_CH - 1::S5_CH]
    zero = jnp.zeros((S5_GROUPS, n, 1), F32)
    dpw_re = (jnp.concatenate([dpr[:, ::-1].transpose(0, 2, 1), zero], axis=-1)
              + jnp.concatenate([zero, dp1[:, :n]], axis=-1)).at[:, :, CHUNK].add(da64[:, :n])
    dpw_im = (jnp.concatenate([dpi[:, ::-1].transpose(0, 2, 1), zero], axis=-1)
              + jnp.concatenate([zero, dp1[:, n:]], axis=-1)).at[:, :, CHUNK].add(da64[:, n:])
    return dbb_re, dbb_im, dc_re, dc_im, dpw_re, dpw_im


def _s5_scan_powers(a_re, a_im, log_dt, nsteps):
    dt = jnp.exp(log_dt)[:, None]
    steps = (CHUNK * (2.0 ** jnp.arange(8, dtype=F32)))[None, :, None]
    mag = jnp.exp((a_re * dt)[:, None, :] * steps)
    ang = (a_im * dt)[:, None, :] * steps
    re, im = mag * jnp.cos(ang), mag * jnp.sin(ang)
    del nsteps
    return jnp.concatenate([re, re], -1), jnp.concatenate([-im, im], -1)


def _build_toeplitz(kmat_ref, tg_ref):
    lane = lax.broadcasted_iota(jnp.int32, (CHUNK, CHUNK * S5_CH), 1)
    srow = lax.broadcasted_iota(jnp.int32, (CHUNK, CHUNK * S5_CH), 0)
    keep = lane >= S5_CH * srow
    for h in range(S5_CH):
        row = jnp.broadcast_to(kmat_ref[0, h:h + 1, :], (CHUNK, CHUNK * S5_CH))
        rolled = pltpu.roll(row, 0, 1, stride=S5_CH, stride_axis=0)
        tg_ref[h * CHUNK:(h + 1) * CHUNK, :] = jnp.where(keep, rolled, 0.0).astype(BF)


def _swap_halves(x):
    return pltpu.roll(x, S5_STATE, 1)


def _split3(x):
    hi = x.astype(BF)
    r = x - hi.astype(F32)
    mid = r.astype(BF)
    return hi, mid, (r - mid.astype(F32)).astype(BF)


def _build_state_tables(bbp_ref, pwr_ref, pwn_ref, cn_ref, wst_ref):
    W = CHUNK * S5_CH
    lane = lax.broadcasted_iota(jnp.int32, (1, 2 * S5_STATE), 1)
    pwr = pwr_ref[0]
    pwr_sw = _swap_halves(pwr)
    for hh in range(S5_CH):
        bb = bbp_ref[0, hh:hh + 1, :]
        bb_sw = _swap_halves(bb)
        re_dup = jnp.where(lane < S5_STATE, bb, bb_sw)
        im_sgn = jnp.where(lane < S5_STATE, -bb_sw, bb)
        wst_ref[hh * CHUNK:(hh + 1) * CHUNK, :] = (re_dup * pwr + im_sgn * pwr_sw).astype(BF)
    e_t = (lax.broadcasted_iota(jnp.int32, (CHUNK, W), 0) == (lax.broadcasted_iota(jnp.int32, (CHUNK, W), 1) >> 4)).astype(BF)
    e_k = (lax.broadcasted_iota(jnp.int32, (S5_CH, W), 0) == (lax.broadcasted_iota(jnp.int32, (S5_CH, W), 1) & (S5_CH - 1))).astype(BF)
    px = sum(_dot(part, e_t) for part in _split3(pwn_ref[0]))
    cx = sum(_dot(part, e_k) for part in _split3(cn_ref[0]))
    cr, ci = cx[0:S5_STATE], cx[S5_STATE:2 * S5_STATE]
    pr, pi = px[0:S5_STATE], px[S5_STATE:2 * S5_STATE]
    woff = jnp.concatenate([cr * pr - ci * pi, -(cr * pi + ci * pr)], axis=0).astype(BF)
    return woff, cx, px


def _s5_core_fwd(uflat, kmat, bbp, pwr, pwn, cn, p1, p2):
    G, nc, W = uflat.shape
    nsteps = max(1, (nc - 1).bit_length())

    def body(u_ref, k_ref, bbp_ref, pwr_ref, pwn_ref, cn_ref, p1_ref, p2_ref, y_ref, h_ref, tg_ref, wst_ref):
        _build_toeplitz(k_ref, tg_ref)
        woff, _, _ = _build_state_tables(bbp_ref, pwr_ref, pwn_ref, cn_ref, wst_ref)
        u = u_ref[0]
        x = _dot(u, wst_ref[...])
        row = lax.broadcasted_iota(jnp.int32, x.shape, 0)
        d = 1
        for k in range(nsteps):
            sh = jnp.where(row >= d, pltpu.roll(x, d, 0), 0.0)
            x = x + p1_ref[0, k:k + 1, :] * sh + p2_ref[0, k:k + 1, :] * _swap_halves(sh)
            d *= 2
        h = jnp.where(row >= 1, pltpu.roll(x, 1, 0), 0.0)
        h_ref[0] = h
        y = _dot(u, tg_ref[...]) + _dot(h.astype(BF), woff)
        y_ref[0] = y.astype(BF)

    spec_g = lambda a, b: pl.BlockSpec((1, a, b), lambda g: (g, 0, 0))
    return pl.pallas_call(
        body, name="s5_core_fwd", grid=(G,),
        in_specs=[spec_g(nc, W), spec_g(S5_CH, W), spec_g(S5_CH, 2 * S5_STATE), spec_g(CHUNK, 2 * S5_STATE),
                  spec_g(2 * S5_STATE, CHUNK), spec_g(2 * S5_STATE, S5_CH), spec_g(8, 2 * S5_STATE), spec_g(8, 2 * S5_STATE)],
        out_specs=[spec_g(nc, W), spec_g(nc, 2 * S5_STATE)],
        out_shape=[jax.ShapeDtypeStruct((G, nc, W), BF), jax.ShapeDtypeStruct((G, nc, 2 * S5_STATE), F32)],
        scratch_shapes=[pltpu.VMEM((W, W), BF), pltpu.VMEM((W, 2 * S5_STATE), BF)],
        compiler_params=_params(("arbitrary",)),
    )(uflat, kmat, bbp, pwr, pwn, cn, p1, p2)


def _s5_core_bwd(uflat, dyflat, hsave, kmat, bbp, pwr, pwn, cn, p1, p2):
    G, nc, W = uflat.shape
    nsteps = max(1, (nc - 1).bit_length())

    def body(u_ref, dy_ref, h_ref, k_ref, bbp_ref, pwr_ref, pwn_ref, cn_ref, p1_ref, p2_ref,
             du_ref, dk_ref, r12_ref, q12_ref, dc_ref, dpx_ref, da_ref, tg_ref, flip_ref, wst_ref):
        @pl.when(pl.program_id(0) == 0)
        def _():
            r = lax.broadcasted_iota(jnp.int32, (W, W), 0)
            c = lax.broadcasted_iota(jnp.int32, (W, W), 1)
            flip_ref[...] = (((r >> 6) == (c >> 6)) & ((r & (CHUNK - 1)) + (c & (CHUNK - 1)) == CHUNK - 1)).astype(BF)

        _build_toeplitz(k_ref, tg_ref)
        woff, cx, px = _build_state_tables(bbp_ref, pwr_ref, pwn_ref, cn_ref, wst_ref)
        u = u_ref[0]
        dy = dy_ref[0]
        h = h_ref[0]
        gh = _dot_nt(dy, woff)
        row = lax.broadcasted_iota(jnp.int32, gh.shape, 0)
        x = jnp.where(row < nc - 1, pltpu.roll(gh, nc - 1, 0), 0.0)
        d = 1
        for k in range(nsteps):
            sh = jnp.where(row < nc - d, pltpu.roll(x, nc - d, 0), 0.0)
            x = x + p1_ref[0, k:k + 1, :] * sh - p2_ref[0, k:k + 1, :] * _swap_halves(sh)
            d *= 2
        gs = x.astype(BF)
        du_ref[0] = (_dot_nt(dy, tg_ref[...]) + _dot_nt(gs, wst_ref[...])).astype(BF)

        dwst = _dot_tn(u, gs)
        pwr = pwr_ref[0]
        pwr_sw = _swap_halves(pwr)
        q1 = jnp.zeros((CHUNK, 2 * S5_STATE), F32)
        q2 = jnp.zeros((CHUNK, 2 * S5_STATE), F32)
        r1_rows, r2_rows = [], []
        for hh in range(S5_CH):
            blk = dwst[hh * CHUNK:(hh + 1) * CHUNK, :]
            r1_rows.append(jnp.sum(blk * pwr, axis=0, keepdims=True))
            r2_rows.append(jnp.sum(blk * pwr_sw, axis=0, keepdims=True))
            bb = bbp_ref[0, hh:hh + 1, :]
            q1 = q1 + blk * bb
            q2 = q2 + blk * _swap_halves(bb)
        r12_ref[0] = jnp.concatenate(r1_rows + r2_rows, axis=0)
        q12_ref[0] = jnp.concatenate([q1, q2], axis=0)

        dwoff = _dot_tn(h.astype(BF), dy)
        b1, b2 = dwoff[0:S5_STATE], dwoff[S5_STATE:2 * S5_STATE]
        cr, ci = cx[0:S5_STATE], cx[S5_STATE:2 * S5_STATE]
        pr, pi = px[0:S5_STATE], px[S5_STATE:2 * S5_STATE]

        def sum_over_t(v):
            sh = S5_CH
            while sh < W:
                v = v + pltpu.roll(v, sh, 1)
                sh *= 2
            return v[:, 0:2 * S5_STATE]

        def sum_over_k(v):
            sh = 1
            while sh < S5_CH:
                v = v + pltpu.roll(v, sh, 1)
                sh *= 2
            return v

        dc_ref[0] = jnp.concatenate([sum_over_t(b1 * pr - b2 * pi), sum_over_t(-b1 * pi - b2 * pr)], axis=0)
        dpx_ref[0] = jnp.concatenate([sum_over_k(b1 * cr - b2 * ci), sum_over_k(-b1 * ci - b2 * cr)], axis=0)
        r1 = jnp.sum(x * h, axis=0, keepdims=True)
        r2 = jnp.sum(x * _swap_halves(h), axis=0, keepdims=True)
        da_ref[0] = jnp.concatenate([r1, r2, jnp.zeros((6, 2 * S5_STATE), F32)], axis=0)
        lane = lax.broadcasted_iota(jnp.int32, (CHUNK, W), 1)
        srow = lax.broadcasted_iota(jnp.int32, (CHUNK, W), 0)
        keep = lane < S5_CH * (srow + 1)
        ur = _dot(u, flip_ref[...]).astype(BF)
        for hh in range(S5_CH):
            dt_h = _dot_tn(ur[:, hh * CHUNK:(hh + 1) * CHUNK], dy)
            back = pltpu.roll(dt_h, S5_CH, 1, stride=S5_CH, stride_axis=0)
            dk_ref[0, hh:hh + 1, :] = jnp.sum(jnp.where(keep, back, 0.0), axis=0, keepdims=True)

    spec_g = lambda a, b: pl.BlockSpec((1, a, b), lambda g: (g, 0, 0))
    return pl.pallas_call(
        body, name="s5_core_bwd", grid=(G,),
        in_specs=[spec_g(nc, W), spec_g(nc, W), spec_g(nc, 2 * S5_STATE), spec_g(S5_CH, W), spec_g(S5_CH, 2 * S5_STATE),
                  spec_g(CHUNK, 2 * S5_STATE), spec_g(2 * S5_STATE, CHUNK), spec_g(2 * S5_STATE, S5_CH),
                  spec_g(8, 2 * S5_STATE), spec_g(8, 2 * S5_STATE)],
        out_specs=[spec_g(nc, W), spec_g(S5_CH, W), spec_g(2 * S5_CH, 2 * S5_STATE), spec_g(2 * CHUNK, 2 * S5_STATE),
                   spec_g(2 * S5_STATE, 2 * S5_STATE), spec_g(2 * S5_STATE, W), spec_g(8, 2 * S5_STATE)],
        out_shape=[jax.ShapeDtypeStruct((G, nc, W), BF), jax.ShapeDtypeStruct((G, S5_CH, W), F32),
                   jax.ShapeDtypeStruct((G, 2 * S5_CH, 2 * S5_STATE), F32), jax.ShapeDtypeStruct((G, 2 * CHUNK, 2 * S5_STATE), F32),
                   jax.ShapeDtypeStruct((G, 2 * S5_STATE, 2 * S5_STATE), F32), jax.ShapeDtypeStruct((G, 2 * S5_STATE, W), F32),
                   jax.ShapeDtypeStruct((G, 8, 2 * S5_STATE), F32)],
        scratch_shapes=[pltpu.VMEM((W, W), BF), pltpu.VMEM((W, W), BF), pltpu.VMEM((W, 2 * S5_STATE), BF)],
        compiler_params=_params(("arbitrary",)),
    )(uflat, dyflat, hsave, kmat, bbp, pwr, pwn, cn, p1, p2)


def _flat_hs(a, nc):
    return a.reshape(nc, CHUNK, S5_GROUPS, S5_CH).transpose(2, 0, 3, 1).reshape(S5_GROUPS, nc, CHUNK * S5_CH)


def _unflat_hs(a, nc):
    return a.reshape(S5_GROUPS, nc, S5_CH, CHUNK).transpose(1, 3, 0, 2).reshape(nc * CHUNK, D_MODEL)


def _flat_tk(a, nc):
    return a.reshape(nc, CHUNK, S5_GROUPS, S5_CH).transpose(2, 0, 1, 3).reshape(S5_GROUPS, nc, CHUNK * S5_CH)


def _unflat_tk(a, nc):
    return a.reshape(S5_GROUPS, nc, CHUNK, S5_CH).transpose(1, 2, 0, 3).reshape(nc * CHUNK, D_MODEL)


def _s5_post_fwd(yssm, proj, s5_d, w_glu, b_glu):
    L = yssm.shape[0]
    tl = min(TOKEN_TILE, L)

    def body(ys_ref, u_ref, z_ref, d_ref, wg_ref, bg_ref, o_ref):
        u = u_ref[...].astype(F32)
        a = _gelu(ys_ref[...].astype(F32) + d_ref[...] * u)
        y = a * _sigmoid(_dot(a.astype(BF), wg_ref[...]) + bg_ref[...])
        z = z_ref[...].astype(F32)
        o_ref[...] = (y * z * _sigmoid(z)).astype(BF)

    return pl.pallas_call(
        body, name="s5_post_fwd", grid=(L // tl,),
        in_specs=[_row_spec(tl, D_MODEL), _row_spec(tl, D_MODEL, 0), _row_spec(tl, D_MODEL, 1), _const_spec((1, D_MODEL)),
                  _const_spec((D_MODEL, D_MODEL)), _const_spec((1, D_MODEL))],
        out_specs=_row_spec(tl, D_MODEL),
        out_shape=jax.ShapeDtypeStruct((L, D_MODEL), BF),
        compiler_params=_params(("arbitrary",)),
    )(yssm, proj, proj, s5_d, w_glu, b_glu)


def _s5_post_bwd(dys5, yssm, proj, s5_d, w_glu, b_glu):
    L = yssm.shape[0]
    tl = min(TOKEN_TILE, L)

    def body(dy_ref, ys_ref, u_ref, z_ref, d_ref, wg_ref, bg_ref, dz_ref, dys_ref, a_ref, dgl_ref, dbg_ref, dd_ref):
        @pl.when(pl.program_id(0) == 0)
        def _():
            dbg_ref[...] = jnp.zeros_like(dbg_ref)
            dd_ref[...] = jnp.zeros_like(dd_ref)

        u = u_ref[...].astype(F32)
        y0 = ys_ref[...].astype(F32) + d_ref[...] * u
        a = _gelu(y0)
        a_bf = a.astype(BF)
        sg = _sigmoid(_dot(a_bf, wg_ref[...]) + bg_ref[...])
        y = a * sg
        z = z_ref[...].astype(F32)
        sz = _sigmoid(z)
        dout = dy_ref[...].astype(F32)
        dz_ref[...] = (dout * y * sz * (1.0 + z * (1.0 - sz))).astype(BF)
        dyv = dout * z * sz
        dgl = dyv * a * sg * (1.0 - sg)
        dgl_bf = dgl.astype(BF)
        da = dyv * sg + _dot_nt(dgl_bf, wg_ref[...])
        dy0 = da * _gelu_grad(y0)
        dbg_ref[...] += jnp.sum(dgl, axis=0, keepdims=True)
        dd_ref[...] += jnp.sum(dy0 * u, axis=0, keepdims=True)
        dys_ref[...] = dy0.astype(BF)
        a_ref[...] = a_bf
        dgl_ref[...] = dgl_bf

    big = jax.ShapeDtypeStruct((L, D_MODEL), BF)
    vec = jax.ShapeDtypeStruct((1, D_MODEL), F32)
    return pl.pallas_call(
        body, name="s5_post_bwd", grid=(L // tl,),
        in_specs=[_row_spec(tl, D_MODEL), _row_spec(tl, D_MODEL), _row_spec(tl, D_MODEL, 0), _row_spec(tl, D_MODEL, 1),
                  _const_spec((1, D_MODEL)), _const_spec((D_MODEL, D_MODEL)), _const_spec((1, D_MODEL))],
        out_specs=[_row_spec(tl, D_MODEL)] * 4 + [_const_spec((1, D_MODEL))] * 2,
        out_shape=[big, big, big, big, vec, vec],
        compiler_params=_params(("arbitrary",)),
    )(dys5, yssm, proj, proj, s5_d, w_glu, b_glu)


def _cumsum_rows(a):
    row = lax.broadcasted_iota(jnp.int32, a.shape, 0)
    d = 1
    while d < a.shape[0]:
        a = a + jnp.where(row >= d, pltpu.roll(a, d, 0), 0.0)
        d *= 2
    return a


def _rev_cumsum_rows(a):
    n = a.shape[0]
    row = lax.broadcasted_iota(jnp.int32, a.shape, 0)
    d = 1
    while d < n:
        a = a + jnp.where(row < n - d, pltpu.roll(a, n - d, 0), 0.0)
        d *= 2
    return a


def _ssd_conv_fwd(first, xs_ref, bc_ref, hx_ref, hb_ref, cw_ref, cb_ref, xp_ref, tl):
    hal = jnp.concatenate([hx_ref[...], hb_ref[...]], axis=1)
    xp_ref[0:8, :] = jnp.where(first, 0.0, hal)
    xp_ref[8:8 + tl, 0:1024] = xs_ref[...]
    xp_ref[8:8 + tl, 1024:2048] = bc_ref[...]
    pre = cb_ref[...] + cw_ref[0:1, :] * xp_ref[5:5 + tl, :]
    for k in range(1, 4):
        pre = pre + cw_ref[k:k + 1, :] * xp_ref[5 + k:5 + k + tl, :]
    return pre


def _onehot_lane(h):
    return (lax.broadcasted_iota(jnp.int32, (1, LANES), 1) == h).astype(F32)


def _dot_exact(x, e):
    hi = x.astype(BF)
    r = x - hi.astype(F32)
    mid = r.astype(BF)
    lo = (r - mid.astype(F32)).astype(BF)
    return _dot(hi, e) + _dot(mid, e) + _dot(lo, e)


def _head_expand_matrices():
    e = lax.broadcasted_iota(jnp.int32, (LANES, D_MODEL), 0) == (lax.broadcasted_iota(jnp.int32, (LANES, D_MODEL), 1) >> 6)
    et = (lax.broadcasted_iota(jnp.int32, (D_MODEL, LANES), 0) >> 6) == lax.broadcasted_iota(jnp.int32, (D_MODEL, LANES), 1)
    return e.astype(BF), et.astype(BF)


def _group_masks():
    r64 = lax.broadcasted_iota(jnp.int32, (4 * CHUNK, CHUNK), 0)
    causal4 = (r64 & (CHUNK - 1)) >= lax.broadcasted_iota(jnp.int32, (4 * CHUNK, CHUNK), 1)
    r256 = lax.broadcasted_iota(jnp.int32, (4 * CHUNK, 4 * SSD_HEAD_DIM), 0)
    same = (r256 >> 6) == (lax.broadcasted_iota(jnp.int32, (4 * CHUNK, 4 * SSD_HEAD_DIM), 1) >> 6)
    return causal4, same


def _group_decay(acs, acs_t, j, causal4):
    col = jnp.concatenate([acs[:, 4 * j + hh:4 * j + hh + 1] for hh in range(4)], axis=0)
    rowv = jnp.concatenate([jnp.broadcast_to(acs_t[4 * j + hh:4 * j + hh + 1, :], (CHUNK, CHUNK)) for hh in range(4)], axis=0)
    return jnp.where(causal4, jnp.exp(col - rowv), 0.0)


def _group_last_decay(acs_t, j):
    return jnp.concatenate([jnp.broadcast_to(jnp.exp(acs_t[4 * j + hh:4 * j + hh + 1, CHUNK - 1:CHUNK]), (SSD_HEAD_DIM, 1))
                            for hh in range(4)], axis=0)


def _fold_heads(r):
    return r[0:CHUNK] + r[CHUNK:2 * CHUNK] + r[2 * CHUNK:3 * CHUNK] + r[3 * CHUNK:4 * CHUNK]


def _ssd_specs_in(tl, nt, rev):
    t_of = (lambda i: nt - 1 - i) if rev else (lambda i: i)
    rows = lambda w, col: pl.BlockSpec((tl, w), lambda i: (t_of(i), col))
    halo = lambda col: pl.BlockSpec((8, 1024), lambda i: (jnp.maximum(t_of(i) * (tl // 8) - 1, 0), col))
    return t_of, rows, halo


def _ssd_fwd(proj, pdt, conv_w, conv_b, dt_bias, a_log, ssd_d, norm_w):
    L = proj.shape[0]
    tl = min(TOKEN_TILE, L)
    nt, ncl = L // tl, tl // CHUNK
    _, rows, halo = _ssd_specs_in(tl, nt, False)

    def body(xs_ref, bc_ref, hx_ref, hb_ref, dt_ref, z_ref, cw_ref, cb_ref, dtb_ref, al_ref, dd_ref, nw_ref,
             y_ref, ypre_ref, st_ref, xp_ref, xbc_ref, dts_ref, hst_ref):
        i = pl.program_id(0)

        @pl.when(i == 0)
        def _():
            hst_ref[...] = jnp.zeros_like(hst_ref)

        pre = _ssd_conv_fwd(i == 0, xs_ref, bc_ref, hx_ref, hb_ref, cw_ref, cb_ref, xp_ref, tl)
        xbc_ref[...] = pre * _sigmoid(pre)
        dts_ref[...] = _softplus(dt_ref[...] + dtb_ref[...])
        a_neg = -jnp.exp(al_ref[...])
        e16, _ = _head_expand_matrices()
        causal4, same = _group_masks()
        dd_x = _dot_exact(jnp.broadcast_to(dd_ref[...], (8, LANES)), e16)[0:1, :]

        def chunk(c, carry):
            r0 = pl.multiple_of(c * CHUNK, CHUNK)
            xbc = xbc_ref[pl.ds(r0, CHUNK), :]
            dtc = dts_ref[pl.ds(r0, CHUNK), :]
            acs = _cumsum_rows(dtc * a_neg)
            acs_t = acs.T
            acs_x = _dot_exact(acs, e16)
            xs = xbc[:, 0:1024]
            xd = xs * _dot_exact(dtc, e16)
            xd_bf = xd.astype(BF)
            xdd = (xd * jnp.exp(acs_x[CHUNK - 1:CHUNK, :] - acs_x)).astype(BF)
            e_x = jnp.exp(acs_x)
            for j in range(SSD_GROUPS):
                sl = slice(256 * j, 256 * (j + 1))
                bj = xbc[:, 1024 + 128 * j:1024 + 128 * (j + 1)].astype(BF)
                cj = xbc[:, 1536 + 128 * j:1536 + 128 * (j + 1)].astype(BF)
                g = _dot_nt(cj, bj)
                hj = hst_ref[sl, :]
                zj = _dot_nt(cj, hj.astype(BF))
                sc = (jnp.concatenate([g] * 4, axis=0) * _group_decay(acs, acs_t, j, causal4)).astype(BF)
                yd = _fold_heads(jnp.where(same, _dot(sc, xd_bf[:, sl]), 0.0))
                ypre_ref[pl.ds(r0, CHUNK), sl] = yd + e_x[:, sl] * zj + dd_x[:, sl] * xs[:, sl]
                st_ref[c, sl, :] = hj
                hst_ref[sl, :] = _group_last_decay(acs_t, j) * hj + _dot_tn(xdd[:, sl], bj)
            return carry

        lax.fori_loop(0, ncl, chunk, 0)
        z = z_ref[...]
        gg = ypre_ref[...] * z * _sigmoid(z)
        for j in range(SSD_GROUPS):
            seg = gg[:, 256 * j:256 * (j + 1)]
            r = lax.rsqrt(jnp.mean(seg * seg, axis=-1, keepdims=True) + EPS)
            y_ref[:, 256 * j:256 * (j + 1)] = (seg * r * nw_ref[:, 256 * j:256 * (j + 1)]).astype(BF)

    nc = L // CHUNK
    return pl.pallas_call(
        body, name="ssd_fwd", grid=(nt,),
        in_specs=[rows(1024, 1), rows(1024, 2), halo(1), halo(2), rows(LANES, 0), rows(1024, 0),
                  _const_spec((4, 2048)), _const_spec((1, 2048)), _const_spec((1, LANES)), _const_spec((1, LANES)),
                  _const_spec((1, LANES)), _const_spec((1, D_MODEL))],
        out_specs=[_row_spec(tl, D_MODEL), _row_spec(tl, D_MODEL), pl.BlockSpec((ncl, 1024, SSD_STATE), lambda i: (i, 0, 0))],
        out_shape=[jax.ShapeDtypeStruct((L, D_MODEL), BF), jax.ShapeDtypeStruct((L, D_MODEL), F32),
                   jax.ShapeDtypeStruct((nc, 1024, SSD_STATE), F32)],
        scratch_shapes=[pltpu.VMEM((tl + 8, 2048), F32), pltpu.VMEM((tl, 2048), F32), pltpu.VMEM((tl, LANES), F32),
                        pltpu.VMEM((1024, SSD_STATE), F32)],
        compiler_params=_params(("arbitrary",)),
    )(proj, proj, proj, proj, pdt, proj, conv_w, conv_b, dt_bias, a_log, ssd_d, norm_w)


def _ssd_bwd(dyssd, ypre, proj, pdt, states, conv_w, conv_b, dt_bias, a_log, ssd_d, norm_w):
    L = proj.shape[0]
    tl = min(TOKEN_TILE, L)
    nt, ncl = L // tl, tl // CHUNK
    t_of, rows, halo = _ssd_specs_in(tl, nt, True)

    def body(dy_ref, ypre_ref, z_ref, xs_ref, bc_ref, hx_ref, hb_ref, dt_ref, st_ref, cw_ref, cb_ref, dtb_ref, al_ref,
             dd_ref, nw_ref,
             dxbc_ref, ddt_ref, dz_ref, dcw_ref, dcb_ref, ddtb_ref, dal_ref, ddd_ref, dnw_ref,
             xp_ref, xbc_ref, pre_ref, dts_ref, dyp_ref, dxs_ref, ddts_ref, dp_ref, dh_ref):
        i = pl.program_id(0)

        @pl.when(i == 0)
        def _():
            for r in (dcw_ref, dcb_ref, ddtb_ref, dal_ref, ddd_ref, dnw_ref, dh_ref):
                r[...] = jnp.zeros_like(r)
            dp_ref[tl:tl + 8, :] = jnp.zeros((8, 2048), F32)

        pre = _ssd_conv_fwd(t_of(i) == 0, xs_ref, bc_ref, hx_ref, hb_ref, cw_ref, cb_ref, xp_ref, tl)
        pre_ref[...] = pre
        xbc_ref[...] = pre * _sigmoid(pre)
        dts_ref[...] = _softplus(dt_ref[...] + dtb_ref[...])
        a_neg = -jnp.exp(al_ref[...])

        ypre = ypre_ref[...]
        z = z_ref[...]
        sz = _sigmoid(z)
        gg = ypre * z * sz
        dout = dy_ref[...]
        for j in range(SSD_GROUPS):
            sl = slice(256 * j, 256 * (j + 1))
            seg = gg[:, sl]
            r = lax.rsqrt(jnp.mean(seg * seg, axis=-1, keepdims=True) + EPS)
            gh = seg * r
            dnw_ref[:, sl] += jnp.sum(dout[:, sl] * gh, axis=0, keepdims=True)
            gw = dout[:, sl] * nw_ref[:, sl]
            dgg = r * (gw - gh * jnp.mean(gw * gh, axis=-1, keepdims=True))
            dyp_ref[:, sl] = dgg * z[:, sl] * sz[:, sl]
            dz_ref[:, sl] = (dgg * ypre[:, sl] * sz[:, sl] * (1.0 + z[:, sl] * (1.0 - sz[:, sl]))).astype(BF)

        e16, e16t = _head_expand_matrices()
        causal4, same = _group_masks()
        dd_x = _dot_exact(jnp.broadcast_to(dd_ref[...], (8, LANES)), e16)[0:1, :]
        last_row = (lax.broadcasted_iota(jnp.int32, (CHUNK, 1), 0) == CHUNK - 1).astype(F32)
        sel_rows = lax.broadcasted_iota(jnp.int32, (4 * CHUNK, LANES), 0) >> 6
        sel_lanes = lax.broadcasted_iota(jnp.int32, (4 * CHUNK, LANES), 1)

        def chunk(k, carry):
            dal_acc, ddx_acc = carry
            c = ncl - 1 - k
            r0 = pl.multiple_of(c * CHUNK, CHUNK)
            xbc = xbc_ref[pl.ds(r0, CHUNK), :]
            dtc = dts_ref[pl.ds(r0, CHUNK), :]
            dyp = dyp_ref[pl.ds(r0, CHUNK), :]
            acs = _cumsum_rows(dtc * a_neg)
            acs_t = acs.T
            acs_x = _dot_exact(acs, e16)
            dt_x = _dot_exact(dtc, e16)
            xs = xbc[:, 0:1024]
            xd = xs * dt_x
            xd_bf = xd.astype(BF)
            dec_x = jnp.exp(acs_x[CHUNK - 1:CHUNK, :] - acs_x)
            xdd = xd * dec_x
            xdd_bf = xdd.astype(BF)
            dz = dyp * jnp.exp(acs_x)
            dz_bf = dz.astype(BF)
            ddx_acc = ddx_acc + jnp.sum(dyp * xs, axis=0, keepdims=True)
            dacs = jnp.zeros((CHUNK, LANES), F32)
            hsum = jnp.zeros((1, LANES), F32)
            p1_l, p2_l, p3_l = [], [], []
            for j in range(SSD_GROUPS):
                sl = slice(256 * j, 256 * (j + 1))
                bj = xbc[:, 1024 + 128 * j:1024 + 128 * (j + 1)].astype(BF)
                cj = xbc[:, 1536 + 128 * j:1536 + 128 * (j + 1)].astype(BF)
                g = _dot_nt(cj, bj)
                hj = st_ref[c, sl, :]
                hj_bf = hj.astype(BF)
                dhj = dh_ref[sl, :]
                dhj_bf = dhj.astype(BF)
                zj = _dot_nt(cj, hj_bf)
                qj = _dot_nt(bj, dhj_bf)
                lm = _group_decay(acs, acs_t, j, causal4)
                sc = jnp.concatenate([g] * 4, axis=0) * lm
                sc_bf = sc.astype(BF)
                dym = jnp.where(same, jnp.concatenate([dyp[:, sl]] * 4, axis=0), 0.0).astype(BF)
                dsc = _dot_nt(dym, xd_bf[:, sl])
                dxd = _dot_tn(sc_bf, dym) + qj * dec_x[:, sl]
                m = dsc * sc
                dg_bf = _fold_heads(dsc * lm).astype(BF)
                rs = jnp.sum(m, axis=1, keepdims=True)
                e2 = dhj * hj
                for hh in range(4):
                    oh = _onehot_lane(4 * j + hh)
                    dacs = dacs + oh * rs[CHUNK * hh:CHUNK * (hh + 1)]
                    hsum = hsum + oh * jnp.sum(jnp.sum(e2[64 * hh:64 * (hh + 1)], axis=0, keepdims=True), axis=1, keepdims=True)
                sel = (sel_rows + 4 * j == sel_lanes).astype(BF)
                hi = m.astype(BF)
                rem = m - hi.astype(F32)
                mid = rem.astype(BF)
                lo = (rem - mid.astype(F32)).astype(BF)
                dacs = dacs - (_dot_tn(hi, sel) + _dot_tn(mid, sel) + _dot_tn(lo, sel))
                p1_l.append(dz[:, sl] * zj)
                p2_l.append(qj * xdd[:, sl])
                p3_l.append(dxd * xs[:, sl])
                dxs_ref[pl.ds(r0, CHUNK), sl] = dd_x[:, sl] * dyp[:, sl] + dxd * dt_x[:, sl]
                dxs_ref[pl.ds(r0, CHUNK), 1536 + 128 * j:1536 + 128 * (j + 1)] = _dot(dg_bf, bj) + _dot(dz_bf[:, sl], hj_bf)
                dxs_ref[pl.ds(r0, CHUNK), 1024 + 128 * j:1024 + 128 * (j + 1)] = _dot_tn(dg_bf, cj) + _dot(xdd_bf[:, sl], dhj_bf)
                dh_ref[sl, :] = _group_last_decay(acs_t, j) * dhj + _dot_tn(dz_bf[:, sl], cj)
            stacked = jnp.concatenate([jnp.concatenate(p1_l, axis=1), jnp.concatenate(p2_l, axis=1), jnp.concatenate(p3_l, axis=1)], axis=0)
            red = _dot_exact(stacked, e16t)
            r1, r2, ddtc = red[0:CHUNK], red[CHUNK:2 * CHUNK], red[2 * CHUNK:3 * CHUNK]
            tot = jnp.sum(r2, axis=0, keepdims=True) + jnp.exp(acs[CHUNK - 1:CHUNK, :]) * hsum
            da = _rev_cumsum_rows(dacs + r1 - r2 + last_row * tot)
            ddts_ref[pl.ds(r0, CHUNK), :] = ddtc + da * a_neg
            dal_acc = dal_acc + jnp.sum(da * dtc, axis=0, keepdims=True)
            return dal_acc, ddx_acc

        dal_acc, ddx_acc = lax.fori_loop(0, ncl, chunk, (jnp.zeros((1, LANES), F32), jnp.zeros((1, D_MODEL), F32)))
        dal_ref[...] += dal_acc * a_neg
        ddd_ref[...] += _dot_exact(jnp.broadcast_to(ddx_acc, (8, D_MODEL)), e16t)[0:1, :]
        ddt_raw = ddts_ref[...] * _sigmoid(dt_ref[...] + dtb_ref[...])
        ddt_ref[...] = ddt_raw
        ddtb_ref[...] += jnp.sum(ddt_raw, axis=0, keepdims=True)

        pre = pre_ref[...]
        sp = _sigmoid(pre)
        dpre = dxs_ref[...] * sp * (1.0 + pre * (1.0 - sp))
        dp_ref[0:tl, :] = dpre
        dcb_ref[...] += jnp.sum(dpre, axis=0, keepdims=True)
        dx = jnp.zeros((tl, 2048), F32)
        for k in range(4):
            dcw_ref[k:k + 1, :] += jnp.sum(dpre * xp_ref[5 + k:5 + k + tl, :], axis=0, keepdims=True)
            dx = dx + cw_ref[k:k + 1, :] * dp_ref[3 - k:3 - k + tl, :]
        dxbc_ref[...] = dx.astype(BF)
        dp_ref[tl:tl + 8, :] = dp_ref[0:8, :]

    vec = lambda w: jax.ShapeDtypeStruct((1, w), F32)
    rrow = lambda w: pl.BlockSpec((tl, w), lambda i: (t_of(i), 0))
    return pl.pallas_call(
        body, name="ssd_bwd", grid=(nt,),
        in_specs=[rrow(D_MODEL), rrow(D_MODEL), rows(1024, 0), rows(1024, 1), rows(1024, 2), halo(1), halo(2), rows(LANES, 0),
                  pl.BlockSpec((ncl, 1024, SSD_STATE), lambda i: (t_of(i), 0, 0)),
                  _const_spec((4, 2048)), _const_spec((1, 2048)), _const_spec((1, LANES)), _const_spec((1, LANES)),
                  _const_spec((1, LANES)), _const_spec((1, D_MODEL))],
        out_specs=[rrow(2048), rrow(LANES), rrow(D_MODEL), _const_spec((8, 2048)), _const_spec((1, 2048)),
                   _const_spec((1, LANES)), _const_spec((1, LANES)), _const_spec((1, LANES)), _const_spec((1, D_MODEL))],
        out_shape=[jax.ShapeDtypeStruct((L, 2048), BF), jax.ShapeDtypeStruct((L, LANES), F32), jax.ShapeDtypeStruct((L, D_MODEL), BF),
                   jax.ShapeDtypeStruct((8, 2048), F32), vec(2048), vec(LANES), vec(LANES), vec(LANES), vec(D_MODEL)],
        scratch_shapes=[pltpu.VMEM((tl + 8, 2048), F32), pltpu.VMEM((tl, 2048), F32), pltpu.VMEM((tl, 2048), F32),
                        pltpu.VMEM((tl, LANES), F32), pltpu.VMEM((tl, D_MODEL), F32), pltpu.VMEM((tl, 2048), F32),
                        pltpu.VMEM((tl, LANES), F32), pltpu.VMEM((tl + 8, 2048), F32), pltpu.VMEM((1024, SSD_STATE), F32)],
        compiler_params=_params(("arbitrary",)),
    )(dyssd, ypre, proj, proj, proj, proj, proj, pdt, states, conv_w, conv_b, dt_bias, a_log, ssd_d, norm_w)


def _head_fwd_bwd(x, ys5, yssd, p, target, w_out, w_gate, w_proj, ple_nw, fin_nw):
    L = x.shape[0]
    tl = min(TOKEN_TILE, L)
    inv_d = 1.0 / D_MODEL

    def body(x_ref, ys_ref, yd_ref, p_ref, t_ref, wo_ref, wg_ref, wp_ref, pnw_ref, fnw_ref,
             loss_ref, dys_ref, dyd_ref, dh1_ref, n2_ref, dgl_ref, dpp_ref, dpnw_ref, dfnw_ref):
        @pl.when(pl.program_id(0) == 0)
        def _():
            loss_ref[...] = jnp.zeros_like(loss_ref)
            dpnw_ref[...] = jnp.zeros_like(dpnw_ref)
            dfnw_ref[...] = jnp.zeros_like(dfnw_ref)

        h1 = x_ref[...] + _dot(ys_ref[...], wo_ref[0:1024, :]) + _dot(yd_ref[...], wo_ref[1024:2048, :])
        r1 = lax.rsqrt(jnp.mean(h1 * h1, axis=-1, keepdims=True) + EPS)
        hh1 = h1 * r1
        n2 = (hh1 * pnw_ref[...]).astype(BF)
        gate = _sigmoid(_dot(n2, wg_ref[...]))
        pp = _dot(p_ref[...].astype(BF), wp_ref[...])
        h2 = h1 + pp * gate
        r2 = lax.rsqrt(jnp.mean(h2 * h2, axis=-1, keepdims=True) + EPS)
        hh2 = h2 * r2
        err = hh2 * fnw_ref[...] - t_ref[...]
        loss_ref[...] += 0.5 * inv_d * jnp.sum(err * err)
        dyo = err * inv_d
        dfnw_ref[...] += jnp.sum(dyo * hh2, axis=0, keepdims=True)
        g2 = dyo * fnw_ref[...]
        dh2 = r2 * (g2 - hh2 * jnp.mean(g2 * hh2, axis=-1, keepdims=True))
        dpp_ref[...] = (dh2 * gate).astype(BF)
        dgl = (dh2 * pp * gate * (1.0 - gate)).astype(BF)
        dgl_ref[...] = dgl
        n2_ref[...] = n2
        dn2 = _dot_nt(dgl, wg_ref[...])
        dpnw_ref[...] += jnp.sum(dn2 * hh1, axis=0, keepdims=True)
        g1 = dn2 * pnw_ref[...]
        dh1 = dh2 + r1 * (g1 - hh1 * jnp.mean(g1 * hh1, axis=-1, keepdims=True))
        dh1_ref[...] = dh1
        dh1_bf = dh1.astype(BF)
        dys_ref[...] = _dot_nt(dh1_bf, wo_ref[0:1024, :]).astype(BF)
        dyd_ref[...] = _dot_nt(dh1_bf, wo_ref[1024:2048, :])

    big = jax.ShapeDtypeStruct((L, D_MODEL), BF)
    vec = jax.ShapeDtypeStruct((1, D_MODEL), F32)
    return pl.pallas_call(
        body, name="head_fwd_bwd", grid=(L // tl,),
        in_specs=[_row_spec(tl, D_MODEL), _row_spec(tl, D_MODEL), _row_spec(tl, D_MODEL), _row_spec(tl, 256), _row_spec(tl, D_MODEL),
                  _const_spec((2048, D_MODEL)), _const_spec((D_MODEL, D_MODEL)), _const_spec((256, D_MODEL)),
                  _const_spec((1, D_MODEL)), _const_spec((1, D_MODEL))],
        out_specs=[_const_spec((8, LANES)), _row_spec(tl, D_MODEL), _row_spec(tl, D_MODEL), _row_spec(tl, D_MODEL),
                   _row_spec(tl, D_MODEL), _row_spec(tl, D_MODEL), _row_spec(tl, D_MODEL), _const_spec((1, D_MODEL)), _const_spec((1, D_MODEL))],
        out_shape=[jax.ShapeDtypeStruct((8, LANES), F32), big, jax.ShapeDtypeStruct((L, D_MODEL), F32),
                   jax.ShapeDtypeStruct((L, D_MODEL), F32), big, big, big, vec, vec],
        compiler_params=_params(("arbitrary",)),
    )(x, ys5, yssd, p, target, w_out, w_gate, w_proj, ple_nw, fin_nw)


def _pad_lanes(v):
    return jnp.pad(v.reshape(1, -1), ((0, 0), (0, LANES - v.size)))


def _local_step(x, p, target, w):
    L = x.shape[0]
    nc = L // CHUNK
    nsteps = max(1, (nc - 1).bit_length())
    w_in = w["w_in"]
    w_main = w_in[:, :D_MAIN]
    w_dt = jnp.pad(w_in[:, D_MAIN:], ((0, 0), (0, LANES - SSD_HEADS)))
    norm_w = w["norm_w"].reshape(1, -1)
    s5_d = w["s5_D"].reshape(1, -1)
    b_glu = w["s5_b_glu"].reshape(1, -1)
    conv_b = w["conv_b"].reshape(1, -1)
    dtb, alog, ssd_d = _pad_lanes(w["dt_bias"]), _pad_lanes(w["A_log"]), _pad_lanes(w["ssd_D"])
    ssd_nw = w["ssd_norm_w"].reshape(1, -1)
    ple_nw = w["ple_norm_w"].reshape(1, -1)
    fin_nw = w["final_norm_w"].reshape(1, -1)

    s5_args = (w["s5_A_re"], w["s5_A_im"], w["s5_log_dt"], w["s5_B_re"], w["s5_B_im"], w["s5_C_re"], w["s5_C_im"])
    small, small_vjp = jax.vjp(_s5_discretise, *s5_args)
    kmat, kmat_vjp = jax.vjp(_s5_kmat, *small)
    bbp, pwr, pwn, cn = _s5_table_factors(*small)
    p1, p2 = _s5_scan_powers(w["s5_A_re"], w["s5_A_im"], w["s5_log_dt"], nsteps)

    hn, proj, pssd, pdt = _in_proj_fwd(x, norm_w, w_main, w_dt)
    uflat = _flat_hs(proj[:, :D_MODEL], nc)
    yflat, hsave = _s5_core_fwd(uflat, kmat, bbp, pwr, pwn, cn, p1, p2)
    yssm = _unflat_tk(yflat, nc)
    ys5 = _s5_post_fwd(yssm, proj, s5_d, w["s5_w_glu"], b_glu)
    yssd, ypre, states = _ssd_fwd(pssd, pdt, w["conv_w"], conv_b, dtb, alog, ssd_d, ssd_nw)
    (loss8, dys5, dyssd, dh1, n2, dgl2, dpp, g_ple_nw, g_fin_nw) = _head_fwd_bwd(
        x, ys5, yssd, p, target, w["w_out"], w["w_ple_gate"], w["w_ple_proj"], ple_nw, fin_nw)

    (dxbc, ddt, dzd, g_cw, g_cb, g_dtb, g_alog, g_ssd_d, g_ssd_nw) = _ssd_bwd(
        dyssd, ypre, pssd, pdt, states, w["conv_w"], conv_b, dtb, alog, ssd_d, ssd_nw)
    dzs, dyssm, a_glu, dgl1, g_bglu, g_s5d = _s5_post_bwd(dys5, yssm, proj, s5_d, w["s5_w_glu"], b_glu)
    duflat, dkmat, r12, q12, dc, dpx, da8 = _s5_core_bwd(uflat, _flat_tk(dyssm, nc), hsave, kmat, bbp, pwr, pwn, cn, p1, p2)
    da64 = jnp.concatenate([da8[:, 0, :S5_STATE] + da8[:, 0, S5_STATE:], da8[:, 1, S5_STATE:] - da8[:, 1, :S5_STATE]], axis=-1)
    d_small = [a + b for a, b in zip(_s5_small_cotangents(r12, q12, dc, dpx, da64), kmat_vjp(dkmat))]
    g_s5 = small_vjp(tuple(d_small))
    gx, du, g_norm_w = _in_proj_bwd(x, norm_w, dh1, _unflat_hs(duflat, nc), dyssm, s5_d, dzs, dzd, dxbc, ddt, w_main, w_dt)

    g_w_in = jnp.concatenate([
        _matmul_tn(hn, du, "dw_in_u"), _matmul_tn(hn, dzs, "dw_in_zs"), _matmul_tn(hn, dzd, "dw_in_zd"),
        _matmul_tn(hn, dxbc, "dw_in_xbc"), _matmul_tn(hn, ddt, "dw_in_dt")[:, :SSD_HEADS]], axis=1)
    grads = {
        "norm_w": g_norm_w, "w_in": g_w_in,
        "s5_A_re": g_s5[0], "s5_A_im": g_s5[1], "s5_log_dt": g_s5[2], "s5_B_re": g_s5[3], "s5_B_im": g_s5[4],
        "s5_C_re": g_s5[5], "s5_C_im": g_s5[6], "s5_D": g_s5d, "s5_w_glu": _matmul_tn(a_glu, dgl1, "dw_glu"), "s5_b_glu": g_bglu,
        "conv_w": g_cw[:4], "conv_b": g_cb, "dt_bias": g_dtb[:, :SSD_HEADS], "A_log": g_alog[:, :SSD_HEADS],
        "ssd_D": g_ssd_d[:, :SSD_HEADS], "ssd_norm_w": g_ssd_nw,
        "w_out": jnp.concatenate([_matmul_tn(ys5, dh1, "dw_out_s5"), _matmul_tn(yssd, dh1, "dw_out_ssd")], axis=0),
        "ple_norm_w": g_ple_nw, "w_ple_gate": _matmul_tn(n2, dgl2, "dw_gate"), "w_ple_proj": _matmul_tn(p, dpp, "dw_proj"),
        "final_norm_w": g_fin_nw,
    }
    return loss8[0, 0], gx, grads


WEIGHTS = ("norm_w", "w_in", "s5_A_re", "s5_A_im", "s5_log_dt", "s5_B_re", "s5_B_im", "s5_C_re", "s5_C_im", "s5_D", "s5_w_glu",
           "s5_b_glu", "conv_w", "conv_b", "dt_bias", "A_log", "ssd_D", "ssd_norm_w", "w_out", "ple_norm_w", "w_ple_gate",
           "w_ple_proj", "final_norm_w")
BIG = {"w_in": ((1024, 1284), 1), "s5_w_glu": ((256, 1024), 0), "w_out": ((512, 1024), 0), "w_ple_gate": ((256, 1024), 0),
       "w_ple_proj": ((256, 256), 1)}
SMALL = {"norm_w": (1024,), "s5_A_re": (64, 64), "s5_A_im": (64, 64), "s5_log_dt": (64,), "s5_B_re": (64, 64, 16),
         "s5_B_im": (64, 64, 16), "s5_C_re": (64, 16, 64), "s5_C_im": (64, 16, 64), "s5_D": (1024,), "s5_b_glu": (1024,),
         "conv_w": (4, 2048), "conv_b": (2048,), "dt_bias": (16,), "A_log": (16,), "ssd_D": (16,), "ssd_norm_w": (1024,),
         "ple_norm_w": (1024,), "final_norm_w": (1024,)}
BIG_ROWS = {n: s[0] * s[1] // LANES for n, (s, _) in BIG.items()}
BIG_ROWS_TOTAL = sum(BIG_ROWS.values())
SMALL_TOTAL = sum(math.prod(s) for s in SMALL.values())
SMALL_PIECE_ROWS = -(-SMALL_TOTAL // (N_CHIPS * 16 * LANES)) * 16
HALF_ROWS = (BIG_ROWS_TOTAL + SMALL_PIECE_ROWS) // 2
SMALL_ROW0 = BIG_ROWS_TOTAL - HALF_ROWS


def _mesh_pos():
    return lax.axis_index("x"), lax.axis_index("y"), lax.axis_index("c")


def _other_chips(x, y):
    return [(1 - x, y), (x, 1 - y), (1 - x, 1 - y)]


def _comm_params():
    return pltpu.CompilerParams(has_side_effects=True)


def _all_gather_chips(wpack, cw):
    half = wpack.shape[0] // 2

    def body(w_ref, c_ref, wo_ref, co_ref, send_sems, recv_sems, fwd_send, fwd_recv, loc_sems):
        x, y, c = _mesh_pos()
        me = 2 * x + y
        sib = (x, y, 1 - c)
        mine = pl.ds(c * half, half)
        theirs = pl.ds((1 - c) * half, half)
        others = _other_chips(x, y)
        loc = [pltpu.make_async_copy(c_ref, co_ref.at[me], loc_sems.at[0])]
        for cp in loc:
            cp.start()

        def from_chip(k, chip, dev):
            return pltpu.make_async_remote_copy(w_ref.at[mine], wo_ref.at[chip, mine], send_sems.at[2 * k], recv_sems.at[2 * k],
                                                device_id=dev, device_id_type=MESH)

        def conv_from(k, chip, dev):
            return pltpu.make_async_remote_copy(c_ref, co_ref.at[chip], send_sems.at[2 * k + 1], recv_sems.at[2 * k + 1],
                                                device_id=dev, device_id_type=MESH)

        def passed(k, chip, rows):
            return pltpu.make_async_remote_copy(wo_ref.at[chip, rows], wo_ref.at[chip, rows], fwd_send.at[k], fwd_recv.at[k],
                                                device_id=sib, device_id_type=MESH)

        sends = []
        for k, (px, py) in enumerate(others):
            sends += [from_chip(k, me, (px, py, c)), conv_from(k, me, (px, py, c))]
        for cp in sends:
            cp.start()
        fwds = []
        for k, (px, py) in enumerate(others):
            chip = 2 * px + py
            from_chip(k, chip, (px, py, c)).wait_recv()
            fwds.append(passed(k, chip, mine))
            fwds[-1].start()
        for k, (px, py) in enumerate(others):
            chip = 2 * px + py
            passed(k, chip, theirs).wait_recv()
            conv_from(k, chip, (px, py, c)).wait_recv()
        for cp in sends + fwds:
            cp.wait_send()
        for cp in loc:
            cp.wait()

    return pl.pallas_call(
        body, name="all_gather_weights", in_specs=[ANY, ANY], out_specs=[ANY, ANY],
        out_shape=[jax.ShapeDtypeStruct((N_CHIPS,) + wpack.shape, wpack.dtype), jax.ShapeDtypeStruct((N_CHIPS,) + cw.shape, cw.dtype)],
        scratch_shapes=[pltpu.SemaphoreType.DMA((6,)), pltpu.SemaphoreType.DMA((6,)), pltpu.SemaphoreType.DMA((3,)),
                        pltpu.SemaphoreType.DMA((3,)), pltpu.SemaphoreType.DMA((1,))],
        compiler_params=_comm_params(),
    )(wpack, cw)


def _exchange_pair(gp):
    def body(g_ref, r_ref, send_sems, recv_sems):
        x, y, c = _mesh_pos()
        cps = [pltpu.make_async_remote_copy(g_ref.at[s, 1 - c], r_ref.at[s], send_sems.at[s], recv_sems.at[s],
                                            device_id=(x, y, 1 - c), device_id_type=MESH) for s in range(N_CHIPS)]
        for cp in cps:
            cp.start()
        for cp in cps:
            cp.wait()

    return pl.pallas_call(
        body, name="grad_exchange_pair", in_specs=[ANY], out_specs=ANY,
        out_shape=jax.ShapeDtypeStruct((N_CHIPS,) + gp.shape[2:], gp.dtype),
        scratch_shapes=[pltpu.SemaphoreType.DMA((N_CHIPS,)), pltpu.SemaphoreType.DMA((N_CHIPS,))],
        compiler_params=_comm_params(),
    )(gp)


def _pair_sum(mine, from_sibling):
    def body(a_ref, b_ref, bf_ref, tail_ref):
        s = a_ref[0] + b_ref[0]
        bf_ref[0] = s.astype(BF)
        tail_ref[0] = s[SMALL_ROW0:, :]

    piece = pl.BlockSpec((1, HALF_ROWS, LANES), lambda i: (i, 0, 0))
    return pl.pallas_call(
        body, name="grad_pair_sum", grid=(N_CHIPS,), in_specs=[piece, piece],
        out_specs=[piece, pl.BlockSpec((1, SMALL_PIECE_ROWS, LANES), lambda i: (i, 0, 0))],
        out_shape=[jax.ShapeDtypeStruct((N_CHIPS, HALF_ROWS, LANES), BF), jax.ShapeDtypeStruct((N_CHIPS, SMALL_PIECE_ROWS, LANES), F32)],
        compiler_params=_params(("parallel",)),
    )(mine, from_sibling)


def _exchange_chips(ps_bf, ps_tail):
    def body(p_ref, t_ref, r_ref, rt_ref, send_sems, recv_sems):
        x, y, c = _mesh_pos()
        cps = []
        for k, (px, py) in enumerate(_other_chips(x, y)):
            cps.append(pltpu.make_async_remote_copy(p_ref.at[2 * px + py], r_ref.at[k], send_sems.at[2 * k], recv_sems.at[2 * k],
                                                    device_id=(px, py, c), device_id_type=MESH))
            cps.append(pltpu.make_async_remote_copy(t_ref.at[2 * px + py], rt_ref.at[k], send_sems.at[2 * k + 1],
                                                    recv_sems.at[2 * k + 1], device_id=(px, py, c), device_id_type=MESH))
        for cp in cps:
            cp.start()
        for cp in cps:
            cp.wait()

    return pl.pallas_call(
        body, name="grad_exchange_chips", in_specs=[ANY, ANY], out_specs=[ANY, ANY],
        out_shape=[jax.ShapeDtypeStruct((N_CHIPS - 1,) + ps_bf.shape[1:], ps_bf.dtype),
                   jax.ShapeDtypeStruct((N_CHIPS - 1,) + ps_tail.shape[1:], ps_tail.dtype)],
        scratch_shapes=[pltpu.SemaphoreType.DMA((6,)), pltpu.SemaphoreType.DMA((6,))],
        compiler_params=_comm_params(),
    )(ps_bf, ps_tail)


def _chip_sum(own_bf, own_tail, others_bf, others_tail):
    def body(ob_ref, ot_ref, b_ref, t_ref, o_ref):
        acc = ob_ref[0:SMALL_ROW0, :].astype(F32)
        tail = ot_ref[...]
        for k in range(N_CHIPS - 1):
            acc = acc + b_ref[k, 0:SMALL_ROW0, :].astype(F32)
            tail = tail + t_ref[k]
        o_ref[0:SMALL_ROW0, :] = acc
        o_ref[SMALL_ROW0:, :] = tail

    return pl.pallas_call(
        body, name="grad_chip_sum", out_shape=jax.ShapeDtypeStruct((HALF_ROWS, LANES), F32),
        compiler_params=_params(),
    )(own_bf, own_tail, others_bf, others_tail)


def _swap_reduced_halves(gh):
    def body(g_ref, o_ref, send_sem, recv_sem):
        x, y, c = _mesh_pos()
        cp = pltpu.make_async_remote_copy(g_ref, o_ref, send_sem, recv_sem, device_id=(x, y, 1 - c), device_id_type=MESH)
        cp.start()
        cp.wait()

    return pl.pallas_call(
        body, name="grad_swap_halves", in_specs=[ANY], out_specs=ANY,
        out_shape=jax.ShapeDtypeStruct(gh.shape, gh.dtype),
        scratch_shapes=[pltpu.SemaphoreType.DMA, pltpu.SemaphoreType.DMA],
        compiler_params=_comm_params(),
    )(gh)


def _gather_small(second_half):
    def body(gs_ref, sm_ref, send_sems, recv_sems, loc_sem):
        x, y, c = _mesh_pos()
        me = 2 * x + y
        small = gs_ref.at[pl.ds(SMALL_ROW0, SMALL_PIECE_ROWS)]
        loc = pltpu.make_async_copy(small, sm_ref.at[me], loc_sem)
        loc.start()
        cps = [pltpu.make_async_remote_copy(small, sm_ref.at[me], send_sems.at[k], recv_sems.at[k],
                                            device_id=(px, py, c), device_id_type=MESH)
               for k, (px, py) in enumerate(_other_chips(x, y))]
        for cp in cps:
            cp.start()
        for cp in cps:
            cp.wait()
        loc.wait()

    return pl.pallas_call(
        body, name="grad_gather_small", in_specs=[ANY], out_specs=ANY,
        out_shape=jax.ShapeDtypeStruct((N_CHIPS, SMALL_PIECE_ROWS, LANES), second_half.dtype),
        scratch_shapes=[pltpu.SemaphoreType.DMA((3,)), pltpu.SemaphoreType.DMA((3,)), pltpu.SemaphoreType.DMA],
        compiler_params=_comm_params(),
    )(second_half)


def _pack_grads(grads):
    small = jnp.concatenate([grads[n].reshape(-1) for n in SMALL])
    small = jnp.pad(small, (0, N_CHIPS * SMALL_PIECE_ROWS * LANES - SMALL_TOTAL)).reshape(N_CHIPS, SMALL_PIECE_ROWS, LANES)
    pieces = []
    for s in range(N_CHIPS):
        rows = []
        for n, (shp, axis) in BIG.items():
            g = grads[n]
            blk = g[s * shp[0]:(s + 1) * shp[0], :] if axis == 0 else g[:, s * shp[1]:(s + 1) * shp[1]]
            rows.append(blk.reshape(-1, LANES))
        rows.append(small[s])
        pieces.append(jnp.concatenate(rows, axis=0).reshape(2, HALF_ROWS, LANES))
    return jnp.stack(pieces)


def _unpack_shard(first_half, second_half):
    rows = jnp.concatenate([first_half, second_half], axis=0)
    out, r0 = {}, 0
    for n, (shp, _) in BIG.items():
        out[n] = rows[r0:r0 + BIG_ROWS[n]].reshape(shp)
        r0 += BIG_ROWS[n]
    return out


def _unpack_small(sm):
    flat = sm.reshape(-1)
    out, o = {}, 0
    for n, shp in SMALL.items():
        k = math.prod(shp)
        out[n] = flat[o:o + k].reshape(shp)
        o += k
    return out


def _as_2d(a):
    n = a.size
    if a.ndim >= 2 and a.shape[-1] > 1024:
        return a.reshape(-1, a.shape[-1])
    if n % 1024 == 0:
        return a.reshape(n // 1024, 1024)
    return a.reshape(1, n)


def _adamw(w, g, m, v, name):
    shape = w.shape
    w2, g2, m2, v2 = (_as_2d(a) for a in (w, g, m, v))
    rows, cols = w2.shape
    rb = 256 if rows >= 512 else rows

    def body(w_ref, g_ref, m_ref, v_ref, d_ref, mo_ref, vo_ref):
        gv = g_ref[...]
        mn = ADAM_B1 * m_ref[...] + (1.0 - ADAM_B1) * gv
        vn = ADAM_B2 * v_ref[...] + (1.0 - ADAM_B2) * (gv * gv)
        m_hat = mn / (1.0 - ADAM_B1 ** ADAM_STEP)
        v_hat = vn / (1.0 - ADAM_B2 ** ADAM_STEP)
        d_ref[...] = -ADAM_LR * (m_hat / (jnp.sqrt(v_hat) + ADAM_EPS) + ADAM_WD * w_ref[...])
        mo_ref[...] = mn
        vo_ref[...] = vn

    spec = _row_spec(rb, cols)
    sds = jax.ShapeDtypeStruct((rows, cols), F32)
    d, mo, vo = pl.pallas_call(
        body, name=name, grid=(rows // rb,), in_specs=[spec] * 4, out_specs=[spec] * 3, out_shape=[sds] * 3,
        compiler_params=_params(("parallel",)),
    )(w2, g2, m2, v2)
    return d.reshape(shape), mo.reshape(shape), vo.reshape(shape)


def kernel(x, p, norm_w, w_in, s5_A_re, s5_A_im, s5_log_dt, s5_B_re, s5_B_im, s5_C_re, s5_C_im, s5_D, s5_w_glu, s5_b_glu, conv_w, conv_b, dt_bias, A_log, ssd_D, ssd_norm_w, w_out, ple_norm_w, w_ple_gate, w_ple_proj, final_norm_w, loss_target, m_norm_w, m_w_in, m_s5_A_re, m_s5_A_im, m_s5_log_dt, m_s5_B_re, m_s5_B_im, m_s5_C_re, m_s5_C_im, m_s5_D, m_s5_w_glu, m_s5_b_glu, m_conv_w, m_conv_b, m_dt_bias, m_A_log, m_ssd_D, m_ssd_norm_w, m_w_out, m_ple_norm_w, m_w_ple_gate, m_w_ple_proj, m_final_norm_w, v_norm_w, v_w_in, v_s5_A_re, v_s5_A_im, v_s5_log_dt, v_s5_B_re, v_s5_B_im, v_s5_C_re, v_s5_C_im, v_s5_D, v_s5_w_glu, v_s5_b_glu, v_conv_w, v_conv_b, v_dt_bias, v_A_log, v_ssd_D, v_ssd_norm_w, v_w_out, v_ple_norm_w, v_w_ple_gate, v_w_ple_proj, v_final_norm_w):
    given = (norm_w, w_in, s5_A_re, s5_A_im, s5_log_dt, s5_B_re, s5_B_im, s5_C_re, s5_C_im, s5_D, s5_w_glu, s5_b_glu, conv_w, conv_b,
             dt_bias, A_log, ssd_D, ssd_norm_w, w_out, ple_norm_w, w_ple_gate, w_ple_proj, final_norm_w)
    given_m = (m_norm_w, m_w_in, m_s5_A_re, m_s5_A_im, m_s5_log_dt, m_s5_B_re, m_s5_B_im, m_s5_C_re, m_s5_C_im, m_s5_D, m_s5_w_glu,
               m_s5_b_glu, m_conv_w, m_conv_b, m_dt_bias, m_A_log, m_ssd_D, m_ssd_norm_w, m_w_out, m_ple_norm_w, m_w_ple_gate,
               m_w_ple_proj, m_final_norm_w)
    given_v = (v_norm_w, v_w_in, v_s5_A_re, v_s5_A_im, v_s5_log_dt, v_s5_B_re, v_s5_B_im, v_s5_C_re, v_s5_C_im, v_s5_D, v_s5_w_glu,
               v_s5_b_glu, v_conv_w, v_conv_b, v_dt_bias, v_A_log, v_ssd_D, v_ssd_norm_w, v_w_out, v_ple_norm_w, v_w_ple_gate,
               v_w_ple_proj, v_final_norm_w)
    wts, mom, var = dict(zip(WEIGHTS, given)), dict(zip(WEIGHTS, given_m)), dict(zip(WEIGHTS, given_v))
    drop = lambda n, a: a if n == "final_norm_w" else a[0]

    wpack = jnp.concatenate([drop(n, wts[n]).astype(BF).reshape(-1, LANES) for n in BIG], axis=0)
    wall, cwall = _all_gather_chips(wpack, drop("conv_w", wts["conv_w"]))
    chip = 2 * lax.axis_index("x") + lax.axis_index("y")
    is_own = (lax.broadcasted_iota(jnp.int32, (N_CHIPS, 1, 1), 0) == chip)
    full, r0 = {}, 0
    for n, (shp, axis) in BIG.items():
        blk = jnp.where(is_own, drop(n, wts[n]).astype(BF)[None], wall[:, r0:r0 + BIG_ROWS[n]].reshape((N_CHIPS,) + shp))
        full[n] = blk.reshape(N_CHIPS * shp[0], shp[1]) if axis == 0 else blk.transpose(1, 0, 2).reshape(shp[0], N_CHIPS * shp[1])
        r0 += BIG_ROWS[n]
    for n in SMALL:
        full[n] = drop(n, wts[n])
    full["conv_w"] = cwall.transpose(1, 0, 2).reshape(4, 2048)

    loss, gx, grads = _local_step(x[0], p[0, 0], loss_target[0], full)
    loss = lax.psum(loss, MESH_AXES)

    gp = _pack_grads({n: grads[n].reshape(SMALL[n]) if n in SMALL else grads[n] for n in WEIGHTS})
    c = lax.axis_index("c")
    from_sibling = _exchange_pair(gp)
    mine = lax.dynamic_index_in_dim(gp, c, axis=1, keepdims=False)
    ps_bf, ps_tail = _pair_sum(mine, from_sibling)
    others_bf, others_tail = _exchange_chips(ps_bf, ps_tail)
    own_bf = lax.dynamic_index_in_dim(ps_bf, chip, axis=0, keepdims=False)
    own_tail = lax.dynamic_index_in_dim(ps_tail, chip, axis=0, keepdims=False)
    reduced_half = _chip_sum(own_bf, own_tail, others_bf, others_tail)
    sibling_half = _swap_reduced_halves(reduced_half)
    first_half = jnp.where(c == 0, reduced_half, sibling_half)
    second_half = jnp.where(c == 0, sibling_half, reduced_half)
    sm = _gather_small(second_half)
    g_final = {**_unpack_small(sm), **_unpack_shard(first_half, second_half)}
    g_final["conv_w"] = lax.dynamic_slice_in_dim(g_final["conv_w"], chip * 512, 512, axis=1)

    outs_g, outs_d, outs_m, outs_v = [], [], [], []
    for n in WEIGHTS:
        g = g_final[n].reshape(wts[n].shape)
        d, mo, vo = _adamw(wts[n], g, mom[n], var[n], "adamw_" + n)
        outs_g.append(g)
        outs_d.append(d)
        outs_m.append(mo)
        outs_v.append(vo)
    return (loss, gx[None], *outs_g, *outs_d, *outs_m, *outs_v)
```

```python
import functools
import math

import jax
import jax.numpy as jnp
from jax import lax
from jax.experimental import pallas as pl
from jax.experimental.pallas import tpu as pltpu

F32 = jnp.float32
BF = jnp.bfloat16
EPS = 1e-6
CHUNK = 64
D_MODEL = 1024
S5_GROUPS = 64
S5_CH = 16
S5_STATE = 64
SSD_HEADS = 16
SSD_HEAD_DIM = 64
SSD_GROUPS = 4
SSD_STATE = 128
D_MAIN = 5120
LANES = 128
TOKEN_TILE = 256
VMEM_LIMIT = 56 * 1024 * 1024
MESH_AXES = ("x", "y", "c")
N_CHIPS = 4
ADAM_LR, ADAM_B1, ADAM_B2, ADAM_EPS, ADAM_WD, ADAM_STEP = 0.001, 0.9, 0.999, 1e-08, 0.01, 10
MESH = pl.DeviceIdType.MESH
ANY = pl.BlockSpec(memory_space=pl.ANY)


def _dot(a, b):
    return jnp.dot(a, b, preferred_element_type=F32)


def _dot_nt(a, b):
    return lax.dot_general(a, b, (((1,), (1,)), ((), ())), preferred_element_type=F32)


def _dot_tn(a, b):
    return lax.dot_general(a, b, (((0,), (0,)), ((), ())), preferred_element_type=F32)


def _sigmoid(x):
    return 1.0 / (1.0 + jnp.exp(-x))


def _softplus(x):
    return jnp.maximum(x, 0.0) + jnp.log(1.0 + jnp.exp(-jnp.abs(x)))


_GELU_C = math.sqrt(2.0 / math.pi)


def _gelu(x):
    return 0.5 * x * (1.0 + jnp.tanh(_GELU_C * (x + 0.044715 * x * x * x)))


def _gelu_grad(x):
    th = jnp.tanh(_GELU_C * (x + 0.044715 * x * x * x))
    return 0.5 * (1.0 + th) + 0.5 * x * (1.0 - th * th) * _GELU_C * (1.0 + 3.0 * 0.044715 * x * x)


def _params(sem=None):
    return pltpu.CompilerParams(dimension_semantics=sem, vmem_limit_bytes=VMEM_LIMIT)


def _row_spec(tl, width, col=0):
    return pl.BlockSpec((tl, width), lambda i, col=col: (i, col))


def _const_spec(shape):
    nd = len(shape)
    return pl.BlockSpec(shape, lambda *_: (0,) * nd)


def _in_proj_fwd(x, norm_w, w_main, w_dt):
    L = x.shape[0]
    tl = min(TOKEN_TILE, L)

    def body(x_ref, nw_ref, wm_ref, wd_ref, hn_ref, ps5_ref, pssd_ref, pd_ref):
        xv = x_ref[...]
        r = lax.rsqrt(jnp.mean(xv * xv, axis=-1, keepdims=True) + EPS)
        hn = (xv * r * nw_ref[...]).astype(BF)
        hn_ref[...] = hn
        for j in range(2):
            ps5_ref[:, j * 1024:(j + 1) * 1024] = _dot(hn, wm_ref[:, j * 1024:(j + 1) * 1024]).astype(BF)
        for j in range(3):
            pssd_ref[:, j * 1024:(j + 1) * 1024] = _dot(hn, wm_ref[:, (j + 2) * 1024:(j + 3) * 1024])
        pd_ref[...] = _dot(hn, wd_ref[...])

    return pl.pallas_call(
        body, name="in_proj_fwd", grid=(L // tl,),
        in_specs=[_row_spec(tl, D_MODEL), _const_spec((1, D_MODEL)), _const_spec((D_MODEL, D_MAIN)), _const_spec((D_MODEL, LANES))],
        out_specs=[_row_spec(tl, D_MODEL), _row_spec(tl, 2048), _row_spec(tl, 3072), _row_spec(tl, LANES)],
        out_shape=[jax.ShapeDtypeStruct((L, D_MODEL), BF), jax.ShapeDtypeStruct((L, 2048), BF), jax.ShapeDtypeStruct((L, 3072), F32),
                   jax.ShapeDtypeStruct((L, LANES), F32)],
        compiler_params=_params(("arbitrary",)),
    )(x, norm_w, w_main, w_dt)


def _in_proj_bwd(x, norm_w, dh1, du_flat, dyssm, s5_d, dzs, dzd, dxbc, ddt, w_main, w_dt):
    L = x.shape[0]
    tl = min(TOKEN_TILE, L)

    def body(x_ref, nw_ref, dh1_ref, duf_ref, dys_ref, d_ref, dzs_ref, dzd_ref, dxbc_ref, ddt_ref, wm_ref, wd_ref,
             gx_ref, du_ref, gnw_ref):
        @pl.when(pl.program_id(0) == 0)
        def _():
            gnw_ref[...] = jnp.zeros_like(gnw_ref)

        du = (duf_ref[...].astype(F32) + dys_ref[...].astype(F32) * d_ref[...]).astype(BF)
        du_ref[...] = du
        dhn = _dot_nt(du, wm_ref[:, 0:1024])
        dhn += _dot_nt(dzs_ref[...], wm_ref[:, 1024:2048])
        dhn += _dot_nt(dzd_ref[...], wm_ref[:, 2048:3072])
        dhn += _dot_nt(dxbc_ref[...], wm_ref[:, 3072:5120])
        dhn += _dot_nt(ddt_ref[...].astype(BF), wd_ref[...])
        xv = x_ref[...]
        r = lax.rsqrt(jnp.mean(xv * xv, axis=-1, keepdims=True) + EPS)
        xh = xv * r
        gnw_ref[...] += jnp.sum(dhn * xh, axis=0, keepdims=True)
        g = dhn * nw_ref[...]
        gx_ref[...] = dh1_ref[...] + r * (g - xh * jnp.mean(g * xh, axis=-1, keepdims=True))

    return pl.pallas_call(
        body, name="in_proj_bwd", grid=(L // tl,),
        in_specs=[_row_spec(tl, D_MODEL), _const_spec((1, D_MODEL)), _row_spec(tl, D_MODEL), _row_spec(tl, D_MODEL),
                  _row_spec(tl, D_MODEL), _const_spec((1, D_MODEL)), _row_spec(tl, D_MODEL), _row_spec(tl, D_MODEL),
                  _row_spec(tl, 2048), _row_spec(tl, LANES), _const_spec((D_MODEL, D_MAIN)), _const_spec((D_MODEL, LANES))],
        out_specs=[_row_spec(tl, D_MODEL), _row_spec(tl, D_MODEL), _const_spec((1, D_MODEL))],
        out_shape=[jax.ShapeDtypeStruct((L, D_MODEL), F32), jax.ShapeDtypeStruct((L, D_MODEL), BF), jax.ShapeDtypeStruct((1, D_MODEL), F32)],
        compiler_params=_params(("arbitrary",)),
    )(x, norm_w, dh1, du_flat, dyssm, s5_d, dzs, dzd, dxbc, ddt, w_main, w_dt)


def _matmul_tn(a, b, name):
    L, M = a.shape
    N = b.shape[1]
    tm, tn, tk = min(M, 1024), min(N, 1024), min(L, 512)

    def body(a_ref, b_ref, o_ref):
        @pl.when(pl.program_id(2) == 0)
        def _():
            o_ref[...] = jnp.zeros_like(o_ref)

        o_ref[...] += _dot_tn(a_ref[...].astype(BF), b_ref[...].astype(BF))

    return pl.pallas_call(
        body, name=name, grid=(M // tm, N // tn, L // tk),
        in_specs=[pl.BlockSpec((tk, tm), lambda i, j, k: (k, i)), pl.BlockSpec((tk, tn), lambda i, j, k: (k, j))],
        out_specs=pl.BlockSpec((tm, tn), lambda i, j, k: (i, j)),
        out_shape=jax.ShapeDtypeStruct((M, N), F32),
        compiler_params=_params(("parallel", "parallel", "arbitrary")),
    )(a, b)


def _s5_discretise(a_re, a_im, log_dt, b_re, b_im, c_re, c_im):
    dt = jnp.exp(log_dt)[:, None]
    tau = jnp.arange(CHUNK + 1, dtype=F32)
    mag = jnp.exp((a_re * dt)[:, :, None] * tau)
    ang = (a_im * dt)[:, :, None] * tau
    pw_re, pw_im = mag * jnp.cos(ang), mag * jnp.sin(ang)
    er, ei = pw_re[:, :, 1] - 1.0, pw_im[:, :, 1]
    den = a_re * a_re + a_im * a_im
    beta_re, beta_im = (er * a_re + ei * a_im) / den, (ei * a_re - er * a_im) / den
    bb_re = (beta_re[:, :, None] * b_re - beta_im[:, :, None] * b_im).transpose(0, 2, 1)
    bb_im = (beta_re[:, :, None] * b_im + beta_im[:, :, None] * b_re).transpose(0, 2, 1)
    return bb_re, bb_im, c_re, c_im, pw_re, pw_im


def _s5_kmat(bb_re, bb_im, c_re, c_im, pw_re, pw_im):
    hi = lax.Precision.HIGHEST
    m_re = bb_re[:, :, None, :] * c_re[:, None, :, :] - bb_im[:, :, None, :] * c_im[:, None, :, :]
    m_im = bb_re[:, :, None, :] * c_im[:, None, :, :] + bb_im[:, :, None, :] * c_re[:, None, :, :]
    k4 = (jnp.einsum("ghkn,gnt->ghtk", m_re, pw_re[:, :, :CHUNK], precision=hi)
          - jnp.einsum("ghkn,gnt->ghtk", m_im, pw_im[:, :, :CHUNK], precision=hi))
    return k4.reshape(S5_GROUPS, S5_CH, CHUNK * S5_CH)


def _s5_table_factors(bb_re, bb_im, c_re, c_im, pw_re, pw_im):
    pr = pw_re[:, :, CHUNK - 1::-1].transpose(0, 2, 1)
    pi = pw_im[:, :, CHUNK - 1::-1].transpose(0, 2, 1)
    bbp = jnp.concatenate([bb_re, bb_im], axis=-1)
    pwr = jnp.concatenate([pr, pi], axis=-1)
    cnp = jnp.concatenate([c_re, c_im], axis=-1)
    pw1 = jnp.concatenate([pw_re[:, :, 1:CHUNK + 1].transpose(0, 2, 1), pw_im[:, :, 1:CHUNK + 1].transpose(0, 2, 1)], axis=-1)
    return bbp, pwr, cnp, pw1


def _s5_small_cotangents(r12, q12, xy, uv, da64):
    n = S5_STATE
    fold = lambda a: a[..., :n] + a[..., n:]
    fold_m = lambda a: a[..., n:] - a[..., :n]
    dbb_re, dbb_im = fold(r12[:, :S5_CH]), fold_m(r12[:, S5_CH:])
    dpr, dpi = fold(q12[:, :CHUNK]), fold_m(q12[:, CHUNK:])
    dc_re, dc_im = -fold_m(xy[:, :S5_CH]), -fold(xy[:, S5_CH:])
    dp1_re, dp1_im = -fold_m(uv[:, :CHUNK]), -fold(uv[:, CHUNK:])
    zero = jnp.zeros((S5_GROUPS, n, 1), F32)
    dpw_re = (jnp.concatenate([dpr[:, ::-1].transpose(0, 2, 1), zero], axis=-1)
              + jnp.concatenate([zero, dp1_re.transpose(0, 2, 1)], axis=-1)).at[:, :, CHUNK].add(da64[:, :n])
    dpw_im = (jnp.concatenate([dpi[:, ::-1].transpose(0, 2, 1), zero], axis=-1)
              + jnp.concatenate([zero, dp1_im.transpose(0, 2, 1)], axis=-1)).at[:, :, CHUNK].add(da64[:, n:])
    return dbb_re, dbb_im, dc_re, dc_im, dpw_re, dpw_im


def _s5_scan_powers(a_re, a_im, log_dt, nsteps):
    dt = jnp.exp(log_dt)[:, None]
    steps = (CHUNK * (2.0 ** jnp.arange(8, dtype=F32)))[None, :, None]
    mag = jnp.exp((a_re * dt)[:, None, :] * steps)
    ang = (a_im * dt)[:, None, :] * steps
    re, im = mag * jnp.cos(ang), mag * jnp.sin(ang)
    del nsteps
    return jnp.concatenate([re, re], -1), jnp.concatenate([-im, im], -1)


def _build_toeplitz(kmat_ref, tg_ref):
    lane = lax.broadcasted_iota(jnp.int32, (CHUNK, CHUNK * S5_CH), 1)
    srow = lax.broadcasted_iota(jnp.int32, (CHUNK, CHUNK * S5_CH), 0)
    keep = lane >= S5_CH * srow
    for h in range(S5_CH):
        row = jnp.broadcast_to(kmat_ref[0, h:h + 1, :], (CHUNK, CHUNK * S5_CH))
        rolled = pltpu.roll(row, 0, 1, stride=S5_CH, stride_axis=0)
        tg_ref[h * CHUNK:(h + 1) * CHUNK, :] = jnp.where(keep, rolled, 0.0).astype(BF)


def _swap_halves(x):
    return pltpu.roll(x, S5_STATE, 1)


def _build_state_tables(bbp_ref, pwr_ref, cnp_ref, pw1_ref, wst_ref, wofft_ref):
    lane = lax.broadcasted_iota(jnp.int32, (1, 2 * S5_STATE), 1)
    left = lane < S5_STATE
    pwr = pwr_ref[0]
    pwr_sw = _swap_halves(pwr)
    for hh in range(S5_CH):
        bb = bbp_ref[0, hh:hh + 1, :]
        bb_sw = _swap_halves(bb)
        wst_ref[hh * CHUNK:(hh + 1) * CHUNK, :] = (jnp.where(left, bb, bb_sw) * pwr
                                                   + jnp.where(left, -bb_sw, bb) * pwr_sw).astype(BF)
    cn = cnp_ref[0]
    cn_sw = _swap_halves(cn)
    c_a = jnp.where(left, cn, -cn_sw)
    c_b = jnp.where(left, -cn_sw, -cn)
    for t in range(CHUNK):
        p = pw1_ref[0, t:t + 1, :]
        wofft_ref[t * S5_CH:(t + 1) * S5_CH, :] = (c_a * p + c_b * _swap_halves(p)).astype(BF)


def _s5_core_fwd(uflat, kmat, bbp, pwr, cnp, pw1, p1, p2):
    G, nc, W = uflat.shape
    nsteps = max(1, (nc - 1).bit_length())

    def body(u_ref, k_ref, bbp_ref, pwr_ref, cnp_ref, pw1_ref, p1_ref, p2_ref, y_ref, h_ref, tg_ref, wst_ref, wofft_ref):
        _build_toeplitz(k_ref, tg_ref)
        _build_state_tables(bbp_ref, pwr_ref, cnp_ref, pw1_ref, wst_ref, wofft_ref)
        u = u_ref[0]
        x = _dot(u, wst_ref[...])
        row = lax.broadcasted_iota(jnp.int32, x.shape, 0)
        d = 1
        for k in range(nsteps):
            sh = jnp.where(row >= d, pltpu.roll(x, d, 0), 0.0)
            x = x + p1_ref[0, k:k + 1, :] * sh + p2_ref[0, k:k + 1, :] * _swap_halves(sh)
            d *= 2
        h = jnp.where(row >= 1, pltpu.roll(x, 1, 0), 0.0)
        h_ref[0] = h
        y = _dot(u, tg_ref[...]) + _dot_nt(h.astype(BF), wofft_ref[...])
        y_ref[0] = y.astype(BF)

    spec_g = lambda a, b: pl.BlockSpec((1, a, b), lambda g: (g, 0, 0))
    return pl.pallas_call(
        body, name="s5_core_fwd", grid=(G,),
        in_specs=[spec_g(nc, W), spec_g(S5_CH, W), spec_g(S5_CH, 2 * S5_STATE), spec_g(CHUNK, 2 * S5_STATE),
                  spec_g(S5_CH, 2 * S5_STATE), spec_g(CHUNK, 2 * S5_STATE), spec_g(8, 2 * S5_STATE), spec_g(8, 2 * S5_STATE)],
        out_specs=[spec_g(nc, W), spec_g(nc, 2 * S5_STATE)],
        out_shape=[jax.ShapeDtypeStruct((G, nc, W), BF), jax.ShapeDtypeStruct((G, nc, 2 * S5_STATE), F32)],
        scratch_shapes=[pltpu.VMEM((W, W), BF), pltpu.VMEM((W, 2 * S5_STATE), BF), pltpu.VMEM((W, 2 * S5_STATE), BF)],
        compiler_params=_params(("arbitrary",)),
    )(uflat, kmat, bbp, pwr, cnp, pw1, p1, p2)


def _s5_core_bwd(uflat, dyflat, hsave, kmat, bbp, pwr, cnp, pw1, p1, p2):
    G, nc, W = uflat.shape
    nsteps = max(1, (nc - 1).bit_length())

    def body(u_ref, dy_ref, h_ref, k_ref, bbp_ref, pwr_ref, cnp_ref, pw1_ref, p1_ref, p2_ref,
             du_ref, dk_ref, r12_ref, q12_ref, xy_ref, uv_ref, da_ref, tg_ref, flip_ref, wst_ref, wofft_ref):
        @pl.when(pl.program_id(0) == 0)
        def _():
            r = lax.broadcasted_iota(jnp.int32, (W, W), 0)
            c = lax.broadcasted_iota(jnp.int32, (W, W), 1)
            flip_ref[...] = (((r >> 6) == (c >> 6)) & ((r & (CHUNK - 1)) + (c & (CHUNK - 1)) == CHUNK - 1)).astype(BF)

        _build_toeplitz(k_ref, tg_ref)
        _build_state_tables(bbp_ref, pwr_ref, cnp_ref, pw1_ref, wst_ref, wofft_ref)
        u = u_ref[0]
        dy = dy_ref[0]
        h = h_ref[0]
        gh = _dot(dy, wofft_ref[...])
        row = lax.broadcasted_iota(jnp.int32, gh.shape, 0)
        x = jnp.where(row < nc - 1, pltpu.roll(gh, nc - 1, 0), 0.0)
        d = 1
        for k in range(nsteps):
            sh = jnp.where(row < nc - d, pltpu.roll(x, nc - d, 0), 0.0)
            x = x + p1_ref[0, k:k + 1, :] * sh - p2_ref[0, k:k + 1, :] * _swap_halves(sh)
            d *= 2
        gs = x.astype(BF)
        du_ref[0] = (_dot_nt(dy, tg_ref[...]) + _dot_nt(gs, wst_ref[...])).astype(BF)

        dwst = _dot_tn(u, gs)
        pwr = pwr_ref[0]
        pwr_sw = _swap_halves(pwr)
        q1 = jnp.zeros((CHUNK, 2 * S5_STATE), F32)
        q2 = jnp.zeros((CHUNK, 2 * S5_STATE), F32)
        r1_rows, r2_rows = [], []
        for hh in range(S5_CH):
            blk = dwst[hh * CHUNK:(hh + 1) * CHUNK, :]
            r1_rows.append(jnp.sum(blk * pwr, axis=0, keepdims=True))
            r2_rows.append(jnp.sum(blk * pwr_sw, axis=0, keepdims=True))
            bb = bbp_ref[0, hh:hh + 1, :]
            q1 = q1 + blk * bb
            q2 = q2 + blk * _swap_halves(bb)
        r12_ref[0] = jnp.concatenate(r1_rows + r2_rows, axis=0)
        q12_ref[0] = jnp.concatenate([q1, q2], axis=0)

        dwofft = _dot_tn(dy, h.astype(BF))
        cn = cnp_ref[0]
        cn_sw = _swap_halves(cn)
        xa = jnp.zeros((S5_CH, 2 * S5_STATE), F32)
        ya = jnp.zeros((S5_CH, 2 * S5_STATE), F32)
        ru_rows, rv_rows = [], []
        for t in range(CHUNK):
            blk = dwofft[t * S5_CH:(t + 1) * S5_CH, :]
            p = pw1_ref[0, t:t + 1, :]
            xa = xa + blk * p
            ya = ya + blk * _swap_halves(p)
            ru_rows.append(jnp.sum(blk * cn, axis=0, keepdims=True))
            rv_rows.append(jnp.sum(blk * cn_sw, axis=0, keepdims=True))
        xy_ref[0] = jnp.concatenate([xa, ya], axis=0)
        uv_ref[0] = jnp.concatenate(ru_rows + rv_rows, axis=0)
        r1 = jnp.sum(x * h, axis=0, keepdims=True)
        r2 = jnp.sum(x * _swap_halves(h), axis=0, keepdims=True)
        da_ref[0] = jnp.concatenate([r1, r2, jnp.zeros((6, 2 * S5_STATE), F32)], axis=0)
        lane = lax.broadcasted_iota(jnp.int32, (CHUNK, W), 1)
        srow = lax.broadcasted_iota(jnp.int32, (CHUNK, W), 0)
        keep = lane < S5_CH * (srow + 1)
        ur = _dot(u, flip_ref[...]).astype(BF)
        for hh in range(S5_CH):
            dt_h = _dot_tn(ur[:, hh * CHUNK:(hh + 1) * CHUNK], dy)
            back = pltpu.roll(dt_h, S5_CH, 1, stride=S5_CH, stride_axis=0)
            dk_ref[0, hh:hh + 1, :] = jnp.sum(jnp.where(keep, back, 0.0), axis=0, keepdims=True)

    spec_g = lambda a, b: pl.BlockSpec((1, a, b), lambda g: (g, 0, 0))
    return pl.pallas_call(
        body, name="s5_core_bwd", grid=(G,),
        in_specs=[spec_g(nc, W), spec_g(nc, W), spec_g(nc, 2 * S5_STATE), spec_g(S5_CH, W), spec_g(S5_CH, 2 * S5_STATE),
                  spec_g(CHUNK, 2 * S5_STATE), spec_g(S5_CH, 2 * S5_STATE), spec_g(CHUNK, 2 * S5_STATE),
                  spec_g(8, 2 * S5_STATE), spec_g(8, 2 * S5_STATE)],
        out_specs=[spec_g(nc, W), spec_g(S5_CH, W), spec_g(2 * S5_CH, 2 * S5_STATE), spec_g(2 * CHUNK, 2 * S5_STATE),
                   spec_g(2 * S5_CH, 2 * S5_STATE), spec_g(2 * CHUNK, 2 * S5_STATE), spec_g(8, 2 * S5_STATE)],
        out_shape=[jax.ShapeDtypeStruct((G, nc, W), BF), jax.ShapeDtypeStruct((G, S5_CH, W), F32),
                   jax.ShapeDtypeStruct((G, 2 * S5_CH, 2 * S5_STATE), F32), jax.ShapeDtypeStruct((G, 2 * CHUNK, 2 * S5_STATE), F32),
                   jax.ShapeDtypeStruct((G, 2 * S5_CH, 2 * S5_STATE), F32), jax.ShapeDtypeStruct((G, 2 * CHUNK, 2 * S5_STATE), F32),
                   jax.ShapeDtypeStruct((G, 8, 2 * S5_STATE), F32)],
        scratch_shapes=[pltpu.VMEM((W, W), BF), pltpu.VMEM((W, W), BF), pltpu.VMEM((W, 2 * S5_STATE), BF),
                        pltpu.VMEM((W, 2 * S5_STATE), BF)],
        compiler_params=_params(("arbitrary",)),
    )(uflat, dyflat, hsave, kmat, bbp, pwr, cnp, pw1, p1, p2)


def _flat_hs(a, nc):
    return a.reshape(nc, CHUNK, S5_GROUPS, S5_CH).transpose(2, 0, 3, 1).reshape(S5_GROUPS, nc, CHUNK * S5_CH)


def _unflat_hs(a, nc):
    return a.reshape(S5_GROUPS, nc, S5_CH, CHUNK).transpose(1, 3, 0, 2).reshape(nc * CHUNK, D_MODEL)


def _flat_tk(a, nc):
    return a.reshape(nc, CHUNK, S5_GROUPS, S5_CH).transpose(2, 0, 1, 3).reshape(S5_GROUPS, nc, CHUNK * S5_CH)


def _unflat_tk(a, nc):
    return a.reshape(S5_GROUPS, nc, CHUNK, S5_CH).transpose(1, 2, 0, 3).reshape(nc * CHUNK, D_MODEL)


def _s5_post_fwd(yssm, proj, s5_d, w_glu, b_glu):
    L = yssm.shape[0]
    tl = min(TOKEN_TILE, L)

    def body(ys_ref, u_ref, z_ref, d_ref, wg_ref, bg_ref, o_ref):
        u = u_ref[...].astype(F32)
        a = _gelu(ys_ref[...].astype(F32) + d_ref[...] * u)
        y = a * _sigmoid(_dot(a.astype(BF), wg_ref[...]) + bg_ref[...])
        z = z_ref[...].astype(F32)
        o_ref[...] = (y * z * _sigmoid(z)).astype(BF)

    return pl.pallas_call(
        body, name="s5_post_fwd", grid=(L // tl,),
        in_specs=[_row_spec(tl, D_MODEL), _row_spec(tl, D_MODEL, 0), _row_spec(tl, D_MODEL, 1), _const_spec((1, D_MODEL)),
                  _const_spec((D_MODEL, D_MODEL)), _const_spec((1, D_MODEL))],
        out_specs=_row_spec(tl, D_MODEL),
        out_shape=jax.ShapeDtypeStruct((L, D_MODEL), BF),
        compiler_params=_params(("arbitrary",)),
    )(yssm, proj, proj, s5_d, w_glu, b_glu)


def _s5_post_bwd(dys5, yssm, proj, s5_d, w_glu, b_glu):
    L = yssm.shape[0]
    tl = min(TOKEN_TILE, L)

    def body(dy_ref, ys_ref, u_ref, z_ref, d_ref, wg_ref, bg_ref, dz_ref, dys_ref, a_ref, dgl_ref, dbg_ref, dd_ref):
        @pl.when(pl.program_id(0) == 0)
        def _():
            dbg_ref[...] = jnp.zeros_like(dbg_ref)
            dd_ref[...] = jnp.zeros_like(dd_ref)

        u = u_ref[...].astype(F32)
        y0 = ys_ref[...].astype(F32) + d_ref[...] * u
        a = _gelu(y0)
        a_bf = a.astype(BF)
        sg = _sigmoid(_dot(a_bf, wg_ref[...]) + bg_ref[...])
        y = a * sg
        z = z_ref[...].astype(F32)
        sz = _sigmoid(z)
        dout = dy_ref[...].astype(F32)
        dz_ref[...] = (dout * y * sz * (1.0 + z * (1.0 - sz))).astype(BF)
        dyv = dout * z * sz
        dgl = dyv * a * sg * (1.0 - sg)
        dgl_bf = dgl.astype(BF)
        da = dyv * sg + _dot_nt(dgl_bf, wg_ref[...])
        dy0 = da * _gelu_grad(y0)
        dbg_ref[...] += jnp.sum(dgl, axis=0, keepdims=True)
        dd_ref[...] += jnp.sum(dy0 * u, axis=0, keepdims=True)
        dys_ref[...] = dy0.astype(BF)
        a_ref[...] = a_bf
        dgl_ref[...] = dgl_bf

    big = jax.ShapeDtypeStruct((L, D_MODEL), BF)
    vec = jax.ShapeDtypeStruct((1, D_MODEL), F32)
    return pl.pallas_call(
        body, name="s5_post_bwd", grid=(L // tl,),
        in_specs=[_row_spec(tl, D_MODEL), _row_spec(tl, D_MODEL), _row_spec(tl, D_MODEL, 0), _row_spec(tl, D_MODEL, 1),
                  _const_spec((1, D_MODEL)), _const_spec((D_MODEL, D_MODEL)), _const_spec((1, D_MODEL))],
        out_specs=[_row_spec(tl, D_MODEL)] * 4 + [_const_spec((1, D_MODEL))] * 2,
        out_shape=[big, big, big, big, vec, vec],
        compiler_params=_params(("arbitrary",)),
    )(dys5, yssm, proj, proj, s5_d, w_glu, b_glu)


def _cumsum_rows(a):
    row = lax.broadcasted_iota(jnp.int32, a.shape, 0)
    d = 1
    while d < a.shape[0]:
        a = a + jnp.where(row >= d, pltpu.roll(a, d, 0), 0.0)
        d *= 2
    return a


def _rev_cumsum_rows(a):
    n = a.shape[0]
    row = lax.broadcasted_iota(jnp.int32, a.shape, 0)
    d = 1
    while d < n:
        a = a + jnp.where(row < n - d, pltpu.roll(a, n - d, 0), 0.0)
        d *= 2
    return a


def _ssd_conv_fwd(first, xs_ref, bc_ref, hx_ref, hb_ref, cw_ref, cb_ref, xp_ref, tl):
    hal = jnp.concatenate([hx_ref[...], hb_ref[...]], axis=1)
    xp_ref[0:8, :] = jnp.where(first, 0.0, hal)
    xp_ref[8:8 + tl, 0:1024] = xs_ref[...]
    xp_ref[8:8 + tl, 1024:2048] = bc_ref[...]
    pre = cb_ref[...] + cw_ref[0:1, :] * xp_ref[5:5 + tl, :]
    for k in range(1, 4):
        pre = pre + cw_ref[k:k + 1, :] * xp_ref[5 + k:5 + k + tl, :]
    return pre


def _onehot_lane(h):
    return (lax.broadcasted_iota(jnp.int32, (1, LANES), 1) == h).astype(F32)


def _dot_exact(x, e):
    hi = x.astype(BF)
    r = x - hi.astype(F32)
    mid = r.astype(BF)
    lo = (r - mid.astype(F32)).astype(BF)
    return _dot(hi, e) + _dot(mid, e) + _dot(lo, e)


def _head_expand_matrices():
    e = lax.broadcasted_iota(jnp.int32, (LANES, D_MODEL), 0) == (lax.broadcasted_iota(jnp.int32, (LANES, D_MODEL), 1) >> 6)
    et = (lax.broadcasted_iota(jnp.int32, (D_MODEL, LANES), 0) >> 6) == lax.broadcasted_iota(jnp.int32, (D_MODEL, LANES), 1)
    return e.astype(BF), et.astype(BF)


def _group_masks():
    r64 = lax.broadcasted_iota(jnp.int32, (4 * CHUNK, CHUNK), 0)
    causal4 = (r64 & (CHUNK - 1)) >= lax.broadcasted_iota(jnp.int32, (4 * CHUNK, CHUNK), 1)
    r256 = lax.broadcasted_iota(jnp.int32, (4 * CHUNK, 4 * SSD_HEAD_DIM), 0)
    same = (r256 >> 6) == (lax.broadcasted_iota(jnp.int32, (4 * CHUNK, 4 * SSD_HEAD_DIM), 1) >> 6)
    return causal4, same


def _group_decay(acs, acs_t, j, causal4):
    col = jnp.concatenate([acs[:, 4 * j + hh:4 * j + hh + 1] for hh in range(4)], axis=0)
    rowv = jnp.concatenate([jnp.broadcast_to(acs_t[4 * j + hh:4 * j + hh + 1, :], (CHUNK, CHUNK)) for hh in range(4)], axis=0)
    return jnp.where(causal4, jnp.exp(col - rowv), 0.0)


def _group_last_decay(acs_t, j):
    return jnp.concatenate([jnp.broadcast_to(jnp.exp(acs_t[4 * j + hh:4 * j + hh + 1, CHUNK - 1:CHUNK]), (SSD_HEAD_DIM, 1))
                            for hh in range(4)], axis=0)


def _fold_heads(r):
    return r[0:CHUNK] + r[CHUNK:2 * CHUNK] + r[2 * CHUNK:3 * CHUNK] + r[3 * CHUNK:4 * CHUNK]


def _ssd_specs_in(tl, nt, rev):
    t_of = (lambda i: nt - 1 - i) if rev else (lambda i: i)
    rows = lambda w, col: pl.BlockSpec((tl, w), lambda i: (t_of(i), col))
    halo = lambda col: pl.BlockSpec((8, 1024), lambda i: (jnp.maximum(t_of(i) * (tl // 8) - 1, 0), col))
    return t_of, rows, halo


def _ssd_fwd(proj, pdt, conv_w, conv_b, dt_bias, a_log, ssd_d, norm_w):
    L = proj.shape[0]
    tl = min(TOKEN_TILE, L)
    nt, ncl = L // tl, tl // CHUNK
    _, rows, halo = _ssd_specs_in(tl, nt, False)

    def body(xs_ref, bc_ref, hx_ref, hb_ref, dt_ref, z_ref, cw_ref, cb_ref, dtb_ref, al_ref, dd_ref, nw_ref,
             y_ref, ypre_ref, st_ref, xp_ref, xbc_ref, dts_ref, hst_ref):
        i = pl.program_id(0)

        @pl.when(i == 0)
        def _():
            hst_ref[...] = jnp.zeros_like(hst_ref)

        pre = _ssd_conv_fwd(i == 0, xs_ref, bc_ref, hx_ref, hb_ref, cw_ref, cb_ref, xp_ref, tl)
        xbc_ref[...] = pre * _sigmoid(pre)
        dts_ref[...] = _softplus(dt_ref[...] + dtb_ref[...])
        a_neg = -jnp.exp(al_ref[...])
        e16, _ = _head_expand_matrices()
        causal4, same = _group_masks()
        dd_x = _dot_exact(jnp.broadcast_to(dd_ref[...], (8, LANES)), e16)[0:1, :]

        def chunk(c, carry):
            r0 = pl.multiple_of(c * CHUNK, CHUNK)
            xbc = xbc_ref[pl.ds(r0, CHUNK), :]
            dtc = dts_ref[pl.ds(r0, CHUNK), :]
            acs = _cumsum_rows(dtc * a_neg)
            acs_t = acs.T
            acs_x = _dot_exact(acs, e16)
            xs = xbc[:, 0:1024]
            xd = xs * _dot_exact(dtc, e16)
            xd_bf = xd.astype(BF)
            xdd = (xd * jnp.exp(acs_x[CHUNK - 1:CHUNK, :] - acs_x)).astype(BF)
            e_x = jnp.exp(acs_x)
            for j in range(SSD_GROUPS):
                sl = slice(256 * j, 256 * (j + 1))
                bj = xbc[:, 1024 + 128 * j:1024 + 128 * (j + 1)].astype(BF)
                cj = xbc[:, 1536 + 128 * j:1536 + 128 * (j + 1)].astype(BF)
                g = _dot_nt(cj, bj)
                hj = hst_ref[sl, :]
                zj = _dot_nt(cj, hj.astype(BF))
                sc = (jnp.concatenate([g] * 4, axis=0) * _group_decay(acs, acs_t, j, causal4)).astype(BF)
                yd = _fold_heads(jnp.where(same, _dot(sc, xd_bf[:, sl]), 0.0))
                ypre_ref[pl.ds(r0, CHUNK), sl] = yd + e_x[:, sl] * zj + dd_x[:, sl] * xs[:, sl]
                st_ref[c, sl, :] = hj
                hst_ref[sl, :] = _group_last_decay(acs_t, j) * hj + _dot_tn(xdd[:, sl], bj)
            return carry

        lax.fori_loop(0, ncl, chunk, 0)
        z = z_ref[...]
        gg = ypre_ref[...] * z * _sigmoid(z)
        for j in range(SSD_GROUPS):
            seg = gg[:, 256 * j:256 * (j + 1)]
            r = lax.rsqrt(jnp.mean(seg * seg, axis=-1, keepdims=True) + EPS)
            y_ref[:, 256 * j:256 * (j + 1)] = (seg * r * nw_ref[:, 256 * j:256 * (j + 1)]).astype(BF)

    nc = L // CHUNK
    return pl.pallas_call(
        body, name="ssd_fwd", grid=(nt,),
        in_specs=[rows(1024, 1), rows(1024, 2), halo(1), halo(2), rows(LANES, 0), rows(1024, 0),
                  _const_spec((4, 2048)), _const_spec((1, 2048)), _const_spec((1, LANES)), _const_spec((1, LANES)),
                  _const_spec((1, LANES)), _const_spec((1, D_MODEL))],
        out_specs=[_row_spec(tl, D_MODEL), _row_spec(tl, D_MODEL), pl.BlockSpec((ncl, 1024, SSD_STATE), lambda i: (i, 0, 0))],
        out_shape=[jax.ShapeDtypeStruct((L, D_MODEL), BF), jax.ShapeDtypeStruct((L, D_MODEL), F32),
                   jax.ShapeDtypeStruct((nc, 1024, SSD_STATE), F32)],
        scratch_shapes=[pltpu.VMEM((tl + 8, 2048), F32), pltpu.VMEM((tl, 2048), F32), pltpu.VMEM((tl, LANES), F32),
                        pltpu.VMEM((1024, SSD_STATE), F32)],
        compiler_params=_params(("arbitrary",)),
    )(proj, proj, proj, proj, pdt, proj, conv_w, conv_b, dt_bias, a_log, ssd_d, norm_w)


def _ssd_bwd(dyssd, ypre, proj, pdt, states, conv_w, conv_b, dt_bias, a_log, ssd_d, norm_w):
    L = proj.shape[0]
    tl = min(TOKEN_TILE, L)
    nt, ncl = L // tl, tl // CHUNK
    t_of, rows, halo = _ssd_specs_in(tl, nt, True)

    def body(dy_ref, ypre_ref, z_ref, xs_ref, bc_ref, hx_ref, hb_ref, dt_ref, st_ref, cw_ref, cb_ref, dtb_ref, al_ref,
             dd_ref, nw_ref,
             dxbc_ref, ddt_ref, dz_ref, dcw_ref, dcb_ref, ddtb_ref, dal_ref, ddd_ref, dnw_ref,
             xp_ref, xbc_ref, pre_ref, dts_ref, dyp_ref, dxs_ref, ddts_ref, dp_ref, dh_ref):
        i = pl.program_id(0)

        @pl.when(i == 0)
        def _():
            for r in (dcw_ref, dcb_ref, ddtb_ref, dal_ref, ddd_ref, dnw_ref, dh_ref):
                r[...] = jnp.zeros_like(r)
            dp_ref[tl:tl + 8, :] = jnp.zeros((8, 2048), F32)

        pre = _ssd_conv_fwd(t_of(i) == 0, xs_ref, bc_ref, hx_ref, hb_ref, cw_ref, cb_ref, xp_ref, tl)
        pre_ref[...] = pre
        xbc_ref[...] = pre * _sigmoid(pre)
        dts_ref[...] = _softplus(dt_ref[...] + dtb_ref[...])
        a_neg = -jnp.exp(al_ref[...])

        ypre = ypre_ref[...]
        z = z_ref[...]
        sz = _sigmoid(z)
        gg = ypre * z * sz
        dout = dy_ref[...]
        for j in range(SSD_GROUPS):
            sl = slice(256 * j, 256 * (j + 1))
            seg = gg[:, sl]
            r = lax.rsqrt(jnp.mean(seg * seg, axis=-1, keepdims=True) + EPS)
            gh = seg * r
            dnw_ref[:, sl] += jnp.sum(dout[:, sl] * gh, axis=0, keepdims=True)
            gw = dout[:, sl] * nw_ref[:, sl]
            dgg = r * (gw - gh * jnp.mean(gw * gh, axis=-1, keepdims=True))
            dyp_ref[:, sl] = dgg * z[:, sl] * sz[:, sl]
            dz_ref[:, sl] = (dgg * ypre[:, sl] * sz[:, sl] * (1.0 + z[:, sl] * (1.0 - sz[:, sl]))).astype(BF)

        e16, e16t = _head_expand_matrices()
        causal4, same = _group_masks()
        dd_x = _dot_exact(jnp.broadcast_to(dd_ref[...], (8, LANES)), e16)[0:1, :]
        last_row = (lax.broadcasted_iota(jnp.int32, (CHUNK, 1), 0) == CHUNK - 1).astype(F32)
        sel_rows = lax.broadcasted_iota(jnp.int32, (4 * CHUNK, LANES), 0) >> 6
        sel_lanes = lax.broadcasted_iota(jnp.int32, (4 * CHUNK, LANES), 1)

        def chunk(k, carry):
            dal_acc, ddx_acc = carry
            c = ncl - 1 - k
            r0 = pl.multiple_of(c * CHUNK, CHUNK)
            xbc = xbc_ref[pl.ds(r0, CHUNK), :]
            dtc = dts_ref[pl.ds(r0, CHUNK), :]
            dyp = dyp_ref[pl.ds(r0, CHUNK), :]
            acs = _cumsum_rows(dtc * a_neg)
            acs_t = acs.T
            acs_x = _dot_exact(acs, e16)
            dt_x = _dot_exact(dtc, e16)
            xs = xbc[:, 0:1024]
            xd = xs * dt_x
            xd_bf = xd.astype(BF)
            dec_x = jnp.exp(acs_x[CHUNK - 1:CHUNK, :] - acs_x)
            xdd = xd * dec_x
            xdd_bf = xdd.astype(BF)
            dz = dyp * jnp.exp(acs_x)
            dz_bf = dz.astype(BF)
            ddx_acc = ddx_acc + jnp.sum(dyp * xs, axis=0, keepdims=True)
            dacs = jnp.zeros((CHUNK, LANES), F32)
            hsum = jnp.zeros((1, LANES), F32)
            p1_l, p2_l, p3_l = [], [], []
            for j in range(SSD_GROUPS):
                sl = slice(256 * j, 256 * (j + 1))
                bj = xbc[:, 1024 + 128 * j:1024 + 128 * (j + 1)].astype(BF)
                cj = xbc[:, 1536 + 128 * j:1536 + 128 * (j + 1)].astype(BF)
                g = _dot_nt(cj, bj)
                hj = st_ref[c, sl, :]
                hj_bf = hj.astype(BF)
                dhj = dh_ref[sl, :]
                dhj_bf = dhj.astype(BF)
                zj = _dot_nt(cj, hj_bf)
                qj = _dot_nt(bj, dhj_bf)
                lm = _group_decay(acs, acs_t, j, causal4)
                sc = jnp.concatenate([g] * 4, axis=0) * lm
                sc_bf = sc.astype(BF)
                dym = jnp.where(same, jnp.concatenate([dyp[:, sl]] * 4, axis=0), 0.0).astype(BF)
                dsc = _dot_nt(dym, xd_bf[:, sl])
                dxd = _dot_tn(sc_bf, dym) + qj * dec_x[:, sl]
                m = dsc * sc
                dg_bf = _fold_heads(dsc * lm).astype(BF)
                rs = jnp.sum(m, axis=1, keepdims=True)
                e2 = dhj * hj
                for hh in range(4):
                    oh = _onehot_lane(4 * j + hh)
                    dacs = dacs + oh * rs[CHUNK * hh:CHUNK * (hh + 1)]
                    hsum = hsum + oh * jnp.sum(jnp.sum(e2[64 * hh:64 * (hh + 1)], axis=0, keepdims=True), axis=1, keepdims=True)
                sel = (sel_rows + 4 * j == sel_lanes).astype(BF)
                hi = m.astype(BF)
                rem = m - hi.astype(F32)
                mid = rem.astype(BF)
                lo = (rem - mid.astype(F32)).astype(BF)
                dacs = dacs - (_dot_tn(hi, sel) + _dot_tn(mid, sel) + _dot_tn(lo, sel))
                p1_l.append(dz[:, sl] * zj)
                p2_l.append(qj * xdd[:, sl])
                p3_l.append(dxd * xs[:, sl])
                dxs_ref[pl.ds(r0, CHUNK), sl] = dd_x[:, sl] * dyp[:, sl] + dxd * dt_x[:, sl]
                dxs_ref[pl.ds(r0, CHUNK), 1536 + 128 * j:1536 + 128 * (j + 1)] = _dot(dg_bf, bj) + _dot(dz_bf[:, sl], hj_bf)
                dxs_ref[pl.ds(r0, CHUNK), 1024 + 128 * j:1024 + 128 * (j + 1)] = _dot_tn(dg_bf, cj) + _dot(xdd_bf[:, sl], dhj_bf)
                dh_ref[sl, :] = _group_last_decay(acs_t, j) * dhj + _dot_tn(dz_bf[:, sl], cj)
            stacked = jnp.concatenate([jnp.concatenate(p1_l, axis=1), jnp.concatenate(p2_l, axis=1), jnp.concatenate(p3_l, axis=1)], axis=0)
            red = _dot_exact(stacked, e16t)
            r1, r2, ddtc = red[0:CHUNK], red[CHUNK:2 * CHUNK], red[2 * CHUNK:3 * CHUNK]
            tot = jnp.sum(r2, axis=0, keepdims=True) + jnp.exp(acs[CHUNK - 1:CHUNK, :]) * hsum
            da = _rev_cumsum_rows(dacs + r1 - r2 + last_row * tot)
            ddts_ref[pl.ds(r0, CHUNK), :] = ddtc + da * a_neg
            dal_acc = dal_acc + jnp.sum(da * dtc, axis=0, keepdims=True)
            return dal_acc, ddx_acc

        dal_acc, ddx_acc = lax.fori_loop(0, ncl, chunk, (jnp.zeros((1, LANES), F32), jnp.zeros((1, D_MODEL), F32)))
        dal_ref[...] += dal_acc * a_neg
        ddd_ref[...] += _dot_exact(jnp.broadcast_to(ddx_acc, (8, D_MODEL)), e16t)[0:1, :]
        ddt_raw = ddts_ref[...] * _sigmoid(dt_ref[...] + dtb_ref[...])
        ddt_ref[...] = ddt_raw
        ddtb_ref[...] += jnp.sum(ddt_raw, axis=0, keepdims=True)

        pre = pre_ref[...]
        sp = _sigmoid(pre)
        dpre = dxs_ref[...] * sp * (1.0 + pre * (1.0 - sp))
        dp_ref[0:tl, :] = dpre
        dcb_ref[...] += jnp.sum(dpre, axis=0, keepdims=True)
        dx = jnp.zeros((tl, 2048), F32)
        for k in range(4):
            dcw_ref[k:k + 1, :] += jnp.sum(dpre * xp_ref[5 + k:5 + k + tl, :], axis=0, keepdims=True)
            dx = dx + cw_ref[k:k + 1, :] * dp_ref[3 - k:3 - k + tl, :]
        dxbc_ref[...] = dx.astype(BF)
        dp_ref[tl:tl + 8, :] = dp_ref[0:8, :]

    vec = lambda w: jax.ShapeDtypeStruct((1, w), F32)
    rrow = lambda w: pl.BlockSpec((tl, w), lambda i: (t_of(i), 0))
    return pl.pallas_call(
        body, name="ssd_bwd", grid=(nt,),
        in_specs=[rrow(D_MODEL), rrow(D_MODEL), rows(1024, 0), rows(1024, 1), rows(1024, 2), halo(1), halo(2), rows(LANES, 0),
                  pl.BlockSpec((ncl, 1024, SSD_STATE), lambda i: (t_of(i), 0, 0)),
                  _const_spec((4, 2048)), _const_spec((1, 2048)), _const_spec((1, LANES)), _const_spec((1, LANES)),
                  _const_spec((1, LANES)), _const_spec((1, D_MODEL))],
        out_specs=[rrow(2048), rrow(LANES), rrow(D_MODEL), _const_spec((8, 2048)), _const_spec((1, 2048)),
                   _const_spec((1, LANES)), _const_spec((1, LANES)), _const_spec((1, LANES)), _const_spec((1, D_MODEL))],
        out_shape=[jax.ShapeDtypeStruct((L, 2048), BF), jax.ShapeDtypeStruct((L, LANES), F32), jax.ShapeDtypeStruct((L, D_MODEL), BF),
                   jax.ShapeDtypeStruct((8, 2048), F32), vec(2048), vec(LANES), vec(LANES), vec(LANES), vec(D_MODEL)],
        scratch_shapes=[pltpu.VMEM((tl + 8, 2048), F32), pltpu.VMEM((tl, 2048), F32), pltpu.VMEM((tl, 2048), F32),
                        pltpu.VMEM((tl, LANES), F32), pltpu.VMEM((tl, D_MODEL), F32), pltpu.VMEM((tl, 2048), F32),
                        pltpu.VMEM((tl, LANES), F32), pltpu.VMEM((tl + 8, 2048), F32), pltpu.VMEM((1024, SSD_STATE), F32)],
        compiler_params=_params(("arbitrary",)),
    )(dyssd, ypre, proj, proj, proj, proj, proj, pdt, states, conv_w, conv_b, dt_bias, a_log, ssd_d, norm_w)


def _head_fwd_bwd(x, ys5, yssd, p, target, w_out, w_gate, w_proj, ple_nw, fin_nw):
    L = x.shape[0]
    tl = min(TOKEN_TILE, L)
    inv_d = 1.0 / D_MODEL

    def body(x_ref, ys_ref, yd_ref, p_ref, t_ref, wo_ref, wg_ref, wp_ref, pnw_ref, fnw_ref,
             loss_ref, dys_ref, dyd_ref, dh1_ref, n2_ref, dgl_ref, dpp_ref, dpnw_ref, dfnw_ref):
        @pl.when(pl.program_id(0) == 0)
        def _():
            loss_ref[...] = jnp.zeros_like(loss_ref)
            dpnw_ref[...] = jnp.zeros_like(dpnw_ref)
            dfnw_ref[...] = jnp.zeros_like(dfnw_ref)

        h1 = x_ref[...] + _dot(ys_ref[...], wo_ref[0:1024, :]) + _dot(yd_ref[...], wo_ref[1024:2048, :])
        r1 = lax.rsqrt(jnp.mean(h1 * h1, axis=-1, keepdims=True) + EPS)
        hh1 = h1 * r1
        n2 = (hh1 * pnw_ref[...]).astype(BF)
        gate = _sigmoid(_dot(n2, wg_ref[...]))
        pp = _dot(p_ref[...].astype(BF), wp_ref[...])
        h2 = h1 + pp * gate
        r2 = lax.rsqrt(jnp.mean(h2 * h2, axis=-1, keepdims=True) + EPS)
        hh2 = h2 * r2
        err = hh2 * fnw_ref[...] - t_ref[...]
        loss_ref[...] += 0.5 * inv_d * jnp.sum(err * err)
        dyo = err * inv_d
        dfnw_ref[...] += jnp.sum(dyo * hh2, axis=0, keepdims=True)
        g2 = dyo * fnw_ref[...]
        dh2 = r2 * (g2 - hh2 * jnp.mean(g2 * hh2, axis=-1, keepdims=True))
        dpp_ref[...] = (dh2 * gate).astype(BF)
        dgl = (dh2 * pp * gate * (1.0 - gate)).astype(BF)
        dgl_ref[...] = dgl
        n2_ref[...] = n2
        dn2 = _dot_nt(dgl, wg_ref[...])
        dpnw_ref[...] += jnp.sum(dn2 * hh1, axis=0, keepdims=True)
        g1 = dn2 * pnw_ref[...]
        dh1 = dh2 + r1 * (g1 - hh1 * jnp.mean(g1 * hh1, axis=-1, keepdims=True))
        dh1_ref[...] = dh1
        dh1_bf = dh1.astype(BF)
        dys_ref[...] = _dot_nt(dh1_bf, wo_ref[0:1024, :]).astype(BF)
        dyd_ref[...] = _dot_nt(dh1_bf, wo_ref[1024:2048, :])

    big = jax.ShapeDtypeStruct((L, D_MODEL), BF)
    vec = jax.ShapeDtypeStruct((1, D_MODEL), F32)
    return pl.pallas_call(
        body, name="head_fwd_bwd", grid=(L // tl,),
        in_specs=[_row_spec(tl, D_MODEL), _row_spec(tl, D_MODEL), _row_spec(tl, D_MODEL), _row_spec(tl, 256), _row_spec(tl, D_MODEL),
                  _const_spec((2048, D_MODEL)), _const_spec((D_MODEL, D_MODEL)), _const_spec((256, D_MODEL)),
                  _const_spec((1, D_MODEL)), _const_spec((1, D_MODEL))],
        out_specs=[_const_spec((8, LANES)), _row_spec(tl, D_MODEL), _row_spec(tl, D_MODEL), _row_spec(tl, D_MODEL),
                   _row_spec(tl, D_MODEL), _row_spec(tl, D_MODEL), _row_spec(tl, D_MODEL), _const_spec((1, D_MODEL)), _const_spec((1, D_MODEL))],
        out_shape=[jax.ShapeDtypeStruct((8, LANES), F32), big, jax.ShapeDtypeStruct((L, D_MODEL), F32),
                   jax.ShapeDtypeStruct((L, D_MODEL), F32), big, big, big, vec, vec],
        compiler_params=_params(("arbitrary",)),
    )(x, ys5, yssd, p, target, w_out, w_gate, w_proj, ple_nw, fin_nw)


def _pad_lanes(v):
    return jnp.pad(v.reshape(1, -1), ((0, 0), (0, LANES - v.size)))


def _local_step(x, p, target, w):
    L = x.shape[0]
    nc = L // CHUNK
    nsteps = max(1, (nc - 1).bit_length())
    w_in = w["w_in"]
    w_main = w_in[:, :D_MAIN]
    w_dt = jnp.pad(w_in[:, D_MAIN:], ((0, 0), (0, LANES - SSD_HEADS)))
    norm_w = w["norm_w"].reshape(1, -1)
    s5_d = w["s5_D"].reshape(1, -1)
    b_glu = w["s5_b_glu"].reshape(1, -1)
    conv_b = w["conv_b"].reshape(1, -1)
    dtb, alog, ssd_d = _pad_lanes(w["dt_bias"]), _pad_lanes(w["A_log"]), _pad_lanes(w["ssd_D"])
    ssd_nw = w["ssd_norm_w"].reshape(1, -1)
    ple_nw = w["ple_norm_w"].reshape(1, -1)
    fin_nw = w["final_norm_w"].reshape(1, -1)

    s5_args = (w["s5_A_re"], w["s5_A_im"], w["s5_log_dt"], w["s5_B_re"], w["s5_B_im"], w["s5_C_re"], w["s5_C_im"])
    small, small_vjp = jax.vjp(_s5_discretise, *s5_args)
    kmat, kmat_vjp = jax.vjp(_s5_kmat, *small)
    bbp, pwr, cnp, pw1 = _s5_table_factors(*small)
    p1, p2 = _s5_scan_powers(w["s5_A_re"], w["s5_A_im"], w["s5_log_dt"], nsteps)

    hn, proj, pssd, pdt = _in_proj_fwd(x, norm_w, w_main, w_dt)
    uflat = _flat_hs(proj[:, :D_MODEL], nc)
    yflat, hsave = _s5_core_fwd(uflat, kmat, bbp, pwr, cnp, pw1, p1, p2)
    yssm = _unflat_tk(yflat, nc)
    ys5 = _s5_post_fwd(yssm, proj, s5_d, w["s5_w_glu"], b_glu)
    yssd, ypre, states = _ssd_fwd(pssd, pdt, w["conv_w"], conv_b, dtb, alog, ssd_d, ssd_nw)
    (loss8, dys5, dyssd, dh1, n2, dgl2, dpp, g_ple_nw, g_fin_nw) = _head_fwd_bwd(
        x, ys5, yssd, p, target, w["w_out"], w["w_ple_gate"], w["w_ple_proj"], ple_nw, fin_nw)

    (dxbc, ddt, dzd, g_cw, g_cb, g_dtb, g_alog, g_ssd_d, g_ssd_nw) = _ssd_bwd(
        dyssd, ypre, pssd, pdt, states, w["conv_w"], conv_b, dtb, alog, ssd_d, ssd_nw)
    dzs, dyssm, a_glu, dgl1, g_bglu, g_s5d = _s5_post_bwd(dys5, yssm, proj, s5_d, w["s5_w_glu"], b_glu)
    duflat, dkmat, r12, q12, xy, uv, da8 = _s5_core_bwd(uflat, _flat_tk(dyssm, nc), hsave, kmat, bbp, pwr, cnp, pw1, p1, p2)
    da64 = jnp.concatenate([da8[:, 0, :S5_STATE] + da8[:, 0, S5_STATE:], da8[:, 1, S5_STATE:] - da8[:, 1, :S5_STATE]], axis=-1)
    d_small = [a + b for a, b in zip(_s5_small_cotangents(r12, q12, xy, uv, da64), kmat_vjp(dkmat))]
    g_s5 = small_vjp(tuple(d_small))
    gx, du, g_norm_w = _in_proj_bwd(x, norm_w, dh1, _unflat_hs(duflat, nc), dyssm, s5_d, dzs, dzd, dxbc, ddt, w_main, w_dt)

    g_w_in = jnp.concatenate([
        _matmul_tn(hn, du, "dw_in_u"), _matmul_tn(hn, dzs, "dw_in_zs"), _matmul_tn(hn, dzd, "dw_in_zd"),
        _matmul_tn(hn, dxbc, "dw_in_xbc"), _matmul_tn(hn, ddt, "dw_in_dt")[:, :SSD_HEADS]], axis=1)
    grads = {
        "norm_w": g_norm_w, "w_in": g_w_in,
        "s5_A_re": g_s5[0], "s5_A_im": g_s5[1], "s5_log_dt": g_s5[2], "s5_B_re": g_s5[3], "s5_B_im": g_s5[4],
        "s5_C_re": g_s5[5], "s5_C_im": g_s5[6], "s5_D": g_s5d, "s5_w_glu": _matmul_tn(a_glu, dgl1, "dw_glu"), "s5_b_glu": g_bglu,
        "conv_w": g_cw[:4], "conv_b": g_cb, "dt_bias": g_dtb[:, :SSD_HEADS], "A_log": g_alog[:, :SSD_HEADS],
        "ssd_D": g_ssd_d[:, :SSD_HEADS], "ssd_norm_w": g_ssd_nw,
        "w_out": jnp.concatenate([_matmul_tn(ys5, dh1, "dw_out_s5"), _matmul_tn(yssd, dh1, "dw_out_ssd")], axis=0),
        "ple_norm_w": g_ple_nw, "w_ple_gate": _matmul_tn(n2, dgl2, "dw_gate"), "w_ple_proj": _matmul_tn(p, dpp, "dw_proj"),
        "final_norm_w": g_fin_nw,
    }
    return loss8[0, 0], gx, grads


WEIGHTS = ("norm_w", "w_in", "s5_A_re", "s5_A_im", "s5_log_dt", "s5_B_re", "s5_B_im", "s5_C_re", "s5_C_im", "s5_D", "s5_w_glu",
           "s5_b_glu", "conv_w", "conv_b", "dt_bias", "A_log", "ssd_D", "ssd_norm_w", "w_out", "ple_norm_w", "w_ple_gate",
           "w_ple_proj", "final_norm_w")
BIG = {"w_in": ((1024, 1284), 1), "s5_w_glu": ((256, 1024), 0), "w_out": ((512, 1024), 0), "w_ple_gate": ((256, 1024), 0),
       "w_ple_proj": ((256, 256), 1)}
SMALL = {"norm_w": (1024,), "s5_A_re": (64, 64), "s5_A_im": (64, 64), "s5_log_dt": (64,), "s5_B_re": (64, 64, 16),
         "s5_B_im": (64, 64, 16), "s5_C_re": (64, 16, 64), "s5_C_im": (64, 16, 64), "s5_D": (1024,), "s5_b_glu": (1024,),
         "conv_w": (4, 2048), "conv_b": (2048,), "dt_bias": (16,), "A_log": (16,), "ssd_D": (16,), "ssd_norm_w": (1024,),
         "ple_norm_w": (1024,), "final_norm_w": (1024,)}
BIG_ROWS = {n: s[0] * s[1] // LANES for n, (s, _) in BIG.items()}
BIG_ROWS_TOTAL = sum(BIG_ROWS.values())
SMALL_TOTAL = sum(math.prod(s) for s in SMALL.values())
SMALL_PIECE_ROWS = -(-SMALL_TOTAL // (N_CHIPS * 16 * LANES)) * 16
HALF_ROWS = (BIG_ROWS_TOTAL + SMALL_PIECE_ROWS) // 2
SMALL_ROW0 = BIG_ROWS_TOTAL - HALF_ROWS


def _mesh_pos():
    return lax.axis_index("x"), lax.axis_index("y"), lax.axis_index("c")


def _other_chips(x, y):
    return [(1 - x, y), (x, 1 - y), (1 - x, 1 - y)]


def _comm_params():
    return pltpu.CompilerParams(has_side_effects=True)


def _all_gather_chips(wpack, cw):
    half = wpack.shape[0] // 2

    def body(w_ref, c_ref, wo_ref, co_ref, send_sems, recv_sems, fwd_send, fwd_recv, loc_sems):
        x, y, c = _mesh_pos()
        me = 2 * x + y
        sib = (x, y, 1 - c)
        mine = pl.ds(c * half, half)
        theirs = pl.ds((1 - c) * half, half)
        others = _other_chips(x, y)
        loc = [pltpu.make_async_copy(c_ref, co_ref.at[me], loc_sems.at[0])]
        for cp in loc:
            cp.start()

        def from_chip(k, chip, dev):
            return pltpu.make_async_remote_copy(w_ref.at[mine], wo_ref.at[chip, mine], send_sems.at[2 * k], recv_sems.at[2 * k],
                                                device_id=dev, device_id_type=MESH)

        def conv_from(k, chip, dev):
            return pltpu.make_async_remote_copy(c_ref, co_ref.at[chip], send_sems.at[2 * k + 1], recv_sems.at[2 * k + 1],
                                                device_id=dev, device_id_type=MESH)

        def passed(k, chip, rows):
            return pltpu.make_async_remote_copy(wo_ref.at[chip, rows], wo_ref.at[chip, rows], fwd_send.at[k], fwd_recv.at[k],
                                                device_id=sib, device_id_type=MESH)

        sends = []
        for k, (px, py) in enumerate(others):
            sends += [from_chip(k, me, (px, py, c)), conv_from(k, me, (px, py, c))]
        for cp in sends:
            cp.start()
        fwds = []
        for k, (px, py) in enumerate(others):
            chip = 2 * px + py
            from_chip(k, chip, (px, py, c)).wait_recv()
            fwds.append(passed(k, chip, mine))
            fwds[-1].start()
        for k, (px, py) in enumerate(others):
            chip = 2 * px + py
            passed(k, chip, theirs).wait_recv()
            conv_from(k, chip, (px, py, c)).wait_recv()
        for cp in sends + fwds:
            cp.wait_send()
        for cp in loc:
            cp.wait()

    return pl.pallas_call(
        body, name="all_gather_weights", in_specs=[ANY, ANY], out_specs=[ANY, ANY],
        out_shape=[jax.ShapeDtypeStruct((N_CHIPS,) + wpack.shape, wpack.dtype), jax.ShapeDtypeStruct((N_CHIPS,) + cw.shape, cw.dtype)],
        scratch_shapes=[pltpu.SemaphoreType.DMA((6,)), pltpu.SemaphoreType.DMA((6,)), pltpu.SemaphoreType.DMA((3,)),
                        pltpu.SemaphoreType.DMA((3,)), pltpu.SemaphoreType.DMA((1,))],
        compiler_params=_comm_params(),
    )(wpack, cw)


def _exchange_pair(gp):
    def body(g_ref, r_ref, send_sems, recv_sems):
        x, y, c = _mesh_pos()
        cps = [pltpu.make_async_remote_copy(g_ref.at[s, 1 - c], r_ref.at[s], send_sems.at[s], recv_sems.at[s],
                                            device_id=(x, y, 1 - c), device_id_type=MESH) for s in range(N_CHIPS)]
        for cp in cps:
            cp.start()
        for cp in cps:
            cp.wait()

    return pl.pallas_call(
        body, name="grad_exchange_pair", in_specs=[ANY], out_specs=ANY,
        out_shape=jax.ShapeDtypeStruct((N_CHIPS,) + gp.shape[2:], gp.dtype),
        scratch_shapes=[pltpu.SemaphoreType.DMA((N_CHIPS,)), pltpu.SemaphoreType.DMA((N_CHIPS,))],
        compiler_params=_comm_params(),
    )(gp)


def _pair_sum(mine, from_sibling):
    def body(a_ref, b_ref, bf_ref, tail_ref):
        s = a_ref[0] + b_ref[0]
        bf_ref[0] = s.astype(BF)
        tail_ref[0] = s[SMALL_ROW0:, :]

    piece = pl.BlockSpec((1, HALF_ROWS, LANES), lambda i: (i, 0, 0))
    return pl.pallas_call(
        body, name="grad_pair_sum", grid=(N_CHIPS,), in_specs=[piece, piece],
        out_specs=[piece, pl.BlockSpec((1, SMALL_PIECE_ROWS, LANES), lambda i: (i, 0, 0))],
        out_shape=[jax.ShapeDtypeStruct((N_CHIPS, HALF_ROWS, LANES), BF), jax.ShapeDtypeStruct((N_CHIPS, SMALL_PIECE_ROWS, LANES), F32)],
        compiler_params=_params(("parallel",)),
    )(mine, from_sibling)


def _exchange_chips(ps_bf, ps_tail):
    def body(p_ref, t_ref, r_ref, rt_ref, send_sems, recv_sems):
        x, y, c = _mesh_pos()
        cps = []
        for k, (px, py) in enumerate(_other_chips(x, y)):
            cps.append(pltpu.make_async_remote_copy(p_ref.at[2 * px + py], r_ref.at[k], send_sems.at[2 * k], recv_sems.at[2 * k],
                                                    device_id=(px, py, c), device_id_type=MESH))
            cps.append(pltpu.make_async_remote_copy(t_ref.at[2 * px + py], rt_ref.at[k], send_sems.at[2 * k + 1],
                                                    recv_sems.at[2 * k + 1], device_id=(px, py, c), device_id_type=MESH))
        for cp in cps:
            cp.start()
        for cp in cps:
            cp.wait()

    return pl.pallas_call(
        body, name="grad_exchange_chips", in_specs=[ANY, ANY], out_specs=[ANY, ANY],
        out_shape=[jax.ShapeDtypeStruct((N_CHIPS - 1,) + ps_bf.shape[1:], ps_bf.dtype),
                   jax.ShapeDtypeStruct((N_CHIPS - 1,) + ps_tail.shape[1:], ps_tail.dtype)],
        scratch_shapes=[pltpu.SemaphoreType.DMA((6,)), pltpu.SemaphoreType.DMA((6,))],
        compiler_params=_comm_params(),
    )(ps_bf, ps_tail)


def _chip_sum(own_bf, own_tail, others_bf, others_tail):
    def body(ob_ref, ot_ref, b_ref, t_ref, o_ref):
        acc = ob_ref[0:SMALL_ROW0, :].astype(F32)
        tail = ot_ref[...]
        for k in range(N_CHIPS - 1):
            acc = acc + b_ref[k, 0:SMALL_ROW0, :].astype(F32)
            tail = tail + t_ref[k]
        o_ref[0:SMALL_ROW0, :] = acc
        o_ref[SMALL_ROW0:, :] = tail

    return pl.pallas_call(
        body, name="grad_chip_sum", out_shape=jax.ShapeDtypeStruct((HALF_ROWS, LANES), F32),
        compiler_params=_params(),
    )(own_bf, own_tail, others_bf, others_tail)


def _swap_reduced_halves(gh):
    def body(g_ref, o_ref, send_sem, recv_sem):
        x, y, c = _mesh_pos()
        cp = pltpu.make_async_remote_copy(g_ref, o_ref, send_sem, recv_sem, device_id=(x, y, 1 - c), device_id_type=MESH)
        cp.start()
        cp.wait()

    return pl.pallas_call(
        body, name="grad_swap_halves", in_specs=[ANY], out_specs=ANY,
        out_shape=jax.ShapeDtypeStruct(gh.shape, gh.dtype),
        scratch_shapes=[pltpu.SemaphoreType.DMA, pltpu.SemaphoreType.DMA],
        compiler_params=_comm_params(),
    )(gh)


def _gather_small(second_half):
    def body(gs_ref, sm_ref, send_sems, recv_sems, loc_sem):
        x, y, c = _mesh_pos()
        me = 2 * x + y
        small = gs_ref.at[pl.ds(SMALL_ROW0, SMALL_PIECE_ROWS)]
        loc = pltpu.make_async_copy(small, sm_ref.at[me], loc_sem)
        loc.start()
        cps = [pltpu.make_async_remote_copy(small, sm_ref.at[me], send_sems.at[k], recv_sems.at[k],
                                            device_id=(px, py, c), device_id_type=MESH)
               for k, (px, py) in enumerate(_other_chips(x, y))]
        for cp in cps:
            cp.start()
        for cp in cps:
            cp.wait()
        loc.wait()

    return pl.pallas_call(
        body, name="grad_gather_small", in_specs=[ANY], out_specs=ANY,
        out_shape=jax.ShapeDtypeStruct((N_CHIPS, SMALL_PIECE_ROWS, LANES), second_half.dtype),
        scratch_shapes=[pltpu.SemaphoreType.DMA((3,)), pltpu.SemaphoreType.DMA((3,)), pltpu.SemaphoreType.DMA],
        compiler_params=_comm_params(),
    )(second_half)


def _pack_grads(grads):
    small = jnp.concatenate([grads[n].reshape(-1) for n in SMALL])
    small = jnp.pad(small, (0, N_CHIPS * SMALL_PIECE_ROWS * LANES - SMALL_TOTAL)).reshape(N_CHIPS, SMALL_PIECE_ROWS, LANES)
    pieces = []
    for s in range(N_CHIPS):
        rows = []
        for n, (shp, axis) in BIG.items():
            g = grads[n]
            blk = g[s * shp[0]:(s + 1) * shp[0], :] if axis == 0 else g[:, s * shp[1]:(s + 1) * shp[1]]
            rows.append(blk.reshape(-1, LANES))
        rows.append(small[s])
        pieces.append(jnp.concatenate(rows, axis=0).reshape(2, HALF_ROWS, LANES))
    return jnp.stack(pieces)


def _unpack_shard(first_half, second_half):
    rows = jnp.concatenate([first_half, second_half], axis=0)
    out, r0 = {}, 0
    for n, (shp, _) in BIG.items():
        out[n] = rows[r0:r0 + BIG_ROWS[n]].reshape(shp)
        r0 += BIG_ROWS[n]
    return out


def _unpack_small(sm):
    flat = sm.reshape(-1)
    out, o = {}, 0
    for n, shp in SMALL.items():
        k = math.prod(shp)
        out[n] = flat[o:o + k].reshape(shp)
        o += k
    return out


def _as_2d(a):
    n = a.size
    if a.ndim >= 2 and a.shape[-1] > 1024:
        return a.reshape(-1, a.shape[-1])
    if n % 1024 == 0:
        return a.reshape(n // 1024, 1024)
    return a.reshape(1, n)


def _adamw(w, g, m, v, name):
    shape = w.shape
    w2, g2, m2, v2 = (_as_2d(a) for a in (w, g, m, v))
    rows, cols = w2.shape
    rb = 256 if rows >= 512 else rows

    def body(w_ref, g_ref, m_ref, v_ref, d_ref, mo_ref, vo_ref):
        gv = g_ref[...]
        mn = ADAM_B1 * m_ref[...] + (1.0 - ADAM_B1) * gv
        vn = ADAM_B2 * v_ref[...] + (1.0 - ADAM_B2) * (gv * gv)
        m_hat = mn / (1.0 - ADAM_B1 ** ADAM_STEP)
        v_hat = vn / (1.0 - ADAM_B2 ** ADAM_STEP)
        d_ref[...] = -ADAM_LR * (m_hat / (jnp.sqrt(v_hat) + ADAM_EPS) + ADAM_WD * w_ref[...])
        mo_ref[...] = mn
        vo_ref[...] = vn

    spec = _row_spec(rb, cols)
    sds = jax.ShapeDtypeStruct((rows, cols), F32)
    d, mo, vo = pl.pallas_call(
        body, name=name, grid=(rows // rb,), in_specs=[spec] * 4, out_specs=[spec] * 3, out_shape=[sds] * 3,
        compiler_params=_params(("parallel",)),
    )(w2, g2, m2, v2)
    return d.reshape(shape), mo.reshape(shape), vo.reshape(shape)


def kernel(x, p, norm_w, w_in, s5_A_re, s5_A_im, s5_log_dt, s5_B_re, s5_B_im, s5_C_re, s5_C_im, s5_D, s5_w_glu, s5_b_glu, conv_w, conv_b, dt_bias, A_log, ssd_D, ssd_norm_w, w_out, ple_norm_w, w_ple_gate, w_ple_proj, final_norm_w, loss_target, m_norm_w, m_w_in, m_s5_A_re, m_s5_A_im, m_s5_log_dt, m_s5_B_re, m_s5_B_im, m_s5_C_re, m_s5_C_im, m_s5_D, m_s5_w_glu, m_s5_b_glu, m_conv_w, m_conv_b, m_dt_bias, m_A_log, m_ssd_D, m_ssd_norm_w, m_w_out, m_ple_norm_w, m_w_ple_gate, m_w_ple_proj, m_final_norm_w, v_norm_w, v_w_in, v_s5_A_re, v_s5_A_im, v_s5_log_dt, v_s5_B_re, v_s5_B_im, v_s5_C_re, v_s5_C_im, v_s5_D, v_s5_w_glu, v_s5_b_glu, v_conv_w, v_conv_b, v_dt_bias, v_A_log, v_ssd_D, v_ssd_norm_w, v_w_out, v_ple_norm_w, v_w_ple_gate, v_w_ple_proj, v_final_norm_w):
    given = (norm_w, w_in, s5_A_re, s5_A_im, s5_log_dt, s5_B_re, s5_B_im, s5_C_re, s5_C_im, s5_D, s5_w_glu, s5_b_glu, conv_w, conv_b,
             dt_bias, A_log, ssd_D, ssd_norm_w, w_out, ple_norm_w, w_ple_gate, w_ple_proj, final_norm_w)
    given_m = (m_norm_w, m_w_in, m_s5_A_re, m_s5_A_im, m_s5_log_dt, m_s5_B_re, m_s5_B_im, m_s5_C_re, m_s5_C_im, m_s5_D, m_s5_w_glu,
               m_s5_b_glu, m_conv_w, m_conv_b, m_dt_bias, m_A_log, m_ssd_D, m_ssd_norm_w, m_w_out, m_ple_norm_w, m_w_ple_gate,
               m_w_ple_proj, m_final_norm_w)
    given_v = (v_norm_w, v_w_in, v_s5_A_re, v_s5_A_im, v_s5_log_dt, v_s5_B_re, v_s5_B_im, v_s5_C_re, v_s5_C_im, v_s5_D, v_s5_w_glu,
               v_s5_b_glu, v_conv_w, v_conv_b, v_dt_bias, v_A_log, v_ssd_D, v_ssd_norm_w, v_w_out, v_ple_norm_w, v_w_ple_gate,
               v_w_ple_proj, v_final_norm_w)
    wts, mom, var = dict(zip(WEIGHTS, given)), dict(zip(WEIGHTS, given_m)), dict(zip(WEIGHTS, given_v))
    drop = lambda n, a: a if n == "final_norm_w" else a[0]

    wpack = jnp.concatenate([drop(n, wts[n]).astype(BF).reshape(-1, LANES) for n in BIG], axis=0)
    wall, cwall = _all_gather_chips(wpack, drop("conv_w", wts["conv_w"]))
    chip = 2 * lax.axis_index("x") + lax.axis_index("y")
    is_own = (lax.broadcasted_iota(jnp.int32, (N_CHIPS, 1, 1), 0) == chip)
    full, r0 = {}, 0
    for n, (shp, axis) in BIG.items():
        blk = jnp.where(is_own, drop(n, wts[n]).astype(BF)[None], wall[:, r0:r0 + BIG_ROWS[n]].reshape((N_CHIPS,) + shp))
        full[n] = blk.reshape(N_CHIPS * shp[0], shp[1]) if axis == 0 else blk.transpose(1, 0, 2).reshape(shp[0], N_CHIPS * shp[1])
        r0 += BIG_ROWS[n]
    for n in SMALL:
        full[n] = drop(n, wts[n])
    full["conv_w"] = cwall.transpose(1, 0, 2).reshape(4, 2048)

    loss, gx, grads = _local_step(x[0], p[0, 0], loss_target[0], full)
    loss = lax.psum(loss, MESH_AXES)

    gp = _pack_grads({n: grads[n].reshape(SMALL[n]) if n in SMALL else grads[n] for n in WEIGHTS})
    c = lax.axis_index("c")
    from_sibling = _exchange_pair(gp)
    mine = lax.dynamic_index_in_dim(gp, c, axis=1, keepdims=False)
    ps_bf, ps_tail = _pair_sum(mine, from_sibling)
    others_bf, others_tail = _exchange_chips(ps_bf, ps_tail)
    own_bf = lax.dynamic_index_in_dim(ps_bf, chip, axis=0, keepdims=False)
    own_tail = lax.dynamic_index_in_dim(ps_tail, chip, axis=0, keepdims=False)
    reduced_half = _chip_sum(own_bf, own_tail, others_bf, others_tail)
    sibling_half = _swap_reduced_halves(reduced_half)
    first_half = jnp.where(c == 0, reduced_half, sibling_half)
    second_half = jnp.where(c == 0, sibling_half, reduced_half)
    sm = _gather_small(second_half)
    g_final = {**_unpack_small(sm), **_unpack_shard(first_half, second_half)}
    g_final["conv_w"] = lax.dynamic_slice_in_dim(g_final["conv_w"], chip * 512, 512, axis=1)

    outs_g, outs_d, outs_m, outs_v = [], [], [], []
    for n in WEIGHTS:
        g = g_final[n].reshape(wts[n].shape)
        d, mo, vo = _adamw(wts[n], g, mom[n], var[n], "adamw_" + n)
        outs_g.append(g)
        outs_d.append(d)
        outs_m.append(mo)
        outs_v.append(vo)
    return (loss, gx[None], *outs_g, *outs_d, *outs_m, *outs_v)
```

```python
import functools
import math

import jax
import jax.numpy as jnp
from jax import lax
from jax.experimental import pallas as pl
from jax.experimental.pallas import tpu as pltpu

F32 = jnp.float32
BF = jnp.bfloat16
EPS = 1e-6
CHUNK = 64
D_MODEL = 1024
S5_GROUPS = 64
S5_CH = 16
S5_STATE = 64
SSD_HEADS = 16
SSD_HEAD_DIM = 64
SSD_GROUPS = 4
SSD_STATE = 128
D_MAIN = 5120
LANES = 128
TOKEN_TILE = 256
VMEM_LIMIT = 56 * 1024 * 1024
MESH_AXES = ("x", "y", "c")
N_CHIPS = 4
ADAM_LR, ADAM_B1, ADAM_B2, ADAM_EPS, ADAM_WD, ADAM_STEP = 0.001, 0.9, 0.999, 1e-08, 0.01, 10
MESH = pl.DeviceIdType.MESH
ANY = pl.BlockSpec(memory_space=pl.ANY)


def _dot(a, b):
    return jnp.dot(a, b, preferred_element_type=F32)


def _dot_nt(a, b):
    return lax.dot_general(a, b, (((1,), (1,)), ((), ())), preferred_element_type=F32)


def _dot_tn(a, b):
    return lax.dot_general(a, b, (((0,), (0,)), ((), ())), preferred_element_type=F32)


def _sigmoid(x):
    return 1.0 / (1.0 + jnp.exp(-x))


def _softplus(x):
    return jnp.maximum(x, 0.0) + jnp.log(1.0 + jnp.exp(-jnp.abs(x)))


_GELU_C = math.sqrt(2.0 / math.pi)


def _gelu(x):
    return 0.5 * x * (1.0 + jnp.tanh(_GELU_C * (x + 0.044715 * x * x * x)))


def _gelu_grad(x):
    th = jnp.tanh(_GELU_C * (x + 0.044715 * x * x * x))
    return 0.5 * (1.0 + th) + 0.5 * x * (1.0 - th * th) * _GELU_C * (1.0 + 3.0 * 0.044715 * x * x)


def _params(sem=None):
    return pltpu.CompilerParams(dimension_semantics=sem, vmem_limit_bytes=VMEM_LIMIT)


def _row_spec(tl, width, col=0):
    return pl.BlockSpec((tl, width), lambda i, col=col: (i, col))


def _const_spec(shape):
    nd = len(shape)
    return pl.BlockSpec(shape, lambda *_: (0,) * nd)


def _in_proj_fwd(x, norm_w, w_main, w_dt):
    L = x.shape[0]
    tl = min(TOKEN_TILE, L)

    def body(x_ref, nw_ref, wm_ref, wd_ref, hn_ref, ps5_ref, pssd_ref, pd_ref):
        xv = x_ref[...]
        r = lax.rsqrt(jnp.mean(xv * xv, axis=-1, keepdims=True) + EPS)
        hn = (xv * r * nw_ref[...]).astype(BF)
        hn_ref[...] = hn
        for j in range(2):
            ps5_ref[:, j * 1024:(j + 1) * 1024] = _dot_nt(hn, wm_ref[j * 1024:(j + 1) * 1024, :]).astype(BF)
        for j in range(3):
            pssd_ref[:, j * 1024:(j + 1) * 1024] = _dot_nt(hn, wm_ref[(j + 2) * 1024:(j + 3) * 1024, :])
        pd_ref[...] = _dot_nt(hn, wd_ref[...])

    return pl.pallas_call(
        body, name="in_proj_fwd", grid=(L // tl,),
        in_specs=[_row_spec(tl, D_MODEL), _const_spec((1, D_MODEL)), _const_spec((D_MAIN, D_MODEL)), _const_spec((LANES, D_MODEL))],
        out_specs=[_row_spec(tl, D_MODEL), _row_spec(tl, 2048), _row_spec(tl, 3072), _row_spec(tl, LANES)],
        out_shape=[jax.ShapeDtypeStruct((L, D_MODEL), BF), jax.ShapeDtypeStruct((L, 2048), BF), jax.ShapeDtypeStruct((L, 3072), F32),
                   jax.ShapeDtypeStruct((L, LANES), F32)],
        compiler_params=_params(("arbitrary",)),
    )(x, norm_w, w_main, w_dt)


def _in_proj_bwd(x, norm_w, dh1, du_flat, dyssm, s5_d, dzs, dzd, dxbc, ddt, w_main, w_dt):
    L = x.shape[0]
    tl = min(TOKEN_TILE, L)

    def body(x_ref, nw_ref, dh1_ref, duf_ref, dys_ref, d_ref, dzs_ref, dzd_ref, dxbc_ref, ddt_ref, wm_ref, wd_ref,
             gx_ref, du_ref, gnw_ref):
        @pl.when(pl.program_id(0) == 0)
        def _():
            gnw_ref[...] = jnp.zeros_like(gnw_ref)

        du = (duf_ref[...].astype(F32) + dys_ref[...].astype(F32) * d_ref[...]).astype(BF)
        du_ref[...] = du
        dhn = _dot(du, wm_ref[0:1024, :])
        dhn += _dot(dzs_ref[...], wm_ref[1024:2048, :])
        dhn += _dot(dzd_ref[...], wm_ref[2048:3072, :])
        dhn += _dot(dxbc_ref[...], wm_ref[3072:5120, :])
        dhn += _dot(ddt_ref[...].astype(BF), wd_ref[...])
        xv = x_ref[...]
        r = lax.rsqrt(jnp.mean(xv * xv, axis=-1, keepdims=True) + EPS)
        xh = xv * r
        gnw_ref[...] += jnp.sum(dhn * xh, axis=0, keepdims=True)
        g = dhn * nw_ref[...]
        gx_ref[...] = dh1_ref[...] + r * (g - xh * jnp.mean(g * xh, axis=-1, keepdims=True))

    return pl.pallas_call(
        body, name="in_proj_bwd", grid=(L // tl,),
        in_specs=[_row_spec(tl, D_MODEL), _const_spec((1, D_MODEL)), _row_spec(tl, D_MODEL), _row_spec(tl, D_MODEL),
                  _row_spec(tl, D_MODEL), _const_spec((1, D_MODEL)), _row_spec(tl, D_MODEL), _row_spec(tl, D_MODEL),
                  _row_spec(tl, 2048), _row_spec(tl, LANES), _const_spec((D_MAIN, D_MODEL)), _const_spec((LANES, D_MODEL))],
        out_specs=[_row_spec(tl, D_MODEL), _row_spec(tl, D_MODEL), _const_spec((1, D_MODEL))],
        out_shape=[jax.ShapeDtypeStruct((L, D_MODEL), F32), jax.ShapeDtypeStruct((L, D_MODEL), BF), jax.ShapeDtypeStruct((1, D_MODEL), F32)],
        compiler_params=_params(("arbitrary",)),
    )(x, norm_w, dh1, du_flat, dyssm, s5_d, dzs, dzd, dxbc, ddt, w_main, w_dt)


def _matmul_tn(a, b, name):
    L, M = a.shape
    N = b.shape[1]
    tm, tn, tk = min(M, 1024), min(N, 1024), min(L, 512)

    def body(a_ref, b_ref, o_ref):
        @pl.when(pl.program_id(2) == 0)
        def _():
            o_ref[...] = jnp.zeros_like(o_ref)

        o_ref[...] += _dot_tn(a_ref[...].astype(BF), b_ref[...].astype(BF))

    return pl.pallas_call(
        body, name=name, grid=(M // tm, N // tn, L // tk),
        in_specs=[pl.BlockSpec((tk, tm), lambda i, j, k: (k, i)), pl.BlockSpec((tk, tn), lambda i, j, k: (k, j))],
        out_specs=pl.BlockSpec((tm, tn), lambda i, j, k: (i, j)),
        out_shape=jax.ShapeDtypeStruct((M, N), F32),
        compiler_params=_params(("parallel", "parallel", "arbitrary")),
    )(a, b)


def _s5_discretise(a_re, a_im, log_dt, b_re, b_im, c_re, c_im):
    dt = jnp.exp(log_dt)[:, None]
    tau = jnp.arange(CHUNK + 1, dtype=F32)
    mag = jnp.exp((a_re * dt)[:, :, None] * tau)
    ang = (a_im * dt)[:, :, None] * tau
    pw_re, pw_im = mag * jnp.cos(ang), mag * jnp.sin(ang)
    er, ei = pw_re[:, :, 1] - 1.0, pw_im[:, :, 1]
    den = a_re * a_re + a_im * a_im
    beta_re, beta_im = (er * a_re + ei * a_im) / den, (ei * a_re - er * a_im) / den
    bb_re = (beta_re[:, :, None] * b_re - beta_im[:, :, None] * b_im).transpose(0, 2, 1)
    bb_im = (beta_re[:, :, None] * b_im + beta_im[:, :, None] * b_re).transpose(0, 2, 1)
    return bb_re, bb_im, c_re, c_im, pw_re, pw_im


def _s5_kmat(bb_re, bb_im, c_re, c_im, pw_re, pw_im):
    hi = lax.Precision.HIGHEST
    m_re = bb_re[:, :, None, :] * c_re[:, None, :, :] - bb_im[:, :, None, :] * c_im[:, None, :, :]
    m_im = bb_re[:, :, None, :] * c_im[:, None, :, :] + bb_im[:, :, None, :] * c_re[:, None, :, :]
    k4 = (jnp.einsum("ghkn,gnt->ghtk", m_re, pw_re[:, :, :CHUNK], precision=hi)
          - jnp.einsum("ghkn,gnt->ghtk", m_im, pw_im[:, :, :CHUNK], precision=hi))
    return k4.reshape(S5_GROUPS, S5_CH, CHUNK * S5_CH)


def _s5_table_factors(bb_re, bb_im, c_re, c_im, pw_re, pw_im):
    pr = pw_re[:, :, CHUNK - 1::-1].transpose(0, 2, 1)
    pi = pw_im[:, :, CHUNK - 1::-1].transpose(0, 2, 1)
    bbp = jnp.concatenate([bb_re, bb_im], axis=-1)
    pwr = jnp.concatenate([pr, pi], axis=-1)
    cnp = jnp.concatenate([c_re, c_im], axis=-1)
    pw1 = jnp.concatenate([pw_re[:, :, 1:CHUNK + 1].transpose(0, 2, 1), pw_im[:, :, 1:CHUNK + 1].transpose(0, 2, 1)], axis=-1)
    return bbp, pwr, cnp, pw1


def _s5_small_cotangents(r12, q12, xy, uv, da64):
    n = S5_STATE
    fold = lambda a: a[..., :n] + a[..., n:]
    fold_m = lambda a: a[..., n:] - a[..., :n]
    dbb_re, dbb_im = fold(r12[:, :S5_CH]), fold_m(r12[:, S5_CH:])
    dpr, dpi = fold(q12[:, :CHUNK]), fold_m(q12[:, CHUNK:])
    dc_re, dc_im = -fold_m(xy[:, :S5_CH]), -fold(xy[:, S5_CH:])
    dp1_re, dp1_im = -fold_m(uv[:, :CHUNK]), -fold(uv[:, CHUNK:])
    zero = jnp.zeros((S5_GROUPS, n, 1), F32)
    dpw_re = (jnp.concatenate([dpr[:, ::-1].transpose(0, 2, 1), zero], axis=-1)
              + jnp.concatenate([zero, dp1_re.transpose(0, 2, 1)], axis=-1)).at[:, :, CHUNK].add(da64[:, :n])
    dpw_im = (jnp.concatenate([dpi[:, ::-1].transpose(0, 2, 1), zero], axis=-1)
              + jnp.concatenate([zero, dp1_im.transpose(0, 2, 1)], axis=-1)).at[:, :, CHUNK].add(da64[:, n:])
    return dbb_re, dbb_im, dc_re, dc_im, dpw_re, dpw_im


def _s5_scan_powers(a_re, a_im, log_dt, nsteps):
    dt = jnp.exp(log_dt)[:, None]
    steps = (CHUNK * (2.0 ** jnp.arange(8, dtype=F32)))[None, :, None]
    mag = jnp.exp((a_re * dt)[:, None, :] * steps)
    ang = (a_im * dt)[:, None, :] * steps
    re, im = mag * jnp.cos(ang), mag * jnp.sin(ang)
    del nsteps
    return jnp.concatenate([re, re], -1), jnp.concatenate([-im, im], -1)


def _build_toeplitz(kmat_ref, tg_ref):
    lane = lax.broadcasted_iota(jnp.int32, (CHUNK, CHUNK * S5_CH), 1)
    srow = lax.broadcasted_iota(jnp.int32, (CHUNK, CHUNK * S5_CH), 0)
    keep = lane >= S5_CH * srow
    for h in range(S5_CH):
        row = jnp.broadcast_to(kmat_ref[0, h:h + 1, :], (CHUNK, CHUNK * S5_CH))
        rolled = pltpu.roll(row, 0, 1, stride=S5_CH, stride_axis=0)
        tg_ref[h * CHUNK:(h + 1) * CHUNK, :] = jnp.where(keep, rolled, 0.0).astype(BF)


def _swap_halves(x):
    return pltpu.roll(x, S5_STATE, 1)


def _build_state_tables(bbp_ref, pwr_ref, cnp_ref, pw1_ref, wst_ref, wofft_ref):
    lane = lax.broadcasted_iota(jnp.int32, (1, 2 * S5_STATE), 1)
    left = lane < S5_STATE
    pwr = pwr_ref[0]
    pwr_sw = _swap_halves(pwr)
    for hh in range(S5_CH):
        bb = bbp_ref[0, hh:hh + 1, :]
        bb_sw = _swap_halves(bb)
        wst_ref[hh * CHUNK:(hh + 1) * CHUNK, :] = (jnp.where(left, bb, bb_sw) * pwr
                                                   + jnp.where(left, -bb_sw, bb) * pwr_sw).astype(BF)
    cn = cnp_ref[0]
    cn_sw = _swap_halves(cn)
    c_a = jnp.where(left, cn, -cn_sw)
    c_b = jnp.where(left, -cn_sw, -cn)
    for t in range(CHUNK):
        p = pw1_ref[0, t:t + 1, :]
        wofft_ref[t * S5_CH:(t + 1) * S5_CH, :] = (c_a * p + c_b * _swap_halves(p)).astype(BF)


def _s5_core_fwd(uflat, kmat, bbp, pwr, cnp, pw1, p1, p2):
    G, nc, W = uflat.shape
    nsteps = max(1, (nc - 1).bit_length())

    def body(u_ref, k_ref, bbp_ref, pwr_ref, cnp_ref, pw1_ref, p1_ref, p2_ref, y_ref, h_ref, tg_ref, wst_ref, wofft_ref):
        _build_toeplitz(k_ref, tg_ref)
        _build_state_tables(bbp_ref, pwr_ref, cnp_ref, pw1_ref, wst_ref, wofft_ref)
        u = u_ref[0]
        x = _dot(u, wst_ref[...])
        row = lax.broadcasted_iota(jnp.int32, x.shape, 0)
        d = 1
        for k in range(nsteps):
            sh = jnp.where(row >= d, pltpu.roll(x, d, 0), 0.0)
            x = x + p1_ref[0, k:k + 1, :] * sh + p2_ref[0, k:k + 1, :] * _swap_halves(sh)
            d *= 2
        h = jnp.where(row >= 1, pltpu.roll(x, 1, 0), 0.0)
        h_ref[0] = h
        y = _dot(u, tg_ref[...]) + _dot_nt(h.astype(BF), wofft_ref[...])
        y_ref[0] = y.astype(BF)

    spec_g = lambda a, b: pl.BlockSpec((1, a, b), lambda g: (g, 0, 0))
    return pl.pallas_call(
        body, name="s5_core_fwd", grid=(G,),
        in_specs=[spec_g(nc, W), spec_g(S5_CH, W), spec_g(S5_CH, 2 * S5_STATE), spec_g(CHUNK, 2 * S5_STATE),
                  spec_g(S5_CH, 2 * S5_STATE), spec_g(CHUNK, 2 * S5_STATE), spec_g(8, 2 * S5_STATE), spec_g(8, 2 * S5_STATE)],
        out_specs=[spec_g(nc, W), spec_g(nc, 2 * S5_STATE)],
        out_shape=[jax.ShapeDtypeStruct((G, nc, W), BF), jax.ShapeDtypeStruct((G, nc, 2 * S5_STATE), F32)],
        scratch_shapes=[pltpu.VMEM((W, W), BF), pltpu.VMEM((W, 2 * S5_STATE), BF), pltpu.VMEM((W, 2 * S5_STATE), BF)],
        compiler_params=_params(("arbitrary",)),
    )(uflat, kmat, bbp, pwr, cnp, pw1, p1, p2)


def _s5_core_bwd(uflat, dyflat, hsave, kmat, bbp, pwr, cnp, pw1, p1, p2):
    G, nc, W = uflat.shape
    nsteps = max(1, (nc - 1).bit_length())

    def body(u_ref, dy_ref, h_ref, k_ref, bbp_ref, pwr_ref, cnp_ref, pw1_ref, p1_ref, p2_ref,
             du_ref, dk_ref, r12_ref, q12_ref, xy_ref, uv_ref, da_ref, tg_ref, flip_ref, wst_ref, wofft_ref):
        @pl.when(pl.program_id(0) == 0)
        def _():
            r = lax.broadcasted_iota(jnp.int32, (W, W), 0)
            c = lax.broadcasted_iota(jnp.int32, (W, W), 1)
            flip_ref[...] = (((r >> 6) == (c >> 6)) & ((r & (CHUNK - 1)) + (c & (CHUNK - 1)) == CHUNK - 1)).astype(BF)

        _build_toeplitz(k_ref, tg_ref)
        _build_state_tables(bbp_ref, pwr_ref, cnp_ref, pw1_ref, wst_ref, wofft_ref)
        u = u_ref[0]
        dy = dy_ref[0]
        h = h_ref[0]
        gh = _dot(dy, wofft_ref[...])
        row = lax.broadcasted_iota(jnp.int32, gh.shape, 0)
        x = jnp.where(row < nc - 1, pltpu.roll(gh, nc - 1, 0), 0.0)
        d = 1
        for k in range(nsteps):
            sh = jnp.where(row < nc - d, pltpu.roll(x, nc - d, 0), 0.0)
            x = x + p1_ref[0, k:k + 1, :] * sh - p2_ref[0, k:k + 1, :] * _swap_halves(sh)
            d *= 2
        gs = x.astype(BF)
        du_ref[0] = (_dot_nt(dy, tg_ref[...]) + _dot_nt(gs, wst_ref[...])).astype(BF)

        dwst = _dot_tn(u, gs)
        pwr = pwr_ref[0]
        pwr_sw = _swap_halves(pwr)
        q1 = jnp.zeros((CHUNK, 2 * S5_STATE), F32)
        q2 = jnp.zeros((CHUNK, 2 * S5_STATE), F32)
        r1_rows, r2_rows = [], []
        for hh in range(S5_CH):
            blk = dwst[hh * CHUNK:(hh + 1) * CHUNK, :]
            r1_rows.append(jnp.sum(blk * pwr, axis=0, keepdims=True))
            r2_rows.append(jnp.sum(blk * pwr_sw, axis=0, keepdims=True))
            bb = bbp_ref[0, hh:hh + 1, :]
            q1 = q1 + blk * bb
            q2 = q2 + blk * _swap_halves(bb)
        r12_ref[0] = jnp.concatenate(r1_rows + r2_rows, axis=0)
        q12_ref[0] = jnp.concatenate([q1, q2], axis=0)

        dwofft = _dot_tn(dy, h.astype(BF))
        cn = cnp_ref[0]
        cn_sw = _swap_halves(cn)
        xa = jnp.zeros((S5_CH, 2 * S5_STATE), F32)
        ya = jnp.zeros((S5_CH, 2 * S5_STATE), F32)
        ru_rows, rv_rows = [], []
        for t in range(CHUNK):
            blk = dwofft[t * S5_CH:(t + 1) * S5_CH, :]
            p = pw1_ref[0, t:t + 1, :]
            xa = xa + blk * p
            ya = ya + blk * _swap_halves(p)
            ru_rows.append(jnp.sum(blk * cn, axis=0, keepdims=True))
            rv_rows.append(jnp.sum(blk * cn_sw, axis=0, keepdims=True))
        xy_ref[0] = jnp.concatenate([xa, ya], axis=0)
        uv_ref[0] = jnp.concatenate(ru_rows + rv_rows, axis=0)
        r1 = jnp.sum(x * h, axis=0, keepdims=True)
        r2 = jnp.sum(x * _swap_halves(h), axis=0, keepdims=True)
        da_ref[0] = jnp.concatenate([r1, r2, jnp.zeros((6, 2 * S5_STATE), F32)], axis=0)
        lane = lax.broadcasted_iota(jnp.int32, (CHUNK, W), 1)
        srow = lax.broadcasted_iota(jnp.int32, (CHUNK, W), 0)
        keep = lane < S5_CH * (srow + 1)
        ur = _dot(u, flip_ref[...]).astype(BF)
        for hh in range(S5_CH):
            dt_h = _dot_tn(ur[:, hh * CHUNK:(hh + 1) * CHUNK], dy)
            back = pltpu.roll(dt_h, S5_CH, 1, stride=S5_CH, stride_axis=0)
            dk_ref[0, hh:hh + 1, :] = jnp.sum(jnp.where(keep, back, 0.0), axis=0, keepdims=True)

    spec_g = lambda a, b: pl.BlockSpec((1, a, b), lambda g: (g, 0, 0))
    return pl.pallas_call(
        body, name="s5_core_bwd", grid=(G,),
        in_specs=[spec_g(nc, W), spec_g(nc, W), spec_g(nc, 2 * S5_STATE), spec_g(S5_CH, W), spec_g(S5_CH, 2 * S5_STATE),
                  spec_g(CHUNK, 2 * S5_STATE), spec_g(S5_CH, 2 * S5_STATE), spec_g(CHUNK, 2 * S5_STATE),
                  spec_g(8, 2 * S5_STATE), spec_g(8, 2 * S5_STATE)],
        out_specs=[spec_g(nc, W), spec_g(S5_CH, W), spec_g(2 * S5_CH, 2 * S5_STATE), spec_g(2 * CHUNK, 2 * S5_STATE),
                   spec_g(2 * S5_CH, 2 * S5_STATE), spec_g(2 * CHUNK, 2 * S5_STATE), spec_g(8, 2 * S5_STATE)],
        out_shape=[jax.ShapeDtypeStruct((G, nc, W), BF), jax.ShapeDtypeStruct((G, S5_CH, W), F32),
                   jax.ShapeDtypeStruct((G, 2 * S5_CH, 2 * S5_STATE), F32), jax.ShapeDtypeStruct((G, 2 * CHUNK, 2 * S5_STATE), F32),
                   jax.ShapeDtypeStruct((G, 2 * S5_CH, 2 * S5_STATE), F32), jax.ShapeDtypeStruct((G, 2 * CHUNK, 2 * S5_STATE), F32),
                   jax.ShapeDtypeStruct((G, 8, 2 * S5_STATE), F32)],
        scratch_shapes=[pltpu.VMEM((W, W), BF), pltpu.VMEM((W, W), BF), pltpu.VMEM((W, 2 * S5_STATE), BF),
                        pltpu.VMEM((W, 2 * S5_STATE), BF)],
        compiler_params=_params(("arbitrary",)),
    )(uflat, dyflat, hsave, kmat, bbp, pwr, cnp, pw1, p1, p2)


def _flat_hs(a, nc):
    return a.reshape(nc, CHUNK, S5_GROUPS, S5_CH).transpose(2, 0, 3, 1).reshape(S5_GROUPS, nc, CHUNK * S5_CH)


def _unflat_hs(a, nc):
    return a.reshape(S5_GROUPS, nc, S5_CH, CHUNK).transpose(1, 3, 0, 2).reshape(nc * CHUNK, D_MODEL)


def _flat_tk(a, nc):
    return a.reshape(nc, CHUNK, S5_GROUPS, S5_CH).transpose(2, 0, 1, 3).reshape(S5_GROUPS, nc, CHUNK * S5_CH)


def _unflat_tk(a, nc):
    return a.reshape(S5_GROUPS, nc, CHUNK, S5_CH).transpose(1, 2, 0, 3).reshape(nc * CHUNK, D_MODEL)


def _s5_post_fwd(yssm, proj, s5_d, w_glu, b_glu):
    L = yssm.shape[0]
    tl = min(TOKEN_TILE, L)

    def body(ys_ref, u_ref, z_ref, d_ref, wg_ref, bg_ref, o_ref):
        u = u_ref[...].astype(F32)
        a = _gelu(ys_ref[...].astype(F32) + d_ref[...] * u)
        y = a * _sigmoid(_dot(a.astype(BF), wg_ref[...]) + bg_ref[...])
        z = z_ref[...].astype(F32)
        o_ref[...] = (y * z * _sigmoid(z)).astype(BF)

    return pl.pallas_call(
        body, name="s5_post_fwd", grid=(L // tl,),
        in_specs=[_row_spec(tl, D_MODEL), _row_spec(tl, D_MODEL, 0), _row_spec(tl, D_MODEL, 1), _const_spec((1, D_MODEL)),
                  _const_spec((D_MODEL, D_MODEL)), _const_spec((1, D_MODEL))],
        out_specs=_row_spec(tl, D_MODEL),
        out_shape=jax.ShapeDtypeStruct((L, D_MODEL), BF),
        compiler_params=_params(("arbitrary",)),
    )(yssm, proj, proj, s5_d, w_glu, b_glu)


def _s5_post_bwd(dys5, yssm, proj, s5_d, w_glu, b_glu):
    L = yssm.shape[0]
    tl = min(TOKEN_TILE, L)

    def body(dy_ref, ys_ref, u_ref, z_ref, d_ref, wg_ref, bg_ref, dz_ref, dys_ref, a_ref, dgl_ref, dbg_ref, dd_ref):
        @pl.when(pl.program_id(0) == 0)
        def _():
            dbg_ref[...] = jnp.zeros_like(dbg_ref)
            dd_ref[...] = jnp.zeros_like(dd_ref)

        u = u_ref[...].astype(F32)
        y0 = ys_ref[...].astype(F32) + d_ref[...] * u
        a = _gelu(y0)
        a_bf = a.astype(BF)
        sg = _sigmoid(_dot(a_bf, wg_ref[...]) + bg_ref[...])
        y = a * sg
        z = z_ref[...].astype(F32)
        sz = _sigmoid(z)
        dout = dy_ref[...].astype(F32)
        dz_ref[...] = (dout * y * sz * (1.0 + z * (1.0 - sz))).astype(BF)
        dyv = dout * z * sz
        dgl = dyv * a * sg * (1.0 - sg)
        dgl_bf = dgl.astype(BF)
        da = dyv * sg + _dot_nt(dgl_bf, wg_ref[...])
        dy0 = da * _gelu_grad(y0)
        dbg_ref[...] += jnp.sum(dgl, axis=0, keepdims=True)
        dd_ref[...] += jnp.sum(dy0 * u, axis=0, keepdims=True)
        dys_ref[...] = dy0.astype(BF)
        a_ref[...] = a_bf
        dgl_ref[...] = dgl_bf

    big = jax.ShapeDtypeStruct((L, D_MODEL), BF)
    vec = jax.ShapeDtypeStruct((1, D_MODEL), F32)
    return pl.pallas_call(
        body, name="s5_post_bwd", grid=(L // tl,),
        in_specs=[_row_spec(tl, D_MODEL), _row_spec(tl, D_MODEL), _row_spec(tl, D_MODEL, 0), _row_spec(tl, D_MODEL, 1),
                  _const_spec((1, D_MODEL)), _const_spec((D_MODEL, D_MODEL)), _const_spec((1, D_MODEL))],
        out_specs=[_row_spec(tl, D_MODEL)] * 4 + [_const_spec((1, D_MODEL))] * 2,
        out_shape=[big, big, big, big, vec, vec],
        compiler_params=_params(("arbitrary",)),
    )(dys5, yssm, proj, proj, s5_d, w_glu, b_glu)


def _cumsum_rows(a):
    row = lax.broadcasted_iota(jnp.int32, a.shape, 0)
    d = 1
    while d < a.shape[0]:
        a = a + jnp.where(row >= d, pltpu.roll(a, d, 0), 0.0)
        d *= 2
    return a


def _rev_cumsum_rows(a):
    n = a.shape[0]
    row = lax.broadcasted_iota(jnp.int32, a.shape, 0)
    d = 1
    while d < n:
        a = a + jnp.where(row < n - d, pltpu.roll(a, n - d, 0), 0.0)
        d *= 2
    return a


def _ssd_conv_fwd(first, xs_ref, bc_ref, hx_ref, hb_ref, cw_ref, cb_ref, xp_ref, tl):
    hal = jnp.concatenate([hx_ref[...], hb_ref[...]], axis=1)
    xp_ref[0:8, :] = jnp.where(first, 0.0, hal)
    xp_ref[8:8 + tl, 0:1024] = xs_ref[...]
    xp_ref[8:8 + tl, 1024:2048] = bc_ref[...]
    pre = cb_ref[...] + cw_ref[0:1, :] * xp_ref[5:5 + tl, :]
    for k in range(1, 4):
        pre = pre + cw_ref[k:k + 1, :] * xp_ref[5 + k:5 + k + tl, :]
    return pre


def _onehot_lane(h):
    return (lax.broadcasted_iota(jnp.int32, (1, LANES), 1) == h).astype(F32)


def _dot_exact(x, e):
    hi = x.astype(BF)
    r = x - hi.astype(F32)
    mid = r.astype(BF)
    lo = (r - mid.astype(F32)).astype(BF)
    return _dot(hi, e) + _dot(mid, e) + _dot(lo, e)


def _head_expand_matrices():
    e = lax.broadcasted_iota(jnp.int32, (LANES, D_MODEL), 0) == (lax.broadcasted_iota(jnp.int32, (LANES, D_MODEL), 1) >> 6)
    et = (lax.broadcasted_iota(jnp.int32, (D_MODEL, LANES), 0) >> 6) == lax.broadcasted_iota(jnp.int32, (D_MODEL, LANES), 1)
    return e.astype(BF), et.astype(BF)


def _group_masks():
    r64 = lax.broadcasted_iota(jnp.int32, (4 * CHUNK, CHUNK), 0)
    causal4 = (r64 & (CHUNK - 1)) >= lax.broadcasted_iota(jnp.int32, (4 * CHUNK, CHUNK), 1)
    r256 = lax.broadcasted_iota(jnp.int32, (4 * CHUNK, 4 * SSD_HEAD_DIM), 0)
    same = (r256 >> 6) == (lax.broadcasted_iota(jnp.int32, (4 * CHUNK, 4 * SSD_HEAD_DIM), 1) >> 6)
    return causal4, same


def _group_decay(acs, acs_t, j, causal4):
    col = jnp.concatenate([acs[:, 4 * j + hh:4 * j + hh + 1] for hh in range(4)], axis=0)
    rowv = jnp.concatenate([jnp.broadcast_to(acs_t[4 * j + hh:4 * j + hh + 1, :], (CHUNK, CHUNK)) for hh in range(4)], axis=0)
    return jnp.where(causal4, jnp.exp(col - rowv), 0.0)


def _group_last_decay(acs_t, j):
    return jnp.concatenate([jnp.broadcast_to(jnp.exp(acs_t[4 * j + hh:4 * j + hh + 1, CHUNK - 1:CHUNK]), (SSD_HEAD_DIM, 1))
                            for hh in range(4)], axis=0)


def _fold_heads(r):
    return r[0:CHUNK] + r[CHUNK:2 * CHUNK] + r[2 * CHUNK:3 * CHUNK] + r[3 * CHUNK:4 * CHUNK]


def _ssd_specs_in(tl, nt, rev):
    t_of = (lambda i: nt - 1 - i) if rev else (lambda i: i)
    rows = lambda w, col: pl.BlockSpec((tl, w), lambda i: (t_of(i), col))
    halo = lambda col: pl.BlockSpec((8, 1024), lambda i: (jnp.maximum(t_of(i) * (tl // 8) - 1, 0), col))
    return t_of, rows, halo


def _ssd_fwd(proj, pdt, conv_w, conv_b, dt_bias, a_log, ssd_d, norm_w):
    L = proj.shape[0]
    tl = min(TOKEN_TILE, L)
    nt, ncl = L // tl, tl // CHUNK
    _, rows, halo = _ssd_specs_in(tl, nt, False)

    def body(xs_ref, bc_ref, hx_ref, hb_ref, dt_ref, z_ref, cw_ref, cb_ref, dtb_ref, al_ref, dd_ref, nw_ref,
             y_ref, ypre_ref, st_ref, xp_ref, xbc_ref, dts_ref, hst_ref):
        i = pl.program_id(0)

        @pl.when(i == 0)
        def _():
            hst_ref[...] = jnp.zeros_like(hst_ref)

        pre = _ssd_conv_fwd(i == 0, xs_ref, bc_ref, hx_ref, hb_ref, cw_ref, cb_ref, xp_ref, tl)
        xbc_ref[...] = pre * _sigmoid(pre)
        dts_ref[...] = _softplus(dt_ref[...] + dtb_ref[...])
        a_neg = -jnp.exp(al_ref[...])
        e16, _ = _head_expand_matrices()
        causal4, same = _group_masks()
        dd_x = _dot_exact(jnp.broadcast_to(dd_ref[...], (8, LANES)), e16)[0:1, :]

        def chunk(c, carry):
            r0 = pl.multiple_of(c * CHUNK, CHUNK)
            xbc = xbc_ref[pl.ds(r0, CHUNK), :]
            dtc = dts_ref[pl.ds(r0, CHUNK), :]
            acs = _cumsum_rows(dtc * a_neg)
            acs_t = acs.T
            acs_x = _dot_exact(acs, e16)
            xs = xbc[:, 0:1024]
            xd = xs * _dot_exact(dtc, e16)
            xd_bf = xd.astype(BF)
            xdd = (xd * jnp.exp(acs_x[CHUNK - 1:CHUNK, :] - acs_x)).astype(BF)
            e_x = jnp.exp(acs_x)
            for j in range(SSD_GROUPS):
                sl = slice(256 * j, 256 * (j + 1))
                bj = xbc[:, 1024 + 128 * j:1024 + 128 * (j + 1)].astype(BF)
                cj = xbc[:, 1536 + 128 * j:1536 + 128 * (j + 1)].astype(BF)
                g = _dot_nt(cj, bj)
                hj = hst_ref[sl, :]
                zj = _dot_nt(cj, hj.astype(BF))
                sc = (jnp.concatenate([g] * 4, axis=0) * _group_decay(acs, acs_t, j, causal4)).astype(BF)
                yd = _fold_heads(jnp.where(same, _dot(sc, xd_bf[:, sl]), 0.0))
                ypre_ref[pl.ds(r0, CHUNK), sl] = yd + e_x[:, sl] * zj + dd_x[:, sl] * xs[:, sl]
                st_ref[c, sl, :] = hj
                hst_ref[sl, :] = _group_last_decay(acs_t, j) * hj + _dot_tn(xdd[:, sl], bj)
            return carry

        lax.fori_loop(0, ncl, chunk, 0)
        z = z_ref[...]
        gg = ypre_ref[...] * z * _sigmoid(z)
        for j in range(SSD_GROUPS):
            seg = gg[:, 256 * j:256 * (j + 1)]
            r = lax.rsqrt(jnp.mean(seg * seg, axis=-1, keepdims=True) + EPS)
            y_ref[:, 256 * j:256 * (j + 1)] = (seg * r * nw_ref[:, 256 * j:256 * (j + 1)]).astype(BF)

    nc = L // CHUNK
    return pl.pallas_call(
        body, name="ssd_fwd", grid=(nt,),
        in_specs=[rows(1024, 1), rows(1024, 2), halo(1), halo(2), rows(LANES, 0), rows(1024, 0),
                  _const_spec((4, 2048)), _const_spec((1, 2048)), _const_spec((1, LANES)), _const_spec((1, LANES)),
                  _const_spec((1, LANES)), _const_spec((1, D_MODEL))],
        out_specs=[_row_spec(tl, D_MODEL), _row_spec(tl, D_MODEL), pl.BlockSpec((ncl, 1024, SSD_STATE), lambda i: (i, 0, 0))],
        out_shape=[jax.ShapeDtypeStruct((L, D_MODEL), BF), jax.ShapeDtypeStruct((L, D_MODEL), F32),
                   jax.ShapeDtypeStruct((nc, 1024, SSD_STATE), F32)],
        scratch_shapes=[pltpu.VMEM((tl + 8, 2048), F32), pltpu.VMEM((tl, 2048), F32), pltpu.VMEM((tl, LANES), F32),
                        pltpu.VMEM((1024, SSD_STATE), F32)],
        compiler_params=_params(("arbitrary",)),
    )(proj, proj, proj, proj, pdt, proj, conv_w, conv_b, dt_bias, a_log, ssd_d, norm_w)


def _ssd_bwd(dyssd, ypre, proj, pdt, states, conv_w, conv_b, dt_bias, a_log, ssd_d, norm_w):
    L = proj.shape[0]
    tl = min(TOKEN_TILE, L)
    nt, ncl = L // tl, tl // CHUNK
    t_of, rows, halo = _ssd_specs_in(tl, nt, True)

    def body(dy_ref, ypre_ref, z_ref, xs_ref, bc_ref, hx_ref, hb_ref, dt_ref, st_ref, cw_ref, cb_ref, dtb_ref, al_ref,
             dd_ref, nw_ref,
             dxbc_ref, ddt_ref, dz_ref, dcw_ref, dcb_ref, ddtb_ref, dal_ref, ddd_ref, dnw_ref,
             xp_ref, xbc_ref, pre_ref, dts_ref, dyp_ref, dxs_ref, ddts_ref, dp_ref, dh_ref):
        i = pl.program_id(0)

        @pl.when(i == 0)
        def _():
            for r in (dcw_ref, dcb_ref, ddtb_ref, dal_ref, ddd_ref, dnw_ref, dh_ref):
                r[...] = jnp.zeros_like(r)
            dp_ref[tl:tl + 8, :] = jnp.zeros((8, 2048), F32)

        pre = _ssd_conv_fwd(t_of(i) == 0, xs_ref, bc_ref, hx_ref, hb_ref, cw_ref, cb_ref, xp_ref, tl)
        pre_ref[...] = pre
        xbc_ref[...] = pre * _sigmoid(pre)
        dts_ref[...] = _softplus(dt_ref[...] + dtb_ref[...])
        a_neg = -jnp.exp(al_ref[...])

        ypre = ypre_ref[...]
        z = z_ref[...]
        sz = _sigmoid(z)
        gg = ypre * z * sz
        dout = dy_ref[...]
        for j in range(SSD_GROUPS):
            sl = slice(256 * j, 256 * (j + 1))
            seg = gg[:, sl]
            r = lax.rsqrt(jnp.mean(seg * seg, axis=-1, keepdims=True) + EPS)
            gh = seg * r
            dnw_ref[:, sl] += jnp.sum(dout[:, sl] * gh, axis=0, keepdims=True)
            gw = dout[:, sl] * nw_ref[:, sl]
            dgg = r * (gw - gh * jnp.mean(gw * gh, axis=-1, keepdims=True))
            dyp_ref[:, sl] = dgg * z[:, sl] * sz[:, sl]
            dz_ref[:, sl] = (dgg * ypre[:, sl] * sz[:, sl] * (1.0 + z[:, sl] * (1.0 - sz[:, sl]))).astype(BF)

        e16, e16t = _head_expand_matrices()
        causal4, same = _group_masks()
        dd_x = _dot_exact(jnp.broadcast_to(dd_ref[...], (8, LANES)), e16)[0:1, :]
        last_row = (lax.broadcasted_iota(jnp.int32, (CHUNK, 1), 0) == CHUNK - 1).astype(F32)
        sel_rows = lax.broadcasted_iota(jnp.int32, (4 * CHUNK, LANES), 0) >> 6
        sel_lanes = lax.broadcasted_iota(jnp.int32, (4 * CHUNK, LANES), 1)

        def chunk(k, carry):
            dal_acc, ddx_acc = carry
            c = ncl - 1 - k
            r0 = pl.multiple_of(c * CHUNK, CHUNK)
            xbc = xbc_ref[pl.ds(r0, CHUNK), :]
            dtc = dts_ref[pl.ds(r0, CHUNK), :]
            dyp = dyp_ref[pl.ds(r0, CHUNK), :]
            acs = _cumsum_rows(dtc * a_neg)
            acs_t = acs.T
            acs_x = _dot_exact(acs, e16)
            dt_x = _dot_exact(dtc, e16)
            xs = xbc[:, 0:1024]
            xd = xs * dt_x
            xd_bf = xd.astype(BF)
            dec_x = jnp.exp(acs_x[CHUNK - 1:CHUNK, :] - acs_x)
            xdd = xd * dec_x
            xdd_bf = xdd.astype(BF)
            dz = dyp * jnp.exp(acs_x)
            dz_bf = dz.astype(BF)
            ddx_acc = ddx_acc + jnp.sum(dyp * xs, axis=0, keepdims=True)
            dacs = jnp.zeros((CHUNK, LANES), F32)
            hsum = jnp.zeros((1, LANES), F32)
            p1_l, p2_l, p3_l = [], [], []
            for j in range(SSD_GROUPS):
                sl = slice(256 * j, 256 * (j + 1))
                bj = xbc[:, 1024 + 128 * j:1024 + 128 * (j + 1)].astype(BF)
                cj = xbc[:, 1536 + 128 * j:1536 + 128 * (j + 1)].astype(BF)
                g = _dot_nt(cj, bj)
                hj = st_ref[c, sl, :]
                hj_bf = hj.astype(BF)
                dhj = dh_ref[sl, :]
                dhj_bf = dhj.astype(BF)
                zj = _dot_nt(cj, hj_bf)
                qj = _dot_nt(bj, dhj_bf)
                lm = _group_decay(acs, acs_t, j, causal4)
                sc = jnp.concatenate([g] * 4, axis=0) * lm
                sc_bf = sc.astype(BF)
                dym = jnp.where(same, jnp.concatenate([dyp[:, sl]] * 4, axis=0), 0.0).astype(BF)
                dsc = _dot_nt(dym, xd_bf[:, sl])
                dxd = _dot_tn(sc_bf, dym) + qj * dec_x[:, sl]
                m = dsc * sc
                dg_bf = _fold_heads(dsc * lm).astype(BF)
                rs = jnp.sum(m, axis=1, keepdims=True)
                e2 = dhj * hj
                for hh in range(4):
                    oh = _onehot_lane(4 * j + hh)
                    dacs = dacs + oh * rs[CHUNK * hh:CHUNK * (hh + 1)]
                    hsum = hsum + oh * jnp.sum(jnp.sum(e2[64 * hh:64 * (hh + 1)], axis=0, keepdims=True), axis=1, keepdims=True)
                sel = (sel_rows + 4 * j == sel_lanes).astype(BF)
                hi = m.astype(BF)
                rem = m - hi.astype(F32)
                mid = rem.astype(BF)
                lo = (rem - mid.astype(F32)).astype(BF)
                dacs = dacs - (_dot_tn(hi, sel) + _dot_tn(mid, sel) + _dot_tn(lo, sel))
                p1_l.append(dz[:, sl] * zj)
                p2_l.append(qj * xdd[:, sl])
                p3_l.append(dxd * xs[:, sl])
                dxs_ref[pl.ds(r0, CHUNK), sl] = dd_x[:, sl] * dyp[:, sl] + dxd * dt_x[:, sl]
                dxs_ref[pl.ds(r0, CHUNK), 1536 + 128 * j:1536 + 128 * (j + 1)] = _dot(dg_bf, bj) + _dot(dz_bf[:, sl], hj_bf)
                dxs_ref[pl.ds(r0, CHUNK), 1024 + 128 * j:1024 + 128 * (j + 1)] = _dot_tn(dg_bf, cj) + _dot(xdd_bf[:, sl], dhj_bf)
                dh_ref[sl, :] = _group_last_decay(acs_t, j) * dhj + _dot_tn(dz_bf[:, sl], cj)
            stacked = jnp.concatenate([jnp.concatenate(p1_l, axis=1), jnp.concatenate(p2_l, axis=1), jnp.concatenate(p3_l, axis=1)], axis=0)
            red = _dot_exact(stacked, e16t)
            r1, r2, ddtc = red[0:CHUNK], red[CHUNK:2 * CHUNK], red[2 * CHUNK:3 * CHUNK]
            tot = jnp.sum(r2, axis=0, keepdims=True) + jnp.exp(acs[CHUNK - 1:CHUNK, :]) * hsum
            da = _rev_cumsum_rows(dacs + r1 - r2 + last_row * tot)
            ddts_ref[pl.ds(r0, CHUNK), :] = ddtc + da * a_neg
            dal_acc = dal_acc + jnp.sum(da * dtc, axis=0, keepdims=True)
            return dal_acc, ddx_acc

        dal_acc, ddx_acc = lax.fori_loop(0, ncl, chunk, (jnp.zeros((1, LANES), F32), jnp.zeros((1, D_MODEL), F32)))
        dal_ref[...] += dal_acc * a_neg
        ddd_ref[...] += _dot_exact(jnp.broadcast_to(ddx_acc, (8, D_MODEL)), e16t)[0:1, :]
        ddt_raw = ddts_ref[...] * _sigmoid(dt_ref[...] + dtb_ref[...])
        ddt_ref[...] = ddt_raw
        ddtb_ref[...] += jnp.sum(ddt_raw, axis=0, keepdims=True)

        pre = pre_ref[...]
        sp = _sigmoid(pre)
        dpre = dxs_ref[...] * sp * (1.0 + pre * (1.0 - sp))
        dp_ref[0:tl, :] = dpre
        dcb_ref[...] += jnp.sum(dpre, axis=0, keepdims=True)
        dx = jnp.zeros((tl, 2048), F32)
        for k in range(4):
            dcw_ref[k:k + 1, :] += jnp.sum(dpre * xp_ref[5 + k:5 + k + tl, :], axis=0, keepdims=True)
            dx = dx + cw_ref[k:k + 1, :] * dp_ref[3 - k:3 - k + tl, :]
        dxbc_ref[...] = dx.astype(BF)
        dp_ref[tl:tl + 8, :] = dp_ref[0:8, :]

    vec = lambda w: jax.ShapeDtypeStruct((1, w), F32)
    rrow = lambda w: pl.BlockSpec((tl, w), lambda i: (t_of(i), 0))
    return pl.pallas_call(
        body, name="ssd_bwd", grid=(nt,),
        in_specs=[rrow(D_MODEL), rrow(D_MODEL), rows(1024, 0), rows(1024, 1), rows(1024, 2), halo(1), halo(2), rows(LANES, 0),
                  pl.BlockSpec((ncl, 1024, SSD_STATE), lambda i: (t_of(i), 0, 0)),
                  _const_spec((4, 2048)), _const_spec((1, 2048)), _const_spec((1, LANES)), _const_spec((1, LANES)),
                  _const_spec((1, LANES)), _const_spec((1, D_MODEL))],
        out_specs=[rrow(2048), rrow(LANES), rrow(D_MODEL), _const_spec((8, 2048)), _const_spec((1, 2048)),
                   _const_spec((1, LANES)), _const_spec((1, LANES)), _const_spec((1, LANES)), _const_spec((1, D_MODEL))],
        out_shape=[jax.ShapeDtypeStruct((L, 2048), BF), jax.ShapeDtypeStruct((L, LANES), F32), jax.ShapeDtypeStruct((L, D_MODEL), BF),
                   jax.ShapeDtypeStruct((8, 2048), F32), vec(2048), vec(LANES), vec(LANES), vec(LANES), vec(D_MODEL)],
        scratch_shapes=[pltpu.VMEM((tl + 8, 2048), F32), pltpu.VMEM((tl, 2048), F32), pltpu.VMEM((tl, 2048), F32),
                        pltpu.VMEM((tl, LANES), F32), pltpu.VMEM((tl, D_MODEL), F32), pltpu.VMEM((tl, 2048), F32),
                        pltpu.VMEM((tl, LANES), F32), pltpu.VMEM((tl + 8, 2048), F32), pltpu.VMEM((1024, SSD_STATE), F32)],
        compiler_params=_params(("arbitrary",)),
    )(dyssd, ypre, proj, proj, proj, proj, proj, pdt, states, conv_w, conv_b, dt_bias, a_log, ssd_d, norm_w)


def _head_fwd_bwd(x, ys5, yssd, p, target, w_out, w_gate, w_proj, ple_nw, fin_nw):
    L = x.shape[0]
    tl = min(TOKEN_TILE, L)
    inv_d = 1.0 / D_MODEL

    def body(x_ref, ys_ref, yd_ref, p_ref, t_ref, wo_ref, wg_ref, wp_ref, pnw_ref, fnw_ref,
             loss_ref, dys_ref, dyd_ref, dh1_ref, n2_ref, dgl_ref, dpp_ref, dpnw_ref, dfnw_ref):
        @pl.when(pl.program_id(0) == 0)
        def _():
            loss_ref[...] = jnp.zeros_like(loss_ref)
            dpnw_ref[...] = jnp.zeros_like(dpnw_ref)
            dfnw_ref[...] = jnp.zeros_like(dfnw_ref)

        h1 = x_ref[...] + _dot(ys_ref[...], wo_ref[0:1024, :]) + _dot(yd_ref[...], wo_ref[1024:2048, :])
        r1 = lax.rsqrt(jnp.mean(h1 * h1, axis=-1, keepdims=True) + EPS)
        hh1 = h1 * r1
        n2 = (hh1 * pnw_ref[...]).astype(BF)
        gate = _sigmoid(_dot(n2, wg_ref[...]))
        pp = _dot(p_ref[...].astype(BF), wp_ref[...])
        h2 = h1 + pp * gate
        r2 = lax.rsqrt(jnp.mean(h2 * h2, axis=-1, keepdims=True) + EPS)
        hh2 = h2 * r2
        err = hh2 * fnw_ref[...] - t_ref[...]
        loss_ref[...] += 0.5 * inv_d * jnp.sum(err * err)
        dyo = err * inv_d
        dfnw_ref[...] += jnp.sum(dyo * hh2, axis=0, keepdims=True)
        g2 = dyo * fnw_ref[...]
        dh2 = r2 * (g2 - hh2 * jnp.mean(g2 * hh2, axis=-1, keepdims=True))
        dpp_ref[...] = (dh2 * gate).astype(BF)
        dgl = (dh2 * pp * gate * (1.0 - gate)).astype(BF)
        dgl_ref[...] = dgl
        n2_ref[...] = n2
        dn2 = _dot_nt(dgl, wg_ref[...])
        dpnw_ref[...] += jnp.sum(dn2 * hh1, axis=0, keepdims=True)
        g1 = dn2 * pnw_ref[...]
        dh1 = dh2 + r1 * (g1 - hh1 * jnp.mean(g1 * hh1, axis=-1, keepdims=True))
        dh1_ref[...] = dh1
        dh1_bf = dh1.astype(BF)
        dys_ref[...] = _dot_nt(dh1_bf, wo_ref[0:1024, :]).astype(BF)
        dyd_ref[...] = _dot_nt(dh1_bf, wo_ref[1024:2048, :])

    big = jax.ShapeDtypeStruct((L, D_MODEL), BF)
    vec = jax.ShapeDtypeStruct((1, D_MODEL), F32)
    return pl.pallas_call(
        body, name="head_fwd_bwd", grid=(L // tl,),
        in_specs=[_row_spec(tl, D_MODEL), _row_spec(tl, D_MODEL), _row_spec(tl, D_MODEL), _row_spec(tl, 256), _row_spec(tl, D_MODEL),
                  _const_spec((2048, D_MODEL)), _const_spec((D_MODEL, D_MODEL)), _const_spec((256, D_MODEL)),
                  _const_spec((1, D_MODEL)), _const_spec((1, D_MODEL))],
        out_specs=[_const_spec((8, LANES)), _row_spec(tl, D_MODEL), _row_spec(tl, D_MODEL), _row_spec(tl, D_MODEL),
                   _row_spec(tl, D_MODEL), _row_spec(tl, D_MODEL), _row_spec(tl, D_MODEL), _const_spec((1, D_MODEL)), _const_spec((1, D_MODEL))],
        out_shape=[jax.ShapeDtypeStruct((8, LANES), F32), big, jax.ShapeDtypeStruct((L, D_MODEL), F32),
                   jax.ShapeDtypeStruct((L, D_MODEL), F32), big, big, big, vec, vec],
        compiler_params=_params(("arbitrary",)),
    )(x, ys5, yssd, p, target, w_out, w_gate, w_proj, ple_nw, fin_nw)


def _pad_lanes(v):
    return jnp.pad(v.reshape(1, -1), ((0, 0), (0, LANES - v.size)))


def _local_step(x, p, target, w):
    L = x.shape[0]
    nc = L // CHUNK
    nsteps = max(1, (nc - 1).bit_length())
    w_in_t = w["w_in"]
    w_main = w_in_t[:D_MAIN]
    w_dt = jnp.pad(w_in_t[D_MAIN:], ((0, LANES - SSD_HEADS), (0, 0)))
    norm_w = w["norm_w"].reshape(1, -1)
    s5_d = w["s5_D"].reshape(1, -1)
    b_glu = w["s5_b_glu"].reshape(1, -1)
    conv_b = w["conv_b"].reshape(1, -1)
    dtb, alog, ssd_d = _pad_lanes(w["dt_bias"]), _pad_lanes(w["A_log"]), _pad_lanes(w["ssd_D"])
    ssd_nw = w["ssd_norm_w"].reshape(1, -1)
    ple_nw = w["ple_norm_w"].reshape(1, -1)
    fin_nw = w["final_norm_w"].reshape(1, -1)

    s5_args = (w["s5_A_re"], w["s5_A_im"], w["s5_log_dt"], w["s5_B_re"], w["s5_B_im"], w["s5_C_re"], w["s5_C_im"])
    small, small_vjp = jax.vjp(_s5_discretise, *s5_args)
    kmat, kmat_vjp = jax.vjp(_s5_kmat, *small)
    bbp, pwr, cnp, pw1 = _s5_table_factors(*small)
    p1, p2 = _s5_scan_powers(w["s5_A_re"], w["s5_A_im"], w["s5_log_dt"], nsteps)

    hn, proj, pssd, pdt = _in_proj_fwd(x, norm_w, w_main, w_dt)
    uflat = _flat_hs(proj[:, :D_MODEL], nc)
    yflat, hsave = _s5_core_fwd(uflat, kmat, bbp, pwr, cnp, pw1, p1, p2)
    yssm = _unflat_tk(yflat, nc)
    ys5 = _s5_post_fwd(yssm, proj, s5_d, w["s5_w_glu"], b_glu)
    yssd, ypre, states = _ssd_fwd(pssd, pdt, w["conv_w"], conv_b, dtb, alog, ssd_d, ssd_nw)
    (loss8, dys5, dyssd, dh1, n2, dgl2, dpp, g_ple_nw, g_fin_nw) = _head_fwd_bwd(
        x, ys5, yssd, p, target, w["w_out"], w["w_ple_gate"], w["w_ple_proj"], ple_nw, fin_nw)

    (dxbc, ddt, dzd, g_cw, g_cb, g_dtb, g_alog, g_ssd_d, g_ssd_nw) = _ssd_bwd(
        dyssd, ypre, pssd, pdt, states, w["conv_w"], conv_b, dtb, alog, ssd_d, ssd_nw)
    dzs, dyssm, a_glu, dgl1, g_bglu, g_s5d = _s5_post_bwd(dys5, yssm, proj, s5_d, w["s5_w_glu"], b_glu)
    duflat, dkmat, r12, q12, xy, uv, da8 = _s5_core_bwd(uflat, _flat_tk(dyssm, nc), hsave, kmat, bbp, pwr, cnp, pw1, p1, p2)
    da64 = jnp.concatenate([da8[:, 0, :S5_STATE] + da8[:, 0, S5_STATE:], da8[:, 1, S5_STATE:] - da8[:, 1, :S5_STATE]], axis=-1)
    d_small = [a + b for a, b in zip(_s5_small_cotangents(r12, q12, xy, uv, da64), kmat_vjp(dkmat))]
    g_s5 = small_vjp(tuple(d_small))
    gx, du, g_norm_w = _in_proj_bwd(x, norm_w, dh1, _unflat_hs(duflat, nc), dyssm, s5_d, dzs, dzd, dxbc, ddt, w_main, w_dt)

    g_w_in = jnp.concatenate([
        _matmul_tn(du, hn, "dw_in_u"), _matmul_tn(dzs, hn, "dw_in_zs"), _matmul_tn(dzd, hn, "dw_in_zd"),
        _matmul_tn(dxbc, hn, "dw_in_xbc"), _matmul_tn(ddt, hn, "dw_in_dt")[:SSD_HEADS]], axis=0)
    grads = {
        "norm_w": g_norm_w, "w_in": g_w_in,
        "s5_A_re": g_s5[0], "s5_A_im": g_s5[1], "s5_log_dt": g_s5[2], "s5_B_re": g_s5[3], "s5_B_im": g_s5[4],
        "s5_C_re": g_s5[5], "s5_C_im": g_s5[6], "s5_D": g_s5d, "s5_w_glu": _matmul_tn(a_glu, dgl1, "dw_glu"), "s5_b_glu": g_bglu,
        "conv_w": g_cw[:4], "conv_b": g_cb, "dt_bias": g_dtb[:, :SSD_HEADS], "A_log": g_alog[:, :SSD_HEADS],
        "ssd_D": g_ssd_d[:, :SSD_HEADS], "ssd_norm_w": g_ssd_nw,
        "w_out": jnp.concatenate([_matmul_tn(ys5, dh1, "dw_out_s5"), _matmul_tn(yssd, dh1, "dw_out_ssd")], axis=0),
        "ple_norm_w": g_ple_nw, "w_ple_gate": _matmul_tn(n2, dgl2, "dw_gate"), "w_ple_proj": _matmul_tn(p, dpp, "dw_proj"),
        "final_norm_w": g_fin_nw,
    }
    return loss8[0, 0], gx, grads


WEIGHTS = ("norm_w", "w_in", "s5_A_re", "s5_A_im", "s5_log_dt", "s5_B_re", "s5_B_im", "s5_C_re", "s5_C_im", "s5_D", "s5_w_glu",
           "s5_b_glu", "conv_w", "conv_b", "dt_bias", "A_log", "ssd_D", "ssd_norm_w", "w_out", "ple_norm_w", "w_ple_gate",
           "w_ple_proj", "final_norm_w")
BIG = {"w_in": ((1284, 1024), 0), "s5_w_glu": ((256, 1024), 0), "w_out": ((512, 1024), 0), "w_ple_gate": ((256, 1024), 0),
       "w_ple_proj": ((256, 256), 1)}
SMALL = {"norm_w": (1024,), "s5_A_re": (64, 64), "s5_A_im": (64, 64), "s5_log_dt": (64,), "s5_B_re": (64, 64, 16),
         "s5_B_im": (64, 64, 16), "s5_C_re": (64, 16, 64), "s5_C_im": (64, 16, 64), "s5_D": (1024,), "s5_b_glu": (1024,),
         "conv_w": (4, 2048), "conv_b": (2048,), "dt_bias": (16,), "A_log": (16,), "ssd_D": (16,), "ssd_norm_w": (1024,),
         "ple_norm_w": (1024,), "final_norm_w": (1024,)}
BIG_ROWS = {n: s[0] * s[1] // LANES for n, (s, _) in BIG.items()}
BIG_ROWS_TOTAL = sum(BIG_ROWS.values())
SMALL_TOTAL = sum(math.prod(s) for s in SMALL.values())
SMALL_PIECE_ROWS = -(-SMALL_TOTAL // (N_CHIPS * 16 * LANES)) * 16
HALF_ROWS = (BIG_ROWS_TOTAL + SMALL_PIECE_ROWS) // 2
SMALL_ROW0 = BIG_ROWS_TOTAL - HALF_ROWS


def _mesh_pos():
    return lax.axis_index("x"), lax.axis_index("y"), lax.axis_index("c")


def _other_chips(x, y):
    return [(1 - x, y), (x, 1 - y), (1 - x, 1 - y)]


def _comm_params():
    return pltpu.CompilerParams(has_side_effects=True)


def _all_gather_chips(wpack, cw):
    half = wpack.shape[0] // 2

    def body(w_ref, c_ref, wo_ref, co_ref, send_sems, recv_sems, fwd_send, fwd_recv, loc_sems):
        x, y, c = _mesh_pos()
        me = 2 * x + y
        sib = (x, y, 1 - c)
        mine = pl.ds(c * half, half)
        theirs = pl.ds((1 - c) * half, half)
        others = _other_chips(x, y)
        loc = [pltpu.make_async_copy(c_ref, co_ref.at[me], loc_sems.at[0])]
        for cp in loc:
            cp.start()

        def from_chip(k, chip, dev):
            return pltpu.make_async_remote_copy(w_ref.at[mine], wo_ref.at[chip, mine], send_sems.at[2 * k], recv_sems.at[2 * k],
                                                device_id=dev, device_id_type=MESH)

        def conv_from(k, chip, dev):
            return pltpu.make_async_remote_copy(c_ref, co_ref.at[chip], send_sems.at[2 * k + 1], recv_sems.at[2 * k + 1],
                                                device_id=dev, device_id_type=MESH)

        def passed(k, chip, rows):
            return pltpu.make_async_remote_copy(wo_ref.at[chip, rows], wo_ref.at[chip, rows], fwd_send.at[k], fwd_recv.at[k],
                                                device_id=sib, device_id_type=MESH)

        sends = []
        for k, (px, py) in enumerate(others):
            sends += [from_chip(k, me, (px, py, c)), conv_from(k, me, (px, py, c))]
        for cp in sends:
            cp.start()
        fwds = []
        for k, (px, py) in enumerate(others):
            chip = 2 * px + py
            from_chip(k, chip, (px, py, c)).wait_recv()
            fwds.append(passed(k, chip, mine))
            fwds[-1].start()
        for k, (px, py) in enumerate(others):
            chip = 2 * px + py
            passed(k, chip, theirs).wait_recv()
            conv_from(k, chip, (px, py, c)).wait_recv()
        for cp in sends + fwds:
            cp.wait_send()
        for cp in loc:
            cp.wait()

    return pl.pallas_call(
        body, name="all_gather_weights", in_specs=[ANY, ANY], out_specs=[ANY, ANY],
        out_shape=[jax.ShapeDtypeStruct((N_CHIPS,) + wpack.shape, wpack.dtype), jax.ShapeDtypeStruct((N_CHIPS,) + cw.shape, cw.dtype)],
        scratch_shapes=[pltpu.SemaphoreType.DMA((6,)), pltpu.SemaphoreType.DMA((6,)), pltpu.SemaphoreType.DMA((3,)),
                        pltpu.SemaphoreType.DMA((3,)), pltpu.SemaphoreType.DMA((1,))],
        compiler_params=_comm_params(),
    )(wpack, cw)


def _exchange_pair(gp):
    def body(g_ref, r_ref, send_sems, recv_sems):
        x, y, c = _mesh_pos()
        cps = [pltpu.make_async_remote_copy(g_ref.at[s, 1 - c], r_ref.at[s], send_sems.at[s], recv_sems.at[s],
                                            device_id=(x, y, 1 - c), device_id_type=MESH) for s in range(N_CHIPS)]
        for cp in cps:
            cp.start()
        for cp in cps:
            cp.wait()

    return pl.pallas_call(
        body, name="grad_exchange_pair", in_specs=[ANY], out_specs=ANY,
        out_shape=jax.ShapeDtypeStruct((N_CHIPS,) + gp.shape[2:], gp.dtype),
        scratch_shapes=[pltpu.SemaphoreType.DMA((N_CHIPS,)), pltpu.SemaphoreType.DMA((N_CHIPS,))],
        compiler_params=_comm_params(),
    )(gp)


def _pair_sum(mine, from_sibling):
    def body(a_ref, b_ref, bf_ref, tail_ref):
        s = a_ref[0] + b_ref[0]
        bf_ref[0] = s.astype(BF)
        tail_ref[0] = s[SMALL_ROW0:, :]

    piece = pl.BlockSpec((1, HALF_ROWS, LANES), lambda i: (i, 0, 0))
    return pl.pallas_call(
        body, name="grad_pair_sum", grid=(N_CHIPS,), in_specs=[piece, piece],
        out_specs=[piece, pl.BlockSpec((1, SMALL_PIECE_ROWS, LANES), lambda i: (i, 0, 0))],
        out_shape=[jax.ShapeDtypeStruct((N_CHIPS, HALF_ROWS, LANES), BF), jax.ShapeDtypeStruct((N_CHIPS, SMALL_PIECE_ROWS, LANES), F32)],
        compiler_params=_params(("parallel",)),
    )(mine, from_sibling)


def _exchange_chips(ps_bf, ps_tail):
    def body(p_ref, t_ref, r_ref, rt_ref, send_sems, recv_sems):
        x, y, c = _mesh_pos()
        cps = []
        for k, (px, py) in enumerate(_other_chips(x, y)):
            cps.append(pltpu.make_async_remote_copy(p_ref.at[2 * px + py], r_ref.at[k], send_sems.at[2 * k], recv_sems.at[2 * k],
                                                    device_id=(px, py, c), device_id_type=MESH))
            cps.append(pltpu.make_async_remote_copy(t_ref.at[2 * px + py], rt_ref.at[k], send_sems.at[2 * k + 1],
                                                    recv_sems.at[2 * k + 1], device_id=(px, py, c), device_id_type=MESH))
        for cp in cps:
            cp.start()
        for cp in cps:
            cp.wait()

    return pl.pallas_call(
        body, name="grad_exchange_chips", in_specs=[ANY, ANY], out_specs=[ANY, ANY],
        out_shape=[jax.ShapeDtypeStruct((N_CHIPS - 1,) + ps_bf.shape[1:], ps_bf.dtype),
                   jax.ShapeDtypeStruct((N_CHIPS - 1,) + ps_tail.shape[1:], ps_tail.dtype)],
        scratch_shapes=[pltpu.SemaphoreType.DMA((6,)), pltpu.SemaphoreType.DMA((6,))],
        compiler_params=_comm_params(),
    )(ps_bf, ps_tail)


def _chip_sum(own_bf, own_tail, others_bf, others_tail):
    def body(ob_ref, ot_ref, b_ref, t_ref, o_ref):
        acc = ob_ref[0:SMALL_ROW0, :].astype(F32)
        tail = ot_ref[...]
        for k in range(N_CHIPS - 1):
            acc = acc + b_ref[k, 0:SMALL_ROW0, :].astype(F32)
            tail = tail + t_ref[k]
        o_ref[0:SMALL_ROW0, :] = acc
        o_ref[SMALL_ROW0:, :] = tail

    return pl.pallas_call(
        body, name="grad_chip_sum", out_shape=jax.ShapeDtypeStruct((HALF_ROWS, LANES), F32),
        compiler_params=_params(),
    )(own_bf, own_tail, others_bf, others_tail)


def _swap_reduced_halves(gh):
    def body(g_ref, o_ref, send_sem, recv_sem):
        x, y, c = _mesh_pos()
        cp = pltpu.make_async_remote_copy(g_ref, o_ref, send_sem, recv_sem, device_id=(x, y, 1 - c), device_id_type=MESH)
        cp.start()
        cp.wait()

    return pl.pallas_call(
        body, name="grad_swap_halves", in_specs=[ANY], out_specs=ANY,
        out_shape=jax.ShapeDtypeStruct(gh.shape, gh.dtype),
        scratch_shapes=[pltpu.SemaphoreType.DMA, pltpu.SemaphoreType.DMA],
        compiler_params=_comm_params(),
    )(gh)


def _gather_small(second_half):
    def body(gs_ref, sm_ref, send_sems, recv_sems, loc_sem):
        x, y, c = _mesh_pos()
        me = 2 * x + y
        small = gs_ref.at[pl.ds(SMALL_ROW0, SMALL_PIECE_ROWS)]
        loc = pltpu.make_async_copy(small, sm_ref.at[me], loc_sem)
        loc.start()
        cps = [pltpu.make_async_remote_copy(small, sm_ref.at[me], send_sems.at[k], recv_sems.at[k],
                                            device_id=(px, py, c), device_id_type=MESH)
               for k, (px, py) in enumerate(_other_chips(x, y))]
        for cp in cps:
            cp.start()
        for cp in cps:
            cp.wait()
        loc.wait()

    return pl.pallas_call(
        body, name="grad_gather_small", in_specs=[ANY], out_specs=ANY,
        out_shape=jax.ShapeDtypeStruct((N_CHIPS, SMALL_PIECE_ROWS, LANES), second_half.dtype),
        scratch_shapes=[pltpu.SemaphoreType.DMA((3,)), pltpu.SemaphoreType.DMA((3,)), pltpu.SemaphoreType.DMA],
        compiler_params=_comm_params(),
    )(second_half)


def _pack_grads(grads):
    small = jnp.concatenate([grads[n].reshape(-1) for n in SMALL])
    small = jnp.pad(small, (0, N_CHIPS * SMALL_PIECE_ROWS * LANES - SMALL_TOTAL)).reshape(N_CHIPS, SMALL_PIECE_ROWS, LANES)
    pieces = []
    for s in range(N_CHIPS):
        rows = []
        for n, (shp, axis) in BIG.items():
            g = grads[n]
            blk = g[s * shp[0]:(s + 1) * shp[0], :] if axis == 0 else g[:, s * shp[1]:(s + 1) * shp[1]]
            rows.append(blk.reshape(-1, LANES))
        rows.append(small[s])
        pieces.append(jnp.concatenate(rows, axis=0).reshape(2, HALF_ROWS, LANES))
    return jnp.stack(pieces)


def _unpack_shard(first_half, second_half):
    rows = jnp.concatenate([first_half, second_half], axis=0)
    out, r0 = {}, 0
    for n, (shp, _) in BIG.items():
        out[n] = rows[r0:r0 + BIG_ROWS[n]].reshape(shp)
        r0 += BIG_ROWS[n]
    return out


def _unpack_small(sm):
    flat = sm.reshape(-1)
    out, o = {}, 0
    for n, shp in SMALL.items():
        k = math.prod(shp)
        out[n] = flat[o:o + k].reshape(shp)
        o += k
    return out


def _as_2d(a):
    n = a.size
    if a.ndim >= 2 and a.shape[-1] > 1024:
        return a.reshape(-1, a.shape[-1])
    if n % 1024 == 0:
        return a.reshape(n // 1024, 1024)
    return a.reshape(1, n)


def _adamw(w, g, m, v, name):
    shape = w.shape
    w2, g2, m2, v2 = (_as_2d(a) for a in (w, g, m, v))
    rows, cols = w2.shape
    rb = 256 if rows >= 512 else rows
    by_cols = rows % rb != 0

    def body(w_ref, g_ref, m_ref, v_ref, d_ref, mo_ref, vo_ref):
        gv = g_ref[...]
        mn = ADAM_B1 * m_ref[...] + (1.0 - ADAM_B1) * gv
        vn = ADAM_B2 * v_ref[...] + (1.0 - ADAM_B2) * (gv * gv)
        m_hat = mn / (1.0 - ADAM_B1 ** ADAM_STEP)
        v_hat = vn / (1.0 - ADAM_B2 ** ADAM_STEP)
        d_ref[...] = -ADAM_LR * (m_hat / (jnp.sqrt(v_hat) + ADAM_EPS) + ADAM_WD * w_ref[...])
        mo_ref[...] = mn
        vo_ref[...] = vn

    spec = pl.BlockSpec((rows, 256), lambda i: (0, i)) if by_cols else _row_spec(rb, cols)
    sds = jax.ShapeDtypeStruct((rows, cols), F32)
    d, mo, vo = pl.pallas_call(
        body, name=name, grid=(cols // 256 if by_cols else rows // rb,), in_specs=[spec] * 4, out_specs=[spec] * 3, out_shape=[sds] * 3,
        compiler_params=_params(("parallel",)),
    )(w2, g2, m2, v2)
    return d.reshape(shape), mo.reshape(shape), vo.reshape(shape)


def kernel(x, p, norm_w, w_in, s5_A_re, s5_A_im, s5_log_dt, s5_B_re, s5_B_im, s5_C_re, s5_C_im, s5_D, s5_w_glu, s5_b_glu, conv_w, conv_b, dt_bias, A_log, ssd_D, ssd_norm_w, w_out, ple_norm_w, w_ple_gate, w_ple_proj, final_norm_w, loss_target, m_norm_w, m_w_in, m_s5_A_re, m_s5_A_im, m_s5_log_dt, m_s5_B_re, m_s5_B_im, m_s5_C_re, m_s5_C_im, m_s5_D, m_s5_w_glu, m_s5_b_glu, m_conv_w, m_conv_b, m_dt_bias, m_A_log, m_ssd_D, m_ssd_norm_w, m_w_out, m_ple_norm_w, m_w_ple_gate, m_w_ple_proj, m_final_norm_w, v_norm_w, v_w_in, v_s5_A_re, v_s5_A_im, v_s5_log_dt, v_s5_B_re, v_s5_B_im, v_s5_C_re, v_s5_C_im, v_s5_D, v_s5_w_glu, v_s5_b_glu, v_conv_w, v_conv_b, v_dt_bias, v_A_log, v_ssd_D, v_ssd_norm_w, v_w_out, v_ple_norm_w, v_w_ple_gate, v_w_ple_proj, v_final_norm_w):
    given = (norm_w, w_in, s5_A_re, s5_A_im, s5_log_dt, s5_B_re, s5_B_im, s5_C_re, s5_C_im, s5_D, s5_w_glu, s5_b_glu, conv_w, conv_b,
             dt_bias, A_log, ssd_D, ssd_norm_w, w_out, ple_norm_w, w_ple_gate, w_ple_proj, final_norm_w)
    given_m = (m_norm_w, m_w_in, m_s5_A_re, m_s5_A_im, m_s5_log_dt, m_s5_B_re, m_s5_B_im, m_s5_C_re, m_s5_C_im, m_s5_D, m_s5_w_glu,
               m_s5_b_glu, m_conv_w, m_conv_b, m_dt_bias, m_A_log, m_ssd_D, m_ssd_norm_w, m_w_out, m_ple_norm_w, m_w_ple_gate,
               m_w_ple_proj, m_final_norm_w)
    given_v = (v_norm_w, v_w_in, v_s5_A_re, v_s5_A_im, v_s5_log_dt, v_s5_B_re, v_s5_B_im, v_s5_C_re, v_s5_C_im, v_s5_D, v_s5_w_glu,
               v_s5_b_glu, v_conv_w, v_conv_b, v_dt_bias, v_A_log, v_ssd_D, v_ssd_norm_w, v_w_out, v_ple_norm_w, v_w_ple_gate,
               v_w_ple_proj, v_final_norm_w)
    wts, mom, var = dict(zip(WEIGHTS, given)), dict(zip(WEIGHTS, given_m)), dict(zip(WEIGHTS, given_v))
    drop = lambda n, a: a if n == "final_norm_w" else a[0]

    shard2d = lambda n, a: a[0].T if n == "w_in" else drop(n, a)
    wpack = jnp.concatenate([shard2d(n, wts[n]).astype(BF).reshape(-1, LANES) for n in BIG], axis=0)
    wall, cwall = _all_gather_chips(wpack, drop("conv_w", wts["conv_w"]))
    chip = 2 * lax.axis_index("x") + lax.axis_index("y")
    is_own = (lax.broadcasted_iota(jnp.int32, (N_CHIPS, 1, 1), 0) == chip)
    full, r0 = {}, 0
    for n, (shp, axis) in BIG.items():
        blk = jnp.where(is_own, shard2d(n, wts[n]).astype(BF)[None], wall[:, r0:r0 + BIG_ROWS[n]].reshape((N_CHIPS,) + shp))
        full[n] = blk.reshape(N_CHIPS * shp[0], shp[1]) if axis == 0 else blk.transpose(1, 0, 2).reshape(shp[0], N_CHIPS * shp[1])
        r0 += BIG_ROWS[n]
    for n in SMALL:
        full[n] = drop(n, wts[n])
    full["conv_w"] = cwall.transpose(1, 0, 2).reshape(4, 2048)

    loss, gx, grads = _local_step(x[0], p[0, 0], loss_target[0], full)
    loss = lax.psum(loss, MESH_AXES)

    gp = _pack_grads({n: grads[n].reshape(SMALL[n]) if n in SMALL else grads[n] for n in WEIGHTS})
    c = lax.axis_index("c")
    from_sibling = _exchange_pair(gp)
    mine = lax.dynamic_index_in_dim(gp, c, axis=1, keepdims=False)
    ps_bf, ps_tail = _pair_sum(mine, from_sibling)
    others_bf, others_tail = _exchange_chips(ps_bf, ps_tail)
    own_bf = lax.dynamic_index_in_dim(ps_bf, chip, axis=0, keepdims=False)
    own_tail = lax.dynamic_index_in_dim(ps_tail, chip, axis=0, keepdims=False)
    reduced_half = _chip_sum(own_bf, own_tail, others_bf, others_tail)
    sibling_half = _swap_reduced_halves(reduced_half)
    first_half = jnp.where(c == 0, reduced_half, sibling_half)
    second_half = jnp.where(c == 0, sibling_half, reduced_half)
    sm = _gather_small(second_half)
    g_final = {**_unpack_small(sm), **_unpack_shard(first_half, second_half)}
    g_final["conv_w"] = lax.dynamic_slice_in_dim(g_final["conv_w"], chip * 512, 512, axis=1)

    outs_g, outs_d, outs_m, outs_v = [], [], [], []
    for n in WEIGHTS:
        if n == "w_in":
            res = _adamw(wts[n][0].T, g_final[n], mom[n][0].T, var[n][0].T, "adamw_" + n)
            g, d, mo, vo = (a.T[None] for a in (g_final[n],) + res)
        else:
            g = g_final[n].reshape(wts[n].shape)
            d, mo, vo = _adamw(wts[n], g, mom[n], var[n], "adamw_" + n)
        outs_g.append(g)
        outs_d.append(d)
        outs_m.append(mo)
        outs_v.append(vo)
    return (loss, gx[None], *outs_g, *outs_d, *outs_m, *outs_v)
```

```python
import functools
import math

import jax
import jax.numpy as jnp
from jax import lax
from jax.experimental import pallas as pl
from jax.experimental.pallas import tpu as pltpu

F32 = jnp.float32
BF = jnp.bfloat16
EPS = 1e-6
CHUNK = 64
D_MODEL = 1024
S5_GROUPS = 64
S5_CH = 16
S5_STATE = 64
SSD_HEADS = 16
SSD_HEAD_DIM = 64
SSD_GROUPS = 4
SSD_STATE = 128
D_MAIN = 5120
LANES = 128
TOKEN_TILE = 256
VMEM_LIMIT = 56 * 1024 * 1024
MESH_AXES = ("x", "y", "c")
N_CHIPS = 4
ADAM_LR, ADAM_B1, ADAM_B2, ADAM_EPS, ADAM_WD, ADAM_STEP = 0.001, 0.9, 0.999, 1e-08, 0.01, 10
MESH = pl.DeviceIdType.MESH
ANY = pl.BlockSpec(memory_space=pl.ANY)


def _dot(a, b):
    return jnp.dot(a, b, preferred_element_type=F32)


def _dot_nt(a, b):
    return lax.dot_general(a, b, (((1,), (1,)), ((), ())), preferred_element_type=F32)


def _dot_tn(a, b):
    return lax.dot_general(a, b, (((0,), (0,)), ((), ())), preferred_element_type=F32)


def _sigmoid(x):
    return 1.0 / (1.0 + jnp.exp(-x))


def _softplus(x):
    return jnp.maximum(x, 0.0) + jnp.log(1.0 + jnp.exp(-jnp.abs(x)))


_GELU_C = math.sqrt(2.0 / math.pi)


def _gelu(x):
    return 0.5 * x * (1.0 + jnp.tanh(_GELU_C * (x + 0.044715 * x * x * x)))


def _gelu_grad(x):
    th = jnp.tanh(_GELU_C * (x + 0.044715 * x * x * x))
    return 0.5 * (1.0 + th) + 0.5 * x * (1.0 - th * th) * _GELU_C * (1.0 + 3.0 * 0.044715 * x * x)


def _params(sem=None):
    return pltpu.CompilerParams(dimension_semantics=sem, vmem_limit_bytes=VMEM_LIMIT)


def _row_spec(tl, width, col=0):
    return pl.BlockSpec((tl, width), lambda i, col=col: (i, col))


def _const_spec(shape):
    nd = len(shape)
    return pl.BlockSpec(shape, lambda *_: (0,) * nd)


def _in_proj_fwd(x, norm_w, w_main, w_dt):
    L = x.shape[0]
    tl = min(TOKEN_TILE, L)

    def body(x_ref, nw_ref, wm_ref, wd_ref, hn_ref, ps5_ref, pssd_ref, pd_ref):
        xv = x_ref[...]
        r = lax.rsqrt(jnp.mean(xv * xv, axis=-1, keepdims=True) + EPS)
        hn = (xv * r * nw_ref[...]).astype(BF)
        hn_ref[...] = hn
        for j in range(2):
            ps5_ref[:, j * 1024:(j + 1) * 1024] = _dot_nt(hn, wm_ref[j * 1024:(j + 1) * 1024, :]).astype(BF)
        for j in range(3):
            pssd_ref[:, j * 1024:(j + 1) * 1024] = _dot_nt(hn, wm_ref[(j + 2) * 1024:(j + 3) * 1024, :])
        pd_ref[...] = _dot_nt(hn, wd_ref[...])

    return pl.pallas_call(
        body, name="in_proj_fwd", grid=(L // tl,),
        in_specs=[_row_spec(tl, D_MODEL), _const_spec((1, D_MODEL)), _const_spec((D_MAIN, D_MODEL)), _const_spec((LANES, D_MODEL))],
        out_specs=[_row_spec(tl, D_MODEL), _row_spec(tl, 2048), _row_spec(tl, 3072), _row_spec(tl, LANES)],
        out_shape=[jax.ShapeDtypeStruct((L, D_MODEL), BF), jax.ShapeDtypeStruct((L, 2048), BF), jax.ShapeDtypeStruct((L, 3072), F32),
                   jax.ShapeDtypeStruct((L, LANES), F32)],
        compiler_params=_params(("arbitrary",)),
    )(x, norm_w, w_main, w_dt)


def _in_proj_bwd(x, norm_w, dh1, du_flat, dyssm, s5_d, dzs, dzd, dxbc, ddt, w_main, w_dt):
    L = x.shape[0]
    tl = min(TOKEN_TILE, L)

    def body(x_ref, nw_ref, dh1_ref, duf_ref, dys_ref, d_ref, dzs_ref, dzd_ref, dxbc_ref, ddt_ref, wm_ref, wd_ref,
             gx_ref, du_ref, gnw_ref):
        @pl.when(pl.program_id(0) == 0)
        def _():
            gnw_ref[...] = jnp.zeros_like(gnw_ref)

        du = (duf_ref[...].astype(F32) + dys_ref[...].astype(F32) * d_ref[...]).astype(BF)
        du_ref[...] = du
        dhn = _dot(du, wm_ref[0:1024, :])
        dhn += _dot(dzs_ref[...], wm_ref[1024:2048, :])
        dhn += _dot(dzd_ref[...], wm_ref[2048:3072, :])
        dhn += _dot(dxbc_ref[...], wm_ref[3072:5120, :])
        dhn += _dot(ddt_ref[...].astype(BF), wd_ref[...])
        xv = x_ref[...]
        r = lax.rsqrt(jnp.mean(xv * xv, axis=-1, keepdims=True) + EPS)
        xh = xv * r
        gnw_ref[...] += jnp.sum(dhn * xh, axis=0, keepdims=True)
        g = dhn * nw_ref[...]
        gx_ref[...] = dh1_ref[...] + r * (g - xh * jnp.mean(g * xh, axis=-1, keepdims=True))

    return pl.pallas_call(
        body, name="in_proj_bwd", grid=(L // tl,),
        in_specs=[_row_spec(tl, D_MODEL), _const_spec((1, D_MODEL)), _row_spec(tl, D_MODEL), _row_spec(tl, D_MODEL),
                  _row_spec(tl, D_MODEL), _const_spec((1, D_MODEL)), _row_spec(tl, D_MODEL), _row_spec(tl, D_MODEL),
                  _row_spec(tl, 2048), _row_spec(tl, LANES), _const_spec((D_MAIN, D_MODEL)), _const_spec((LANES, D_MODEL))],
        out_specs=[_row_spec(tl, D_MODEL), _row_spec(tl, D_MODEL), _const_spec((1, D_MODEL))],
        out_shape=[jax.ShapeDtypeStruct((L, D_MODEL), F32), jax.ShapeDtypeStruct((L, D_MODEL), BF), jax.ShapeDtypeStruct((1, D_MODEL), F32)],
        compiler_params=_params(("arbitrary",)),
    )(x, norm_w, dh1, du_flat, dyssm, s5_d, dzs, dzd, dxbc, ddt, w_main, w_dt)


def _matmul_tn(a, b, name):
    L, M = a.shape
    N = b.shape[1]
    tm, tn, tk = min(M, 1024), min(N, 1024), min(L, 1024)

    def body(a_ref, b_ref, o_ref):
        @pl.when(pl.program_id(2) == 0)
        def _():
            o_ref[...] = jnp.zeros_like(o_ref)

        o_ref[...] += _dot_tn(a_ref[...].astype(BF), b_ref[...].astype(BF))

    return pl.pallas_call(
        body, name=name, grid=(M // tm, N // tn, L // tk),
        in_specs=[pl.BlockSpec((tk, tm), lambda i, j, k: (k, i)), pl.BlockSpec((tk, tn), lambda i, j, k: (k, j))],
        out_specs=pl.BlockSpec((tm, tn), lambda i, j, k: (i, j)),
        out_shape=jax.ShapeDtypeStruct((M, N), F32),
        compiler_params=_params(("parallel", "parallel", "arbitrary")),
    )(a, b)


def _s5_discretise(a_re, a_im, log_dt, b_re, b_im, c_re, c_im):
    dt = jnp.exp(log_dt)[:, None]
    tau = jnp.arange(CHUNK + 1, dtype=F32)
    mag = jnp.exp((a_re * dt)[:, :, None] * tau)
    ang = (a_im * dt)[:, :, None] * tau
    pw_re, pw_im = mag * jnp.cos(ang), mag * jnp.sin(ang)
    er, ei = pw_re[:, :, 1] - 1.0, pw_im[:, :, 1]
    den = a_re * a_re + a_im * a_im
    beta_re, beta_im = (er * a_re + ei * a_im) / den, (ei * a_re - er * a_im) / den
    bb_re = (beta_re[:, :, None] * b_re - beta_im[:, :, None] * b_im).transpose(0, 2, 1)
    bb_im = (beta_re[:, :, None] * b_im + beta_im[:, :, None] * b_re).transpose(0, 2, 1)
    return bb_re, bb_im, c_re, c_im, pw_re, pw_im


def _s5_kmat(bb_re, bb_im, c_re, c_im, pw_re, pw_im):
    hi = lax.Precision.HIGHEST
    m_re = bb_re[:, :, None, :] * c_re[:, None, :, :] - bb_im[:, :, None, :] * c_im[:, None, :, :]
    m_im = bb_re[:, :, None, :] * c_im[:, None, :, :] + bb_im[:, :, None, :] * c_re[:, None, :, :]
    k4 = (jnp.einsum("ghkn,gnt->ghtk", m_re, pw_re[:, :, :CHUNK], precision=hi)
          - jnp.einsum("ghkn,gnt->ghtk", m_im, pw_im[:, :, :CHUNK], precision=hi))
    return k4.reshape(S5_GROUPS, S5_CH, CHUNK * S5_CH)


def _s5_table_factors(bb_re, bb_im, c_re, c_im, pw_re, pw_im):
    pr = pw_re[:, :, CHUNK - 1::-1].transpose(0, 2, 1)
    pi = pw_im[:, :, CHUNK - 1::-1].transpose(0, 2, 1)
    bbp = jnp.concatenate([bb_re, bb_im], axis=-1)
    pwr = jnp.concatenate([pr, pi], axis=-1)
    cnp = jnp.concatenate([c_re, c_im], axis=-1)
    pw1 = jnp.concatenate([pw_re[:, :, 1:CHUNK + 1].transpose(0, 2, 1), pw_im[:, :, 1:CHUNK + 1].transpose(0, 2, 1)], axis=-1)
    return bbp, pwr, cnp, pw1


def _s5_small_cotangents(r12, q12, xy, uv, da64):
    n = S5_STATE
    fold = lambda a: a[..., :n] + a[..., n:]
    fold_m = lambda a: a[..., n:] - a[..., :n]
    dbb_re, dbb_im = fold(r12[:, :S5_CH]), fold_m(r12[:, S5_CH:])
    dpr, dpi = fold(q12[:, :CHUNK]), fold_m(q12[:, CHUNK:])
    dc_re, dc_im = -fold_m(xy[:, :S5_CH]), -fold(xy[:, S5_CH:])
    dp1_re, dp1_im = -fold_m(uv[:, :CHUNK]), -fold(uv[:, CHUNK:])
    zero = jnp.zeros((S5_GROUPS, n, 1), F32)
    dpw_re = (jnp.concatenate([dpr[:, ::-1].transpose(0, 2, 1), zero], axis=-1)
              + jnp.concatenate([zero, dp1_re.transpose(0, 2, 1)], axis=-1)).at[:, :, CHUNK].add(da64[:, :n])
    dpw_im = (jnp.concatenate([dpi[:, ::-1].transpose(0, 2, 1), zero], axis=-1)
              + jnp.concatenate([zero, dp1_im.transpose(0, 2, 1)], axis=-1)).at[:, :, CHUNK].add(da64[:, n:])
    return dbb_re, dbb_im, dc_re, dc_im, dpw_re, dpw_im


def _s5_scan_powers(a_re, a_im, log_dt, nsteps):
    dt = jnp.exp(log_dt)[:, None]
    steps = (CHUNK * (2.0 ** jnp.arange(8, dtype=F32)))[None, :, None]
    mag = jnp.exp((a_re * dt)[:, None, :] * steps)
    ang = (a_im * dt)[:, None, :] * steps
    re, im = mag * jnp.cos(ang), mag * jnp.sin(ang)
    del nsteps
    return jnp.concatenate([re, re], -1), jnp.concatenate([-im, im], -1)


def _build_toeplitz(kmat_ref, tg_ref):
    lane = lax.broadcasted_iota(jnp.int32, (CHUNK, CHUNK * S5_CH), 1)
    srow = lax.broadcasted_iota(jnp.int32, (CHUNK, CHUNK * S5_CH), 0)
    keep = lane >= S5_CH * srow
    for h in range(S5_CH):
        row = jnp.broadcast_to(kmat_ref[0, h:h + 1, :], (CHUNK, CHUNK * S5_CH))
        rolled = pltpu.roll(row, 0, 1, stride=S5_CH, stride_axis=0)
        tg_ref[h * CHUNK:(h + 1) * CHUNK, :] = jnp.where(keep, rolled, 0.0).astype(BF)


def _swap_halves(x):
    return pltpu.roll(x, S5_STATE, 1)


def _build_state_tables(bbp_ref, pwr_ref, cnp_ref, pw1_ref, wst_ref, wofft_ref):
    lane = lax.broadcasted_iota(jnp.int32, (1, 2 * S5_STATE), 1)
    left = lane < S5_STATE
    pwr = pwr_ref[0]
    pwr_sw = _swap_halves(pwr)
    for hh in range(S5_CH):
        bb = bbp_ref[0, hh:hh + 1, :]
        bb_sw = _swap_halves(bb)
        wst_ref[hh * CHUNK:(hh + 1) * CHUNK, :] = (jnp.where(left, bb, bb_sw) * pwr
                                                   + jnp.where(left, -bb_sw, bb) * pwr_sw).astype(BF)
    cn = cnp_ref[0]
    cn_sw = _swap_halves(cn)
    c_a = jnp.where(left, cn, -cn_sw)
    c_b = jnp.where(left, -cn_sw, -cn)
    for t in range(CHUNK):
        p = pw1_ref[0, t:t + 1, :]
        wofft_ref[t * S5_CH:(t + 1) * S5_CH, :] = (c_a * p + c_b * _swap_halves(p)).astype(BF)


def _s5_core_fwd(uflat, kmat, bbp, pwr, cnp, pw1, p1, p2):
    G, nc, W = uflat.shape
    nsteps = max(1, (nc - 1).bit_length())

    def body(u_ref, k_ref, bbp_ref, pwr_ref, cnp_ref, pw1_ref, p1_ref, p2_ref, y_ref, h_ref, tg_ref, wst_ref, wofft_ref):
        _build_toeplitz(k_ref, tg_ref)
        _build_state_tables(bbp_ref, pwr_ref, cnp_ref, pw1_ref, wst_ref, wofft_ref)
        u = u_ref[0]
        x = _dot(u, wst_ref[...])
        row = lax.broadcasted_iota(jnp.int32, x.shape, 0)
        d = 1
        for k in range(nsteps):
            sh = jnp.where(row >= d, pltpu.roll(x, d, 0), 0.0)
            x = x + p1_ref[0, k:k + 1, :] * sh + p2_ref[0, k:k + 1, :] * _swap_halves(sh)
            d *= 2
        h = jnp.where(row >= 1, pltpu.roll(x, 1, 0), 0.0)
        h_ref[0] = h
        y = _dot(u, tg_ref[...]) + _dot_nt(h.astype(BF), wofft_ref[...])
        y_ref[0] = y.astype(BF)

    spec_g = lambda a, b: pl.BlockSpec((1, a, b), lambda g: (g, 0, 0))
    return pl.pallas_call(
        body, name="s5_core_fwd", grid=(G,),
        in_specs=[spec_g(nc, W), spec_g(S5_CH, W), spec_g(S5_CH, 2 * S5_STATE), spec_g(CHUNK, 2 * S5_STATE),
                  spec_g(S5_CH, 2 * S5_STATE), spec_g(CHUNK, 2 * S5_STATE), spec_g(8, 2 * S5_STATE), spec_g(8, 2 * S5_STATE)],
        out_specs=[spec_g(nc, W), spec_g(nc, 2 * S5_STATE)],
        out_shape=[jax.ShapeDtypeStruct((G, nc, W), BF), jax.ShapeDtypeStruct((G, nc, 2 * S5_STATE), F32)],
        scratch_shapes=[pltpu.VMEM((W, W), BF), pltpu.VMEM((W, 2 * S5_STATE), BF), pltpu.VMEM((W, 2 * S5_STATE), BF)],
        compiler_params=_params(("arbitrary",)),
    )(uflat, kmat, bbp, pwr, cnp, pw1, p1, p2)


def _s5_core_bwd(uflat, dyflat, hsave, kmat, bbp, pwr, cnp, pw1, p1, p2):
    G, nc, W = uflat.shape
    nsteps = max(1, (nc - 1).bit_length())

    def body(u_ref, dy_ref, h_ref, k_ref, bbp_ref, pwr_ref, cnp_ref, pw1_ref, p1_ref, p2_ref,
             du_ref, dk_ref, r12_ref, q12_ref, xy_ref, uv_ref, da_ref, tg_ref, flip_ref, wst_ref, wofft_ref):
        @pl.when(pl.program_id(0) == 0)
        def _():
            r = lax.broadcasted_iota(jnp.int32, (W, W), 0)
            c = lax.broadcasted_iota(jnp.int32, (W, W), 1)
            flip_ref[...] = (((r >> 6) == (c >> 6)) & ((r & (CHUNK - 1)) + (c & (CHUNK - 1)) == CHUNK - 1)).astype(BF)

        _build_toeplitz(k_ref, tg_ref)
        _build_state_tables(bbp_ref, pwr_ref, cnp_ref, pw1_ref, wst_ref, wofft_ref)
        u = u_ref[0]
        dy = dy_ref[0]
        h = h_ref[0]
        gh = _dot(dy, wofft_ref[...])
        row = lax.broadcasted_iota(jnp.int32, gh.shape, 0)
        x = jnp.where(row < nc - 1, pltpu.roll(gh, nc - 1, 0), 0.0)
        d = 1
        for k in range(nsteps):
            sh = jnp.where(row < nc - d, pltpu.roll(x, nc - d, 0), 0.0)
            x = x + p1_ref[0, k:k + 1, :] * sh - p2_ref[0, k:k + 1, :] * _swap_halves(sh)
            d *= 2
        gs = x.astype(BF)
        du_ref[0] = (_dot_nt(dy, tg_ref[...]) + _dot_nt(gs, wst_ref[...])).astype(BF)

        dwst = _dot_tn(u, gs)
        pwr = pwr_ref[0]
        pwr_sw = _swap_halves(pwr)
        q1 = jnp.zeros((CHUNK, 2 * S5_STATE), F32)
        q2 = jnp.zeros((CHUNK, 2 * S5_STATE), F32)
        r1_rows, r2_rows = [], []
        for hh in range(S5_CH):
            blk = dwst[hh * CHUNK:(hh + 1) * CHUNK, :]
            r1_rows.append(jnp.sum(blk * pwr, axis=0, keepdims=True))
            r2_rows.append(jnp.sum(blk * pwr_sw, axis=0, keepdims=True))
            bb = bbp_ref[0, hh:hh + 1, :]
            q1 = q1 + blk * bb
            q2 = q2 + blk * _swap_halves(bb)
        r12_ref[0] = jnp.concatenate(r1_rows + r2_rows, axis=0)
        q12_ref[0] = jnp.concatenate([q1, q2], axis=0)

        dwofft = _dot_tn(dy, h.astype(BF))
        cn = cnp_ref[0]
        cn_sw = _swap_halves(cn)
        xa = jnp.zeros((S5_CH, 2 * S5_STATE), F32)
        ya = jnp.zeros((S5_CH, 2 * S5_STATE), F32)
        ru_rows, rv_rows = [], []
        for t in range(CHUNK):
            blk = dwofft[t * S5_CH:(t + 1) * S5_CH, :]
            p = pw1_ref[0, t:t + 1, :]
            xa = xa + blk * p
            ya = ya + blk * _swap_halves(p)
            ru_rows.append(jnp.sum(blk * cn, axis=0, keepdims=True))
            rv_rows.append(jnp.sum(blk * cn_sw, axis=0, keepdims=True))
        xy_ref[0] = jnp.concatenate([xa, ya], axis=0)
        uv_ref[0] = jnp.concatenate(ru_rows + rv_rows, axis=0)
        r1 = jnp.sum(x * h, axis=0, keepdims=True)
        r2 = jnp.sum(x * _swap_halves(h), axis=0, keepdims=True)
        da_ref[0] = jnp.concatenate([r1, r2, jnp.zeros((6, 2 * S5_STATE), F32)], axis=0)
        lane = lax.broadcasted_iota(jnp.int32, (CHUNK, W), 1)
        srow = lax.broadcasted_iota(jnp.int32, (CHUNK, W), 0)
        keep = lane < S5_CH * (srow + 1)
        ur = _dot(u, flip_ref[...]).astype(BF)
        for hh in range(S5_CH):
            dt_h = _dot_tn(ur[:, hh * CHUNK:(hh + 1) * CHUNK], dy)
            back = pltpu.roll(dt_h, S5_CH, 1, stride=S5_CH, stride_axis=0)
            dk_ref[0, hh:hh + 1, :] = jnp.sum(jnp.where(keep, back, 0.0), axis=0, keepdims=True)

    spec_g = lambda a, b: pl.BlockSpec((1, a, b), lambda g: (g, 0, 0))
    return pl.pallas_call(
        body, name="s5_core_bwd", grid=(G,),
        in_specs=[spec_g(nc, W), spec_g(nc, W), spec_g(nc, 2 * S5_STATE), spec_g(S5_CH, W), spec_g(S5_CH, 2 * S5_STATE),
                  spec_g(CHUNK, 2 * S5_STATE), spec_g(S5_CH, 2 * S5_STATE), spec_g(CHUNK, 2 * S5_STATE),
                  spec_g(8, 2 * S5_STATE), spec_g(8, 2 * S5_STATE)],
        out_specs=[spec_g(nc, W), spec_g(S5_CH, W), spec_g(2 * S5_CH, 2 * S5_STATE), spec_g(2 * CHUNK, 2 * S5_STATE),
                   spec_g(2 * S5_CH, 2 * S5_STATE), spec_g(2 * CHUNK, 2 * S5_STATE), spec_g(8, 2 * S5_STATE)],
        out_shape=[jax.ShapeDtypeStruct((G, nc, W), BF), jax.ShapeDtypeStruct((G, S5_CH, W), F32),
                   jax.ShapeDtypeStruct((G, 2 * S5_CH, 2 * S5_STATE), F32), jax.ShapeDtypeStruct((G, 2 * CHUNK, 2 * S5_STATE), F32),
                   jax.ShapeDtypeStruct((G, 2 * S5_CH, 2 * S5_STATE), F32), jax.ShapeDtypeStruct((G, 2 * CHUNK, 2 * S5_STATE), F32),
                   jax.ShapeDtypeStruct((G, 8, 2 * S5_STATE), F32)],
        scratch_shapes=[pltpu.VMEM((W, W), BF), pltpu.VMEM((W, W), BF), pltpu.VMEM((W, 2 * S5_STATE), BF),
                        pltpu.VMEM((W, 2 * S5_STATE), BF)],
        compiler_params=_params(("arbitrary",)),
    )(uflat, dyflat, hsave, kmat, bbp, pwr, cnp, pw1, p1, p2)


def _flat_hs(a, nc):
    return a.reshape(nc, CHUNK, S5_GROUPS, S5_CH).transpose(2, 0, 3, 1).reshape(S5_GROUPS, nc, CHUNK * S5_CH)


def _unflat_hs(a, nc):
    return a.reshape(S5_GROUPS, nc, S5_CH, CHUNK).transpose(1, 3, 0, 2).reshape(nc * CHUNK, D_MODEL)


def _flat_tk(a, nc):
    return a.reshape(nc, CHUNK, S5_GROUPS, S5_CH).transpose(2, 0, 1, 3).reshape(S5_GROUPS, nc, CHUNK * S5_CH)


def _unflat_tk(a, nc):
    return a.reshape(S5_GROUPS, nc, CHUNK, S5_CH).transpose(1, 2, 0, 3).reshape(nc * CHUNK, D_MODEL)


def _s5_post_fwd(yssm, proj, s5_d, w_glu, b_glu):
    L = yssm.shape[0]
    tl = min(TOKEN_TILE, L)

    def body(ys_ref, u_ref, z_ref, d_ref, wg_ref, bg_ref, o_ref):
        u = u_ref[...].astype(F32)
        a = _gelu(ys_ref[...].astype(F32) + d_ref[...] * u)
        y = a * _sigmoid(_dot(a.astype(BF), wg_ref[...]) + bg_ref[...])
        z = z_ref[...].astype(F32)
        o_ref[...] = (y * z * _sigmoid(z)).astype(BF)

    return pl.pallas_call(
        body, name="s5_post_fwd", grid=(L // tl,),
        in_specs=[_row_spec(tl, D_MODEL), _row_spec(tl, D_MODEL, 0), _row_spec(tl, D_MODEL, 1), _const_spec((1, D_MODEL)),
                  _const_spec((D_MODEL, D_MODEL)), _const_spec((1, D_MODEL))],
        out_specs=_row_spec(tl, D_MODEL),
        out_shape=jax.ShapeDtypeStruct((L, D_MODEL), BF),
        compiler_params=_params(("arbitrary",)),
    )(yssm, proj, proj, s5_d, w_glu, b_glu)


def _s5_post_bwd(dys5, yssm, proj, s5_d, w_glu, b_glu):
    L = yssm.shape[0]
    tl = min(TOKEN_TILE, L)

    def body(dy_ref, ys_ref, u_ref, z_ref, d_ref, wg_ref, bg_ref, dz_ref, dys_ref, a_ref, dgl_ref, dbg_ref, dd_ref):
        @pl.when(pl.program_id(0) == 0)
        def _():
            dbg_ref[...] = jnp.zeros_like(dbg_ref)
            dd_ref[...] = jnp.zeros_like(dd_ref)

        u = u_ref[...].astype(F32)
        y0 = ys_ref[...].astype(F32) + d_ref[...] * u
        a = _gelu(y0)
        a_bf = a.astype(BF)
        sg = _sigmoid(_dot(a_bf, wg_ref[...]) + bg_ref[...])
        y = a * sg
        z = z_ref[...].astype(F32)
        sz = _sigmoid(z)
        dout = dy_ref[...].astype(F32)
        dz_ref[...] = (dout * y * sz * (1.0 + z * (1.0 - sz))).astype(BF)
        dyv = dout * z * sz
        dgl = dyv * a * sg * (1.0 - sg)
        dgl_bf = dgl.astype(BF)
        da = dyv * sg + _dot_nt(dgl_bf, wg_ref[...])
        dy0 = da * _gelu_grad(y0)
        dbg_ref[...] += jnp.sum(dgl, axis=0, keepdims=True)
        dd_ref[...] += jnp.sum(dy0 * u, axis=0, keepdims=True)
        dys_ref[...] = dy0.astype(BF)
        a_ref[...] = a_bf
        dgl_ref[...] = dgl_bf

    big = jax.ShapeDtypeStruct((L, D_MODEL), BF)
    vec = jax.ShapeDtypeStruct((1, D_MODEL), F32)
    return pl.pallas_call(
        body, name="s5_post_bwd", grid=(L // tl,),
        in_specs=[_row_spec(tl, D_MODEL), _row_spec(tl, D_MODEL), _row_spec(tl, D_MODEL, 0), _row_spec(tl, D_MODEL, 1),
                  _const_spec((1, D_MODEL)), _const_spec((D_MODEL, D_MODEL)), _const_spec((1, D_MODEL))],
        out_specs=[_row_spec(tl, D_MODEL)] * 4 + [_const_spec((1, D_MODEL))] * 2,
        out_shape=[big, big, big, big, vec, vec],
        compiler_params=_params(("arbitrary",)),
    )(dys5, yssm, proj, proj, s5_d, w_glu, b_glu)


def _cumsum_rows(a):
    row = lax.broadcasted_iota(jnp.int32, a.shape, 0)
    d = 1
    while d < a.shape[0]:
        a = a + jnp.where(row >= d, pltpu.roll(a, d, 0), 0.0)
        d *= 2
    return a


def _rev_cumsum_rows(a):
    n = a.shape[0]
    row = lax.broadcasted_iota(jnp.int32, a.shape, 0)
    d = 1
    while d < n:
        a = a + jnp.where(row < n - d, pltpu.roll(a, n - d, 0), 0.0)
        d *= 2
    return a


def _ssd_fill_padded(first, xs_ref, bc_ref, hx_ref, hb_ref, xp_ref, tl):
    hal = jnp.concatenate([hx_ref[...], hb_ref[...]], axis=1)
    xp_ref[0:8, :] = jnp.where(first, 0.0, hal)
    xp_ref[8:8 + tl, 0:1024] = xs_ref[...]
    xp_ref[8:8 + tl, 1024:2048] = bc_ref[...]


def _ssd_conv_fwd(first, xs_ref, bc_ref, hx_ref, hb_ref, cw_ref, cb_ref, xp_ref, tl):
    _ssd_fill_padded(first, xs_ref, bc_ref, hx_ref, hb_ref, xp_ref, tl)
    pre = cb_ref[...] + cw_ref[0:1, :] * xp_ref[5:5 + tl, :]
    for k in range(1, 4):
        pre = pre + cw_ref[k:k + 1, :] * xp_ref[5 + k:5 + k + tl, :]
    return pre


def _onehot_lane(h):
    return (lax.broadcasted_iota(jnp.int32, (1, LANES), 1) == h).astype(F32)


def _dot_exact(x, e):
    hi = x.astype(BF)
    r = x - hi.astype(F32)
    mid = r.astype(BF)
    lo = (r - mid.astype(F32)).astype(BF)
    return _dot(hi, e) + _dot(mid, e) + _dot(lo, e)


def _head_expand_matrices():
    e = lax.broadcasted_iota(jnp.int32, (LANES, D_MODEL), 0) == (lax.broadcasted_iota(jnp.int32, (LANES, D_MODEL), 1) >> 6)
    et = (lax.broadcasted_iota(jnp.int32, (D_MODEL, LANES), 0) >> 6) == lax.broadcasted_iota(jnp.int32, (D_MODEL, LANES), 1)
    return e.astype(BF), et.astype(BF)


def _group_masks():
    r64 = lax.broadcasted_iota(jnp.int32, (4 * CHUNK, CHUNK), 0)
    causal4 = (r64 & (CHUNK - 1)) >= lax.broadcasted_iota(jnp.int32, (4 * CHUNK, CHUNK), 1)
    r256 = lax.broadcasted_iota(jnp.int32, (4 * CHUNK, 4 * SSD_HEAD_DIM), 0)
    same = (r256 >> 6) == (lax.broadcasted_iota(jnp.int32, (4 * CHUNK, 4 * SSD_HEAD_DIM), 1) >> 6)
    return causal4, same


def _group_decay(acs, acs_t, j, causal4):
    col = jnp.concatenate([acs[:, 4 * j + hh:4 * j + hh + 1] for hh in range(4)], axis=0)
    rowv = jnp.concatenate([jnp.broadcast_to(acs_t[4 * j + hh:4 * j + hh + 1, :], (CHUNK, CHUNK)) for hh in range(4)], axis=0)
    return jnp.where(causal4, jnp.exp(col - rowv), 0.0)


def _group_last_decay(acs_t, j):
    return jnp.concatenate([jnp.broadcast_to(jnp.exp(acs_t[4 * j + hh:4 * j + hh + 1, CHUNK - 1:CHUNK]), (SSD_HEAD_DIM, 1))
                            for hh in range(4)], axis=0)


def _fold_heads(r):
    return r[0:CHUNK] + r[CHUNK:2 * CHUNK] + r[2 * CHUNK:3 * CHUNK] + r[3 * CHUNK:4 * CHUNK]


def _ssd_specs_in(tl, nt, rev):
    t_of = (lambda i: nt - 1 - i) if rev else (lambda i: i)
    rows = lambda w, col: pl.BlockSpec((tl, w), lambda i: (t_of(i), col))
    halo = lambda col: pl.BlockSpec((8, 1024), lambda i: (jnp.maximum(t_of(i) * (tl // 8) - 1, 0), col))
    return t_of, rows, halo


def _ssd_fwd(proj, pdt, conv_w, conv_b, dt_bias, a_log, ssd_d, norm_w):
    L = proj.shape[0]
    tl = min(TOKEN_TILE, L)
    nt, ncl = L // tl, tl // CHUNK
    _, rows, halo = _ssd_specs_in(tl, nt, False)

    def body(xs_ref, bc_ref, hx_ref, hb_ref, dt_ref, z_ref, cw_ref, cb_ref, dtb_ref, al_ref, dd_ref, nw_ref,
             y_ref, ypre_ref, st_ref, pre_ref, xp_ref, xbc_ref, dts_ref, hst_ref):
        i = pl.program_id(0)

        @pl.when(i == 0)
        def _():
            hst_ref[...] = jnp.zeros_like(hst_ref)

        pre = _ssd_conv_fwd(i == 0, xs_ref, bc_ref, hx_ref, hb_ref, cw_ref, cb_ref, xp_ref, tl)
        pre_ref[...] = pre
        xbc_ref[...] = pre * _sigmoid(pre)
        dts_ref[...] = _softplus(dt_ref[...] + dtb_ref[...])
        a_neg = -jnp.exp(al_ref[...])
        e16, _ = _head_expand_matrices()
        causal4, same = _group_masks()
        dd_x = _dot_exact(jnp.broadcast_to(dd_ref[...], (8, LANES)), e16)[0:1, :]

        def chunk(c, carry):
            r0 = pl.multiple_of(c * CHUNK, CHUNK)
            xbc = xbc_ref[pl.ds(r0, CHUNK), :]
            dtc = dts_ref[pl.ds(r0, CHUNK), :]
            acs = _cumsum_rows(dtc * a_neg)
            acs_t = acs.T
            acs_x = _dot_exact(acs, e16)
            xs = xbc[:, 0:1024]
            xd = xs * _dot_exact(dtc, e16)
            xd_bf = xd.astype(BF)
            xdd = (xd * jnp.exp(acs_x[CHUNK - 1:CHUNK, :] - acs_x)).astype(BF)
            e_x = jnp.exp(acs_x)
            for j in range(SSD_GROUPS):
                sl = slice(256 * j, 256 * (j + 1))
                bj = xbc[:, 1024 + 128 * j:1024 + 128 * (j + 1)].astype(BF)
                cj = xbc[:, 1536 + 128 * j:1536 + 128 * (j + 1)].astype(BF)
                g = _dot_nt(cj, bj)
                hj = hst_ref[sl, :]
                zj = _dot_nt(cj, hj.astype(BF))
                sc = (jnp.concatenate([g] * 4, axis=0) * _group_decay(acs, acs_t, j, causal4)).astype(BF)
                yd = _fold_heads(jnp.where(same, _dot(sc, xd_bf[:, sl]), 0.0))
                ypre_ref[pl.ds(r0, CHUNK), sl] = yd + e_x[:, sl] * zj + dd_x[:, sl] * xs[:, sl]
                st_ref[c, sl, :] = hj
                hst_ref[sl, :] = _group_last_decay(acs_t, j) * hj + _dot_tn(xdd[:, sl], bj)
            return carry

        lax.fori_loop(0, ncl, chunk, 0)
        z = z_ref[...]
        gg = ypre_ref[...] * z * _sigmoid(z)
        for j in range(SSD_GROUPS):
            seg = gg[:, 256 * j:256 * (j + 1)]
            r = lax.rsqrt(jnp.mean(seg * seg, axis=-1, keepdims=True) + EPS)
            y_ref[:, 256 * j:256 * (j + 1)] = (seg * r * nw_ref[:, 256 * j:256 * (j + 1)]).astype(BF)

    nc = L // CHUNK
    return pl.pallas_call(
        body, name="ssd_fwd", grid=(nt,),
        in_specs=[rows(1024, 1), rows(1024, 2), halo(1), halo(2), rows(LANES, 0), rows(1024, 0),
                  _const_spec((4, 2048)), _const_spec((1, 2048)), _const_spec((1, LANES)), _const_spec((1, LANES)),
                  _const_spec((1, LANES)), _const_spec((1, D_MODEL))],
        out_specs=[_row_spec(tl, D_MODEL), _row_spec(tl, D_MODEL), pl.BlockSpec((ncl, 1024, SSD_STATE), lambda i: (i, 0, 0)),
                   _row_spec(tl, 2048)],
        out_shape=[jax.ShapeDtypeStruct((L, D_MODEL), BF), jax.ShapeDtypeStruct((L, D_MODEL), F32),
                   jax.ShapeDtypeStruct((nc, 1024, SSD_STATE), F32), jax.ShapeDtypeStruct((L, 2048), F32)],
        scratch_shapes=[pltpu.VMEM((tl + 8, 2048), F32), pltpu.VMEM((tl, 2048), F32), pltpu.VMEM((tl, LANES), F32),
                        pltpu.VMEM((1024, SSD_STATE), F32)],
        compiler_params=_params(("arbitrary",)),
    )(proj, proj, proj, proj, pdt, proj, conv_w, conv_b, dt_bias, a_log, ssd_d, norm_w)


def _ssd_bwd(dyssd, ypre, proj, pdt, states, pre_act, conv_w, dt_bias, a_log, ssd_d, norm_w):
    L = proj.shape[0]
    tl = min(TOKEN_TILE, L)
    nt, ncl = L // tl, tl // CHUNK
    t_of, rows, halo = _ssd_specs_in(tl, nt, True)

    def body(dy_ref, ypre_ref, z_ref, xs_ref, bc_ref, hx_ref, hb_ref, dt_ref, st_ref, pre_ref, cw_ref, dtb_ref, al_ref,
             dd_ref, nw_ref,
             dxbc_ref, ddt_ref, dz_ref, dcw_ref, dcb_ref, ddtb_ref, dal_ref, ddd_ref, dnw_ref,
             xp_ref, xbc_ref, dts_ref, dyp_ref, dxs_ref, ddts_ref, dp_ref, dh_ref):
        i = pl.program_id(0)

        @pl.when(i == 0)
        def _():
            for r in (dcw_ref, dcb_ref, ddtb_ref, dal_ref, ddd_ref, dnw_ref, dh_ref):
                r[...] = jnp.zeros_like(r)
            dp_ref[tl:tl + 8, :] = jnp.zeros((8, 2048), F32)

        _ssd_fill_padded(t_of(i) == 0, xs_ref, bc_ref, hx_ref, hb_ref, xp_ref, tl)
        pre = pre_ref[...]
        xbc_ref[...] = pre * _sigmoid(pre)
        dts_ref[...] = _softplus(dt_ref[...] + dtb_ref[...])
        a_neg = -jnp.exp(al_ref[...])

        ypre = ypre_ref[...]
        z = z_ref[...]
        sz = _sigmoid(z)
        gg = ypre * z * sz
        dout = dy_ref[...]
        for j in range(SSD_GROUPS):
            sl = slice(256 * j, 256 * (j + 1))
            seg = gg[:, sl]
            r = lax.rsqrt(jnp.mean(seg * seg, axis=-1, keepdims=True) + EPS)
            gh = seg * r
            dnw_ref[:, sl] += jnp.sum(dout[:, sl] * gh, axis=0, keepdims=True)
            gw = dout[:, sl] * nw_ref[:, sl]
            dgg = r * (gw - gh * jnp.mean(gw * gh, axis=-1, keepdims=True))
            dyp_ref[:, sl] = dgg * z[:, sl] * sz[:, sl]
            dz_ref[:, sl] = (dgg * ypre[:, sl] * sz[:, sl] * (1.0 + z[:, sl] * (1.0 - sz[:, sl]))).astype(BF)

        e16, e16t = _head_expand_matrices()
        causal4, same = _group_masks()
        dd_x = _dot_exact(jnp.broadcast_to(dd_ref[...], (8, LANES)), e16)[0:1, :]
        last_row = (lax.broadcasted_iota(jnp.int32, (CHUNK, 1), 0) == CHUNK - 1).astype(F32)
        sel_rows = lax.broadcasted_iota(jnp.int32, (4 * CHUNK, LANES), 0) >> 6
        sel_lanes = lax.broadcasted_iota(jnp.int32, (4 * CHUNK, LANES), 1)

        def chunk(k, carry):
            dal_acc, ddx_acc = carry
            c = ncl - 1 - k
            r0 = pl.multiple_of(c * CHUNK, CHUNK)
            xbc = xbc_ref[pl.ds(r0, CHUNK), :]
            dtc = dts_ref[pl.ds(r0, CHUNK), :]
            dyp = dyp_ref[pl.ds(r0, CHUNK), :]
            acs = _cumsum_rows(dtc * a_neg)
            acs_t = acs.T
            acs_x = _dot_exact(acs, e16)
            dt_x = _dot_exact(dtc, e16)
            xs = xbc[:, 0:1024]
            xd = xs * dt_x
            xd_bf = xd.astype(BF)
            dec_x = jnp.exp(acs_x[CHUNK - 1:CHUNK, :] - acs_x)
            xdd = xd * dec_x
            xdd_bf = xdd.astype(BF)
            dz = dyp * jnp.exp(acs_x)
            dz_bf = dz.astype(BF)
            ddx_acc = ddx_acc + jnp.sum(dyp * xs, axis=0, keepdims=True)
            dacs = jnp.zeros((CHUNK, LANES), F32)
            hsum = jnp.zeros((1, LANES), F32)
            p1_l, p2_l, p3_l = [], [], []
            for j in range(SSD_GROUPS):
                sl = slice(256 * j, 256 * (j + 1))
                bj = xbc[:, 1024 + 128 * j:1024 + 128 * (j + 1)].astype(BF)
                cj = xbc[:, 1536 + 128 * j:1536 + 128 * (j + 1)].astype(BF)
                g = _dot_nt(cj, bj)
                hj = st_ref[c, sl, :]
                hj_bf = hj.astype(BF)
                dhj = dh_ref[sl, :]
                dhj_bf = dhj.astype(BF)
                zj = _dot_nt(cj, hj_bf)
                qj = _dot_nt(bj, dhj_bf)
                lm = _group_decay(acs, acs_t, j, causal4)
                sc = jnp.concatenate([g] * 4, axis=0) * lm
                sc_bf = sc.astype(BF)
                dym = jnp.where(same, jnp.concatenate([dyp[:, sl]] * 4, axis=0), 0.0).astype(BF)
                dsc = _dot_nt(dym, xd_bf[:, sl])
                dxd = _dot_tn(sc_bf, dym) + qj * dec_x[:, sl]
                m = dsc * sc
                dg_bf = _fold_heads(dsc * lm).astype(BF)
                rs = jnp.sum(m, axis=1, keepdims=True)
                e2 = dhj * hj
                for hh in range(4):
                    oh = _onehot_lane(4 * j + hh)
                    dacs = dacs + oh * rs[CHUNK * hh:CHUNK * (hh + 1)]
                    hsum = hsum + oh * jnp.sum(jnp.sum(e2[64 * hh:64 * (hh + 1)], axis=0, keepdims=True), axis=1, keepdims=True)
                sel = (sel_rows + 4 * j == sel_lanes).astype(BF)
                hi = m.astype(BF)
                rem = m - hi.astype(F32)
                mid = rem.astype(BF)
                lo = (rem - mid.astype(F32)).astype(BF)
                dacs = dacs - (_dot_tn(hi, sel) + _dot_tn(mid, sel) + _dot_tn(lo, sel))
                p1_l.append(dz[:, sl] * zj)
                p2_l.append(qj * xdd[:, sl])
                p3_l.append(dxd * xs[:, sl])
                dxs_ref[pl.ds(r0, CHUNK), sl] = dd_x[:, sl] * dyp[:, sl] + dxd * dt_x[:, sl]
                dxs_ref[pl.ds(r0, CHUNK), 1536 + 128 * j:1536 + 128 * (j + 1)] = _dot(dg_bf, bj) + _dot(dz_bf[:, sl], hj_bf)
                dxs_ref[pl.ds(r0, CHUNK), 1024 + 128 * j:1024 + 128 * (j + 1)] = _dot_tn(dg_bf, cj) + _dot(xdd_bf[:, sl], dhj_bf)
                dh_ref[sl, :] = _group_last_decay(acs_t, j) * dhj + _dot_tn(dz_bf[:, sl], cj)
            stacked = jnp.concatenate([jnp.concatenate(p1_l, axis=1), jnp.concatenate(p2_l, axis=1), jnp.concatenate(p3_l, axis=1)], axis=0)
            red = _dot_exact(stacked, e16t)
            r1, r2, ddtc = red[0:CHUNK], red[CHUNK:2 * CHUNK], red[2 * CHUNK:3 * CHUNK]
            tot = jnp.sum(r2, axis=0, keepdims=True) + jnp.exp(acs[CHUNK - 1:CHUNK, :]) * hsum
            da = _rev_cumsum_rows(dacs + r1 - r2 + last_row * tot)
            ddts_ref[pl.ds(r0, CHUNK), :] = ddtc + da * a_neg
            dal_acc = dal_acc + jnp.sum(da * dtc, axis=0, keepdims=True)
            return dal_acc, ddx_acc

        dal_acc, ddx_acc = lax.fori_loop(0, ncl, chunk, (jnp.zeros((1, LANES), F32), jnp.zeros((1, D_MODEL), F32)))
        dal_ref[...] += dal_acc * a_neg
        ddd_ref[...] += _dot_exact(jnp.broadcast_to(ddx_acc, (8, D_MODEL)), e16t)[0:1, :]
        ddt_raw = ddts_ref[...] * _sigmoid(dt_ref[...] + dtb_ref[...])
        ddt_ref[...] = ddt_raw
        ddtb_ref[...] += jnp.sum(ddt_raw, axis=0, keepdims=True)

        pre = pre_ref[...]
        sp = _sigmoid(pre)
        dpre = dxs_ref[...] * sp * (1.0 + pre * (1.0 - sp))
        dp_ref[0:tl, :] = dpre
        dcb_ref[...] += jnp.sum(dpre, axis=0, keepdims=True)
        dx = jnp.zeros((tl, 2048), F32)
        for k in range(4):
            dcw_ref[k:k + 1, :] += jnp.sum(dpre * xp_ref[5 + k:5 + k + tl, :], axis=0, keepdims=True)
            dx = dx + cw_ref[k:k + 1, :] * dp_ref[3 - k:3 - k + tl, :]
        dxbc_ref[...] = dx.astype(BF)
        dp_ref[tl:tl + 8, :] = dp_ref[0:8, :]

    vec = lambda w: jax.ShapeDtypeStruct((1, w), F32)
    rrow = lambda w: pl.BlockSpec((tl, w), lambda i: (t_of(i), 0))
    return pl.pallas_call(
        body, name="ssd_bwd", grid=(nt,),
        in_specs=[rrow(D_MODEL), rrow(D_MODEL), rows(1024, 0), rows(1024, 1), rows(1024, 2), halo(1), halo(2), rows(LANES, 0),
                  pl.BlockSpec((ncl, 1024, SSD_STATE), lambda i: (t_of(i), 0, 0)), rrow(2048),
                  _const_spec((4, 2048)), _const_spec((1, LANES)), _const_spec((1, LANES)),
                  _const_spec((1, LANES)), _const_spec((1, D_MODEL))],
        out_specs=[rrow(2048), rrow(LANES), rrow(D_MODEL), _const_spec((8, 2048)), _const_spec((1, 2048)),
                   _const_spec((1, LANES)), _const_spec((1, LANES)), _const_spec((1, LANES)), _const_spec((1, D_MODEL))],
        out_shape=[jax.ShapeDtypeStruct((L, 2048), BF), jax.ShapeDtypeStruct((L, LANES), F32), jax.ShapeDtypeStruct((L, D_MODEL), BF),
                   jax.ShapeDtypeStruct((8, 2048), F32), vec(2048), vec(LANES), vec(LANES), vec(LANES), vec(D_MODEL)],
        scratch_shapes=[pltpu.VMEM((tl + 8, 2048), F32), pltpu.VMEM((tl, 2048), F32),
                        pltpu.VMEM((tl, LANES), F32), pltpu.VMEM((tl, D_MODEL), F32), pltpu.VMEM((tl, 2048), F32),
                        pltpu.VMEM((tl, LANES), F32), pltpu.VMEM((tl + 8, 2048), F32), pltpu.VMEM((1024, SSD_STATE), F32)],
        compiler_params=_params(("arbitrary",)),
    )(dyssd, ypre, proj, proj, proj, proj, proj, pdt, states, pre_act, conv_w, dt_bias, a_log, ssd_d, norm_w)


def _head_fwd_bwd(x, ys5, yssd, p, target, w_out, w_gate, w_proj, ple_nw, fin_nw):
    L = x.shape[0]
    tl = min(TOKEN_TILE, L)
    inv_d = 1.0 / D_MODEL

    def body(x_ref, ys_ref, yd_ref, p_ref, t_ref, wo_ref, wg_ref, wp_ref, pnw_ref, fnw_ref,
             loss_ref, dys_ref, dyd_ref, dh1_ref, n2_ref, dgl_ref, dpp_ref, dpnw_ref, dfnw_ref):
        @pl.when(pl.program_id(0) == 0)
        def _():
            loss_ref[...] = jnp.zeros_like(loss_ref)
            dpnw_ref[...] = jnp.zeros_like(dpnw_ref)
            dfnw_ref[...] = jnp.zeros_like(dfnw_ref)

        h1 = x_ref[...] + _dot(ys_ref[...], wo_ref[0:1024, :]) + _dot(yd_ref[...], wo_ref[1024:2048, :])
        r1 = lax.rsqrt(jnp.mean(h1 * h1, axis=-1, keepdims=True) + EPS)
        hh1 = h1 * r1
        n2 = (hh1 * pnw_ref[...]).astype(BF)
        gate = _sigmoid(_dot(n2, wg_ref[...]))
        pp = _dot(p_ref[...].astype(BF), wp_ref[...])
        h2 = h1 + pp * gate
        r2 = lax.rsqrt(jnp.mean(h2 * h2, axis=-1, keepdims=True) + EPS)
        hh2 = h2 * r2
        err = hh2 * fnw_ref[...] - t_ref[...]
        loss_ref[...] += 0.5 * inv_d * jnp.sum(err * err)
        dyo = err * inv_d
        dfnw_ref[...] += jnp.sum(dyo * hh2, axis=0, keepdims=True)
        g2 = dyo * fnw_ref[...]
        dh2 = r2 * (g2 - hh2 * jnp.mean(g2 * hh2, axis=-1, keepdims=True))
        dpp_ref[...] = (dh2 * gate).astype(BF)
        dgl = (dh2 * pp * gate * (1.0 - gate)).astype(BF)
        dgl_ref[...] = dgl
        n2_ref[...] = n2
        dn2 = _dot_nt(dgl, wg_ref[...])
        dpnw_ref[...] += jnp.sum(dn2 * hh1, axis=0, keepdims=True)
        g1 = dn2 * pnw_ref[...]
        dh1 = dh2 + r1 * (g1 - hh1 * jnp.mean(g1 * hh1, axis=-1, keepdims=True))
        dh1_ref[...] = dh1
        dh1_bf = dh1.astype(BF)
        dys_ref[...] = _dot_nt(dh1_bf, wo_ref[0:1024, :]).astype(BF)
        dyd_ref[...] = _dot_nt(dh1_bf, wo_ref[1024:2048, :])

    big = jax.ShapeDtypeStruct((L, D_MODEL), BF)
    vec = jax.ShapeDtypeStruct((1, D_MODEL), F32)
    return pl.pallas_call(
        body, name="head_fwd_bwd", grid=(L // tl,),
        in_specs=[_row_spec(tl, D_MODEL), _row_spec(tl, D_MODEL), _row_spec(tl, D_MODEL), _row_spec(tl, 256), _row_spec(tl, D_MODEL),
                  _const_spec((2048, D_MODEL)), _const_spec((D_MODEL, D_MODEL)), _const_spec((256, D_MODEL)),
                  _const_spec((1, D_MODEL)), _const_spec((1, D_MODEL))],
        out_specs=[_const_spec((8, LANES)), _row_spec(tl, D_MODEL), _row_spec(tl, D_MODEL), _row_spec(tl, D_MODEL),
                   _row_spec(tl, D_MODEL), _row_spec(tl, D_MODEL), _row_spec(tl, D_MODEL), _const_spec((1, D_MODEL)), _const_spec((1, D_MODEL))],
        out_shape=[jax.ShapeDtypeStruct((8, LANES), F32), big, jax.ShapeDtypeStruct((L, D_MODEL), F32),
                   jax.ShapeDtypeStruct((L, D_MODEL), F32), big, big, big, vec, vec],
        compiler_params=_params(("arbitrary",)),
    )(x, ys5, yssd, p, target, w_out, w_gate, w_proj, ple_nw, fin_nw)


def _pad_lanes(v):
    return jnp.pad(v.reshape(1, -1), ((0, 0), (0, LANES - v.size)))


def _local_step(x, p, target, w):
    L = x.shape[0]
    nc = L // CHUNK
    nsteps = max(1, (nc - 1).bit_length())
    w_in_t = w["w_in"]
    w_main = w_in_t[:D_MAIN]
    w_dt = jnp.pad(w_in_t[D_MAIN:], ((0, LANES - SSD_HEADS), (0, 0)))
    norm_w = w["norm_w"].reshape(1, -1)
    s5_d = w["s5_D"].reshape(1, -1)
    b_glu = w["s5_b_glu"].reshape(1, -1)
    conv_b = w["conv_b"].reshape(1, -1)
    dtb, alog, ssd_d = _pad_lanes(w["dt_bias"]), _pad_lanes(w["A_log"]), _pad_lanes(w["ssd_D"])
    ssd_nw = w["ssd_norm_w"].reshape(1, -1)
    ple_nw = w["ple_norm_w"].reshape(1, -1)
    fin_nw = w["final_norm_w"].reshape(1, -1)

    s5_args = (w["s5_A_re"], w["s5_A_im"], w["s5_log_dt"], w["s5_B_re"], w["s5_B_im"], w["s5_C_re"], w["s5_C_im"])
    small, small_vjp = jax.vjp(_s5_discretise, *s5_args)
    kmat, kmat_vjp = jax.vjp(_s5_kmat, *small)
    bbp, pwr, cnp, pw1 = _s5_table_factors(*small)
    p1, p2 = _s5_scan_powers(w["s5_A_re"], w["s5_A_im"], w["s5_log_dt"], nsteps)

    hn, proj, pssd, pdt = _in_proj_fwd(x, norm_w, w_main, w_dt)
    uflat = _flat_hs(proj[:, :D_MODEL], nc)
    yflat, hsave = _s5_core_fwd(uflat, kmat, bbp, pwr, cnp, pw1, p1, p2)
    yssm = _unflat_tk(yflat, nc)
    ys5 = _s5_post_fwd(yssm, proj, s5_d, w["s5_w_glu"], b_glu)
    yssd, ypre, states, pre_act = _ssd_fwd(pssd, pdt, w["conv_w"], conv_b, dtb, alog, ssd_d, ssd_nw)
    (loss8, dys5, dyssd, dh1, n2, dgl2, dpp, g_ple_nw, g_fin_nw) = _head_fwd_bwd(
        x, ys5, yssd, p, target, w["w_out"], w["w_ple_gate"], w["w_ple_proj"], ple_nw, fin_nw)

    (dxbc, ddt, dzd, g_cw, g_cb, g_dtb, g_alog, g_ssd_d, g_ssd_nw) = _ssd_bwd(
        dyssd, ypre, pssd, pdt, states, pre_act, w["conv_w"], dtb, alog, ssd_d, ssd_nw)
    dzs, dyssm, a_glu, dgl1, g_bglu, g_s5d = _s5_post_bwd(dys5, yssm, proj, s5_d, w["s5_w_glu"], b_glu)
    duflat, dkmat, r12, q12, xy, uv, da8 = _s5_core_bwd(uflat, _flat_tk(dyssm, nc), hsave, kmat, bbp, pwr, cnp, pw1, p1, p2)
    da64 = jnp.concatenate([da8[:, 0, :S5_STATE] + da8[:, 0, S5_STATE:], da8[:, 1, S5_STATE:] - da8[:, 1, :S5_STATE]], axis=-1)
    d_small = [a + b for a, b in zip(_s5_small_cotangents(r12, q12, xy, uv, da64), kmat_vjp(dkmat))]
    g_s5 = small_vjp(tuple(d_small))
    gx, du, g_norm_w = _in_proj_bwd(x, norm_w, dh1, _unflat_hs(duflat, nc), dyssm, s5_d, dzs, dzd, dxbc, ddt, w_main, w_dt)

    g_w_in = jnp.concatenate([
        _matmul_tn(du, hn, "dw_in_u"), _matmul_tn(dzs, hn, "dw_in_zs"), _matmul_tn(dzd, hn, "dw_in_zd"),
        _matmul_tn(dxbc, hn, "dw_in_xbc"), _matmul_tn(ddt, hn, "dw_in_dt")[:SSD_HEADS]], axis=0)
    grads = {
        "norm_w": g_norm_w, "w_in": g_w_in,
        "s5_A_re": g_s5[0], "s5_A_im": g_s5[1], "s5_log_dt": g_s5[2], "s5_B_re": g_s5[3], "s5_B_im": g_s5[4],
        "s5_C_re": g_s5[5], "s5_C_im": g_s5[6], "s5_D": g_s5d, "s5_w_glu": _matmul_tn(a_glu, dgl1, "dw_glu"), "s5_b_glu": g_bglu,
        "conv_w": g_cw[:4], "conv_b": g_cb, "dt_bias": g_dtb[:, :SSD_HEADS], "A_log": g_alog[:, :SSD_HEADS],
        "ssd_D": g_ssd_d[:, :SSD_HEADS], "ssd_norm_w": g_ssd_nw,
        "w_out": jnp.concatenate([_matmul_tn(ys5, dh1, "dw_out_s5"), _matmul_tn(yssd, dh1, "dw_out_ssd")], axis=0),
        "ple_norm_w": g_ple_nw, "w_ple_gate": _matmul_tn(n2, dgl2, "dw_gate"), "w_ple_proj": _matmul_tn(p, dpp, "dw_proj"),
        "final_norm_w": g_fin_nw,
    }
    return loss8[0, 0], gx, grads


WEIGHTS = ("norm_w", "w_in", "s5_A_re", "s5_A_im", "s5_log_dt", "s5_B_re", "s5_B_im", "s5_C_re", "s5_C_im", "s5_D", "s5_w_glu",
           "s5_b_glu", "conv_w", "conv_b", "dt_bias", "A_log", "ssd_D", "ssd_norm_w", "w_out", "ple_norm_w", "w_ple_gate",
           "w_ple_proj", "final_norm_w")
BIG = {"w_in": ((1284, 1024), 0), "s5_w_glu": ((256, 1024), 0), "w_out": ((512, 1024), 0), "w_ple_gate": ((256, 1024), 0),
       "w_ple_proj": ((256, 256), 1)}
SMALL = {"norm_w": (1024,), "s5_A_re": (64, 64), "s5_A_im": (64, 64), "s5_log_dt": (64,), "s5_B_re": (64, 64, 16),
         "s5_B_im": (64, 64, 16), "s5_C_re": (64, 16, 64), "s5_C_im": (64, 16, 64), "s5_D": (1024,), "s5_b_glu": (1024,),
         "conv_w": (4, 2048), "conv_b": (2048,), "dt_bias": (16,), "A_log": (16,), "ssd_D": (16,), "ssd_norm_w": (1024,),
         "ple_norm_w": (1024,), "final_norm_w": (1024,)}
BIG_ROWS = {n: s[0] * s[1] // LANES for n, (s, _) in BIG.items()}
BIG_ROWS_TOTAL = sum(BIG_ROWS.values())
SMALL_TOTAL = sum(math.prod(s) for s in SMALL.values())
SMALL_PIECE_ROWS = -(-SMALL_TOTAL // (N_CHIPS * 16 * LANES)) * 16
HALF_ROWS = (BIG_ROWS_TOTAL + SMALL_PIECE_ROWS) // 2
SMALL_ROW0 = BIG_ROWS_TOTAL - HALF_ROWS


def _mesh_pos():
    return lax.axis_index("x"), lax.axis_index("y"), lax.axis_index("c")


def _other_chips(x, y):
    return [(1 - x, y), (x, 1 - y), (1 - x, 1 - y)]


def _comm_params():
    return pltpu.CompilerParams(has_side_effects=True)


def _all_gather_chips(wpack, cw):
    half = wpack.shape[0] // 2

    def body(w_ref, c_ref, wo_ref, co_ref, send_sems, recv_sems, fwd_send, fwd_recv, loc_sems):
        x, y, c = _mesh_pos()
        me = 2 * x + y
        sib = (x, y, 1 - c)
        mine = pl.ds(c * half, half)
        theirs = pl.ds((1 - c) * half, half)
        others = _other_chips(x, y)
        loc = [pltpu.make_async_copy(c_ref, co_ref.at[me], loc_sems.at[0])]
        for cp in loc:
            cp.start()

        def from_chip(k, chip, dev):
            return pltpu.make_async_remote_copy(w_ref.at[mine], wo_ref.at[chip, mine], send_sems.at[2 * k], recv_sems.at[2 * k],
                                                device_id=dev, device_id_type=MESH)

        def conv_from(k, chip, dev):
            return pltpu.make_async_remote_copy(c_ref, co_ref.at[chip], send_sems.at[2 * k + 1], recv_sems.at[2 * k + 1],
                                                device_id=dev, device_id_type=MESH)

        def passed(k, chip, rows):
            return pltpu.make_async_remote_copy(wo_ref.at[chip, rows], wo_ref.at[chip, rows], fwd_send.at[k], fwd_recv.at[k],
                                                device_id=sib, device_id_type=MESH)

        sends = []
        for k, (px, py) in enumerate(others):
            sends += [from_chip(k, me, (px, py, c)), conv_from(k, me, (px, py, c))]
        for cp in sends:
            cp.start()
        fwds = []
        for k, (px, py) in enumerate(others):
            chip = 2 * px + py
            from_chip(k, chip, (px, py, c)).wait_recv()
            fwds.append(passed(k, chip, mine))
            fwds[-1].start()
        for k, (px, py) in enumerate(others):
            chip = 2 * px + py
            passed(k, chip, theirs).wait_recv()
            conv_from(k, chip, (px, py, c)).wait_recv()
        for cp in sends + fwds:
            cp.wait_send()
        for cp in loc:
            cp.wait()

    return pl.pallas_call(
        body, name="all_gather_weights", in_specs=[ANY, ANY], out_specs=[ANY, ANY],
        out_shape=[jax.ShapeDtypeStruct((N_CHIPS,) + wpack.shape, wpack.dtype), jax.ShapeDtypeStruct((N_CHIPS,) + cw.shape, cw.dtype)],
        scratch_shapes=[pltpu.SemaphoreType.DMA((6,)), pltpu.SemaphoreType.DMA((6,)), pltpu.SemaphoreType.DMA((3,)),
                        pltpu.SemaphoreType.DMA((3,)), pltpu.SemaphoreType.DMA((1,))],
        compiler_params=_comm_params(),
    )(wpack, cw)


def _exchange_pair(gp):
    def body(g_ref, r_ref, send_sems, recv_sems):
        x, y, c = _mesh_pos()
        cps = [pltpu.make_async_remote_copy(g_ref.at[s, 1 - c], r_ref.at[s], send_sems.at[s], recv_sems.at[s],
                                            device_id=(x, y, 1 - c), device_id_type=MESH) for s in range(N_CHIPS)]
        for cp in cps:
            cp.start()
        for cp in cps:
            cp.wait()

    return pl.pallas_call(
        body, name="grad_exchange_pair", in_specs=[ANY], out_specs=ANY,
        out_shape=jax.ShapeDtypeStruct((N_CHIPS,) + gp.shape[2:], gp.dtype),
        scratch_shapes=[pltpu.SemaphoreType.DMA((N_CHIPS,)), pltpu.SemaphoreType.DMA((N_CHIPS,))],
        compiler_params=_comm_params(),
    )(gp)


def _pair_sum(mine, from_sibling):
    def body(a_ref, b_ref, bf_ref, tail_ref):
        s = a_ref[0] + b_ref[0]
        bf_ref[0] = s.astype(BF)
        tail_ref[0] = s[SMALL_ROW0:, :]

    piece = pl.BlockSpec((1, HALF_ROWS, LANES), lambda i: (i, 0, 0))
    return pl.pallas_call(
        body, name="grad_pair_sum", grid=(N_CHIPS,), in_specs=[piece, piece],
        out_specs=[piece, pl.BlockSpec((1, SMALL_PIECE_ROWS, LANES), lambda i: (i, 0, 0))],
        out_shape=[jax.ShapeDtypeStruct((N_CHIPS, HALF_ROWS, LANES), BF), jax.ShapeDtypeStruct((N_CHIPS, SMALL_PIECE_ROWS, LANES), F32)],
        compiler_params=_params(("parallel",)),
    )(mine, from_sibling)


def _exchange_chips(ps_bf, ps_tail):
    def body(p_ref, t_ref, r_ref, rt_ref, send_sems, recv_sems):
        x, y, c = _mesh_pos()
        cps = []
        for k, (px, py) in enumerate(_other_chips(x, y)):
            cps.append(pltpu.make_async_remote_copy(p_ref.at[2 * px + py], r_ref.at[k], send_sems.at[2 * k], recv_sems.at[2 * k],
                                                    device_id=(px, py, c), device_id_type=MESH))
            cps.append(pltpu.make_async_remote_copy(t_ref.at[2 * px + py], rt_ref.at[k], send_sems.at[2 * k + 1],
                                                    recv_sems.at[2 * k + 1], device_id=(px, py, c), device_id_type=MESH))
        for cp in cps:
            cp.start()
        for cp in cps:
            cp.wait()

    return pl.pallas_call(
        body, name="grad_exchange_chips", in_specs=[ANY, ANY], out_specs=[ANY, ANY],
        out_shape=[jax.ShapeDtypeStruct((N_CHIPS - 1,) + ps_bf.shape[1:], ps_bf.dtype),
                   jax.ShapeDtypeStruct((N_CHIPS - 1,) + ps_tail.shape[1:], ps_tail.dtype)],
        scratch_shapes=[pltpu.SemaphoreType.DMA((6,)), pltpu.SemaphoreType.DMA((6,))],
        compiler_params=_comm_params(),
    )(ps_bf, ps_tail)


def _chip_sum(own_bf, own_tail, others_bf, others_tail):
    def body(ob_ref, ot_ref, b_ref, t_ref, o_ref):
        acc = ob_ref[0:SMALL_ROW0, :].astype(F32)
        tail = ot_ref[...]
        for k in range(N_CHIPS - 1):
            acc = acc + b_ref[k, 0:SMALL_ROW0, :].astype(F32)
            tail = tail + t_ref[k]
        o_ref[0:SMALL_ROW0, :] = acc
        o_ref[SMALL_ROW0:, :] = tail

    return pl.pallas_call(
        body, name="grad_chip_sum", out_shape=jax.ShapeDtypeStruct((HALF_ROWS, LANES), F32),
        compiler_params=_params(),
    )(own_bf, own_tail, others_bf, others_tail)


def _swap_reduced_halves(gh):
    def body(g_ref, o_ref, send_sem, recv_sem):
        x, y, c = _mesh_pos()
        cp = pltpu.make_async_remote_copy(g_ref, o_ref, send_sem, recv_sem, device_id=(x, y, 1 - c), device_id_type=MESH)
        cp.start()
        cp.wait()

    return pl.pallas_call(
        body, name="grad_swap_halves", in_specs=[ANY], out_specs=ANY,
        out_shape=jax.ShapeDtypeStruct(gh.shape, gh.dtype),
        scratch_shapes=[pltpu.SemaphoreType.DMA, pltpu.SemaphoreType.DMA],
        compiler_params=_comm_params(),
    )(gh)


def _gather_small(second_half):
    def body(gs_ref, sm_ref, send_sems, recv_sems, loc_sem):
        x, y, c = _mesh_pos()
        me = 2 * x + y
        small = gs_ref.at[pl.ds(SMALL_ROW0, SMALL_PIECE_ROWS)]
        loc = pltpu.make_async_copy(small, sm_ref.at[me], loc_sem)
        loc.start()
        cps = [pltpu.make_async_remote_copy(small, sm_ref.at[me], send_sems.at[k], recv_sems.at[k],
                                            device_id=(px, py, c), device_id_type=MESH)
               for k, (px, py) in enumerate(_other_chips(x, y))]
        for cp in cps:
            cp.start()
        for cp in cps:
            cp.wait()
        loc.wait()

    return pl.pallas_call(
        body, name="grad_gather_small", in_specs=[ANY], out_specs=ANY,
        out_shape=jax.ShapeDtypeStruct((N_CHIPS, SMALL_PIECE_ROWS, LANES), second_half.dtype),
        scratch_shapes=[pltpu.SemaphoreType.DMA((3,)), pltpu.SemaphoreType.DMA((3,)), pltpu.SemaphoreType.DMA],
        compiler_params=_comm_params(),
    )(second_half)


def _pack_grads(grads):
    small = jnp.concatenate([grads[n].reshape(-1) for n in SMALL])
    small = jnp.pad(small, (0, N_CHIPS * SMALL_PIECE_ROWS * LANES - SMALL_TOTAL)).reshape(N_CHIPS, SMALL_PIECE_ROWS, LANES)
    pieces = []
    for s in range(N_CHIPS):
        rows = []
        for n, (shp, axis) in BIG.items():
            g = grads[n]
            blk = g[s * shp[0]:(s + 1) * shp[0], :] if axis == 0 else g[:, s * shp[1]:(s + 1) * shp[1]]
            rows.append(blk.reshape(-1, LANES))
        rows.append(small[s])
        pieces.append(jnp.concatenate(rows, axis=0).reshape(2, HALF_ROWS, LANES))
    return jnp.stack(pieces)


def _unpack_shard(first_half, second_half):
    rows = jnp.concatenate([first_half, second_half], axis=0)
    out, r0 = {}, 0
    for n, (shp, _) in BIG.items():
        out[n] = rows[r0:r0 + BIG_ROWS[n]].reshape(shp)
        r0 += BIG_ROWS[n]
    return out


def _unpack_small(sm):
    flat = sm.reshape(-1)
    out, o = {}, 0
    for n, shp in SMALL.items():
        k = math.prod(shp)
        out[n] = flat[o:o + k].reshape(shp)
        o += k
    return out


def _as_2d(a):
    n = a.size
    if a.ndim >= 2 and a.shape[-1] > 1024:
        return a.reshape(-1, a.shape[-1])
    if n % 1024 == 0:
        return a.reshape(n // 1024, 1024)
    return a.reshape(1, n)


def _adamw(w, g, m, v, name):
    shape = w.shape
    w2, g2, m2, v2 = (_as_2d(a) for a in (w, g, m, v))
    rows, cols = w2.shape
    rb = 256 if rows >= 512 else rows
    by_cols = rows % rb != 0

    def body(w_ref, g_ref, m_ref, v_ref, d_ref, mo_ref, vo_ref):
        gv = g_ref[...]
        mn = ADAM_B1 * m_ref[...] + (1.0 - ADAM_B1) * gv
        vn = ADAM_B2 * v_ref[...] + (1.0 - ADAM_B2) * (gv * gv)
        m_hat = mn / (1.0 - ADAM_B1 ** ADAM_STEP)
        v_hat = vn / (1.0 - ADAM_B2 ** ADAM_STEP)
        d_ref[...] = -ADAM_LR * (m_hat / (jnp.sqrt(v_hat) + ADAM_EPS) + ADAM_WD * w_ref[...])
        mo_ref[...] = mn
        vo_ref[...] = vn

    spec = pl.BlockSpec((rows, 256), lambda i: (0, i)) if by_cols else _row_spec(rb, cols)
    sds = jax.ShapeDtypeStruct((rows, cols), F32)
    d, mo, vo = pl.pallas_call(
        body, name=name, grid=(cols // 256 if by_cols else rows // rb,), in_specs=[spec] * 4, out_specs=[spec] * 3, out_shape=[sds] * 3,
        compiler_params=_params(("parallel",)),
    )(w2, g2, m2, v2)
    return d.reshape(shape), mo.reshape(shape), vo.reshape(shape)


def kernel(x, p, norm_w, w_in, s5_A_re, s5_A_im, s5_log_dt, s5_B_re, s5_B_im, s5_C_re, s5_C_im, s5_D, s5_w_glu, s5_b_glu, conv_w, conv_b, dt_bias, A_log, ssd_D, ssd_norm_w, w_out, ple_norm_w, w_ple_gate, w_ple_proj, final_norm_w, loss_target, m_norm_w, m_w_in, m_s5_A_re, m_s5_A_im, m_s5_log_dt, m_s5_B_re, m_s5_B_im, m_s5_C_re, m_s5_C_im, m_s5_D, m_s5_w_glu, m_s5_b_glu, m_conv_w, m_conv_b, m_dt_bias, m_A_log, m_ssd_D, m_ssd_norm_w, m_w_out, m_ple_norm_w, m_w_ple_gate, m_w_ple_proj, m_final_norm_w, v_norm_w, v_w_in, v_s5_A_re, v_s5_A_im, v_s5_log_dt, v_s5_B_re, v_s5_B_im, v_s5_C_re, v_s5_C_im, v_s5_D, v_s5_w_glu, v_s5_b_glu, v_conv_w, v_conv_b, v_dt_bias, v_A_log, v_ssd_D, v_ssd_norm_w, v_w_out, v_ple_norm_w, v_w_ple_gate, v_w_ple_proj, v_final_norm_w):
    given = (norm_w, w_in, s5_A_re, s5_A_im, s5_log_dt, s5_B_re, s5_B_im, s5_C_re, s5_C_im, s5_D, s5_w_glu, s5_b_glu, conv_w, conv_b,
             dt_bias, A_log, ssd_D, ssd_norm_w, w_out, ple_norm_w, w_ple_gate, w_ple_proj, final_norm_w)
    given_m = (m_norm_w, m_w_in, m_s5_A_re, m_s5_A_im, m_s5_log_dt, m_s5_B_re, m_s5_B_im, m_s5_C_re, m_s5_C_im, m_s5_D, m_s5_w_glu,
               m_s5_b_glu, m_conv_w, m_conv_b, m_dt_bias, m_A_log, m_ssd_D, m_ssd_norm_w, m_w_out, m_ple_norm_w, m_w_ple_gate,
               m_w_ple_proj, m_final_norm_w)
    given_v = (v_norm_w, v_w_in, v_s5_A_re, v_s5_A_im, v_s5_log_dt, v_s5_B_re, v_s5_B_im, v_s5_C_re, v_s5_C_im, v_s5_D, v_s5_w_glu,
               v_s5_b_glu, v_conv_w, v_conv_b, v_dt_bias, v_A_log, v_ssd_D, v_ssd_norm_w, v_w_out, v_ple_norm_w, v_w_ple_gate,
               v_w_ple_proj, v_final_norm_w)
    wts, mom, var = dict(zip(WEIGHTS, given)), dict(zip(WEIGHTS, given_m)), dict(zip(WEIGHTS, given_v))
    drop = lambda n, a: a if n == "final_norm_w" else a[0]

    shard2d = lambda n, a: a[0].T if n == "w_in" else drop(n, a)
    wpack = jnp.concatenate([shard2d(n, wts[n]).astype(BF).reshape(-1, LANES) for n in BIG], axis=0)
    wall, cwall = _all_gather_chips(wpack, drop("conv_w", wts["conv_w"]))
    chip = 2 * lax.axis_index("x") + lax.axis_index("y")
    full, r0 = {}, 0
    for n, (shp, axis) in BIG.items():
        blk = lax.dynamic_update_slice_in_dim(wall[:, r0:r0 + BIG_ROWS[n]].reshape((N_CHIPS,) + shp),
                                              shard2d(n, wts[n]).astype(BF)[None], chip, axis=0)
        full[n] = blk.reshape(N_CHIPS * shp[0], shp[1]) if axis == 0 else blk.transpose(1, 0, 2).reshape(shp[0], N_CHIPS * shp[1])
        r0 += BIG_ROWS[n]
    for n in SMALL:
        full[n] = drop(n, wts[n])
    full["conv_w"] = cwall.transpose(1, 0, 2).reshape(4, 2048)

    loss, gx, grads = _local_step(x[0], p[0, 0], loss_target[0], full)
    loss = lax.psum(loss, MESH_AXES)

    gp = _pack_grads({n: grads[n].reshape(SMALL[n]) if n in SMALL else grads[n] for n in WEIGHTS})
    c = lax.axis_index("c")
    from_sibling = _exchange_pair(gp)
    mine = lax.dynamic_index_in_dim(gp, c, axis=1, keepdims=False)
    ps_bf, ps_tail = _pair_sum(mine, from_sibling)
    others_bf, others_tail = _exchange_chips(ps_bf, ps_tail)
    own_bf = lax.dynamic_index_in_dim(ps_bf, chip, axis=0, keepdims=False)
    own_tail = lax.dynamic_index_in_dim(ps_tail, chip, axis=0, keepdims=False)
    reduced_half = _chip_sum(own_bf, own_tail, others_bf, others_tail)
    sibling_half = _swap_reduced_halves(reduced_half)
    first_half = jnp.where(c == 0, reduced_half, sibling_half)
    second_half = jnp.where(c == 0, sibling_half, reduced_half)
    sm = _gather_small(second_half)
    g_final = {**_unpack_small(sm), **_unpack_shard(first_half, second_half)}
    g_final["conv_w"] = lax.dynamic_slice_in_dim(g_final["conv_w"], chip * 512, 512, axis=1)

    outs_g, outs_d, outs_m, outs_v = [], [], [], []
    for n in WEIGHTS:
        if n == "w_in":
            res = _adamw(wts[n][0].T, g_final[n], mom[n][0].T, var[n][0].T, "adamw_" + n)
            g, d, mo, vo = (a.T[None] for a in (g_final[n],) + res)
        else:
            g = g_final[n].reshape(wts[n].shape)
            d, mo, vo = _adamw(wts[n], g, mom[n], var[n], "adamw_" + n)
        outs_g.append(g)
        outs_d.append(d)
        outs_m.append(mo)
        outs_v.append(vo)
    return (loss, gx[None], *outs_g, *outs_d, *outs_m, *outs_v)
```

```python
import functools
import math

import jax
import jax.numpy as jnp
from jax import lax
from jax.experimental import pallas as pl
from jax.experimental.pallas import tpu as pltpu

F32 = jnp.float32
BF = jnp.bfloat16
EPS = 1e-6
CHUNK = 64
D_MODEL = 1024
S5_GROUPS = 64
S5_CH = 16
S5_STATE = 64
SSD_HEADS = 16
SSD_HEAD_DIM = 64
SSD_GROUPS = 4
SSD_STATE = 128
D_MAIN = 5120
LANES = 128
TOKEN_TILE = 256
VMEM_LIMIT = 56 * 1024 * 1024
MESH_AXES = ("x", "y", "c")
N_CHIPS = 4
ADAM_LR, ADAM_B1, ADAM_B2, ADAM_EPS, ADAM_WD, ADAM_STEP = 0.001, 0.9, 0.999, 1e-08, 0.01, 10
MESH = pl.DeviceIdType.MESH
ANY = pl.BlockSpec(memory_space=pl.ANY)


def _dot(a, b):
    return jnp.dot(a, b, preferred_element_type=F32)


def _dot_nt(a, b):
    return lax.dot_general(a, b, (((1,), (1,)), ((), ())), preferred_element_type=F32)


def _dot_tn(a, b):
    return lax.dot_general(a, b, (((0,), (0,)), ((), ())), preferred_element_type=F32)


def _sigmoid(x):
    return 1.0 / (1.0 + jnp.exp(-x))


def _softplus(x):
    return jnp.maximum(x, 0.0) + jnp.log(1.0 + jnp.exp(-jnp.abs(x)))


_GELU_C = math.sqrt(2.0 / math.pi)


def _gelu(x):
    return 0.5 * x * (1.0 + jnp.tanh(_GELU_C * (x + 0.044715 * x * x * x)))


def _gelu_grad(x):
    th = jnp.tanh(_GELU_C * (x + 0.044715 * x * x * x))
    return 0.5 * (1.0 + th) + 0.5 * x * (1.0 - th * th) * _GELU_C * (1.0 + 3.0 * 0.044715 * x * x)


def _params(sem=None):
    return pltpu.CompilerParams(dimension_semantics=sem, vmem_limit_bytes=VMEM_LIMIT)


def _row_spec(tl, width, col=0):
    return pl.BlockSpec((tl, width), lambda i, col=col: (i, col))


def _const_spec(shape):
    nd = len(shape)
    return pl.BlockSpec(shape, lambda *_: (0,) * nd)


def _in_proj_fwd(x, norm_w, w_main, w_dt):
    L = x.shape[0]
    tl = min(TOKEN_TILE, L)

    def body(x_ref, nw_ref, wm_ref, wd_ref, hn_ref, ps5_ref, pssd_ref, pd_ref):
        xv = x_ref[...]
        r = lax.rsqrt(jnp.mean(xv * xv, axis=-1, keepdims=True) + EPS)
        hn = (xv * r * nw_ref[...]).astype(BF)
        hn_ref[...] = hn
        for j in range(2):
            ps5_ref[:, j * 1024:(j + 1) * 1024] = _dot_nt(hn, wm_ref[j * 1024:(j + 1) * 1024, :]).astype(BF)
        for j in range(3):
            pssd_ref[:, j * 1024:(j + 1) * 1024] = _dot_nt(hn, wm_ref[(j + 2) * 1024:(j + 3) * 1024, :])
        pd_ref[...] = _dot_nt(hn, wd_ref[...])

    return pl.pallas_call(
        body, name="in_proj_fwd", grid=(L // tl,),
        in_specs=[_row_spec(tl, D_MODEL), _const_spec((1, D_MODEL)), _const_spec((D_MAIN, D_MODEL)), _const_spec((LANES, D_MODEL))],
        out_specs=[_row_spec(tl, D_MODEL), _row_spec(tl, 2048), _row_spec(tl, 3072), _row_spec(tl, LANES)],
        out_shape=[jax.ShapeDtypeStruct((L, D_MODEL), BF), jax.ShapeDtypeStruct((L, 2048), BF), jax.ShapeDtypeStruct((L, 3072), F32),
                   jax.ShapeDtypeStruct((L, LANES), F32)],
        compiler_params=_params(("arbitrary",)),
    )(x, norm_w, w_main, w_dt)


def _in_proj_bwd(x, norm_w, dh1, du_flat, dyssm, s5_d, dzs, dzd, dxbc, ddt, w_main, w_dt):
    L = x.shape[0]
    tl = min(TOKEN_TILE, L)

    def body(x_ref, nw_ref, dh1_ref, duf_ref, dys_ref, d_ref, dzs_ref, dzd_ref, dxbc_ref, ddt_ref, wm_ref, wd_ref,
             gx_ref, du_ref, gnw_ref):
        @pl.when(pl.program_id(0) == 0)
        def _():
            gnw_ref[...] = jnp.zeros_like(gnw_ref)

        du = (duf_ref[...].astype(F32) + dys_ref[...].astype(F32) * d_ref[...]).astype(BF)
        du_ref[...] = du
        dhn = _dot(du, wm_ref[0:1024, :])
        dhn += _dot(dzs_ref[...], wm_ref[1024:2048, :])
        dhn += _dot(dzd_ref[...], wm_ref[2048:3072, :])
        dhn += _dot(dxbc_ref[...], wm_ref[3072:5120, :])
        dhn += _dot(ddt_ref[...].astype(BF), wd_ref[...])
        xv = x_ref[...]
        r = lax.rsqrt(jnp.mean(xv * xv, axis=-1, keepdims=True) + EPS)
        xh = xv * r
        gnw_ref[...] += jnp.sum(dhn * xh, axis=0, keepdims=True)
        g = dhn * nw_ref[...]
        gx_ref[...] = dh1_ref[...] + r * (g - xh * jnp.mean(g * xh, axis=-1, keepdims=True))

    return pl.pallas_call(
        body, name="in_proj_bwd", grid=(L // tl,),
        in_specs=[_row_spec(tl, D_MODEL), _const_spec((1, D_MODEL)), _row_spec(tl, D_MODEL), _row_spec(tl, D_MODEL),
                  _row_spec(tl, D_MODEL), _const_spec((1, D_MODEL)), _row_spec(tl, D_MODEL), _row_spec(tl, D_MODEL),
                  _row_spec(tl, 2048), _row_spec(tl, LANES), _const_spec((D_MAIN, D_MODEL)), _const_spec((LANES, D_MODEL))],
        out_specs=[_row_spec(tl, D_MODEL), _row_spec(tl, D_MODEL), _const_spec((1, D_MODEL))],
        out_shape=[jax.ShapeDtypeStruct((L, D_MODEL), F32), jax.ShapeDtypeStruct((L, D_MODEL), BF), jax.ShapeDtypeStruct((1, D_MODEL), F32)],
        compiler_params=_params(("arbitrary",)),
    )(x, norm_w, dh1, du_flat, dyssm, s5_d, dzs, dzd, dxbc, ddt, w_main, w_dt)


def _matmul_tn(a, b, name):
    L, M = a.shape
    N = b.shape[1]
    tm, tn, tk = min(M, 1024), min(N, 1024), min(L, 1024)

    def body(a_ref, b_ref, o_ref):
        @pl.when(pl.program_id(2) == 0)
        def _():
            o_ref[...] = jnp.zeros_like(o_ref)

        o_ref[...] += _dot_tn(a_ref[...].astype(BF), b_ref[...].astype(BF))

    return pl.pallas_call(
        body, name=name, grid=(M // tm, N // tn, L // tk),
        in_specs=[pl.BlockSpec((tk, tm), lambda i, j, k: (k, i)), pl.BlockSpec((tk, tn), lambda i, j, k: (k, j))],
        out_specs=pl.BlockSpec((tm, tn), lambda i, j, k: (i, j)),
        out_shape=jax.ShapeDtypeStruct((M, N), F32),
        compiler_params=_params(("parallel", "parallel", "arbitrary")),
    )(a, b)


def _s5_discretise(a_re, a_im, log_dt, b_re, b_im, c_re, c_im):
    dt = jnp.exp(log_dt)[:, None]
    tau = jnp.arange(CHUNK + 1, dtype=F32)
    mag = jnp.exp((a_re * dt)[:, :, None] * tau)
    ang = (a_im * dt)[:, :, None] * tau
    pw_re, pw_im = mag * jnp.cos(ang), mag * jnp.sin(ang)
    er, ei = pw_re[:, :, 1] - 1.0, pw_im[:, :, 1]
    den = a_re * a_re + a_im * a_im
    beta_re, beta_im = (er * a_re + ei * a_im) / den, (ei * a_re - er * a_im) / den
    bb_re = (beta_re[:, :, None] * b_re - beta_im[:, :, None] * b_im).transpose(0, 2, 1)
    bb_im = (beta_re[:, :, None] * b_im + beta_im[:, :, None] * b_re).transpose(0, 2, 1)
    return bb_re, bb_im, c_re, c_im, pw_re, pw_im


def _s5_table_factors(bb_re, bb_im, c_re, c_im, pw_re, pw_im):
    pr = pw_re[:, :, CHUNK - 1::-1].transpose(0, 2, 1)
    pi = pw_im[:, :, CHUNK - 1::-1].transpose(0, 2, 1)
    bbp = jnp.concatenate([bb_re, bb_im], axis=-1)
    pwr = jnp.concatenate([pr, pi], axis=-1)
    cnp = jnp.concatenate([c_re, c_im], axis=-1)
    pwf = jnp.concatenate([pw_re.transpose(0, 2, 1), pw_im.transpose(0, 2, 1)], axis=-1)
    return bbp, pwr, cnp, jnp.pad(pwf, ((0, 0), (0, 7), (0, 0)))


def _s5_small_cotangents(dbk, r12, q12, xy, uv, xy0, uv0, da64):
    n = S5_STATE
    fold = lambda a: a[..., :n] + a[..., n:]
    fold_m = lambda a: a[..., n:] - a[..., :n]
    dbb_re = fold(r12[:, :S5_CH]) + dbk[..., :n]
    dbb_im = fold_m(r12[:, S5_CH:]) + dbk[..., n:]
    dpr, dpi = fold(q12[:, :CHUNK]), fold_m(q12[:, CHUNK:])
    dc_re = -fold_m(xy[:, :S5_CH]) - fold_m(xy0[:, :S5_CH])
    dc_im = -fold(xy[:, S5_CH:]) - fold(xy0[:, S5_CH:])
    dp1_re, dp1_im = -fold_m(uv[:, :CHUNK]), -fold(uv[:, CHUNK:])
    dp0_re, dp0_im = -fold_m(uv0[:, :CHUNK]), -fold(uv0[:, CHUNK:])
    zero = jnp.zeros((S5_GROUPS, n, 1), F32)
    dpw_re = (jnp.concatenate([(dpr[:, ::-1] + dp0_re).transpose(0, 2, 1), zero], axis=-1)
              + jnp.concatenate([zero, dp1_re.transpose(0, 2, 1)], axis=-1)).at[:, :, CHUNK].add(da64[:, :n])
    dpw_im = (jnp.concatenate([(dpi[:, ::-1] + dp0_im).transpose(0, 2, 1), zero], axis=-1)
              + jnp.concatenate([zero, dp1_im.transpose(0, 2, 1)], axis=-1)).at[:, :, CHUNK].add(da64[:, n:])
    return dbb_re, dbb_im, dc_re, dc_im, dpw_re, dpw_im


def _s5_scan_powers(a_re, a_im, log_dt, nsteps):
    dt = jnp.exp(log_dt)[:, None]
    steps = (CHUNK * (2.0 ** jnp.arange(8, dtype=F32)))[None, :, None]
    mag = jnp.exp((a_re * dt)[:, None, :] * steps)
    ang = (a_im * dt)[:, None, :] * steps
    re, im = mag * jnp.cos(ang), mag * jnp.sin(ang)
    del nsteps
    return jnp.concatenate([re, re], -1), jnp.concatenate([-im, im], -1)


def _build_toeplitz(kmat, tg_ref):
    lane = lax.broadcasted_iota(jnp.int32, (CHUNK, CHUNK * S5_CH), 1)
    srow = lax.broadcasted_iota(jnp.int32, (CHUNK, CHUNK * S5_CH), 0)
    keep = lane >= S5_CH * srow
    for h in range(S5_CH):
        row = jnp.broadcast_to(kmat[h:h + 1, :], (CHUNK, CHUNK * S5_CH))
        rolled = pltpu.roll(row, 0, 1, stride=S5_CH, stride_axis=0)
        tg_ref[h * CHUNK:(h + 1) * CHUNK, :] = jnp.where(keep, rolled, 0.0).astype(BF)


def _swap_halves(x):
    return pltpu.roll(x, S5_STATE, 1)


def _hi_lo(x):
    hi = x.astype(BF)
    return hi, (x - hi.astype(F32)).astype(BF)


def _dot3(dot, a, b):
    a_hi, a_lo = _hi_lo(a)
    b_hi, b_lo = _hi_lo(b)
    return dot(a_hi, b_hi) + dot(a_hi, b_lo) + dot(a_lo, b_hi)


def _build_state_tables(bbp_ref, pwr_ref, cnp_ref, pwf_ref, wst_ref, wofft_ref, w0_ref):
    lane = lax.broadcasted_iota(jnp.int32, (1, 2 * S5_STATE), 1)
    left = lane < S5_STATE
    pwr = pwr_ref[0]
    pwr_sw = _swap_halves(pwr)
    for hh in range(S5_CH):
        bb = bbp_ref[0, hh:hh + 1, :]
        bb_sw = _swap_halves(bb)
        wst_ref[hh * CHUNK:(hh + 1) * CHUNK, :] = (jnp.where(left, bb, bb_sw) * pwr
                                                   + jnp.where(left, -bb_sw, bb) * pwr_sw).astype(BF)
    cn = cnp_ref[0]
    cn_sw = _swap_halves(cn)
    c_a = jnp.where(left, cn, -cn_sw)
    c_b = jnp.where(left, -cn_sw, -cn)
    p_prev = pwf_ref[0, 0:1, :]
    for t in range(CHUNK):
        p = pwf_ref[0, t + 1:t + 2, :]
        w0_ref[t * S5_CH:(t + 1) * S5_CH, :] = c_a * p_prev + c_b * _swap_halves(p_prev)
        wofft_ref[t * S5_CH:(t + 1) * S5_CH, :] = (c_a * p + c_b * _swap_halves(p)).astype(BF)
        p_prev = p
    return _dot3(_dot_nt, bbp_ref[0], w0_ref[...])


def _reduce_table_cotangent(d_table, cnp_ref, pwf_ref, first_power):
    cn = cnp_ref[0]
    cn_sw = _swap_halves(cn)
    xa = jnp.zeros((S5_CH, 2 * S5_STATE), F32)
    ya = jnp.zeros((S5_CH, 2 * S5_STATE), F32)
    ru_rows, rv_rows = [], []
    for t in range(CHUNK):
        blk = d_table[t * S5_CH:(t + 1) * S5_CH, :]
        p = pwf_ref[0, t + first_power:t + first_power + 1, :]
        xa = xa + blk * p
        ya = ya + blk * _swap_halves(p)
        ru_rows.append(jnp.sum(blk * cn, axis=0, keepdims=True))
        rv_rows.append(jnp.sum(blk * cn_sw, axis=0, keepdims=True))
    return jnp.concatenate([xa, ya], axis=0), jnp.concatenate(ru_rows + rv_rows, axis=0)


def _s5_core_fwd(uflat, bbp, pwr, cnp, pwf, p1, p2):
    G, nc, W = uflat.shape
    nsteps = max(1, (nc - 1).bit_length())

    def body(u_ref, bbp_ref, pwr_ref, cnp_ref, pwf_ref, p1_ref, p2_ref, y_ref, h_ref, tg_ref, wst_ref, wofft_ref, w0_ref):
        _build_toeplitz(_build_state_tables(bbp_ref, pwr_ref, cnp_ref, pwf_ref, wst_ref, wofft_ref, w0_ref), tg_ref)
        u = u_ref[0]
        x = _dot(u, wst_ref[...])
        row = lax.broadcasted_iota(jnp.int32, x.shape, 0)
        d = 1
        for k in range(nsteps):
            sh = jnp.where(row >= d, pltpu.roll(x, d, 0), 0.0)
            x = x + p1_ref[0, k:k + 1, :] * sh + p2_ref[0, k:k + 1, :] * _swap_halves(sh)
            d *= 2
        h = jnp.where(row >= 1, pltpu.roll(x, 1, 0), 0.0)
        h_ref[0] = h
        y = _dot(u, tg_ref[...]) + _dot_nt(h.astype(BF), wofft_ref[...])
        y_ref[0] = y.astype(BF)

    spec_g = lambda a, b: pl.BlockSpec((1, a, b), lambda g: (g, 0, 0))
    return pl.pallas_call(
        body, name="s5_core_fwd", grid=(G,),
        in_specs=[spec_g(nc, W), spec_g(S5_CH, 2 * S5_STATE), spec_g(CHUNK, 2 * S5_STATE),
                  spec_g(S5_CH, 2 * S5_STATE), spec_g(CHUNK + 8, 2 * S5_STATE), spec_g(8, 2 * S5_STATE), spec_g(8, 2 * S5_STATE)],
        out_specs=[spec_g(nc, W), spec_g(nc, 2 * S5_STATE)],
        out_shape=[jax.ShapeDtypeStruct((G, nc, W), BF), jax.ShapeDtypeStruct((G, nc, 2 * S5_STATE), F32)],
        scratch_shapes=[pltpu.VMEM((W, W), BF), pltpu.VMEM((W, 2 * S5_STATE), BF), pltpu.VMEM((W, 2 * S5_STATE), BF),
                        pltpu.VMEM((W, 2 * S5_STATE), F32)],
        compiler_params=_params(("arbitrary",)),
    )(uflat, bbp, pwr, cnp, pwf, p1, p2)


def _s5_core_bwd(uflat, dyflat, hsave, bbp, pwr, cnp, pwf, p1, p2):
    G, nc, W = uflat.shape
    nsteps = max(1, (nc - 1).bit_length())

    def body(u_ref, dy_ref, h_ref, bbp_ref, pwr_ref, cnp_ref, pwf_ref, p1_ref, p2_ref,
             du_ref, dbk_ref, r12_ref, q12_ref, xy_ref, uv_ref, xy0_ref, uv0_ref, da_ref,
             tg_ref, flip_ref, wst_ref, wofft_ref, w0_ref):
        @pl.when(pl.program_id(0) == 0)
        def _():
            r = lax.broadcasted_iota(jnp.int32, (W, W), 0)
            c = lax.broadcasted_iota(jnp.int32, (W, W), 1)
            flip_ref[...] = (((r >> 6) == (c >> 6)) & ((r & (CHUNK - 1)) + (c & (CHUNK - 1)) == CHUNK - 1)).astype(BF)

        _build_toeplitz(_build_state_tables(bbp_ref, pwr_ref, cnp_ref, pwf_ref, wst_ref, wofft_ref, w0_ref), tg_ref)
        u = u_ref[0]
        dy = dy_ref[0]
        h = h_ref[0]
        gh = _dot(dy, wofft_ref[...])
        row = lax.broadcasted_iota(jnp.int32, gh.shape, 0)
        x = jnp.where(row < nc - 1, pltpu.roll(gh, nc - 1, 0), 0.0)
        d = 1
        for k in range(nsteps):
            sh = jnp.where(row < nc - d, pltpu.roll(x, nc - d, 0), 0.0)
            x = x + p1_ref[0, k:k + 1, :] * sh - p2_ref[0, k:k + 1, :] * _swap_halves(sh)
            d *= 2
        gs = x.astype(BF)
        du_ref[0] = (_dot_nt(dy, tg_ref[...]) + _dot_nt(gs, wst_ref[...])).astype(BF)

        dwst = _dot_tn(u, gs)
        pwr = pwr_ref[0]
        pwr_sw = _swap_halves(pwr)
        q1 = jnp.zeros((CHUNK, 2 * S5_STATE), F32)
        q2 = jnp.zeros((CHUNK, 2 * S5_STATE), F32)
        r1_rows, r2_rows = [], []
        for hh in range(S5_CH):
            blk = dwst[hh * CHUNK:(hh + 1) * CHUNK, :]
            r1_rows.append(jnp.sum(blk * pwr, axis=0, keepdims=True))
            r2_rows.append(jnp.sum(blk * pwr_sw, axis=0, keepdims=True))
            bb = bbp_ref[0, hh:hh + 1, :]
            q1 = q1 + blk * bb
            q2 = q2 + blk * _swap_halves(bb)
        r12_ref[0] = jnp.concatenate(r1_rows + r2_rows, axis=0)
        q12_ref[0] = jnp.concatenate([q1, q2], axis=0)

        xy_ref[0], uv_ref[0] = _reduce_table_cotangent(_dot_tn(dy, h.astype(BF)), cnp_ref, pwf_ref, 1)
        r1 = jnp.sum(x * h, axis=0, keepdims=True)
        r2 = jnp.sum(x * _swap_halves(h), axis=0, keepdims=True)
        da_ref[0] = jnp.concatenate([r1, r2, jnp.zeros((6, 2 * S5_STATE), F32)], axis=0)
        lane = lax.broadcasted_iota(jnp.int32, (CHUNK, W), 1)
        srow = lax.broadcasted_iota(jnp.int32, (CHUNK, W), 0)
        keep = lane < S5_CH * (srow + 1)
        ur = _dot(u, flip_ref[...]).astype(BF)
        dk_rows = []
        for hh in range(S5_CH):
            dt_h = _dot_tn(ur[:, hh * CHUNK:(hh + 1) * CHUNK], dy)
            back = pltpu.roll(dt_h, S5_CH, 1, stride=S5_CH, stride_axis=0)
            dk_rows.append(jnp.sum(jnp.where(keep, back, 0.0), axis=0, keepdims=True))
        dk = jnp.concatenate(dk_rows, axis=0)
        dbk_ref[0] = _dot3(_dot, dk, w0_ref[...])
        xy0_ref[0], uv0_ref[0] = _reduce_table_cotangent(_dot3(_dot_tn, dk, bbp_ref[0]), cnp_ref, pwf_ref, 0)

    spec_g = lambda a, b: pl.BlockSpec((1, a, b), lambda g: (g, 0, 0))
    small_c = jax.ShapeDtypeStruct((G, 2 * S5_CH, 2 * S5_STATE), F32)
    small_p = jax.ShapeDtypeStruct((G, 2 * CHUNK, 2 * S5_STATE), F32)
    return pl.pallas_call(
        body, name="s5_core_bwd", grid=(G,),
        in_specs=[spec_g(nc, W), spec_g(nc, W), spec_g(nc, 2 * S5_STATE), spec_g(S5_CH, 2 * S5_STATE),
                  spec_g(CHUNK, 2 * S5_STATE), spec_g(S5_CH, 2 * S5_STATE), spec_g(CHUNK + 8, 2 * S5_STATE),
                  spec_g(8, 2 * S5_STATE), spec_g(8, 2 * S5_STATE)],
        out_specs=[spec_g(nc, W), spec_g(S5_CH, 2 * S5_STATE), spec_g(2 * S5_CH, 2 * S5_STATE), spec_g(2 * CHUNK, 2 * S5_STATE),
                   spec_g(2 * S5_CH, 2 * S5_STATE), spec_g(2 * CHUNK, 2 * S5_STATE),
                   spec_g(2 * S5_CH, 2 * S5_STATE), spec_g(2 * CHUNK, 2 * S5_STATE), spec_g(8, 2 * S5_STATE)],
        out_shape=[jax.ShapeDtypeStruct((G, nc, W), BF), jax.ShapeDtypeStruct((G, S5_CH, 2 * S5_STATE), F32),
                   small_c, small_p, small_c, small_p, small_c, small_p, jax.ShapeDtypeStruct((G, 8, 2 * S5_STATE), F32)],
        scratch_shapes=[pltpu.VMEM((W, W), BF), pltpu.VMEM((W, W), BF), pltpu.VMEM((W, 2 * S5_STATE), BF),
                        pltpu.VMEM((W, 2 * S5_STATE), BF), pltpu.VMEM((W, 2 * S5_STATE), F32)],
        compiler_params=_params(("arbitrary",)),
    )(uflat, dyflat, hsave, bbp, pwr, cnp, pwf, p1, p2)


def _flat_hs(a, nc):
    return a.reshape(nc, CHUNK, S5_GROUPS, S5_CH).transpose(2, 0, 3, 1).reshape(S5_GROUPS, nc, CHUNK * S5_CH)


def _unflat_hs(a, nc):
    return a.reshape(S5_GROUPS, nc, S5_CH, CHUNK).transpose(1, 3, 0, 2).reshape(nc * CHUNK, D_MODEL)


def _flat_tk(a, nc):
    return a.reshape(nc, CHUNK, S5_GROUPS, S5_CH).transpose(2, 0, 1, 3).reshape(S5_GROUPS, nc, CHUNK * S5_CH)


def _unflat_tk(a, nc):
    return a.reshape(S5_GROUPS, nc, CHUNK, S5_CH).transpose(1, 2, 0, 3).reshape(nc * CHUNK, D_MODEL)


def _s5_post_fwd(yssm, proj, s5_d, w_glu, b_glu):
    L = yssm.shape[0]
    tl = min(TOKEN_TILE, L)

    def body(ys_ref, u_ref, z_ref, d_ref, wg_ref, bg_ref, o_ref):
        u = u_ref[...].astype(F32)
        a = _gelu(ys_ref[...].astype(F32) + d_ref[...] * u)
        y = a * _sigmoid(_dot(a.astype(BF), wg_ref[...]) + bg_ref[...])
        z = z_ref[...].astype(F32)
        o_ref[...] = (y * z * _sigmoid(z)).astype(BF)

    return pl.pallas_call(
        body, name="s5_post_fwd", grid=(L // tl,),
        in_specs=[_row_spec(tl, D_MODEL), _row_spec(tl, D_MODEL, 0), _row_spec(tl, D_MODEL, 1), _const_spec((1, D_MODEL)),
                  _const_spec((D_MODEL, D_MODEL)), _const_spec((1, D_MODEL))],
        out_specs=_row_spec(tl, D_MODEL),
        out_shape=jax.ShapeDtypeStruct((L, D_MODEL), BF),
        compiler_params=_params(("arbitrary",)),
    )(yssm, proj, proj, s5_d, w_glu, b_glu)


def _s5_post_bwd(dys5, yssm, proj, s5_d, w_glu, b_glu):
    L = yssm.shape[0]
    tl = min(TOKEN_TILE, L)

    def body(dy_ref, ys_ref, u_ref, z_ref, d_ref, wg_ref, bg_ref, dz_ref, dys_ref, a_ref, dgl_ref, dbg_ref, dd_ref):
        @pl.when(pl.program_id(0) == 0)
        def _():
            dbg_ref[...] = jnp.zeros_like(dbg_ref)
            dd_ref[...] = jnp.zeros_like(dd_ref)

        u = u_ref[...].astype(F32)
        y0 = ys_ref[...].astype(F32) + d_ref[...] * u
        a = _gelu(y0)
        a_bf = a.astype(BF)
        sg = _sigmoid(_dot(a_bf, wg_ref[...]) + bg_ref[...])
        y = a * sg
        z = z_ref[...].astype(F32)
        sz = _sigmoid(z)
        dout = dy_ref[...].astype(F32)
        dz_ref[...] = (dout * y * sz * (1.0 + z * (1.0 - sz))).astype(BF)
        dyv = dout * z * sz
        dgl = dyv * a * sg * (1.0 - sg)
        dgl_bf = dgl.astype(BF)
        da = dyv * sg + _dot_nt(dgl_bf, wg_ref[...])
        dy0 = da * _gelu_grad(y0)
        dbg_ref[...] += jnp.sum(dgl, axis=0, keepdims=True)
        dd_ref[...] += jnp.sum(dy0 * u, axis=0, keepdims=True)
        dys_ref[...] = dy0.astype(BF)
        a_ref[...] = a_bf
        dgl_ref[...] = dgl_bf

    big = jax.ShapeDtypeStruct((L, D_MODEL), BF)
    vec = jax.ShapeDtypeStruct((1, D_MODEL), F32)
    return pl.pallas_call(
        body, name="s5_post_bwd", grid=(L // tl,),
        in_specs=[_row_spec(tl, D_MODEL), _row_spec(tl, D_MODEL), _row_spec(tl, D_MODEL, 0), _row_spec(tl, D_MODEL, 1),
                  _const_spec((1, D_MODEL)), _const_spec((D_MODEL, D_MODEL)), _const_spec((1, D_MODEL))],
        out_specs=[_row_spec(tl, D_MODEL)] * 4 + [_const_spec((1, D_MODEL))] * 2,
        out_shape=[big, big, big, big, vec, vec],
        compiler_params=_params(("arbitrary",)),
    )(dys5, yssm, proj, proj, s5_d, w_glu, b_glu)


def _cumsum_rows(a):
    row = lax.broadcasted_iota(jnp.int32, a.shape, 0)
    d = 1
    while d < a.shape[0]:
        a = a + jnp.where(row >= d, pltpu.roll(a, d, 0), 0.0)
        d *= 2
    return a


def _rev_cumsum_rows(a):
    n = a.shape[0]
    row = lax.broadcasted_iota(jnp.int32, a.shape, 0)
    d = 1
    while d < n:
        a = a + jnp.where(row < n - d, pltpu.roll(a, n - d, 0), 0.0)
        d *= 2
    return a


def _ssd_fill_padded(first, xs_ref, bc_ref, hx_ref, hb_ref, xp_ref, tl):
    hal = jnp.concatenate([hx_ref[...], hb_ref[...]], axis=1)
    xp_ref[0:8, :] = jnp.where(first, 0.0, hal)
    xp_ref[8:8 + tl, 0:1024] = xs_ref[...]
    xp_ref[8:8 + tl, 1024:2048] = bc_ref[...]


def _ssd_conv_fwd(first, xs_ref, bc_ref, hx_ref, hb_ref, cw_ref, cb_ref, xp_ref, tl):
    _ssd_fill_padded(first, xs_ref, bc_ref, hx_ref, hb_ref, xp_ref, tl)
    pre = cb_ref[...] + cw_ref[0:1, :] * xp_ref[5:5 + tl, :]
    for k in range(1, 4):
        pre = pre + cw_ref[k:k + 1, :] * xp_ref[5 + k:5 + k + tl, :]
    return pre


def _onehot_lane(h):
    return (lax.broadcasted_iota(jnp.int32, (1, LANES), 1) == h).astype(F32)


def _dot_exact(x, e):
    hi = x.astype(BF)
    r = x - hi.astype(F32)
    mid = r.astype(BF)
    lo = (r - mid.astype(F32)).astype(BF)
    return _dot(hi, e) + _dot(mid, e) + _dot(lo, e)


def _head_expand_matrices():
    e = lax.broadcasted_iota(jnp.int32, (LANES, D_MODEL), 0) == (lax.broadcasted_iota(jnp.int32, (LANES, D_MODEL), 1) >> 6)
    et = (lax.broadcasted_iota(jnp.int32, (D_MODEL, LANES), 0) >> 6) == lax.broadcasted_iota(jnp.int32, (D_MODEL, LANES), 1)
    return e.astype(BF), et.astype(BF)


def _group_masks():
    r64 = lax.broadcasted_iota(jnp.int32, (4 * CHUNK, CHUNK), 0)
    causal4 = (r64 & (CHUNK - 1)) >= lax.broadcasted_iota(jnp.int32, (4 * CHUNK, CHUNK), 1)
    r256 = lax.broadcasted_iota(jnp.int32, (4 * CHUNK, 4 * SSD_HEAD_DIM), 0)
    same = (r256 >> 6) == (lax.broadcasted_iota(jnp.int32, (4 * CHUNK, 4 * SSD_HEAD_DIM), 1) >> 6)
    return causal4, same


def _group_decay(acs, acs_t, j, causal4):
    col = jnp.concatenate([acs[:, 4 * j + hh:4 * j + hh + 1] for hh in range(4)], axis=0)
    rowv = jnp.concatenate([jnp.broadcast_to(acs_t[4 * j + hh:4 * j + hh + 1, :], (CHUNK, CHUNK)) for hh in range(4)], axis=0)
    return jnp.where(causal4, jnp.exp(col - rowv), 0.0)


def _group_last_decay(acs_t, j):
    return jnp.concatenate([jnp.broadcast_to(jnp.exp(acs_t[4 * j + hh:4 * j + hh + 1, CHUNK - 1:CHUNK]), (SSD_HEAD_DIM, 1))
                            for hh in range(4)], axis=0)


def _fold_heads(r):
    return r[0:CHUNK] + r[CHUNK:2 * CHUNK] + r[2 * CHUNK:3 * CHUNK] + r[3 * CHUNK:4 * CHUNK]


def _ssd_specs_in(tl, nt, rev):
    t_of = (lambda i: nt - 1 - i) if rev else (lambda i: i)
    rows = lambda w, col: pl.BlockSpec((tl, w), lambda i: (t_of(i), col))
    halo = lambda col: pl.BlockSpec((8, 1024), lambda i: (jnp.maximum(t_of(i) * (tl // 8) - 1, 0), col))
    return t_of, rows, halo


def _ssd_fwd(proj, pdt, conv_w, conv_b, dt_bias, a_log, ssd_d, norm_w):
    L = proj.shape[0]
    tl = min(TOKEN_TILE, L)
    nt, ncl = L // tl, tl // CHUNK
    _, rows, halo = _ssd_specs_in(tl, nt, False)

    def body(xs_ref, bc_ref, hx_ref, hb_ref, dt_ref, z_ref, cw_ref, cb_ref, dtb_ref, al_ref, dd_ref, nw_ref,
             y_ref, ypre_ref, st_ref, pre_ref, xp_ref, xbc_ref, dts_ref, hst_ref):
        i = pl.program_id(0)

        @pl.when(i == 0)
        def _():
            hst_ref[...] = jnp.zeros_like(hst_ref)

        pre = _ssd_conv_fwd(i == 0, xs_ref, bc_ref, hx_ref, hb_ref, cw_ref, cb_ref, xp_ref, tl)
        pre_ref[...] = pre
        xbc_ref[...] = pre * _sigmoid(pre)
        dts_ref[...] = _softplus(dt_ref[...] + dtb_ref[...])
        a_neg = -jnp.exp(al_ref[...])
        e16, _ = _head_expand_matrices()
        causal4, same = _group_masks()
        dd_x = _dot_exact(jnp.broadcast_to(dd_ref[...], (8, LANES)), e16)[0:1, :]

        def chunk(c, carry):
            r0 = pl.multiple_of(c * CHUNK, CHUNK)
            xbc = xbc_ref[pl.ds(r0, CHUNK), :]
            dtc = dts_ref[pl.ds(r0, CHUNK), :]
            acs = _cumsum_rows(dtc * a_neg)
            acs_t = acs.T
            acs_x = _dot_exact(acs, e16)
            xs = xbc[:, 0:1024]
            xd = xs * _dot_exact(dtc, e16)
            xd_bf = xd.astype(BF)
            xdd = (xd * jnp.exp(acs_x[CHUNK - 1:CHUNK, :] - acs_x)).astype(BF)
            e_x = jnp.exp(acs_x)
            for j in range(SSD_GROUPS):
                sl = slice(256 * j, 256 * (j + 1))
                bj = xbc[:, 1024 + 128 * j:1024 + 128 * (j + 1)].astype(BF)
                cj = xbc[:, 1536 + 128 * j:1536 + 128 * (j + 1)].astype(BF)
                g = _dot_nt(cj, bj)
                hj = hst_ref[sl, :]
                zj = _dot_nt(cj, hj.astype(BF))
                sc = (jnp.concatenate([g] * 4, axis=0) * _group_decay(acs, acs_t, j, causal4)).astype(BF)
                yd = _fold_heads(jnp.where(same, _dot(sc, xd_bf[:, sl]), 0.0))
                ypre_ref[pl.ds(r0, CHUNK), sl] = yd + e_x[:, sl] * zj + dd_x[:, sl] * xs[:, sl]
                st_ref[c, sl, :] = hj
                hst_ref[sl, :] = _group_last_decay(acs_t, j) * hj + _dot_tn(xdd[:, sl], bj)
            return carry

        lax.fori_loop(0, ncl, chunk, 0)
        z = z_ref[...]
        gg = ypre_ref[...] * z * _sigmoid(z)
        for j in range(SSD_GROUPS):
            seg = gg[:, 256 * j:256 * (j + 1)]
            r = lax.rsqrt(jnp.mean(seg * seg, axis=-1, keepdims=True) + EPS)
            y_ref[:, 256 * j:256 * (j + 1)] = (seg * r * nw_ref[:, 256 * j:256 * (j + 1)]).astype(BF)

    nc = L // CHUNK
    return pl.pallas_call(
        body, name="ssd_fwd", grid=(nt,),
        in_specs=[rows(1024, 1), rows(1024, 2), halo(1), halo(2), rows(LANES, 0), rows(1024, 0),
                  _const_spec((4, 2048)), _const_spec((1, 2048)), _const_spec((1, LANES)), _const_spec((1, LANES)),
                  _const_spec((1, LANES)), _const_spec((1, D_MODEL))],
        out_specs=[_row_spec(tl, D_MODEL), _row_spec(tl, D_MODEL), pl.BlockSpec((ncl, 1024, SSD_STATE), lambda i: (i, 0, 0)),
                   _row_spec(tl, 2048)],
        out_shape=[jax.ShapeDtypeStruct((L, D_MODEL), BF), jax.ShapeDtypeStruct((L, D_MODEL), F32),
                   jax.ShapeDtypeStruct((nc, 1024, SSD_STATE), F32), jax.ShapeDtypeStruct((L, 2048), F32)],
        scratch_shapes=[pltpu.VMEM((tl + 8, 2048), F32), pltpu.VMEM((tl, 2048), F32), pltpu.VMEM((tl, LANES), F32),
                        pltpu.VMEM((1024, SSD_STATE), F32)],
        compiler_params=_params(("arbitrary",)),
    )(proj, proj, proj, proj, pdt, proj, conv_w, conv_b, dt_bias, a_log, ssd_d, norm_w)


def _ssd_bwd(dyssd, ypre, proj, pdt, states, pre_act, conv_w, dt_bias, a_log, ssd_d, norm_w):
    L = proj.shape[0]
    tl = min(TOKEN_TILE, L)
    nt, ncl = L // tl, tl // CHUNK
    t_of, rows, halo = _ssd_specs_in(tl, nt, True)

    def body(dy_ref, ypre_ref, z_ref, xs_ref, bc_ref, hx_ref, hb_ref, dt_ref, st_ref, pre_ref, cw_ref, dtb_ref, al_ref,
             dd_ref, nw_ref,
             dxbc_ref, ddt_ref, dz_ref, dcw_ref, dcb_ref, ddtb_ref, dal_ref, ddd_ref, dnw_ref,
             xp_ref, xbc_ref, dts_ref, dyp_ref, dxs_ref, ddts_ref, dp_ref, dh_ref):
        i = pl.program_id(0)

        @pl.when(i == 0)
        def _():
            for r in (dcw_ref, dcb_ref, ddtb_ref, dal_ref, ddd_ref, dnw_ref, dh_ref):
                r[...] = jnp.zeros_like(r)
            dp_ref[tl:tl + 8, :] = jnp.zeros((8, 2048), F32)

        _ssd_fill_padded(t_of(i) == 0, xs_ref, bc_ref, hx_ref, hb_ref, xp_ref, tl)
        pre = pre_ref[...]
        xbc_ref[...] = pre * _sigmoid(pre)
        dts_ref[...] = _softplus(dt_ref[...] + dtb_ref[...])
        a_neg = -jnp.exp(al_ref[...])

        ypre = ypre_ref[...]
        z = z_ref[...]
        sz = _sigmoid(z)
        gg = ypre * z * sz
        dout = dy_ref[...]
        for j in range(SSD_GROUPS):
            sl = slice(256 * j, 256 * (j + 1))
            seg = gg[:, sl]
            r = lax.rsqrt(jnp.mean(seg * seg, axis=-1, keepdims=True) + EPS)
            gh = seg * r
            dnw_ref[:, sl] += jnp.sum(dout[:, sl] * gh, axis=0, keepdims=True)
            gw = dout[:, sl] * nw_ref[:, sl]
            dgg = r * (gw - gh * jnp.mean(gw * gh, axis=-1, keepdims=True))
            dyp_ref[:, sl] = dgg * z[:, sl] * sz[:, sl]
            dz_ref[:, sl] = (dgg * ypre[:, sl] * sz[:, sl] * (1.0 + z[:, sl] * (1.0 - sz[:, sl]))).astype(BF)

        e16, e16t = _head_expand_matrices()
        causal4, same = _group_masks()
        dd_x = _dot_exact(jnp.broadcast_to(dd_ref[...], (8, LANES)), e16)[0:1, :]
        last_row = (lax.broadcasted_iota(jnp.int32, (CHUNK, 1), 0) == CHUNK - 1).astype(F32)
        sel_rows = lax.broadcasted_iota(jnp.int32, (4 * CHUNK, LANES), 0) >> 6
        sel_lanes = lax.broadcasted_iota(jnp.int32, (4 * CHUNK, LANES), 1)

        def chunk(k, carry):
            dal_acc, ddx_acc = carry
            c = ncl - 1 - k
            r0 = pl.multiple_of(c * CHUNK, CHUNK)
            xbc = xbc_ref[pl.ds(r0, CHUNK), :]
            dtc = dts_ref[pl.ds(r0, CHUNK), :]
            dyp = dyp_ref[pl.ds(r0, CHUNK), :]
            acs = _cumsum_rows(dtc * a_neg)
            acs_t = acs.T
            acs_x = _dot_exact(acs, e16)
            dt_x = _dot_exact(dtc, e16)
            xs = xbc[:, 0:1024]
            xd = xs * dt_x
            xd_bf = xd.astype(BF)
            dec_x = jnp.exp(acs_x[CHUNK - 1:CHUNK, :] - acs_x)
            xdd = xd * dec_x
            xdd_bf = xdd.astype(BF)
            dz = dyp * jnp.exp(acs_x)
            dz_bf = dz.astype(BF)
            ddx_acc = ddx_acc + jnp.sum(dyp * xs, axis=0, keepdims=True)
            dacs = jnp.zeros((CHUNK, LANES), F32)
            hsum = jnp.zeros((1, LANES), F32)
            p1_l, p2_l, p3_l = [], [], []
            for j in range(SSD_GROUPS):
                sl = slice(256 * j, 256 * (j + 1))
                bj = xbc[:, 1024 + 128 * j:1024 + 128 * (j + 1)].astype(BF)
                cj = xbc[:, 1536 + 128 * j:1536 + 128 * (j + 1)].astype(BF)
                g = _dot_nt(cj, bj)
                hj = st_ref[c, sl, :]
                hj_bf = hj.astype(BF)
                dhj = dh_ref[sl, :]
                dhj_bf = dhj.astype(BF)
                zj = _dot_nt(cj, hj_bf)
                qj = _dot_nt(bj, dhj_bf)
                lm = _group_decay(acs, acs_t, j, causal4)
                sc = jnp.concatenate([g] * 4, axis=0) * lm
                sc_bf = sc.astype(BF)
                dym = jnp.where(same, jnp.concatenate([dyp[:, sl]] * 4, axis=0), 0.0).astype(BF)
                dsc = _dot_nt(dym, xd_bf[:, sl])
                dxd = _dot_tn(sc_bf, dym) + qj * dec_x[:, sl]
                m = dsc * sc
                dg_bf = _fold_heads(dsc * lm).astype(BF)
                rs = jnp.sum(m, axis=1, keepdims=True)
                e2 = dhj * hj
                for hh in range(4):
                    oh = _onehot_lane(4 * j + hh)
                    dacs = dacs + oh * rs[CHUNK * hh:CHUNK * (hh + 1)]
                    hsum = hsum + oh * jnp.sum(jnp.sum(e2[64 * hh:64 * (hh + 1)], axis=0, keepdims=True), axis=1, keepdims=True)
                sel = (sel_rows + 4 * j == sel_lanes).astype(BF)
                hi = m.astype(BF)
                rem = m - hi.astype(F32)
                mid = rem.astype(BF)
                lo = (rem - mid.astype(F32)).astype(BF)
                dacs = dacs - (_dot_tn(hi, sel) + _dot_tn(mid, sel) + _dot_tn(lo, sel))
                p1_l.append(dz[:, sl] * zj)
                p2_l.append(qj * xdd[:, sl])
                p3_l.append(dxd * xs[:, sl])
                dxs_ref[pl.ds(r0, CHUNK), sl] = dd_x[:, sl] * dyp[:, sl] + dxd * dt_x[:, sl]
                dxs_ref[pl.ds(r0, CHUNK), 1536 + 128 * j:1536 + 128 * (j + 1)] = _dot(dg_bf, bj) + _dot(dz_bf[:, sl], hj_bf)
                dxs_ref[pl.ds(r0, CHUNK), 1024 + 128 * j:1024 + 128 * (j + 1)] = _dot_tn(dg_bf, cj) + _dot(xdd_bf[:, sl], dhj_bf)
                dh_ref[sl, :] = _group_last_decay(acs_t, j) * dhj + _dot_tn(dz_bf[:, sl], cj)
            stacked = jnp.concatenate([jnp.concatenate(p1_l, axis=1), jnp.concatenate(p2_l, axis=1), jnp.concatenate(p3_l, axis=1)], axis=0)
            red = _dot_exact(stacked, e16t)
            r1, r2, ddtc = red[0:CHUNK], red[CHUNK:2 * CHUNK], red[2 * CHUNK:3 * CHUNK]
            tot = jnp.sum(r2, axis=0, keepdims=True) + jnp.exp(acs[CHUNK - 1:CHUNK, :]) * hsum
            da = _rev_cumsum_rows(dacs + r1 - r2 + last_row * tot)
            ddts_ref[pl.ds(r0, CHUNK), :] = ddtc + da * a_neg
            dal_acc = dal_acc + jnp.sum(da * dtc, axis=0, keepdims=True)
            return dal_acc, ddx_acc

        dal_acc, ddx_acc = lax.fori_loop(0, ncl, chunk, (jnp.zeros((1, LANES), F32), jnp.zeros((1, D_MODEL), F32)))
        dal_ref[...] += dal_acc * a_neg
        ddd_ref[...] += _dot_exact(jnp.broadcast_to(ddx_acc, (8, D_MODEL)), e16t)[0:1, :]
        ddt_raw = ddts_ref[...] * _sigmoid(dt_ref[...] + dtb_ref[...])
        ddt_ref[...] = ddt_raw
        ddtb_ref[...] += jnp.sum(ddt_raw, axis=0, keepdims=True)

        pre = pre_ref[...]
        sp = _sigmoid(pre)
        dpre = dxs_ref[...] * sp * (1.0 + pre * (1.0 - sp))
        dp_ref[0:tl, :] = dpre
        dcb_ref[...] += jnp.sum(dpre, axis=0, keepdims=True)
        dx = jnp.zeros((tl, 2048), F32)
        for k in range(4):
            dcw_ref[k:k + 1, :] += jnp.sum(dpre * xp_ref[5 + k:5 + k + tl, :], axis=0, keepdims=True)
            dx = dx + cw_ref[k:k + 1, :] * dp_ref[3 - k:3 - k + tl, :]
        dxbc_ref[...] = dx.astype(BF)
        dp_ref[tl:tl + 8, :] = dp_ref[0:8, :]

    vec = lambda w: jax.ShapeDtypeStruct((1, w), F32)
    rrow = lambda w: pl.BlockSpec((tl, w), lambda i: (t_of(i), 0))
    return pl.pallas_call(
        body, name="ssd_bwd", grid=(nt,),
        in_specs=[rrow(D_MODEL), rrow(D_MODEL), rows(1024, 0), rows(1024, 1), rows(1024, 2), halo(1), halo(2), rows(LANES, 0),
                  pl.BlockSpec((ncl, 1024, SSD_STATE), lambda i: (t_of(i), 0, 0)), rrow(2048),
                  _const_spec((4, 2048)), _const_spec((1, LANES)), _const_spec((1, LANES)),
                  _const_spec((1, LANES)), _const_spec((1, D_MODEL))],
        out_specs=[rrow(2048), rrow(LANES), rrow(D_MODEL), _const_spec((8, 2048)), _const_spec((1, 2048)),
                   _const_spec((1, LANES)), _const_spec((1, LANES)), _const_spec((1, LANES)), _const_spec((1, D_MODEL))],
        out_shape=[jax.ShapeDtypeStruct((L, 2048), BF), jax.ShapeDtypeStruct((L, LANES), F32), jax.ShapeDtypeStruct((L, D_MODEL), BF),
                   jax.ShapeDtypeStruct((8, 2048), F32), vec(2048), vec(LANES), vec(LANES), vec(LANES), vec(D_MODEL)],
        scratch_shapes=[pltpu.VMEM((tl + 8, 2048), F32), pltpu.VMEM((tl, 2048), F32),
                        pltpu.VMEM((tl, LANES), F32), pltpu.VMEM((tl, D_MODEL), F32), pltpu.VMEM((tl, 2048), F32),
                        pltpu.VMEM((tl, LANES), F32), pltpu.VMEM((tl + 8, 2048), F32), pltpu.VMEM((1024, SSD_STATE), F32)],
        compiler_params=_params(("arbitrary",)),
    )(dyssd, ypre, proj, proj, proj, proj, proj, pdt, states, pre_act, conv_w, dt_bias, a_log, ssd_d, norm_w)


def _head_fwd_bwd(x, ys5, yssd, p, target, w_out, w_gate, w_proj, ple_nw, fin_nw):
    L = x.shape[0]
    tl = min(TOKEN_TILE, L)
    inv_d = 1.0 / D_MODEL

    def body(x_ref, ys_ref, yd_ref, p_ref, t_ref, wo_ref, wg_ref, wp_ref, pnw_ref, fnw_ref,
             loss_ref, dys_ref, dyd_ref, dh1_ref, n2_ref, dgl_ref, dpp_ref, dpnw_ref, dfnw_ref):
        @pl.when(pl.program_id(0) == 0)
        def _():
            loss_ref[...] = jnp.zeros_like(loss_ref)
            dpnw_ref[...] = jnp.zeros_like(dpnw_ref)
            dfnw_ref[...] = jnp.zeros_like(dfnw_ref)

        h1 = x_ref[...] + _dot(ys_ref[...], wo_ref[0:1024, :]) + _dot(yd_ref[...], wo_ref[1024:2048, :])
        r1 = lax.rsqrt(jnp.mean(h1 * h1, axis=-1, keepdims=True) + EPS)
        hh1 = h1 * r1
        n2 = (hh1 * pnw_ref[...]).astype(BF)
        gate = _sigmoid(_dot(n2, wg_ref[...]))
        pp = _dot(p_ref[...].astype(BF), wp_ref[...])
        h2 = h1 + pp * gate
        r2 = lax.rsqrt(jnp.mean(h2 * h2, axis=-1, keepdims=True) + EPS)
        hh2 = h2 * r2
        err = hh2 * fnw_ref[...] - t_ref[...]
        loss_ref[...] += 0.5 * inv_d * jnp.sum(err * err)
        dyo = err * inv_d
        dfnw_ref[...] += jnp.sum(dyo * hh2, axis=0, keepdims=True)
        g2 = dyo * fnw_ref[...]
        dh2 = r2 * (g2 - hh2 * jnp.mean(g2 * hh2, axis=-1, keepdims=True))
        dpp_ref[...] = (dh2 * gate).astype(BF)
        dgl = (dh2 * pp * gate * (1.0 - gate)).astype(BF)
        dgl_ref[...] = dgl
        n2_ref[...] = n2
        dn2 = _dot_nt(dgl, wg_ref[...])
        dpnw_ref[...] += jnp.sum(dn2 * hh1, axis=0, keepdims=True)
        g1 = dn2 * pnw_ref[...]
        dh1 = dh2 + r1 * (g1 - hh1 * jnp.mean(g1 * hh1, axis=-1, keepdims=True))
        dh1_ref[...] = dh1
        dh1_bf = dh1.astype(BF)
        dys_ref[...] = _dot_nt(dh1_bf, wo_ref[0:1024, :]).astype(BF)
        dyd_ref[...] = _dot_nt(dh1_bf, wo_ref[1024:2048, :])

    big = jax.ShapeDtypeStruct((L, D_MODEL), BF)
    vec = jax.ShapeDtypeStruct((1, D_MODEL), F32)
    return pl.pallas_call(
        body, name="head_fwd_bwd", grid=(L // tl,),
        in_specs=[_row_spec(tl, D_MODEL), _row_spec(tl, D_MODEL), _row_spec(tl, D_MODEL), _row_spec(tl, 256), _row_spec(tl, D_MODEL),
                  _const_spec((2048, D_MODEL)), _const_spec((D_MODEL, D_MODEL)), _const_spec((256, D_MODEL)),
                  _const_spec((1, D_MODEL)), _const_spec((1, D_MODEL))],
        out_specs=[_const_spec((8, LANES)), _row_spec(tl, D_MODEL), _row_spec(tl, D_MODEL), _row_spec(tl, D_MODEL),
                   _row_spec(tl, D_MODEL), _row_spec(tl, D_MODEL), _row_spec(tl, D_MODEL), _const_spec((1, D_MODEL)), _const_spec((1, D_MODEL))],
        out_shape=[jax.ShapeDtypeStruct((8, LANES), F32), big, jax.ShapeDtypeStruct((L, D_MODEL), F32),
                   jax.ShapeDtypeStruct((L, D_MODEL), F32), big, big, big, vec, vec],
        compiler_params=_params(("arbitrary",)),
    )(x, ys5, yssd, p, target, w_out, w_gate, w_proj, ple_nw, fin_nw)


def _pad_lanes(v):
    return jnp.pad(v.reshape(1, -1), ((0, 0), (0, LANES - v.size)))


def _local_step(x, p, target, w):
    L = x.shape[0]
    nc = L // CHUNK
    nsteps = max(1, (nc - 1).bit_length())
    w_in_t = w["w_in"]
    w_main = w_in_t[:D_MAIN]
    w_dt = jnp.pad(w_in_t[D_MAIN:], ((0, LANES - SSD_HEADS), (0, 0)))
    norm_w = w["norm_w"].reshape(1, -1)
    s5_d = w["s5_D"].reshape(1, -1)
    b_glu = w["s5_b_glu"].reshape(1, -1)
    conv_b = w["conv_b"].reshape(1, -1)
    dtb, alog, ssd_d = _pad_lanes(w["dt_bias"]), _pad_lanes(w["A_log"]), _pad_lanes(w["ssd_D"])
    ssd_nw = w["ssd_norm_w"].reshape(1, -1)
    ple_nw = w["ple_norm_w"].reshape(1, -1)
    fin_nw = w["final_norm_w"].reshape(1, -1)

    s5_args = (w["s5_A_re"], w["s5_A_im"], w["s5_log_dt"], w["s5_B_re"], w["s5_B_im"], w["s5_C_re"], w["s5_C_im"])
    small, small_vjp = jax.vjp(_s5_discretise, *s5_args)
    bbp, pwr, cnp, pwf = _s5_table_factors(*small)
    p1, p2 = _s5_scan_powers(w["s5_A_re"], w["s5_A_im"], w["s5_log_dt"], nsteps)

    hn, proj, pssd, pdt = _in_proj_fwd(x, norm_w, w_main, w_dt)
    uflat = _flat_hs(proj[:, :D_MODEL], nc)
    yflat, hsave = _s5_core_fwd(uflat, bbp, pwr, cnp, pwf, p1, p2)
    yssm = _unflat_tk(yflat, nc)
    ys5 = _s5_post_fwd(yssm, proj, s5_d, w["s5_w_glu"], b_glu)
    yssd, ypre, states, pre_act = _ssd_fwd(pssd, pdt, w["conv_w"], conv_b, dtb, alog, ssd_d, ssd_nw)
    (loss8, dys5, dyssd, dh1, n2, dgl2, dpp, g_ple_nw, g_fin_nw) = _head_fwd_bwd(
        x, ys5, yssd, p, target, w["w_out"], w["w_ple_gate"], w["w_ple_proj"], ple_nw, fin_nw)

    (dxbc, ddt, dzd, g_cw, g_cb, g_dtb, g_alog, g_ssd_d, g_ssd_nw) = _ssd_bwd(
        dyssd, ypre, pssd, pdt, states, pre_act, w["conv_w"], dtb, alog, ssd_d, ssd_nw)
    dzs, dyssm, a_glu, dgl1, g_bglu, g_s5d = _s5_post_bwd(dys5, yssm, proj, s5_d, w["s5_w_glu"], b_glu)
    duflat, dbk, r12, q12, xy, uv, xy0, uv0, da8 = _s5_core_bwd(uflat, _flat_tk(dyssm, nc), hsave, bbp, pwr, cnp, pwf, p1, p2)
    da64 = jnp.concatenate([da8[:, 0, :S5_STATE] + da8[:, 0, S5_STATE:], da8[:, 1, S5_STATE:] - da8[:, 1, :S5_STATE]], axis=-1)
    g_s5 = small_vjp(_s5_small_cotangents(dbk, r12, q12, xy, uv, xy0, uv0, da64))
    gx, du, g_norm_w = _in_proj_bwd(x, norm_w, dh1, _unflat_hs(duflat, nc), dyssm, s5_d, dzs, dzd, dxbc, ddt, w_main, w_dt)

    g_w_in = jnp.concatenate([
        _matmul_tn(du, hn, "dw_in_u"), _matmul_tn(dzs, hn, "dw_in_zs"), _matmul_tn(dzd, hn, "dw_in_zd"),
        _matmul_tn(dxbc, hn, "dw_in_xbc"), _matmul_tn(ddt, hn, "dw_in_dt")[:SSD_HEADS]], axis=0)
    grads = {
        "norm_w": g_norm_w, "w_in": g_w_in,
        "s5_A_re": g_s5[0], "s5_A_im": g_s5[1], "s5_log_dt": g_s5[2], "s5_B_re": g_s5[3], "s5_B_im": g_s5[4],
        "s5_C_re": g_s5[5], "s5_C_im": g_s5[6], "s5_D": g_s5d, "s5_w_glu": _matmul_tn(a_glu, dgl1, "dw_glu"), "s5_b_glu": g_bglu,
        "conv_w": g_cw[:4], "conv_b": g_cb, "dt_bias": g_dtb[:, :SSD_HEADS], "A_log": g_alog[:, :SSD_HEADS],
        "ssd_D": g_ssd_d[:, :SSD_HEADS], "ssd_norm_w": g_ssd_nw,
        "w_out": jnp.concatenate([_matmul_tn(ys5, dh1, "dw_out_s5"), _matmul_tn(yssd, dh1, "dw_out_ssd")], axis=0),
        "ple_norm_w": g_ple_nw, "w_ple_gate": _matmul_tn(n2, dgl2, "dw_gate"), "w_ple_proj": _matmul_tn(p, dpp, "dw_proj"),
        "final_norm_w": g_fin_nw,
    }
    return loss8[0, 0], gx, grads


WEIGHTS = ("norm_w", "w_in", "s5_A_re", "s5_A_im", "s5_log_dt", "s5_B_re", "s5_B_im", "s5_C_re", "s5_C_im", "s5_D", "s5_w_glu",
           "s5_b_glu", "conv_w", "conv_b", "dt_bias", "A_log", "ssd_D", "ssd_norm_w", "w_out", "ple_norm_w", "w_ple_gate",
           "w_ple_proj", "final_norm_w")
BIG = {"w_in": ((1284, 1024), 0), "s5_w_glu": ((256, 1024), 0), "w_out": ((512, 1024), 0), "w_ple_gate": ((256, 1024), 0),
       "w_ple_proj": ((256, 256), 1)}
SMALL = {"norm_w": (1024,), "s5_A_re": (64, 64), "s5_A_im": (64, 64), "s5_log_dt": (64,), "s5_B_re": (64, 64, 16),
         "s5_B_im": (64, 64, 16), "s5_C_re": (64, 16, 64), "s5_C_im": (64, 16, 64), "s5_D": (1024,), "s5_b_glu": (1024,),
         "conv_w": (4, 2048), "conv_b": (2048,), "dt_bias": (16,), "A_log": (16,), "ssd_D": (16,), "ssd_norm_w": (1024,),
         "ple_norm_w": (1024,), "final_norm_w": (1024,)}
BIG_ROWS = {n: s[0] * s[1] // LANES for n, (s, _) in BIG.items()}
BIG_ROWS_TOTAL = sum(BIG_ROWS.values())
SMALL_TOTAL = sum(math.prod(s) for s in SMALL.values())
SMALL_PIECE_ROWS = -(-SMALL_TOTAL // (N_CHIPS * 16 * LANES)) * 16
HALF_ROWS = (BIG_ROWS_TOTAL + SMALL_PIECE_ROWS) // 2
SMALL_ROW0 = BIG_ROWS_TOTAL - HALF_ROWS


def _mesh_pos():
    return lax.axis_index("x"), lax.axis_index("y"), lax.axis_index("c")


def _other_chips(x, y):
    return [(1 - x, y), (x, 1 - y), (1 - x, 1 - y)]


def _comm_params():
    return pltpu.CompilerParams(has_side_effects=True)


def _all_gather_chips(wpack, cw):
    half = wpack.shape[0] // 2

    def body(w_ref, c_ref, wo_ref, co_ref, send_sems, recv_sems, fwd_send, fwd_recv, loc_sems):
        x, y, c = _mesh_pos()
        me = 2 * x + y
        sib = (x, y, 1 - c)
        mine = pl.ds(c * half, half)
        theirs = pl.ds((1 - c) * half, half)
        others = _other_chips(x, y)
        loc = [pltpu.make_async_copy(c_ref, co_ref.at[me], loc_sems.at[0])]
        for cp in loc:
            cp.start()

        def from_chip(k, chip, dev):
            return pltpu.make_async_remote_copy(w_ref.at[mine], wo_ref.at[chip, mine], send_sems.at[2 * k], recv_sems.at[2 * k],
                                                device_id=dev, device_id_type=MESH)

        def conv_from(k, chip, dev):
            return pltpu.make_async_remote_copy(c_ref, co_ref.at[chip], send_sems.at[2 * k + 1], recv_sems.at[2 * k + 1],
                                                device_id=dev, device_id_type=MESH)

        def passed(k, chip, rows):
            return pltpu.make_async_remote_copy(wo_ref.at[chip, rows], wo_ref.at[chip, rows], fwd_send.at[k], fwd_recv.at[k],
                                                device_id=sib, device_id_type=MESH)

        sends = []
        for k, (px, py) in enumerate(others):
            sends += [from_chip(k, me, (px, py, c)), conv_from(k, me, (px, py, c))]
        for cp in sends:
            cp.start()
        fwds = []
        for k, (px, py) in enumerate(others):
            chip = 2 * px + py
            from_chip(k, chip, (px, py, c)).wait_recv()
            fwds.append(passed(k, chip, mine))
            fwds[-1].start()
        for k, (px, py) in enumerate(others):
            chip = 2 * px + py
            passed(k, chip, theirs).wait_recv()
            conv_from(k, chip, (px, py, c)).wait_recv()
        for cp in sends + fwds:
            cp.wait_send()
        for cp in loc:
            cp.wait()

    return pl.pallas_call(
        body, name="all_gather_weights", in_specs=[ANY, ANY], out_specs=[ANY, ANY],
        out_shape=[jax.ShapeDtypeStruct((N_CHIPS,) + wpack.shape, wpack.dtype), jax.ShapeDtypeStruct((N_CHIPS,) + cw.shape, cw.dtype)],
        scratch_shapes=[pltpu.SemaphoreType.DMA((6,)), pltpu.SemaphoreType.DMA((6,)), pltpu.SemaphoreType.DMA((3,)),
                        pltpu.SemaphoreType.DMA((3,)), pltpu.SemaphoreType.DMA((1,))],
        compiler_params=_comm_params(),
    )(wpack, cw)


def _exchange_pair(gp):
    def body(g_ref, r_ref, send_sems, recv_sems):
        x, y, c = _mesh_pos()
        cps = [pltpu.make_async_remote_copy(g_ref.at[s, 1 - c], r_ref.at[s], send_sems.at[s], recv_sems.at[s],
                                            device_id=(x, y, 1 - c), device_id_type=MESH) for s in range(N_CHIPS)]
        for cp in cps:
            cp.start()
        for cp in cps:
            cp.wait()

    return pl.pallas_call(
        body, name="grad_exchange_pair", in_specs=[ANY], out_specs=ANY,
        out_shape=jax.ShapeDtypeStruct((N_CHIPS,) + gp.shape[2:], gp.dtype),
        scratch_shapes=[pltpu.SemaphoreType.DMA((N_CHIPS,)), pltpu.SemaphoreType.DMA((N_CHIPS,))],
        compiler_params=_comm_params(),
    )(gp)


def _pair_sum(mine, from_sibling):
    def body(a_ref, b_ref, bf_ref, tail_ref):
        s = a_ref[0] + b_ref[0]
        bf_ref[0] = s.astype(BF)
        tail_ref[0] = s[SMALL_ROW0:, :]

    piece = pl.BlockSpec((1, HALF_ROWS, LANES), lambda i: (i, 0, 0))
    return pl.pallas_call(
        body, name="grad_pair_sum", grid=(N_CHIPS,), in_specs=[piece, piece],
        out_specs=[piece, pl.BlockSpec((1, SMALL_PIECE_ROWS, LANES), lambda i: (i, 0, 0))],
        out_shape=[jax.ShapeDtypeStruct((N_CHIPS, HALF_ROWS, LANES), BF), jax.ShapeDtypeStruct((N_CHIPS, SMALL_PIECE_ROWS, LANES), F32)],
        compiler_params=_params(("parallel",)),
    )(mine, from_sibling)


def _exchange_chips(ps_bf, ps_tail):
    def body(p_ref, t_ref, r_ref, rt_ref, send_sems, recv_sems):
        x, y, c = _mesh_pos()
        cps = []
        for k, (px, py) in enumerate(_other_chips(x, y)):
            cps.append(pltpu.make_async_remote_copy(p_ref.at[2 * px + py], r_ref.at[k], send_sems.at[2 * k], recv_sems.at[2 * k],
                                                    device_id=(px, py, c), device_id_type=MESH))
            cps.append(pltpu.make_async_remote_copy(t_ref.at[2 * px + py], rt_ref.at[k], send_sems.at[2 * k + 1],
                                                    recv_sems.at[2 * k + 1], device_id=(px, py, c), device_id_type=MESH))
        for cp in cps:
            cp.start()
        for cp in cps:
            cp.wait()

    return pl.pallas_call(
        body, name="grad_exchange_chips", in_specs=[ANY, ANY], out_specs=[ANY, ANY],
        out_shape=[jax.ShapeDtypeStruct((N_CHIPS - 1,) + ps_bf.shape[1:], ps_bf.dtype),
                   jax.ShapeDtypeStruct((N_CHIPS - 1,) + ps_tail.shape[1:], ps_tail.dtype)],
        scratch_shapes=[pltpu.SemaphoreType.DMA((6,)), pltpu.SemaphoreType.DMA((6,))],
        compiler_params=_comm_params(),
    )(ps_bf, ps_tail)


def _chip_sum(own_bf, own_tail, others_bf, others_tail):
    def body(ob_ref, ot_ref, b_ref, t_ref, o_ref):
        acc = ob_ref[0:SMALL_ROW0, :].astype(F32)
        tail = ot_ref[...]
        for k in range(N_CHIPS - 1):
            acc = acc + b_ref[k, 0:SMALL_ROW0, :].astype(F32)
            tail = tail + t_ref[k]
        o_ref[0:SMALL_ROW0, :] = acc
        o_ref[SMALL_ROW0:, :] = tail

    return pl.pallas_call(
        body, name="grad_chip_sum", out_shape=jax.ShapeDtypeStruct((HALF_ROWS, LANES), F32),
        compiler_params=_params(),
    )(own_bf, own_tail, others_bf, others_tail)


def _swap_reduced_halves(gh):
    def body(g_ref, o_ref, send_sem, recv_sem):
        x, y, c = _mesh_pos()
        cp = pltpu.make_async_remote_copy(g_ref, o_ref, send_sem, recv_sem, device_id=(x, y, 1 - c), device_id_type=MESH)
        cp.start()
        cp.wait()

    return pl.pallas_call(
        body, name="grad_swap_halves", in_specs=[ANY], out_specs=ANY,
        out_shape=jax.ShapeDtypeStruct(gh.shape, gh.dtype),
        scratch_shapes=[pltpu.SemaphoreType.DMA, pltpu.SemaphoreType.DMA],
        compiler_params=_comm_params(),
    )(gh)


def _gather_small(second_half):
    def body(gs_ref, sm_ref, send_sems, recv_sems, loc_sem):
        x, y, c = _mesh_pos()
        me = 2 * x + y
        small = gs_ref.at[pl.ds(SMALL_ROW0, SMALL_PIECE_ROWS)]
        loc = pltpu.make_async_copy(small, sm_ref.at[me], loc_sem)
        loc.start()
        cps = [pltpu.make_async_remote_copy(small, sm_ref.at[me], send_sems.at[k], recv_sems.at[k],
                                            device_id=(px, py, c), device_id_type=MESH)
               for k, (px, py) in enumerate(_other_chips(x, y))]
        for cp in cps:
            cp.start()
        for cp in cps:
            cp.wait()
        loc.wait()

    return pl.pallas_call(
        body, name="grad_gather_small", in_specs=[ANY], out_specs=ANY,
        out_shape=jax.ShapeDtypeStruct((N_CHIPS, SMALL_PIECE_ROWS, LANES), second_half.dtype),
        scratch_shapes=[pltpu.SemaphoreType.DMA((3,)), pltpu.SemaphoreType.DMA((3,)), pltpu.SemaphoreType.DMA],
        compiler_params=_comm_params(),
    )(second_half)


def _pack_grads(grads):
    small = jnp.concatenate([grads[n].reshape(-1) for n in SMALL])
    small = jnp.pad(small, (0, N_CHIPS * SMALL_PIECE_ROWS * LANES - SMALL_TOTAL)).reshape(N_CHIPS, SMALL_PIECE_ROWS, LANES)
    pieces = []
    for s in range(N_CHIPS):
        rows = []
        for n, (shp, axis) in BIG.items():
            g = grads[n]
            blk = g[s * shp[0]:(s + 1) * shp[0], :] if axis == 0 else g[:, s * shp[1]:(s + 1) * shp[1]]
            rows.append(blk.reshape(-1, LANES))
        rows.append(small[s])
        pieces.append(jnp.concatenate(rows, axis=0).reshape(2, HALF_ROWS, LANES))
    return jnp.stack(pieces)


def _unpack_shard(first_half, second_half):
    rows = jnp.concatenate([first_half, second_half], axis=0)
    out, r0 = {}, 0
    for n, (shp, _) in BIG.items():
        out[n] = rows[r0:r0 + BIG_ROWS[n]].reshape(shp)
        r0 += BIG_ROWS[n]
    return out


def _unpack_small(sm):
    flat = sm.reshape(-1)
    out, o = {}, 0
    for n, shp in SMALL.items():
        k = math.prod(shp)
        out[n] = flat[o:o + k].reshape(shp)
        o += k
    return out


def _as_2d(a):
    n = a.size
    if a.ndim >= 2 and a.shape[-1] > 1024:
        return a.reshape(-1, a.shape[-1])
    if n % 1024 == 0:
        return a.reshape(n // 1024, 1024)
    return a.reshape(1, n)


def _adamw(w, g, m, v, name):
    shape = w.shape
    w2, g2, m2, v2 = (_as_2d(a) for a in (w, g, m, v))
    rows, cols = w2.shape
    rb = 256 if rows >= 512 else rows
    by_cols = rows % rb != 0

    def body(w_ref, g_ref, m_ref, v_ref, d_ref, mo_ref, vo_ref):
        gv = g_ref[...]
        mn = ADAM_B1 * m_ref[...] + (1.0 - ADAM_B1) * gv
        vn = ADAM_B2 * v_ref[...] + (1.0 - ADAM_B2) * (gv * gv)
        m_hat = mn / (1.0 - ADAM_B1 ** ADAM_STEP)
        v_hat = vn / (1.0 - ADAM_B2 ** ADAM_STEP)
        d_ref[...] = -ADAM_LR * (m_hat / (jnp.sqrt(v_hat) + ADAM_EPS) + ADAM_WD * w_ref[...])
        mo_ref[...] = mn
        vo_ref[...] = vn

    spec = pl.BlockSpec((rows, 256), lambda i: (0, i)) if by_cols else _row_spec(rb, cols)
    sds = jax.ShapeDtypeStruct((rows, cols), F32)
    d, mo, vo = pl.pallas_call(
        body, name=name, grid=(cols // 256 if by_cols else rows // rb,), in_specs=[spec] * 4, out_specs=[spec] * 3, out_shape=[sds] * 3,
        compiler_params=_params(("parallel",)),
    )(w2, g2, m2, v2)
    return d.reshape(shape), mo.reshape(shape), vo.reshape(shape)


def kernel(x, p, norm_w, w_in, s5_A_re, s5_A_im, s5_log_dt, s5_B_re, s5_B_im, s5_C_re, s5_C_im, s5_D, s5_w_glu, s5_b_glu, conv_w, conv_b, dt_bias, A_log, ssd_D, ssd_norm_w, w_out, ple_norm_w, w_ple_gate, w_ple_proj, final_norm_w, loss_target, m_norm_w, m_w_in, m_s5_A_re, m_s5_A_im, m_s5_log_dt, m_s5_B_re, m_s5_B_im, m_s5_C_re, m_s5_C_im, m_s5_D, m_s5_w_glu, m_s5_b_glu, m_conv_w, m_conv_b, m_dt_bias, m_A_log, m_ssd_D, m_ssd_norm_w, m_w_out, m_ple_norm_w, m_w_ple_gate, m_w_ple_proj, m_final_norm_w, v_norm_w, v_w_in, v_s5_A_re, v_s5_A_im, v_s5_log_dt, v_s5_B_re, v_s5_B_im, v_s5_C_re, v_s5_C_im, v_s5_D, v_s5_w_glu, v_s5_b_glu, v_conv_w, v_conv_b, v_dt_bias, v_A_log, v_ssd_D, v_ssd_norm_w, v_w_out, v_ple_norm_w, v_w_ple_gate, v_w_ple_proj, v_final_norm_w):
    given = (norm_w, w_in, s5_A_re, s5_A_im, s5_log_dt, s5_B_re, s5_B_im, s5_C_re, s5_C_im, s5_D, s5_w_glu, s5_b_glu, conv_w, conv_b,
             dt_bias, A_log, ssd_D, ssd_norm_w, w_out, ple_norm_w, w_ple_gate, w_ple_proj, final_norm_w)
    given_m = (m_norm_w, m_w_in, m_s5_A_re, m_s5_A_im, m_s5_log_dt, m_s5_B_re, m_s5_B_im, m_s5_C_re, m_s5_C_im, m_s5_D, m_s5_w_glu,
               m_s5_b_glu, m_conv_w, m_conv_b, m_dt_bias, m_A_log, m_ssd_D, m_ssd_norm_w, m_w_out, m_ple_norm_w, m_w_ple_gate,
               m_w_ple_proj, m_final_norm_w)
    given_v = (v_norm_w, v_w_in, v_s5_A_re, v_s5_A_im, v_s5_log_dt, v_s5_B_re, v_s5_B_im, v_s5_C_re, v_s5_C_im, v_s5_D, v_s5_w_glu,
               v_s5_b_glu, v_conv_w, v_conv_b, v_dt_bias, v_A_log, v_ssd_D, v_ssd_norm_w, v_w_out, v_ple_norm_w, v_w_ple_gate,
               v_w_ple_proj, v_final_norm_w)
    wts, mom, var = dict(zip(WEIGHTS, given)), dict(zip(WEIGHTS, given_m)), dict(zip(WEIGHTS, given_v))
    drop = lambda n, a: a if n == "final_norm_w" else a[0]

    shard2d = lambda n, a: a[0].T if n == "w_in" else drop(n, a)
    wpack = jnp.concatenate([shard2d(n, wts[n]).astype(BF).reshape(-1, LANES) for n in BIG], axis=0)
    wall, cwall = _all_gather_chips(wpack, drop("conv_w", wts["conv_w"]))
    chip = 2 * lax.axis_index("x") + lax.axis_index("y")
    full, r0 = {}, 0
    for n, (shp, axis) in BIG.items():
        blk = lax.dynamic_update_slice_in_dim(wall[:, r0:r0 + BIG_ROWS[n]].reshape((N_CHIPS,) + shp),
                                              shard2d(n, wts[n]).astype(BF)[None], chip, axis=0)
        full[n] = blk.reshape(N_CHIPS * shp[0], shp[1]) if axis == 0 else blk.transpose(1, 0, 2).reshape(shp[0], N_CHIPS * shp[1])
        r0 += BIG_ROWS[n]
    for n in SMALL:
        full[n] = drop(n, wts[n])
    full["conv_w"] = cwall.transpose(1, 0, 2).reshape(4, 2048)

    loss, gx, grads = _local_step(x[0], p[0, 0], loss_target[0], full)
    loss = lax.psum(loss, MESH_AXES)

    gp = _pack_grads({n: grads[n].reshape(SMALL[n]) if n in SMALL else grads[n] for n in WEIGHTS})
    c = lax.axis_index("c")
    from_sibling = _exchange_pair(gp)
    mine = lax.dynamic_index_in_dim(gp, c, axis=1, keepdims=False)
    ps_bf, ps_tail = _pair_sum(mine, from_sibling)
    others_bf, others_tail = _exchange_chips(ps_bf, ps_tail)
    own_bf = lax.dynamic_index_in_dim(ps_bf, chip, axis=0, keepdims=False)
    own_tail = lax.dynamic_index_in_dim(ps_tail, chip, axis=0, keepdims=False)
    reduced_half = _chip_sum(own_bf, own_tail, others_bf, others_tail)
    sibling_half = _swap_reduced_halves(reduced_half)
    first_half = jnp.where(c == 0, reduced_half, sibling_half)
    second_half = jnp.where(c == 0, sibling_half, reduced_half)
    sm = _gather_small(second_half)
    g_final = {**_unpack_small(sm), **_unpack_shard(first_half, second_half)}
    g_final["conv_w"] = lax.dynamic_slice_in_dim(g_final["conv_w"], chip * 512, 512, axis=1)

    outs_g, outs_d, outs_m, outs_v = [], [], [], []
    for n in WEIGHTS:
        if n == "w_in":
            res = _adamw(wts[n][0].T, g_final[n], mom[n][0].T, var[n][0].T, "adamw_" + n)
            g, d, mo, vo = (a.T[None] for a in (g_final[n],) + res)
        else:
            g = g_final[n].reshape(wts[n].shape)
            d, mo, vo = _adamw(wts[n], g, mom[n], var[n], "adamw_" + n)
        outs_g.append(g)
        outs_d.append(d)
        outs_m.append(mo)
        outs_v.append(vo)
    return (loss, gx[None], *outs_g, *outs_d, *outs_m, *outs_v)
```

```python
import functools
import math

import jax
import jax.numpy as jnp
from jax import lax
from jax.experimental import pallas as pl
from jax.experimental.pallas import tpu as pltpu

F32 = jnp.float32
BF = jnp.bfloat16
EPS = 1e-6
CHUNK = 64
D_MODEL = 1024
S5_GROUPS = 64
S5_CH = 16
S5_STATE = 64
SSD_HEADS = 16
SSD_HEAD_DIM = 64
SSD_GROUPS = 4
SSD_STATE = 128
D_MAIN = 5120
LANES = 128
TOKEN_TILE = 256
VMEM_LIMIT = 56 * 1024 * 1024
MESH_AXES = ("x", "y", "c")
N_CHIPS = 4
ADAM_LR, ADAM_B1, ADAM_B2, ADAM_EPS, ADAM_WD, ADAM_STEP = 0.001, 0.9, 0.999, 1e-08, 0.01, 10
MESH = pl.DeviceIdType.MESH
ANY = pl.BlockSpec(memory_space=pl.ANY)


def _dot(a, b):
    return jnp.dot(a, b, preferred_element_type=F32)


def _dot_nt(a, b):
    return lax.dot_general(a, b, (((1,), (1,)), ((), ())), preferred_element_type=F32)


def _dot_tn(a, b):
    return lax.dot_general(a, b, (((0,), (0,)), ((), ())), preferred_element_type=F32)


def _sigmoid(x):
    return 1.0 / (1.0 + jnp.exp(-x))


def _softplus(x):
    return jnp.maximum(x, 0.0) + jnp.log(1.0 + jnp.exp(-jnp.abs(x)))


_GELU_C = math.sqrt(2.0 / math.pi)


def _gelu(x):
    return 0.5 * x * (1.0 + jnp.tanh(_GELU_C * (x + 0.044715 * x * x * x)))


def _gelu_grad(x):
    th = jnp.tanh(_GELU_C * (x + 0.044715 * x * x * x))
    return 0.5 * (1.0 + th) + 0.5 * x * (1.0 - th * th) * _GELU_C * (1.0 + 3.0 * 0.044715 * x * x)


def _params(sem=None):
    return pltpu.CompilerParams(dimension_semantics=sem, vmem_limit_bytes=VMEM_LIMIT)


def _row_spec(tl, width, col=0):
    return pl.BlockSpec((tl, width), lambda i, col=col: (i, col))


def _const_spec(shape):
    nd = len(shape)
    return pl.BlockSpec(shape, lambda *_: (0,) * nd)


def _in_proj_fwd(x, norm_w, w_main, w_dt):
    L = x.shape[0]
    tl = min(TOKEN_TILE, L)

    def body(x_ref, nw_ref, wm_ref, wd_ref, hn_ref, ps5_ref, pssd_ref, pd_ref):
        xv = x_ref[...]
        r = lax.rsqrt(jnp.mean(xv * xv, axis=-1, keepdims=True) + EPS)
        hn = (xv * r * nw_ref[...]).astype(BF)
        hn_ref[...] = hn
        for j in range(2):
            ps5_ref[:, j * 1024:(j + 1) * 1024] = _dot_nt(hn, wm_ref[j * 1024:(j + 1) * 1024, :]).astype(BF)
        for j in range(3):
            pssd_ref[:, j * 1024:(j + 1) * 1024] = _dot_nt(hn, wm_ref[(j + 2) * 1024:(j + 3) * 1024, :])
        pd_ref[...] = _dot_nt(hn, wd_ref[...])

    return pl.pallas_call(
        body, name="in_proj_fwd", grid=(L // tl,),
        in_specs=[_row_spec(tl, D_MODEL), _const_spec((1, D_MODEL)), _const_spec((D_MAIN, D_MODEL)), _const_spec((LANES, D_MODEL))],
        out_specs=[_row_spec(tl, D_MODEL), _row_spec(tl, 2048), _row_spec(tl, 3072), _row_spec(tl, LANES)],
        out_shape=[jax.ShapeDtypeStruct((L, D_MODEL), BF), jax.ShapeDtypeStruct((L, 2048), BF), jax.ShapeDtypeStruct((L, 3072), F32),
                   jax.ShapeDtypeStruct((L, LANES), F32)],
        compiler_params=_params(("arbitrary",)),
    )(x, norm_w, w_main, w_dt)


def _in_proj_bwd(x, norm_w, dh1, du_flat, dyssm, s5_d, dzs, dzd, dxbc, ddt, w_main, w_dt):
    L = x.shape[0]
    tl = min(TOKEN_TILE, L)

    def body(x_ref, nw_ref, dh1_ref, duf_ref, dys_ref, d_ref, dzs_ref, dzd_ref, dxbc_ref, ddt_ref, wm_ref, wd_ref,
             gx_ref, du_ref, gnw_ref):
        @pl.when(pl.program_id(0) == 0)
        def _():
            gnw_ref[...] = jnp.zeros_like(gnw_ref)

        du = (duf_ref[...].astype(F32) + dys_ref[...].astype(F32) * d_ref[...]).astype(BF)
        du_ref[...] = du
        dhn = _dot(du, wm_ref[0:1024, :])
        dhn += _dot(dzs_ref[...], wm_ref[1024:2048, :])
        dhn += _dot(dzd_ref[...], wm_ref[2048:3072, :])
        dhn += _dot(dxbc_ref[...], wm_ref[3072:5120, :])
        dhn += _dot(ddt_ref[...].astype(BF), wd_ref[...])
        xv = x_ref[...]
        r = lax.rsqrt(jnp.mean(xv * xv, axis=-1, keepdims=True) + EPS)
        xh = xv * r
        gnw_ref[...] += jnp.sum(dhn * xh, axis=0, keepdims=True)
        g = dhn * nw_ref[...]
        gx_ref[...] = dh1_ref[...] + r * (g - xh * jnp.mean(g * xh, axis=-1, keepdims=True))

    return pl.pallas_call(
        body, name="in_proj_bwd", grid=(L // tl,),
        in_specs=[_row_spec(tl, D_MODEL), _const_spec((1, D_MODEL)), _row_spec(tl, D_MODEL), _row_spec(tl, D_MODEL),
                  _row_spec(tl, D_MODEL), _const_spec((1, D_MODEL)), _row_spec(tl, D_MODEL), _row_spec(tl, D_MODEL),
                  _row_spec(tl, 2048), _row_spec(tl, LANES), _const_spec((D_MAIN, D_MODEL)), _const_spec((LANES, D_MODEL))],
        out_specs=[_row_spec(tl, D_MODEL), _row_spec(tl, D_MODEL), _const_spec((1, D_MODEL))],
        out_shape=[jax.ShapeDtypeStruct((L, D_MODEL), F32), jax.ShapeDtypeStruct((L, D_MODEL), BF), jax.ShapeDtypeStruct((1, D_MODEL), F32)],
        compiler_params=_params(("arbitrary",)),
    )(x, norm_w, dh1, du_flat, dyssm, s5_d, dzs, dzd, dxbc, ddt, w_main, w_dt)


def _matmul_tn(a, b, name):
    L, M = a.shape
    N = b.shape[1]
    tm, tn, tk = min(M, 1024), min(N, 1024), min(L, 1024)

    def body(a_ref, b_ref, o_ref):
        @pl.when(pl.program_id(2) == 0)
        def _():
            o_ref[...] = jnp.zeros_like(o_ref)

        o_ref[...] += _dot_tn(a_ref[...].astype(BF), b_ref[...].astype(BF))

    return pl.pallas_call(
        body, name=name, grid=(M // tm, N // tn, L // tk),
        in_specs=[pl.BlockSpec((tk, tm), lambda i, j, k: (k, i)), pl.BlockSpec((tk, tn), lambda i, j, k: (k, j))],
        out_specs=pl.BlockSpec((tm, tn), lambda i, j, k: (i, j)),
        out_shape=jax.ShapeDtypeStruct((M, N), F32),
        compiler_params=_params(("parallel", "parallel", "arbitrary")),
    )(a, b)


def _s5_discretise(a_re, a_im, log_dt, b_re, b_im, c_re, c_im):
    dt = jnp.exp(log_dt)[:, None]
    tau = jnp.arange(CHUNK + 1, dtype=F32)
    mag = jnp.exp((a_re * dt)[:, :, None] * tau)
    ang = (a_im * dt)[:, :, None] * tau
    pw_re, pw_im = mag * jnp.cos(ang), mag * jnp.sin(ang)
    er, ei = pw_re[:, :, 1] - 1.0, pw_im[:, :, 1]
    den = a_re * a_re + a_im * a_im
    beta_re, beta_im = (er * a_re + ei * a_im) / den, (ei * a_re - er * a_im) / den
    bb_re = (beta_re[:, :, None] * b_re - beta_im[:, :, None] * b_im).transpose(0, 2, 1)
    bb_im = (beta_re[:, :, None] * b_im + beta_im[:, :, None] * b_re).transpose(0, 2, 1)
    return bb_re, bb_im, c_re, c_im, pw_re, pw_im


def _s5_table_factors(bb_re, bb_im, c_re, c_im, pw_re, pw_im):
    pr = pw_re[:, :, CHUNK - 1::-1].transpose(0, 2, 1)
    pi = pw_im[:, :, CHUNK - 1::-1].transpose(0, 2, 1)
    bbp = jnp.concatenate([bb_re, bb_im], axis=-1)
    pwr = jnp.concatenate([pr, pi], axis=-1)
    cnp = jnp.concatenate([c_re, c_im], axis=-1)
    pwf = jnp.concatenate([pw_re.transpose(0, 2, 1), pw_im.transpose(0, 2, 1)], axis=-1)
    return bbp, pwr, cnp, jnp.pad(pwf, ((0, 0), (0, 7), (0, 0)))


def _s5_small_cotangents(dbk, r12, q12, xy, uv, xy0, uv0, da64):
    n = S5_STATE
    fold = lambda a: a[..., :n] + a[..., n:]
    fold_m = lambda a: a[..., n:] - a[..., :n]
    dbb_re = fold(r12[:, :S5_CH]) + dbk[..., :n]
    dbb_im = fold_m(r12[:, S5_CH:]) + dbk[..., n:]
    dpr, dpi = fold(q12[:, :CHUNK]), fold_m(q12[:, CHUNK:])
    dc_re = -fold_m(xy[:, :S5_CH]) - fold_m(xy0[:, :S5_CH])
    dc_im = -fold(xy[:, S5_CH:]) - fold(xy0[:, S5_CH:])
    dp1_re, dp1_im = -fold_m(uv[:, :CHUNK]), -fold(uv[:, CHUNK:])
    dp0_re, dp0_im = -fold_m(uv0[:, :CHUNK]), -fold(uv0[:, CHUNK:])
    zero = jnp.zeros((S5_GROUPS, n, 1), F32)
    dpw_re = (jnp.concatenate([(dpr[:, ::-1] + dp0_re).transpose(0, 2, 1), zero], axis=-1)
              + jnp.concatenate([zero, dp1_re.transpose(0, 2, 1)], axis=-1)).at[:, :, CHUNK].add(da64[:, :n])
    dpw_im = (jnp.concatenate([(dpi[:, ::-1] + dp0_im).transpose(0, 2, 1), zero], axis=-1)
              + jnp.concatenate([zero, dp1_im.transpose(0, 2, 1)], axis=-1)).at[:, :, CHUNK].add(da64[:, n:])
    return dbb_re, dbb_im, dc_re, dc_im, dpw_re, dpw_im


def _s5_scan_powers(a_re, a_im, log_dt, nsteps):
    dt = jnp.exp(log_dt)[:, None]
    steps = (CHUNK * (2.0 ** jnp.arange(8, dtype=F32)))[None, :, None]
    mag = jnp.exp((a_re * dt)[:, None, :] * steps)
    ang = (a_im * dt)[:, None, :] * steps
    re, im = mag * jnp.cos(ang), mag * jnp.sin(ang)
    del nsteps
    return jnp.concatenate([re, re], -1), jnp.concatenate([-im, im], -1)


def _build_toeplitz(kmat, tg_ref):
    lane = lax.broadcasted_iota(jnp.int32, (CHUNK, CHUNK * S5_CH), 1)
    srow = lax.broadcasted_iota(jnp.int32, (CHUNK, CHUNK * S5_CH), 0)
    keep = lane >= S5_CH * srow
    for h in range(S5_CH):
        row = jnp.broadcast_to(kmat[h:h + 1, :], (CHUNK, CHUNK * S5_CH))
        rolled = pltpu.roll(row, 0, 1, stride=S5_CH, stride_axis=0)
        tg_ref[h * CHUNK:(h + 1) * CHUNK, :] = jnp.where(keep, rolled, 0.0).astype(BF)


def _swap_halves(x):
    return pltpu.roll(x, S5_STATE, 1)


def _hi_lo(x):
    hi = x.astype(BF)
    return hi, (x - hi.astype(F32)).astype(BF)


def _dot3(dot, a, b):
    a_hi, a_lo = _hi_lo(a)
    b_hi, b_lo = _hi_lo(b)
    return dot(a_hi, b_hi) + dot(a_hi, b_lo) + dot(a_lo, b_hi)


def _build_state_tables(bbp_ref, pwr_ref, cnp_ref, pwf_ref, wst_ref, wofft_ref, w0_ref):
    lane = lax.broadcasted_iota(jnp.int32, (1, 2 * S5_STATE), 1)
    left = lane < S5_STATE
    pwr = pwr_ref[0]
    pwr_sw = _swap_halves(pwr)
    for hh in range(S5_CH):
        bb = bbp_ref[0, hh:hh + 1, :]
        bb_sw = _swap_halves(bb)
        wst_ref[hh * CHUNK:(hh + 1) * CHUNK, :] = (jnp.where(left, bb, bb_sw) * pwr
                                                   + jnp.where(left, -bb_sw, bb) * pwr_sw).astype(BF)
    cn = cnp_ref[0]
    cn_sw = _swap_halves(cn)
    c_a = jnp.where(left, cn, -cn_sw)
    c_b = jnp.where(left, -cn_sw, -cn)
    p_prev = pwf_ref[0, 0:1, :]
    for t in range(CHUNK):
        p = pwf_ref[0, t + 1:t + 2, :]
        w0_ref[t * S5_CH:(t + 1) * S5_CH, :] = c_a * p_prev + c_b * _swap_halves(p_prev)
        wofft_ref[t * S5_CH:(t + 1) * S5_CH, :] = (c_a * p + c_b * _swap_halves(p)).astype(BF)
        p_prev = p
    return _dot3(_dot_nt, bbp_ref[0], w0_ref[...])


def _reduce_table_cotangent(d_table, cnp_ref, pwf_ref, first_power):
    cn = cnp_ref[0]
    cn_sw = _swap_halves(cn)
    xa = jnp.zeros((S5_CH, 2 * S5_STATE), F32)
    ya = jnp.zeros((S5_CH, 2 * S5_STATE), F32)
    ru_rows, rv_rows = [], []
    for t in range(CHUNK):
        blk = d_table[t * S5_CH:(t + 1) * S5_CH, :]
        p = pwf_ref[0, t + first_power:t + first_power + 1, :]
        xa = xa + blk * p
        ya = ya + blk * _swap_halves(p)
        ru_rows.append(jnp.sum(blk * cn, axis=0, keepdims=True))
        rv_rows.append(jnp.sum(blk * cn_sw, axis=0, keepdims=True))
    return jnp.concatenate([xa, ya], axis=0), jnp.concatenate(ru_rows + rv_rows, axis=0)


def _s5_core_fwd(uflat, bbp, pwr, cnp, pwf, p1, p2):
    G, nc, W = uflat.shape
    nsteps = max(1, (nc - 1).bit_length())

    def body(u_ref, bbp_ref, pwr_ref, cnp_ref, pwf_ref, p1_ref, p2_ref, y_ref, h_ref, tg_ref, wst_ref, wofft_ref, w0_ref):
        _build_toeplitz(_build_state_tables(bbp_ref, pwr_ref, cnp_ref, pwf_ref, wst_ref, wofft_ref, w0_ref), tg_ref)
        u = u_ref[0]
        x = _dot(u, wst_ref[...])
        row = lax.broadcasted_iota(jnp.int32, x.shape, 0)
        d = 1
        for k in range(nsteps):
            sh = jnp.where(row >= d, pltpu.roll(x, d, 0), 0.0)
            x = x + p1_ref[0, k:k + 1, :] * sh + p2_ref[0, k:k + 1, :] * _swap_halves(sh)
            d *= 2
        h = jnp.where(row >= 1, pltpu.roll(x, 1, 0), 0.0)
        h_ref[0] = h
        y = _dot(u, tg_ref[...]) + _dot_nt(h.astype(BF), wofft_ref[...])
        y_ref[0] = y.astype(BF)

    spec_g = lambda a, b: pl.BlockSpec((1, a, b), lambda g: (g, 0, 0))
    return pl.pallas_call(
        body, name="s5_core_fwd", grid=(G,),
        in_specs=[spec_g(nc, W), spec_g(S5_CH, 2 * S5_STATE), spec_g(CHUNK, 2 * S5_STATE),
                  spec_g(S5_CH, 2 * S5_STATE), spec_g(CHUNK + 8, 2 * S5_STATE), spec_g(8, 2 * S5_STATE), spec_g(8, 2 * S5_STATE)],
        out_specs=[spec_g(nc, W), spec_g(nc, 2 * S5_STATE)],
        out_shape=[jax.ShapeDtypeStruct((G, nc, W), BF), jax.ShapeDtypeStruct((G, nc, 2 * S5_STATE), F32)],
        scratch_shapes=[pltpu.VMEM((W, W), BF), pltpu.VMEM((W, 2 * S5_STATE), BF), pltpu.VMEM((W, 2 * S5_STATE), BF),
                        pltpu.VMEM((W, 2 * S5_STATE), F32)],
        compiler_params=_params(("arbitrary",)),
    )(uflat, bbp, pwr, cnp, pwf, p1, p2)


def _s5_core_bwd(uflat, dyflat, hsave, bbp, pwr, cnp, pwf, p1, p2):
    G, nc, W = uflat.shape
    nsteps = max(1, (nc - 1).bit_length())

    def body(u_ref, dy_ref, h_ref, bbp_ref, pwr_ref, cnp_ref, pwf_ref, p1_ref, p2_ref,
             du_ref, dbk_ref, r12_ref, q12_ref, xy_ref, uv_ref, xy0_ref, uv0_ref, da_ref,
             tg_ref, flip_ref, wst_ref, wofft_ref, w0_ref):
        @pl.when(pl.program_id(0) == 0)
        def _():
            r = lax.broadcasted_iota(jnp.int32, (W, W), 0)
            c = lax.broadcasted_iota(jnp.int32, (W, W), 1)
            flip_ref[...] = (((r >> 6) == (c >> 6)) & ((r & (CHUNK - 1)) + (c & (CHUNK - 1)) == CHUNK - 1)).astype(BF)

        _build_toeplitz(_build_state_tables(bbp_ref, pwr_ref, cnp_ref, pwf_ref, wst_ref, wofft_ref, w0_ref), tg_ref)
        u = u_ref[0]
        dy = dy_ref[0]
        h = h_ref[0]
        gh = _dot(dy, wofft_ref[...])
        row = lax.broadcasted_iota(jnp.int32, gh.shape, 0)
        x = jnp.where(row < nc - 1, pltpu.roll(gh, nc - 1, 0), 0.0)
        d = 1
        for k in range(nsteps):
            sh = jnp.where(row < nc - d, pltpu.roll(x, nc - d, 0), 0.0)
            x = x + p1_ref[0, k:k + 1, :] * sh - p2_ref[0, k:k + 1, :] * _swap_halves(sh)
            d *= 2
        gs = x.astype(BF)
        du_ref[0] = (_dot_nt(dy, tg_ref[...]) + _dot_nt(gs, wst_ref[...])).astype(BF)

        dwst = _dot_tn(u, gs)
        pwr = pwr_ref[0]
        pwr_sw = _swap_halves(pwr)
        q1 = jnp.zeros((CHUNK, 2 * S5_STATE), F32)
        q2 = jnp.zeros((CHUNK, 2 * S5_STATE), F32)
        r1_rows, r2_rows = [], []
        for hh in range(S5_CH):
            blk = dwst[hh * CHUNK:(hh + 1) * CHUNK, :]
            r1_rows.append(jnp.sum(blk * pwr, axis=0, keepdims=True))
            r2_rows.append(jnp.sum(blk * pwr_sw, axis=0, keepdims=True))
            bb = bbp_ref[0, hh:hh + 1, :]
            q1 = q1 + blk * bb
            q2 = q2 + blk * _swap_halves(bb)
        r12_ref[0] = jnp.concatenate(r1_rows + r2_rows, axis=0)
        q12_ref[0] = jnp.concatenate([q1, q2], axis=0)

        xy_ref[0], uv_ref[0] = _reduce_table_cotangent(_dot_tn(dy, h.astype(BF)), cnp_ref, pwf_ref, 1)
        r1 = jnp.sum(x * h, axis=0, keepdims=True)
        r2 = jnp.sum(x * _swap_halves(h), axis=0, keepdims=True)
        da_ref[0] = jnp.concatenate([r1, r2, jnp.zeros((6, 2 * S5_STATE), F32)], axis=0)
        lane = lax.broadcasted_iota(jnp.int32, (CHUNK, W), 1)
        srow = lax.broadcasted_iota(jnp.int32, (CHUNK, W), 0)
        keep = lane < S5_CH * (srow + 1)
        ur = _dot(u, flip_ref[...]).astype(BF)
        dk_rows = []
        for hh in range(S5_CH):
            dt_h = _dot_tn(ur[:, hh * CHUNK:(hh + 1) * CHUNK], dy)
            back = pltpu.roll(dt_h, S5_CH, 1, stride=S5_CH, stride_axis=0)
            dk_rows.append(jnp.sum(jnp.where(keep, back, 0.0), axis=0, keepdims=True))
        dk = jnp.concatenate(dk_rows, axis=0)
        dbk_ref[0] = _dot3(_dot, dk, w0_ref[...])
        xy0_ref[0], uv0_ref[0] = _reduce_table_cotangent(_dot3(_dot_tn, dk, bbp_ref[0]), cnp_ref, pwf_ref, 0)

    spec_g = lambda a, b: pl.BlockSpec((1, a, b), lambda g: (g, 0, 0))
    small_c = jax.ShapeDtypeStruct((G, 2 * S5_CH, 2 * S5_STATE), F32)
    small_p = jax.ShapeDtypeStruct((G, 2 * CHUNK, 2 * S5_STATE), F32)
    return pl.pallas_call(
        body, name="s5_core_bwd", grid=(G,),
        in_specs=[spec_g(nc, W), spec_g(nc, W), spec_g(nc, 2 * S5_STATE), spec_g(S5_CH, 2 * S5_STATE),
                  spec_g(CHUNK, 2 * S5_STATE), spec_g(S5_CH, 2 * S5_STATE), spec_g(CHUNK + 8, 2 * S5_STATE),
                  spec_g(8, 2 * S5_STATE), spec_g(8, 2 * S5_STATE)],
        out_specs=[spec_g(nc, W), spec_g(S5_CH, 2 * S5_STATE), spec_g(2 * S5_CH, 2 * S5_STATE), spec_g(2 * CHUNK, 2 * S5_STATE),
                   spec_g(2 * S5_CH, 2 * S5_STATE), spec_g(2 * CHUNK, 2 * S5_STATE),
                   spec_g(2 * S5_CH, 2 * S5_STATE), spec_g(2 * CHUNK, 2 * S5_STATE), spec_g(8, 2 * S5_STATE)],
        out_shape=[jax.ShapeDtypeStruct((G, nc, W), BF), jax.ShapeDtypeStruct((G, S5_CH, 2 * S5_STATE), F32),
                   small_c, small_p, small_c, small_p, small_c, small_p, jax.ShapeDtypeStruct((G, 8, 2 * S5_STATE), F32)],
        scratch_shapes=[pltpu.VMEM((W, W), BF), pltpu.VMEM((W, W), BF), pltpu.VMEM((W, 2 * S5_STATE), BF),
                        pltpu.VMEM((W, 2 * S5_STATE), BF), pltpu.VMEM((W, 2 * S5_STATE), F32)],
        compiler_params=_params(("arbitrary",)),
    )(uflat, dyflat, hsave, bbp, pwr, cnp, pwf, p1, p2)


def _flat_hs(a, nc):
    return a.reshape(nc, CHUNK, S5_GROUPS, S5_CH).transpose(2, 0, 3, 1).reshape(S5_GROUPS, nc, CHUNK * S5_CH)


def _unflat_hs(a, nc):
    return a.reshape(S5_GROUPS, nc, S5_CH, CHUNK).transpose(1, 3, 0, 2).reshape(nc * CHUNK, D_MODEL)


def _flat_tk(a, nc):
    return a.reshape(nc, CHUNK, S5_GROUPS, S5_CH).transpose(2, 0, 1, 3).reshape(S5_GROUPS, nc, CHUNK * S5_CH)


def _unflat_tk(a, nc):
    return a.reshape(S5_GROUPS, nc, CHUNK, S5_CH).transpose(1, 2, 0, 3).reshape(nc * CHUNK, D_MODEL)


def _s5_post_fwd(yssm, proj, s5_d, w_glu, b_glu):
    L = yssm.shape[0]
    tl = min(TOKEN_TILE, L)

    def body(ys_ref, u_ref, z_ref, d_ref, wg_ref, bg_ref, o_ref):
        u = u_ref[...].astype(F32)
        a = _gelu(ys_ref[...].astype(F32) + d_ref[...] * u)
        y = a * _sigmoid(_dot(a.astype(BF), wg_ref[...]) + bg_ref[...])
        z = z_ref[...].astype(F32)
        o_ref[...] = (y * z * _sigmoid(z)).astype(BF)

    return pl.pallas_call(
        body, name="s5_post_fwd", grid=(L // tl,),
        in_specs=[_row_spec(tl, D_MODEL), _row_spec(tl, D_MODEL, 0), _row_spec(tl, D_MODEL, 1), _const_spec((1, D_MODEL)),
                  _const_spec((D_MODEL, D_MODEL)), _const_spec((1, D_MODEL))],
        out_specs=_row_spec(tl, D_MODEL),
        out_shape=jax.ShapeDtypeStruct((L, D_MODEL), BF),
        compiler_params=_params(("arbitrary",)),
    )(yssm, proj, proj, s5_d, w_glu, b_glu)


def _s5_post_bwd(dys5, yssm, proj, s5_d, w_glu, b_glu):
    L = yssm.shape[0]
    tl = min(TOKEN_TILE, L)

    def body(dy_ref, ys_ref, u_ref, z_ref, d_ref, wg_ref, bg_ref, dz_ref, dys_ref, a_ref, dgl_ref, dbg_ref, dd_ref):
        @pl.when(pl.program_id(0) == 0)
        def _():
            dbg_ref[...] = jnp.zeros_like(dbg_ref)
            dd_ref[...] = jnp.zeros_like(dd_ref)

        u = u_ref[...].astype(F32)
        y0 = ys_ref[...].astype(F32) + d_ref[...] * u
        a = _gelu(y0)
        a_bf = a.astype(BF)
        sg = _sigmoid(_dot(a_bf, wg_ref[...]) + bg_ref[...])
        y = a * sg
        z = z_ref[...].astype(F32)
        sz = _sigmoid(z)
        dout = dy_ref[...].astype(F32)
        dz_ref[...] = (dout * y * sz * (1.0 + z * (1.0 - sz))).astype(BF)
        dyv = dout * z * sz
        dgl = dyv * a * sg * (1.0 - sg)
        dgl_bf = dgl.astype(BF)
        da = dyv * sg + _dot_nt(dgl_bf, wg_ref[...])
        dy0 = da * _gelu_grad(y0)
        dbg_ref[...] += jnp.sum(dgl, axis=0, keepdims=True)
        dd_ref[...] += jnp.sum(dy0 * u, axis=0, keepdims=True)
        dys_ref[...] = dy0.astype(BF)
        a_ref[...] = a_bf
        dgl_ref[...] = dgl_bf

    big = jax.ShapeDtypeStruct((L, D_MODEL), BF)
    vec = jax.ShapeDtypeStruct((1, D_MODEL), F32)
    return pl.pallas_call(
        body, name="s5_post_bwd", grid=(L // tl,),
        in_specs=[_row_spec(tl, D_MODEL), _row_spec(tl, D_MODEL), _row_spec(tl, D_MODEL, 0), _row_spec(tl, D_MODEL, 1),
                  _const_spec((1, D_MODEL)), _const_spec((D_MODEL, D_MODEL)), _const_spec((1, D_MODEL))],
        out_specs=[_row_spec(tl, D_MODEL)] * 4 + [_const_spec((1, D_MODEL))] * 2,
        out_shape=[big, big, big, big, vec, vec],
        compiler_params=_params(("arbitrary",)),
    )(dys5, yssm, proj, proj, s5_d, w_glu, b_glu)


def _cumsum_rows(a):
    row = lax.broadcasted_iota(jnp.int32, a.shape, 0)
    d = 1
    while d < a.shape[0]:
        a = a + jnp.where(row >= d, pltpu.roll(a, d, 0), 0.0)
        d *= 2
    return a


def _rev_cumsum_rows(a):
    n = a.shape[0]
    row = lax.broadcasted_iota(jnp.int32, a.shape, 0)
    d = 1
    while d < n:
        a = a + jnp.where(row < n - d, pltpu.roll(a, n - d, 0), 0.0)
        d *= 2
    return a


def _ssd_fill_padded(first, xs_ref, bc_ref, hx_ref, hb_ref, xp_ref, tl):
    hal = jnp.concatenate([hx_ref[...], hb_ref[...]], axis=1)
    xp_ref[0:8, :] = jnp.where(first, 0.0, hal)
    xp_ref[8:8 + tl, 0:1024] = xs_ref[...]
    xp_ref[8:8 + tl, 1024:2048] = bc_ref[...]


def _ssd_conv_fwd(first, xs_ref, bc_ref, hx_ref, hb_ref, cw_ref, cb_ref, xp_ref, tl):
    _ssd_fill_padded(first, xs_ref, bc_ref, hx_ref, hb_ref, xp_ref, tl)
    pre = cb_ref[...] + cw_ref[0:1, :] * xp_ref[5:5 + tl, :]
    for k in range(1, 4):
        pre = pre + cw_ref[k:k + 1, :] * xp_ref[5 + k:5 + k + tl, :]
    return pre


def _onehot_lane(h):
    return (lax.broadcasted_iota(jnp.int32, (1, LANES), 1) == h).astype(F32)


def _dot_exact(x, e):
    hi = x.astype(BF)
    r = x - hi.astype(F32)
    mid = r.astype(BF)
    lo = (r - mid.astype(F32)).astype(BF)
    return _dot(hi, e) + _dot(mid, e) + _dot(lo, e)


def _head_expand_matrices():
    e = lax.broadcasted_iota(jnp.int32, (LANES, D_MODEL), 0) == (lax.broadcasted_iota(jnp.int32, (LANES, D_MODEL), 1) >> 6)
    et = (lax.broadcasted_iota(jnp.int32, (D_MODEL, LANES), 0) >> 6) == lax.broadcasted_iota(jnp.int32, (D_MODEL, LANES), 1)
    return e.astype(BF), et.astype(BF)


def _group_masks():
    r64 = lax.broadcasted_iota(jnp.int32, (4 * CHUNK, CHUNK), 0)
    causal4 = (r64 & (CHUNK - 1)) >= lax.broadcasted_iota(jnp.int32, (4 * CHUNK, CHUNK), 1)
    r256 = lax.broadcasted_iota(jnp.int32, (4 * CHUNK, 4 * SSD_HEAD_DIM), 0)
    same = (r256 >> 6) == (lax.broadcasted_iota(jnp.int32, (4 * CHUNK, 4 * SSD_HEAD_DIM), 1) >> 6)
    return causal4, same


def _group_decay(acs, acs_t, j, causal4):
    col = jnp.concatenate([acs[:, 4 * j + hh:4 * j + hh + 1] for hh in range(4)], axis=0)
    rowv = jnp.concatenate([jnp.broadcast_to(acs_t[4 * j + hh:4 * j + hh + 1, :], (CHUNK, CHUNK)) for hh in range(4)], axis=0)
    return jnp.where(causal4, jnp.exp(col - rowv), 0.0)


def _group_last_decay(acs_t, j):
    return jnp.concatenate([jnp.broadcast_to(jnp.exp(acs_t[4 * j + hh:4 * j + hh + 1, CHUNK - 1:CHUNK]), (SSD_HEAD_DIM, 1))
                            for hh in range(4)], axis=0)


def _fold_heads(r):
    return r[0:CHUNK] + r[CHUNK:2 * CHUNK] + r[2 * CHUNK:3 * CHUNK] + r[3 * CHUNK:4 * CHUNK]


def _ssd_specs_in(tl, nt, rev):
    t_of = (lambda i: nt - 1 - i) if rev else (lambda i: i)
    rows = lambda w, col: pl.BlockSpec((tl, w), lambda i: (t_of(i), col))
    halo = lambda col: pl.BlockSpec((8, 1024), lambda i: (jnp.maximum(t_of(i) * (tl // 8) - 1, 0), col))
    return t_of, rows, halo


def _ssd_fwd(proj, pdt, conv_w, conv_b, dt_bias, a_log, ssd_d, norm_w):
    L = proj.shape[0]
    tl = min(TOKEN_TILE, L)
    nt, ncl = L // tl, tl // CHUNK
    _, rows, halo = _ssd_specs_in(tl, nt, False)

    def body(xs_ref, bc_ref, hx_ref, hb_ref, dt_ref, z_ref, cw_ref, cb_ref, dtb_ref, al_ref, dd_ref, nw_ref,
             y_ref, ypre_ref, st_ref, pre_ref, xp_ref, xbc_ref, dts_ref, hst_ref):
        i = pl.program_id(0)

        @pl.when(i == 0)
        def _():
            hst_ref[...] = jnp.zeros_like(hst_ref)

        pre = _ssd_conv_fwd(i == 0, xs_ref, bc_ref, hx_ref, hb_ref, cw_ref, cb_ref, xp_ref, tl)
        pre_ref[...] = pre
        xbc_ref[...] = pre * _sigmoid(pre)
        dts_ref[...] = _softplus(dt_ref[...] + dtb_ref[...])
        a_neg = -jnp.exp(al_ref[...])
        e16, _ = _head_expand_matrices()
        causal4, same = _group_masks()
        dd_x = _dot_exact(jnp.broadcast_to(dd_ref[...], (8, LANES)), e16)[0:1, :]

        def chunk(c, carry):
            r0 = pl.multiple_of(c * CHUNK, CHUNK)
            xbc = xbc_ref[pl.ds(r0, CHUNK), :]
            dtc = dts_ref[pl.ds(r0, CHUNK), :]
            acs = _cumsum_rows(dtc * a_neg)
            acs_t = acs.T
            acs_x = _dot_exact(acs, e16)
            xs = xbc[:, 0:1024]
            xd = xs * _dot_exact(dtc, e16)
            xd_bf = xd.astype(BF)
            xdd = (xd * jnp.exp(acs_x[CHUNK - 1:CHUNK, :] - acs_x)).astype(BF)
            e_x = jnp.exp(acs_x)
            for j in range(SSD_GROUPS):
                sl = slice(256 * j, 256 * (j + 1))
                bj = xbc[:, 1024 + 128 * j:1024 + 128 * (j + 1)].astype(BF)
                cj = xbc[:, 1536 + 128 * j:1536 + 128 * (j + 1)].astype(BF)
                g = _dot_nt(cj, bj)
                hj = hst_ref[sl, :]
                zj = _dot_nt(cj, hj.astype(BF))
                sc = (jnp.concatenate([g] * 4, axis=0) * _group_decay(acs, acs_t, j, causal4)).astype(BF)
                yd = _fold_heads(jnp.where(same, _dot(sc, xd_bf[:, sl]), 0.0))
                ypre_ref[pl.ds(r0, CHUNK), sl] = yd + e_x[:, sl] * zj + dd_x[:, sl] * xs[:, sl]
                st_ref[c, sl, :] = hj
                hst_ref[sl, :] = _group_last_decay(acs_t, j) * hj + _dot_tn(xdd[:, sl], bj)
            return carry

        lax.fori_loop(0, ncl, chunk, 0, unroll=2)
        z = z_ref[...]
        gg = ypre_ref[...] * z * _sigmoid(z)
        for j in range(SSD_GROUPS):
            seg = gg[:, 256 * j:256 * (j + 1)]
            r = lax.rsqrt(jnp.mean(seg * seg, axis=-1, keepdims=True) + EPS)
            y_ref[:, 256 * j:256 * (j + 1)] = (seg * r * nw_ref[:, 256 * j:256 * (j + 1)]).astype(BF)

    nc = L // CHUNK
    return pl.pallas_call(
        body, name="ssd_fwd", grid=(nt,),
        in_specs=[rows(1024, 1), rows(1024, 2), halo(1), halo(2), rows(LANES, 0), rows(1024, 0),
                  _const_spec((4, 2048)), _const_spec((1, 2048)), _const_spec((1, LANES)), _const_spec((1, LANES)),
                  _const_spec((1, LANES)), _const_spec((1, D_MODEL))],
        out_specs=[_row_spec(tl, D_MODEL), _row_spec(tl, D_MODEL), pl.BlockSpec((ncl, 1024, SSD_STATE), lambda i: (i, 0, 0)),
                   _row_spec(tl, 2048)],
        out_shape=[jax.ShapeDtypeStruct((L, D_MODEL), BF), jax.ShapeDtypeStruct((L, D_MODEL), F32),
                   jax.ShapeDtypeStruct((nc, 1024, SSD_STATE), F32), jax.ShapeDtypeStruct((L, 2048), F32)],
        scratch_shapes=[pltpu.VMEM((tl + 8, 2048), F32), pltpu.VMEM((tl, 2048), F32), pltpu.VMEM((tl, LANES), F32),
                        pltpu.VMEM((1024, SSD_STATE), F32)],
        compiler_params=_params(("arbitrary",)),
    )(proj, proj, proj, proj, pdt, proj, conv_w, conv_b, dt_bias, a_log, ssd_d, norm_w)


def _ssd_bwd(dyssd, ypre, proj, pdt, states, pre_act, conv_w, dt_bias, a_log, ssd_d, norm_w):
    L = proj.shape[0]
    tl = min(TOKEN_TILE, L)
    nt, ncl = L // tl, tl // CHUNK
    t_of, rows, halo = _ssd_specs_in(tl, nt, True)

    def body(dy_ref, ypre_ref, z_ref, xs_ref, bc_ref, hx_ref, hb_ref, dt_ref, st_ref, pre_ref, cw_ref, dtb_ref, al_ref,
             dd_ref, nw_ref,
             dxbc_ref, ddt_ref, dz_ref, dcw_ref, dcb_ref, ddtb_ref, dal_ref, ddd_ref, dnw_ref,
             xp_ref, xbc_ref, dts_ref, dyp_ref, dxs_ref, ddts_ref, dp_ref, dh_ref):
        i = pl.program_id(0)

        @pl.when(i == 0)
        def _():
            for r in (dcw_ref, dcb_ref, ddtb_ref, dal_ref, ddd_ref, dnw_ref, dh_ref):
                r[...] = jnp.zeros_like(r)
            dp_ref[tl:tl + 8, :] = jnp.zeros((8, 2048), F32)

        _ssd_fill_padded(t_of(i) == 0, xs_ref, bc_ref, hx_ref, hb_ref, xp_ref, tl)
        pre = pre_ref[...]
        xbc_ref[...] = pre * _sigmoid(pre)
        dts_ref[...] = _softplus(dt_ref[...] + dtb_ref[...])
        a_neg = -jnp.exp(al_ref[...])

        ypre = ypre_ref[...]
        z = z_ref[...]
        sz = _sigmoid(z)
        gg = ypre * z * sz
        dout = dy_ref[...]
        for j in range(SSD_GROUPS):
            sl = slice(256 * j, 256 * (j + 1))
            seg = gg[:, sl]
            r = lax.rsqrt(jnp.mean(seg * seg, axis=-1, keepdims=True) + EPS)
            gh = seg * r
            dnw_ref[:, sl] += jnp.sum(dout[:, sl] * gh, axis=0, keepdims=True)
            gw = dout[:, sl] * nw_ref[:, sl]
            dgg = r * (gw - gh * jnp.mean(gw * gh, axis=-1, keepdims=True))
            dyp_ref[:, sl] = dgg * z[:, sl] * sz[:, sl]
            dz_ref[:, sl] = (dgg * ypre[:, sl] * sz[:, sl] * (1.0 + z[:, sl] * (1.0 - sz[:, sl]))).astype(BF)

        e16, e16t = _head_expand_matrices()
        causal4, same = _group_masks()
        dd_x = _dot_exact(jnp.broadcast_to(dd_ref[...], (8, LANES)), e16)[0:1, :]
        last_row = (lax.broadcasted_iota(jnp.int32, (CHUNK, 1), 0) == CHUNK - 1).astype(F32)
        sel_rows = lax.broadcasted_iota(jnp.int32, (4 * CHUNK, LANES), 0) >> 6
        sel_lanes = lax.broadcasted_iota(jnp.int32, (4 * CHUNK, LANES), 1)

        def chunk(k, carry):
            dal_acc, ddx_acc = carry
            c = ncl - 1 - k
            r0 = pl.multiple_of(c * CHUNK, CHUNK)
            xbc = xbc_ref[pl.ds(r0, CHUNK), :]
            dtc = dts_ref[pl.ds(r0, CHUNK), :]
            dyp = dyp_ref[pl.ds(r0, CHUNK), :]
            acs = _cumsum_rows(dtc * a_neg)
            acs_t = acs.T
            acs_x = _dot_exact(acs, e16)
            dt_x = _dot_exact(dtc, e16)
            xs = xbc[:, 0:1024]
            xd = xs * dt_x
            xd_bf = xd.astype(BF)
            dec_x = jnp.exp(acs_x[CHUNK - 1:CHUNK, :] - acs_x)
            xdd = xd * dec_x
            xdd_bf = xdd.astype(BF)
            dz = dyp * jnp.exp(acs_x)
            dz_bf = dz.astype(BF)
            ddx_acc = ddx_acc + jnp.sum(dyp * xs, axis=0, keepdims=True)
            dacs = jnp.zeros((CHUNK, LANES), F32)
            hsum = jnp.zeros((1, LANES), F32)
            p1_l, p2_l, p3_l = [], [], []
            for j in range(SSD_GROUPS):
                sl = slice(256 * j, 256 * (j + 1))
                bj = xbc[:, 1024 + 128 * j:1024 + 128 * (j + 1)].astype(BF)
                cj = xbc[:, 1536 + 128 * j:1536 + 128 * (j + 1)].astype(BF)
                g = _dot_nt(cj, bj)
                hj = st_ref[c, sl, :]
                hj_bf = hj.astype(BF)
                dhj = dh_ref[sl, :]
                dhj_bf = dhj.astype(BF)
                zj = _dot_nt(cj, hj_bf)
                qj = _dot_nt(bj, dhj_bf)
                lm = _group_decay(acs, acs_t, j, causal4)
                sc = jnp.concatenate([g] * 4, axis=0) * lm
                sc_bf = sc.astype(BF)
                dym = jnp.where(same, jnp.concatenate([dyp[:, sl]] * 4, axis=0), 0.0).astype(BF)
                dsc = _dot_nt(dym, xd_bf[:, sl])
                dxd = _dot_tn(sc_bf, dym) + qj * dec_x[:, sl]
                m = dsc * sc
                dg_bf = _fold_heads(dsc * lm).astype(BF)
                rs = jnp.sum(m, axis=1, keepdims=True)
                e2 = dhj * hj
                for hh in range(4):
                    oh = _onehot_lane(4 * j + hh)
                    dacs = dacs + oh * rs[CHUNK * hh:CHUNK * (hh + 1)]
                    hsum = hsum + oh * jnp.sum(jnp.sum(e2[64 * hh:64 * (hh + 1)], axis=0, keepdims=True), axis=1, keepdims=True)
                sel = (sel_rows + 4 * j == sel_lanes).astype(BF)
                hi = m.astype(BF)
                rem = m - hi.astype(F32)
                mid = rem.astype(BF)
                lo = (rem - mid.astype(F32)).astype(BF)
                dacs = dacs - (_dot_tn(hi, sel) + _dot_tn(mid, sel) + _dot_tn(lo, sel))
                p1_l.append(dz[:, sl] * zj)
                p2_l.append(qj * xdd[:, sl])
                p3_l.append(dxd * xs[:, sl])
                dxs_ref[pl.ds(r0, CHUNK), sl] = dd_x[:, sl] * dyp[:, sl] + dxd * dt_x[:, sl]
                dxs_ref[pl.ds(r0, CHUNK), 1536 + 128 * j:1536 + 128 * (j + 1)] = _dot(dg_bf, bj) + _dot(dz_bf[:, sl], hj_bf)
                dxs_ref[pl.ds(r0, CHUNK), 1024 + 128 * j:1024 + 128 * (j + 1)] = _dot_tn(dg_bf, cj) + _dot(xdd_bf[:, sl], dhj_bf)
                dh_ref[sl, :] = _group_last_decay(acs_t, j) * dhj + _dot_tn(dz_bf[:, sl], cj)
            stacked = jnp.concatenate([jnp.concatenate(p1_l, axis=1), jnp.concatenate(p2_l, axis=1), jnp.concatenate(p3_l, axis=1)], axis=0)
            red = _dot_exact(stacked, e16t)
            r1, r2, ddtc = red[0:CHUNK], red[CHUNK:2 * CHUNK], red[2 * CHUNK:3 * CHUNK]
            tot = jnp.sum(r2, axis=0, keepdims=True) + jnp.exp(acs[CHUNK - 1:CHUNK, :]) * hsum
            da = _rev_cumsum_rows(dacs + r1 - r2 + last_row * tot)
            ddts_ref[pl.ds(r0, CHUNK), :] = ddtc + da * a_neg
            dal_acc = dal_acc + jnp.sum(da * dtc, axis=0, keepdims=True)
            return dal_acc, ddx_acc

        def chunk_pair(k2, carry):
            return chunk(2 * k2 + 1, chunk(2 * k2, carry))

        dal_acc, ddx_acc = lax.fori_loop(0, ncl // 2, chunk_pair, (jnp.zeros((1, LANES), F32), jnp.zeros((1, D_MODEL), F32)))
        dal_ref[...] += dal_acc * a_neg
        ddd_ref[...] += _dot_exact(jnp.broadcast_to(ddx_acc, (8, D_MODEL)), e16t)[0:1, :]
        ddt_raw = ddts_ref[...] * _sigmoid(dt_ref[...] + dtb_ref[...])
        ddt_ref[...] = ddt_raw
        ddtb_ref[...] += jnp.sum(ddt_raw, axis=0, keepdims=True)

        pre = pre_ref[...]
        sp = _sigmoid(pre)
        dpre = dxs_ref[...] * sp * (1.0 + pre * (1.0 - sp))
        dp_ref[0:tl, :] = dpre
        dcb_ref[...] += jnp.sum(dpre, axis=0, keepdims=True)
        dx = jnp.zeros((tl, 2048), F32)
        for k in range(4):
            dcw_ref[k:k + 1, :] += jnp.sum(dpre * xp_ref[5 + k:5 + k + tl, :], axis=0, keepdims=True)
            dx = dx + cw_ref[k:k + 1, :] * dp_ref[3 - k:3 - k + tl, :]
        dxbc_ref[...] = dx.astype(BF)
        dp_ref[tl:tl + 8, :] = dp_ref[0:8, :]

    vec = lambda w: jax.ShapeDtypeStruct((1, w), F32)
    rrow = lambda w: pl.BlockSpec((tl, w), lambda i: (t_of(i), 0))
    return pl.pallas_call(
        body, name="ssd_bwd", grid=(nt,),
        in_specs=[rrow(D_MODEL), rrow(D_MODEL), rows(1024, 0), rows(1024, 1), rows(1024, 2), halo(1), halo(2), rows(LANES, 0),
                  pl.BlockSpec((ncl, 1024, SSD_STATE), lambda i: (t_of(i), 0, 0)), rrow(2048),
                  _const_spec((4, 2048)), _const_spec((1, LANES)), _const_spec((1, LANES)),
                  _const_spec((1, LANES)), _const_spec((1, D_MODEL))],
        out_specs=[rrow(2048), rrow(LANES), rrow(D_MODEL), _const_spec((8, 2048)), _const_spec((1, 2048)),
                   _const_spec((1, LANES)), _const_spec((1, LANES)), _const_spec((1, LANES)), _const_spec((1, D_MODEL))],
        out_shape=[jax.ShapeDtypeStruct((L, 2048), BF), jax.ShapeDtypeStruct((L, LANES), F32), jax.ShapeDtypeStruct((L, D_MODEL), BF),
                   jax.ShapeDtypeStruct((8, 2048), F32), vec(2048), vec(LANES), vec(LANES), vec(LANES), vec(D_MODEL)],
        scratch_shapes=[pltpu.VMEM((tl + 8, 2048), F32), pltpu.VMEM((tl, 2048), F32),
                        pltpu.VMEM((tl, LANES), F32), pltpu.VMEM((tl, D_MODEL), F32), pltpu.VMEM((tl, 2048), F32),
                        pltpu.VMEM((tl, LANES), F32), pltpu.VMEM((tl + 8, 2048), F32), pltpu.VMEM((1024, SSD_STATE), F32)],
        compiler_params=_params(("arbitrary",)),
    )(dyssd, ypre, proj, proj, proj, proj, proj, pdt, states, pre_act, conv_w, dt_bias, a_log, ssd_d, norm_w)


def _head_fwd_bwd(x, ys5, yssd, p, target, w_out, w_gate, w_proj, ple_nw, fin_nw):
    L = x.shape[0]
    tl = min(TOKEN_TILE, L)
    inv_d = 1.0 / D_MODEL

    def body(x_ref, ys_ref, yd_ref, p_ref, t_ref, wo_ref, wg_ref, wp_ref, pnw_ref, fnw_ref,
             loss_ref, dys_ref, dyd_ref, dh1_ref, n2_ref, dgl_ref, dpp_ref, dpnw_ref, dfnw_ref):
        @pl.when(pl.program_id(0) == 0)
        def _():
            loss_ref[...] = jnp.zeros_like(loss_ref)
            dpnw_ref[...] = jnp.zeros_like(dpnw_ref)
            dfnw_ref[...] = jnp.zeros_like(dfnw_ref)

        h1 = x_ref[...] + _dot(ys_ref[...], wo_ref[0:1024, :]) + _dot(yd_ref[...], wo_ref[1024:2048, :])
        r1 = lax.rsqrt(jnp.mean(h1 * h1, axis=-1, keepdims=True) + EPS)
        hh1 = h1 * r1
        n2 = (hh1 * pnw_ref[...]).astype(BF)
        gate = _sigmoid(_dot(n2, wg_ref[...]))
        pp = _dot(p_ref[...].astype(BF), wp_ref[...])
        h2 = h1 + pp * gate
        r2 = lax.rsqrt(jnp.mean(h2 * h2, axis=-1, keepdims=True) + EPS)
        hh2 = h2 * r2
        err = hh2 * fnw_ref[...] - t_ref[...]
        loss_ref[...] += 0.5 * inv_d * jnp.sum(err * err)
        dyo = err * inv_d
        dfnw_ref[...] += jnp.sum(dyo * hh2, axis=0, keepdims=True)
        g2 = dyo * fnw_ref[...]
        dh2 = r2 * (g2 - hh2 * jnp.mean(g2 * hh2, axis=-1, keepdims=True))
        dpp_ref[...] = (dh2 * gate).astype(BF)
        dgl = (dh2 * pp * gate * (1.0 - gate)).astype(BF)
        dgl_ref[...] = dgl
        n2_ref[...] = n2
        dn2 = _dot_nt(dgl, wg_ref[...])
        dpnw_ref[...] += jnp.sum(dn2 * hh1, axis=0, keepdims=True)
        g1 = dn2 * pnw_ref[...]
        dh1 = dh2 + r1 * (g1 - hh1 * jnp.mean(g1 * hh1, axis=-1, keepdims=True))
        dh1_ref[...] = dh1
        dh1_bf = dh1.astype(BF)
        dys_ref[...] = _dot_nt(dh1_bf, wo_ref[0:1024, :]).astype(BF)
        dyd_ref[...] = _dot_nt(dh1_bf, wo_ref[1024:2048, :])

    big = jax.ShapeDtypeStruct((L, D_MODEL), BF)
    vec = jax.ShapeDtypeStruct((1, D_MODEL), F32)
    return pl.pallas_call(
        body, name="head_fwd_bwd", grid=(L // tl,),
        in_specs=[_row_spec(tl, D_MODEL), _row_spec(tl, D_MODEL), _row_spec(tl, D_MODEL), _row_spec(tl, 256), _row_spec(tl, D_MODEL),
                  _const_spec((2048, D_MODEL)), _const_spec((D_MODEL, D_MODEL)), _const_spec((256, D_MODEL)),
                  _const_spec((1, D_MODEL)), _const_spec((1, D_MODEL))],
        out_specs=[_const_spec((8, LANES)), _row_spec(tl, D_MODEL), _row_spec(tl, D_MODEL), _row_spec(tl, D_MODEL),
                   _row_spec(tl, D_MODEL), _row_spec(tl, D_MODEL), _row_spec(tl, D_MODEL), _const_spec((1, D_MODEL)), _const_spec((1, D_MODEL))],
        out_shape=[jax.ShapeDtypeStruct((8, LANES), F32), big, jax.ShapeDtypeStruct((L, D_MODEL), F32),
                   jax.ShapeDtypeStruct((L, D_MODEL), F32), big, big, big, vec, vec],
        compiler_params=_params(("arbitrary",)),
    )(x, ys5, yssd, p, target, w_out, w_gate, w_proj, ple_nw, fin_nw)


def _pad_lanes(v):
    return jnp.pad(v.reshape(1, -1), ((0, 0), (0, LANES - v.size)))


def _local_step(x, p, target, w):
    L = x.shape[0]
    nc = L // CHUNK
    nsteps = max(1, (nc - 1).bit_length())
    w_in_t = w["w_in"]
    w_main = w_in_t[:D_MAIN]
    w_dt = jnp.pad(w_in_t[D_MAIN:], ((0, LANES - SSD_HEADS), (0, 0)))
    norm_w = w["norm_w"].reshape(1, -1)
    s5_d = w["s5_D"].reshape(1, -1)
    b_glu = w["s5_b_glu"].reshape(1, -1)
    conv_b = w["conv_b"].reshape(1, -1)
    dtb, alog, ssd_d = _pad_lanes(w["dt_bias"]), _pad_lanes(w["A_log"]), _pad_lanes(w["ssd_D"])
    ssd_nw = w["ssd_norm_w"].reshape(1, -1)
    ple_nw = w["ple_norm_w"].reshape(1, -1)
    fin_nw = w["final_norm_w"].reshape(1, -1)

    s5_args = (w["s5_A_re"], w["s5_A_im"], w["s5_log_dt"], w["s5_B_re"], w["s5_B_im"], w["s5_C_re"], w["s5_C_im"])
    small, small_vjp = jax.vjp(_s5_discretise, *s5_args)
    bbp, pwr, cnp, pwf = _s5_table_factors(*small)
    p1, p2 = _s5_scan_powers(w["s5_A_re"], w["s5_A_im"], w["s5_log_dt"], nsteps)

    hn, proj, pssd, pdt = _in_proj_fwd(x, norm_w, w_main, w_dt)
    uflat = _flat_hs(proj[:, :D_MODEL], nc)
    yflat, hsave = _s5_core_fwd(uflat, bbp, pwr, cnp, pwf, p1, p2)
    yssm = _unflat_tk(yflat, nc)
    ys5 = _s5_post_fwd(yssm, proj, s5_d, w["s5_w_glu"], b_glu)
    yssd, ypre, states, pre_act = _ssd_fwd(pssd, pdt, w["conv_w"], conv_b, dtb, alog, ssd_d, ssd_nw)
    (loss8, dys5, dyssd, dh1, n2, dgl2, dpp, g_ple_nw, g_fin_nw) = _head_fwd_bwd(
        x, ys5, yssd, p, target, w["w_out"], w["w_ple_gate"], w["w_ple_proj"], ple_nw, fin_nw)

    (dxbc, ddt, dzd, g_cw, g_cb, g_dtb, g_alog, g_ssd_d, g_ssd_nw) = _ssd_bwd(
        dyssd, ypre, pssd, pdt, states, pre_act, w["conv_w"], dtb, alog, ssd_d, ssd_nw)
    dzs, dyssm, a_glu, dgl1, g_bglu, g_s5d = _s5_post_bwd(dys5, yssm, proj, s5_d, w["s5_w_glu"], b_glu)
    duflat, dbk, r12, q12, xy, uv, xy0, uv0, da8 = _s5_core_bwd(uflat, _flat_tk(dyssm, nc), hsave, bbp, pwr, cnp, pwf, p1, p2)
    da64 = jnp.concatenate([da8[:, 0, :S5_STATE] + da8[:, 0, S5_STATE:], da8[:, 1, S5_STATE:] - da8[:, 1, :S5_STATE]], axis=-1)
    g_s5 = small_vjp(_s5_small_cotangents(dbk, r12, q12, xy, uv, xy0, uv0, da64))
    gx, du, g_norm_w = _in_proj_bwd(x, norm_w, dh1, _unflat_hs(duflat, nc), dyssm, s5_d, dzs, dzd, dxbc, ddt, w_main, w_dt)

    g_w_in = jnp.concatenate([
        _matmul_tn(du, hn, "dw_in_u"), _matmul_tn(dzs, hn, "dw_in_zs"), _matmul_tn(dzd, hn, "dw_in_zd"),
        _matmul_tn(dxbc, hn, "dw_in_xbc"), _matmul_tn(ddt, hn, "dw_in_dt")[:SSD_HEADS]], axis=0)
    grads = {
        "norm_w": g_norm_w, "w_in": g_w_in,
        "s5_A_re": g_s5[0], "s5_A_im": g_s5[1], "s5_log_dt": g_s5[2], "s5_B_re": g_s5[3], "s5_B_im": g_s5[4],
        "s5_C_re": g_s5[5], "s5_C_im": g_s5[6], "s5_D": g_s5d, "s5_w_glu": _matmul_tn(a_glu, dgl1, "dw_glu"), "s5_b_glu": g_bglu,
        "conv_w": g_cw[:4], "conv_b": g_cb, "dt_bias": g_dtb[:, :SSD_HEADS], "A_log": g_alog[:, :SSD_HEADS],
        "ssd_D": g_ssd_d[:, :SSD_HEADS], "ssd_norm_w": g_ssd_nw,
        "w_out": jnp.concatenate([_matmul_tn(ys5, dh1, "dw_out_s5"), _matmul_tn(yssd, dh1, "dw_out_ssd")], axis=0),
        "ple_norm_w": g_ple_nw, "w_ple_gate": _matmul_tn(n2, dgl2, "dw_gate"), "w_ple_proj": _matmul_tn(p, dpp, "dw_proj"),
        "final_norm_w": g_fin_nw,
    }
    return loss8[0, 0], gx, grads


WEIGHTS = ("norm_w", "w_in", "s5_A_re", "s5_A_im", "s5_log_dt", "s5_B_re", "s5_B_im", "s5_C_re", "s5_C_im", "s5_D", "s5_w_glu",
           "s5_b_glu", "conv_w", "conv_b", "dt_bias", "A_log", "ssd_D", "ssd_norm_w", "w_out", "ple_norm_w", "w_ple_gate",
           "w_ple_proj", "final_norm_w")
BIG = {"w_in": ((1284, 1024), 0), "s5_w_glu": ((256, 1024), 0), "w_out": ((512, 1024), 0), "w_ple_gate": ((256, 1024), 0),
       "w_ple_proj": ((256, 256), 1)}
SMALL = {"norm_w": (1024,), "s5_A_re": (64, 64), "s5_A_im": (64, 64), "s5_log_dt": (64,), "s5_B_re": (64, 64, 16),
         "s5_B_im": (64, 64, 16), "s5_C_re": (64, 16, 64), "s5_C_im": (64, 16, 64), "s5_D": (1024,), "s5_b_glu": (1024,),
         "conv_w": (4, 2048), "conv_b": (2048,), "dt_bias": (16,), "A_log": (16,), "ssd_D": (16,), "ssd_norm_w": (1024,),
         "ple_norm_w": (1024,), "final_norm_w": (1024,)}
BIG_ROWS = {n: s[0] * s[1] // LANES for n, (s, _) in BIG.items()}
BIG_ROWS_TOTAL = sum(BIG_ROWS.values())
SMALL_TOTAL = sum(math.prod(s) for s in SMALL.values())
SMALL_PIECE_ROWS = -(-SMALL_TOTAL // (N_CHIPS * 16 * LANES)) * 16
HALF_ROWS = (BIG_ROWS_TOTAL + SMALL_PIECE_ROWS) // 2
SMALL_ROW0 = BIG_ROWS_TOTAL - HALF_ROWS


def _mesh_pos():
    return lax.axis_index("x"), lax.axis_index("y"), lax.axis_index("c")


def _other_chips(x, y):
    return [(1 - x, y), (x, 1 - y), (1 - x, 1 - y)]


def _comm_params():
    return pltpu.CompilerParams(has_side_effects=True)


def _all_gather_chips(wpack, cw):
    half = wpack.shape[0] // 2

    def body(w_ref, c_ref, wo_ref, co_ref, send_sems, recv_sems, fwd_send, fwd_recv, loc_sems):
        x, y, c = _mesh_pos()
        me = 2 * x + y
        sib = (x, y, 1 - c)
        mine = pl.ds(c * half, half)
        theirs = pl.ds((1 - c) * half, half)
        others = _other_chips(x, y)
        loc = [pltpu.make_async_copy(c_ref, co_ref.at[me], loc_sems.at[0])]
        for cp in loc:
            cp.start()

        def from_chip(k, chip, dev):
            return pltpu.make_async_remote_copy(w_ref.at[mine], wo_ref.at[chip, mine], send_sems.at[2 * k], recv_sems.at[2 * k],
                                                device_id=dev, device_id_type=MESH)

        def conv_from(k, chip, dev):
            return pltpu.make_async_remote_copy(c_ref, co_ref.at[chip], send_sems.at[2 * k + 1], recv_sems.at[2 * k + 1],
                                                device_id=dev, device_id_type=MESH)

        def passed(k, chip, rows):
            return pltpu.make_async_remote_copy(wo_ref.at[chip, rows], wo_ref.at[chip, rows], fwd_send.at[k], fwd_recv.at[k],
                                                device_id=sib, device_id_type=MESH)

        sends = []
        for k, (px, py) in enumerate(others):
            sends += [from_chip(k, me, (px, py, c)), conv_from(k, me, (px, py, c))]
        for cp in sends:
            cp.start()
        fwds = []
        for k, (px, py) in enumerate(others):
            chip = 2 * px + py
            from_chip(k, chip, (px, py, c)).wait_recv()
            fwds.append(passed(k, chip, mine))
            fwds[-1].start()
        for k, (px, py) in enumerate(others):
            chip = 2 * px + py
            passed(k, chip, theirs).wait_recv()
            conv_from(k, chip, (px, py, c)).wait_recv()
        for cp in sends + fwds:
            cp.wait_send()
        for cp in loc:
            cp.wait()

    return pl.pallas_call(
        body, name="all_gather_weights", in_specs=[ANY, ANY], out_specs=[ANY, ANY],
        out_shape=[jax.ShapeDtypeStruct((N_CHIPS,) + wpack.shape, wpack.dtype), jax.ShapeDtypeStruct((N_CHIPS,) + cw.shape, cw.dtype)],
        scratch_shapes=[pltpu.SemaphoreType.DMA((6,)), pltpu.SemaphoreType.DMA((6,)), pltpu.SemaphoreType.DMA((3,)),
                        pltpu.SemaphoreType.DMA((3,)), pltpu.SemaphoreType.DMA((1,))],
        compiler_params=_comm_params(),
    )(wpack, cw)


def _exchange_pair(gp):
    def body(g_ref, r_ref, send_sems, recv_sems):
        x, y, c = _mesh_pos()
        cps = [pltpu.make_async_remote_copy(g_ref.at[s, 1 - c], r_ref.at[s], send_sems.at[s], recv_sems.at[s],
                                            device_id=(x, y, 1 - c), device_id_type=MESH) for s in range(N_CHIPS)]
        for cp in cps:
            cp.start()
        for cp in cps:
            cp.wait()

    return pl.pallas_call(
        body, name="grad_exchange_pair", in_specs=[ANY], out_specs=ANY,
        out_shape=jax.ShapeDtypeStruct((N_CHIPS,) + gp.shape[2:], gp.dtype),
        scratch_shapes=[pltpu.SemaphoreType.DMA((N_CHIPS,)), pltpu.SemaphoreType.DMA((N_CHIPS,))],
        compiler_params=_comm_params(),
    )(gp)


def _pair_sum(mine, from_sibling):
    def body(a_ref, b_ref, bf_ref, tail_ref):
        s = a_ref[0] + b_ref[0]
        bf_ref[0] = s.astype(BF)
        tail_ref[0] = s[SMALL_ROW0:, :]

    piece = pl.BlockSpec((1, HALF_ROWS, LANES), lambda i: (i, 0, 0))
    return pl.pallas_call(
        body, name="grad_pair_sum", grid=(N_CHIPS,), in_specs=[piece, piece],
        out_specs=[piece, pl.BlockSpec((1, SMALL_PIECE_ROWS, LANES), lambda i: (i, 0, 0))],
        out_shape=[jax.ShapeDtypeStruct((N_CHIPS, HALF_ROWS, LANES), BF), jax.ShapeDtypeStruct((N_CHIPS, SMALL_PIECE_ROWS, LANES), F32)],
        compiler_params=_params(("parallel",)),
    )(mine, from_sibling)


def _exchange_chips(ps_bf, ps_tail):
    def body(p_ref, t_ref, r_ref, rt_ref, send_sems, recv_sems):
        x, y, c = _mesh_pos()
        cps = []
        for k, (px, py) in enumerate(_other_chips(x, y)):
            cps.append(pltpu.make_async_remote_copy(p_ref.at[2 * px + py], r_ref.at[k], send_sems.at[2 * k], recv_sems.at[2 * k],
                                                    device_id=(px, py, c), device_id_type=MESH))
            cps.append(pltpu.make_async_remote_copy(t_ref.at[2 * px + py], rt_ref.at[k], send_sems.at[2 * k + 1],
                                                    recv_sems.at[2 * k + 1], device_id=(px, py, c), device_id_type=MESH))
        for cp in cps:
            cp.start()
        for cp in cps:
            cp.wait()

    return pl.pallas_call(
        body, name="grad_exchange_chips", in_specs=[ANY, ANY], out_specs=[ANY, ANY],
        out_shape=[jax.ShapeDtypeStruct((N_CHIPS - 1,) + ps_bf.shape[1:], ps_bf.dtype),
                   jax.ShapeDtypeStruct((N_CHIPS - 1,) + ps_tail.shape[1:], ps_tail.dtype)],
        scratch_shapes=[pltpu.SemaphoreType.DMA((6,)), pltpu.SemaphoreType.DMA((6,))],
        compiler_params=_comm_params(),
    )(ps_bf, ps_tail)


def _chip_sum(own_bf, own_tail, others_bf, others_tail):
    def body(ob_ref, ot_ref, b_ref, t_ref, o_ref):
        acc = ob_ref[0:SMALL_ROW0, :].astype(F32)
        tail = ot_ref[...]
        for k in range(N_CHIPS - 1):
            acc = acc + b_ref[k, 0:SMALL_ROW0, :].astype(F32)
            tail = tail + t_ref[k]
        o_ref[0:SMALL_ROW0, :] = acc
        o_ref[SMALL_ROW0:, :] = tail

    return pl.pallas_call(
        body, name="grad_chip_sum", out_shape=jax.ShapeDtypeStruct((HALF_ROWS, LANES), F32),
        compiler_params=_params(),
    )(own_bf, own_tail, others_bf, others_tail)


def _swap_reduced_halves(gh):
    def body(g_ref, o_ref, send_sem, recv_sem):
        x, y, c = _mesh_pos()
        cp = pltpu.make_async_remote_copy(g_ref, o_ref, send_sem, recv_sem, device_id=(x, y, 1 - c), device_id_type=MESH)
        cp.start()
        cp.wait()

    return pl.pallas_call(
        body, name="grad_swap_halves", in_specs=[ANY], out_specs=ANY,
        out_shape=jax.ShapeDtypeStruct(gh.shape, gh.dtype),
        scratch_shapes=[pltpu.SemaphoreType.DMA, pltpu.SemaphoreType.DMA],
        compiler_params=_comm_params(),
    )(gh)


def _gather_small(second_half):
    def body(gs_ref, sm_ref, send_sems, recv_sems, loc_sem):
        x, y, c = _mesh_pos()
        me = 2 * x + y
        small = gs_ref.at[pl.ds(SMALL_ROW0, SMALL_PIECE_ROWS)]
        loc = pltpu.make_async_copy(small, sm_ref.at[me], loc_sem)
        loc.start()
        cps = [pltpu.make_async_remote_copy(small, sm_ref.at[me], send_sems.at[k], recv_sems.at[k],
                                            device_id=(px, py, c), device_id_type=MESH)
               for k, (px, py) in enumerate(_other_chips(x, y))]
        for cp in cps:
            cp.start()
        for cp in cps:
            cp.wait()
        loc.wait()

    return pl.pallas_call(
        body, name="grad_gather_small", in_specs=[ANY], out_specs=ANY,
        out_shape=jax.ShapeDtypeStruct((N_CHIPS, SMALL_PIECE_ROWS, LANES), second_half.dtype),
        scratch_shapes=[pltpu.SemaphoreType.DMA((3,)), pltpu.SemaphoreType.DMA((3,)), pltpu.SemaphoreType.DMA],
        compiler_params=_comm_params(),
    )(second_half)


def _pack_grads(grads):
    small = jnp.concatenate([grads[n].reshape(-1) for n in SMALL])
    small = jnp.pad(small, (0, N_CHIPS * SMALL_PIECE_ROWS * LANES - SMALL_TOTAL)).reshape(N_CHIPS, SMALL_PIECE_ROWS, LANES)
    pieces = []
    for s in range(N_CHIPS):
        rows = []
        for n, (shp, axis) in BIG.items():
            g = grads[n]
            blk = g[s * shp[0]:(s + 1) * shp[0], :] if axis == 0 else g[:, s * shp[1]:(s + 1) * shp[1]]
            rows.append(blk.reshape(-1, LANES))
        rows.append(small[s])
        pieces.append(jnp.concatenate(rows, axis=0).reshape(2, HALF_ROWS, LANES))
    return jnp.stack(pieces)


def _unpack_shard(first_half, second_half):
    rows = jnp.concatenate([first_half, second_half], axis=0)
    out, r0 = {}, 0
    for n, (shp, _) in BIG.items():
        out[n] = rows[r0:r0 + BIG_ROWS[n]].reshape(shp)
        r0 += BIG_ROWS[n]
    return out


def _unpack_small(sm):
    flat = sm.reshape(-1)
    out, o = {}, 0
    for n, shp in SMALL.items():
        k = math.prod(shp)
        out[n] = flat[o:o + k].reshape(shp)
        o += k
    return out


def _as_2d(a):
    n = a.size
    if a.ndim >= 2 and a.shape[-1] > 1024:
        return a.reshape(-1, a.shape[-1])
    if n % 1024 == 0:
        return a.reshape(n // 1024, 1024)
    return a.reshape(1, n)


def _adamw(w, g, m, v, name):
    shape = w.shape
    w2, g2, m2, v2 = (_as_2d(a) for a in (w, g, m, v))
    rows, cols = w2.shape
    rb = 256 if rows >= 512 else rows
    by_cols = rows % rb != 0

    def body(w_ref, g_ref, m_ref, v_ref, d_ref, mo_ref, vo_ref):
        gv = g_ref[...]
        mn = ADAM_B1 * m_ref[...] + (1.0 - ADAM_B1) * gv
        vn = ADAM_B2 * v_ref[...] + (1.0 - ADAM_B2) * (gv * gv)
        m_hat = mn / (1.0 - ADAM_B1 ** ADAM_STEP)
        v_hat = vn / (1.0 - ADAM_B2 ** ADAM_STEP)
        d_ref[...] = -ADAM_LR * (m_hat / (jnp.sqrt(v_hat) + ADAM_EPS) + ADAM_WD * w_ref[...])
        mo_ref[...] = mn
        vo_ref[...] = vn

    spec = pl.BlockSpec((rows, 256), lambda i: (0, i)) if by_cols else _row_spec(rb, cols)
    sds = jax.ShapeDtypeStruct((rows, cols), F32)
    d, mo, vo = pl.pallas_call(
        body, name=name, grid=(cols // 256 if by_cols else rows // rb,), in_specs=[spec] * 4, out_specs=[spec] * 3, out_shape=[sds] * 3,
        compiler_params=_params(("parallel",)),
    )(w2, g2, m2, v2)
    return d.reshape(shape), mo.reshape(shape), vo.reshape(shape)


def kernel(x, p, norm_w, w_in, s5_A_re, s5_A_im, s5_log_dt, s5_B_re, s5_B_im, s5_C_re, s5_C_im, s5_D, s5_w_glu, s5_b_glu, conv_w, conv_b, dt_bias, A_log, ssd_D, ssd_norm_w, w_out, ple_norm_w, w_ple_gate, w_ple_proj, final_norm_w, loss_target, m_norm_w, m_w_in, m_s5_A_re, m_s5_A_im, m_s5_log_dt, m_s5_B_re, m_s5_B_im, m_s5_C_re, m_s5_C_im, m_s5_D, m_s5_w_glu, m_s5_b_glu, m_conv_w, m_conv_b, m_dt_bias, m_A_log, m_ssd_D, m_ssd_norm_w, m_w_out, m_ple_norm_w, m_w_ple_gate, m_w_ple_proj, m_final_norm_w, v_norm_w, v_w_in, v_s5_A_re, v_s5_A_im, v_s5_log_dt, v_s5_B_re, v_s5_B_im, v_s5_C_re, v_s5_C_im, v_s5_D, v_s5_w_glu, v_s5_b_glu, v_conv_w, v_conv_b, v_dt_bias, v_A_log, v_ssd_D, v_ssd_norm_w, v_w_out, v_ple_norm_w, v_w_ple_gate, v_w_ple_proj, v_final_norm_w):
    given = (norm_w, w_in, s5_A_re, s5_A_im, s5_log_dt, s5_B_re, s5_B_im, s5_C_re, s5_C_im, s5_D, s5_w_glu, s5_b_glu, conv_w, conv_b,
             dt_bias, A_log, ssd_D, ssd_norm_w, w_out, ple_norm_w, w_ple_gate, w_ple_proj, final_norm_w)
    given_m = (m_norm_w, m_w_in, m_s5_A_re, m_s5_A_im, m_s5_log_dt, m_s5_B_re, m_s5_B_im, m_s5_C_re, m_s5_C_im, m_s5_D, m_s5_w_glu,
               m_s5_b_glu, m_conv_w, m_conv_b, m_dt_bias, m_A_log, m_ssd_D, m_ssd_norm_w, m_w_out, m_ple_norm_w, m_w_ple_gate,
               m_w_ple_proj, m_final_norm_w)
    given_v = (v_norm_w, v_w_in, v_s5_A_re, v_s5_A_im, v_s5_log_dt, v_s5_B_re, v_s5_B_im, v_s5_C_re, v_s5_C_im, v_s5_D, v_s5_w_glu,
               v_s5_b_glu, v_conv_w, v_conv_b, v_dt_bias, v_A_log, v_ssd_D, v_ssd_norm_w, v_w_out, v_ple_norm_w, v_w_ple_gate,
               v_w_ple_proj, v_final_norm_w)
    wts, mom, var = dict(zip(WEIGHTS, given)), dict(zip(WEIGHTS, given_m)), dict(zip(WEIGHTS, given_v))
    drop = lambda n, a: a if n == "final_norm_w" else a[0]

    shard2d = lambda n, a: a[0].T if n == "w_in" else drop(n, a)
    wpack = jnp.concatenate([shard2d(n, wts[n]).astype(BF).reshape(-1, LANES) for n in BIG], axis=0)
    wall, cwall = _all_gather_chips(wpack, drop("conv_w", wts["conv_w"]))
    chip = 2 * lax.axis_index("x") + lax.axis_index("y")
    full, r0 = {}, 0
    for n, (shp, axis) in BIG.items():
        blk = lax.dynamic_update_slice_in_dim(wall[:, r0:r0 + BIG_ROWS[n]].reshape((N_CHIPS,) + shp),
                                              shard2d(n, wts[n]).astype(BF)[None], chip, axis=0)
        full[n] = blk.reshape(N_CHIPS * shp[0], shp[1]) if axis == 0 else blk.transpose(1, 0, 2).reshape(shp[0], N_CHIPS * shp[1])
        r0 += BIG_ROWS[n]
    for n in SMALL:
        full[n] = drop(n, wts[n])
    full["conv_w"] = cwall.transpose(1, 0, 2).reshape(4, 2048)

    loss, gx, grads = _local_step(x[0], p[0, 0], loss_target[0], full)
    loss = lax.psum(loss, MESH_AXES)

    gp = _pack_grads({n: grads[n].reshape(SMALL[n]) if n in SMALL else grads[n] for n in WEIGHTS})
    c = lax.axis_index("c")
    from_sibling = _exchange_pair(gp)
    mine = lax.dynamic_index_in_dim(gp, c, axis=1, keepdims=False)
    ps_bf, ps_tail = _pair_sum(mine, from_sibling)
    others_bf, others_tail = _exchange_chips(ps_bf, ps_tail)
    own_bf = lax.dynamic_index_in_dim(ps_bf, chip, axis=0, keepdims=False)
    own_tail = lax.dynamic_index_in_dim(ps_tail, chip, axis=0, keepdims=False)
    reduced_half = _chip_sum(own_bf, own_tail, others_bf, others_tail)
    sibling_half = _swap_reduced_halves(reduced_half)
    first_half = jnp.where(c == 0, reduced_half, sibling_half)
    second_half = jnp.where(c == 0, sibling_half, reduced_half)
    sm = _gather_small(second_half)
    g_final = {**_unpack_small(sm), **_unpack_shard(first_half, second_half)}
    g_final["conv_w"] = lax.dynamic_slice_in_dim(g_final["conv_w"], chip * 512, 512, axis=1)

    outs_g, outs_d, outs_m, outs_v = [], [], [], []
    for n in WEIGHTS:
        if n == "w_in":
            res = _adamw(wts[n][0].T, g_final[n], mom[n][0].T, var[n][0].T, "adamw_" + n)
            g, d, mo, vo = (a.T[None] for a in (g_final[n],) + res)
        else:
            g = g_final[n].reshape(wts[n].shape)
            d, mo, vo = _adamw(wts[n], g, mom[n], var[n], "adamw_" + n)
        outs_g.append(g)
        outs_d.append(d)
        outs_m.append(mo)
        outs_v.append(vo)
    return (loss, gx[None], *outs_g, *outs_d, *outs_m, *outs_v)
```

```python
import functools
import math

import jax
import jax.numpy as jnp
from jax import lax
from jax.experimental import pallas as pl
from jax.experimental.pallas import tpu as pltpu

F32 = jnp.float32
BF = jnp.bfloat16
EPS = 1e-6
CHUNK = 64
D_MODEL = 1024
S5_GROUPS = 64
S5_CH = 16
S5_STATE = 64
SSD_HEADS = 16
SSD_HEAD_DIM = 64
SSD_GROUPS = 4
SSD_STATE = 128
D_MAIN = 5120
LANES = 128
TOKEN_TILE = 256
VMEM_LIMIT = 56 * 1024 * 1024
MESH_AXES = ("x", "y", "c")
N_CHIPS = 4
ADAM_LR, ADAM_B1, ADAM_B2, ADAM_EPS, ADAM_WD, ADAM_STEP = 0.001, 0.9, 0.999, 1e-08, 0.01, 10
MESH = pl.DeviceIdType.MESH
ANY = pl.BlockSpec(memory_space=pl.ANY)


def _dot(a, b):
    return jnp.dot(a, b, preferred_element_type=F32)


def _dot_nt(a, b):
    return lax.dot_general(a, b, (((1,), (1,)), ((), ())), preferred_element_type=F32)


def _dot_tn(a, b):
    return lax.dot_general(a, b, (((0,), (0,)), ((), ())), preferred_element_type=F32)


def _sigmoid(x):
    return 1.0 / (1.0 + jnp.exp(-x))


def _softplus(x):
    return jnp.maximum(x, 0.0) + jnp.log(1.0 + jnp.exp(-jnp.abs(x)))


_GELU_C = math.sqrt(2.0 / math.pi)


def _gelu(x):
    return 0.5 * x * (1.0 + jnp.tanh(_GELU_C * (x + 0.044715 * x * x * x)))


def _gelu_grad(x):
    th = jnp.tanh(_GELU_C * (x + 0.044715 * x * x * x))
    return 0.5 * (1.0 + th) + 0.5 * x * (1.0 - th * th) * _GELU_C * (1.0 + 3.0 * 0.044715 * x * x)


def _params(sem=None):
    return pltpu.CompilerParams(dimension_semantics=sem, vmem_limit_bytes=VMEM_LIMIT)


def _row_spec(tl, width, col=0):
    return pl.BlockSpec((tl, width), lambda i, col=col: (i, col))


def _const_spec(shape):
    nd = len(shape)
    return pl.BlockSpec(shape, lambda *_: (0,) * nd)


def _in_proj_fwd(x, norm_w, w_main, w_dt):
    L = x.shape[0]
    tl = min(TOKEN_TILE, L)

    def body(x_ref, nw_ref, wm_ref, wd_ref, hn_ref, u_ref, zs_ref, pssd_ref, pd_ref):
        xv = x_ref[...]
        r = lax.rsqrt(jnp.mean(xv * xv, axis=-1, keepdims=True) + EPS)
        hn = (xv * r * nw_ref[...]).astype(BF)
        hn_ref[...] = hn
        for j, o_ref in enumerate((u_ref, zs_ref)):
            o_ref[...] = _dot_nt(hn, wm_ref[j * 1024:(j + 1) * 1024, :]).astype(BF)
        for j in range(3):
            pssd_ref[:, j * 1024:(j + 1) * 1024] = _dot_nt(hn, wm_ref[(j + 2) * 1024:(j + 3) * 1024, :])
        pd_ref[...] = _dot_nt(hn, wd_ref[...])

    return pl.pallas_call(
        body, name="in_proj_fwd", grid=(L // tl,),
        in_specs=[_row_spec(tl, D_MODEL), _const_spec((1, D_MODEL)), _const_spec((D_MAIN, D_MODEL)), _const_spec((LANES, D_MODEL))],
        out_specs=[_row_spec(tl, D_MODEL), _row_spec(tl, D_MODEL), _row_spec(tl, D_MODEL), _row_spec(tl, 3072), _row_spec(tl, LANES)],
        out_shape=[jax.ShapeDtypeStruct((L, D_MODEL), BF), jax.ShapeDtypeStruct((L, D_MODEL), BF), jax.ShapeDtypeStruct((L, D_MODEL), BF),
                   jax.ShapeDtypeStruct((L, 3072), F32),
                   jax.ShapeDtypeStruct((L, LANES), F32)],
        compiler_params=_params(("arbitrary",)),
    )(x, norm_w, w_main, w_dt)


def _in_proj_bwd(x, norm_w, dh1, du_flat, dyssm, s5_d, dzs, dzd, dxbc, ddt, w_main, w_dt):
    L = x.shape[0]
    tl = min(TOKEN_TILE, L)

    def body(x_ref, nw_ref, dh1_ref, duf_ref, dys_ref, d_ref, dzs_ref, dzd_ref, dxbc_ref, ddt_ref, wm_ref, wd_ref,
             gx_ref, du_ref, gnw_ref):
        @pl.when(pl.program_id(0) == 0)
        def _():
            gnw_ref[...] = jnp.zeros_like(gnw_ref)

        du = (duf_ref[...].astype(F32) + dys_ref[...].astype(F32) * d_ref[...]).astype(BF)
        du_ref[...] = du
        dhn = _dot(du, wm_ref[0:1024, :])
        dhn += _dot(dzs_ref[...], wm_ref[1024:2048, :])
        dhn += _dot(dzd_ref[...], wm_ref[2048:3072, :])
        dhn += _dot(dxbc_ref[...], wm_ref[3072:5120, :])
        dhn += _dot(ddt_ref[...].astype(BF), wd_ref[...])
        xv = x_ref[...]
        r = lax.rsqrt(jnp.mean(xv * xv, axis=-1, keepdims=True) + EPS)
        xh = xv * r
        gnw_ref[...] += jnp.sum(dhn * xh, axis=0, keepdims=True)
        g = dhn * nw_ref[...]
        gx_ref[...] = dh1_ref[...] + r * (g - xh * jnp.mean(g * xh, axis=-1, keepdims=True))

    return pl.pallas_call(
        body, name="in_proj_bwd", grid=(L // tl,),
        in_specs=[_row_spec(tl, D_MODEL), _const_spec((1, D_MODEL)), _row_spec(tl, D_MODEL), _row_spec(tl, D_MODEL),
                  _row_spec(tl, D_MODEL), _const_spec((1, D_MODEL)), _row_spec(tl, D_MODEL), _row_spec(tl, D_MODEL),
                  _row_spec(tl, 2048), _row_spec(tl, LANES), _const_spec((D_MAIN, D_MODEL)), _const_spec((LANES, D_MODEL))],
        out_specs=[_row_spec(tl, D_MODEL), _row_spec(tl, D_MODEL), _const_spec((1, D_MODEL))],
        out_shape=[jax.ShapeDtypeStruct((L, D_MODEL), F32), jax.ShapeDtypeStruct((L, D_MODEL), BF), jax.ShapeDtypeStruct((1, D_MODEL), F32)],
        compiler_params=_params(("arbitrary",)),
    )(x, norm_w, dh1, du_flat, dyssm, s5_d, dzs, dzd, dxbc, ddt, w_main, w_dt)


def _matmul_tn(a, b, name):
    L, M = a.shape
    N = b.shape[1]
    tm, tn, tk = min(M, 1024), min(N, 1024), min(L, 1024)

    def body(a_ref, b_ref, o_ref):
        @pl.when(pl.program_id(2) == 0)
        def _():
            o_ref[...] = jnp.zeros_like(o_ref)

        o_ref[...] += _dot_tn(a_ref[...].astype(BF), b_ref[...].astype(BF))

    return pl.pallas_call(
        body, name=name, grid=(M // tm, N // tn, L // tk),
        in_specs=[pl.BlockSpec((tk, tm), lambda i, j, k: (k, i)), pl.BlockSpec((tk, tn), lambda i, j, k: (k, j))],
        out_specs=pl.BlockSpec((tm, tn), lambda i, j, k: (i, j)),
        out_shape=jax.ShapeDtypeStruct((M, N), F32),
        compiler_params=_params(("parallel", "parallel", "arbitrary")),
    )(a, b)


def _s5_discretise(a_re, a_im, log_dt, b_re, b_im, c_re, c_im):
    dt = jnp.exp(log_dt)[:, None]
    tau = jnp.arange(CHUNK + 1, dtype=F32)
    mag = jnp.exp((a_re * dt)[:, :, None] * tau)
    ang = (a_im * dt)[:, :, None] * tau
    pw_re, pw_im = mag * jnp.cos(ang), mag * jnp.sin(ang)
    er, ei = pw_re[:, :, 1] - 1.0, pw_im[:, :, 1]
    den = a_re * a_re + a_im * a_im
    beta_re, beta_im = (er * a_re + ei * a_im) / den, (ei * a_re - er * a_im) / den
    bb_re = (beta_re[:, :, None] * b_re - beta_im[:, :, None] * b_im).transpose(0, 2, 1)
    bb_im = (beta_re[:, :, None] * b_im + beta_im[:, :, None] * b_re).transpose(0, 2, 1)
    return bb_re, bb_im, c_re, c_im, pw_re, pw_im


def _s5_table_factors(bb_re, bb_im, c_re, c_im, pw_re, pw_im):
    pr = pw_re[:, :, CHUNK - 1::-1].transpose(0, 2, 1)
    pi = pw_im[:, :, CHUNK - 1::-1].transpose(0, 2, 1)
    bbp = jnp.concatenate([bb_re, bb_im], axis=-1)
    pwr = jnp.concatenate([pr, pi], axis=-1)
    cnp = jnp.concatenate([c_re, c_im], axis=-1)
    pwf = jnp.concatenate([pw_re.transpose(0, 2, 1), pw_im.transpose(0, 2, 1)], axis=-1)
    return bbp, pwr, cnp, jnp.pad(pwf, ((0, 0), (0, 7), (0, 0)))


def _s5_small_cotangents(dbk, r12, q12, xy, uv, xy0, uv0, da64):
    n = S5_STATE
    fold = lambda a: a[..., :n] + a[..., n:]
    fold_m = lambda a: a[..., n:] - a[..., :n]
    dbb_re = fold(r12[:, :S5_CH]) + dbk[..., :n]
    dbb_im = fold_m(r12[:, S5_CH:]) + dbk[..., n:]
    dpr, dpi = fold(q12[:, :CHUNK]), fold_m(q12[:, CHUNK:])
    dc_re = -fold_m(xy[:, :S5_CH]) - fold_m(xy0[:, :S5_CH])
    dc_im = -fold(xy[:, S5_CH:]) - fold(xy0[:, S5_CH:])
    dp1_re, dp1_im = -fold_m(uv[:, :CHUNK]), -fold(uv[:, CHUNK:])
    dp0_re, dp0_im = -fold_m(uv0[:, :CHUNK]), -fold(uv0[:, CHUNK:])
    zero = jnp.zeros((S5_GROUPS, n, 1), F32)
    dpw_re = (jnp.concatenate([(dpr[:, ::-1] + dp0_re).transpose(0, 2, 1), zero], axis=-1)
              + jnp.concatenate([zero, dp1_re.transpose(0, 2, 1)], axis=-1)).at[:, :, CHUNK].add(da64[:, :n])
    dpw_im = (jnp.concatenate([(dpi[:, ::-1] + dp0_im).transpose(0, 2, 1), zero], axis=-1)
              + jnp.concatenate([zero, dp1_im.transpose(0, 2, 1)], axis=-1)).at[:, :, CHUNK].add(da64[:, n:])
    return dbb_re, dbb_im, dc_re, dc_im, dpw_re, dpw_im


def _s5_scan_powers(a_re, a_im, log_dt, nsteps):
    dt = jnp.exp(log_dt)[:, None]
    steps = (CHUNK * (2.0 ** jnp.arange(8, dtype=F32)))[None, :, None]
    mag = jnp.exp((a_re * dt)[:, None, :] * steps)
    ang = (a_im * dt)[:, None, :] * steps
    re, im = mag * jnp.cos(ang), mag * jnp.sin(ang)
    del nsteps
    return jnp.concatenate([re, re], -1), jnp.concatenate([-im, im], -1)


def _build_toeplitz(kmat, tg_ref):
    lane = lax.broadcasted_iota(jnp.int32, (CHUNK, CHUNK * S5_CH), 1)
    srow = lax.broadcasted_iota(jnp.int32, (CHUNK, CHUNK * S5_CH), 0)
    keep = lane >= S5_CH * srow
    for h in range(S5_CH):
        row = jnp.broadcast_to(kmat[h:h + 1, :], (CHUNK, CHUNK * S5_CH))
        rolled = pltpu.roll(row, 0, 1, stride=S5_CH, stride_axis=0)
        tg_ref[h * CHUNK:(h + 1) * CHUNK, :] = jnp.where(keep, rolled, 0.0).astype(BF)


def _swap_halves(x):
    return pltpu.roll(x, S5_STATE, 1)


def _hi_lo(x):
    hi = x.astype(BF)
    return hi, (x - hi.astype(F32)).astype(BF)


def _dot3(dot, a, b):
    a_hi, a_lo = _hi_lo(a)
    b_hi, b_lo = _hi_lo(b)
    return dot(a_hi, b_hi) + dot(a_hi, b_lo) + dot(a_lo, b_hi)


def _build_state_tables(bbp_ref, pwr_ref, cnp_ref, pwf_ref, wst_ref, wofft_ref, w0_ref):
    lane = lax.broadcasted_iota(jnp.int32, (1, 2 * S5_STATE), 1)
    left = lane < S5_STATE
    pwr = pwr_ref[0]
    pwr_sw = _swap_halves(pwr)
    for hh in range(S5_CH):
        bb = bbp_ref[0, hh:hh + 1, :]
        bb_sw = _swap_halves(bb)
        wst_ref[hh * CHUNK:(hh + 1) * CHUNK, :] = (jnp.where(left, bb, bb_sw) * pwr
                                                   + jnp.where(left, -bb_sw, bb) * pwr_sw).astype(BF)
    cn = cnp_ref[0]
    cn_sw = _swap_halves(cn)
    c_a = jnp.where(left, cn, -cn_sw)
    c_b = jnp.where(left, -cn_sw, -cn)
    p_prev = pwf_ref[0, 0:1, :]
    for t in range(CHUNK):
        p = pwf_ref[0, t + 1:t + 2, :]
        w0_ref[t * S5_CH:(t + 1) * S5_CH, :] = c_a * p_prev + c_b * _swap_halves(p_prev)
        wofft_ref[t * S5_CH:(t + 1) * S5_CH, :] = (c_a * p + c_b * _swap_halves(p)).astype(BF)
        p_prev = p
    return _dot3(_dot_nt, bbp_ref[0], w0_ref[...])


def _reduce_table_cotangent(d_table, cnp_ref, pwf_ref, first_power):
    cn = cnp_ref[0]
    cn_sw = _swap_halves(cn)
    xa = jnp.zeros((S5_CH, 2 * S5_STATE), F32)
    ya = jnp.zeros((S5_CH, 2 * S5_STATE), F32)
    ru_rows, rv_rows = [], []
    for t in range(CHUNK):
        blk = d_table[t * S5_CH:(t + 1) * S5_CH, :]
        p = pwf_ref[0, t + first_power:t + first_power + 1, :]
        xa = xa + blk * p
        ya = ya + blk * _swap_halves(p)
        ru_rows.append(jnp.sum(blk * cn, axis=0, keepdims=True))
        rv_rows.append(jnp.sum(blk * cn_sw, axis=0, keepdims=True))
    return jnp.concatenate([xa, ya], axis=0), jnp.concatenate(ru_rows + rv_rows, axis=0)


def _s5_core_fwd(uflat, bbp, pwr, cnp, pwf, p1, p2):
    G, nc, W = uflat.shape
    nsteps = max(1, (nc - 1).bit_length())

    def body(u_ref, bbp_ref, pwr_ref, cnp_ref, pwf_ref, p1_ref, p2_ref, y_ref, h_ref, tg_ref, wst_ref, wofft_ref, w0_ref):
        _build_toeplitz(_build_state_tables(bbp_ref, pwr_ref, cnp_ref, pwf_ref, wst_ref, wofft_ref, w0_ref), tg_ref)
        u = u_ref[0]
        x = _dot(u, wst_ref[...])
        row = lax.broadcasted_iota(jnp.int32, x.shape, 0)
        d = 1
        for k in range(nsteps):
            sh = jnp.where(row >= d, pltpu.roll(x, d, 0), 0.0)
            x = x + p1_ref[0, k:k + 1, :] * sh + p2_ref[0, k:k + 1, :] * _swap_halves(sh)
            d *= 2
        h = jnp.where(row >= 1, pltpu.roll(x, 1, 0), 0.0)
        h_ref[0] = h
        y = _dot(u, tg_ref[...]) + _dot_nt(h.astype(BF), wofft_ref[...])
        y_ref[0] = y.astype(BF)

    spec_g = lambda a, b: pl.BlockSpec((1, a, b), lambda g: (g, 0, 0))
    return pl.pallas_call(
        body, name="s5_core_fwd", grid=(G,),
        in_specs=[spec_g(nc, W), spec_g(S5_CH, 2 * S5_STATE), spec_g(CHUNK, 2 * S5_STATE),
                  spec_g(S5_CH, 2 * S5_STATE), spec_g(CHUNK + 8, 2 * S5_STATE), spec_g(8, 2 * S5_STATE), spec_g(8, 2 * S5_STATE)],
        out_specs=[spec_g(nc, W), spec_g(nc, 2 * S5_STATE)],
        out_shape=[jax.ShapeDtypeStruct((G, nc, W), BF), jax.ShapeDtypeStruct((G, nc, 2 * S5_STATE), F32)],
        scratch_shapes=[pltpu.VMEM((W, W), BF), pltpu.VMEM((W, 2 * S5_STATE), BF), pltpu.VMEM((W, 2 * S5_STATE), BF),
                        pltpu.VMEM((W, 2 * S5_STATE), F32)],
        compiler_params=_params(("arbitrary",)),
    )(uflat, bbp, pwr, cnp, pwf, p1, p2)


def _s5_core_bwd(uflat, dyflat, hsave, bbp, pwr, cnp, pwf, p1, p2):
    G, nc, W = uflat.shape
    nsteps = max(1, (nc - 1).bit_length())

    def body(u_ref, dy_ref, h_ref, bbp_ref, pwr_ref, cnp_ref, pwf_ref, p1_ref, p2_ref,
             du_ref, dbk_ref, r12_ref, q12_ref, xy_ref, uv_ref, xy0_ref, uv0_ref, da_ref,
             tg_ref, flip_ref, wst_ref, wofft_ref, w0_ref):
        @pl.when(pl.program_id(0) == 0)
        def _():
            r = lax.broadcasted_iota(jnp.int32, (W, W), 0)
            c = lax.broadcasted_iota(jnp.int32, (W, W), 1)
            flip_ref[...] = (((r >> 6) == (c >> 6)) & ((r & (CHUNK - 1)) + (c & (CHUNK - 1)) == CHUNK - 1)).astype(BF)

        _build_toeplitz(_build_state_tables(bbp_ref, pwr_ref, cnp_ref, pwf_ref, wst_ref, wofft_ref, w0_ref), tg_ref)
        u = u_ref[0]
        dy = dy_ref[0]
        h = h_ref[0]
        gh = _dot(dy, wofft_ref[...])
        row = lax.broadcasted_iota(jnp.int32, gh.shape, 0)
        x = jnp.where(row < nc - 1, pltpu.roll(gh, nc - 1, 0), 0.0)
        d = 1
        for k in range(nsteps):
            sh = jnp.where(row < nc - d, pltpu.roll(x, nc - d, 0), 0.0)
            x = x + p1_ref[0, k:k + 1, :] * sh - p2_ref[0, k:k + 1, :] * _swap_halves(sh)
            d *= 2
        gs = x.astype(BF)
        du_ref[0] = (_dot_nt(dy, tg_ref[...]) + _dot_nt(gs, wst_ref[...])).astype(BF)

        dwst = _dot_tn(u, gs)
        pwr = pwr_ref[0]
        pwr_sw = _swap_halves(pwr)
        q1 = jnp.zeros((CHUNK, 2 * S5_STATE), F32)
        q2 = jnp.zeros((CHUNK, 2 * S5_STATE), F32)
        r1_rows, r2_rows = [], []
        for hh in range(S5_CH):
            blk = dwst[hh * CHUNK:(hh + 1) * CHUNK, :]
            r1_rows.append(jnp.sum(blk * pwr, axis=0, keepdims=True))
            r2_rows.append(jnp.sum(blk * pwr_sw, axis=0, keepdims=True))
            bb = bbp_ref[0, hh:hh + 1, :]
            q1 = q1 + blk * bb
            q2 = q2 + blk * _swap_halves(bb)
        r12_ref[0] = jnp.concatenate(r1_rows + r2_rows, axis=0)
        q12_ref[0] = jnp.concatenate([q1, q2], axis=0)

        xy_ref[0], uv_ref[0] = _reduce_table_cotangent(_dot_tn(dy, h.astype(BF)), cnp_ref, pwf_ref, 1)
        r1 = jnp.sum(x * h, axis=0, keepdims=True)
        r2 = jnp.sum(x * _swap_halves(h), axis=0, keepdims=True)
        da_ref[0] = jnp.concatenate([r1, r2, jnp.zeros((6, 2 * S5_STATE), F32)], axis=0)
        lane = lax.broadcasted_iota(jnp.int32, (CHUNK, W), 1)
        srow = lax.broadcasted_iota(jnp.int32, (CHUNK, W), 0)
        keep = lane < S5_CH * (srow + 1)
        ur = _dot(u, flip_ref[...]).astype(BF)
        dk_rows = []
        for hh in range(S5_CH):
            dt_h = _dot_tn(ur[:, hh * CHUNK:(hh + 1) * CHUNK], dy)
            back = pltpu.roll(dt_h, S5_CH, 1, stride=S5_CH, stride_axis=0)
            dk_rows.append(jnp.sum(jnp.where(keep, back, 0.0), axis=0, keepdims=True))
        dk = jnp.concatenate(dk_rows, axis=0)
        dbk_ref[0] = _dot3(_dot, dk, w0_ref[...])
        xy0_ref[0], uv0_ref[0] = _reduce_table_cotangent(_dot3(_dot_tn, dk, bbp_ref[0]), cnp_ref, pwf_ref, 0)

    spec_g = lambda a, b: pl.BlockSpec((1, a, b), lambda g: (g, 0, 0))
    small_c = jax.ShapeDtypeStruct((G, 2 * S5_CH, 2 * S5_STATE), F32)
    small_p = jax.ShapeDtypeStruct((G, 2 * CHUNK, 2 * S5_STATE), F32)
    return pl.pallas_call(
        body, name="s5_core_bwd", grid=(G,),
        in_specs=[spec_g(nc, W), spec_g(nc, W), spec_g(nc, 2 * S5_STATE), spec_g(S5_CH, 2 * S5_STATE),
                  spec_g(CHUNK, 2 * S5_STATE), spec_g(S5_CH, 2 * S5_STATE), spec_g(CHUNK + 8, 2 * S5_STATE),
                  spec_g(8, 2 * S5_STATE), spec_g(8, 2 * S5_STATE)],
        out_specs=[spec_g(nc, W), spec_g(S5_CH, 2 * S5_STATE), spec_g(2 * S5_CH, 2 * S5_STATE), spec_g(2 * CHUNK, 2 * S5_STATE),
                   spec_g(2 * S5_CH, 2 * S5_STATE), spec_g(2 * CHUNK, 2 * S5_STATE),
                   spec_g(2 * S5_CH, 2 * S5_STATE), spec_g(2 * CHUNK, 2 * S5_STATE), spec_g(8, 2 * S5_STATE)],
        out_shape=[jax.ShapeDtypeStruct((G, nc, W), BF), jax.ShapeDtypeStruct((G, S5_CH, 2 * S5_STATE), F32),
                   small_c, small_p, small_c, small_p, small_c, small_p, jax.ShapeDtypeStruct((G, 8, 2 * S5_STATE), F32)],
        scratch_shapes=[pltpu.VMEM((W, W), BF), pltpu.VMEM((W, W), BF), pltpu.VMEM((W, 2 * S5_STATE), BF),
                        pltpu.VMEM((W, 2 * S5_STATE), BF), pltpu.VMEM((W, 2 * S5_STATE), F32)],
        compiler_params=_params(("arbitrary",)),
    )(uflat, dyflat, hsave, bbp, pwr, cnp, pwf, p1, p2)


def _flat_hs(a, nc):
    return a.reshape(nc, CHUNK, S5_GROUPS, S5_CH).transpose(2, 0, 3, 1).reshape(S5_GROUPS, nc, CHUNK * S5_CH)


def _unflat_hs(a, nc):
    return a.reshape(S5_GROUPS, nc, S5_CH, CHUNK).transpose(1, 3, 0, 2).reshape(nc * CHUNK, D_MODEL)


def _flat_tk(a, nc):
    return a.reshape(nc, CHUNK, S5_GROUPS, S5_CH).transpose(2, 0, 1, 3).reshape(S5_GROUPS, nc, CHUNK * S5_CH)


def _unflat_tk(a, nc):
    return a.reshape(S5_GROUPS, nc, CHUNK, S5_CH).transpose(1, 2, 0, 3).reshape(nc * CHUNK, D_MODEL)


def _s5_post_fwd(yssm, u, zs, s5_d, w_glu, b_glu):
    L = yssm.shape[0]
    tl = min(TOKEN_TILE, L)

    def body(ys_ref, u_ref, z_ref, d_ref, wg_ref, bg_ref, o_ref):
        u = u_ref[...].astype(F32)
        a = _gelu(ys_ref[...].astype(F32) + d_ref[...] * u)
        y = a * _sigmoid(_dot(a.astype(BF), wg_ref[...]) + bg_ref[...])
        z = z_ref[...].astype(F32)
        o_ref[...] = (y * z * _sigmoid(z)).astype(BF)

    return pl.pallas_call(
        body, name="s5_post_fwd", grid=(L // tl,),
        in_specs=[_row_spec(tl, D_MODEL), _row_spec(tl, D_MODEL), _row_spec(tl, D_MODEL), _const_spec((1, D_MODEL)),
                  _const_spec((D_MODEL, D_MODEL)), _const_spec((1, D_MODEL))],
        out_specs=_row_spec(tl, D_MODEL),
        out_shape=jax.ShapeDtypeStruct((L, D_MODEL), BF),
        compiler_params=_params(("arbitrary",)),
    )(yssm, u, zs, s5_d, w_glu, b_glu)


def _s5_post_bwd(dys5, yssm, u, zs, s5_d, w_glu, b_glu):
    L = yssm.shape[0]
    tl = min(TOKEN_TILE, L)

    def body(dy_ref, ys_ref, u_ref, z_ref, d_ref, wg_ref, bg_ref, dz_ref, dys_ref, a_ref, dgl_ref, dbg_ref, dd_ref):
        @pl.when(pl.program_id(0) == 0)
        def _():
            dbg_ref[...] = jnp.zeros_like(dbg_ref)
            dd_ref[...] = jnp.zeros_like(dd_ref)

        u = u_ref[...].astype(F32)
        y0 = ys_ref[...].astype(F32) + d_ref[...] * u
        a = _gelu(y0)
        a_bf = a.astype(BF)
        sg = _sigmoid(_dot(a_bf, wg_ref[...]) + bg_ref[...])
        y = a * sg
        z = z_ref[...].astype(F32)
        sz = _sigmoid(z)
        dout = dy_ref[...].astype(F32)
        dz_ref[...] = (dout * y * sz * (1.0 + z * (1.0 - sz))).astype(BF)
        dyv = dout * z * sz
        dgl = dyv * a * sg * (1.0 - sg)
        dgl_bf = dgl.astype(BF)
        da = dyv * sg + _dot_nt(dgl_bf, wg_ref[...])
        dy0 = da * _gelu_grad(y0)
        dbg_ref[...] += jnp.sum(dgl, axis=0, keepdims=True)
        dd_ref[...] += jnp.sum(dy0 * u, axis=0, keepdims=True)
        dys_ref[...] = dy0.astype(BF)
        a_ref[...] = a_bf
        dgl_ref[...] = dgl_bf

    big = jax.ShapeDtypeStruct((L, D_MODEL), BF)
    vec = jax.ShapeDtypeStruct((1, D_MODEL), F32)
    return pl.pallas_call(
        body, name="s5_post_bwd", grid=(L // tl,),
        in_specs=[_row_spec(tl, D_MODEL), _row_spec(tl, D_MODEL), _row_spec(tl, D_MODEL), _row_spec(tl, D_MODEL),
                  _const_spec((1, D_MODEL)), _const_spec((D_MODEL, D_MODEL)), _const_spec((1, D_MODEL))],
        out_specs=[_row_spec(tl, D_MODEL)] * 4 + [_const_spec((1, D_MODEL))] * 2,
        out_shape=[big, big, big, big, vec, vec],
        compiler_params=_params(("arbitrary",)),
    )(dys5, yssm, u, zs, s5_d, w_glu, b_glu)


def _cumsum_rows(a):
    row = lax.broadcasted_iota(jnp.int32, a.shape, 0)
    d = 1
    while d < a.shape[0]:
        a = a + jnp.where(row >= d, pltpu.roll(a, d, 0), 0.0)
        d *= 2
    return a


def _rev_cumsum_rows(a):
    n = a.shape[0]
    row = lax.broadcasted_iota(jnp.int32, a.shape, 0)
    d = 1
    while d < n:
        a = a + jnp.where(row < n - d, pltpu.roll(a, n - d, 0), 0.0)
        d *= 2
    return a


def _ssd_fill_padded(first, xs_ref, bc_ref, hx_ref, hb_ref, xp_ref, tl):
    hal = jnp.concatenate([hx_ref[...], hb_ref[...]], axis=1)
    xp_ref[0:8, :] = jnp.where(first, 0.0, hal)
    xp_ref[8:8 + tl, 0:1024] = xs_ref[...]
    xp_ref[8:8 + tl, 1024:2048] = bc_ref[...]


def _ssd_conv_fwd(first, xs_ref, bc_ref, hx_ref, hb_ref, cw_ref, cb_ref, xp_ref, tl):
    _ssd_fill_padded(first, xs_ref, bc_ref, hx_ref, hb_ref, xp_ref, tl)
    pre = cb_ref[...] + cw_ref[0:1, :] * xp_ref[5:5 + tl, :]
    for k in range(1, 4):
        pre = pre + cw_ref[k:k + 1, :] * xp_ref[5 + k:5 + k + tl, :]
    return pre


def _onehot_lane(h):
    return (lax.broadcasted_iota(jnp.int32, (1, LANES), 1) == h).astype(F32)


def _dot_exact(x, e):
    hi = x.astype(BF)
    r = x - hi.astype(F32)
    mid = r.astype(BF)
    lo = (r - mid.astype(F32)).astype(BF)
    return _dot(hi, e) + _dot(mid, e) + _dot(lo, e)


def _head_expand_matrices():
    e = lax.broadcasted_iota(jnp.int32, (LANES, D_MODEL), 0) == (lax.broadcasted_iota(jnp.int32, (LANES, D_MODEL), 1) >> 6)
    et = (lax.broadcasted_iota(jnp.int32, (D_MODEL, LANES), 0) >> 6) == lax.broadcasted_iota(jnp.int32, (D_MODEL, LANES), 1)
    return e.astype(BF), et.astype(BF)


def _group_masks():
    r64 = lax.broadcasted_iota(jnp.int32, (4 * CHUNK, CHUNK), 0)
    causal4 = (r64 & (CHUNK - 1)) >= lax.broadcasted_iota(jnp.int32, (4 * CHUNK, CHUNK), 1)
    r256 = lax.broadcasted_iota(jnp.int32, (4 * CHUNK, 4 * SSD_HEAD_DIM), 0)
    same = (r256 >> 6) == (lax.broadcasted_iota(jnp.int32, (4 * CHUNK, 4 * SSD_HEAD_DIM), 1) >> 6)
    return causal4, same


def _group_decay(acs, acs_t, j, causal4):
    col = jnp.concatenate([acs[:, 4 * j + hh:4 * j + hh + 1] for hh in range(4)], axis=0)
    rowv = jnp.concatenate([jnp.broadcast_to(acs_t[4 * j + hh:4 * j + hh + 1, :], (CHUNK, CHUNK)) for hh in range(4)], axis=0)
    return jnp.where(causal4, jnp.exp(col - rowv), 0.0)


def _group_last_decay(acs_t, j):
    return jnp.concatenate([jnp.broadcast_to(jnp.exp(acs_t[4 * j + hh:4 * j + hh + 1, CHUNK - 1:CHUNK]), (SSD_HEAD_DIM, 1))
                            for hh in range(4)], axis=0)


def _fold_heads(r):
    return r[0:CHUNK] + r[CHUNK:2 * CHUNK] + r[2 * CHUNK:3 * CHUNK] + r[3 * CHUNK:4 * CHUNK]


def _ssd_specs_in(tl, nt, rev):
    t_of = (lambda i: nt - 1 - i) if rev else (lambda i: i)
    rows = lambda w, col: pl.BlockSpec((tl, w), lambda i: (t_of(i), col))
    halo = lambda col: pl.BlockSpec((8, 1024), lambda i: (jnp.maximum(t_of(i) * (tl // 8) - 1, 0), col))
    return t_of, rows, halo


def _ssd_fwd(proj, pdt, conv_w, conv_b, dt_bias, a_log, ssd_d, norm_w):
    L = proj.shape[0]
    tl = min(TOKEN_TILE, L)
    nt, ncl = L // tl, tl // CHUNK
    _, rows, halo = _ssd_specs_in(tl, nt, False)

    def body(xs_ref, bc_ref, hx_ref, hb_ref, dt_ref, z_ref, cw_ref, cb_ref, dtb_ref, al_ref, dd_ref, nw_ref,
             y_ref, ypre_ref, st_ref, pre_ref, xp_ref, xbc_ref, dts_ref, hst_ref):
        i = pl.program_id(0)

        @pl.when(i == 0)
        def _():
            hst_ref[...] = jnp.zeros_like(hst_ref)

        pre = _ssd_conv_fwd(i == 0, xs_ref, bc_ref, hx_ref, hb_ref, cw_ref, cb_ref, xp_ref, tl)
        pre_ref[...] = pre
        xbc_ref[...] = pre * _sigmoid(pre)
        dts_ref[...] = _softplus(dt_ref[...] + dtb_ref[...])
        a_neg = -jnp.exp(al_ref[...])
        e16, _ = _head_expand_matrices()
        causal4, same = _group_masks()
        dd_x = _dot_exact(jnp.broadcast_to(dd_ref[...], (8, LANES)), e16)[0:1, :]

        def chunk(c, carry):
            r0 = pl.multiple_of(c * CHUNK, CHUNK)
            xbc = xbc_ref[pl.ds(r0, CHUNK), :]
            dtc = dts_ref[pl.ds(r0, CHUNK), :]
            acs = _cumsum_rows(dtc * a_neg)
            acs_t = acs.T
            acs_x = _dot_exact(acs, e16)
            xs = xbc[:, 0:1024]
            xd = xs * _dot_exact(dtc, e16)
            xd_bf = xd.astype(BF)
            xdd = (xd * jnp.exp(acs_x[CHUNK - 1:CHUNK, :] - acs_x)).astype(BF)
            e_x = jnp.exp(acs_x)
            for j in range(SSD_GROUPS):
                sl = slice(256 * j, 256 * (j + 1))
                bj = xbc[:, 1024 + 128 * j:1024 + 128 * (j + 1)].astype(BF)
                cj = xbc[:, 1536 + 128 * j:1536 + 128 * (j + 1)].astype(BF)
                g = _dot_nt(cj, bj)
                hj = hst_ref[sl, :]
                zj = _dot_nt(cj, hj.astype(BF))
                sc = (jnp.concatenate([g] * 4, axis=0) * _group_decay(acs, acs_t, j, causal4)).astype(BF)
                yd = _fold_heads(jnp.where(same, _dot(sc, xd_bf[:, sl]), 0.0))
                ypre_ref[pl.ds(r0, CHUNK), sl] = yd + e_x[:, sl] * zj + dd_x[:, sl] * xs[:, sl]
                st_ref[c, sl, :] = hj
                hst_ref[sl, :] = _group_last_decay(acs_t, j) * hj + _dot_tn(xdd[:, sl], bj)
            return carry

        lax.fori_loop(0, ncl, chunk, 0, unroll=True)
        z = z_ref[...]
        gg = ypre_ref[...] * z * _sigmoid(z)
        for j in range(SSD_GROUPS):
            seg = gg[:, 256 * j:256 * (j + 1)]
            r = lax.rsqrt(jnp.mean(seg * seg, axis=-1, keepdims=True) + EPS)
            y_ref[:, 256 * j:256 * (j + 1)] = (seg * r * nw_ref[:, 256 * j:256 * (j + 1)]).astype(BF)

    nc = L // CHUNK
    return pl.pallas_call(
        body, name="ssd_fwd", grid=(nt,),
        in_specs=[rows(1024, 1), rows(1024, 2), halo(1), halo(2), rows(LANES, 0), rows(1024, 0),
                  _const_spec((4, 2048)), _const_spec((1, 2048)), _const_spec((1, LANES)), _const_spec((1, LANES)),
                  _const_spec((1, LANES)), _const_spec((1, D_MODEL))],
        out_specs=[_row_spec(tl, D_MODEL), _row_spec(tl, D_MODEL), pl.BlockSpec((ncl, 1024, SSD_STATE), lambda i: (i, 0, 0)),
                   _row_spec(tl, 2048)],
        out_shape=[jax.ShapeDtypeStruct((L, D_MODEL), BF), jax.ShapeDtypeStruct((L, D_MODEL), F32),
                   jax.ShapeDtypeStruct((nc, 1024, SSD_STATE), F32), jax.ShapeDtypeStruct((L, 2048), F32)],
        scratch_shapes=[pltpu.VMEM((tl + 8, 2048), F32), pltpu.VMEM((tl, 2048), F32), pltpu.VMEM((tl, LANES), F32),
                        pltpu.VMEM((1024, SSD_STATE), F32)],
        compiler_params=_params(("arbitrary",)),
    )(proj, proj, proj, proj, pdt, proj, conv_w, conv_b, dt_bias, a_log, ssd_d, norm_w)


def _ssd_bwd(dyssd, ypre, proj, pdt, states, pre_act, conv_w, dt_bias, a_log, ssd_d, norm_w):
    L = proj.shape[0]
    tl = min(TOKEN_TILE, L)
    nt, ncl = L // tl, tl // CHUNK
    t_of, rows, halo = _ssd_specs_in(tl, nt, True)

    def body(dy_ref, ypre_ref, z_ref, xs_ref, bc_ref, hx_ref, hb_ref, dt_ref, st_ref, pre_ref, cw_ref, dtb_ref, al_ref,
             dd_ref, nw_ref,
             dxbc_ref, ddt_ref, dz_ref, dcw_ref, dcb_ref, ddtb_ref, dal_ref, ddd_ref, dnw_ref,
             xp_ref, xbc_ref, dts_ref, dyp_ref, dxs_ref, ddts_ref, dp_ref, dh_ref):
        i = pl.program_id(0)

        @pl.when(i == 0)
        def _():
            for r in (dcw_ref, dcb_ref, ddtb_ref, dal_ref, ddd_ref, dnw_ref, dh_ref):
                r[...] = jnp.zeros_like(r)
            dp_ref[tl:tl + 8, :] = jnp.zeros((8, 2048), F32)

        _ssd_fill_padded(t_of(i) == 0, xs_ref, bc_ref, hx_ref, hb_ref, xp_ref, tl)
        pre = pre_ref[...]
        xbc_ref[...] = pre * _sigmoid(pre)
        dts_ref[...] = _softplus(dt_ref[...] + dtb_ref[...])
        a_neg = -jnp.exp(al_ref[...])

        ypre = ypre_ref[...]
        z = z_ref[...]
        sz = _sigmoid(z)
        gg = ypre * z * sz
        dout = dy_ref[...]
        for j in range(SSD_GROUPS):
            sl = slice(256 * j, 256 * (j + 1))
            seg = gg[:, sl]
            r = lax.rsqrt(jnp.mean(seg * seg, axis=-1, keepdims=True) + EPS)
            gh = seg * r
            dnw_ref[:, sl] += jnp.sum(dout[:, sl] * gh, axis=0, keepdims=True)
            gw = dout[:, sl] * nw_ref[:, sl]
            dgg = r * (gw - gh * jnp.mean(gw * gh, axis=-1, keepdims=True))
            dyp_ref[:, sl] = dgg * z[:, sl] * sz[:, sl]
            dz_ref[:, sl] = (dgg * ypre[:, sl] * sz[:, sl] * (1.0 + z[:, sl] * (1.0 - sz[:, sl]))).astype(BF)

        e16, e16t = _head_expand_matrices()
        causal4, same = _group_masks()
        dd_x = _dot_exact(jnp.broadcast_to(dd_ref[...], (8, LANES)), e16)[0:1, :]
        last_row = (lax.broadcasted_iota(jnp.int32, (CHUNK, 1), 0) == CHUNK - 1).astype(F32)
        sel_rows = lax.broadcasted_iota(jnp.int32, (4 * CHUNK, LANES), 0) >> 6
        sel_lanes = lax.broadcasted_iota(jnp.int32, (4 * CHUNK, LANES), 1)

        def chunk(k, carry):
            dal_acc, ddx_acc = carry
            c = ncl - 1 - k
            r0 = pl.multiple_of(c * CHUNK, CHUNK)
            xbc = xbc_ref[pl.ds(r0, CHUNK), :]
            dtc = dts_ref[pl.ds(r0, CHUNK), :]
            dyp = dyp_ref[pl.ds(r0, CHUNK), :]
            acs = _cumsum_rows(dtc * a_neg)
            acs_t = acs.T
            acs_x = _dot_exact(acs, e16)
            dt_x = _dot_exact(dtc, e16)
            xs = xbc[:, 0:1024]
            xd = xs * dt_x
            xd_bf = xd.astype(BF)
            dec_x = jnp.exp(acs_x[CHUNK - 1:CHUNK, :] - acs_x)
            xdd = xd * dec_x
            xdd_bf = xdd.astype(BF)
            dz = dyp * jnp.exp(acs_x)
            dz_bf = dz.astype(BF)
            ddx_acc = ddx_acc + jnp.sum(dyp * xs, axis=0, keepdims=True)
            dacs = jnp.zeros((CHUNK, LANES), F32)
            hsum = jnp.zeros((1, LANES), F32)
            p1_l, p2_l, p3_l = [], [], []
            for j in range(SSD_GROUPS):
                sl = slice(256 * j, 256 * (j + 1))
                bj = xbc[:, 1024 + 128 * j:1024 + 128 * (j + 1)].astype(BF)
                cj = xbc[:, 1536 + 128 * j:1536 + 128 * (j + 1)].astype(BF)
                g = _dot_nt(cj, bj)
                hj = st_ref[c, sl, :]
                hj_bf = hj.astype(BF)
                dhj = dh_ref[sl, :]
                dhj_bf = dhj.astype(BF)
                zj = _dot_nt(cj, hj_bf)
                qj = _dot_nt(bj, dhj_bf)
                lm = _group_decay(acs, acs_t, j, causal4)
                sc = jnp.concatenate([g] * 4, axis=0) * lm
                sc_bf = sc.astype(BF)
                dym = jnp.where(same, jnp.concatenate([dyp[:, sl]] * 4, axis=0), 0.0).astype(BF)
                dsc = _dot_nt(dym, xd_bf[:, sl])
                dxd = _dot_tn(sc_bf, dym) + qj * dec_x[:, sl]
                m = dsc * sc
                dg_bf = _fold_heads(dsc * lm).astype(BF)
                rs = jnp.sum(m, axis=1, keepdims=True)
                e2 = dhj * hj
                for hh in range(4):
                    oh = _onehot_lane(4 * j + hh)
                    dacs = dacs + oh * rs[CHUNK * hh:CHUNK * (hh + 1)]
                    hsum = hsum + oh * jnp.sum(jnp.sum(e2[64 * hh:64 * (hh + 1)], axis=0, keepdims=True), axis=1, keepdims=True)
                sel = (sel_rows + 4 * j == sel_lanes).astype(BF)
                hi = m.astype(BF)
                rem = m - hi.astype(F32)
                mid = rem.astype(BF)
                lo = (rem - mid.astype(F32)).astype(BF)
                dacs = dacs - (_dot_tn(hi, sel) + _dot_tn(mid, sel) + _dot_tn(lo, sel))
                p1_l.append(dz[:, sl] * zj)
                p2_l.append(qj * xdd[:, sl])
                p3_l.append(dxd * xs[:, sl])
                dxs_ref[pl.ds(r0, CHUNK), sl] = dd_x[:, sl] * dyp[:, sl] + dxd * dt_x[:, sl]
                dxs_ref[pl.ds(r0, CHUNK), 1536 + 128 * j:1536 + 128 * (j + 1)] = _dot(dg_bf, bj) + _dot(dz_bf[:, sl], hj_bf)
                dxs_ref[pl.ds(r0, CHUNK), 1024 + 128 * j:1024 + 128 * (j + 1)] = _dot_tn(dg_bf, cj) + _dot(xdd_bf[:, sl], dhj_bf)
                dh_ref[sl, :] = _group_last_decay(acs_t, j) * dhj + _dot_tn(dz_bf[:, sl], cj)
            stacked = jnp.concatenate([jnp.concatenate(p1_l, axis=1), jnp.concatenate(p2_l, axis=1), jnp.concatenate(p3_l, axis=1)], axis=0)
            red = _dot_exact(stacked, e16t)
            r1, r2, ddtc = red[0:CHUNK], red[CHUNK:2 * CHUNK], red[2 * CHUNK:3 * CHUNK]
            tot = jnp.sum(r2, axis=0, keepdims=True) + jnp.exp(acs[CHUNK - 1:CHUNK, :]) * hsum
            da = _rev_cumsum_rows(dacs + r1 - r2 + last_row * tot)
            ddts_ref[pl.ds(r0, CHUNK), :] = ddtc + da * a_neg
            dal_acc = dal_acc + jnp.sum(da * dtc, axis=0, keepdims=True)
            return dal_acc, ddx_acc

        carry = (jnp.zeros((1, LANES), F32), jnp.zeros((1, D_MODEL), F32))
        for k in range(ncl):
            carry = chunk(k, carry)
        dal_acc, ddx_acc = carry
        dal_ref[...] += dal_acc * a_neg
        ddd_ref[...] += _dot_exact(jnp.broadcast_to(ddx_acc, (8, D_MODEL)), e16t)[0:1, :]
        ddt_raw = ddts_ref[...] * _sigmoid(dt_ref[...] + dtb_ref[...])
        ddt_ref[...] = ddt_raw
        ddtb_ref[...] += jnp.sum(ddt_raw, axis=0, keepdims=True)

        pre = pre_ref[...]
        sp = _sigmoid(pre)
        dpre = dxs_ref[...] * sp * (1.0 + pre * (1.0 - sp))
        dp_ref[0:tl, :] = dpre
        dcb_ref[...] += jnp.sum(dpre, axis=0, keepdims=True)
        dx = jnp.zeros((tl, 2048), F32)
        for k in range(4):
            dcw_ref[k:k + 1, :] += jnp.sum(dpre * xp_ref[5 + k:5 + k + tl, :], axis=0, keepdims=True)
            dx = dx + cw_ref[k:k + 1, :] * dp_ref[3 - k:3 - k + tl, :]
        dxbc_ref[...] = dx.astype(BF)
        dp_ref[tl:tl + 8, :] = dp_ref[0:8, :]

    vec = lambda w: jax.ShapeDtypeStruct((1, w), F32)
    rrow = lambda w: pl.BlockSpec((tl, w), lambda i: (t_of(i), 0))
    return pl.pallas_call(
        body, name="ssd_bwd", grid=(nt,),
        in_specs=[rrow(D_MODEL), rrow(D_MODEL), rows(1024, 0), rows(1024, 1), rows(1024, 2), halo(1), halo(2), rows(LANES, 0),
                  pl.BlockSpec((ncl, 1024, SSD_STATE), lambda i: (t_of(i), 0, 0)), rrow(2048),
                  _const_spec((4, 2048)), _const_spec((1, LANES)), _const_spec((1, LANES)),
                  _const_spec((1, LANES)), _const_spec((1, D_MODEL))],
        out_specs=[rrow(2048), rrow(LANES), rrow(D_MODEL), _const_spec((8, 2048)), _const_spec((1, 2048)),
                   _const_spec((1, LANES)), _const_spec((1, LANES)), _const_spec((1, LANES)), _const_spec((1, D_MODEL))],
        out_shape=[jax.ShapeDtypeStruct((L, 2048), BF), jax.ShapeDtypeStruct((L, LANES), F32), jax.ShapeDtypeStruct((L, D_MODEL), BF),
                   jax.ShapeDtypeStruct((8, 2048), F32), vec(2048), vec(LANES), vec(LANES), vec(LANES), vec(D_MODEL)],
        scratch_shapes=[pltpu.VMEM((tl + 8, 2048), F32), pltpu.VMEM((tl, 2048), F32),
                        pltpu.VMEM((tl, LANES), F32), pltpu.VMEM((tl, D_MODEL), F32), pltpu.VMEM((tl, 2048), F32),
                        pltpu.VMEM((tl, LANES), F32), pltpu.VMEM((tl + 8, 2048), F32), pltpu.VMEM((1024, SSD_STATE), F32)],
        compiler_params=_params(("arbitrary",)),
    )(dyssd, ypre, proj, proj, proj, proj, proj, pdt, states, pre_act, conv_w, dt_bias, a_log, ssd_d, norm_w)


def _head_fwd_bwd(x, ys5, yssd, p, target, w_out, w_gate, w_proj, ple_nw, fin_nw):
    L = x.shape[0]
    tl = min(TOKEN_TILE, L)
    inv_d = 1.0 / D_MODEL

    def body(x_ref, ys_ref, yd_ref, p_ref, t_ref, wo_ref, wg_ref, wp_ref, pnw_ref, fnw_ref,
             loss_ref, dys_ref, dyd_ref, dh1_ref, n2_ref, dgl_ref, dpp_ref, dpnw_ref, dfnw_ref):
        @pl.when(pl.program_id(0) == 0)
        def _():
            loss_ref[...] = jnp.zeros_like(loss_ref)
            dpnw_ref[...] = jnp.zeros_like(dpnw_ref)
            dfnw_ref[...] = jnp.zeros_like(dfnw_ref)

        h1 = x_ref[...] + _dot(ys_ref[...], wo_ref[0:1024, :]) + _dot(yd_ref[...], wo_ref[1024:2048, :])
        r1 = lax.rsqrt(jnp.mean(h1 * h1, axis=-1, keepdims=True) + EPS)
        hh1 = h1 * r1
        n2 = (hh1 * pnw_ref[...]).astype(BF)
        gate = _sigmoid(_dot(n2, wg_ref[...]))
        pp = _dot(p_ref[...].astype(BF), wp_ref[...])
        h2 = h1 + pp * gate
        r2 = lax.rsqrt(jnp.mean(h2 * h2, axis=-1, keepdims=True) + EPS)
        hh2 = h2 * r2
        err = hh2 * fnw_ref[...] - t_ref[...]
        loss_ref[...] += 0.5 * inv_d * jnp.sum(err * err)
        dyo = err * inv_d
        dfnw_ref[...] += jnp.sum(dyo * hh2, axis=0, keepdims=True)
        g2 = dyo * fnw_ref[...]
        dh2 = r2 * (g2 - hh2 * jnp.mean(g2 * hh2, axis=-1, keepdims=True))
        dpp_ref[...] = (dh2 * gate).astype(BF)
        dgl = (dh2 * pp * gate * (1.0 - gate)).astype(BF)
        dgl_ref[...] = dgl
        n2_ref[...] = n2
        dn2 = _dot_nt(dgl, wg_ref[...])
        dpnw_ref[...] += jnp.sum(dn2 * hh1, axis=0, keepdims=True)
        g1 = dn2 * pnw_ref[...]
        dh1 = dh2 + r1 * (g1 - hh1 * jnp.mean(g1 * hh1, axis=-1, keepdims=True))
        dh1_ref[...] = dh1
        dh1_bf = dh1.astype(BF)
        dys_ref[...] = _dot_nt(dh1_bf, wo_ref[0:1024, :]).astype(BF)
        dyd_ref[...] = _dot_nt(dh1_bf, wo_ref[1024:2048, :])

    big = jax.ShapeDtypeStruct((L, D_MODEL), BF)
    vec = jax.ShapeDtypeStruct((1, D_MODEL), F32)
    return pl.pallas_call(
        body, name="head_fwd_bwd", grid=(L // tl,),
        in_specs=[_row_spec(tl, D_MODEL), _row_spec(tl, D_MODEL), _row_spec(tl, D_MODEL), _row_spec(tl, 256), _row_spec(tl, D_MODEL),
                  _const_spec((2048, D_MODEL)), _const_spec((D_MODEL, D_MODEL)), _const_spec((256, D_MODEL)),
                  _const_spec((1, D_MODEL)), _const_spec((1, D_MODEL))],
        out_specs=[_const_spec((8, LANES)), _row_spec(tl, D_MODEL), _row_spec(tl, D_MODEL), _row_spec(tl, D_MODEL),
                   _row_spec(tl, D_MODEL), _row_spec(tl, D_MODEL), _row_spec(tl, D_MODEL), _const_spec((1, D_MODEL)), _const_spec((1, D_MODEL))],
        out_shape=[jax.ShapeDtypeStruct((8, LANES), F32), big, jax.ShapeDtypeStruct((L, D_MODEL), F32),
                   jax.ShapeDtypeStruct((L, D_MODEL), F32), big, big, big, vec, vec],
        compiler_params=_params(("arbitrary",)),
    )(x, ys5, yssd, p, target, w_out, w_gate, w_proj, ple_nw, fin_nw)


def _pad_lanes(v):
    return jnp.pad(v.reshape(1, -1), ((0, 0), (0, LANES - v.size)))


def _local_step(x, p, target, w):
    L = x.shape[0]
    nc = L // CHUNK
    nsteps = max(1, (nc - 1).bit_length())
    w_in_t = w["w_in"]
    w_main = w_in_t[:D_MAIN]
    w_dt = jnp.pad(w_in_t[D_MAIN:], ((0, LANES - SSD_HEADS), (0, 0)))
    norm_w = w["norm_w"].reshape(1, -1)
    s5_d = w["s5_D"].reshape(1, -1)
    b_glu = w["s5_b_glu"].reshape(1, -1)
    conv_b = w["conv_b"].reshape(1, -1)
    dtb, alog, ssd_d = _pad_lanes(w["dt_bias"]), _pad_lanes(w["A_log"]), _pad_lanes(w["ssd_D"])
    ssd_nw = w["ssd_norm_w"].reshape(1, -1)
    ple_nw = w["ple_norm_w"].reshape(1, -1)
    fin_nw = w["final_norm_w"].reshape(1, -1)

    s5_args = (w["s5_A_re"], w["s5_A_im"], w["s5_log_dt"], w["s5_B_re"], w["s5_B_im"], w["s5_C_re"], w["s5_C_im"])
    small, small_vjp = jax.vjp(_s5_discretise, *s5_args)
    bbp, pwr, cnp, pwf = _s5_table_factors(*small)
    p1, p2 = _s5_scan_powers(w["s5_A_re"], w["s5_A_im"], w["s5_log_dt"], nsteps)

    hn, u_s5, z_s5, pssd, pdt = _in_proj_fwd(x, norm_w, w_main, w_dt)
    uflat = _flat_hs(u_s5, nc)
    yflat, hsave = _s5_core_fwd(uflat, bbp, pwr, cnp, pwf, p1, p2)
    yssm = _unflat_tk(yflat, nc)
    ys5 = _s5_post_fwd(yssm, u_s5, z_s5, s5_d, w["s5_w_glu"], b_glu)
    yssd, ypre, states, pre_act = _ssd_fwd(pssd, pdt, w["conv_w"], conv_b, dtb, alog, ssd_d, ssd_nw)
    (loss8, dys5, dyssd, dh1, n2, dgl2, dpp, g_ple_nw, g_fin_nw) = _head_fwd_bwd(
        x, ys5, yssd, p, target, w["w_out"], w["w_ple_gate"], w["w_ple_proj"], ple_nw, fin_nw)

    (dxbc, ddt, dzd, g_cw, g_cb, g_dtb, g_alog, g_ssd_d, g_ssd_nw) = _ssd_bwd(
        dyssd, ypre, pssd, pdt, states, pre_act, w["conv_w"], dtb, alog, ssd_d, ssd_nw)
    dzs, dyssm, a_glu, dgl1, g_bglu, g_s5d = _s5_post_bwd(dys5, yssm, u_s5, z_s5, s5_d, w["s5_w_glu"], b_glu)
    duflat, dbk, r12, q12, xy, uv, xy0, uv0, da8 = _s5_core_bwd(uflat, _flat_tk(dyssm, nc), hsave, bbp, pwr, cnp, pwf, p1, p2)
    da64 = jnp.concatenate([da8[:, 0, :S5_STATE] + da8[:, 0, S5_STATE:], da8[:, 1, S5_STATE:] - da8[:, 1, :S5_STATE]], axis=-1)
    g_s5 = small_vjp(_s5_small_cotangents(dbk, r12, q12, xy, uv, xy0, uv0, da64))
    gx, du, g_norm_w = _in_proj_bwd(x, norm_w, dh1, _unflat_hs(duflat, nc), dyssm, s5_d, dzs, dzd, dxbc, ddt, w_main, w_dt)

    g_w_in = jnp.concatenate([
        _matmul_tn(du, hn, "dw_in_u"), _matmul_tn(dzs, hn, "dw_in_zs"), _matmul_tn(dzd, hn, "dw_in_zd"),
        _matmul_tn(dxbc, hn, "dw_in_xbc"), _matmul_tn(ddt, hn, "dw_in_dt")[:SSD_HEADS]], axis=0)
    grads = {
        "norm_w": g_norm_w, "w_in": g_w_in,
        "s5_A_re": g_s5[0], "s5_A_im": g_s5[1], "s5_log_dt": g_s5[2], "s5_B_re": g_s5[3], "s5_B_im": g_s5[4],
        "s5_C_re": g_s5[5], "s5_C_im": g_s5[6], "s5_D": g_s5d, "s5_w_glu": _matmul_tn(a_glu, dgl1, "dw_glu"), "s5_b_glu": g_bglu,
        "conv_w": g_cw[:4], "conv_b": g_cb, "dt_bias": g_dtb[:, :SSD_HEADS], "A_log": g_alog[:, :SSD_HEADS],
        "ssd_D": g_ssd_d[:, :SSD_HEADS], "ssd_norm_w": g_ssd_nw,
        "w_out": jnp.concatenate([_matmul_tn(ys5, dh1, "dw_out_s5"), _matmul_tn(yssd, dh1, "dw_out_ssd")], axis=0),
        "ple_norm_w": g_ple_nw, "w_ple_gate": _matmul_tn(n2, dgl2, "dw_gate"), "w_ple_proj": _matmul_tn(p, dpp, "dw_proj"),
        "final_norm_w": g_fin_nw,
    }
    return loss8[0, 0], gx, grads


WEIGHTS = ("norm_w", "w_in", "s5_A_re", "s5_A_im", "s5_log_dt", "s5_B_re", "s5_B_im", "s5_C_re", "s5_C_im", "s5_D", "s5_w_glu",
           "s5_b_glu", "conv_w", "conv_b", "dt_bias", "A_log", "ssd_D", "ssd_norm_w", "w_out", "ple_norm_w", "w_ple_gate",
           "w_ple_proj", "final_norm_w")
BIG = {"w_in": ((1284, 1024), 0), "s5_w_glu": ((256, 1024), 0), "w_out": ((512, 1024), 0), "w_ple_gate": ((256, 1024), 0),
       "w_ple_proj": ((256, 256), 1)}
SMALL = {"norm_w": (1024,), "s5_A_re": (64, 64), "s5_A_im": (64, 64), "s5_log_dt": (64,), "s5_B_re": (64, 64, 16),
         "s5_B_im": (64, 64, 16), "s5_C_re": (64, 16, 64), "s5_C_im": (64, 16, 64), "s5_D": (1024,), "s5_b_glu": (1024,),
         "conv_w": (4, 2048), "conv_b": (2048,), "dt_bias": (16,), "A_log": (16,), "ssd_D": (16,), "ssd_norm_w": (1024,),
         "ple_norm_w": (1024,), "final_norm_w": (1024,)}
BIG_ROWS = {n: s[0] * s[1] // LANES for n, (s, _) in BIG.items()}
BIG_ROWS_TOTAL = sum(BIG_ROWS.values())
SMALL_TOTAL = sum(math.prod(s) for s in SMALL.values())
SMALL_PIECE_ROWS = -(-SMALL_TOTAL // (N_CHIPS * 16 * LANES)) * 16
HALF_ROWS = (BIG_ROWS_TOTAL + SMALL_PIECE_ROWS) // 2
SMALL_ROW0 = BIG_ROWS_TOTAL - HALF_ROWS


def _mesh_pos():
    return lax.axis_index("x"), lax.axis_index("y"), lax.axis_index("c")


def _other_chips(x, y):
    return [(1 - x, y), (x, 1 - y), (1 - x, 1 - y)]


def _comm_params():
    return pltpu.CompilerParams(has_side_effects=True)


def _all_gather_chips(wpack, cw):
    half = wpack.shape[0] // 2

    def body(w_ref, c_ref, wo_ref, co_ref, send_sems, recv_sems, fwd_send, fwd_recv, loc_sems):
        x, y, c = _mesh_pos()
        me = 2 * x + y
        sib = (x, y, 1 - c)
        mine = pl.ds(c * half, half)
        theirs = pl.ds((1 - c) * half, half)
        others = _other_chips(x, y)
        loc = [pltpu.make_async_copy(c_ref, co_ref.at[me], loc_sems.at[0])]
        for cp in loc:
            cp.start()

        def from_chip(k, chip, dev):
            return pltpu.make_async_remote_copy(w_ref.at[mine], wo_ref.at[chip, mine], send_sems.at[2 * k], recv_sems.at[2 * k],
                                                device_id=dev, device_id_type=MESH)

        def conv_from(k, chip, dev):
            return pltpu.make_async_remote_copy(c_ref, co_ref.at[chip], send_sems.at[2 * k + 1], recv_sems.at[2 * k + 1],
                                                device_id=dev, device_id_type=MESH)

        def passed(k, chip, rows):
            return pltpu.make_async_remote_copy(wo_ref.at[chip, rows], wo_ref.at[chip, rows], fwd_send.at[k], fwd_recv.at[k],
                                                device_id=sib, device_id_type=MESH)

        sends = []
        for k, (px, py) in enumerate(others):
            sends += [from_chip(k, me, (px, py, c)), conv_from(k, me, (px, py, c))]
        for cp in sends:
            cp.start()
        fwds = []
        for k, (px, py) in enumerate(others):
            chip = 2 * px + py
            from_chip(k, chip, (px, py, c)).wait_recv()
            fwds.append(passed(k, chip, mine))
            fwds[-1].start()
        for k, (px, py) in enumerate(others):
            chip = 2 * px + py
            passed(k, chip, theirs).wait_recv()
            conv_from(k, chip, (px, py, c)).wait_recv()
        for cp in sends + fwds:
            cp.wait_send()
        for cp in loc:
            cp.wait()

    return pl.pallas_call(
        body, name="all_gather_weights", in_specs=[ANY, ANY], out_specs=[ANY, ANY],
        out_shape=[jax.ShapeDtypeStruct((N_CHIPS,) + wpack.shape, wpack.dtype), jax.ShapeDtypeStruct((N_CHIPS,) + cw.shape, cw.dtype)],
        scratch_shapes=[pltpu.SemaphoreType.DMA((6,)), pltpu.SemaphoreType.DMA((6,)), pltpu.SemaphoreType.DMA((3,)),
                        pltpu.SemaphoreType.DMA((3,)), pltpu.SemaphoreType.DMA((1,))],
        compiler_params=_comm_params(),
    )(wpack, cw)


def _exchange_pair(gp):
    def body(g_ref, r_ref, send_sems, recv_sems):
        x, y, c = _mesh_pos()
        cps = [pltpu.make_async_remote_copy(g_ref.at[s, 1 - c], r_ref.at[s], send_sems.at[s], recv_sems.at[s],
                                            device_id=(x, y, 1 - c), device_id_type=MESH) for s in range(N_CHIPS)]
        for cp in cps:
            cp.start()
        for cp in cps:
            cp.wait()

    return pl.pallas_call(
        body, name="grad_exchange_pair", in_specs=[ANY], out_specs=ANY,
        out_shape=jax.ShapeDtypeStruct((N_CHIPS,) + gp.shape[2:], gp.dtype),
        scratch_shapes=[pltpu.SemaphoreType.DMA((N_CHIPS,)), pltpu.SemaphoreType.DMA((N_CHIPS,))],
        compiler_params=_comm_params(),
    )(gp)


def _pair_sum(mine, from_sibling):
    def body(a_ref, b_ref, bf_ref, tail_ref):
        s = a_ref[0] + b_ref[0]
        bf_ref[0] = s.astype(BF)
        tail_ref[0] = s[SMALL_ROW0:, :]

    piece = pl.BlockSpec((1, HALF_ROWS, LANES), lambda i: (i, 0, 0))
    return pl.pallas_call(
        body, name="grad_pair_sum", grid=(N_CHIPS,), in_specs=[piece, piece],
        out_specs=[piece, pl.BlockSpec((1, SMALL_PIECE_ROWS, LANES), lambda i: (i, 0, 0))],
        out_shape=[jax.ShapeDtypeStruct((N_CHIPS, HALF_ROWS, LANES), BF), jax.ShapeDtypeStruct((N_CHIPS, SMALL_PIECE_ROWS, LANES), F32)],
        compiler_params=_params(("parallel",)),
    )(mine, from_sibling)


def _exchange_chips(ps_bf, ps_tail):
    def body(p_ref, t_ref, r_ref, rt_ref, send_sems, recv_sems):
        x, y, c = _mesh_pos()
        cps = []
        for k, (px, py) in enumerate(_other_chips(x, y)):
            cps.append(pltpu.make_async_remote_copy(p_ref.at[2 * px + py], r_ref.at[k], send_sems.at[2 * k], recv_sems.at[2 * k],
                                                    device_id=(px, py, c), device_id_type=MESH))
            cps.append(pltpu.make_async_remote_copy(t_ref.at[2 * px + py], rt_ref.at[k], send_sems.at[2 * k + 1],
                                                    recv_sems.at[2 * k + 1], device_id=(px, py, c), device_id_type=MESH))
        for cp in cps:
            cp.start()
        for cp in cps:
            cp.wait()

    return pl.pallas_call(
        body, name="grad_exchange_chips", in_specs=[ANY, ANY], out_specs=[ANY, ANY],
        out_shape=[jax.ShapeDtypeStruct((N_CHIPS - 1,) + ps_bf.shape[1:], ps_bf.dtype),
                   jax.ShapeDtypeStruct((N_CHIPS - 1,) + ps_tail.shape[1:], ps_tail.dtype)],
        scratch_shapes=[pltpu.SemaphoreType.DMA((6,)), pltpu.SemaphoreType.DMA((6,))],
        compiler_params=_comm_params(),
    )(ps_bf, ps_tail)


def _chip_sum(own_bf, own_tail, others_bf, others_tail):
    def body(ob_ref, ot_ref, b_ref, t_ref, o_ref):
        acc = ob_ref[0:SMALL_ROW0, :].astype(F32)
        tail = ot_ref[...]
        for k in range(N_CHIPS - 1):
            acc = acc + b_ref[k, 0:SMALL_ROW0, :].astype(F32)
            tail = tail + t_ref[k]
        o_ref[0:SMALL_ROW0, :] = acc
        o_ref[SMALL_ROW0:, :] = tail

    return pl.pallas_call(
        body, name="grad_chip_sum", out_shape=jax.ShapeDtypeStruct((HALF_ROWS, LANES), F32),
        compiler_params=_params(),
    )(own_bf, own_tail, others_bf, others_tail)


def _swap_reduced_halves(gh):
    def body(g_ref, o_ref, send_sem, recv_sem):
        x, y, c = _mesh_pos()
        cp = pltpu.make_async_remote_copy(g_ref, o_ref, send_sem, recv_sem, device_id=(x, y, 1 - c), device_id_type=MESH)
        cp.start()
        cp.wait()

    return pl.pallas_call(
        body, name="grad_swap_halves", in_specs=[ANY], out_specs=ANY,
        out_shape=jax.ShapeDtypeStruct(gh.shape, gh.dtype),
        scratch_shapes=[pltpu.SemaphoreType.DMA, pltpu.SemaphoreType.DMA],
        compiler_params=_comm_params(),
    )(gh)


def _gather_small(second_half):
    def body(gs_ref, sm_ref, send_sems, recv_sems, loc_sem):
        x, y, c = _mesh_pos()
        me = 2 * x + y
        small = gs_ref.at[pl.ds(SMALL_ROW0, SMALL_PIECE_ROWS)]
        loc = pltpu.make_async_copy(small, sm_ref.at[me], loc_sem)
        loc.start()
        cps = [pltpu.make_async_remote_copy(small, sm_ref.at[me], send_sems.at[k], recv_sems.at[k],
                                            device_id=(px, py, c), device_id_type=MESH)
               for k, (px, py) in enumerate(_other_chips(x, y))]
        for cp in cps:
            cp.start()
        for cp in cps:
            cp.wait()
        loc.wait()

    return pl.pallas_call(
        body, name="grad_gather_small", in_specs=[ANY], out_specs=ANY,
        out_shape=jax.ShapeDtypeStruct((N_CHIPS, SMALL_PIECE_ROWS, LANES), second_half.dtype),
        scratch_shapes=[pltpu.SemaphoreType.DMA((3,)), pltpu.SemaphoreType.DMA((3,)), pltpu.SemaphoreType.DMA],
        compiler_params=_comm_params(),
    )(second_half)


def _pack_grads(grads):
    small = jnp.concatenate([grads[n].reshape(-1) for n in SMALL])
    small = jnp.pad(small, (0, N_CHIPS * SMALL_PIECE_ROWS * LANES - SMALL_TOTAL)).reshape(N_CHIPS, SMALL_PIECE_ROWS, LANES)
    pieces = []
    for s in range(N_CHIPS):
        rows = []
        for n, (shp, axis) in BIG.items():
            g = grads[n]
            blk = g[s * shp[0]:(s + 1) * shp[0], :] if axis == 0 else g[:, s * shp[1]:(s + 1) * shp[1]]
            rows.append(blk.reshape(-1, LANES))
        rows.append(small[s])
        pieces.append(jnp.concatenate(rows, axis=0).reshape(2, HALF_ROWS, LANES))
    return jnp.stack(pieces)


def _unpack_shard(first_half, second_half):
    rows = jnp.concatenate([first_half, second_half], axis=0)
    out, r0 = {}, 0
    for n, (shp, _) in BIG.items():
        out[n] = rows[r0:r0 + BIG_ROWS[n]].reshape(shp)
        r0 += BIG_ROWS[n]
    return out


def _unpack_small(sm):
    flat = sm.reshape(-1)
    out, o = {}, 0
    for n, shp in SMALL.items():
        k = math.prod(shp)
        out[n] = flat[o:o + k].reshape(shp)
        o += k
    return out


def _as_2d(a):
    n = a.size
    if a.ndim >= 2 and a.shape[-1] > 1024:
        return a.reshape(-1, a.shape[-1])
    if n % 1024 == 0:
        return a.reshape(n // 1024, 1024)
    return a.reshape(1, n)


def _adamw(w, g, m, v, name):
    shape = w.shape
    w2, g2, m2, v2 = (_as_2d(a) for a in (w, g, m, v))
    rows, cols = w2.shape
    rb = 256 if rows >= 512 else rows
    by_cols = rows % rb != 0

    def body(w_ref, g_ref, m_ref, v_ref, d_ref, mo_ref, vo_ref):
        gv = g_ref[...]
        mn = ADAM_B1 * m_ref[...] + (1.0 - ADAM_B1) * gv
        vn = ADAM_B2 * v_ref[...] + (1.0 - ADAM_B2) * (gv * gv)
        m_hat = mn / (1.0 - ADAM_B1 ** ADAM_STEP)
        v_hat = vn / (1.0 - ADAM_B2 ** ADAM_STEP)
        d_ref[...] = -ADAM_LR * (m_hat / (jnp.sqrt(v_hat) + ADAM_EPS) + ADAM_WD * w_ref[...])
        mo_ref[...] = mn
        vo_ref[...] = vn

    spec = pl.BlockSpec((rows, 256), lambda i: (0, i)) if by_cols else _row_spec(rb, cols)
    sds = jax.ShapeDtypeStruct((rows, cols), F32)
    d, mo, vo = pl.pallas_call(
        body, name=name, grid=(cols // 256 if by_cols else rows // rb,), in_specs=[spec] * 4, out_specs=[spec] * 3, out_shape=[sds] * 3,
        compiler_params=_params(("parallel",)),
    )(w2, g2, m2, v2)
    return d.reshape(shape), mo.reshape(shape), vo.reshape(shape)


def kernel(x, p, norm_w, w_in, s5_A_re, s5_A_im, s5_log_dt, s5_B_re, s5_B_im, s5_C_re, s5_C_im, s5_D, s5_w_glu, s5_b_glu, conv_w, conv_b, dt_bias, A_log, ssd_D, ssd_norm_w, w_out, ple_norm_w, w_ple_gate, w_ple_proj, final_norm_w, loss_target, m_norm_w, m_w_in, m_s5_A_re, m_s5_A_im, m_s5_log_dt, m_s5_B_re, m_s5_B_im, m_s5_C_re, m_s5_C_im, m_s5_D, m_s5_w_glu, m_s5_b_glu, m_conv_w, m_conv_b, m_dt_bias, m_A_log, m_ssd_D, m_ssd_norm_w, m_w_out, m_ple_norm_w, m_w_ple_gate, m_w_ple_proj, m_final_norm_w, v_norm_w, v_w_in, v_s5_A_re, v_s5_A_im, v_s5_log_dt, v_s5_B_re, v_s5_B_im, v_s5_C_re, v_s5_C_im, v_s5_D, v_s5_w_glu, v_s5_b_glu, v_conv_w, v_conv_b, v_dt_bias, v_A_log, v_ssd_D, v_ssd_norm_w, v_w_out, v_ple_norm_w, v_w_ple_gate, v_w_ple_proj, v_final_norm_w):
    given = (norm_w, w_in, s5_A_re, s5_A_im, s5_log_dt, s5_B_re, s5_B_im, s5_C_re, s5_C_im, s5_D, s5_w_glu, s5_b_glu, conv_w, conv_b,
             dt_bias, A_log, ssd_D, ssd_norm_w, w_out, ple_norm_w, w_ple_gate, w_ple_proj, final_norm_w)
    given_m = (m_norm_w, m_w_in, m_s5_A_re, m_s5_A_im, m_s5_log_dt, m_s5_B_re, m_s5_B_im, m_s5_C_re, m_s5_C_im, m_s5_D, m_s5_w_glu,
               m_s5_b_glu, m_conv_w, m_conv_b, m_dt_bias, m_A_log, m_ssd_D, m_ssd_norm_w, m_w_out, m_ple_norm_w, m_w_ple_gate,
               m_w_ple_proj, m_final_norm_w)
    given_v = (v_norm_w, v_w_in, v_s5_A_re, v_s5_A_im, v_s5_log_dt, v_s5_B_re, v_s5_B_im, v_s5_C_re, v_s5_C_im, v_s5_D, v_s5_w_glu,
               v_s5_b_glu, v_conv_w, v_conv_b, v_dt_bias, v_A_log, v_ssd_D, v_ssd_norm_w, v_w_out, v_ple_norm_w, v_w_ple_gate,
               v_w_ple_proj, v_final_norm_w)
    wts, mom, var = dict(zip(WEIGHTS, given)), dict(zip(WEIGHTS, given_m)), dict(zip(WEIGHTS, given_v))
    drop = lambda n, a: a if n == "final_norm_w" else a[0]

    shard2d = lambda n, a: a[0].T if n == "w_in" else drop(n, a)
    wpack = jnp.concatenate([shard2d(n, wts[n]).astype(BF).reshape(-1, LANES) for n in BIG], axis=0)
    wall, cwall = _all_gather_chips(wpack, drop("conv_w", wts["conv_w"]))
    chip = 2 * lax.axis_index("x") + lax.axis_index("y")
    full, r0 = {}, 0
    for n, (shp, axis) in BIG.items():
        blk = lax.dynamic_update_slice_in_dim(wall[:, r0:r0 + BIG_ROWS[n]].reshape((N_CHIPS,) + shp),
                                              shard2d(n, wts[n]).astype(BF)[None], chip, axis=0)
        full[n] = blk.reshape(N_CHIPS * shp[0], shp[1]) if axis == 0 else blk.transpose(1, 0, 2).reshape(shp[0], N_CHIPS * shp[1])
        r0 += BIG_ROWS[n]
    for n in SMALL:
        full[n] = drop(n, wts[n])
    full["conv_w"] = cwall.transpose(1, 0, 2).reshape(4, 2048)

    loss, gx, grads = _local_step(x[0], p[0, 0], loss_target[0], full)
    loss = lax.psum(loss, MESH_AXES)

    gp = _pack_grads({n: grads[n].reshape(SMALL[n]) if n in SMALL else grads[n] for n in WEIGHTS})
    c = lax.axis_index("c")
    from_sibling = _exchange_pair(gp)
    mine = lax.dynamic_index_in_dim(gp, c, axis=1, keepdims=False)
    ps_bf, ps_tail = _pair_sum(mine, from_sibling)
    others_bf, others_tail = _exchange_chips(ps_bf, ps_tail)
    own_bf = lax.dynamic_index_in_dim(ps_bf, chip, axis=0, keepdims=False)
    own_tail = lax.dynamic_index_in_dim(ps_tail, chip, axis=0, keepdims=False)
    reduced_half = _chip_sum(own_bf, own_tail, others_bf, others_tail)
    sibling_half = _swap_reduced_halves(reduced_half)
    first_half = jnp.where(c == 0, reduced_half, sibling_half)
    second_half = jnp.where(c == 0, sibling_half, reduced_half)
    sm = _gather_small(second_half)
    g_final = {**_unpack_small(sm), **_unpack_shard(first_half, second_half)}
    g_final["conv_w"] = lax.dynamic_slice_in_dim(g_final["conv_w"], chip * 512, 512, axis=1)

    outs_g, outs_d, outs_m, outs_v = [], [], [], []
    for n in WEIGHTS:
        if n == "w_in":
            res = _adamw(wts[n][0].T, g_final[n], mom[n][0].T, var[n][0].T, "adamw_" + n)
            g, d, mo, vo = (a.T[None] for a in (g_final[n],) + res)
        else:
            g = g_final[n].reshape(wts[n].shape)
            d, mo, vo = _adamw(wts[n], g, mom[n], var[n], "adamw_" + n)
        outs_g.append(g)
        outs_d.append(d)
        outs_m.append(mo)
        outs_v.append(vo)
    return (loss, gx[None], *outs_g, *outs_d, *outs_m, *outs_v)
```

```python
import functools
import math

import jax
import jax.numpy as jnp
from jax import lax
from jax.experimental import pallas as pl
from jax.experimental.pallas import tpu as pltpu

F32 = jnp.float32
BF = jnp.bfloat16
EPS = 1e-6
CHUNK = 64
D_MODEL = 1024
S5_GROUPS = 64
S5_CH = 16
S5_STATE = 64
S5_GROUP_BATCH = 2
SSD_HEADS = 16
SSD_HEAD_DIM = 64
SSD_GROUPS = 4
SSD_STATE = 128
D_MAIN = 5120
LANES = 128
TOKEN_TILE = 256
VMEM_LIMIT = 56 * 1024 * 1024
MESH_AXES = ("x", "y", "c")
N_CHIPS = 4
ADAM_LR, ADAM_B1, ADAM_B2, ADAM_EPS, ADAM_WD, ADAM_STEP = 0.001, 0.9, 0.999, 1e-08, 0.01, 10
MESH = pl.DeviceIdType.MESH
ANY = pl.BlockSpec(memory_space=pl.ANY)


def _dot(a, b):
    return jnp.dot(a, b, preferred_element_type=F32)


def _dot_nt(a, b):
    return lax.dot_general(a, b, (((1,), (1,)), ((), ())), preferred_element_type=F32)


def _dot_tn(a, b):
    return lax.dot_general(a, b, (((0,), (0,)), ((), ())), preferred_element_type=F32)


def _sigmoid(x):
    return 1.0 / (1.0 + jnp.exp(-x))


def _softplus(x):
    return jnp.maximum(x, 0.0) + jnp.log(1.0 + jnp.exp(-jnp.abs(x)))


_GELU_C = math.sqrt(2.0 / math.pi)


def _gelu(x):
    return 0.5 * x * (1.0 + jnp.tanh(_GELU_C * (x + 0.044715 * x * x * x)))


def _gelu_grad(x):
    th = jnp.tanh(_GELU_C * (x + 0.044715 * x * x * x))
    return 0.5 * (1.0 + th) + 0.5 * x * (1.0 - th * th) * _GELU_C * (1.0 + 3.0 * 0.044715 * x * x)


def _params(sem=None):
    return pltpu.CompilerParams(dimension_semantics=sem, vmem_limit_bytes=VMEM_LIMIT)


def _row_spec(tl, width, col=0):
    return pl.BlockSpec((tl, width), lambda i, col=col: (i, col))


def _const_spec(shape):
    nd = len(shape)
    return pl.BlockSpec(shape, lambda *_: (0,) * nd)


def _in_proj_fwd(x, norm_w, w_main, w_dt):
    L = x.shape[0]
    tl = min(TOKEN_TILE, L)

    def body(x_ref, nw_ref, wm_ref, wd_ref, hn_ref, u_ref, zs_ref, pssd_ref, pd_ref):
        xv = x_ref[...]
        r = lax.rsqrt(jnp.mean(xv * xv, axis=-1, keepdims=True) + EPS)
        hn = (xv * r * nw_ref[...]).astype(BF)
        hn_ref[...] = hn
        for j, o_ref in enumerate((u_ref, zs_ref)):
            o_ref[...] = _dot_nt(hn, wm_ref[j * 1024:(j + 1) * 1024, :]).astype(BF)
        for j in range(3):
            pssd_ref[:, j * 1024:(j + 1) * 1024] = _dot_nt(hn, wm_ref[(j + 2) * 1024:(j + 3) * 1024, :])
        pd_ref[...] = _dot_nt(hn, wd_ref[...])

    return pl.pallas_call(
        body, name="in_proj_fwd", grid=(L // tl,),
        in_specs=[_row_spec(tl, D_MODEL), _const_spec((1, D_MODEL)), _const_spec((D_MAIN, D_MODEL)), _const_spec((LANES, D_MODEL))],
        out_specs=[_row_spec(tl, D_MODEL), _row_spec(tl, D_MODEL), _row_spec(tl, D_MODEL), _row_spec(tl, 3072), _row_spec(tl, LANES)],
        out_shape=[jax.ShapeDtypeStruct((L, D_MODEL), BF), jax.ShapeDtypeStruct((L, D_MODEL), BF), jax.ShapeDtypeStruct((L, D_MODEL), BF),
                   jax.ShapeDtypeStruct((L, 3072), F32),
                   jax.ShapeDtypeStruct((L, LANES), F32)],
        compiler_params=_params(("arbitrary",)),
    )(x, norm_w, w_main, w_dt)


def _in_proj_bwd(x, norm_w, dh1, du_flat, dyssm, s5_d, dzs, dzd, dxbc, ddt, w_main, w_dt):
    L = x.shape[0]
    tl = min(TOKEN_TILE, L)

    def body(x_ref, nw_ref, dh1_ref, duf_ref, dys_ref, d_ref, dzs_ref, dzd_ref, dxbc_ref, ddt_ref, wm_ref, wd_ref,
             gx_ref, du_ref, gnw_ref):
        @pl.when(pl.program_id(0) == 0)
        def _():
            gnw_ref[...] = jnp.zeros_like(gnw_ref)

        du = (duf_ref[...].astype(F32) + dys_ref[...].astype(F32) * d_ref[...]).astype(BF)
        du_ref[...] = du
        dhn = _dot(du, wm_ref[0:1024, :])
        dhn += _dot(dzs_ref[...], wm_ref[1024:2048, :])
        dhn += _dot(dzd_ref[...], wm_ref[2048:3072, :])
        dhn += _dot(dxbc_ref[...], wm_ref[3072:5120, :])
        dhn += _dot(ddt_ref[...].astype(BF), wd_ref[...])
        xv = x_ref[...]
        r = lax.rsqrt(jnp.mean(xv * xv, axis=-1, keepdims=True) + EPS)
        xh = xv * r
        gnw_ref[...] += jnp.sum(dhn * xh, axis=0, keepdims=True)
        g = dhn * nw_ref[...]
        gx_ref[...] = dh1_ref[...] + r * (g - xh * jnp.mean(g * xh, axis=-1, keepdims=True))

    return pl.pallas_call(
        body, name="in_proj_bwd", grid=(L // tl,),
        in_specs=[_row_spec(tl, D_MODEL), _const_spec((1, D_MODEL)), _row_spec(tl, D_MODEL), _row_spec(tl, D_MODEL),
                  _row_spec(tl, D_MODEL), _const_spec((1, D_MODEL)), _row_spec(tl, D_MODEL), _row_spec(tl, D_MODEL),
                  _row_spec(tl, 2048), _row_spec(tl, LANES), _const_spec((D_MAIN, D_MODEL)), _const_spec((LANES, D_MODEL))],
        out_specs=[_row_spec(tl, D_MODEL), _row_spec(tl, D_MODEL), _const_spec((1, D_MODEL))],
        out_shape=[jax.ShapeDtypeStruct((L, D_MODEL), F32), jax.ShapeDtypeStruct((L, D_MODEL), BF), jax.ShapeDtypeStruct((1, D_MODEL), F32)],
        compiler_params=_params(("arbitrary",)),
    )(x, norm_w, dh1, du_flat, dyssm, s5_d, dzs, dzd, dxbc, ddt, w_main, w_dt)


def _matmul_tn(a, b, name):
    L, M = a.shape
    N = b.shape[1]
    tm, tn, tk = min(M, 1024), min(N, 1024), min(L, 1024)

    def body(a_ref, b_ref, o_ref):
        @pl.when(pl.program_id(2) == 0)
        def _():
            o_ref[...] = jnp.zeros_like(o_ref)

        o_ref[...] += _dot_tn(a_ref[...].astype(BF), b_ref[...].astype(BF))

    return pl.pallas_call(
        body, name=name, grid=(M // tm, N // tn, L // tk),
        in_specs=[pl.BlockSpec((tk, tm), lambda i, j, k: (k, i)), pl.BlockSpec((tk, tn), lambda i, j, k: (k, j))],
        out_specs=pl.BlockSpec((tm, tn), lambda i, j, k: (i, j)),
        out_shape=jax.ShapeDtypeStruct((M, N), F32),
        compiler_params=_params(("parallel", "parallel", "arbitrary")),
    )(a, b)


def _s5_discretise(a_re, a_im, log_dt, b_re, b_im, c_re, c_im):
    dt = jnp.exp(log_dt)[:, None]
    tau = jnp.arange(CHUNK + 1, dtype=F32)
    mag = jnp.exp((a_re * dt)[:, :, None] * tau)
    ang = (a_im * dt)[:, :, None] * tau
    pw_re, pw_im = mag * jnp.cos(ang), mag * jnp.sin(ang)
    er, ei = pw_re[:, :, 1] - 1.0, pw_im[:, :, 1]
    den = a_re * a_re + a_im * a_im
    beta_re, beta_im = (er * a_re + ei * a_im) / den, (ei * a_re - er * a_im) / den
    bb_re = (beta_re[:, :, None] * b_re - beta_im[:, :, None] * b_im).transpose(0, 2, 1)
    bb_im = (beta_re[:, :, None] * b_im + beta_im[:, :, None] * b_re).transpose(0, 2, 1)
    return bb_re, bb_im, c_re, c_im, pw_re, pw_im


def _s5_table_factors(bb_re, bb_im, c_re, c_im, pw_re, pw_im):
    pr = pw_re[:, :, CHUNK - 1::-1].transpose(0, 2, 1)
    pi = pw_im[:, :, CHUNK - 1::-1].transpose(0, 2, 1)
    bbp = jnp.concatenate([bb_re, bb_im], axis=-1)
    pwr = jnp.concatenate([pr, pi], axis=-1)
    cnp = jnp.concatenate([c_re, c_im], axis=-1)
    pwf = jnp.concatenate([pw_re.transpose(0, 2, 1), pw_im.transpose(0, 2, 1)], axis=-1)
    return bbp, pwr, cnp, jnp.pad(pwf, ((0, 0), (0, 7), (0, 0)))


def _s5_small_cotangents(dbk, r12, q12, xy, uv, xy0, uv0, da64):
    n = S5_STATE
    fold = lambda a: a[..., :n] + a[..., n:]
    fold_m = lambda a: a[..., n:] - a[..., :n]
    dbb_re = fold(r12[:, :S5_CH]) + dbk[..., :n]
    dbb_im = fold_m(r12[:, S5_CH:]) + dbk[..., n:]
    dpr, dpi = fold(q12[:, :CHUNK]), fold_m(q12[:, CHUNK:])
    dc_re = -fold_m(xy[:, :S5_CH]) - fold_m(xy0[:, :S5_CH])
    dc_im = -fold(xy[:, S5_CH:]) - fold(xy0[:, S5_CH:])
    dp1_re, dp1_im = -fold_m(uv[:, :CHUNK]), -fold(uv[:, CHUNK:])
    dp0_re, dp0_im = -fold_m(uv0[:, :CHUNK]), -fold(uv0[:, CHUNK:])
    zero = jnp.zeros((S5_GROUPS, n, 1), F32)
    dpw_re = (jnp.concatenate([(dpr[:, ::-1] + dp0_re).transpose(0, 2, 1), zero], axis=-1)
              + jnp.concatenate([zero, dp1_re.transpose(0, 2, 1)], axis=-1)).at[:, :, CHUNK].add(da64[:, :n])
    dpw_im = (jnp.concatenate([(dpi[:, ::-1] + dp0_im).transpose(0, 2, 1), zero], axis=-1)
              + jnp.concatenate([zero, dp1_im.transpose(0, 2, 1)], axis=-1)).at[:, :, CHUNK].add(da64[:, n:])
    return dbb_re, dbb_im, dc_re, dc_im, dpw_re, dpw_im


def _s5_scan_powers(a_re, a_im, log_dt, nsteps):
    dt = jnp.exp(log_dt)[:, None]
    steps = (CHUNK * (2.0 ** jnp.arange(8, dtype=F32)))[None, :, None]
    mag = jnp.exp((a_re * dt)[:, None, :] * steps)
    ang = (a_im * dt)[:, None, :] * steps
    re, im = mag * jnp.cos(ang), mag * jnp.sin(ang)
    del nsteps
    return jnp.concatenate([re, re], -1), jnp.concatenate([-im, im], -1)


def _build_toeplitz(kmat, tg_ref):
    lane = lax.broadcasted_iota(jnp.int32, (CHUNK, CHUNK * S5_CH), 1)
    srow = lax.broadcasted_iota(jnp.int32, (CHUNK, CHUNK * S5_CH), 0)
    keep = lane >= S5_CH * srow
    for h in range(S5_CH):
        row = jnp.broadcast_to(kmat[h:h + 1, :], (CHUNK, CHUNK * S5_CH))
        rolled = pltpu.roll(row, 0, 1, stride=S5_CH, stride_axis=0)
        tg_ref[h * CHUNK:(h + 1) * CHUNK, :] = jnp.where(keep, rolled, 0.0).astype(BF)


def _swap_halves(x):
    return pltpu.roll(x, S5_STATE, 1)


def _hi_lo(x):
    hi = x.astype(BF)
    return hi, (x - hi.astype(F32)).astype(BF)


def _dot3(dot, a, b):
    a_hi, a_lo = _hi_lo(a)
    b_hi, b_lo = _hi_lo(b)
    return dot(a_hi, b_hi) + dot(a_hi, b_lo) + dot(a_lo, b_hi)


def _build_state_tables(bbp_ref, pwr_ref, cnp_ref, pwf_ref, wst_ref, wofft_ref, w0_ref):
    lane = lax.broadcasted_iota(jnp.int32, (1, 2 * S5_STATE), 1)
    left = lane < S5_STATE
    pwr = pwr_ref[0]
    pwr_sw = _swap_halves(pwr)
    for hh in range(S5_CH):
        bb = bbp_ref[0, hh:hh + 1, :]
        bb_sw = _swap_halves(bb)
        wst_ref[hh * CHUNK:(hh + 1) * CHUNK, :] = (jnp.where(left, bb, bb_sw) * pwr
                                                   + jnp.where(left, -bb_sw, bb) * pwr_sw).astype(BF)
    cn = cnp_ref[0]
    cn_sw = _swap_halves(cn)
    c_a = jnp.where(left, cn, -cn_sw)
    c_b = jnp.where(left, -cn_sw, -cn)
    p_prev = pwf_ref[0, 0:1, :]
    for t in range(CHUNK):
        p = pwf_ref[0, t + 1:t + 2, :]
        w0_ref[t * S5_CH:(t + 1) * S5_CH, :] = c_a * p_prev + c_b * _swap_halves(p_prev)
        wofft_ref[t * S5_CH:(t + 1) * S5_CH, :] = (c_a * p + c_b * _swap_halves(p)).astype(BF)
        p_prev = p
    return _dot3(_dot_nt, bbp_ref[0], w0_ref[...])


def _reduce_table_cotangent(d_table, cnp_ref, pwf_ref, first_power):
    cn = cnp_ref[0]
    cn_sw = _swap_halves(cn)
    xa = jnp.zeros((S5_CH, 2 * S5_STATE), F32)
    ya = jnp.zeros((S5_CH, 2 * S5_STATE), F32)
    ru_rows, rv_rows = [], []
    for t in range(CHUNK):
        blk = d_table[t * S5_CH:(t + 1) * S5_CH, :]
        p = pwf_ref[0, t + first_power:t + first_power + 1, :]
        xa = xa + blk * p
        ya = ya + blk * _swap_halves(p)
        ru_rows.append(jnp.sum(blk * cn, axis=0, keepdims=True))
        rv_rows.append(jnp.sum(blk * cn_sw, axis=0, keepdims=True))
    return jnp.concatenate([xa, ya], axis=0), jnp.concatenate(ru_rows + rv_rows, axis=0)


def _s5_core_fwd(uflat, bbp, pwr, cnp, pwf, p1, p2):
    G, nc, W = uflat.shape
    nsteps = max(1, (nc - 1).bit_length())

    def body(*refs):
        blocked, scratch = refs[:9], refs[9:]
        for j in range(S5_GROUP_BATCH):
            group_body(*[r.at[pl.ds(j, 1)] for r in blocked], *[s.at[j] for s in scratch])

    def group_body(u_ref, bbp_ref, pwr_ref, cnp_ref, pwf_ref, p1_ref, p2_ref, y_ref, h_ref, tg_ref, wst_ref, wofft_ref, w0_ref):
        _build_toeplitz(_build_state_tables(bbp_ref, pwr_ref, cnp_ref, pwf_ref, wst_ref, wofft_ref, w0_ref), tg_ref)
        u = u_ref[0]
        x = _dot(u, wst_ref[...])
        row = lax.broadcasted_iota(jnp.int32, x.shape, 0)
        d = 1
        for k in range(nsteps):
            sh = jnp.where(row >= d, pltpu.roll(x, d, 0), 0.0)
            x = x + p1_ref[0, k:k + 1, :] * sh + p2_ref[0, k:k + 1, :] * _swap_halves(sh)
            d *= 2
        h = jnp.where(row >= 1, pltpu.roll(x, 1, 0), 0.0)
        h_ref[0] = h
        y = _dot(u, tg_ref[...]) + _dot_nt(h.astype(BF), wofft_ref[...])
        y_ref[0] = y.astype(BF)

    gb = S5_GROUP_BATCH
    spec_g = lambda a, b: pl.BlockSpec((gb, a, b), lambda g: (g, 0, 0))
    return pl.pallas_call(
        body, name="s5_core_fwd", grid=(G // gb,),
        in_specs=[spec_g(nc, W), spec_g(S5_CH, 2 * S5_STATE), spec_g(CHUNK, 2 * S5_STATE),
                  spec_g(S5_CH, 2 * S5_STATE), spec_g(CHUNK + 8, 2 * S5_STATE), spec_g(8, 2 * S5_STATE), spec_g(8, 2 * S5_STATE)],
        out_specs=[spec_g(nc, W), spec_g(nc, 2 * S5_STATE)],
        out_shape=[jax.ShapeDtypeStruct((G, nc, W), BF), jax.ShapeDtypeStruct((G, nc, 2 * S5_STATE), F32)],
        scratch_shapes=[pltpu.VMEM((gb, W, W), BF), pltpu.VMEM((gb, W, 2 * S5_STATE), BF), pltpu.VMEM((gb, W, 2 * S5_STATE), BF),
                        pltpu.VMEM((gb, W, 2 * S5_STATE), F32)],
        compiler_params=_params(("arbitrary",)),
    )(uflat, bbp, pwr, cnp, pwf, p1, p2)


def _s5_core_bwd(uflat, dyflat, hsave, bbp, pwr, cnp, pwf, p1, p2):
    G, nc, W = uflat.shape
    nsteps = max(1, (nc - 1).bit_length())

    def body(*refs):
        blocked, (tg_ref, flip_ref, wst_ref, wofft_ref, w0_ref) = refs[:18], refs[18:]
        for j in range(S5_GROUP_BATCH):
            group_body(*[r.at[pl.ds(j, 1)] for r in blocked], tg_ref.at[j], flip_ref, wst_ref.at[j], wofft_ref.at[j], w0_ref.at[j])

    def group_body(u_ref, dy_ref, h_ref, bbp_ref, pwr_ref, cnp_ref, pwf_ref, p1_ref, p2_ref,
                   du_ref, dbk_ref, r12_ref, q12_ref, xy_ref, uv_ref, xy0_ref, uv0_ref, da_ref,
                   tg_ref, flip_ref, wst_ref, wofft_ref, w0_ref):
        @pl.when(pl.program_id(0) == 0)
        def _():
            r = lax.broadcasted_iota(jnp.int32, (W, W), 0)
            c = lax.broadcasted_iota(jnp.int32, (W, W), 1)
            flip_ref[...] = (((r >> 6) == (c >> 6)) & ((r & (CHUNK - 1)) + (c & (CHUNK - 1)) == CHUNK - 1)).astype(BF)

        _build_toeplitz(_build_state_tables(bbp_ref, pwr_ref, cnp_ref, pwf_ref, wst_ref, wofft_ref, w0_ref), tg_ref)
        u = u_ref[0]
        dy = dy_ref[0]
        h = h_ref[0]
        gh = _dot(dy, wofft_ref[...])
        row = lax.broadcasted_iota(jnp.int32, gh.shape, 0)
        x = jnp.where(row < nc - 1, pltpu.roll(gh, nc - 1, 0), 0.0)
        d = 1
        for k in range(nsteps):
            sh = jnp.where(row < nc - d, pltpu.roll(x, nc - d, 0), 0.0)
            x = x + p1_ref[0, k:k + 1, :] * sh - p2_ref[0, k:k + 1, :] * _swap_halves(sh)
            d *= 2
        gs = x.astype(BF)
        du_ref[0] = (_dot_nt(dy, tg_ref[...]) + _dot_nt(gs, wst_ref[...])).astype(BF)

        dwst = _dot_tn(u, gs)
        pwr = pwr_ref[0]
        pwr_sw = _swap_halves(pwr)
        q1 = jnp.zeros((CHUNK, 2 * S5_STATE), F32)
        q2 = jnp.zeros((CHUNK, 2 * S5_STATE), F32)
        r1_rows, r2_rows = [], []
        for hh in range(S5_CH):
            blk = dwst[hh * CHUNK:(hh + 1) * CHUNK, :]
            r1_rows.append(jnp.sum(blk * pwr, axis=0, keepdims=True))
            r2_rows.append(jnp.sum(blk * pwr_sw, axis=0, keepdims=True))
            bb = bbp_ref[0, hh:hh + 1, :]
            q1 = q1 + blk * bb
            q2 = q2 + blk * _swap_halves(bb)
        r12_ref[0] = jnp.concatenate(r1_rows + r2_rows, axis=0)
        q12_ref[0] = jnp.concatenate([q1, q2], axis=0)

        xy_ref[0], uv_ref[0] = _reduce_table_cotangent(_dot_tn(dy, h.astype(BF)), cnp_ref, pwf_ref, 1)
        r1 = jnp.sum(x * h, axis=0, keepdims=True)
        r2 = jnp.sum(x * _swap_halves(h), axis=0, keepdims=True)
        da_ref[0] = jnp.concatenate([r1, r2, jnp.zeros((6, 2 * S5_STATE), F32)], axis=0)
        lane = lax.broadcasted_iota(jnp.int32, (CHUNK, W), 1)
        srow = lax.broadcasted_iota(jnp.int32, (CHUNK, W), 0)
        keep = lane < S5_CH * (srow + 1)
        ur = _dot(u, flip_ref[...]).astype(BF)
        dk_rows = []
        for hh in range(S5_CH):
            dt_h = _dot_tn(ur[:, hh * CHUNK:(hh + 1) * CHUNK], dy)
            back = pltpu.roll(dt_h, S5_CH, 1, stride=S5_CH, stride_axis=0)
            dk_rows.append(jnp.sum(jnp.where(keep, back, 0.0), axis=0, keepdims=True))
        dk = jnp.concatenate(dk_rows, axis=0)
        dbk_ref[0] = _dot3(_dot, dk, w0_ref[...])
        xy0_ref[0], uv0_ref[0] = _reduce_table_cotangent(_dot3(_dot_tn, dk, bbp_ref[0]), cnp_ref, pwf_ref, 0)

    gb = S5_GROUP_BATCH
    spec_g = lambda a, b: pl.BlockSpec((gb, a, b), lambda g: (g, 0, 0))
    small_c = jax.ShapeDtypeStruct((G, 2 * S5_CH, 2 * S5_STATE), F32)
    small_p = jax.ShapeDtypeStruct((G, 2 * CHUNK, 2 * S5_STATE), F32)
    return pl.pallas_call(
        body, name="s5_core_bwd", grid=(G // gb,),
        in_specs=[spec_g(nc, W), spec_g(nc, W), spec_g(nc, 2 * S5_STATE), spec_g(S5_CH, 2 * S5_STATE),
                  spec_g(CHUNK, 2 * S5_STATE), spec_g(S5_CH, 2 * S5_STATE), spec_g(CHUNK + 8, 2 * S5_STATE),
                  spec_g(8, 2 * S5_STATE), spec_g(8, 2 * S5_STATE)],
        out_specs=[spec_g(nc, W), spec_g(S5_CH, 2 * S5_STATE), spec_g(2 * S5_CH, 2 * S5_STATE), spec_g(2 * CHUNK, 2 * S5_STATE),
                   spec_g(2 * S5_CH, 2 * S5_STATE), spec_g(2 * CHUNK, 2 * S5_STATE),
                   spec_g(2 * S5_CH, 2 * S5_STATE), spec_g(2 * CHUNK, 2 * S5_STATE), spec_g(8, 2 * S5_STATE)],
        out_shape=[jax.ShapeDtypeStruct((G, nc, W), BF), jax.ShapeDtypeStruct((G, S5_CH, 2 * S5_STATE), F32),
                   small_c, small_p, small_c, small_p, small_c, small_p, jax.ShapeDtypeStruct((G, 8, 2 * S5_STATE), F32)],
        scratch_shapes=[pltpu.VMEM((gb, W, W), BF), pltpu.VMEM((W, W), BF), pltpu.VMEM((gb, W, 2 * S5_STATE), BF),
                        pltpu.VMEM((gb, W, 2 * S5_STATE), BF), pltpu.VMEM((gb, W, 2 * S5_STATE), F32)],
        compiler_params=_params(("arbitrary",)),
    )(uflat, dyflat, hsave, bbp, pwr, cnp, pwf, p1, p2)


def _flat_hs(a, nc):
    return a.reshape(nc, CHUNK, S5_GROUPS, S5_CH).transpose(2, 0, 3, 1).reshape(S5_GROUPS, nc, CHUNK * S5_CH)


def _unflat_hs(a, nc):
    return a.reshape(S5_GROUPS, nc, S5_CH, CHUNK).transpose(1, 3, 0, 2).reshape(nc * CHUNK, D_MODEL)


def _flat_tk(a, nc):
    return a.reshape(nc, CHUNK, S5_GROUPS, S5_CH).transpose(2, 0, 1, 3).reshape(S5_GROUPS, nc, CHUNK * S5_CH)


def _unflat_tk(a, nc):
    return a.reshape(S5_GROUPS, nc, CHUNK, S5_CH).transpose(1, 2, 0, 3).reshape(nc * CHUNK, D_MODEL)


def _s5_post_fwd(yssm, u, zs, s5_d, w_glu, b_glu):
    L = yssm.shape[0]
    tl = min(TOKEN_TILE, L)

    def body(ys_ref, u_ref, z_ref, d_ref, wg_ref, bg_ref, o_ref):
        u = u_ref[...].astype(F32)
        a = _gelu(ys_ref[...].astype(F32) + d_ref[...] * u)
        y = a * _sigmoid(_dot(a.astype(BF), wg_ref[...]) + bg_ref[...])
        z = z_ref[...].astype(F32)
        o_ref[...] = (y * z * _sigmoid(z)).astype(BF)

    return pl.pallas_call(
        body, name="s5_post_fwd", grid=(L // tl,),
        in_specs=[_row_spec(tl, D_MODEL), _row_spec(tl, D_MODEL), _row_spec(tl, D_MODEL), _const_spec((1, D_MODEL)),
                  _const_spec((D_MODEL, D_MODEL)), _const_spec((1, D_MODEL))],
        out_specs=_row_spec(tl, D_MODEL),
        out_shape=jax.ShapeDtypeStruct((L, D_MODEL), BF),
        compiler_params=_params(("arbitrary",)),
    )(yssm, u, zs, s5_d, w_glu, b_glu)


def _s5_post_bwd(dys5, yssm, u, zs, s5_d, w_glu, b_glu):
    L = yssm.shape[0]
    tl = min(TOKEN_TILE, L)

    def body(dy_ref, ys_ref, u_ref, z_ref, d_ref, wg_ref, bg_ref, dz_ref, dys_ref, a_ref, dgl_ref, dbg_ref, dd_ref):
        @pl.when(pl.program_id(0) == 0)
        def _():
            dbg_ref[...] = jnp.zeros_like(dbg_ref)
            dd_ref[...] = jnp.zeros_like(dd_ref)

        u = u_ref[...].astype(F32)
        y0 = ys_ref[...].astype(F32) + d_ref[...] * u
        a = _gelu(y0)
        a_bf = a.astype(BF)
        sg = _sigmoid(_dot(a_bf, wg_ref[...]) + bg_ref[...])
        y = a * sg
        z = z_ref[...].astype(F32)
        sz = _sigmoid(z)
        dout = dy_ref[...].astype(F32)
        dz_ref[...] = (dout * y * sz * (1.0 + z * (1.0 - sz))).astype(BF)
        dyv = dout * z * sz
        dgl = dyv * a * sg * (1.0 - sg)
        dgl_bf = dgl.astype(BF)
        da = dyv * sg + _dot_nt(dgl_bf, wg_ref[...])
        dy0 = da * _gelu_grad(y0)
        dbg_ref[...] += jnp.sum(dgl, axis=0, keepdims=True)
        dd_ref[...] += jnp.sum(dy0 * u, axis=0, keepdims=True)
        dys_ref[...] = dy0.astype(BF)
        a_ref[...] = a_bf
        dgl_ref[...] = dgl_bf

    big = jax.ShapeDtypeStruct((L, D_MODEL), BF)
    vec = jax.ShapeDtypeStruct((1, D_MODEL), F32)
    return pl.pallas_call(
        body, name="s5_post_bwd", grid=(L // tl,),
        in_specs=[_row_spec(tl, D_MODEL), _row_spec(tl, D_MODEL), _row_spec(tl, D_MODEL), _row_spec(tl, D_MODEL),
                  _const_spec((1, D_MODEL)), _const_spec((D_MODEL, D_MODEL)), _const_spec((1, D_MODEL))],
        out_specs=[_row_spec(tl, D_MODEL)] * 4 + [_const_spec((1, D_MODEL))] * 2,
        out_shape=[big, big, big, big, vec, vec],
        compiler_params=_params(("arbitrary",)),
    )(dys5, yssm, u, zs, s5_d, w_glu, b_glu)


def _cumsum_rows(a):
    row = lax.broadcasted_iota(jnp.int32, a.shape, 0)
    d = 1
    while d < a.shape[0]:
        a = a + jnp.where(row >= d, pltpu.roll(a, d, 0), 0.0)
        d *= 2
    return a


def _rev_cumsum_rows(a):
    n = a.shape[0]
    row = lax.broadcasted_iota(jnp.int32, a.shape, 0)
    d = 1
    while d < n:
        a = a + jnp.where(row < n - d, pltpu.roll(a, n - d, 0), 0.0)
        d *= 2
    return a


def _ssd_fill_padded(first, xs_ref, bc_ref, hx_ref, hb_ref, xp_ref, tl):
    hal = jnp.concatenate([hx_ref[...], hb_ref[...]], axis=1)
    xp_ref[0:8, :] = jnp.where(first, 0.0, hal)
    xp_ref[8:8 + tl, 0:1024] = xs_ref[...]
    xp_ref[8:8 + tl, 1024:2048] = bc_ref[...]


def _ssd_conv_fwd(first, xs_ref, bc_ref, hx_ref, hb_ref, cw_ref, cb_ref, xp_ref, tl):
    _ssd_fill_padded(first, xs_ref, bc_ref, hx_ref, hb_ref, xp_ref, tl)
    pre = cb_ref[...] + cw_ref[0:1, :] * xp_ref[5:5 + tl, :]
    for k in range(1, 4):
        pre = pre + cw_ref[k:k + 1, :] * xp_ref[5 + k:5 + k + tl, :]
    return pre


def _onehot_lane(h):
    return (lax.broadcasted_iota(jnp.int32, (1, LANES), 1) == h).astype(F32)


def _dot_exact(x, e):
    hi = x.astype(BF)
    r = x - hi.astype(F32)
    mid = r.astype(BF)
    lo = (r - mid.astype(F32)).astype(BF)
    return _dot(hi, e) + _dot(mid, e) + _dot(lo, e)


def _head_expand_matrices():
    e = lax.broadcasted_iota(jnp.int32, (LANES, D_MODEL), 0) == (lax.broadcasted_iota(jnp.int32, (LANES, D_MODEL), 1) >> 6)
    et = (lax.broadcasted_iota(jnp.int32, (D_MODEL, LANES), 0) >> 6) == lax.broadcasted_iota(jnp.int32, (D_MODEL, LANES), 1)
    return e.astype(BF), et.astype(BF)


def _group_masks():
    r64 = lax.broadcasted_iota(jnp.int32, (4 * CHUNK, CHUNK), 0)
    causal4 = (r64 & (CHUNK - 1)) >= lax.broadcasted_iota(jnp.int32, (4 * CHUNK, CHUNK), 1)
    r256 = lax.broadcasted_iota(jnp.int32, (4 * CHUNK, 4 * SSD_HEAD_DIM), 0)
    same = (r256 >> 6) == (lax.broadcasted_iota(jnp.int32, (4 * CHUNK, 4 * SSD_HEAD_DIM), 1) >> 6)
    return causal4, same


def _group_decay(acs, acs_t, j, causal4):
    col = jnp.concatenate([acs[:, 4 * j + hh:4 * j + hh + 1] for hh in range(4)], axis=0)
    rowv = jnp.concatenate([jnp.broadcast_to(acs_t[4 * j + hh:4 * j + hh + 1, :], (CHUNK, CHUNK)) for hh in range(4)], axis=0)
    return jnp.where(causal4, jnp.exp(col - rowv), 0.0)


def _group_last_decay(acs_t, j):
    return jnp.concatenate([jnp.broadcast_to(jnp.exp(acs_t[4 * j + hh:4 * j + hh + 1, CHUNK - 1:CHUNK]), (SSD_HEAD_DIM, 1))
                            for hh in range(4)], axis=0)


def _fold_heads(r):
    return r[0:CHUNK] + r[CHUNK:2 * CHUNK] + r[2 * CHUNK:3 * CHUNK] + r[3 * CHUNK:4 * CHUNK]


def _ssd_specs_in(tl, nt, rev):
    t_of = (lambda i: nt - 1 - i) if rev else (lambda i: i)
    rows = lambda w, col: pl.BlockSpec((tl, w), lambda i: (t_of(i), col))
    halo = lambda col: pl.BlockSpec((8, 1024), lambda i: (jnp.maximum(t_of(i) * (tl // 8) - 1, 0), col))
    return t_of, rows, halo


def _ssd_fwd(proj, pdt, conv_w, conv_b, dt_bias, a_log, ssd_d, norm_w):
    L = proj.shape[0]
    tl = min(TOKEN_TILE, L)
    nt, ncl = L // tl, tl // CHUNK
    _, rows, halo = _ssd_specs_in(tl, nt, False)

    def body(xs_ref, bc_ref, hx_ref, hb_ref, dt_ref, z_ref, cw_ref, cb_ref, dtb_ref, al_ref, dd_ref, nw_ref,
             y_ref, ypre_ref, st_ref, pre_ref, xp_ref, xbc_ref, dts_ref, hst_ref):
        i = pl.program_id(0)

        @pl.when(i == 0)
        def _():
            hst_ref[...] = jnp.zeros_like(hst_ref)

        pre = _ssd_conv_fwd(i == 0, xs_ref, bc_ref, hx_ref, hb_ref, cw_ref, cb_ref, xp_ref, tl)
        pre_ref[...] = pre
        xbc_ref[...] = pre * _sigmoid(pre)
        dts_ref[...] = _softplus(dt_ref[...] + dtb_ref[...])
        a_neg = -jnp.exp(al_ref[...])
        e16, _ = _head_expand_matrices()
        causal4, same = _group_masks()
        dd_x = _dot_exact(jnp.broadcast_to(dd_ref[...], (8, LANES)), e16)[0:1, :]

        def chunk(c, carry):
            r0 = pl.multiple_of(c * CHUNK, CHUNK)
            xbc = xbc_ref[pl.ds(r0, CHUNK), :]
            dtc = dts_ref[pl.ds(r0, CHUNK), :]
            acs = _cumsum_rows(dtc * a_neg)
            acs_t = acs.T
            acs_x = _dot_exact(acs, e16)
            xs = xbc[:, 0:1024]
            xd = xs * _dot_exact(dtc, e16)
            xd_bf = xd.astype(BF)
            xdd = (xd * jnp.exp(acs_x[CHUNK - 1:CHUNK, :] - acs_x)).astype(BF)
            e_x = jnp.exp(acs_x)
            for j in range(SSD_GROUPS):
                sl = slice(256 * j, 256 * (j + 1))
                bj = xbc[:, 1024 + 128 * j:1024 + 128 * (j + 1)].astype(BF)
                cj = xbc[:, 1536 + 128 * j:1536 + 128 * (j + 1)].astype(BF)
                g = _dot_nt(cj, bj)
                hj = hst_ref[sl, :]
                zj = _dot_nt(cj, hj.astype(BF))
                sc = (jnp.concatenate([g] * 4, axis=0) * _group_decay(acs, acs_t, j, causal4)).astype(BF)
                yd = _fold_heads(jnp.where(same, _dot(sc, xd_bf[:, sl]), 0.0))
                ypre_ref[pl.ds(r0, CHUNK), sl] = yd + e_x[:, sl] * zj + dd_x[:, sl] * xs[:, sl]
                st_ref[c, sl, :] = hj
                hst_ref[sl, :] = _group_last_decay(acs_t, j) * hj + _dot_tn(xdd[:, sl], bj)
            return carry

        lax.fori_loop(0, ncl, chunk, 0, unroll=True)
        z = z_ref[...]
        gg = ypre_ref[...] * z * _sigmoid(z)
        for j in range(SSD_GROUPS):
            seg = gg[:, 256 * j:256 * (j + 1)]
            r = lax.rsqrt(jnp.mean(seg * seg, axis=-1, keepdims=True) + EPS)
            y_ref[:, 256 * j:256 * (j + 1)] = (seg * r * nw_ref[:, 256 * j:256 * (j + 1)]).astype(BF)

    nc = L // CHUNK
    return pl.pallas_call(
        body, name="ssd_fwd", grid=(nt,),
        in_specs=[rows(1024, 1), rows(1024, 2), halo(1), halo(2), rows(LANES, 0), rows(1024, 0),
                  _const_spec((4, 2048)), _const_spec((1, 2048)), _const_spec((1, LANES)), _const_spec((1, LANES)),
                  _const_spec((1, LANES)), _const_spec((1, D_MODEL))],
        out_specs=[_row_spec(tl, D_MODEL), _row_spec(tl, D_MODEL), pl.BlockSpec((ncl, 1024, SSD_STATE), lambda i: (i, 0, 0)),
                   _row_spec(tl, 2048)],
        out_shape=[jax.ShapeDtypeStruct((L, D_MODEL), BF), jax.ShapeDtypeStruct((L, D_MODEL), F32),
                   jax.ShapeDtypeStruct((nc, 1024, SSD_STATE), F32), jax.ShapeDtypeStruct((L, 2048), F32)],
        scratch_shapes=[pltpu.VMEM((tl + 8, 2048), F32), pltpu.VMEM((tl, 2048), F32), pltpu.VMEM((tl, LANES), F32),
                        pltpu.VMEM((1024, SSD_STATE), F32)],
        compiler_params=_params(("arbitrary",)),
    )(proj, proj, proj, proj, pdt, proj, conv_w, conv_b, dt_bias, a_log, ssd_d, norm_w)


def _ssd_bwd(dyssd, ypre, proj, pdt, states, pre_act, conv_w, dt_bias, a_log, ssd_d, norm_w):
    L = proj.shape[0]
    tl = min(TOKEN_TILE, L)
    nt, ncl = L // tl, tl // CHUNK
    t_of, rows, halo = _ssd_specs_in(tl, nt, True)

    def body(dy_ref, ypre_ref, z_ref, xs_ref, bc_ref, hx_ref, hb_ref, dt_ref, st_ref, pre_ref, cw_ref, dtb_ref, al_ref,
             dd_ref, nw_ref,
             dxbc_ref, ddt_ref, dz_ref, dcw_ref, dcb_ref, ddtb_ref, dal_ref, ddd_ref, dnw_ref,
             xp_ref, xbc_ref, dts_ref, dyp_ref, dxs_ref, ddts_ref, dp_ref, dh_ref):
        i = pl.program_id(0)

        @pl.when(i == 0)
        def _():
            for r in (dcw_ref, dcb_ref, ddtb_ref, dal_ref, ddd_ref, dnw_ref, dh_ref):
                r[...] = jnp.zeros_like(r)
            dp_ref[tl:tl + 8, :] = jnp.zeros((8, 2048), F32)

        _ssd_fill_padded(t_of(i) == 0, xs_ref, bc_ref, hx_ref, hb_ref, xp_ref, tl)
        pre = pre_ref[...]
        xbc_ref[...] = pre * _sigmoid(pre)
        dts_ref[...] = _softplus(dt_ref[...] + dtb_ref[...])
        a_neg = -jnp.exp(al_ref[...])

        ypre = ypre_ref[...]
        z = z_ref[...]
        sz = _sigmoid(z)
        gg = ypre * z * sz
        dout = dy_ref[...]
        for j in range(SSD_GROUPS):
            sl = slice(256 * j, 256 * (j + 1))
            seg = gg[:, sl]
            r = lax.rsqrt(jnp.mean(seg * seg, axis=-1, keepdims=True) + EPS)
            gh = seg * r
            dnw_ref[:, sl] += jnp.sum(dout[:, sl] * gh, axis=0, keepdims=True)
            gw = dout[:, sl] * nw_ref[:, sl]
            dgg = r * (gw - gh * jnp.mean(gw * gh, axis=-1, keepdims=True))
            dyp_ref[:, sl] = dgg * z[:, sl] * sz[:, sl]
            dz_ref[:, sl] = (dgg * ypre[:, sl] * sz[:, sl] * (1.0 + z[:, sl] * (1.0 - sz[:, sl]))).astype(BF)

        e16, e16t = _head_expand_matrices()
        causal4, same = _group_masks()
        dd_x = _dot_exact(jnp.broadcast_to(dd_ref[...], (8, LANES)), e16)[0:1, :]
        last_row = (lax.broadcasted_iota(jnp.int32, (CHUNK, 1), 0) == CHUNK - 1).astype(F32)
        sel_rows = lax.broadcasted_iota(jnp.int32, (4 * CHUNK, LANES), 0) >> 6
        sel_lanes = lax.broadcasted_iota(jnp.int32, (4 * CHUNK, LANES), 1)

        def chunk(k, carry):
            dal_acc, ddx_acc = carry
            c = ncl - 1 - k
            r0 = pl.multiple_of(c * CHUNK, CHUNK)
            xbc = xbc_ref[pl.ds(r0, CHUNK), :]
            dtc = dts_ref[pl.ds(r0, CHUNK), :]
            dyp = dyp_ref[pl.ds(r0, CHUNK), :]
            acs = _cumsum_rows(dtc * a_neg)
            acs_t = acs.T
            acs_x = _dot_exact(acs, e16)
            dt_x = _dot_exact(dtc, e16)
            xs = xbc[:, 0:1024]
            xd = xs * dt_x
            xd_bf = xd.astype(BF)
            dec_x = jnp.exp(acs_x[CHUNK - 1:CHUNK, :] - acs_x)
            xdd = xd * dec_x
            xdd_bf = xdd.astype(BF)
            dz = dyp * jnp.exp(acs_x)
            dz_bf = dz.astype(BF)
            ddx_acc = ddx_acc + jnp.sum(dyp * xs, axis=0, keepdims=True)
            dacs = jnp.zeros((CHUNK, LANES), F32)
            hsum = jnp.zeros((1, LANES), F32)
            p1_l, p2_l, p3_l = [], [], []
            for j in range(SSD_GROUPS):
                sl = slice(256 * j, 256 * (j + 1))
                bj = xbc[:, 1024 + 128 * j:1024 + 128 * (j + 1)].astype(BF)
                cj = xbc[:, 1536 + 128 * j:1536 + 128 * (j + 1)].astype(BF)
                g = _dot_nt(cj, bj)
                hj = st_ref[c, sl, :]
                hj_bf = hj.astype(BF)
                dhj = dh_ref[sl, :]
                dhj_bf = dhj.astype(BF)
                zj = _dot_nt(cj, hj_bf)
                qj = _dot_nt(bj, dhj_bf)
                lm = _group_decay(acs, acs_t, j, causal4)
                sc = jnp.concatenate([g] * 4, axis=0) * lm
                sc_bf = sc.astype(BF)
                dym = jnp.where(same, jnp.concatenate([dyp[:, sl]] * 4, axis=0), 0.0).astype(BF)
                dsc = _dot_nt(dym, xd_bf[:, sl])
                dxd = _dot_tn(sc_bf, dym) + qj * dec_x[:, sl]
                m = dsc * sc
                dg_bf = _fold_heads(dsc * lm).astype(BF)
                rs = jnp.sum(m, axis=1, keepdims=True)
                e2 = dhj * hj
                for hh in range(4):
                    oh = _onehot_lane(4 * j + hh)
                    dacs = dacs + oh * rs[CHUNK * hh:CHUNK * (hh + 1)]
                    hsum = hsum + oh * jnp.sum(jnp.sum(e2[64 * hh:64 * (hh + 1)], axis=0, keepdims=True), axis=1, keepdims=True)
                sel = (sel_rows + 4 * j == sel_lanes).astype(BF)
                hi = m.astype(BF)
                rem = m - hi.astype(F32)
                mid = rem.astype(BF)
                lo = (rem - mid.astype(F32)).astype(BF)
                dacs = dacs - (_dot_tn(hi, sel) + _dot_tn(mid, sel) + _dot_tn(lo, sel))
                p1_l.append(dz[:, sl] * zj)
                p2_l.append(qj * xdd[:, sl])
                p3_l.append(dxd * xs[:, sl])
                dxs_ref[pl.ds(r0, CHUNK), sl] = dd_x[:, sl] * dyp[:, sl] + dxd * dt_x[:, sl]
                dxs_ref[pl.ds(r0, CHUNK), 1536 + 128 * j:1536 + 128 * (j + 1)] = _dot(dg_bf, bj) + _dot(dz_bf[:, sl], hj_bf)
                dxs_ref[pl.ds(r0, CHUNK), 1024 + 128 * j:1024 + 128 * (j + 1)] = _dot_tn(dg_bf, cj) + _dot(xdd_bf[:, sl], dhj_bf)
                dh_ref[sl, :] = _group_last_decay(acs_t, j) * dhj + _dot_tn(dz_bf[:, sl], cj)
            stacked = jnp.concatenate([jnp.concatenate(p1_l, axis=1), jnp.concatenate(p2_l, axis=1), jnp.concatenate(p3_l, axis=1)], axis=0)
            red = _dot_exact(stacked, e16t)
            r1, r2, ddtc = red[0:CHUNK], red[CHUNK:2 * CHUNK], red[2 * CHUNK:3 * CHUNK]
            tot = jnp.sum(r2, axis=0, keepdims=True) + jnp.exp(acs[CHUNK - 1:CHUNK, :]) * hsum
            da = _rev_cumsum_rows(dacs + r1 - r2 + last_row * tot)
            ddts_ref[pl.ds(r0, CHUNK), :] = ddtc + da * a_neg
            dal_acc = dal_acc + jnp.sum(da * dtc, axis=0, keepdims=True)
            return dal_acc, ddx_acc

        carry = (jnp.zeros((1, LANES), F32), jnp.zeros((1, D_MODEL), F32))
        for k in range(ncl):
            carry = chunk(k, carry)
        dal_acc, ddx_acc = carry
        dal_ref[...] += dal_acc * a_neg
        ddd_ref[...] += _dot_exact(jnp.broadcast_to(ddx_acc, (8, D_MODEL)), e16t)[0:1, :]
        ddt_raw = ddts_ref[...] * _sigmoid(dt_ref[...] + dtb_ref[...])
        ddt_ref[...] = ddt_raw
        ddtb_ref[...] += jnp.sum(ddt_raw, axis=0, keepdims=True)

        pre = pre_ref[...]
        sp = _sigmoid(pre)
        dpre = dxs_ref[...] * sp * (1.0 + pre * (1.0 - sp))
        dp_ref[0:tl, :] = dpre
        dcb_ref[...] += jnp.sum(dpre, axis=0, keepdims=True)
        dx = jnp.zeros((tl, 2048), F32)
        for k in range(4):
            dcw_ref[k:k + 1, :] += jnp.sum(dpre * xp_ref[5 + k:5 + k + tl, :], axis=0, keepdims=True)
            dx = dx + cw_ref[k:k + 1, :] * dp_ref[3 - k:3 - k + tl, :]
        dxbc_ref[...] = dx.astype(BF)
        dp_ref[tl:tl + 8, :] = dp_ref[0:8, :]

    vec = lambda w: jax.ShapeDtypeStruct((1, w), F32)
    rrow = lambda w: pl.BlockSpec((tl, w), lambda i: (t_of(i), 0))
    return pl.pallas_call(
        body, name="ssd_bwd", grid=(nt,),
        in_specs=[rrow(D_MODEL), rrow(D_MODEL), rows(1024, 0), rows(1024, 1), rows(1024, 2), halo(1), halo(2), rows(LANES, 0),
                  pl.BlockSpec((ncl, 1024, SSD_STATE), lambda i: (t_of(i), 0, 0)), rrow(2048),
                  _const_spec((4, 2048)), _const_spec((1, LANES)), _const_spec((1, LANES)),
                  _const_spec((1, LANES)), _const_spec((1, D_MODEL))],
        out_specs=[rrow(2048), rrow(LANES), rrow(D_MODEL), _const_spec((8, 2048)), _const_spec((1, 2048)),
                   _const_spec((1, LANES)), _const_spec((1, LANES)), _const_spec((1, LANES)), _const_spec((1, D_MODEL))],
        out_shape=[jax.ShapeDtypeStruct((L, 2048), BF), jax.ShapeDtypeStruct((L, LANES), F32), jax.ShapeDtypeStruct((L, D_MODEL), BF),
                   jax.ShapeDtypeStruct((8, 2048), F32), vec(2048), vec(LANES), vec(LANES), vec(LANES), vec(D_MODEL)],
        scratch_shapes=[pltpu.VMEM((tl + 8, 2048), F32), pltpu.VMEM((tl, 2048), F32),
                        pltpu.VMEM((tl, LANES), F32), pltpu.VMEM((tl, D_MODEL), F32), pltpu.VMEM((tl, 2048), F32),
                        pltpu.VMEM((tl, LANES), F32), pltpu.VMEM((tl + 8, 2048), F32), pltpu.VMEM((1024, SSD_STATE), F32)],
        compiler_params=_params(("arbitrary",)),
    )(dyssd, ypre, proj, proj, proj, proj, proj, pdt, states, pre_act, conv_w, dt_bias, a_log, ssd_d, norm_w)


def _head_fwd_bwd(x, ys5, yssd, p, target, w_out, w_gate, w_proj, ple_nw, fin_nw):
    L = x.shape[0]
    tl = min(TOKEN_TILE, L)
    inv_d = 1.0 / D_MODEL

    def body(x_ref, ys_ref, yd_ref, p_ref, t_ref, wo_ref, wg_ref, wp_ref, pnw_ref, fnw_ref,
             loss_ref, dys_ref, dyd_ref, dh1_ref, n2_ref, dgl_ref, dpp_ref, dpnw_ref, dfnw_ref):
        @pl.when(pl.program_id(0) == 0)
        def _():
            loss_ref[...] = jnp.zeros_like(loss_ref)
            dpnw_ref[...] = jnp.zeros_like(dpnw_ref)
            dfnw_ref[...] = jnp.zeros_like(dfnw_ref)

        h1 = x_ref[...] + _dot(ys_ref[...], wo_ref[0:1024, :]) + _dot(yd_ref[...], wo_ref[1024:2048, :])
        r1 = lax.rsqrt(jnp.mean(h1 * h1, axis=-1, keepdims=True) + EPS)
        hh1 = h1 * r1
        n2 = (hh1 * pnw_ref[...]).astype(BF)
        gate = _sigmoid(_dot(n2, wg_ref[...]))
        pp = _dot(p_ref[...].astype(BF), wp_ref[...])
        h2 = h1 + pp * gate
        r2 = lax.rsqrt(jnp.mean(h2 * h2, axis=-1, keepdims=True) + EPS)
        hh2 = h2 * r2
        err = hh2 * fnw_ref[...] - t_ref[...]
        loss_ref[...] += 0.5 * inv_d * jnp.sum(err * err)
        dyo = err * inv_d
        dfnw_ref[...] += jnp.sum(dyo * hh2, axis=0, keepdims=True)
        g2 = dyo * fnw_ref[...]
        dh2 = r2 * (g2 - hh2 * jnp.mean(g2 * hh2, axis=-1, keepdims=True))
        dpp_ref[...] = (dh2 * gate).astype(BF)
        dgl = (dh2 * pp * gate * (1.0 - gate)).astype(BF)
        dgl_ref[...] = dgl
        n2_ref[...] = n2
        dn2 = _dot_nt(dgl, wg_ref[...])
        dpnw_ref[...] += jnp.sum(dn2 * hh1, axis=0, keepdims=True)
        g1 = dn2 * pnw_ref[...]
        dh1 = dh2 + r1 * (g1 - hh1 * jnp.mean(g1 * hh1, axis=-1, keepdims=True))
        dh1_ref[...] = dh1
        dh1_bf = dh1.astype(BF)
        dys_ref[...] = _dot_nt(dh1_bf, wo_ref[0:1024, :]).astype(BF)
        dyd_ref[...] = _dot_nt(dh1_bf, wo_ref[1024:2048, :])

    big = jax.ShapeDtypeStruct((L, D_MODEL), BF)
    vec = jax.ShapeDtypeStruct((1, D_MODEL), F32)
    return pl.pallas_call(
        body, name="head_fwd_bwd", grid=(L // tl,),
        in_specs=[_row_spec(tl, D_MODEL), _row_spec(tl, D_MODEL), _row_spec(tl, D_MODEL), _row_spec(tl, 256), _row_spec(tl, D_MODEL),
                  _const_spec((2048, D_MODEL)), _const_spec((D_MODEL, D_MODEL)), _const_spec((256, D_MODEL)),
                  _const_spec((1, D_MODEL)), _const_spec((1, D_MODEL))],
        out_specs=[_const_spec((8, LANES)), _row_spec(tl, D_MODEL), _row_spec(tl, D_MODEL), _row_spec(tl, D_MODEL),
                   _row_spec(tl, D_MODEL), _row_spec(tl, D_MODEL), _row_spec(tl, D_MODEL), _const_spec((1, D_MODEL)), _const_spec((1, D_MODEL))],
        out_shape=[jax.ShapeDtypeStruct((8, LANES), F32), big, jax.ShapeDtypeStruct((L, D_MODEL), F32),
                   jax.ShapeDtypeStruct((L, D_MODEL), F32), big, big, big, vec, vec],
        compiler_params=_params(("arbitrary",)),
    )(x, ys5, yssd, p, target, w_out, w_gate, w_proj, ple_nw, fin_nw)


def _pad_lanes(v):
    return jnp.pad(v.reshape(1, -1), ((0, 0), (0, LANES - v.size)))


def _local_step(x, p, target, w):
    L = x.shape[0]
    nc = L // CHUNK
    nsteps = max(1, (nc - 1).bit_length())
    w_in_t = w["w_in"]
    w_main = w_in_t[:D_MAIN]
    w_dt = jnp.pad(w_in_t[D_MAIN:], ((0, LANES - SSD_HEADS), (0, 0)))
    norm_w = w["norm_w"].reshape(1, -1)
    s5_d = w["s5_D"].reshape(1, -1)
    b_glu = w["s5_b_glu"].reshape(1, -1)
    conv_b = w["conv_b"].reshape(1, -1)
    dtb, alog, ssd_d = _pad_lanes(w["dt_bias"]), _pad_lanes(w["A_log"]), _pad_lanes(w["ssd_D"])
    ssd_nw = w["ssd_norm_w"].reshape(1, -1)
    ple_nw = w["ple_norm_w"].reshape(1, -1)
    fin_nw = w["final_norm_w"].reshape(1, -1)

    s5_args = (w["s5_A_re"], w["s5_A_im"], w["s5_log_dt"], w["s5_B_re"], w["s5_B_im"], w["s5_C_re"], w["s5_C_im"])
    small, small_vjp = jax.vjp(_s5_discretise, *s5_args)
    bbp, pwr, cnp, pwf = _s5_table_factors(*small)
    p1, p2 = _s5_scan_powers(w["s5_A_re"], w["s5_A_im"], w["s5_log_dt"], nsteps)

    hn, u_s5, z_s5, pssd, pdt = _in_proj_fwd(x, norm_w, w_main, w_dt)
    uflat = _flat_hs(u_s5, nc)
    yflat, hsave = _s5_core_fwd(uflat, bbp, pwr, cnp, pwf, p1, p2)
    yssm = _unflat_tk(yflat, nc)
    ys5 = _s5_post_fwd(yssm, u_s5, z_s5, s5_d, w["s5_w_glu"], b_glu)
    yssd, ypre, states, pre_act = _ssd_fwd(pssd, pdt, w["conv_w"], conv_b, dtb, alog, ssd_d, ssd_nw)
    (loss8, dys5, dyssd, dh1, n2, dgl2, dpp, g_ple_nw, g_fin_nw) = _head_fwd_bwd(
        x, ys5, yssd, p, target, w["w_out"], w["w_ple_gate"], w["w_ple_proj"], ple_nw, fin_nw)

    (dxbc, ddt, dzd, g_cw, g_cb, g_dtb, g_alog, g_ssd_d, g_ssd_nw) = _ssd_bwd(
        dyssd, ypre, pssd, pdt, states, pre_act, w["conv_w"], dtb, alog, ssd_d, ssd_nw)
    dzs, dyssm, a_glu, dgl1, g_bglu, g_s5d = _s5_post_bwd(dys5, yssm, u_s5, z_s5, s5_d, w["s5_w_glu"], b_glu)
    duflat, dbk, r12, q12, xy, uv, xy0, uv0, da8 = _s5_core_bwd(uflat, _flat_tk(dyssm, nc), hsave, bbp, pwr, cnp, pwf, p1, p2)
    da64 = jnp.concatenate([da8[:, 0, :S5_STATE] + da8[:, 0, S5_STATE:], da8[:, 1, S5_STATE:] - da8[:, 1, :S5_STATE]], axis=-1)
    g_s5 = small_vjp(_s5_small_cotangents(dbk, r12, q12, xy, uv, xy0, uv0, da64))
    gx, du, g_norm_w = _in_proj_bwd(x, norm_w, dh1, _unflat_hs(duflat, nc), dyssm, s5_d, dzs, dzd, dxbc, ddt, w_main, w_dt)

    g_w_in = jnp.concatenate([
        _matmul_tn(du, hn, "dw_in_u"), _matmul_tn(dzs, hn, "dw_in_zs"), _matmul_tn(dzd, hn, "dw_in_zd"),
        _matmul_tn(dxbc, hn, "dw_in_xbc"), _matmul_tn(ddt, hn, "dw_in_dt")[:SSD_HEADS]], axis=0)
    grads = {
        "norm_w": g_norm_w, "w_in": g_w_in,
        "s5_A_re": g_s5[0], "s5_A_im": g_s5[1], "s5_log_dt": g_s5[2], "s5_B_re": g_s5[3], "s5_B_im": g_s5[4],
        "s5_C_re": g_s5[5], "s5_C_im": g_s5[6], "s5_D": g_s5d, "s5_w_glu": _matmul_tn(a_glu, dgl1, "dw_glu"), "s5_b_glu": g_bglu,
        "conv_w": g_cw[:4], "conv_b": g_cb, "dt_bias": g_dtb[:, :SSD_HEADS], "A_log": g_alog[:, :SSD_HEADS],
        "ssd_D": g_ssd_d[:, :SSD_HEADS], "ssd_norm_w": g_ssd_nw,
        "w_out": jnp.concatenate([_matmul_tn(ys5, dh1, "dw_out_s5"), _matmul_tn(yssd, dh1, "dw_out_ssd")], axis=0),
        "ple_norm_w": g_ple_nw, "w_ple_gate": _matmul_tn(n2, dgl2, "dw_gate"), "w_ple_proj": _matmul_tn(p, dpp, "dw_proj"),
        "final_norm_w": g_fin_nw,
    }
    return loss8[0, 0], gx, grads


WEIGHTS = ("norm_w", "w_in", "s5_A_re", "s5_A_im", "s5_log_dt", "s5_B_re", "s5_B_im", "s5_C_re", "s5_C_im", "s5_D", "s5_w_glu",
           "s5_b_glu", "conv_w", "conv_b", "dt_bias", "A_log", "ssd_D", "ssd_norm_w", "w_out", "ple_norm_w", "w_ple_gate",
           "w_ple_proj", "final_norm_w")
BIG = {"w_in": ((1284, 1024), 0), "s5_w_glu": ((256, 1024), 0), "w_out": ((512, 1024), 0), "w_ple_gate": ((256, 1024), 0),
       "w_ple_proj": ((256, 256), 1)}
SMALL = {"norm_w": (1024,), "s5_A_re": (64, 64), "s5_A_im": (64, 64), "s5_log_dt": (64,), "s5_B_re": (64, 64, 16),
         "s5_B_im": (64, 64, 16), "s5_C_re": (64, 16, 64), "s5_C_im": (64, 16, 64), "s5_D": (1024,), "s5_b_glu": (1024,),
         "conv_w": (4, 2048), "conv_b": (2048,), "dt_bias": (16,), "A_log": (16,), "ssd_D": (16,), "ssd_norm_w": (1024,),
         "ple_norm_w": (1024,), "final_norm_w": (1024,)}
BIG_ROWS = {n: s[0] * s[1] // LANES for n, (s, _) in BIG.items()}
BIG_ROWS_TOTAL = sum(BIG_ROWS.values())
SMALL_TOTAL = sum(math.prod(s) for s in SMALL.values())
SMALL_PIECE_ROWS = -(-SMALL_TOTAL // (N_CHIPS * 16 * LANES)) * 16
HALF_ROWS = (BIG_ROWS_TOTAL + SMALL_PIECE_ROWS) // 2
SMALL_ROW0 = BIG_ROWS_TOTAL - HALF_ROWS


def _mesh_pos():
    return lax.axis_index("x"), lax.axis_index("y"), lax.axis_index("c")


def _other_chips(x, y):
    return [(1 - x, y), (x, 1 - y), (1 - x, 1 - y)]


def _comm_params():
    return pltpu.CompilerParams(has_side_effects=True)


def _all_gather_chips(wpack, cw):
    half = wpack.shape[0] // 2

    def body(w_ref, c_ref, wo_ref, co_ref, send_sems, recv_sems, fwd_send, fwd_recv, loc_sems):
        x, y, c = _mesh_pos()
        me = 2 * x + y
        sib = (x, y, 1 - c)
        mine = pl.ds(c * half, half)
        theirs = pl.ds((1 - c) * half, half)
        others = _other_chips(x, y)
        loc = [pltpu.make_async_copy(c_ref, co_ref.at[me], loc_sems.at[0])]
        for cp in loc:
            cp.start()

        def from_chip(k, chip, dev):
            return pltpu.make_async_remote_copy(w_ref.at[mine], wo_ref.at[chip, mine], send_sems.at[2 * k], recv_sems.at[2 * k],
                                                device_id=dev, device_id_type=MESH)

        def conv_from(k, chip, dev):
            return pltpu.make_async_remote_copy(c_ref, co_ref.at[chip], send_sems.at[2 * k + 1], recv_sems.at[2 * k + 1],
                                                device_id=dev, device_id_type=MESH)

        def passed(k, chip, rows):
            return pltpu.make_async_remote_copy(wo_ref.at[chip, rows], wo_ref.at[chip, rows], fwd_send.at[k], fwd_recv.at[k],
                                                device_id=sib, device_id_type=MESH)

        sends = []
        for k, (px, py) in enumerate(others):
            sends += [from_chip(k, me, (px, py, c)), conv_from(k, me, (px, py, c))]
        for cp in sends:
            cp.start()
        fwds = []
        for k, (px, py) in enumerate(others):
            chip = 2 * px + py
            from_chip(k, chip, (px, py, c)).wait_recv()
            fwds.append(passed(k, chip, mine))
            fwds[-1].start()
        for k, (px, py) in enumerate(others):
            chip = 2 * px + py
            passed(k, chip, theirs).wait_recv()
            conv_from(k, chip, (px, py, c)).wait_recv()
        for cp in sends + fwds:
            cp.wait_send()
        for cp in loc:
            cp.wait()

    return pl.pallas_call(
        body, name="all_gather_weights", in_specs=[ANY, ANY], out_specs=[ANY, ANY],
        out_shape=[jax.ShapeDtypeStruct((N_CHIPS,) + wpack.shape, wpack.dtype), jax.ShapeDtypeStruct((N_CHIPS,) + cw.shape, cw.dtype)],
        scratch_shapes=[pltpu.SemaphoreType.DMA((6,)), pltpu.SemaphoreType.DMA((6,)), pltpu.SemaphoreType.DMA((3,)),
                        pltpu.SemaphoreType.DMA((3,)), pltpu.SemaphoreType.DMA((1,))],
        compiler_params=_comm_params(),
    )(wpack, cw)


def _exchange_pair(gp):
    def body(g_ref, r_ref, send_sems, recv_sems):
        x, y, c = _mesh_pos()
        cps = [pltpu.make_async_remote_copy(g_ref.at[s, 1 - c], r_ref.at[s], send_sems.at[s], recv_sems.at[s],
                                            device_id=(x, y, 1 - c), device_id_type=MESH) for s in range(N_CHIPS)]
        for cp in cps:
            cp.start()
        for cp in cps:
            cp.wait()

    return pl.pallas_call(
        body, name="grad_exchange_pair", in_specs=[ANY], out_specs=ANY,
        out_shape=jax.ShapeDtypeStruct((N_CHIPS,) + gp.shape[2:], gp.dtype),
        scratch_shapes=[pltpu.SemaphoreType.DMA((N_CHIPS,)), pltpu.SemaphoreType.DMA((N_CHIPS,))],
        compiler_params=_comm_params(),
    )(gp)


def _pair_sum(mine, from_sibling):
    def body(a_ref, b_ref, bf_ref, tail_ref):
        s = a_ref[0] + b_ref[0]
        bf_ref[0] = s.astype(BF)
        tail_ref[0] = s[SMALL_ROW0:, :]

    piece = pl.BlockSpec((1, HALF_ROWS, LANES), lambda i: (i, 0, 0))
    return pl.pallas_call(
        body, name="grad_pair_sum", grid=(N_CHIPS,), in_specs=[piece, piece],
        out_specs=[piece, pl.BlockSpec((1, SMALL_PIECE_ROWS, LANES), lambda i: (i, 0, 0))],
        out_shape=[jax.ShapeDtypeStruct((N_CHIPS, HALF_ROWS, LANES), BF), jax.ShapeDtypeStruct((N_CHIPS, SMALL_PIECE_ROWS, LANES), F32)],
        compiler_params=_params(("parallel",)),
    )(mine, from_sibling)


def _exchange_chips(ps_bf, ps_tail):
    def body(p_ref, t_ref, r_ref, rt_ref, send_sems, recv_sems):
        x, y, c = _mesh_pos()
        cps = []
        for k, (px, py) in enumerate(_other_chips(x, y)):
            cps.append(pltpu.make_async_remote_copy(p_ref.at[2 * px + py], r_ref.at[k], send_sems.at[2 * k], recv_sems.at[2 * k],
                                                    device_id=(px, py, c), device_id_type=MESH))
            cps.append(pltpu.make_async_remote_copy(t_ref.at[2 * px + py], rt_ref.at[k], send_sems.at[2 * k + 1],
                                                    recv_sems.at[2 * k + 1], device_id=(px, py, c), device_id_type=MESH))
        for cp in cps:
            cp.start()
        for cp in cps:
            cp.wait()

    return pl.pallas_call(
        body, name="grad_exchange_chips", in_specs=[ANY, ANY], out_specs=[ANY, ANY],
        out_shape=[jax.ShapeDtypeStruct((N_CHIPS - 1,) + ps_bf.shape[1:], ps_bf.dtype),
                   jax.ShapeDtypeStruct((N_CHIPS - 1,) + ps_tail.shape[1:], ps_tail.dtype)],
        scratch_shapes=[pltpu.SemaphoreType.DMA((6,)), pltpu.SemaphoreType.DMA((6,))],
        compiler_params=_comm_params(),
    )(ps_bf, ps_tail)


def _chip_sum(own_bf, own_tail, others_bf, others_tail):
    def body(ob_ref, ot_ref, b_ref, t_ref, o_ref):
        acc = ob_ref[0:SMALL_ROW0, :].astype(F32)
        tail = ot_ref[...]
        for k in range(N_CHIPS - 1):
            acc = acc + b_ref[k, 0:SMALL_ROW0, :].astype(F32)
            tail = tail + t_ref[k]
        o_ref[0:SMALL_ROW0, :] = acc
        o_ref[SMALL_ROW0:, :] = tail

    return pl.pallas_call(
        body, name="grad_chip_sum", out_shape=jax.ShapeDtypeStruct((HALF_ROWS, LANES), F32),
        compiler_params=_params(),
    )(own_bf, own_tail, others_bf, others_tail)


def _swap_reduced_halves(gh):
    def body(g_ref, o_ref, send_sem, recv_sem):
        x, y, c = _mesh_pos()
        cp = pltpu.make_async_remote_copy(g_ref, o_ref, send_sem, recv_sem, device_id=(x, y, 1 - c), device_id_type=MESH)
        cp.start()
        cp.wait()

    return pl.pallas_call(
        body, name="grad_swap_halves", in_specs=[ANY], out_specs=ANY,
        out_shape=jax.ShapeDtypeStruct(gh.shape, gh.dtype),
        scratch_shapes=[pltpu.SemaphoreType.DMA, pltpu.SemaphoreType.DMA],
        compiler_params=_comm_params(),
    )(gh)


def _gather_small(second_half):
    def body(gs_ref, sm_ref, send_sems, recv_sems, loc_sem):
        x, y, c = _mesh_pos()
        me = 2 * x + y
        small = gs_ref.at[pl.ds(SMALL_ROW0, SMALL_PIECE_ROWS)]
        loc = pltpu.make_async_copy(small, sm_ref.at[me], loc_sem)
        loc.start()
        cps = [pltpu.make_async_remote_copy(small, sm_ref.at[me], send_sems.at[k], recv_sems.at[k],
                                            device_id=(px, py, c), device_id_type=MESH)
               for k, (px, py) in enumerate(_other_chips(x, y))]
        for cp in cps:
            cp.start()
        for cp in cps:
            cp.wait()
        loc.wait()

    return pl.pallas_call(
        body, name="grad_gather_small", in_specs=[ANY], out_specs=ANY,
        out_shape=jax.ShapeDtypeStruct((N_CHIPS, SMALL_PIECE_ROWS, LANES), second_half.dtype),
        scratch_shapes=[pltpu.SemaphoreType.DMA((3,)), pltpu.SemaphoreType.DMA((3,)), pltpu.SemaphoreType.DMA],
        compiler_params=_comm_params(),
    )(second_half)


def _pack_grads(grads):
    small = jnp.concatenate([grads[n].reshape(-1) for n in SMALL])
    small = jnp.pad(small, (0, N_CHIPS * SMALL_PIECE_ROWS * LANES - SMALL_TOTAL)).reshape(N_CHIPS, SMALL_PIECE_ROWS, LANES)
    pieces = []
    for s in range(N_CHIPS):
        rows = []
        for n, (shp, axis) in BIG.items():
            g = grads[n]
            blk = g[s * shp[0]:(s + 1) * shp[0], :] if axis == 0 else g[:, s * shp[1]:(s + 1) * shp[1]]
            rows.append(blk.reshape(-1, LANES))
        rows.append(small[s])
        pieces.append(jnp.concatenate(rows, axis=0).reshape(2, HALF_ROWS, LANES))
    return jnp.stack(pieces)


def _unpack_shard(first_half, second_half):
    rows = jnp.concatenate([first_half, second_half], axis=0)
    out, r0 = {}, 0
    for n, (shp, _) in BIG.items():
        out[n] = rows[r0:r0 + BIG_ROWS[n]].reshape(shp)
        r0 += BIG_ROWS[n]
    return out


def _unpack_small(sm):
    flat = sm.reshape(-1)
    out, o = {}, 0
    for n, shp in SMALL.items():
        k = math.prod(shp)
        out[n] = flat[o:o + k].reshape(shp)
        o += k
    return out


def _as_2d(a):
    n = a.size
    if a.ndim >= 2 and a.shape[-1] > 1024:
        return a.reshape(-1, a.shape[-1])
    if n % 1024 == 0:
        return a.reshape(n // 1024, 1024)
    return a.reshape(1, n)


def _adamw(w, g, m, v, name):
    shape = w.shape
    w2, g2, m2, v2 = (_as_2d(a) for a in (w, g, m, v))
    rows, cols = w2.shape
    rb = 256 if rows >= 512 else rows
    by_cols = rows % rb != 0

    def body(w_ref, g_ref, m_ref, v_ref, d_ref, mo_ref, vo_ref):
        gv = g_ref[...]
        mn = ADAM_B1 * m_ref[...] + (1.0 - ADAM_B1) * gv
        vn = ADAM_B2 * v_ref[...] + (1.0 - ADAM_B2) * (gv * gv)
        m_hat = mn / (1.0 - ADAM_B1 ** ADAM_STEP)
        v_hat = vn / (1.0 - ADAM_B2 ** ADAM_STEP)
        d_ref[...] = -ADAM_LR * (m_hat / (jnp.sqrt(v_hat) + ADAM_EPS) + ADAM_WD * w_ref[...])
        mo_ref[...] = mn
        vo_ref[...] = vn

    spec = pl.BlockSpec((rows, 256), lambda i: (0, i)) if by_cols else _row_spec(rb, cols)
    sds = jax.ShapeDtypeStruct((rows, cols), F32)
    d, mo, vo = pl.pallas_call(
        body, name=name, grid=(cols // 256 if by_cols else rows // rb,), in_specs=[spec] * 4, out_specs=[spec] * 3, out_shape=[sds] * 3,
        compiler_params=_params(("parallel",)),
    )(w2, g2, m2, v2)
    return d.reshape(shape), mo.reshape(shape), vo.reshape(shape)


def kernel(x, p, norm_w, w_in, s5_A_re, s5_A_im, s5_log_dt, s5_B_re, s5_B_im, s5_C_re, s5_C_im, s5_D, s5_w_glu, s5_b_glu, conv_w, conv_b, dt_bias, A_log, ssd_D, ssd_norm_w, w_out, ple_norm_w, w_ple_gate, w_ple_proj, final_norm_w, loss_target, m_norm_w, m_w_in, m_s5_A_re, m_s5_A_im, m_s5_log_dt, m_s5_B_re, m_s5_B_im, m_s5_C_re, m_s5_C_im, m_s5_D, m_s5_w_glu, m_s5_b_glu, m_conv_w, m_conv_b, m_dt_bias, m_A_log, m_ssd_D, m_ssd_norm_w, m_w_out, m_ple_norm_w, m_w_ple_gate, m_w_ple_proj, m_final_norm_w, v_norm_w, v_w_in, v_s5_A_re, v_s5_A_im, v_s5_log_dt, v_s5_B_re, v_s5_B_im, v_s5_C_re, v_s5_C_im, v_s5_D, v_s5_w_glu, v_s5_b_glu, v_conv_w, v_conv_b, v_dt_bias, v_A_log, v_ssd_D, v_ssd_norm_w, v_w_out, v_ple_norm_w, v_w_ple_gate, v_w_ple_proj, v_final_norm_w):
    given = (norm_w, w_in, s5_A_re, s5_A_im, s5_log_dt, s5_B_re, s5_B_im, s5_C_re, s5_C_im, s5_D, s5_w_glu, s5_b_glu, conv_w, conv_b,
             dt_bias, A_log, ssd_D, ssd_norm_w, w_out, ple_norm_w, w_ple_gate, w_ple_proj, final_norm_w)
    given_m = (m_norm_w, m_w_in, m_s5_A_re, m_s5_A_im, m_s5_log_dt, m_s5_B_re, m_s5_B_im, m_s5_C_re, m_s5_C_im, m_s5_D, m_s5_w_glu,
               m_s5_b_glu, m_conv_w, m_conv_b, m_dt_bias, m_A_log, m_ssd_D, m_ssd_norm_w, m_w_out, m_ple_norm_w, m_w_ple_gate,
               m_w_ple_proj, m_final_norm_w)
    given_v = (v_norm_w, v_w_in, v_s5_A_re, v_s5_A_im, v_s5_log_dt, v_s5_B_re, v_s5_B_im, v_s5_C_re, v_s5_C_im, v_s5_D, v_s5_w_glu,
               v_s5_b_glu, v_conv_w, v_conv_b, v_dt_bias, v_A_log, v_ssd_D, v_ssd_norm_w, v_w_out, v_ple_norm_w, v_w_ple_gate,
               v_w_ple_proj, v_final_norm_w)
    wts, mom, var = dict(zip(WEIGHTS, given)), dict(zip(WEIGHTS, given_m)), dict(zip(WEIGHTS, given_v))
    drop = lambda n, a: a if n == "final_norm_w" else a[0]

    shard2d = lambda n, a: a[0].T if n == "w_in" else drop(n, a)
    wpack = jnp.concatenate([shard2d(n, wts[n]).astype(BF).reshape(-1, LANES) for n in BIG], axis=0)
    wall, cwall = _all_gather_chips(wpack, drop("conv_w", wts["conv_w"]))
    chip = 2 * lax.axis_index("x") + lax.axis_index("y")
    full, r0 = {}, 0
    for n, (shp, axis) in BIG.items():
        blk = lax.dynamic_update_slice_in_dim(wall[:, r0:r0 + BIG_ROWS[n]].reshape((N_CHIPS,) + shp),
                                              shard2d(n, wts[n]).astype(BF)[None], chip, axis=0)
        full[n] = blk.reshape(N_CHIPS * shp[0], shp[1]) if axis == 0 else blk.transpose(1, 0, 2).reshape(shp[0], N_CHIPS * shp[1])
        r0 += BIG_ROWS[n]
    for n in SMALL:
        full[n] = drop(n, wts[n])
    full["conv_w"] = cwall.transpose(1, 0, 2).reshape(4, 2048)

    loss, gx, grads = _local_step(x[0], p[0, 0], loss_target[0], full)
    loss = lax.psum(loss, MESH_AXES)

    gp = _pack_grads({n: grads[n].reshape(SMALL[n]) if n in SMALL else grads[n] for n in WEIGHTS})
    c = lax.axis_index("c")
    from_sibling = _exchange_pair(gp)
    mine = lax.dynamic_index_in_dim(gp, c, axis=1, keepdims=False)
    ps_bf, ps_tail = _pair_sum(mine, from_sibling)
    others_bf, others_tail = _exchange_chips(ps_bf, ps_tail)
    own_bf = lax.dynamic_index_in_dim(ps_bf, chip, axis=0, keepdims=False)
    own_tail = lax.dynamic_index_in_dim(ps_tail, chip, axis=0, keepdims=False)
    reduced_half = _chip_sum(own_bf, own_tail, others_bf, others_tail)
    sibling_half = _swap_reduced_halves(reduced_half)
    first_half = jnp.where(c == 0, reduced_half, sibling_half)
    second_half = jnp.where(c == 0, sibling_half, reduced_half)
    sm = _gather_small(second_half)
    g_final = {**_unpack_small(sm), **_unpack_shard(first_half, second_half)}
    g_final["conv_w"] = lax.dynamic_slice_in_dim(g_final["conv_w"], chip * 512, 512, axis=1)

    outs_g, outs_d, outs_m, outs_v = [], [], [], []
    for n in WEIGHTS:
        if n == "w_in":
            res = _adamw(wts[n][0].T, g_final[n], mom[n][0].T, var[n][0].T, "adamw_" + n)
            g, d, mo, vo = (a.T[None] for a in (g_final[n],) + res)
        else:
            g = g_final[n].reshape(wts[n].shape)
            d, mo, vo = _adamw(wts[n], g, mom[n], var[n], "adamw_" + n)
        outs_g.append(g)
        outs_d.append(d)
        outs_m.append(mo)
        outs_v.append(vo)
    return (loss, gx[None], *outs_g, *outs_d, *outs_m, *outs_v)
```

```python
import functools
import math

import jax
import jax.numpy as jnp
from jax import lax
from jax.experimental import pallas as pl
from jax.experimental.pallas import tpu as pltpu

F32 = jnp.float32
BF = jnp.bfloat16
EPS = 1e-6
CHUNK = 64
D_MODEL = 1024
S5_GROUPS = 64
S5_CH = 16
S5_STATE = 64
SSD_HEADS = 16
SSD_HEAD_DIM = 64
SSD_GROUPS = 4
SSD_STATE = 128
D_MAIN = 5120
LANES = 128
TOKEN_TILE = 256
VMEM_LIMIT = 56 * 1024 * 1024
MESH_AXES = ("x", "y", "c")
N_CHIPS = 4
ADAM_LR, ADAM_B1, ADAM_B2, ADAM_EPS, ADAM_WD, ADAM_STEP = 0.001, 0.9, 0.999, 1e-08, 0.01, 10
MESH = pl.DeviceIdType.MESH
ANY = pl.BlockSpec(memory_space=pl.ANY)


def _dot(a, b):
    return jnp.dot(a, b, preferred_element_type=F32)


def _dot_nt(a, b):
    return lax.dot_general(a, b, (((1,), (1,)), ((), ())), preferred_element_type=F32)


def _dot_tn(a, b):
    return lax.dot_general(a, b, (((0,), (0,)), ((), ())), preferred_element_type=F32)


def _sigmoid(x):
    return 1.0 / (1.0 + jnp.exp(-x))


def _softplus(x):
    return jnp.maximum(x, 0.0) + jnp.log(1.0 + jnp.exp(-jnp.abs(x)))


_GELU_C = math.sqrt(2.0 / math.pi)


def _gelu(x):
    return 0.5 * x * (1.0 + jnp.tanh(_GELU_C * (x + 0.044715 * x * x * x)))


def _gelu_grad(x):
    th = jnp.tanh(_GELU_C * (x + 0.044715 * x * x * x))
    return 0.5 * (1.0 + th) + 0.5 * x * (1.0 - th * th) * _GELU_C * (1.0 + 3.0 * 0.044715 * x * x)


def _params(sem=None):
    return pltpu.CompilerParams(dimension_semantics=sem, vmem_limit_bytes=VMEM_LIMIT)


def _row_spec(tl, width, col=0):
    return pl.BlockSpec((tl, width), lambda i, col=col: (i, col))


def _const_spec(shape):
    nd = len(shape)
    return pl.BlockSpec(shape, lambda *_: (0,) * nd)


def _in_proj_fwd(x, norm_w, w_main, w_dt):
    L = x.shape[0]
    tl = min(TOKEN_TILE, L)

    def body(x_ref, nw_ref, wm_ref, wd_ref, hn_ref, u_ref, zs_ref, pssd_ref, pd_ref):
        xv = x_ref[...]
        r = lax.rsqrt(jnp.mean(xv * xv, axis=-1, keepdims=True) + EPS)
        hn = (xv * r * nw_ref[...]).astype(BF)
        hn_ref[...] = hn
        for j, o_ref in enumerate((u_ref, zs_ref)):
            o_ref[...] = _dot_nt(hn, wm_ref[j * 1024:(j + 1) * 1024, :]).astype(BF)
        for j in range(3):
            pssd_ref[:, j * 1024:(j + 1) * 1024] = _dot_nt(hn, wm_ref[(j + 2) * 1024:(j + 3) * 1024, :])
        pd_ref[...] = _dot_nt(hn, wd_ref[...])

    return pl.pallas_call(
        body, name="in_proj_fwd", grid=(L // tl,),
        in_specs=[_row_spec(tl, D_MODEL), _const_spec((1, D_MODEL)), _const_spec((D_MAIN, D_MODEL)), _const_spec((LANES, D_MODEL))],
        out_specs=[_row_spec(tl, D_MODEL), _row_spec(tl, D_MODEL), _row_spec(tl, D_MODEL), _row_spec(tl, 3072), _row_spec(tl, LANES)],
        out_shape=[jax.ShapeDtypeStruct((L, D_MODEL), BF), jax.ShapeDtypeStruct((L, D_MODEL), BF), jax.ShapeDtypeStruct((L, D_MODEL), BF),
                   jax.ShapeDtypeStruct((L, 3072), F32),
                   jax.ShapeDtypeStruct((L, LANES), F32)],
        compiler_params=_params(("arbitrary",)),
    )(x, norm_w, w_main, w_dt)


def _in_proj_bwd(x, norm_w, dh1, du_flat, dyssm, s5_d, dzs, dzd, dxbc, ddt, w_main, w_dt):
    L = x.shape[0]
    tl = min(TOKEN_TILE, L)

    def body(x_ref, nw_ref, dh1_ref, duf_ref, dys_ref, d_ref, dzs_ref, dzd_ref, dxbc_ref, ddt_ref, wm_ref, wd_ref,
             gx_ref, du_ref, gnw_ref):
        @pl.when(pl.program_id(0) == 0)
        def _():
            gnw_ref[...] = jnp.zeros_like(gnw_ref)

        du = (duf_ref[...].astype(F32) + dys_ref[...].astype(F32) * d_ref[...]).astype(BF)
        du_ref[...] = du
        dhn = _dot(du, wm_ref[0:1024, :])
        dhn += _dot(dzs_ref[...], wm_ref[1024:2048, :])
        dhn += _dot(dzd_ref[...], wm_ref[2048:3072, :])
        dhn += _dot(dxbc_ref[...], wm_ref[3072:5120, :])
        dhn += _dot(ddt_ref[...].astype(BF), wd_ref[...])
        xv = x_ref[...]
        r = lax.rsqrt(jnp.mean(xv * xv, axis=-1, keepdims=True) + EPS)
        xh = xv * r
        gnw_ref[...] += jnp.sum(dhn * xh, axis=0, keepdims=True)
        g = dhn * nw_ref[...]
        gx_ref[...] = dh1_ref[...] + r * (g - xh * jnp.mean(g * xh, axis=-1, keepdims=True))

    return pl.pallas_call(
        body, name="in_proj_bwd", grid=(L // tl,),
        in_specs=[_row_spec(tl, D_MODEL), _const_spec((1, D_MODEL)), _row_spec(tl, D_MODEL), _row_spec(tl, D_MODEL),
                  _row_spec(tl, D_MODEL), _const_spec((1, D_MODEL)), _row_spec(tl, D_MODEL), _row_spec(tl, D_MODEL),
                  _row_spec(tl, 2048), _row_spec(tl, LANES), _const_spec((D_MAIN, D_MODEL)), _const_spec((LANES, D_MODEL))],
        out_specs=[_row_spec(tl, D_MODEL), _row_spec(tl, D_MODEL), _const_spec((1, D_MODEL))],
        out_shape=[jax.ShapeDtypeStruct((L, D_MODEL), F32), jax.ShapeDtypeStruct((L, D_MODEL), BF), jax.ShapeDtypeStruct((1, D_MODEL), F32)],
        compiler_params=_params(("arbitrary",)),
    )(x, norm_w, dh1, du_flat, dyssm, s5_d, dzs, dzd, dxbc, ddt, w_main, w_dt)


def _matmul_tn(a, b, name):
    L, M = a.shape
    N = b.shape[1]
    tm, tn, tk = min(M, 1024), min(N, 1024), min(L, 1024)

    def body(a_ref, b_ref, o_ref):
        @pl.when(pl.program_id(2) == 0)
        def _():
            o_ref[...] = jnp.zeros_like(o_ref)

        o_ref[...] += _dot_tn(a_ref[...].astype(BF), b_ref[...].astype(BF))

    return pl.pallas_call(
        body, name=name, grid=(M // tm, N // tn, L // tk),
        in_specs=[pl.BlockSpec((tk, tm), lambda i, j, k: (k, i)), pl.BlockSpec((tk, tn), lambda i, j, k: (k, j))],
        out_specs=pl.BlockSpec((tm, tn), lambda i, j, k: (i, j)),
        out_shape=jax.ShapeDtypeStruct((M, N), F32),
        compiler_params=_params(("parallel", "parallel", "arbitrary")),
    )(a, b)


def _s5_discretise(a_re, a_im, log_dt, b_re, b_im, c_re, c_im):
    dt = jnp.exp(log_dt)[:, None]
    tau = jnp.arange(CHUNK + 1, dtype=F32)
    mag = jnp.exp((a_re * dt)[:, :, None] * tau)
    ang = (a_im * dt)[:, :, None] * tau
    pw_re, pw_im = mag * jnp.cos(ang), mag * jnp.sin(ang)
    er, ei = pw_re[:, :, 1] - 1.0, pw_im[:, :, 1]
    den = a_re * a_re + a_im * a_im
    beta_re, beta_im = (er * a_re + ei * a_im) / den, (ei * a_re - er * a_im) / den
    bb_re = (beta_re[:, :, None] * b_re - beta_im[:, :, None] * b_im).transpose(0, 2, 1)
    bb_im = (beta_re[:, :, None] * b_im + beta_im[:, :, None] * b_re).transpose(0, 2, 1)
    return bb_re, bb_im, c_re, c_im, pw_re, pw_im


def _s5_table_factors(bb_re, bb_im, c_re, c_im, pw_re, pw_im):
    pr = pw_re[:, :, CHUNK - 1::-1].transpose(0, 2, 1)
    pi = pw_im[:, :, CHUNK - 1::-1].transpose(0, 2, 1)
    bbp = jnp.concatenate([bb_re, bb_im], axis=-1)
    pwr = jnp.concatenate([pr, pi], axis=-1)
    cnp = jnp.concatenate([c_re, c_im], axis=-1)
    pwf = jnp.concatenate([pw_re.transpose(0, 2, 1), pw_im.transpose(0, 2, 1)], axis=-1)
    return bbp, pwr, cnp, jnp.pad(pwf, ((0, 0), (0, 7), (0, 0)))


def _s5_small_cotangents(dbk, r12, q12, xy, uv, xy0, uv0, da64):
    n = S5_STATE
    fold = lambda a: a[..., :n] + a[..., n:]
    fold_m = lambda a: a[..., n:] - a[..., :n]
    dbb_re = fold(r12[:, :S5_CH]) + dbk[..., :n]
    dbb_im = fold_m(r12[:, S5_CH:]) + dbk[..., n:]
    dpr, dpi = fold(q12[:, :CHUNK]), fold_m(q12[:, CHUNK:])
    dc_re = -fold_m(xy[:, :S5_CH]) - fold_m(xy0[:, :S5_CH])
    dc_im = -fold(xy[:, S5_CH:]) - fold(xy0[:, S5_CH:])
    dp1_re, dp1_im = -fold_m(uv[:, :CHUNK]), -fold(uv[:, CHUNK:])
    dp0_re, dp0_im = -fold_m(uv0[:, :CHUNK]), -fold(uv0[:, CHUNK:])
    zero = jnp.zeros((S5_GROUPS, n, 1), F32)
    dpw_re = (jnp.concatenate([(dpr[:, ::-1] + dp0_re).transpose(0, 2, 1), zero], axis=-1)
              + jnp.concatenate([zero, dp1_re.transpose(0, 2, 1)], axis=-1)).at[:, :, CHUNK].add(da64[:, :n])
    dpw_im = (jnp.concatenate([(dpi[:, ::-1] + dp0_im).transpose(0, 2, 1), zero], axis=-1)
              + jnp.concatenate([zero, dp1_im.transpose(0, 2, 1)], axis=-1)).at[:, :, CHUNK].add(da64[:, n:])
    return dbb_re, dbb_im, dc_re, dc_im, dpw_re, dpw_im


def _s5_scan_powers(a_re, a_im, log_dt, nsteps):
    dt = jnp.exp(log_dt)[:, None]
    steps = (CHUNK * (2.0 ** jnp.arange(8, dtype=F32)))[None, :, None]
    mag = jnp.exp((a_re * dt)[:, None, :] * steps)
    ang = (a_im * dt)[:, None, :] * steps
    re, im = mag * jnp.cos(ang), mag * jnp.sin(ang)
    del nsteps
    return jnp.concatenate([re, re], -1), jnp.concatenate([-im, im], -1)


def _build_toeplitz(kmat, tg_ref):
    lane = lax.broadcasted_iota(jnp.int32, (CHUNK, CHUNK * S5_CH), 1)
    srow = lax.broadcasted_iota(jnp.int32, (CHUNK, CHUNK * S5_CH), 0)
    keep = lane >= S5_CH * srow
    for h in range(S5_CH):
        row = jnp.broadcast_to(kmat[h:h + 1, :], (CHUNK, CHUNK * S5_CH))
        rolled = pltpu.roll(row, 0, 1, stride=S5_CH, stride_axis=0)
        tg_ref[h * CHUNK:(h + 1) * CHUNK, :] = jnp.where(keep, rolled, 0.0).astype(BF)


def _swap_halves(x):
    return pltpu.roll(x, S5_STATE, 1)


def _hi_lo(x):
    hi = x.astype(BF)
    return hi, (x - hi.astype(F32)).astype(BF)


def _dot3(dot, a, b):
    a_hi, a_lo = _hi_lo(a)
    b_hi, b_lo = _hi_lo(b)
    return dot(a_hi, b_hi) + dot(a_hi, b_lo) + dot(a_lo, b_hi)


def _build_state_tables(bbp_ref, pwr_ref, cnp_ref, pwf_ref, wst_ref, wofft_ref, w0_ref):
    lane = lax.broadcasted_iota(jnp.int32, (1, 2 * S5_STATE), 1)
    left = lane < S5_STATE
    pwr = pwr_ref[0]
    pwr_sw = _swap_halves(pwr)
    for hh in range(S5_CH):
        bb = bbp_ref[0, hh:hh + 1, :]
        bb_sw = _swap_halves(bb)
        wst_ref[hh * CHUNK:(hh + 1) * CHUNK, :] = (jnp.where(left, bb, bb_sw) * pwr
                                                   + jnp.where(left, -bb_sw, bb) * pwr_sw).astype(BF)
    cn = cnp_ref[0]
    cn_sw = _swap_halves(cn)
    c_a = jnp.where(left, cn, -cn_sw)
    c_b = jnp.where(left, -cn_sw, -cn)
    p_prev = pwf_ref[0, 0:1, :]
    for t in range(CHUNK):
        p = pwf_ref[0, t + 1:t + 2, :]
        w0_ref[t * S5_CH:(t + 1) * S5_CH, :] = c_a * p_prev + c_b * _swap_halves(p_prev)
        wofft_ref[t * S5_CH:(t + 1) * S5_CH, :] = (c_a * p + c_b * _swap_halves(p)).astype(BF)
        p_prev = p
    return _dot3(_dot_nt, bbp_ref[0], w0_ref[...])


def _reduce_table_cotangent(d_table, cnp_ref, pwf_ref, first_power):
    cn = cnp_ref[0]
    cn_sw = _swap_halves(cn)
    xa = jnp.zeros((S5_CH, 2 * S5_STATE), F32)
    ya = jnp.zeros((S5_CH, 2 * S5_STATE), F32)
    ru_rows, rv_rows = [], []
    for t in range(CHUNK):
        blk = d_table[t * S5_CH:(t + 1) * S5_CH, :]
        p = pwf_ref[0, t + first_power:t + first_power + 1, :]
        xa = xa + blk * p
        ya = ya + blk * _swap_halves(p)
        ru_rows.append(jnp.sum(blk * cn, axis=0, keepdims=True))
        rv_rows.append(jnp.sum(blk * cn_sw, axis=0, keepdims=True))
    return jnp.concatenate([xa, ya], axis=0), jnp.concatenate(ru_rows + rv_rows, axis=0)


def _s5_core_fwd(uflat, bbp, pwr, cnp, pwf, p1, p2):
    G, nc, W = uflat.shape
    nsteps = max(1, (nc - 1).bit_length())

    def body(u_ref, bbp_ref, pwr_ref, cnp_ref, pwf_ref, p1_ref, p2_ref, y_ref, h_ref, tg_ref, wst_ref, wofft_ref, w0_ref):
        _build_toeplitz(_build_state_tables(bbp_ref, pwr_ref, cnp_ref, pwf_ref, wst_ref, wofft_ref, w0_ref), tg_ref)
        u = u_ref[0]
        x = _dot(u, wst_ref[...])
        row = lax.broadcasted_iota(jnp.int32, x.shape, 0)
        d = 1
        for k in range(nsteps):
            sh = jnp.where(row >= d, pltpu.roll(x, d, 0), 0.0)
            x = x + p1_ref[0, k:k + 1, :] * sh + p2_ref[0, k:k + 1, :] * _swap_halves(sh)
            d *= 2
        h = jnp.where(row >= 1, pltpu.roll(x, 1, 0), 0.0)
        h_ref[0] = h
        y = _dot(u, tg_ref[...]) + _dot_nt(h.astype(BF), wofft_ref[...])
        y_ref[0] = y.astype(BF)

    spec_g = lambda a, b: pl.BlockSpec((1, a, b), lambda g: (g, 0, 0))
    return pl.pallas_call(
        body, name="s5_core_fwd", grid=(G,),
        in_specs=[spec_g(nc, W), spec_g(S5_CH, 2 * S5_STATE), spec_g(CHUNK, 2 * S5_STATE),
                  spec_g(S5_CH, 2 * S5_STATE), spec_g(CHUNK + 8, 2 * S5_STATE), spec_g(8, 2 * S5_STATE), spec_g(8, 2 * S5_STATE)],
        out_specs=[spec_g(nc, W), spec_g(nc, 2 * S5_STATE)],
        out_shape=[jax.ShapeDtypeStruct((G, nc, W), BF), jax.ShapeDtypeStruct((G, nc, 2 * S5_STATE), F32)],
        scratch_shapes=[pltpu.VMEM((W, W), BF), pltpu.VMEM((W, 2 * S5_STATE), BF), pltpu.VMEM((W, 2 * S5_STATE), BF),
                        pltpu.VMEM((W, 2 * S5_STATE), F32)],
        compiler_params=_params(("arbitrary",)),
    )(uflat, bbp, pwr, cnp, pwf, p1, p2)


def _s5_core_bwd(uflat, dyflat, hsave, bbp, pwr, cnp, pwf, p1, p2):
    G, nc, W = uflat.shape
    nsteps = max(1, (nc - 1).bit_length())

    def body(u_ref, dy_ref, h_ref, bbp_ref, pwr_ref, cnp_ref, pwf_ref, p1_ref, p2_ref,
             du_ref, dbk_ref, r12_ref, q12_ref, xy_ref, uv_ref, xy0_ref, uv0_ref, da_ref,
             tg_ref, flip_ref, wst_ref, wofft_ref, w0_ref):
        @pl.when(pl.program_id(0) == 0)
        def _():
            r = lax.broadcasted_iota(jnp.int32, (W, W), 0)
            c = lax.broadcasted_iota(jnp.int32, (W, W), 1)
            flip_ref[...] = (((r >> 6) == (c >> 6)) & ((r & (CHUNK - 1)) + (c & (CHUNK - 1)) == CHUNK - 1)).astype(BF)

        _build_toeplitz(_build_state_tables(bbp_ref, pwr_ref, cnp_ref, pwf_ref, wst_ref, wofft_ref, w0_ref), tg_ref)
        u = u_ref[0]
        dy = dy_ref[0]
        h = h_ref[0]
        gh = _dot(dy, wofft_ref[...])
        row = lax.broadcasted_iota(jnp.int32, gh.shape, 0)
        x = jnp.where(row < nc - 1, pltpu.roll(gh, nc - 1, 0), 0.0)
        d = 1
        for k in range(nsteps):
            sh = jnp.where(row < nc - d, pltpu.roll(x, nc - d, 0), 0.0)
            x = x + p1_ref[0, k:k + 1, :] * sh - p2_ref[0, k:k + 1, :] * _swap_halves(sh)
            d *= 2
        gs = x.astype(BF)
        du_ref[0] = (_dot_nt(dy, tg_ref[...]) + _dot_nt(gs, wst_ref[...])).astype(BF)

        dwst = _dot_tn(u, gs)
        pwr = pwr_ref[0]
        pwr_sw = _swap_halves(pwr)
        q1 = jnp.zeros((CHUNK, 2 * S5_STATE), F32)
        q2 = jnp.zeros((CHUNK, 2 * S5_STATE), F32)
        r1_rows, r2_rows = [], []
        for hh in range(S5_CH):
            blk = dwst[hh * CHUNK:(hh + 1) * CHUNK, :]
            r1_rows.append(jnp.sum(blk * pwr, axis=0, keepdims=True))
            r2_rows.append(jnp.sum(blk * pwr_sw, axis=0, keepdims=True))
            bb = bbp_ref[0, hh:hh + 1, :]
            q1 = q1 + blk * bb
            q2 = q2 + blk * _swap_halves(bb)
        r12_ref[0] = jnp.concatenate(r1_rows + r2_rows, axis=0)
        q12_ref[0] = jnp.concatenate([q1, q2], axis=0)

        xy_ref[0], uv_ref[0] = _reduce_table_cotangent(_dot_tn(dy, h.astype(BF)), cnp_ref, pwf_ref, 1)
        r1 = jnp.sum(x * h, axis=0, keepdims=True)
        r2 = jnp.sum(x * _swap_halves(h), axis=0, keepdims=True)
        da_ref[0] = jnp.concatenate([r1, r2, jnp.zeros((6, 2 * S5_STATE), F32)], axis=0)
        lane = lax.broadcasted_iota(jnp.int32, (CHUNK, W), 1)
        srow = lax.broadcasted_iota(jnp.int32, (CHUNK, W), 0)
        keep = lane < S5_CH * (srow + 1)
        ur = _dot(u, flip_ref[...]).astype(BF)
        dk_rows = []
        for hh in range(S5_CH):
            dt_h = _dot_tn(ur[:, hh * CHUNK:(hh + 1) * CHUNK], dy)
            back = pltpu.roll(dt_h, S5_CH, 1, stride=S5_CH, stride_axis=0)
            dk_rows.append(jnp.sum(jnp.where(keep, back, 0.0), axis=0, keepdims=True))
        dk = jnp.concatenate(dk_rows, axis=0)
        dbk_ref[0] = _dot3(_dot, dk, w0_ref[...])
        xy0_ref[0], uv0_ref[0] = _reduce_table_cotangent(_dot3(_dot_tn, dk, bbp_ref[0]), cnp_ref, pwf_ref, 0)

    spec_g = lambda a, b: pl.BlockSpec((1, a, b), lambda g: (g, 0, 0))
    small_c = jax.ShapeDtypeStruct((G, 2 * S5_CH, 2 * S5_STATE), F32)
    small_p = jax.ShapeDtypeStruct((G, 2 * CHUNK, 2 * S5_STATE), F32)
    return pl.pallas_call(
        body, name="s5_core_bwd", grid=(G,),
        in_specs=[spec_g(nc, W), spec_g(nc, W), spec_g(nc, 2 * S5_STATE), spec_g(S5_CH, 2 * S5_STATE),
                  spec_g(CHUNK, 2 * S5_STATE), spec_g(S5_CH, 2 * S5_STATE), spec_g(CHUNK + 8, 2 * S5_STATE),
                  spec_g(8, 2 * S5_STATE), spec_g(8, 2 * S5_STATE)],
        out_specs=[spec_g(nc, W), spec_g(S5_CH, 2 * S5_STATE), spec_g(2 * S5_CH, 2 * S5_STATE), spec_g(2 * CHUNK, 2 * S5_STATE),
                   spec_g(2 * S5_CH, 2 * S5_STATE), spec_g(2 * CHUNK, 2 * S5_STATE),
                   spec_g(2 * S5_CH, 2 * S5_STATE), spec_g(2 * CHUNK, 2 * S5_STATE), spec_g(8, 2 * S5_STATE)],
        out_shape=[jax.ShapeDtypeStruct((G, nc, W), BF), jax.ShapeDtypeStruct((G, S5_CH, 2 * S5_STATE), F32),
                   small_c, small_p, small_c, small_p, small_c, small_p, jax.ShapeDtypeStruct((G, 8, 2 * S5_STATE), F32)],
        scratch_shapes=[pltpu.VMEM((W, W), BF), pltpu.VMEM((W, W), BF), pltpu.VMEM((W, 2 * S5_STATE), BF),
                        pltpu.VMEM((W, 2 * S5_STATE), BF), pltpu.VMEM((W, 2 * S5_STATE), F32)],
        compiler_params=_params(("arbitrary",)),
    )(uflat, dyflat, hsave, bbp, pwr, cnp, pwf, p1, p2)


def _flat_hs(a, nc):
    return a.reshape(nc, CHUNK, S5_GROUPS, S5_CH).transpose(2, 0, 3, 1).reshape(S5_GROUPS, nc, CHUNK * S5_CH)


def _unflat_hs(a, nc):
    return a.reshape(S5_GROUPS, nc, S5_CH, CHUNK).transpose(1, 3, 0, 2).reshape(nc * CHUNK, D_MODEL)


def _flat_tk(a, nc):
    return a.reshape(nc, CHUNK, S5_GROUPS, S5_CH).transpose(2, 0, 1, 3).reshape(S5_GROUPS, nc, CHUNK * S5_CH)


def _unflat_tk(a, nc):
    return a.reshape(S5_GROUPS, nc, CHUNK, S5_CH).transpose(1, 2, 0, 3).reshape(nc * CHUNK, D_MODEL)


def _s5_post_fwd(yssm, u, zs, s5_d, w_glu, b_glu):
    L = yssm.shape[0]
    tl = min(TOKEN_TILE, L)

    def body(ys_ref, u_ref, z_ref, d_ref, wg_ref, bg_ref, o_ref):
        u = u_ref[...].astype(F32)
        a = _gelu(ys_ref[...].astype(F32) + d_ref[...] * u)
        y = a * _sigmoid(_dot(a.astype(BF), wg_ref[...]) + bg_ref[...])
        z = z_ref[...].astype(F32)
        o_ref[...] = (y * z * _sigmoid(z)).astype(BF)

    return pl.pallas_call(
        body, name="s5_post_fwd", grid=(L // tl,),
        in_specs=[_row_spec(tl, D_MODEL), _row_spec(tl, D_MODEL), _row_spec(tl, D_MODEL), _const_spec((1, D_MODEL)),
                  _const_spec((D_MODEL, D_MODEL)), _const_spec((1, D_MODEL))],
        out_specs=_row_spec(tl, D_MODEL),
        out_shape=jax.ShapeDtypeStruct((L, D_MODEL), BF),
        compiler_params=_params(("arbitrary",)),
    )(yssm, u, zs, s5_d, w_glu, b_glu)


def _s5_post_bwd(dys5, yssm, u, zs, s5_d, w_glu, b_glu):
    L = yssm.shape[0]
    tl = min(TOKEN_TILE, L)

    def body(dy_ref, ys_ref, u_ref, z_ref, d_ref, wg_ref, bg_ref, dz_ref, dys_ref, a_ref, dgl_ref, dbg_ref, dd_ref):
        @pl.when(pl.program_id(0) == 0)
        def _():
            dbg_ref[...] = jnp.zeros_like(dbg_ref)
            dd_ref[...] = jnp.zeros_like(dd_ref)

        u = u_ref[...].astype(F32)
        y0 = ys_ref[...].astype(F32) + d_ref[...] * u
        a = _gelu(y0)
        a_bf = a.astype(BF)
        sg = _sigmoid(_dot(a_bf, wg_ref[...]) + bg_ref[...])
        y = a * sg
        z = z_ref[...].astype(F32)
        sz = _sigmoid(z)
        dout = dy_ref[...].astype(F32)
        dz_ref[...] = (dout * y * sz * (1.0 + z * (1.0 - sz))).astype(BF)
        dyv = dout * z * sz
        dgl = dyv * a * sg * (1.0 - sg)
        dgl_bf = dgl.astype(BF)
        da = dyv * sg + _dot_nt(dgl_bf, wg_ref[...])
        dy0 = da * _gelu_grad(y0)
        dbg_ref[...] += jnp.sum(dgl, axis=0, keepdims=True)
        dd_ref[...] += jnp.sum(dy0 * u, axis=0, keepdims=True)
        dys_ref[...] = dy0.astype(BF)
        a_ref[...] = a_bf
        dgl_ref[...] = dgl_bf

    big = jax.ShapeDtypeStruct((L, D_MODEL), BF)
    vec = jax.ShapeDtypeStruct((1, D_MODEL), F32)
    return pl.pallas_call(
        body, name="s5_post_bwd", grid=(L // tl,),
        in_specs=[_row_spec(tl, D_MODEL), _row_spec(tl, D_MODEL), _row_spec(tl, D_MODEL), _row_spec(tl, D_MODEL),
                  _const_spec((1, D_MODEL)), _const_spec((D_MODEL, D_MODEL)), _const_spec((1, D_MODEL))],
        out_specs=[_row_spec(tl, D_MODEL)] * 4 + [_const_spec((1, D_MODEL))] * 2,
        out_shape=[big, big, big, big, vec, vec],
        compiler_params=_params(("arbitrary",)),
    )(dys5, yssm, u, zs, s5_d, w_glu, b_glu)


def _cumsum_rows(a):
    row = lax.broadcasted_iota(jnp.int32, a.shape, 0)
    d = 1
    while d < a.shape[0]:
        a = a + jnp.where(row >= d, pltpu.roll(a, d, 0), 0.0)
        d *= 2
    return a


def _rev_cumsum_rows(a):
    n = a.shape[0]
    row = lax.broadcasted_iota(jnp.int32, a.shape, 0)
    d = 1
    while d < n:
        a = a + jnp.where(row < n - d, pltpu.roll(a, n - d, 0), 0.0)
        d *= 2
    return a


def _ssd_fill_padded(first, xs_ref, bc_ref, hx_ref, hb_ref, xp_ref, tl):
    hal = jnp.concatenate([hx_ref[...], hb_ref[...]], axis=1)
    xp_ref[0:8, :] = jnp.where(first, 0.0, hal)
    xp_ref[8:8 + tl, 0:1024] = xs_ref[...]
    xp_ref[8:8 + tl, 1024:2048] = bc_ref[...]


def _ssd_conv_fwd(first, xs_ref, bc_ref, hx_ref, hb_ref, cw_ref, cb_ref, xp_ref, tl):
    _ssd_fill_padded(first, xs_ref, bc_ref, hx_ref, hb_ref, xp_ref, tl)
    pre = cb_ref[...] + cw_ref[0:1, :] * xp_ref[5:5 + tl, :]
    for k in range(1, 4):
        pre = pre + cw_ref[k:k + 1, :] * xp_ref[5 + k:5 + k + tl, :]
    return pre


def _onehot_lane(h):
    return (lax.broadcasted_iota(jnp.int32, (1, LANES), 1) == h).astype(F32)


def _dot_exact(x, e):
    hi = x.astype(BF)
    r = x - hi.astype(F32)
    mid = r.astype(BF)
    lo = (r - mid.astype(F32)).astype(BF)
    return _dot(hi, e) + _dot(mid, e) + _dot(lo, e)


def _head_expand_matrices():
    e = lax.broadcasted_iota(jnp.int32, (LANES, D_MODEL), 0) == (lax.broadcasted_iota(jnp.int32, (LANES, D_MODEL), 1) >> 6)
    et = (lax.broadcasted_iota(jnp.int32, (D_MODEL, LANES), 0) >> 6) == lax.broadcasted_iota(jnp.int32, (D_MODEL, LANES), 1)
    return e.astype(BF), et.astype(BF)


def _group_masks():
    r64 = lax.broadcasted_iota(jnp.int32, (4 * CHUNK, CHUNK), 0)
    causal4 = (r64 & (CHUNK - 1)) >= lax.broadcasted_iota(jnp.int32, (4 * CHUNK, CHUNK), 1)
    r256 = lax.broadcasted_iota(jnp.int32, (4 * CHUNK, 4 * SSD_HEAD_DIM), 0)
    same = (r256 >> 6) == (lax.broadcasted_iota(jnp.int32, (4 * CHUNK, 4 * SSD_HEAD_DIM), 1) >> 6)
    return causal4, same


def _group_decay(acs, acs_t, j, causal4):
    col = jnp.concatenate([acs[:, 4 * j + hh:4 * j + hh + 1] for hh in range(4)], axis=0)
    rowv = jnp.concatenate([jnp.broadcast_to(acs_t[4 * j + hh:4 * j + hh + 1, :], (CHUNK, CHUNK)) for hh in range(4)], axis=0)
    return jnp.where(causal4, jnp.exp(col - rowv), 0.0)


def _group_last_decay(acs_t, j):
    return jnp.concatenate([jnp.broadcast_to(jnp.exp(acs_t[4 * j + hh:4 * j + hh + 1, CHUNK - 1:CHUNK]), (SSD_HEAD_DIM, 1))
                            for hh in range(4)], axis=0)


def _fold_heads(r):
    return r[0:CHUNK] + r[CHUNK:2 * CHUNK] + r[2 * CHUNK:3 * CHUNK] + r[3 * CHUNK:4 * CHUNK]


def _ssd_specs_in(tl, nt, rev):
    t_of = (lambda i: nt - 1 - i) if rev else (lambda i: i)
    rows = lambda w, col: pl.BlockSpec((tl, w), lambda i: (t_of(i), col))
    halo = lambda col: pl.BlockSpec((8, 1024), lambda i: (jnp.maximum(t_of(i) * (tl // 8) - 1, 0), col))
    return t_of, rows, halo


def _ssd_fwd(proj, pdt, conv_w, conv_b, dt_bias, a_log, ssd_d, norm_w):
    L = proj.shape[0]
    tl = min(TOKEN_TILE, L)
    nt, ncl = L // tl, tl // CHUNK
    _, rows, halo = _ssd_specs_in(tl, nt, False)

    def body(xs_ref, bc_ref, hx_ref, hb_ref, dt_ref, z_ref, cw_ref, cb_ref, dtb_ref, al_ref, dd_ref, nw_ref,
             y_ref, ypre_ref, st_ref, pre_ref, xp_ref, xbc_ref, dts_ref, hst_ref):
        i = pl.program_id(0)

        @pl.when(i == 0)
        def _():
            hst_ref[...] = jnp.zeros_like(hst_ref)

        pre = _ssd_conv_fwd(i == 0, xs_ref, bc_ref, hx_ref, hb_ref, cw_ref, cb_ref, xp_ref, tl)
        pre_ref[...] = pre
        xbc_ref[...] = pre * _sigmoid(pre)
        dts_ref[...] = _softplus(dt_ref[...] + dtb_ref[...])
        a_neg = -jnp.exp(al_ref[...])
        e16, _ = _head_expand_matrices()
        causal4, same = _group_masks()
        dd_x = _dot_exact(jnp.broadcast_to(dd_ref[...], (8, LANES)), e16)[0:1, :]

        def chunk(c, carry):
            r0 = pl.multiple_of(c * CHUNK, CHUNK)
            xbc = xbc_ref[pl.ds(r0, CHUNK), :]
            dtc = dts_ref[pl.ds(r0, CHUNK), :]
            acs = _cumsum_rows(dtc * a_neg)
            acs_t = acs.T
            acs_x = _dot_exact(acs, e16)
            xs = xbc[:, 0:1024]
            xd = xs * _dot_exact(dtc, e16)
            xd_bf = xd.astype(BF)
            xdd = (xd * jnp.exp(acs_x[CHUNK - 1:CHUNK, :] - acs_x)).astype(BF)
            e_x = jnp.exp(acs_x)
            for j in range(SSD_GROUPS):
                sl = slice(256 * j, 256 * (j + 1))
                bj = xbc[:, 1024 + 128 * j:1024 + 128 * (j + 1)].astype(BF)
                cj = xbc[:, 1536 + 128 * j:1536 + 128 * (j + 1)].astype(BF)
                g = _dot_nt(cj, bj)
                hj = hst_ref[sl, :]
                zj = _dot_nt(cj, hj.astype(BF))
                sc = (jnp.concatenate([g] * 4, axis=0) * _group_decay(acs, acs_t, j, causal4)).astype(BF)
                yd = _fold_heads(jnp.where(same, _dot(sc, xd_bf[:, sl]), 0.0))
                ypre_ref[pl.ds(r0, CHUNK), sl] = yd + e_x[:, sl] * zj + dd_x[:, sl] * xs[:, sl]
                st_ref[c, sl, :] = hj
                hst_ref[sl, :] = _group_last_decay(acs_t, j) * hj + _dot_tn(xdd[:, sl], bj)
            return carry

        lax.fori_loop(0, ncl, chunk, 0, unroll=True)
        z = z_ref[...]
        gg = ypre_ref[...] * z * _sigmoid(z)
        for j in range(SSD_GROUPS):
            seg = gg[:, 256 * j:256 * (j + 1)]
            r = lax.rsqrt(jnp.mean(seg * seg, axis=-1, keepdims=True) + EPS)
            y_ref[:, 256 * j:256 * (j + 1)] = (seg * r * nw_ref[:, 256 * j:256 * (j + 1)]).astype(BF)

    nc = L // CHUNK
    return pl.pallas_call(
        body, name="ssd_fwd", grid=(nt,),
        in_specs=[rows(1024, 1), rows(1024, 2), halo(1), halo(2), rows(LANES, 0), rows(1024, 0),
                  _const_spec((4, 2048)), _const_spec((1, 2048)), _const_spec((1, LANES)), _const_spec((1, LANES)),
                  _const_spec((1, LANES)), _const_spec((1, D_MODEL))],
        out_specs=[_row_spec(tl, D_MODEL), _row_spec(tl, D_MODEL), pl.BlockSpec((ncl, 1024, SSD_STATE), lambda i: (i, 0, 0)),
                   _row_spec(tl, 2048)],
        out_shape=[jax.ShapeDtypeStruct((L, D_MODEL), BF), jax.ShapeDtypeStruct((L, D_MODEL), F32),
                   jax.ShapeDtypeStruct((nc, 1024, SSD_STATE), F32), jax.ShapeDtypeStruct((L, 2048), F32)],
        scratch_shapes=[pltpu.VMEM((tl + 8, 2048), F32), pltpu.VMEM((tl, 2048), F32), pltpu.VMEM((tl, LANES), F32),
                        pltpu.VMEM((1024, SSD_STATE), F32)],
        compiler_params=_params(("arbitrary",)),
    )(proj, proj, proj, proj, pdt, proj, conv_w, conv_b, dt_bias, a_log, ssd_d, norm_w)


def _ssd_bwd(dyssd, ypre, proj, pdt, states, pre_act, conv_w, dt_bias, a_log, ssd_d, norm_w):
    L = proj.shape[0]
    tl = min(TOKEN_TILE, L)
    nt, ncl = L // tl, tl // CHUNK
    t_of, rows, halo = _ssd_specs_in(tl, nt, True)

    def body(dy_ref, ypre_ref, z_ref, xs_ref, bc_ref, hx_ref, hb_ref, dt_ref, st_ref, pre_ref, cw_ref, dtb_ref, al_ref,
             dd_ref, nw_ref,
             dxbc_ref, ddt_ref, dz_ref, dcw_ref, dcb_ref, ddtb_ref, dal_ref, ddd_ref, dnw_ref,
             xp_ref, xbc_ref, dts_ref, dyp_ref, dxs_ref, ddts_ref, dp_ref, dh_ref):
        i = pl.program_id(0)

        @pl.when(i == 0)
        def _():
            for r in (dcw_ref, dcb_ref, ddtb_ref, dal_ref, ddd_ref, dnw_ref, dh_ref):
                r[...] = jnp.zeros_like(r)
            dp_ref[tl:tl + 8, :] = jnp.zeros((8, 2048), F32)

        _ssd_fill_padded(t_of(i) == 0, xs_ref, bc_ref, hx_ref, hb_ref, xp_ref, tl)
        dts_ref[...] = _softplus(dt_ref[...] + dtb_ref[...])
        a_neg = -jnp.exp(al_ref[...])

        def gated_norm_bwd(rows_):
            ypre = ypre_ref[rows_, :]
            z = z_ref[rows_, :]
            sz = _sigmoid(z)
            gg = ypre * z * sz
            dout = dy_ref[rows_, :]
            dyp_l = []
            for j in range(SSD_GROUPS):
                sl = slice(256 * j, 256 * (j + 1))
                seg = gg[:, sl]
                r = lax.rsqrt(jnp.mean(seg * seg, axis=-1, keepdims=True) + EPS)
                gh = seg * r
                dnw_ref[:, sl] += jnp.sum(dout[:, sl] * gh, axis=0, keepdims=True)
                gw = dout[:, sl] * nw_ref[:, sl]
                dgg = r * (gw - gh * jnp.mean(gw * gh, axis=-1, keepdims=True))
                dyp_l.append(dgg * z[:, sl] * sz[:, sl])
                dz_ref[rows_, sl] = (dgg * ypre[:, sl] * sz[:, sl] * (1.0 + z[:, sl] * (1.0 - sz[:, sl]))).astype(BF)
            return jnp.concatenate(dyp_l, axis=1)

        e16, e16t = _head_expand_matrices()
        causal4, same = _group_masks()
        dd_x = _dot_exact(jnp.broadcast_to(dd_ref[...], (8, LANES)), e16)[0:1, :]
        last_row = (lax.broadcasted_iota(jnp.int32, (CHUNK, 1), 0) == CHUNK - 1).astype(F32)
        sel_rows = lax.broadcasted_iota(jnp.int32, (4 * CHUNK, LANES), 0) >> 6
        sel_lanes = lax.broadcasted_iota(jnp.int32, (4 * CHUNK, LANES), 1)

        def chunk(k, carry):
            dal_acc, ddx_acc = carry
            c = ncl - 1 - k
            r0 = pl.multiple_of(c * CHUNK, CHUNK)
            pre_c = pre_ref[pl.ds(r0, CHUNK), :]
            xbc = pre_c * _sigmoid(pre_c)
            dtc = dts_ref[pl.ds(r0, CHUNK), :]
            dyp = gated_norm_bwd(pl.ds(r0, CHUNK))
            acs = _cumsum_rows(dtc * a_neg)
            acs_t = acs.T
            acs_x = _dot_exact(acs, e16)
            dt_x = _dot_exact(dtc, e16)
            xs = xbc[:, 0:1024]
            xd = xs * dt_x
            xd_bf = xd.astype(BF)
            dec_x = jnp.exp(acs_x[CHUNK - 1:CHUNK, :] - acs_x)
            xdd = xd * dec_x
            xdd_bf = xdd.astype(BF)
            dz = dyp * jnp.exp(acs_x)
            dz_bf = dz.astype(BF)
            ddx_acc = ddx_acc + jnp.sum(dyp * xs, axis=0, keepdims=True)
            dacs = jnp.zeros((CHUNK, LANES), F32)
            hsum = jnp.zeros((1, LANES), F32)
            p1_l, p2_l, p3_l = [], [], []
            for j in range(SSD_GROUPS):
                sl = slice(256 * j, 256 * (j + 1))
                bj = xbc[:, 1024 + 128 * j:1024 + 128 * (j + 1)].astype(BF)
                cj = xbc[:, 1536 + 128 * j:1536 + 128 * (j + 1)].astype(BF)
                g = _dot_nt(cj, bj)
                hj = st_ref[c, sl, :]
                hj_bf = hj.astype(BF)
                dhj = dh_ref[sl, :]
                dhj_bf = dhj.astype(BF)
                zj = _dot_nt(cj, hj_bf)
                qj = _dot_nt(bj, dhj_bf)
                lm = _group_decay(acs, acs_t, j, causal4)
                sc = jnp.concatenate([g] * 4, axis=0) * lm
                sc_bf = sc.astype(BF)
                dym = jnp.where(same, jnp.concatenate([dyp[:, sl]] * 4, axis=0), 0.0).astype(BF)
                dsc = _dot_nt(dym, xd_bf[:, sl])
                dxd = _dot_tn(sc_bf, dym) + qj * dec_x[:, sl]
                m = dsc * sc
                dg_bf = _fold_heads(dsc * lm).astype(BF)
                rs = jnp.sum(m, axis=1, keepdims=True)
                e2 = dhj * hj
                for hh in range(4):
                    oh = _onehot_lane(4 * j + hh)
                    dacs = dacs + oh * rs[CHUNK * hh:CHUNK * (hh + 1)]
                    hsum = hsum + oh * jnp.sum(jnp.sum(e2[64 * hh:64 * (hh + 1)], axis=0, keepdims=True), axis=1, keepdims=True)
                sel = (sel_rows + 4 * j == sel_lanes).astype(BF)
                hi = m.astype(BF)
                rem = m - hi.astype(F32)
                mid = rem.astype(BF)
                lo = (rem - mid.astype(F32)).astype(BF)
                dacs = dacs - (_dot_tn(hi, sel) + _dot_tn(mid, sel) + _dot_tn(lo, sel))
                p1_l.append(dz[:, sl] * zj)
                p2_l.append(qj * xdd[:, sl])
                p3_l.append(dxd * xs[:, sl])
                dxs_ref[pl.ds(r0, CHUNK), sl] = dd_x[:, sl] * dyp[:, sl] + dxd * dt_x[:, sl]
                dxs_ref[pl.ds(r0, CHUNK), 1536 + 128 * j:1536 + 128 * (j + 1)] = _dot(dg_bf, bj) + _dot(dz_bf[:, sl], hj_bf)
                dxs_ref[pl.ds(r0, CHUNK), 1024 + 128 * j:1024 + 128 * (j + 1)] = _dot_tn(dg_bf, cj) + _dot(xdd_bf[:, sl], dhj_bf)
                dh_ref[sl, :] = _group_last_decay(acs_t, j) * dhj + _dot_tn(dz_bf[:, sl], cj)
            stacked = jnp.concatenate([jnp.concatenate(p1_l, axis=1), jnp.concatenate(p2_l, axis=1), jnp.concatenate(p3_l, axis=1)], axis=0)
            red = _dot_exact(stacked, e16t)
            r1, r2, ddtc = red[0:CHUNK], red[CHUNK:2 * CHUNK], red[2 * CHUNK:3 * CHUNK]
            tot = jnp.sum(r2, axis=0, keepdims=True) + jnp.exp(acs[CHUNK - 1:CHUNK, :]) * hsum
            da = _rev_cumsum_rows(dacs + r1 - r2 + last_row * tot)
            ddts_ref[pl.ds(r0, CHUNK), :] = ddtc + da * a_neg
            dal_acc = dal_acc + jnp.sum(da * dtc, axis=0, keepdims=True)
            return dal_acc, ddx_acc

        carry = (jnp.zeros((1, LANES), F32), jnp.zeros((1, D_MODEL), F32))
        for k in range(ncl):
            carry = chunk(k, carry)
        dal_acc, ddx_acc = carry
        dal_ref[...] += dal_acc * a_neg
        ddd_ref[...] += _dot_exact(jnp.broadcast_to(ddx_acc, (8, D_MODEL)), e16t)[0:1, :]
        ddt_raw = ddts_ref[...] * _sigmoid(dt_ref[...] + dtb_ref[...])
        ddt_ref[...] = ddt_raw
        ddtb_ref[...] += jnp.sum(ddt_raw, axis=0, keepdims=True)

        pre = pre_ref[...]
        sp = _sigmoid(pre)
        dpre = dxs_ref[...] * sp * (1.0 + pre * (1.0 - sp))
        dp_ref[0:tl, :] = dpre
        dcb_ref[...] += jnp.sum(dpre, axis=0, keepdims=True)
        dx = jnp.zeros((tl, 2048), F32)
        for k in range(4):
            dcw_ref[k:k + 1, :] += jnp.sum(dpre * xp_ref[5 + k:5 + k + tl, :], axis=0, keepdims=True)
            dx = dx + cw_ref[k:k + 1, :] * dp_ref[3 - k:3 - k + tl, :]
        dxbc_ref[...] = dx.astype(BF)
        dp_ref[tl:tl + 8, :] = dp_ref[0:8, :]

    vec = lambda w: jax.ShapeDtypeStruct((1, w), F32)
    rrow = lambda w: pl.BlockSpec((tl, w), lambda i: (t_of(i), 0))
    return pl.pallas_call(
        body, name="ssd_bwd", grid=(nt,),
        in_specs=[rrow(D_MODEL), rrow(D_MODEL), rows(1024, 0), rows(1024, 1), rows(1024, 2), halo(1), halo(2), rows(LANES, 0),
                  pl.BlockSpec((ncl, 1024, SSD_STATE), lambda i: (t_of(i), 0, 0)), rrow(2048),
                  _const_spec((4, 2048)), _const_spec((1, LANES)), _const_spec((1, LANES)),
                  _const_spec((1, LANES)), _const_spec((1, D_MODEL))],
        out_specs=[rrow(2048), rrow(LANES), rrow(D_MODEL), _const_spec((8, 2048)), _const_spec((1, 2048)),
                   _const_spec((1, LANES)), _const_spec((1, LANES)), _const_spec((1, LANES)), _const_spec((1, D_MODEL))],
        out_shape=[jax.ShapeDtypeStruct((L, 2048), BF), jax.ShapeDtypeStruct((L, LANES), F32), jax.ShapeDtypeStruct((L, D_MODEL), BF),
                   jax.ShapeDtypeStruct((8, 2048), F32), vec(2048), vec(LANES), vec(LANES), vec(LANES), vec(D_MODEL)],
        scratch_shapes=[pltpu.VMEM((tl + 8, 2048), F32), pltpu.VMEM((tl, 2048), F32),
                        pltpu.VMEM((tl, LANES), F32), pltpu.VMEM((tl, D_MODEL), F32), pltpu.VMEM((tl, 2048), F32),
                        pltpu.VMEM((tl, LANES), F32), pltpu.VMEM((tl + 8, 2048), F32), pltpu.VMEM((1024, SSD_STATE), F32)],
        compiler_params=_params(("arbitrary",)),
    )(dyssd, ypre, proj, proj, proj, proj, proj, pdt, states, pre_act, conv_w, dt_bias, a_log, ssd_d, norm_w)


def _head_fwd_bwd(x, ys5, yssd, p, target, w_out, w_gate, w_proj, ple_nw, fin_nw):
    L = x.shape[0]
    tl = min(TOKEN_TILE, L)
    inv_d = 1.0 / D_MODEL

    def body(x_ref, ys_ref, yd_ref, p_ref, t_ref, wo_ref, wg_ref, wp_ref, pnw_ref, fnw_ref,
             loss_ref, dys_ref, dyd_ref, dh1_ref, n2_ref, dgl_ref, dpp_ref, dpnw_ref, dfnw_ref):
        @pl.when(pl.program_id(0) == 0)
        def _():
            loss_ref[...] = jnp.zeros_like(loss_ref)
            dpnw_ref[...] = jnp.zeros_like(dpnw_ref)
            dfnw_ref[...] = jnp.zeros_like(dfnw_ref)

        h1 = x_ref[...] + _dot(ys_ref[...], wo_ref[0:1024, :]) + _dot(yd_ref[...], wo_ref[1024:2048, :])
        r1 = lax.rsqrt(jnp.mean(h1 * h1, axis=-1, keepdims=True) + EPS)
        hh1 = h1 * r1
        n2 = (hh1 * pnw_ref[...]).astype(BF)
        gate = _sigmoid(_dot(n2, wg_ref[...]))
        pp = _dot(p_ref[...].astype(BF), wp_ref[...])
        h2 = h1 + pp * gate
        r2 = lax.rsqrt(jnp.mean(h2 * h2, axis=-1, keepdims=True) + EPS)
        hh2 = h2 * r2
        err = hh2 * fnw_ref[...] - t_ref[...]
        loss_ref[...] += 0.5 * inv_d * jnp.sum(err * err)
        dyo = err * inv_d
        dfnw_ref[...] += jnp.sum(dyo * hh2, axis=0, keepdims=True)
        g2 = dyo * fnw_ref[...]
        dh2 = r2 * (g2 - hh2 * jnp.mean(g2 * hh2, axis=-1, keepdims=True))
        dpp_ref[...] = (dh2 * gate).astype(BF)
        dgl = (dh2 * pp * gate * (1.0 - gate)).astype(BF)
        dgl_ref[...] = dgl
        n2_ref[...] = n2
        dn2 = _dot_nt(dgl, wg_ref[...])
        dpnw_ref[...] += jnp.sum(dn2 * hh1, axis=0, keepdims=True)
        g1 = dn2 * pnw_ref[...]
        dh1 = dh2 + r1 * (g1 - hh1 * jnp.mean(g1 * hh1, axis=-1, keepdims=True))
        dh1_ref[...] = dh1
        dh1_bf = dh1.astype(BF)
        dys_ref[...] = _dot_nt(dh1_bf, wo_ref[0:1024, :]).astype(BF)
        dyd_ref[...] = _dot_nt(dh1_bf, wo_ref[1024:2048, :])

    big = jax.ShapeDtypeStruct((L, D_MODEL), BF)
    vec = jax.ShapeDtypeStruct((1, D_MODEL), F32)
    return pl.pallas_call(
        body, name="head_fwd_bwd", grid=(L // tl,),
        in_specs=[_row_spec(tl, D_MODEL), _row_spec(tl, D_MODEL), _row_spec(tl, D_MODEL), _row_spec(tl, 256), _row_spec(tl, D_MODEL),
                  _const_spec((2048, D_MODEL)), _const_spec((D_MODEL, D_MODEL)), _const_spec((256, D_MODEL)),
                  _const_spec((1, D_MODEL)), _const_spec((1, D_MODEL))],
        out_specs=[_const_spec((8, LANES)), _row_spec(tl, D_MODEL), _row_spec(tl, D_MODEL), _row_spec(tl, D_MODEL),
                   _row_spec(tl, D_MODEL), _row_spec(tl, D_MODEL), _row_spec(tl, D_MODEL), _const_spec((1, D_MODEL)), _const_spec((1, D_MODEL))],
        out_shape=[jax.ShapeDtypeStruct((8, LANES), F32), big, jax.ShapeDtypeStruct((L, D_MODEL), F32),
                   jax.ShapeDtypeStruct((L, D_MODEL), F32), big, big, big, vec, vec],
        compiler_params=_params(("arbitrary",)),
    )(x, ys5, yssd, p, target, w_out, w_gate, w_proj, ple_nw, fin_nw)


def _pad_lanes(v):
    return jnp.pad(v.reshape(1, -1), ((0, 0), (0, LANES - v.size)))


def _local_step(x, p, target, w):
    L = x.shape[0]
    nc = L // CHUNK
    nsteps = max(1, (nc - 1).bit_length())
    w_in_t = w["w_in"]
    w_main = w_in_t[:D_MAIN]
    w_dt = jnp.pad(w_in_t[D_MAIN:], ((0, LANES - SSD_HEADS), (0, 0)))
    norm_w = w["norm_w"].reshape(1, -1)
    s5_d = w["s5_D"].reshape(1, -1)
    b_glu = w["s5_b_glu"].reshape(1, -1)
    conv_b = w["conv_b"].reshape(1, -1)
    dtb, alog, ssd_d = _pad_lanes(w["dt_bias"]), _pad_lanes(w["A_log"]), _pad_lanes(w["ssd_D"])
    ssd_nw = w["ssd_norm_w"].reshape(1, -1)
    ple_nw = w["ple_norm_w"].reshape(1, -1)
    fin_nw = w["final_norm_w"].reshape(1, -1)

    s5_args = (w["s5_A_re"], w["s5_A_im"], w["s5_log_dt"], w["s5_B_re"], w["s5_B_im"], w["s5_C_re"], w["s5_C_im"])
    small, small_vjp = jax.vjp(_s5_discretise, *s5_args)
    bbp, pwr, cnp, pwf = _s5_table_factors(*small)
    p1, p2 = _s5_scan_powers(w["s5_A_re"], w["s5_A_im"], w["s5_log_dt"], nsteps)

    hn, u_s5, z_s5, pssd, pdt = _in_proj_fwd(x, norm_w, w_main, w_dt)
    uflat = _flat_hs(u_s5, nc)
    yflat, hsave = _s5_core_fwd(uflat, bbp, pwr, cnp, pwf, p1, p2)
    yssm = _unflat_tk(yflat, nc)
    ys5 = _s5_post_fwd(yssm, u_s5, z_s5, s5_d, w["s5_w_glu"], b_glu)
    yssd, ypre, states, pre_act = _ssd_fwd(pssd, pdt, w["conv_w"], conv_b, dtb, alog, ssd_d, ssd_nw)
    (loss8, dys5, dyssd, dh1, n2, dgl2, dpp, g_ple_nw, g_fin_nw) = _head_fwd_bwd(
        x, ys5, yssd, p, target, w["w_out"], w["w_ple_gate"], w["w_ple_proj"], ple_nw, fin_nw)

    (dxbc, ddt, dzd, g_cw, g_cb, g_dtb, g_alog, g_ssd_d, g_ssd_nw) = _ssd_bwd(
        dyssd, ypre, pssd, pdt, states, pre_act, w["conv_w"], dtb, alog, ssd_d, ssd_nw)
    dzs, dyssm, a_glu, dgl1, g_bglu, g_s5d = _s5_post_bwd(dys5, yssm, u_s5, z_s5, s5_d, w["s5_w_glu"], b_glu)
    duflat, dbk, r12, q12, xy, uv, xy0, uv0, da8 = _s5_core_bwd(uflat, _flat_tk(dyssm, nc), hsave, bbp, pwr, cnp, pwf, p1, p2)
    da64 = jnp.concatenate([da8[:, 0, :S5_STATE] + da8[:, 0, S5_STATE:], da8[:, 1, S5_STATE:] - da8[:, 1, :S5_STATE]], axis=-1)
    g_s5 = small_vjp(_s5_small_cotangents(dbk, r12, q12, xy, uv, xy0, uv0, da64))
    gx, du, g_norm_w = _in_proj_bwd(x, norm_w, dh1, _unflat_hs(duflat, nc), dyssm, s5_d, dzs, dzd, dxbc, ddt, w_main, w_dt)

    g_w_in = jnp.concatenate([
        _matmul_tn(du, hn, "dw_in_u"), _matmul_tn(dzs, hn, "dw_in_zs"), _matmul_tn(dzd, hn, "dw_in_zd"),
        _matmul_tn(dxbc, hn, "dw_in_xbc"), _matmul_tn(ddt, hn, "dw_in_dt")[:SSD_HEADS]], axis=0)
    grads = {
        "norm_w": g_norm_w, "w_in": g_w_in,
        "s5_A_re": g_s5[0], "s5_A_im": g_s5[1], "s5_log_dt": g_s5[2], "s5_B_re": g_s5[3], "s5_B_im": g_s5[4],
        "s5_C_re": g_s5[5], "s5_C_im": g_s5[6], "s5_D": g_s5d, "s5_w_glu": _matmul_tn(a_glu, dgl1, "dw_glu"), "s5_b_glu": g_bglu,
        "conv_w": g_cw[:4], "conv_b": g_cb, "dt_bias": g_dtb[:, :SSD_HEADS], "A_log": g_alog[:, :SSD_HEADS],
        "ssd_D": g_ssd_d[:, :SSD_HEADS], "ssd_norm_w": g_ssd_nw,
        "w_out": jnp.concatenate([_matmul_tn(ys5, dh1, "dw_out_s5"), _matmul_tn(yssd, dh1, "dw_out_ssd")], axis=0),
        "ple_norm_w": g_ple_nw, "w_ple_gate": _matmul_tn(n2, dgl2, "dw_gate"), "w_ple_proj": _matmul_tn(p, dpp, "dw_proj"),
        "final_norm_w": g_fin_nw,
    }
    return loss8[0, 0], gx, grads


WEIGHTS = ("norm_w", "w_in", "s5_A_re", "s5_A_im", "s5_log_dt", "s5_B_re", "s5_B_im", "s5_C_re", "s5_C_im", "s5_D", "s5_w_glu",
           "s5_b_glu", "conv_w", "conv_b", "dt_bias", "A_log", "ssd_D", "ssd_norm_w", "w_out", "ple_norm_w", "w_ple_gate",
           "w_ple_proj", "final_norm_w")
BIG = {"w_in": ((1284, 1024), 0), "s5_w_glu": ((256, 1024), 0), "w_out": ((512, 1024), 0), "w_ple_gate": ((256, 1024), 0),
       "w_ple_proj": ((256, 256), 1)}
SMALL = {"norm_w": (1024,), "s5_A_re": (64, 64), "s5_A_im": (64, 64), "s5_log_dt": (64,), "s5_B_re": (64, 64, 16),
         "s5_B_im": (64, 64, 16), "s5_C_re": (64, 16, 64), "s5_C_im": (64, 16, 64), "s5_D": (1024,), "s5_b_glu": (1024,),
         "conv_w": (4, 2048), "conv_b": (2048,), "dt_bias": (16,), "A_log": (16,), "ssd_D": (16,), "ssd_norm_w": (1024,),
         "ple_norm_w": (1024,), "final_norm_w": (1024,)}
BIG_ROWS = {n: s[0] * s[1] // LANES for n, (s, _) in BIG.items()}
BIG_ROWS_TOTAL = sum(BIG_ROWS.values())
SMALL_TOTAL = sum(math.prod(s) for s in SMALL.values())
SMALL_PIECE_ROWS = -(-SMALL_TOTAL // (N_CHIPS * 16 * LANES)) * 16
HALF_ROWS = (BIG_ROWS_TOTAL + SMALL_PIECE_ROWS) // 2
SMALL_ROW0 = BIG_ROWS_TOTAL - HALF_ROWS


def _mesh_pos():
    return lax.axis_index("x"), lax.axis_index("y"), lax.axis_index("c")


def _other_chips(x, y):
    return [(1 - x, y), (x, 1 - y), (1 - x, 1 - y)]


def _comm_params():
    return pltpu.CompilerParams(has_side_effects=True)


def _all_gather_chips(wpack, cw):
    half = wpack.shape[0] // 2

    def body(w_ref, c_ref, wo_ref, co_ref, send_sems, recv_sems, fwd_send, fwd_recv, loc_sems):
        x, y, c = _mesh_pos()
        me = 2 * x + y
        sib = (x, y, 1 - c)
        mine = pl.ds(c * half, half)
        theirs = pl.ds((1 - c) * half, half)
        others = _other_chips(x, y)
        loc = [pltpu.make_async_copy(c_ref, co_ref.at[me], loc_sems.at[0])]
        for cp in loc:
            cp.start()

        def from_chip(k, chip, dev):
            return pltpu.make_async_remote_copy(w_ref.at[mine], wo_ref.at[chip, mine], send_sems.at[2 * k], recv_sems.at[2 * k],
                                                device_id=dev, device_id_type=MESH)

        def conv_from(k, chip, dev):
            return pltpu.make_async_remote_copy(c_ref, co_ref.at[chip], send_sems.at[2 * k + 1], recv_sems.at[2 * k + 1],
                                                device_id=dev, device_id_type=MESH)

        def passed(k, chip, rows):
            return pltpu.make_async_remote_copy(wo_ref.at[chip, rows], wo_ref.at[chip, rows], fwd_send.at[k], fwd_recv.at[k],
                                                device_id=sib, device_id_type=MESH)

        sends = []
        for k, (px, py) in enumerate(others):
            sends += [from_chip(k, me, (px, py, c)), conv_from(k, me, (px, py, c))]
        for cp in sends:
            cp.start()
        fwds = []
        for k, (px, py) in enumerate(others):
            chip = 2 * px + py
            from_chip(k, chip, (px, py, c)).wait_recv()
            fwds.append(passed(k, chip, mine))
            fwds[-1].start()
        for k, (px, py) in enumerate(others):
            chip = 2 * px + py
            passed(k, chip, theirs).wait_recv()
            conv_from(k, chip, (px, py, c)).wait_recv()
        for cp in sends + fwds:
            cp.wait_send()
        for cp in loc:
            cp.wait()

    return pl.pallas_call(
        body, name="all_gather_weights", in_specs=[ANY, ANY], out_specs=[ANY, ANY],
        out_shape=[jax.ShapeDtypeStruct((N_CHIPS,) + wpack.shape, wpack.dtype), jax.ShapeDtypeStruct((N_CHIPS,) + cw.shape, cw.dtype)],
        scratch_shapes=[pltpu.SemaphoreType.DMA((6,)), pltpu.SemaphoreType.DMA((6,)), pltpu.SemaphoreType.DMA((3,)),
                        pltpu.SemaphoreType.DMA((3,)), pltpu.SemaphoreType.DMA((1,))],
        compiler_params=_comm_params(),
    )(wpack, cw)


def _exchange_pair(gp):
    def body(g_ref, r_ref, send_sems, recv_sems):
        x, y, c = _mesh_pos()
        cps = [pltpu.make_async_remote_copy(g_ref.at[s, 1 - c], r_ref.at[s], send_sems.at[s], recv_sems.at[s],
                                            device_id=(x, y, 1 - c), device_id_type=MESH) for s in range(N_CHIPS)]
        for cp in cps:
            cp.start()
        for cp in cps:
            cp.wait()

    return pl.pallas_call(
        body, name="grad_exchange_pair", in_specs=[ANY], out_specs=ANY,
        out_shape=jax.ShapeDtypeStruct((N_CHIPS,) + gp.shape[2:], gp.dtype),
        scratch_shapes=[pltpu.SemaphoreType.DMA((N_CHIPS,)), pltpu.SemaphoreType.DMA((N_CHIPS,))],
        compiler_params=_comm_params(),
    )(gp)


def _pair_sum(mine, from_sibling):
    def body(a_ref, b_ref, bf_ref, tail_ref):
        s = a_ref[0] + b_ref[0]
        bf_ref[0] = s.astype(BF)
        tail_ref[0] = s[SMALL_ROW0:, :]

    piece = pl.BlockSpec((1, HALF_ROWS, LANES), lambda i: (i, 0, 0))
    return pl.pallas_call(
        body, name="grad_pair_sum", grid=(N_CHIPS,), in_specs=[piece, piece],
        out_specs=[piece, pl.BlockSpec((1, SMALL_PIECE_ROWS, LANES), lambda i: (i, 0, 0))],
        out_shape=[jax.ShapeDtypeStruct((N_CHIPS, HALF_ROWS, LANES), BF), jax.ShapeDtypeStruct((N_CHIPS, SMALL_PIECE_ROWS, LANES), F32)],
        compiler_params=_params(("parallel",)),
    )(mine, from_sibling)


def _exchange_chips(ps_bf, ps_tail):
    def body(p_ref, t_ref, r_ref, rt_ref, send_sems, recv_sems):
        x, y, c = _mesh_pos()
        cps = []
        for k, (px, py) in enumerate(_other_chips(x, y)):
            cps.append(pltpu.make_async_remote_copy(p_ref.at[2 * px + py], r_ref.at[k], send_sems.at[2 * k], recv_sems.at[2 * k],
                                                    device_id=(px, py, c), device_id_type=MESH))
            cps.append(pltpu.make_async_remote_copy(t_ref.at[2 * px + py], rt_ref.at[k], send_sems.at[2 * k + 1],
                                                    recv_sems.at[2 * k + 1], device_id=(px, py, c), device_id_type=MESH))
        for cp in cps:
            cp.start()
        for cp in cps:
            cp.wait()

    return pl.pallas_call(
        body, name="grad_exchange_chips", in_specs=[ANY, ANY], out_specs=[ANY, ANY],
        out_shape=[jax.ShapeDtypeStruct((N_CHIPS - 1,) + ps_bf.shape[1:], ps_bf.dtype),
                   jax.ShapeDtypeStruct((N_CHIPS - 1,) + ps_tail.shape[1:], ps_tail.dtype)],
        scratch_shapes=[pltpu.SemaphoreType.DMA((6,)), pltpu.SemaphoreType.DMA((6,))],
        compiler_params=_comm_params(),
    )(ps_bf, ps_tail)


def _chip_sum(own_bf, own_tail, others_bf, others_tail):
    def body(ob_ref, ot_ref, b_ref, t_ref, o_ref):
        acc = ob_ref[0:SMALL_ROW0, :].astype(F32)
        tail = ot_ref[...]
        for k in range(N_CHIPS - 1):
            acc = acc + b_ref[k, 0:SMALL_ROW0, :].astype(F32)
            tail = tail + t_ref[k]
        o_ref[0:SMALL_ROW0, :] = acc
        o_ref[SMALL_ROW0:, :] = tail

    return pl.pallas_call(
        body, name="grad_chip_sum", out_shape=jax.ShapeDtypeStruct((HALF_ROWS, LANES), F32),
        compiler_params=_params(),
    )(own_bf, own_tail, others_bf, others_tail)


def _swap_reduced_halves(gh):
    def body(g_ref, o_ref, send_sem, recv_sem):
        x, y, c = _mesh_pos()
        cp = pltpu.make_async_remote_copy(g_ref, o_ref, send_sem, recv_sem, device_id=(x, y, 1 - c), device_id_type=MESH)
        cp.start()
        cp.wait()

    return pl.pallas_call(
        body, name="grad_swap_halves", in_specs=[ANY], out_specs=ANY,
        out_shape=jax.ShapeDtypeStruct(gh.shape, gh.dtype),
        scratch_shapes=[pltpu.SemaphoreType.DMA, pltpu.SemaphoreType.DMA],
        compiler_params=_comm_params(),
    )(gh)


def _gather_small(second_half):
    def body(gs_ref, sm_ref, send_sems, recv_sems, loc_sem):
        x, y, c = _mesh_pos()
        me = 2 * x + y
        small = gs_ref.at[pl.ds(SMALL_ROW0, SMALL_PIECE_ROWS)]
        loc = pltpu.make_async_copy(small, sm_ref.at[me], loc_sem)
        loc.start()
        cps = [pltpu.make_async_remote_copy(small, sm_ref.at[me], send_sems.at[k], recv_sems.at[k],
                                            device_id=(px, py, c), device_id_type=MESH)
               for k, (px, py) in enumerate(_other_chips(x, y))]
        for cp in cps:
            cp.start()
        for cp in cps:
            cp.wait()
        loc.wait()

    return pl.pallas_call(
        body, name="grad_gather_small", in_specs=[ANY], out_specs=ANY,
        out_shape=jax.ShapeDtypeStruct((N_CHIPS, SMALL_PIECE_ROWS, LANES), second_half.dtype),
        scratch_shapes=[pltpu.SemaphoreType.DMA((3,)), pltpu.SemaphoreType.DMA((3,)), pltpu.SemaphoreType.DMA],
        compiler_params=_comm_params(),
    )(second_half)


def _pack_grads(grads):
    small = jnp.concatenate([grads[n].reshape(-1) for n in SMALL])
    small = jnp.pad(small, (0, N_CHIPS * SMALL_PIECE_ROWS * LANES - SMALL_TOTAL)).reshape(N_CHIPS, SMALL_PIECE_ROWS, LANES)
    pieces = []
    for s in range(N_CHIPS):
        rows = []
        for n, (shp, axis) in BIG.items():
            g = grads[n]
            blk = g[s * shp[0]:(s + 1) * shp[0], :] if axis == 0 else g[:, s * shp[1]:(s + 1) * shp[1]]
            rows.append(blk.reshape(-1, LANES))
        rows.append(small[s])
        pieces.append(jnp.concatenate(rows, axis=0).reshape(2, HALF_ROWS, LANES))
    return jnp.stack(pieces)


def _unpack_shard(first_half, second_half):
    rows = jnp.concatenate([first_half, second_half], axis=0)
    out, r0 = {}, 0
    for n, (shp, _) in BIG.items():
        out[n] = rows[r0:r0 + BIG_ROWS[n]].reshape(shp)
        r0 += BIG_ROWS[n]
    return out


def _unpack_small(sm):
    flat = sm.reshape(-1)
    out, o = {}, 0
    for n, shp in SMALL.items():
        k = math.prod(shp)
        out[n] = flat[o:o + k].reshape(shp)
        o += k
    return out


def _as_2d(a):
    n = a.size
    if a.ndim >= 2 and a.shape[-1] > 1024:
        return a.reshape(-1, a.shape[-1])
    if n % 1024 == 0:
        return a.reshape(n // 1024, 1024)
    return a.reshape(1, n)


def _adamw(w, g, m, v, name):
    shape = w.shape
    w2, g2, m2, v2 = (_as_2d(a) for a in (w, g, m, v))
    rows, cols = w2.shape
    rb = 256 if rows >= 512 else rows
    by_cols = rows % rb != 0

    def body(w_ref, g_ref, m_ref, v_ref, d_ref, mo_ref, vo_ref):
        gv = g_ref[...]
        mn = ADAM_B1 * m_ref[...] + (1.0 - ADAM_B1) * gv
        vn = ADAM_B2 * v_ref[...] + (1.0 - ADAM_B2) * (gv * gv)
        m_hat = mn / (1.0 - ADAM_B1 ** ADAM_STEP)
        v_hat = vn / (1.0 - ADAM_B2 ** ADAM_STEP)
        d_ref[...] = -ADAM_LR * (m_hat / (jnp.sqrt(v_hat) + ADAM_EPS) + ADAM_WD * w_ref[...])
        mo_ref[...] = mn
        vo_ref[...] = vn

    spec = pl.BlockSpec((rows, 256), lambda i: (0, i)) if by_cols else _row_spec(rb, cols)
    sds = jax.ShapeDtypeStruct((rows, cols), F32)
    d, mo, vo = pl.pallas_call(
        body, name=name, grid=(cols // 256 if by_cols else rows // rb,), in_specs=[spec] * 4, out_specs=[spec] * 3, out_shape=[sds] * 3,
        compiler_params=_params(("parallel",)),
    )(w2, g2, m2, v2)
    return d.reshape(shape), mo.reshape(shape), vo.reshape(shape)


def kernel(x, p, norm_w, w_in, s5_A_re, s5_A_im, s5_log_dt, s5_B_re, s5_B_im, s5_C_re, s5_C_im, s5_D, s5_w_glu, s5_b_glu, conv_w, conv_b, dt_bias, A_log, ssd_D, ssd_norm_w, w_out, ple_norm_w, w_ple_gate, w_ple_proj, final_norm_w, loss_target, m_norm_w, m_w_in, m_s5_A_re, m_s5_A_im, m_s5_log_dt, m_s5_B_re, m_s5_B_im, m_s5_C_re, m_s5_C_im, m_s5_D, m_s5_w_glu, m_s5_b_glu, m_conv_w, m_conv_b, m_dt_bias, m_A_log, m_ssd_D, m_ssd_norm_w, m_w_out, m_ple_norm_w, m_w_ple_gate, m_w_ple_proj, m_final_norm_w, v_norm_w, v_w_in, v_s5_A_re, v_s5_A_im, v_s5_log_dt, v_s5_B_re, v_s5_B_im, v_s5_C_re, v_s5_C_im, v_s5_D, v_s5_w_glu, v_s5_b_glu, v_conv_w, v_conv_b, v_dt_bias, v_A_log, v_ssd_D, v_ssd_norm_w, v_w_out, v_ple_norm_w, v_w_ple_gate, v_w_ple_proj, v_final_norm_w):
    given = (norm_w, w_in, s5_A_re, s5_A_im, s5_log_dt, s5_B_re, s5_B_im, s5_C_re, s5_C_im, s5_D, s5_w_glu, s5_b_glu, conv_w, conv_b,
             dt_bias, A_log, ssd_D, ssd_norm_w, w_out, ple_norm_w, w_ple_gate, w_ple_proj, final_norm_w)
    given_m = (m_norm_w, m_w_in, m_s5_A_re, m_s5_A_im, m_s5_log_dt, m_s5_B_re, m_s5_B_im, m_s5_C_re, m_s5_C_im, m_s5_D, m_s5_w_glu,
               m_s5_b_glu, m_conv_w, m_conv_b, m_dt_bias, m_A_log, m_ssd_D, m_ssd_norm_w, m_w_out, m_ple_norm_w, m_w_ple_gate,
               m_w_ple_proj, m_final_norm_w)
    given_v = (v_norm_w, v_w_in, v_s5_A_re, v_s5_A_im, v_s5_log_dt, v_s5_B_re, v_s5_B_im, v_s5_C_re, v_s5_C_im, v_s5_D, v_s5_w_glu,
               v_s5_b_glu, v_conv_w, v_conv_b, v_dt_bias, v_A_log, v_ssd_D, v_ssd_norm_w, v_w_out, v_ple_norm_w, v_w_ple_gate,
               v_w_ple_proj, v_final_norm_w)
    wts, mom, var = dict(zip(WEIGHTS, given)), dict(zip(WEIGHTS, given_m)), dict(zip(WEIGHTS, given_v))
    drop = lambda n, a: a if n == "final_norm_w" else a[0]

    shard2d = lambda n, a: a[0].T if n == "w_in" else drop(n, a)
    wpack = jnp.concatenate([shard2d(n, wts[n]).astype(BF).reshape(-1, LANES) for n in BIG], axis=0)
    wall, cwall = _all_gather_chips(wpack, drop("conv_w", wts["conv_w"]))
    chip = 2 * lax.axis_index("x") + lax.axis_index("y")
    full, r0 = {}, 0
    for n, (shp, axis) in BIG.items():
        blk = lax.dynamic_update_slice_in_dim(wall[:, r0:r0 + BIG_ROWS[n]].reshape((N_CHIPS,) + shp),
                                              shard2d(n, wts[n]).astype(BF)[None], chip, axis=0)
        full[n] = blk.reshape(N_CHIPS * shp[0], shp[1]) if axis == 0 else blk.transpose(1, 0, 2).reshape(shp[0], N_CHIPS * shp[1])
        r0 += BIG_ROWS[n]
    for n in SMALL:
        full[n] = drop(n, wts[n])
    full["conv_w"] = cwall.transpose(1, 0, 2).reshape(4, 2048)

    loss, gx, grads = _local_step(x[0], p[0, 0], loss_target[0], full)
    loss = lax.psum(loss, MESH_AXES)

    gp = _pack_grads({n: grads[n].reshape(SMALL[n]) if n in SMALL else grads[n] for n in WEIGHTS})
    c = lax.axis_index("c")
    from_sibling = _exchange_pair(gp)
    mine = lax.dynamic_index_in_dim(gp, c, axis=1, keepdims=False)
    ps_bf, ps_tail = _pair_sum(mine, from_sibling)
    others_bf, others_tail = _exchange_chips(ps_bf, ps_tail)
    own_bf = lax.dynamic_index_in_dim(ps_bf, chip, axis=0, keepdims=False)
    own_tail = lax.dynamic_index_in_dim(ps_tail, chip, axis=0, keepdims=False)
    reduced_half = _chip_sum(own_bf, own_tail, others_bf, others_tail)
    sibling_half = _swap_reduced_halves(reduced_half)
    first_half = jnp.where(c == 0, reduced_half, sibling_half)
    second_half = jnp.where(c == 0, sibling_half, reduced_half)
    sm = _gather_small(second_half)
    g_final = {**_unpack_small(sm), **_unpack_shard(first_half, second_half)}
    g_final["conv_w"] = lax.dynamic_slice_in_dim(g_final["conv_w"], chip * 512, 512, axis=1)

    outs_g, outs_d, outs_m, outs_v = [], [], [], []
    for n in WEIGHTS:
        if n == "w_in":
            res = _adamw(wts[n][0].T, g_final[n], mom[n][0].T, var[n][0].T, "adamw_" + n)
            g, d, mo, vo = (a.T[None] for a in (g_final[n],) + res)
        else:
            g = g_final[n].reshape(wts[n].shape)
            d, mo, vo = _adamw(wts[n], g, mom[n], var[n], "adamw_" + n)
        outs_g.append(g)
        outs_d.append(d)
        outs_m.append(mo)
        outs_v.append(vo)
    return (loss, gx[None], *outs_g, *outs_d, *outs_m, *outs_v)
```

```python
import functools
import math

import jax
import jax.numpy as jnp
from jax import lax
from jax.experimental import pallas as pl
from jax.experimental.pallas import tpu as pltpu

F32 = jnp.float32
BF = jnp.bfloat16
EPS = 1e-6
CHUNK = 64
D_MODEL = 1024
S5_GROUPS = 64
S5_CH = 16
S5_STATE = 64
SSD_HEADS = 16
SSD_HEAD_DIM = 64
SSD_GROUPS = 4
SSD_STATE = 128
D_MAIN = 5120
LANES = 128
TOKEN_TILE = 256
VMEM_LIMIT = 56 * 1024 * 1024
MESH_AXES = ("x", "y", "c")
N_CHIPS = 4
ADAM_LR, ADAM_B1, ADAM_B2, ADAM_EPS, ADAM_WD, ADAM_STEP = 0.001, 0.9, 0.999, 1e-08, 0.01, 10
MESH = pl.DeviceIdType.MESH
ANY = pl.BlockSpec(memory_space=pl.ANY)


def _dot(a, b):
    return jnp.dot(a, b, preferred_element_type=F32)


def _dot_nt(a, b):
    return lax.dot_general(a, b, (((1,), (1,)), ((), ())), preferred_element_type=F32)


def _dot_tn(a, b):
    return lax.dot_general(a, b, (((0,), (0,)), ((), ())), preferred_element_type=F32)


def _sigmoid(x):
    return 1.0 / (1.0 + jnp.exp(-x))


def _softplus(x):
    return jnp.maximum(x, 0.0) + jnp.log(1.0 + jnp.exp(-jnp.abs(x)))


_GELU_C = math.sqrt(2.0 / math.pi)


def _gelu(x):
    return 0.5 * x * (1.0 + jnp.tanh(_GELU_C * (x + 0.044715 * x * x * x)))


def _gelu_grad(x):
    th = jnp.tanh(_GELU_C * (x + 0.044715 * x * x * x))
    return 0.5 * (1.0 + th) + 0.5 * x * (1.0 - th * th) * _GELU_C * (1.0 + 3.0 * 0.044715 * x * x)


def _params(sem=None):
    return pltpu.CompilerParams(dimension_semantics=sem, vmem_limit_bytes=VMEM_LIMIT)


def _row_spec(tl, width, col=0):
    return pl.BlockSpec((tl, width), lambda i, col=col: (i, col))


def _const_spec(shape):
    nd = len(shape)
    return pl.BlockSpec(shape, lambda *_: (0,) * nd)


def _in_proj_fwd(x, norm_w, w_main, w_dt):
    L = x.shape[0]
    tl = min(TOKEN_TILE, L)

    def body(x_ref, nw_ref, wm_ref, wd_ref, hn_ref, u_ref, zs_ref, pssd_ref, pd_ref):
        xv = x_ref[...]
        r = lax.rsqrt(jnp.mean(xv * xv, axis=-1, keepdims=True) + EPS)
        hn = (xv * r * nw_ref[...]).astype(BF)
        hn_ref[...] = hn
        for j, o_ref in enumerate((u_ref, zs_ref)):
            o_ref[...] = _dot_nt(hn, wm_ref[j * 1024:(j + 1) * 1024, :]).astype(BF)
        for j in range(3):
            pssd_ref[:, j * 1024:(j + 1) * 1024] = _dot_nt(hn, wm_ref[(j + 2) * 1024:(j + 3) * 1024, :])
        pd_ref[...] = _dot_nt(hn, wd_ref[...])

    return pl.pallas_call(
        body, name="in_proj_fwd", grid=(L // tl,),
        in_specs=[_row_spec(tl, D_MODEL), _const_spec((1, D_MODEL)), _const_spec((D_MAIN, D_MODEL)), _const_spec((LANES, D_MODEL))],
        out_specs=[_row_spec(tl, D_MODEL), _row_spec(tl, D_MODEL), _row_spec(tl, D_MODEL), _row_spec(tl, 3072), _row_spec(tl, LANES)],
        out_shape=[jax.ShapeDtypeStruct((L, D_MODEL), BF), jax.ShapeDtypeStruct((L, D_MODEL), BF), jax.ShapeDtypeStruct((L, D_MODEL), BF),
                   jax.ShapeDtypeStruct((L, 3072), F32),
                   jax.ShapeDtypeStruct((L, LANES), F32)],
        compiler_params=_params(("arbitrary",)),
    )(x, norm_w, w_main, w_dt)


def _in_proj_bwd(x, norm_w, dh1, du_flat, dyssm, s5_d, dzs, dzd, dxbc, ddt, w_main, w_dt):
    L = x.shape[0]
    tl = min(TOKEN_TILE, L)

    def body(x_ref, nw_ref, dh1_ref, duf_ref, dys_ref, d_ref, dzs_ref, dzd_ref, dxbc_ref, ddt_ref, wm_ref, wd_ref,
             gx_ref, du_ref, gnw_ref):
        @pl.when(pl.program_id(0) == 0)
        def _():
            gnw_ref[...] = jnp.zeros_like(gnw_ref)

        du = (duf_ref[...].astype(F32) + dys_ref[...].astype(F32) * d_ref[...]).astype(BF)
        du_ref[...] = du
        dhn = _dot(du, wm_ref[0:1024, :])
        dhn += _dot(dzs_ref[...], wm_ref[1024:2048, :])
        dhn += _dot(dzd_ref[...], wm_ref[2048:3072, :])
        dhn += _dot(dxbc_ref[...], wm_ref[3072:5120, :])
        dhn += _dot(ddt_ref[...].astype(BF), wd_ref[...])
        xv = x_ref[...]
        r = lax.rsqrt(jnp.mean(xv * xv, axis=-1, keepdims=True) + EPS)
        xh = xv * r
        gnw_ref[...] += jnp.sum(dhn * xh, axis=0, keepdims=True)
        g = dhn * nw_ref[...]
        gx_ref[...] = dh1_ref[...] + r * (g - xh * jnp.mean(g * xh, axis=-1, keepdims=True))

    return pl.pallas_call(
        body, name="in_proj_bwd", grid=(L // tl,),
        in_specs=[_row_spec(tl, D_MODEL), _const_spec((1, D_MODEL)), _row_spec(tl, D_MODEL), _row_spec(tl, D_MODEL),
                  _row_spec(tl, D_MODEL), _const_spec((1, D_MODEL)), _row_spec(tl, D_MODEL), _row_spec(tl, D_MODEL),
                  _row_spec(tl, 2048), _row_spec(tl, LANES), _const_spec((D_MAIN, D_MODEL)), _const_spec((LANES, D_MODEL))],
        out_specs=[_row_spec(tl, D_MODEL), _row_spec(tl, D_MODEL), _const_spec((1, D_MODEL))],
        out_shape=[jax.ShapeDtypeStruct((L, D_MODEL), F32), jax.ShapeDtypeStruct((L, D_MODEL), BF), jax.ShapeDtypeStruct((1, D_MODEL), F32)],
        compiler_params=_params(("arbitrary",)),
    )(x, norm_w, dh1, du_flat, dyssm, s5_d, dzs, dzd, dxbc, ddt, w_main, w_dt)


def _matmul_tn(a, b, name):
    L, M = a.shape
    N = b.shape[1]
    tm, tn, tk = min(M, 1024), min(N, 1024), min(L, 1024)

    def body(a_ref, b_ref, o_ref):
        @pl.when(pl.program_id(2) == 0)
        def _():
            o_ref[...] = jnp.zeros_like(o_ref)

        o_ref[...] += _dot_tn(a_ref[...].astype(BF), b_ref[...].astype(BF))

    return pl.pallas_call(
        body, name=name, grid=(M // tm, N // tn, L // tk),
        in_specs=[pl.BlockSpec((tk, tm), lambda i, j, k: (k, i)), pl.BlockSpec((tk, tn), lambda i, j, k: (k, j))],
        out_specs=pl.BlockSpec((tm, tn), lambda i, j, k: (i, j)),
        out_shape=jax.ShapeDtypeStruct((M, N), F32),
        compiler_params=_params(("parallel", "parallel", "arbitrary")),
    )(a, b)


def _s5_discretise(a_re, a_im, log_dt, b_re, b_im, c_re, c_im):
    dt = jnp.exp(log_dt)[:, None]
    tau = jnp.arange(CHUNK + 1, dtype=F32)
    mag = jnp.exp((a_re * dt)[:, :, None] * tau)
    ang = (a_im * dt)[:, :, None] * tau
    pw_re, pw_im = mag * jnp.cos(ang), mag * jnp.sin(ang)
    er, ei = pw_re[:, :, 1] - 1.0, pw_im[:, :, 1]
    den = a_re * a_re + a_im * a_im
    beta_re, beta_im = (er * a_re + ei * a_im) / den, (ei * a_re - er * a_im) / den
    bb_re = (beta_re[:, :, None] * b_re - beta_im[:, :, None] * b_im).transpose(0, 2, 1)
    bb_im = (beta_re[:, :, None] * b_im + beta_im[:, :, None] * b_re).transpose(0, 2, 1)
    return bb_re, bb_im, c_re, c_im, pw_re, pw_im


def _s5_table_factors(bb_re, bb_im, c_re, c_im, pw_re, pw_im):
    pr = pw_re[:, :, CHUNK - 1::-1].transpose(0, 2, 1)
    pi = pw_im[:, :, CHUNK - 1::-1].transpose(0, 2, 1)
    bbp = jnp.concatenate([bb_re, bb_im], axis=-1)
    pwr = jnp.concatenate([pr, pi], axis=-1)
    cnp = jnp.concatenate([c_re, c_im], axis=-1)
    pwf = jnp.concatenate([pw_re.transpose(0, 2, 1), pw_im.transpose(0, 2, 1)], axis=-1)
    return bbp, pwr, cnp, jnp.pad(pwf, ((0, 0), (0, 7), (0, 0)))


def _s5_small_cotangents(dbk, r12, q12, xy, uv, xy0, uv0, da64):
    n = S5_STATE
    fold = lambda a: a[..., :n] + a[..., n:]
    fold_m = lambda a: a[..., n:] - a[..., :n]
    dbb_re = fold(r12[:, :S5_CH]) + dbk[..., :n]
    dbb_im = fold_m(r12[:, S5_CH:]) + dbk[..., n:]
    dpr, dpi = fold(q12[:, :CHUNK]), fold_m(q12[:, CHUNK:])
    dc_re = -fold_m(xy[:, :S5_CH]) - fold_m(xy0[:, :S5_CH])
    dc_im = -fold(xy[:, S5_CH:]) - fold(xy0[:, S5_CH:])
    dp1_re, dp1_im = -fold_m(uv[:, :CHUNK]), -fold(uv[:, CHUNK:])
    dp0_re, dp0_im = -fold_m(uv0[:, :CHUNK]), -fold(uv0[:, CHUNK:])
    zero = jnp.zeros((S5_GROUPS, n, 1), F32)
    dpw_re = (jnp.concatenate([(dpr[:, ::-1] + dp0_re).transpose(0, 2, 1), zero], axis=-1)
              + jnp.concatenate([zero, dp1_re.transpose(0, 2, 1)], axis=-1)).at[:, :, CHUNK].add(da64[:, :n])
    dpw_im = (jnp.concatenate([(dpi[:, ::-1] + dp0_im).transpose(0, 2, 1), zero], axis=-1)
              + jnp.concatenate([zero, dp1_im.transpose(0, 2, 1)], axis=-1)).at[:, :, CHUNK].add(da64[:, n:])
    return dbb_re, dbb_im, dc_re, dc_im, dpw_re, dpw_im


def _s5_scan_powers(a_re, a_im, log_dt, nsteps):
    dt = jnp.exp(log_dt)[:, None]
    steps = (CHUNK * (2.0 ** jnp.arange(8, dtype=F32)))[None, :, None]
    mag = jnp.exp((a_re * dt)[:, None, :] * steps)
    ang = (a_im * dt)[:, None, :] * steps
    re, im = mag * jnp.cos(ang), mag * jnp.sin(ang)
    del nsteps
    return jnp.concatenate([re, re], -1), jnp.concatenate([-im, im], -1)


def _build_toeplitz(kmat, tg_ref):
    lane = lax.broadcasted_iota(jnp.int32, (CHUNK, CHUNK * S5_CH), 1)
    srow = lax.broadcasted_iota(jnp.int32, (CHUNK, CHUNK * S5_CH), 0)
    keep = lane >= S5_CH * srow
    for h in range(S5_CH):
        row = jnp.broadcast_to(kmat[h:h + 1, :], (CHUNK, CHUNK * S5_CH))
        rolled = pltpu.roll(row, 0, 1, stride=S5_CH, stride_axis=0)
        tg_ref[h * CHUNK:(h + 1) * CHUNK, :] = jnp.where(keep, rolled, 0.0).astype(BF)


def _swap_halves(x):
    return pltpu.roll(x, S5_STATE, 1)


def _hi_lo(x):
    hi = x.astype(BF)
    return hi, (x - hi.astype(F32)).astype(BF)


def _dot3(dot, a, b):
    a_hi, a_lo = _hi_lo(a)
    b_hi, b_lo = _hi_lo(b)
    return dot(a_hi, b_hi) + dot(a_hi, b_lo) + dot(a_lo, b_hi)


def _build_state_tables(bbp_ref, pwr_ref, cnp_ref, pwf_ref, wst_ref, wofft_ref, w0_ref):
    lane = lax.broadcasted_iota(jnp.int32, (1, 2 * S5_STATE), 1)
    left = lane < S5_STATE
    pwr = pwr_ref[0]
    pwr_sw = _swap_halves(pwr)
    for hh in range(S5_CH):
        bb = bbp_ref[0, hh:hh + 1, :]
        bb_sw = _swap_halves(bb)
        wst_ref[hh * CHUNK:(hh + 1) * CHUNK, :] = (jnp.where(left, bb, bb_sw) * pwr
                                                   + jnp.where(left, -bb_sw, bb) * pwr_sw).astype(BF)
    cn = cnp_ref[0]
    cn_sw = _swap_halves(cn)
    c_a = jnp.where(left, cn, -cn_sw)
    c_b = jnp.where(left, -cn_sw, -cn)
    p_prev = pwf_ref[0, 0:1, :]
    for t in range(CHUNK):
        p = pwf_ref[0, t + 1:t + 2, :]
        w0_ref[t * S5_CH:(t + 1) * S5_CH, :] = c_a * p_prev + c_b * _swap_halves(p_prev)
        wofft_ref[t * S5_CH:(t + 1) * S5_CH, :] = (c_a * p + c_b * _swap_halves(p)).astype(BF)
        p_prev = p
    return _dot3(_dot_nt, bbp_ref[0], w0_ref[...])


def _reduce_table_cotangent(d_table, cnp_ref, pwf_ref, first_power):
    cn = cnp_ref[0]
    cn_sw = _swap_halves(cn)
    xa = jnp.zeros((S5_CH, 2 * S5_STATE), F32)
    ya = jnp.zeros((S5_CH, 2 * S5_STATE), F32)
    ru_rows, rv_rows = [], []
    for t in range(CHUNK):
        blk = d_table[t * S5_CH:(t + 1) * S5_CH, :]
        p = pwf_ref[0, t + first_power:t + first_power + 1, :]
        xa = xa + blk * p
        ya = ya + blk * _swap_halves(p)
        ru_rows.append(jnp.sum(blk * cn, axis=0, keepdims=True))
        rv_rows.append(jnp.sum(blk * cn_sw, axis=0, keepdims=True))
    return jnp.concatenate([xa, ya], axis=0), jnp.concatenate(ru_rows + rv_rows, axis=0)


def _s5_core_fwd(uflat, bbp, pwr, cnp, pwf, p1, p2):
    G, nc, W = uflat.shape
    nsteps = max(1, (nc - 1).bit_length())

    def body(u_ref, bbp_ref, pwr_ref, cnp_ref, pwf_ref, p1_ref, p2_ref, y_ref, h_ref, tg_ref, wst_ref, wofft_ref, w0_ref):
        _build_toeplitz(_build_state_tables(bbp_ref, pwr_ref, cnp_ref, pwf_ref, wst_ref, wofft_ref, w0_ref), tg_ref)
        u = u_ref[0]
        x = _dot(u, wst_ref[...])
        row = lax.broadcasted_iota(jnp.int32, x.shape, 0)
        d = 1
        for k in range(nsteps):
            sh = jnp.where(row >= d, pltpu.roll(x, d, 0), 0.0)
            x = x + p1_ref[0, k:k + 1, :] * sh + p2_ref[0, k:k + 1, :] * _swap_halves(sh)
            d *= 2
        h = jnp.where(row >= 1, pltpu.roll(x, 1, 0), 0.0)
        h_ref[0] = h
        y = _dot(u, tg_ref[...]) + _dot_nt(h.astype(BF), wofft_ref[...])
        y_ref[0] = y.astype(BF)

    spec_g = lambda a, b: pl.BlockSpec((1, a, b), lambda g: (g, 0, 0))
    return pl.pallas_call(
        body, name="s5_core_fwd", grid=(G,),
        in_specs=[spec_g(nc, W), spec_g(S5_CH, 2 * S5_STATE), spec_g(CHUNK, 2 * S5_STATE),
                  spec_g(S5_CH, 2 * S5_STATE), spec_g(CHUNK + 8, 2 * S5_STATE), spec_g(8, 2 * S5_STATE), spec_g(8, 2 * S5_STATE)],
        out_specs=[spec_g(nc, W), spec_g(nc, 2 * S5_STATE)],
        out_shape=[jax.ShapeDtypeStruct((G, nc, W), BF), jax.ShapeDtypeStruct((G, nc, 2 * S5_STATE), F32)],
        scratch_shapes=[pltpu.VMEM((W, W), BF), pltpu.VMEM((W, 2 * S5_STATE), BF), pltpu.VMEM((W, 2 * S5_STATE), BF),
                        pltpu.VMEM((W, 2 * S5_STATE), F32)],
        compiler_params=_params(("arbitrary",)),
    )(uflat, bbp, pwr, cnp, pwf, p1, p2)


def _s5_core_bwd(uflat, dyflat, hsave, bbp, pwr, cnp, pwf, p1, p2):
    G, nc, W = uflat.shape
    nsteps = max(1, (nc - 1).bit_length())

    def body(u_ref, dy_ref, h_ref, bbp_ref, pwr_ref, cnp_ref, pwf_ref, p1_ref, p2_ref,
             du_ref, dbk_ref, r12_ref, q12_ref, xy_ref, uv_ref, xy0_ref, uv0_ref, da_ref,
             tg_ref, flip_ref, wst_ref, wofft_ref, w0_ref):
        @pl.when(pl.program_id(0) == 0)
        def _():
            r = lax.broadcasted_iota(jnp.int32, (W, W), 0)
            c = lax.broadcasted_iota(jnp.int32, (W, W), 1)
            flip_ref[...] = (((r >> 6) == (c >> 6)) & ((r & (CHUNK - 1)) + (c & (CHUNK - 1)) == CHUNK - 1)).astype(BF)

        _build_toeplitz(_build_state_tables(bbp_ref, pwr_ref, cnp_ref, pwf_ref, wst_ref, wofft_ref, w0_ref), tg_ref)
        u = u_ref[0]
        dy = dy_ref[0]
        h = h_ref[0]
        gh = _dot(dy, wofft_ref[...])
        row = lax.broadcasted_iota(jnp.int32, gh.shape, 0)
        x = jnp.where(row < nc - 1, pltpu.roll(gh, nc - 1, 0), 0.0)
        d = 1
        for k in range(nsteps):
            sh = jnp.where(row < nc - d, pltpu.roll(x, nc - d, 0), 0.0)
            x = x + p1_ref[0, k:k + 1, :] * sh - p2_ref[0, k:k + 1, :] * _swap_halves(sh)
            d *= 2
        gs = x.astype(BF)
        du_ref[0] = (_dot_nt(dy, tg_ref[...]) + _dot_nt(gs, wst_ref[...])).astype(BF)

        dwst = _dot_tn(u, gs)
        pwr = pwr_ref[0]
        pwr_sw = _swap_halves(pwr)
        q1 = jnp.zeros((CHUNK, 2 * S5_STATE), F32)
        q2 = jnp.zeros((CHUNK, 2 * S5_STATE), F32)
        r1_rows, r2_rows = [], []
        for hh in range(S5_CH):
            blk = dwst[hh * CHUNK:(hh + 1) * CHUNK, :]
            r1_rows.append(jnp.sum(blk * pwr, axis=0, keepdims=True))
            r2_rows.append(jnp.sum(blk * pwr_sw, axis=0, keepdims=True))
            bb = bbp_ref[0, hh:hh + 1, :]
            q1 = q1 + blk * bb
            q2 = q2 + blk * _swap_halves(bb)
        r12_ref[0] = jnp.concatenate(r1_rows + r2_rows, axis=0)
        q12_ref[0] = jnp.concatenate([q1, q2], axis=0)

        xy_ref[0], uv_ref[0] = _reduce_table_cotangent(_dot_tn(dy, h.astype(BF)), cnp_ref, pwf_ref, 1)
        r1 = jnp.sum(x * h, axis=0, keepdims=True)
        r2 = jnp.sum(x * _swap_halves(h), axis=0, keepdims=True)
        da_ref[0] = jnp.concatenate([r1, r2, jnp.zeros((6, 2 * S5_STATE), F32)], axis=0)
        lane = lax.broadcasted_iota(jnp.int32, (CHUNK, W), 1)
        srow = lax.broadcasted_iota(jnp.int32, (CHUNK, W), 0)
        keep = lane < S5_CH * (srow + 1)
        ur = _dot(u, flip_ref[...]).astype(BF)
        dk_rows = []
        for hh in range(S5_CH):
            dt_h = _dot_tn(ur[:, hh * CHUNK:(hh + 1) * CHUNK], dy)
            back = pltpu.roll(dt_h, S5_CH, 1, stride=S5_CH, stride_axis=0)
            dk_rows.append(jnp.sum(jnp.where(keep, back, 0.0), axis=0, keepdims=True))
        dk = jnp.concatenate(dk_rows, axis=0)
        dbk_ref[0] = _dot3(_dot, dk, w0_ref[...])
        xy0_ref[0], uv0_ref[0] = _reduce_table_cotangent(_dot3(_dot_tn, dk, bbp_ref[0]), cnp_ref, pwf_ref, 0)

    spec_g = lambda a, b: pl.BlockSpec((1, a, b), lambda g: (g, 0, 0))
    small_c = jax.ShapeDtypeStruct((G, 2 * S5_CH, 2 * S5_STATE), F32)
    small_p = jax.ShapeDtypeStruct((G, 2 * CHUNK, 2 * S5_STATE), F32)
    return pl.pallas_call(
        body, name="s5_core_bwd", grid=(G,),
        in_specs=[spec_g(nc, W), spec_g(nc, W), spec_g(nc, 2 * S5_STATE), spec_g(S5_CH, 2 * S5_STATE),
                  spec_g(CHUNK, 2 * S5_STATE), spec_g(S5_CH, 2 * S5_STATE), spec_g(CHUNK + 8, 2 * S5_STATE),
                  spec_g(8, 2 * S5_STATE), spec_g(8, 2 * S5_STATE)],
        out_specs=[spec_g(nc, W), spec_g(S5_CH, 2 * S5_STATE), spec_g(2 * S5_CH, 2 * S5_STATE), spec_g(2 * CHUNK, 2 * S5_STATE),
                   spec_g(2 * S5_CH, 2 * S5_STATE), spec_g(2 * CHUNK, 2 * S5_STATE),
                   spec_g(2 * S5_CH, 2 * S5_STATE), spec_g(2 * CHUNK, 2 * S5_STATE), spec_g(8, 2 * S5_STATE)],
        out_shape=[jax.ShapeDtypeStruct((G, nc, W), BF), jax.ShapeDtypeStruct((G, S5_CH, 2 * S5_STATE), F32),
                   small_c, small_p, small_c, small_p, small_c, small_p, jax.ShapeDtypeStruct((G, 8, 2 * S5_STATE), F32)],
        scratch_shapes=[pltpu.VMEM((W, W), BF), pltpu.VMEM((W, W), BF), pltpu.VMEM((W, 2 * S5_STATE), BF),
                        pltpu.VMEM((W, 2 * S5_STATE), BF), pltpu.VMEM((W, 2 * S5_STATE), F32)],
        compiler_params=_params(("arbitrary",)),
    )(uflat, dyflat, hsave, bbp, pwr, cnp, pwf, p1, p2)


def _flat_hs(a, nc):
    return a.reshape(nc, CHUNK, S5_GROUPS, S5_CH).transpose(2, 0, 3, 1).reshape(S5_GROUPS, nc, CHUNK * S5_CH)


def _unflat_hs(a, nc):
    return a.reshape(S5_GROUPS, nc, S5_CH, CHUNK).transpose(1, 3, 0, 2).reshape(nc * CHUNK, D_MODEL)


def _flat_tk(a, nc):
    return a.reshape(nc, CHUNK, S5_GROUPS, S5_CH).transpose(2, 0, 1, 3).reshape(S5_GROUPS, nc, CHUNK * S5_CH)


def _unflat_tk(a, nc):
    return a.reshape(S5_GROUPS, nc, CHUNK, S5_CH).transpose(1, 2, 0, 3).reshape(nc * CHUNK, D_MODEL)


def _s5_post_fwd(yssm, u, zs, s5_d, w_glu, b_glu):
    L = yssm.shape[0]
    tl = min(TOKEN_TILE, L)

    def body(ys_ref, u_ref, z_ref, d_ref, wg_ref, bg_ref, o_ref):
        u = u_ref[...].astype(F32)
        a = _gelu(ys_ref[...].astype(F32) + d_ref[...] * u)
        y = a * _sigmoid(_dot(a.astype(BF), wg_ref[...]) + bg_ref[...])
        z = z_ref[...].astype(F32)
        o_ref[...] = (y * z * _sigmoid(z)).astype(BF)

    return pl.pallas_call(
        body, name="s5_post_fwd", grid=(L // tl,),
        in_specs=[_row_spec(tl, D_MODEL), _row_spec(tl, D_MODEL), _row_spec(tl, D_MODEL), _const_spec((1, D_MODEL)),
                  _const_spec((D_MODEL, D_MODEL)), _const_spec((1, D_MODEL))],
        out_specs=_row_spec(tl, D_MODEL),
        out_shape=jax.ShapeDtypeStruct((L, D_MODEL), BF),
        compiler_params=_params(("arbitrary",)),
    )(yssm, u, zs, s5_d, w_glu, b_glu)


def _s5_post_bwd(dys5, yssm, u, zs, s5_d, w_glu, b_glu):
    L = yssm.shape[0]
    tl = min(TOKEN_TILE, L)

    def body(dy_ref, ys_ref, u_ref, z_ref, d_ref, wg_ref, bg_ref, dz_ref, dys_ref, a_ref, dgl_ref, dbg_ref, dd_ref):
        @pl.when(pl.program_id(0) == 0)
        def _():
            dbg_ref[...] = jnp.zeros_like(dbg_ref)
            dd_ref[...] = jnp.zeros_like(dd_ref)

        u = u_ref[...].astype(F32)
        y0 = ys_ref[...].astype(F32) + d_ref[...] * u
        a = _gelu(y0)
        a_bf = a.astype(BF)
        sg = _sigmoid(_dot(a_bf, wg_ref[...]) + bg_ref[...])
        y = a * sg
        z = z_ref[...].astype(F32)
        sz = _sigmoid(z)
        dout = dy_ref[...].astype(F32)
        dz_ref[...] = (dout * y * sz * (1.0 + z * (1.0 - sz))).astype(BF)
        dyv = dout * z * sz
        dgl = dyv * a * sg * (1.0 - sg)
        dgl_bf = dgl.astype(BF)
        da = dyv * sg + _dot_nt(dgl_bf, wg_ref[...])
        dy0 = da * _gelu_grad(y0)
        dbg_ref[...] += jnp.sum(dgl, axis=0, keepdims=True)
        dd_ref[...] += jnp.sum(dy0 * u, axis=0, keepdims=True)
        dys_ref[...] = dy0.astype(BF)
        a_ref[...] = a_bf
        dgl_ref[...] = dgl_bf

    big = jax.ShapeDtypeStruct((L, D_MODEL), BF)
    vec = jax.ShapeDtypeStruct((1, D_MODEL), F32)
    return pl.pallas_call(
        body, name="s5_post_bwd", grid=(L // tl,),
        in_specs=[_row_spec(tl, D_MODEL), _row_spec(tl, D_MODEL), _row_spec(tl, D_MODEL), _row_spec(tl, D_MODEL),
                  _const_spec((1, D_MODEL)), _const_spec((D_MODEL, D_MODEL)), _const_spec((1, D_MODEL))],
        out_specs=[_row_spec(tl, D_MODEL)] * 4 + [_const_spec((1, D_MODEL))] * 2,
        out_shape=[big, big, big, big, vec, vec],
        compiler_params=_params(("arbitrary",)),
    )(dys5, yssm, u, zs, s5_d, w_glu, b_glu)


def _cumsum_rows(a):
    row = lax.broadcasted_iota(jnp.int32, a.shape, 0)
    d = 1
    while d < a.shape[0]:
        a = a + jnp.where(row >= d, pltpu.roll(a, d, 0), 0.0)
        d *= 2
    return a


def _rev_cumsum_rows(a):
    n = a.shape[0]
    row = lax.broadcasted_iota(jnp.int32, a.shape, 0)
    d = 1
    while d < n:
        a = a + jnp.where(row < n - d, pltpu.roll(a, n - d, 0), 0.0)
        d *= 2
    return a


def _ssd_fill_padded(first, xs_ref, bc_ref, hx_ref, hb_ref, xp_ref, tl):
    hal = jnp.concatenate([hx_ref[...], hb_ref[...]], axis=1)
    xp_ref[0:8, :] = jnp.where(first, 0.0, hal)
    xp_ref[8:8 + tl, 0:1024] = xs_ref[...]
    xp_ref[8:8 + tl, 1024:2048] = bc_ref[...]


def _ssd_conv_fwd(first, xs_ref, bc_ref, hx_ref, hb_ref, cw_ref, cb_ref, xp_ref, tl):
    _ssd_fill_padded(first, xs_ref, bc_ref, hx_ref, hb_ref, xp_ref, tl)
    pre = cb_ref[...] + cw_ref[0:1, :] * xp_ref[5:5 + tl, :]
    for k in range(1, 4):
        pre = pre + cw_ref[k:k + 1, :] * xp_ref[5 + k:5 + k + tl, :]
    return pre


def _onehot_lane(h):
    return (lax.broadcasted_iota(jnp.int32, (1, LANES), 1) == h).astype(F32)


def _dot_exact(x, e):
    hi = x.astype(BF)
    r = x - hi.astype(F32)
    mid = r.astype(BF)
    lo = (r - mid.astype(F32)).astype(BF)
    return _dot(hi, e) + _dot(mid, e) + _dot(lo, e)


def _head_expand_matrices():
    e = lax.broadcasted_iota(jnp.int32, (LANES, D_MODEL), 0) == (lax.broadcasted_iota(jnp.int32, (LANES, D_MODEL), 1) >> 6)
    et = (lax.broadcasted_iota(jnp.int32, (D_MODEL, LANES), 0) >> 6) == lax.broadcasted_iota(jnp.int32, (D_MODEL, LANES), 1)
    return e.astype(BF), et.astype(BF)


def _group_masks():
    r64 = lax.broadcasted_iota(jnp.int32, (4 * CHUNK, CHUNK), 0)
    causal4 = (r64 & (CHUNK - 1)) >= lax.broadcasted_iota(jnp.int32, (4 * CHUNK, CHUNK), 1)
    r256 = lax.broadcasted_iota(jnp.int32, (4 * CHUNK, 4 * SSD_HEAD_DIM), 0)
    same = (r256 >> 6) == (lax.broadcasted_iota(jnp.int32, (4 * CHUNK, 4 * SSD_HEAD_DIM), 1) >> 6)
    return causal4, same


def _group_decay(acs, acs_t, j, causal4):
    col = jnp.concatenate([acs[:, 4 * j + hh:4 * j + hh + 1] for hh in range(4)], axis=0)
    rowv = jnp.concatenate([jnp.broadcast_to(acs_t[4 * j + hh:4 * j + hh + 1, :], (CHUNK, CHUNK)) for hh in range(4)], axis=0)
    return jnp.where(causal4, jnp.exp(col - rowv), 0.0)


def _group_last_decay(acs_t, j):
    return jnp.concatenate([jnp.broadcast_to(jnp.exp(acs_t[4 * j + hh:4 * j + hh + 1, CHUNK - 1:CHUNK]), (SSD_HEAD_DIM, 1))
                            for hh in range(4)], axis=0)


def _fold_heads(r):
    return r[0:CHUNK] + r[CHUNK:2 * CHUNK] + r[2 * CHUNK:3 * CHUNK] + r[3 * CHUNK:4 * CHUNK]


def _ssd_specs_in(tl, nt, rev):
    t_of = (lambda i: nt - 1 - i) if rev else (lambda i: i)
    rows = lambda w, col: pl.BlockSpec((tl, w), lambda i: (t_of(i), col))
    halo = lambda col: pl.BlockSpec((8, 1024), lambda i: (jnp.maximum(t_of(i) * (tl // 8) - 1, 0), col))
    return t_of, rows, halo


def _ssd_fwd(proj, pdt, conv_w, conv_b, dt_bias, a_log, ssd_d, norm_w):
    L = proj.shape[0]
    tl = min(TOKEN_TILE, L)
    nt, ncl = L // tl, tl // CHUNK
    _, rows, halo = _ssd_specs_in(tl, nt, False)

    def body(xs_ref, bc_ref, hx_ref, hb_ref, dt_ref, z_ref, cw_ref, cb_ref, dtb_ref, al_ref, dd_ref, nw_ref,
             y_ref, ypre_ref, st_ref, pre_ref, xp_ref, xbc_ref, dts_ref, hst_ref):
        i = pl.program_id(0)

        @pl.when(i == 0)
        def _():
            hst_ref[...] = jnp.zeros_like(hst_ref)

        pre = _ssd_conv_fwd(i == 0, xs_ref, bc_ref, hx_ref, hb_ref, cw_ref, cb_ref, xp_ref, tl)
        pre_ref[...] = pre
        xbc_ref[...] = pre * _sigmoid(pre)
        dts_ref[...] = _softplus(dt_ref[...] + dtb_ref[...])
        a_neg = -jnp.exp(al_ref[...])
        e16, _ = _head_expand_matrices()
        causal4, same = _group_masks()
        dd_x = _dot_exact(jnp.broadcast_to(dd_ref[...], (8, LANES)), e16)[0:1, :]

        def chunk(c, carry):
            r0 = pl.multiple_of(c * CHUNK, CHUNK)
            xbc = xbc_ref[pl.ds(r0, CHUNK), :]
            dtc = dts_ref[pl.ds(r0, CHUNK), :]
            acs = _cumsum_rows(dtc * a_neg)
            acs_t = acs.T
            acs_x = _dot_exact(acs, e16)
            xs = xbc[:, 0:1024]
            xd = xs * _dot_exact(dtc, e16)
            xd_bf = xd.astype(BF)
            xdd = (xd * jnp.exp(acs_x[CHUNK - 1:CHUNK, :] - acs_x)).astype(BF)
            e_x = jnp.exp(acs_x)
            for j in range(SSD_GROUPS):
                sl = slice(256 * j, 256 * (j + 1))
                bj = xbc[:, 1024 + 128 * j:1024 + 128 * (j + 1)].astype(BF)
                cj = xbc[:, 1536 + 128 * j:1536 + 128 * (j + 1)].astype(BF)
                g = _dot_nt(cj, bj)
                hj = hst_ref[sl, :]
                zj = _dot_nt(cj, hj.astype(BF))
                sc = (jnp.concatenate([g] * 4, axis=0) * _group_decay(acs, acs_t, j, causal4)).astype(BF)
                yd = _fold_heads(jnp.where(same, _dot(sc, xd_bf[:, sl]), 0.0))
                ypre_ref[pl.ds(r0, CHUNK), sl] = yd + e_x[:, sl] * zj + dd_x[:, sl] * xs[:, sl]
                st_ref[c, sl, :] = hj
                hst_ref[sl, :] = _group_last_decay(acs_t, j) * hj + _dot_tn(xdd[:, sl], bj)
            return carry

        lax.fori_loop(0, ncl, chunk, 0, unroll=True)
        z = z_ref[...]
        gg = ypre_ref[...] * z * _sigmoid(z)
        for j in range(SSD_GROUPS):
            seg = gg[:, 256 * j:256 * (j + 1)]
            r = lax.rsqrt(jnp.mean(seg * seg, axis=-1, keepdims=True) + EPS)
            y_ref[:, 256 * j:256 * (j + 1)] = (seg * r * nw_ref[:, 256 * j:256 * (j + 1)]).astype(BF)

    nc = L // CHUNK
    return pl.pallas_call(
        body, name="ssd_fwd", grid=(nt,),
        in_specs=[rows(1024, 1), rows(1024, 2), halo(1), halo(2), rows(LANES, 0), rows(1024, 0),
                  _const_spec((4, 2048)), _const_spec((1, 2048)), _const_spec((1, LANES)), _const_spec((1, LANES)),
                  _const_spec((1, LANES)), _const_spec((1, D_MODEL))],
        out_specs=[_row_spec(tl, D_MODEL), _row_spec(tl, D_MODEL), pl.BlockSpec((ncl, 1024, SSD_STATE), lambda i: (i, 0, 0)),
                   _row_spec(tl, 2048)],
        out_shape=[jax.ShapeDtypeStruct((L, D_MODEL), BF), jax.ShapeDtypeStruct((L, D_MODEL), F32),
                   jax.ShapeDtypeStruct((nc, 1024, SSD_STATE), F32), jax.ShapeDtypeStruct((L, 2048), F32)],
        scratch_shapes=[pltpu.VMEM((tl + 8, 2048), F32), pltpu.VMEM((tl, 2048), F32), pltpu.VMEM((tl, LANES), F32),
                        pltpu.VMEM((1024, SSD_STATE), F32)],
        compiler_params=_params(("arbitrary",)),
    )(proj, proj, proj, proj, pdt, proj, conv_w, conv_b, dt_bias, a_log, ssd_d, norm_w)


def _ssd_bwd(dyssd, ypre, proj, pdt, states, pre_act, conv_w, dt_bias, a_log, ssd_d, norm_w):
    L = proj.shape[0]
    tl = min(TOKEN_TILE, L)
    nt, ncl = L // tl, tl // CHUNK
    t_of, rows, halo = _ssd_specs_in(tl, nt, True)

    def body(dy_ref, ypre_ref, z_ref, xs_ref, bc_ref, hx_ref, hb_ref, dt_ref, st_ref, pre_ref, cw_ref, dtb_ref, al_ref,
             dd_ref, nw_ref,
             dxbc_ref, ddt_ref, dz_ref, dcw_ref, dcb_ref, ddtb_ref, dal_ref, ddd_ref, dnw_ref,
             xp_ref, xbc_ref, dts_ref, dyp_ref, dxs_ref, ddts_ref, dp_ref, dh_ref):
        i = pl.program_id(0)

        @pl.when(i == 0)
        def _():
            for r in (dcw_ref, dcb_ref, ddtb_ref, dal_ref, ddd_ref, dnw_ref, dh_ref):
                r[...] = jnp.zeros_like(r)
            dp_ref[tl:tl + 8, :] = jnp.zeros((8, 2048), F32)

        _ssd_fill_padded(t_of(i) == 0, xs_ref, bc_ref, hx_ref, hb_ref, xp_ref, tl)
        pre = pre_ref[...]
        xbc_ref[...] = pre * _sigmoid(pre)
        dts_ref[...] = _softplus(dt_ref[...] + dtb_ref[...])
        a_neg = -jnp.exp(al_ref[...])

        ypre = ypre_ref[...]
        z = z_ref[...]
        sz = _sigmoid(z)
        gg = ypre * z * sz
        dout = dy_ref[...]
        for j in range(SSD_GROUPS):
            sl = slice(256 * j, 256 * (j + 1))
            seg = gg[:, sl]
            r = lax.rsqrt(jnp.mean(seg * seg, axis=-1, keepdims=True) + EPS)
            gh = seg * r
            dnw_ref[:, sl] += jnp.sum(dout[:, sl] * gh, axis=0, keepdims=True)
            gw = dout[:, sl] * nw_ref[:, sl]
            dgg = r * (gw - gh * jnp.mean(gw * gh, axis=-1, keepdims=True))
            dyp_ref[:, sl] = dgg * z[:, sl] * sz[:, sl]
            dz_ref[:, sl] = (dgg * ypre[:, sl] * sz[:, sl] * (1.0 + z[:, sl] * (1.0 - sz[:, sl]))).astype(BF)

        e16, e16t = _head_expand_matrices()
        causal4, same = _group_masks()
        dd_x = _dot_exact(jnp.broadcast_to(dd_ref[...], (8, LANES)), e16)[0:1, :]
        last_row = (lax.broadcasted_iota(jnp.int32, (CHUNK, 1), 0) == CHUNK - 1).astype(F32)
        sel_rows = lax.broadcasted_iota(jnp.int32, (4 * CHUNK, LANES), 0) >> 6
        sel_lanes = lax.broadcasted_iota(jnp.int32, (4 * CHUNK, LANES), 1)

        def chunk(k, carry):
            dal_acc, ddx_acc = carry
            c = ncl - 1 - k
            r0 = pl.multiple_of(c * CHUNK, CHUNK)
            xbc = xbc_ref[pl.ds(r0, CHUNK), :]
            dtc = dts_ref[pl.ds(r0, CHUNK), :]
            dyp = dyp_ref[pl.ds(r0, CHUNK), :]
            acs = _cumsum_rows(dtc * a_neg)
            acs_t = acs.T
            acs_x = _dot_exact(acs, e16)
            dt_x = _dot_exact(dtc, e16)
            xs = xbc[:, 0:1024]
            xd = xs * dt_x
            xd_bf = xd.astype(BF)
            dec_x = jnp.exp(acs_x[CHUNK - 1:CHUNK, :] - acs_x)
            xdd = xd * dec_x
            xdd_bf = xdd.astype(BF)
            dz = dyp * jnp.exp(acs_x)
            dz_bf = dz.astype(BF)
            ddx_acc = ddx_acc + jnp.sum(dyp * xs, axis=0, keepdims=True)
            dacs = jnp.zeros((CHUNK, LANES), F32)
            hsum = jnp.zeros((1, LANES), F32)
            p1_l, p2_l, p3_l = [], [], []
            for j in range(SSD_GROUPS):
                sl = slice(256 * j, 256 * (j + 1))
                bj = xbc[:, 1024 + 128 * j:1024 + 128 * (j + 1)].astype(BF)
                cj = xbc[:, 1536 + 128 * j:1536 + 128 * (j + 1)].astype(BF)
                g = _dot_nt(cj, bj)
                hj = st_ref[c, sl, :]
                hj_bf = hj.astype(BF)
                dhj = dh_ref[sl, :]
                dhj_bf = dhj.astype(BF)
                zj = _dot_nt(cj, hj_bf)
                qj = _dot_nt(bj, dhj_bf)
                lm = _group_decay(acs, acs_t, j, causal4)
                sc = jnp.concatenate([g] * 4, axis=0) * lm
                sc_bf = sc.astype(BF)
                dym = jnp.where(same, jnp.concatenate([dyp[:, sl]] * 4, axis=0), 0.0).astype(BF)
                dsc = _dot_nt(dym, xd_bf[:, sl])
                dxd = _dot_tn(sc_bf, dym) + qj * dec_x[:, sl]
                m = dsc * sc
                dg_bf = _fold_heads(dsc * lm).astype(BF)
                rs = jnp.sum(m, axis=1, keepdims=True)
                e2 = dhj * hj
                for hh in range(4):
                    oh = _onehot_lane(4 * j + hh)
                    dacs = dacs + oh * rs[CHUNK * hh:CHUNK * (hh + 1)]
                    hsum = hsum + oh * jnp.sum(jnp.sum(e2[64 * hh:64 * (hh + 1)], axis=0, keepdims=True), axis=1, keepdims=True)
                sel = (sel_rows + 4 * j == sel_lanes).astype(BF)
                hi = m.astype(BF)
                rem = m - hi.astype(F32)
                mid = rem.astype(BF)
                lo = (rem - mid.astype(F32)).astype(BF)
                dacs = dacs - (_dot_tn(hi, sel) + _dot_tn(mid, sel) + _dot_tn(lo, sel))
                p1_l.append(dz[:, sl] * zj)
                p2_l.append(qj * xdd[:, sl])
                p3_l.append(dxd * xs[:, sl])
                dxs_ref[pl.ds(r0, CHUNK), sl] = dd_x[:, sl] * dyp[:, sl] + dxd * dt_x[:, sl]
                dxs_ref[pl.ds(r0, CHUNK), 1536 + 128 * j:1536 + 128 * (j + 1)] = _dot(dg_bf, bj) + _dot(dz_bf[:, sl], hj_bf)
                dxs_ref[pl.ds(r0, CHUNK), 1024 + 128 * j:1024 + 128 * (j + 1)] = _dot_tn(dg_bf, cj) + _dot(xdd_bf[:, sl], dhj_bf)
                dh_ref[sl, :] = _group_last_decay(acs_t, j) * dhj + _dot_tn(dz_bf[:, sl], cj)
            stacked = jnp.concatenate([jnp.concatenate(p1_l, axis=1), jnp.concatenate(p2_l, axis=1), jnp.concatenate(p3_l, axis=1)], axis=0)
            red = _dot_exact(stacked, e16t)
            r1, r2, ddtc = red[0:CHUNK], red[CHUNK:2 * CHUNK], red[2 * CHUNK:3 * CHUNK]
            tot = jnp.sum(r2, axis=0, keepdims=True) + jnp.exp(acs[CHUNK - 1:CHUNK, :]) * hsum
            da = _rev_cumsum_rows(dacs + r1 - r2 + last_row * tot)
            ddts_ref[pl.ds(r0, CHUNK), :] = ddtc + da * a_neg
            dal_acc = dal_acc + jnp.sum(da * dtc, axis=0, keepdims=True)
            return dal_acc, ddx_acc

        carry = (jnp.zeros((1, LANES), F32), jnp.zeros((1, D_MODEL), F32))
        for k in range(ncl):
            carry = chunk(k, carry)
        dal_acc, ddx_acc = carry
        dal_ref[...] += dal_acc * a_neg
        ddd_ref[...] += _dot_exact(jnp.broadcast_to(ddx_acc, (8, D_MODEL)), e16t)[0:1, :]
        ddt_raw = ddts_ref[...] * _sigmoid(dt_ref[...] + dtb_ref[...])
        ddt_ref[...] = ddt_raw
        ddtb_ref[...] += jnp.sum(ddt_raw, axis=0, keepdims=True)

        pre = pre_ref[...]
        sp = _sigmoid(pre)
        dpre = dxs_ref[...] * sp * (1.0 + pre * (1.0 - sp))
        dp_ref[0:tl, :] = dpre
        dcb_ref[...] += jnp.sum(dpre, axis=0, keepdims=True)
        dx = jnp.zeros((tl, 2048), F32)
        for k in range(4):
            dcw_ref[k:k + 1, :] += jnp.sum(dpre * xp_ref[5 + k:5 + k + tl, :], axis=0, keepdims=True)
            dx = dx + cw_ref[k:k + 1, :] * dp_ref[3 - k:3 - k + tl, :]
        dxbc_ref[...] = dx.astype(BF)
        dp_ref[tl:tl + 8, :] = dp_ref[0:8, :]

    vec = lambda w: jax.ShapeDtypeStruct((1, w), F32)
    rrow = lambda w: pl.BlockSpec((tl, w), lambda i: (t_of(i), 0))
    return pl.pallas_call(
        body, name="ssd_bwd", grid=(nt,),
        in_specs=[rrow(D_MODEL), rrow(D_MODEL), rows(1024, 0), rows(1024, 1), rows(1024, 2), halo(1), halo(2), rows(LANES, 0),
                  pl.BlockSpec((ncl, 1024, SSD_STATE), lambda i: (t_of(i), 0, 0)), rrow(2048),
                  _const_spec((4, 2048)), _const_spec((1, LANES)), _const_spec((1, LANES)),
                  _const_spec((1, LANES)), _const_spec((1, D_MODEL))],
        out_specs=[rrow(2048), rrow(LANES), rrow(D_MODEL), _const_spec((8, 2048)), _const_spec((1, 2048)),
                   _const_spec((1, LANES)), _const_spec((1, LANES)), _const_spec((1, LANES)), _const_spec((1, D_MODEL))],
        out_shape=[jax.ShapeDtypeStruct((L, 2048), BF), jax.ShapeDtypeStruct((L, LANES), F32), jax.ShapeDtypeStruct((L, D_MODEL), BF),
                   jax.ShapeDtypeStruct((8, 2048), F32), vec(2048), vec(LANES), vec(LANES), vec(LANES), vec(D_MODEL)],
        scratch_shapes=[pltpu.VMEM((tl + 8, 2048), F32), pltpu.VMEM((tl, 2048), F32),
                        pltpu.VMEM((tl, LANES), F32), pltpu.VMEM((tl, D_MODEL), F32), pltpu.VMEM((tl, 2048), F32),
                        pltpu.VMEM((tl, LANES), F32), pltpu.VMEM((tl + 8, 2048), F32), pltpu.VMEM((1024, SSD_STATE), F32)],
        compiler_params=_params(("arbitrary",)),
    )(dyssd, ypre, proj, proj, proj, proj, proj, pdt, states, pre_act, conv_w, dt_bias, a_log, ssd_d, norm_w)


def _head_fwd_bwd(x, ys5, yssd, p, target, w_out, w_gate, w_proj, ple_nw, fin_nw):
    L = x.shape[0]
    tl = min(TOKEN_TILE, L)
    inv_d = 1.0 / D_MODEL

    def body(x_ref, ys_ref, yd_ref, p_ref, t_ref, wo_ref, wg_ref, wp_ref, pnw_ref, fnw_ref,
             loss_ref, dys_ref, dyd_ref, dh1_ref, n2_ref, dgl_ref, dpp_ref, dpnw_ref, dfnw_ref, dh1b_ref):
        @pl.when(pl.program_id(0) == 0)
        def _():
            loss_ref[...] = jnp.zeros_like(loss_ref)
            dpnw_ref[...] = jnp.zeros_like(dpnw_ref)
            dfnw_ref[...] = jnp.zeros_like(dfnw_ref)

        h1 = x_ref[...] + _dot(ys_ref[...], wo_ref[0:1024, :]) + _dot(yd_ref[...], wo_ref[1024:2048, :])
        r1 = lax.rsqrt(jnp.mean(h1 * h1, axis=-1, keepdims=True) + EPS)
        hh1 = h1 * r1
        n2 = (hh1 * pnw_ref[...]).astype(BF)
        gate = _sigmoid(_dot(n2, wg_ref[...]))
        pp = _dot(p_ref[...].astype(BF), wp_ref[...])
        h2 = h1 + pp * gate
        r2 = lax.rsqrt(jnp.mean(h2 * h2, axis=-1, keepdims=True) + EPS)
        hh2 = h2 * r2
        err = hh2 * fnw_ref[...] - t_ref[...]
        loss_ref[...] += 0.5 * inv_d * jnp.sum(err * err)
        dyo = err * inv_d
        dfnw_ref[...] += jnp.sum(dyo * hh2, axis=0, keepdims=True)
        g2 = dyo * fnw_ref[...]
        dh2 = r2 * (g2 - hh2 * jnp.mean(g2 * hh2, axis=-1, keepdims=True))
        dpp_ref[...] = (dh2 * gate).astype(BF)
        dgl = (dh2 * pp * gate * (1.0 - gate)).astype(BF)
        dgl_ref[...] = dgl
        n2_ref[...] = n2
        dn2 = _dot_nt(dgl, wg_ref[...])
        dpnw_ref[...] += jnp.sum(dn2 * hh1, axis=0, keepdims=True)
        g1 = dn2 * pnw_ref[...]
        dh1 = dh2 + r1 * (g1 - hh1 * jnp.mean(g1 * hh1, axis=-1, keepdims=True))
        dh1_ref[...] = dh1
        dh1_bf = dh1.astype(BF)
        dh1b_ref[...] = dh1_bf
        dys_ref[...] = _dot_nt(dh1_bf, wo_ref[0:1024, :]).astype(BF)
        dyd_ref[...] = _dot_nt(dh1_bf, wo_ref[1024:2048, :])

    big = jax.ShapeDtypeStruct((L, D_MODEL), BF)
    vec = jax.ShapeDtypeStruct((1, D_MODEL), F32)
    return pl.pallas_call(
        body, name="head_fwd_bwd", grid=(L // tl,),
        in_specs=[_row_spec(tl, D_MODEL), _row_spec(tl, D_MODEL), _row_spec(tl, D_MODEL), _row_spec(tl, 256), _row_spec(tl, D_MODEL),
                  _const_spec((2048, D_MODEL)), _const_spec((D_MODEL, D_MODEL)), _const_spec((256, D_MODEL)),
                  _const_spec((1, D_MODEL)), _const_spec((1, D_MODEL))],
        out_specs=[_const_spec((8, LANES)), _row_spec(tl, D_MODEL), _row_spec(tl, D_MODEL), _row_spec(tl, D_MODEL),
                   _row_spec(tl, D_MODEL), _row_spec(tl, D_MODEL), _row_spec(tl, D_MODEL), _const_spec((1, D_MODEL)), _const_spec((1, D_MODEL)),
                   _row_spec(tl, D_MODEL)],
        out_shape=[jax.ShapeDtypeStruct((8, LANES), F32), big, jax.ShapeDtypeStruct((L, D_MODEL), F32),
                   jax.ShapeDtypeStruct((L, D_MODEL), F32), big, big, big, vec, vec, big],
        compiler_params=_params(("arbitrary",)),
    )(x, ys5, yssd, p, target, w_out, w_gate, w_proj, ple_nw, fin_nw)


def _pad_lanes(v):
    return jnp.pad(v.reshape(1, -1), ((0, 0), (0, LANES - v.size)))


def _local_step(x, p, target, w):
    L = x.shape[0]
    nc = L // CHUNK
    nsteps = max(1, (nc - 1).bit_length())
    w_in_t = w["w_in"]
    w_main = w_in_t[:D_MAIN]
    w_dt = jnp.pad(w_in_t[D_MAIN:], ((0, LANES - SSD_HEADS), (0, 0)))
    norm_w = w["norm_w"].reshape(1, -1)
    s5_d = w["s5_D"].reshape(1, -1)
    b_glu = w["s5_b_glu"].reshape(1, -1)
    conv_b = w["conv_b"].reshape(1, -1)
    dtb, alog, ssd_d = _pad_lanes(w["dt_bias"]), _pad_lanes(w["A_log"]), _pad_lanes(w["ssd_D"])
    ssd_nw = w["ssd_norm_w"].reshape(1, -1)
    ple_nw = w["ple_norm_w"].reshape(1, -1)
    fin_nw = w["final_norm_w"].reshape(1, -1)

    s5_args = (w["s5_A_re"], w["s5_A_im"], w["s5_log_dt"], w["s5_B_re"], w["s5_B_im"], w["s5_C_re"], w["s5_C_im"])
    small, small_vjp = jax.vjp(_s5_discretise, *s5_args)
    bbp, pwr, cnp, pwf = _s5_table_factors(*small)
    p1, p2 = _s5_scan_powers(w["s5_A_re"], w["s5_A_im"], w["s5_log_dt"], nsteps)

    hn, u_s5, z_s5, pssd, pdt = _in_proj_fwd(x, norm_w, w_main, w_dt)
    uflat = _flat_hs(u_s5, nc)
    yflat, hsave = _s5_core_fwd(uflat, bbp, pwr, cnp, pwf, p1, p2)
    yssm = _unflat_tk(yflat, nc)
    ys5 = _s5_post_fwd(yssm, u_s5, z_s5, s5_d, w["s5_w_glu"], b_glu)
    yssd, ypre, states, pre_act = _ssd_fwd(pssd, pdt, w["conv_w"], conv_b, dtb, alog, ssd_d, ssd_nw)
    (loss8, dys5, dyssd, dh1, n2, dgl2, dpp, g_ple_nw, g_fin_nw, dh1b) = _head_fwd_bwd(
        x, ys5, yssd, p, target, w["w_out"], w["w_ple_gate"], w["w_ple_proj"], ple_nw, fin_nw)

    (dxbc, ddt, dzd, g_cw, g_cb, g_dtb, g_alog, g_ssd_d, g_ssd_nw) = _ssd_bwd(
        dyssd, ypre, pssd, pdt, states, pre_act, w["conv_w"], dtb, alog, ssd_d, ssd_nw)
    dzs, dyssm, a_glu, dgl1, g_bglu, g_s5d = _s5_post_bwd(dys5, yssm, u_s5, z_s5, s5_d, w["s5_w_glu"], b_glu)
    duflat, dbk, r12, q12, xy, uv, xy0, uv0, da8 = _s5_core_bwd(uflat, _flat_tk(dyssm, nc), hsave, bbp, pwr, cnp, pwf, p1, p2)
    da64 = jnp.concatenate([da8[:, 0, :S5_STATE] + da8[:, 0, S5_STATE:], da8[:, 1, S5_STATE:] - da8[:, 1, :S5_STATE]], axis=-1)
    g_s5 = small_vjp(_s5_small_cotangents(dbk, r12, q12, xy, uv, xy0, uv0, da64))
    gx, du, g_norm_w = _in_proj_bwd(x, norm_w, dh1, _unflat_hs(duflat, nc), dyssm, s5_d, dzs, dzd, dxbc, ddt, w_main, w_dt)

    g_w_in = jnp.concatenate([
        _matmul_tn(du, hn, "dw_in_u"), _matmul_tn(dzs, hn, "dw_in_zs"), _matmul_tn(dzd, hn, "dw_in_zd"),
        _matmul_tn(dxbc, hn, "dw_in_xbc"), _matmul_tn(ddt, hn, "dw_in_dt")[:SSD_HEADS]], axis=0)
    grads = {
        "norm_w": g_norm_w, "w_in": g_w_in,
        "s5_A_re": g_s5[0], "s5_A_im": g_s5[1], "s5_log_dt": g_s5[2], "s5_B_re": g_s5[3], "s5_B_im": g_s5[4],
        "s5_C_re": g_s5[5], "s5_C_im": g_s5[6], "s5_D": g_s5d, "s5_w_glu": _matmul_tn(a_glu, dgl1, "dw_glu"), "s5_b_glu": g_bglu,
        "conv_w": g_cw[:4], "conv_b": g_cb, "dt_bias": g_dtb[:, :SSD_HEADS], "A_log": g_alog[:, :SSD_HEADS],
        "ssd_D": g_ssd_d[:, :SSD_HEADS], "ssd_norm_w": g_ssd_nw,
        "w_out": jnp.concatenate([_matmul_tn(ys5, dh1b, "dw_out_s5"), _matmul_tn(yssd, dh1b, "dw_out_ssd")], axis=0),
        "ple_norm_w": g_ple_nw, "w_ple_gate": _matmul_tn(n2, dgl2, "dw_gate"), "w_ple_proj": _matmul_tn(p, dpp, "dw_proj"),
        "final_norm_w": g_fin_nw,
    }
    return loss8[0, 0], gx, grads


WEIGHTS = ("norm_w", "w_in", "s5_A_re", "s5_A_im", "s5_log_dt", "s5_B_re", "s5_B_im", "s5_C_re", "s5_C_im", "s5_D", "s5_w_glu",
           "s5_b_glu", "conv_w", "conv_b", "dt_bias", "A_log", "ssd_D", "ssd_norm_w", "w_out", "ple_norm_w", "w_ple_gate",
           "w_ple_proj", "final_norm_w")
BIG = {"w_in": ((1284, 1024), 0), "s5_w_glu": ((256, 1024), 0), "w_out": ((512, 1024), 0), "w_ple_gate": ((256, 1024), 0),
       "w_ple_proj": ((256, 256), 1)}
SMALL = {"norm_w": (1024,), "s5_A_re": (64, 64), "s5_A_im": (64, 64), "s5_log_dt": (64,), "s5_B_re": (64, 64, 16),
         "s5_B_im": (64, 64, 16), "s5_C_re": (64, 16, 64), "s5_C_im": (64, 16, 64), "s5_D": (1024,), "s5_b_glu": (1024,),
         "conv_w": (4, 2048), "conv_b": (2048,), "dt_bias": (16,), "A_log": (16,), "ssd_D": (16,), "ssd_norm_w": (1024,),
         "ple_norm_w": (1024,), "final_norm_w": (1024,)}
BIG_ROWS = {n: s[0] * s[1] // LANES for n, (s, _) in BIG.items()}
BIG_ROWS_TOTAL = sum(BIG_ROWS.values())
SMALL_TOTAL = sum(math.prod(s) for s in SMALL.values())
SMALL_PIECE_ROWS = -(-SMALL_TOTAL // (N_CHIPS * 16 * LANES)) * 16
HALF_ROWS = (BIG_ROWS_TOTAL + SMALL_PIECE_ROWS) // 2
SMALL_ROW0 = BIG_ROWS_TOTAL - HALF_ROWS


def _mesh_pos():
    return lax.axis_index("x"), lax.axis_index("y"), lax.axis_index("c")


def _other_chips(x, y):
    return [(1 - x, y), (x, 1 - y), (1 - x, 1 - y)]


def _comm_params():
    return pltpu.CompilerParams(has_side_effects=True)


def _all_gather_chips(wpack, cw):
    half = wpack.shape[0] // 2

    def body(w_ref, c_ref, wo_ref, co_ref, send_sems, recv_sems, fwd_send, fwd_recv, loc_sems):
        x, y, c = _mesh_pos()
        me = 2 * x + y
        sib = (x, y, 1 - c)
        mine = pl.ds(c * half, half)
        theirs = pl.ds((1 - c) * half, half)
        others = _other_chips(x, y)
        loc = [pltpu.make_async_copy(c_ref, co_ref.at[me], loc_sems.at[0])]
        for cp in loc:
            cp.start()

        def from_chip(k, chip, dev):
            return pltpu.make_async_remote_copy(w_ref.at[mine], wo_ref.at[chip, mine], send_sems.at[2 * k], recv_sems.at[2 * k],
                                                device_id=dev, device_id_type=MESH)

        def conv_from(k, chip, dev):
            return pltpu.make_async_remote_copy(c_ref, co_ref.at[chip], send_sems.at[2 * k + 1], recv_sems.at[2 * k + 1],
                                                device_id=dev, device_id_type=MESH)

        def passed(k, chip, rows):
            return pltpu.make_async_remote_copy(wo_ref.at[chip, rows], wo_ref.at[chip, rows], fwd_send.at[k], fwd_recv.at[k],
                                                device_id=sib, device_id_type=MESH)

        sends = []
        for k, (px, py) in enumerate(others):
            sends += [from_chip(k, me, (px, py, c)), conv_from(k, me, (px, py, c))]
        for cp in sends:
            cp.start()
        fwds = []
        for k, (px, py) in enumerate(others):
            chip = 2 * px + py
            from_chip(k, chip, (px, py, c)).wait_recv()
            fwds.append(passed(k, chip, mine))
            fwds[-1].start()
        for k, (px, py) in enumerate(others):
            chip = 2 * px + py
            passed(k, chip, theirs).wait_recv()
            conv_from(k, chip, (px, py, c)).wait_recv()
        for cp in sends + fwds:
            cp.wait_send()
        for cp in loc:
            cp.wait()

    return pl.pallas_call(
        body, name="all_gather_weights", in_specs=[ANY, ANY], out_specs=[ANY, ANY],
        out_shape=[jax.ShapeDtypeStruct((N_CHIPS,) + wpack.shape, wpack.dtype), jax.ShapeDtypeStruct((N_CHIPS,) + cw.shape, cw.dtype)],
        scratch_shapes=[pltpu.SemaphoreType.DMA((6,)), pltpu.SemaphoreType.DMA((6,)), pltpu.SemaphoreType.DMA((3,)),
                        pltpu.SemaphoreType.DMA((3,)), pltpu.SemaphoreType.DMA((1,))],
        compiler_params=_comm_params(),
    )(wpack, cw)


def _exchange_pair(gp):
    def body(g_ref, r_ref, send_sems, recv_sems):
        x, y, c = _mesh_pos()
        cps = [pltpu.make_async_remote_copy(g_ref.at[s, 1 - c], r_ref.at[s], send_sems.at[s], recv_sems.at[s],
                                            device_id=(x, y, 1 - c), device_id_type=MESH) for s in range(N_CHIPS)]
        for cp in cps:
            cp.start()
        for cp in cps:
            cp.wait()

    return pl.pallas_call(
        body, name="grad_exchange_pair", in_specs=[ANY], out_specs=ANY,
        out_shape=jax.ShapeDtypeStruct((N_CHIPS,) + gp.shape[2:], gp.dtype),
        scratch_shapes=[pltpu.SemaphoreType.DMA((N_CHIPS,)), pltpu.SemaphoreType.DMA((N_CHIPS,))],
        compiler_params=_comm_params(),
    )(gp)


def _pair_sum(mine, from_sibling):
    def body(a_ref, b_ref, bf_ref, tail_ref):
        s = a_ref[0] + b_ref[0]
        bf_ref[0] = s.astype(BF)
        tail_ref[0] = s[SMALL_ROW0:, :]

    piece = pl.BlockSpec((1, HALF_ROWS, LANES), lambda i: (i, 0, 0))
    return pl.pallas_call(
        body, name="grad_pair_sum", grid=(N_CHIPS,), in_specs=[piece, piece],
        out_specs=[piece, pl.BlockSpec((1, SMALL_PIECE_ROWS, LANES), lambda i: (i, 0, 0))],
        out_shape=[jax.ShapeDtypeStruct((N_CHIPS, HALF_ROWS, LANES), BF), jax.ShapeDtypeStruct((N_CHIPS, SMALL_PIECE_ROWS, LANES), F32)],
        compiler_params=_params(("parallel",)),
    )(mine, from_sibling)


def _exchange_chips(ps_bf, ps_tail):
    def body(p_ref, t_ref, r_ref, rt_ref, send_sems, recv_sems):
        x, y, c = _mesh_pos()
        cps = []
        for k, (px, py) in enumerate(_other_chips(x, y)):
            cps.append(pltpu.make_async_remote_copy(p_ref.at[2 * px + py], r_ref.at[k], send_sems.at[2 * k], recv_sems.at[2 * k],
                                                    device_id=(px, py, c), device_id_type=MESH))
            cps.append(pltpu.make_async_remote_copy(t_ref.at[2 * px + py], rt_ref.at[k], send_sems.at[2 * k + 1],
                                                    recv_sems.at[2 * k + 1], device_id=(px, py, c), device_id_type=MESH))
        for cp in cps:
            cp.start()
        for cp in cps:
            cp.wait()

    return pl.pallas_call(
        body, name="grad_exchange_chips", in_specs=[ANY, ANY], out_specs=[ANY, ANY],
        out_shape=[jax.ShapeDtypeStruct((N_CHIPS - 1,) + ps_bf.shape[1:], ps_bf.dtype),
                   jax.ShapeDtypeStruct((N_CHIPS - 1,) + ps_tail.shape[1:], ps_tail.dtype)],
        scratch_shapes=[pltpu.SemaphoreType.DMA((6,)), pltpu.SemaphoreType.DMA((6,))],
        compiler_params=_comm_params(),
    )(ps_bf, ps_tail)


def _chip_sum(own_bf, own_tail, others_bf, others_tail):
    def body(ob_ref, ot_ref, b_ref, t_ref, o_ref):
        acc = ob_ref[0:SMALL_ROW0, :].astype(F32)
        tail = ot_ref[...]
        for k in range(N_CHIPS - 1):
            acc = acc + b_ref[k, 0:SMALL_ROW0, :].astype(F32)
            tail = tail + t_ref[k]
        o_ref[0:SMALL_ROW0, :] = acc
        o_ref[SMALL_ROW0:, :] = tail

    return pl.pallas_call(
        body, name="grad_chip_sum", out_shape=jax.ShapeDtypeStruct((HALF_ROWS, LANES), F32),
        compiler_params=_params(),
    )(own_bf, own_tail, others_bf, others_tail)


def _swap_reduced_halves(gh):
    def body(g_ref, o_ref, send_sem, recv_sem):
        x, y, c = _mesh_pos()
        cp = pltpu.make_async_remote_copy(g_ref, o_ref, send_sem, recv_sem, device_id=(x, y, 1 - c), device_id_type=MESH)
        cp.start()
        cp.wait()

    return pl.pallas_call(
        body, name="grad_swap_halves", in_specs=[ANY], out_specs=ANY,
        out_shape=jax.ShapeDtypeStruct(gh.shape, gh.dtype),
        scratch_shapes=[pltpu.SemaphoreType.DMA, pltpu.SemaphoreType.DMA],
        compiler_params=_comm_params(),
    )(gh)


def _gather_small(second_half):
    def body(gs_ref, sm_ref, send_sems, recv_sems, loc_sem):
        x, y, c = _mesh_pos()
        me = 2 * x + y
        small = gs_ref.at[pl.ds(SMALL_ROW0, SMALL_PIECE_ROWS)]
        loc = pltpu.make_async_copy(small, sm_ref.at[me], loc_sem)
        loc.start()
        cps = [pltpu.make_async_remote_copy(small, sm_ref.at[me], send_sems.at[k], recv_sems.at[k],
                                            device_id=(px, py, c), device_id_type=MESH)
               for k, (px, py) in enumerate(_other_chips(x, y))]
        for cp in cps:
            cp.start()
        for cp in cps:
            cp.wait()
        loc.wait()

    return pl.pallas_call(
        body, name="grad_gather_small", in_specs=[ANY], out_specs=ANY,
        out_shape=jax.ShapeDtypeStruct((N_CHIPS, SMALL_PIECE_ROWS, LANES), second_half.dtype),
        scratch_shapes=[pltpu.SemaphoreType.DMA((3,)), pltpu.SemaphoreType.DMA((3,)), pltpu.SemaphoreType.DMA],
        compiler_params=_comm_params(),
    )(second_half)


def _pack_grads(grads):
    small = jnp.concatenate([grads[n].reshape(-1) for n in SMALL])
    small = jnp.pad(small, (0, N_CHIPS * SMALL_PIECE_ROWS * LANES - SMALL_TOTAL)).reshape(N_CHIPS, SMALL_PIECE_ROWS, LANES)
    pieces = []
    for s in range(N_CHIPS):
        rows = []
        for n, (shp, axis) in BIG.items():
            g = grads[n]
            blk = g[s * shp[0]:(s + 1) * shp[0], :] if axis == 0 else g[:, s * shp[1]:(s + 1) * shp[1]]
            rows.append(blk.reshape(-1, LANES))
        rows.append(small[s])
        pieces.append(jnp.concatenate(rows, axis=0).reshape(2, HALF_ROWS, LANES))
    return jnp.stack(pieces)


def _unpack_shard(first_half, second_half):
    rows = jnp.concatenate([first_half, second_half], axis=0)
    out, r0 = {}, 0
    for n, (shp, _) in BIG.items():
        out[n] = rows[r0:r0 + BIG_ROWS[n]].reshape(shp)
        r0 += BIG_ROWS[n]
    return out


def _unpack_small(sm):
    flat = sm.reshape(-1)
    out, o = {}, 0
    for n, shp in SMALL.items():
        k = math.prod(shp)
        out[n] = flat[o:o + k].reshape(shp)
        o += k
    return out


def _as_2d(a):
    n = a.size
    if a.ndim >= 2 and a.shape[-1] > 1024:
        return a.reshape(-1, a.shape[-1])
    if n % 1024 == 0:
        return a.reshape(n // 1024, 1024)
    return a.reshape(1, n)


def _adamw(w, g, m, v, name):
    shape = w.shape
    w2, g2, m2, v2 = (_as_2d(a) for a in (w, g, m, v))
    rows, cols = w2.shape
    rb = 256 if rows >= 512 else rows
    by_cols = rows % rb != 0

    def body(w_ref, g_ref, m_ref, v_ref, d_ref, mo_ref, vo_ref):
        gv = g_ref[...]
        mn = ADAM_B1 * m_ref[...] + (1.0 - ADAM_B1) * gv
        vn = ADAM_B2 * v_ref[...] + (1.0 - ADAM_B2) * (gv * gv)
        m_hat = mn / (1.0 - ADAM_B1 ** ADAM_STEP)
        v_hat = vn / (1.0 - ADAM_B2 ** ADAM_STEP)
        d_ref[...] = -ADAM_LR * (m_hat / (jnp.sqrt(v_hat) + ADAM_EPS) + ADAM_WD * w_ref[...])
        mo_ref[...] = mn
        vo_ref[...] = vn

    spec = pl.BlockSpec((rows, 256), lambda i: (0, i)) if by_cols else _row_spec(rb, cols)
    sds = jax.ShapeDtypeStruct((rows, cols), F32)
    d, mo, vo = pl.pallas_call(
        body, name=name, grid=(cols // 256 if by_cols else rows // rb,), in_specs=[spec] * 4, out_specs=[spec] * 3, out_shape=[sds] * 3,
        compiler_params=_params(("parallel",)),
    )(w2, g2, m2, v2)
    return d.reshape(shape), mo.reshape(shape), vo.reshape(shape)


def kernel(x, p, norm_w, w_in, s5_A_re, s5_A_im, s5_log_dt, s5_B_re, s5_B_im, s5_C_re, s5_C_im, s5_D, s5_w_glu, s5_b_glu, conv_w, conv_b, dt_bias, A_log, ssd_D, ssd_norm_w, w_out, ple_norm_w, w_ple_gate, w_ple_proj, final_norm_w, loss_target, m_norm_w, m_w_in, m_s5_A_re, m_s5_A_im, m_s5_log_dt, m_s5_B_re, m_s5_B_im, m_s5_C_re, m_s5_C_im, m_s5_D, m_s5_w_glu, m_s5_b_glu, m_conv_w, m_conv_b, m_dt_bias, m_A_log, m_ssd_D, m_ssd_norm_w, m_w_out, m_ple_norm_w, m_w_ple_gate, m_w_ple_proj, m_final_norm_w, v_norm_w, v_w_in, v_s5_A_re, v_s5_A_im, v_s5_log_dt, v_s5_B_re, v_s5_B_im, v_s5_C_re, v_s5_C_im, v_s5_D, v_s5_w_glu, v_s5_b_glu, v_conv_w, v_conv_b, v_dt_bias, v_A_log, v_ssd_D, v_ssd_norm_w, v_w_out, v_ple_norm_w, v_w_ple_gate, v_w_ple_proj, v_final_norm_w):
    given = (norm_w, w_in, s5_A_re, s5_A_im, s5_log_dt, s5_B_re, s5_B_im, s5_C_re, s5_C_im, s5_D, s5_w_glu, s5_b_glu, conv_w, conv_b,
             dt_bias, A_log, ssd_D, ssd_norm_w, w_out, ple_norm_w, w_ple_gate, w_ple_proj, final_norm_w)
    given_m = (m_norm_w, m_w_in, m_s5_A_re, m_s5_A_im, m_s5_log_dt, m_s5_B_re, m_s5_B_im, m_s5_C_re, m_s5_C_im, m_s5_D, m_s5_w_glu,
               m_s5_b_glu, m_conv_w, m_conv_b, m_dt_bias, m_A_log, m_ssd_D, m_ssd_norm_w, m_w_out, m_ple_norm_w, m_w_ple_gate,
               m_w_ple_proj, m_final_norm_w)
    given_v = (v_norm_w, v_w_in, v_s5_A_re, v_s5_A_im, v_s5_log_dt, v_s5_B_re, v_s5_B_im, v_s5_C_re, v_s5_C_im, v_s5_D, v_s5_w_glu,
               v_s5_b_glu, v_conv_w, v_conv_b, v_dt_bias, v_A_log, v_ssd_D, v_ssd_norm_w, v_w_out, v_ple_norm_w, v_w_ple_gate,
               v_w_ple_proj, v_final_norm_w)
    wts, mom, var = dict(zip(WEIGHTS, given)), dict(zip(WEIGHTS, given_m)), dict(zip(WEIGHTS, given_v))
    drop = lambda n, a: a if n == "final_norm_w" else a[0]

    shard2d = lambda n, a: a[0].T if n == "w_in" else drop(n, a)
    wpack = jnp.concatenate([shard2d(n, wts[n]).astype(BF).reshape(-1, LANES) for n in BIG], axis=0)
    wall, cwall = _all_gather_chips(wpack, drop("conv_w", wts["conv_w"]))
    chip = 2 * lax.axis_index("x") + lax.axis_index("y")
    full, r0 = {}, 0
    for n, (shp, axis) in BIG.items():
        blk = lax.dynamic_update_slice_in_dim(wall[:, r0:r0 + BIG_ROWS[n]].reshape((N_CHIPS,) + shp),
                                              shard2d(n, wts[n]).astype(BF)[None], chip, axis=0)
        full[n] = blk.reshape(N_CHIPS * shp[0], shp[1]) if axis == 0 else blk.transpose(1, 0, 2).reshape(shp[0], N_CHIPS * shp[1])
        r0 += BIG_ROWS[n]
    for n in SMALL:
        full[n] = drop(n, wts[n])
    full["conv_w"] = cwall.transpose(1, 0, 2).reshape(4, 2048)

    loss, gx, grads = _local_step(x[0], p[0, 0], loss_target[0], full)
    loss = lax.psum(loss, MESH_AXES)

    gp = _pack_grads({n: grads[n].reshape(SMALL[n]) if n in SMALL else grads[n] for n in WEIGHTS})
    c = lax.axis_index("c")
    from_sibling = _exchange_pair(gp)
    mine = lax.dynamic_index_in_dim(gp, c, axis=1, keepdims=False)
    ps_bf, ps_tail = _pair_sum(mine, from_sibling)
    others_bf, others_tail = _exchange_chips(ps_bf, ps_tail)
    own_bf = lax.dynamic_index_in_dim(ps_bf, chip, axis=0, keepdims=False)
    own_tail = lax.dynamic_index_in_dim(ps_tail, chip, axis=0, keepdims=False)
    reduced_half = _chip_sum(own_bf, own_tail, others_bf, others_tail)
    sibling_half = _swap_reduced_halves(reduced_half)
    first_half = jnp.where(c == 0, reduced_half, sibling_half)
    second_half = jnp.where(c == 0, sibling_half, reduced_half)
    sm = _gather_small(second_half)
    g_final = {**_unpack_small(sm), **_unpack_shard(first_half, second_half)}
    g_final["conv_w"] = lax.dynamic_slice_in_dim(g_final["conv_w"], chip * 512, 512, axis=1)

    outs_g, outs_d, outs_m, outs_v = [], [], [], []
    for n in WEIGHTS:
        if n == "w_in":
            res = _adamw(wts[n][0].T, g_final[n], mom[n][0].T, var[n][0].T, "adamw_" + n)
            g, d, mo, vo = (a.T[None] for a in (g_final[n],) + res)
        else:
            g = g_final[n].reshape(wts[n].shape)
            d, mo, vo = _adamw(wts[n], g, mom[n], var[n], "adamw_" + n)
        outs_g.append(g)
        outs_d.append(d)
        outs_m.append(mo)
        outs_v.append(vo)
    return (loss, gx[None], *outs_g, *outs_d, *outs_m, *outs_v)
```

```python
import functools
import math

import jax
import jax.numpy as jnp
from jax import lax
from jax.experimental import pallas as pl
from jax.experimental.pallas import tpu as pltpu

F32 = jnp.float32
BF = jnp.bfloat16
EPS = 1e-6
CHUNK = 64
D_MODEL = 1024
S5_GROUPS = 64
S5_CH = 16
S5_STATE = 64
SSD_HEADS = 16
SSD_HEAD_DIM = 64
SSD_GROUPS = 4
SSD_STATE = 128
D_MAIN = 5120
LANES = 128
TOKEN_TILE = 256
VMEM_LIMIT = 56 * 1024 * 1024
MESH_AXES = ("x", "y", "c")
N_CHIPS = 4
ADAM_LR, ADAM_B1, ADAM_B2, ADAM_EPS, ADAM_WD, ADAM_STEP = 0.001, 0.9, 0.999, 1e-08, 0.01, 10
MESH = pl.DeviceIdType.MESH
ANY = pl.BlockSpec(memory_space=pl.ANY)


def _dot(a, b):
    return jnp.dot(a, b, preferred_element_type=F32)


def _dot_nt(a, b):
    return lax.dot_general(a, b, (((1,), (1,)), ((), ())), preferred_element_type=F32)


def _dot_tn(a, b):
    return lax.dot_general(a, b, (((0,), (0,)), ((), ())), preferred_element_type=F32)


def _sigmoid(x):
    return 1.0 / (1.0 + jnp.exp(-x))


def _softplus(x):
    return jnp.maximum(x, 0.0) + jnp.log(1.0 + jnp.exp(-jnp.abs(x)))


_GELU_C = math.sqrt(2.0 / math.pi)


def _gelu(x):
    return 0.5 * x * (1.0 + jnp.tanh(_GELU_C * (x + 0.044715 * x * x * x)))


def _gelu_grad(x):
    th = jnp.tanh(_GELU_C * (x + 0.044715 * x * x * x))
    return 0.5 * (1.0 + th) + 0.5 * x * (1.0 - th * th) * _GELU_C * (1.0 + 3.0 * 0.044715 * x * x)


def _params(sem=None):
    return pltpu.CompilerParams(dimension_semantics=sem, vmem_limit_bytes=VMEM_LIMIT)


def _row_spec(tl, width, col=0):
    return pl.BlockSpec((tl, width), lambda i, col=col: (i, col))


def _const_spec(shape):
    nd = len(shape)
    return pl.BlockSpec(shape, lambda *_: (0,) * nd)


def _in_proj_fwd(x, norm_w, w_main, w_dt):
    L = x.shape[0]
    tl = min(TOKEN_TILE, L)

    def body(x_ref, nw_ref, wm_ref, wd_ref, hn_ref, u_ref, zs_ref, pssd_ref, pd_ref):
        xv = x_ref[...]
        r = lax.rsqrt(jnp.mean(xv * xv, axis=-1, keepdims=True) + EPS)
        hn = (xv * r * nw_ref[...]).astype(BF)
        hn_ref[...] = hn
        for j, o_ref in enumerate((u_ref, zs_ref)):
            o_ref[...] = _dot_nt(hn, wm_ref[j * 1024:(j + 1) * 1024, :]).astype(BF)
        for j in range(3):
            pssd_ref[:, j * 1024:(j + 1) * 1024] = _dot_nt(hn, wm_ref[(j + 2) * 1024:(j + 3) * 1024, :])
        pd_ref[...] = _dot_nt(hn, wd_ref[...])

    return pl.pallas_call(
        body, name="in_proj_fwd", grid=(L // tl,),
        in_specs=[_row_spec(tl, D_MODEL), _const_spec((1, D_MODEL)), _const_spec((D_MAIN, D_MODEL)), _const_spec((LANES, D_MODEL))],
        out_specs=[_row_spec(tl, D_MODEL), _row_spec(tl, D_MODEL), _row_spec(tl, D_MODEL), _row_spec(tl, 3072), _row_spec(tl, LANES)],
        out_shape=[jax.ShapeDtypeStruct((L, D_MODEL), BF), jax.ShapeDtypeStruct((L, D_MODEL), BF), jax.ShapeDtypeStruct((L, D_MODEL), BF),
                   jax.ShapeDtypeStruct((L, 3072), F32),
                   jax.ShapeDtypeStruct((L, LANES), F32)],
        compiler_params=_params(("arbitrary",)),
    )(x, norm_w, w_main, w_dt)


def _in_proj_bwd(x, norm_w, dh1, du_flat, dyssm, s5_d, dzs, dzd, dxbc, ddt, w_main, w_dt):
    L = x.shape[0]
    tl = min(TOKEN_TILE, L)

    def body(x_ref, nw_ref, dh1_ref, duf_ref, dys_ref, d_ref, dzs_ref, dzd_ref, dxbc_ref, ddt_ref, wm_ref, wd_ref,
             gx_ref, du_ref, gnw_ref):
        @pl.when(pl.program_id(0) == 0)
        def _():
            gnw_ref[...] = jnp.zeros_like(gnw_ref)

        du = (duf_ref[...].astype(F32) + dys_ref[...].astype(F32) * d_ref[...]).astype(BF)
        du_ref[...] = du
        dhn = _dot(du, wm_ref[0:1024, :])
        dhn += _dot(dzs_ref[...], wm_ref[1024:2048, :])
        dhn += _dot(dzd_ref[...], wm_ref[2048:3072, :])
        dhn += _dot(dxbc_ref[...], wm_ref[3072:5120, :])
        dhn += _dot(ddt_ref[...].astype(BF), wd_ref[...])
        xv = x_ref[...]
        r = lax.rsqrt(jnp.mean(xv * xv, axis=-1, keepdims=True) + EPS)
        xh = xv * r
        gnw_ref[...] += jnp.sum(dhn * xh, axis=0, keepdims=True)
        g = dhn * nw_ref[...]
        gx_ref[...] = dh1_ref[...] + r * (g - xh * jnp.mean(g * xh, axis=-1, keepdims=True))

    return pl.pallas_call(
        body, name="in_proj_bwd", grid=(L // tl,),
        in_specs=[_row_spec(tl, D_MODEL), _const_spec((1, D_MODEL)), _row_spec(tl, D_MODEL), _row_spec(tl, D_MODEL),
                  _row_spec(tl, D_MODEL), _const_spec((1, D_MODEL)), _row_spec(tl, D_MODEL), _row_spec(tl, D_MODEL),
                  _row_spec(tl, 2048), _row_spec(tl, LANES), _const_spec((D_MAIN, D_MODEL)), _const_spec((LANES, D_MODEL))],
        out_specs=[_row_spec(tl, D_MODEL), _row_spec(tl, D_MODEL), _const_spec((1, D_MODEL))],
        out_shape=[jax.ShapeDtypeStruct((L, D_MODEL), F32), jax.ShapeDtypeStruct((L, D_MODEL), BF), jax.ShapeDtypeStruct((1, D_MODEL), F32)],
        compiler_params=_params(("arbitrary",)),
    )(x, norm_w, dh1, du_flat, dyssm, s5_d, dzs, dzd, dxbc, ddt, w_main, w_dt)


def _matmul_tn(a, b, name):
    L, M = a.shape
    N = b.shape[1]
    tm, tn, tk = min(M, 1024), min(N, 1024), min(L, 2048)

    def body(a_ref, b_ref, o_ref):
        @pl.when(pl.program_id(2) == 0)
        def _():
            o_ref[...] = jnp.zeros_like(o_ref)

        o_ref[...] += _dot_tn(a_ref[...].astype(BF), b_ref[...].astype(BF))

    return pl.pallas_call(
        body, name=name, grid=(M // tm, N // tn, L // tk),
        in_specs=[pl.BlockSpec((tk, tm), lambda i, j, k: (k, i)), pl.BlockSpec((tk, tn), lambda i, j, k: (k, j))],
        out_specs=pl.BlockSpec((tm, tn), lambda i, j, k: (i, j)),
        out_shape=jax.ShapeDtypeStruct((M, N), F32),
        compiler_params=_params(("parallel", "parallel", "arbitrary")),
    )(a, b)


def _s5_discretise(a_re, a_im, log_dt, b_re, b_im, c_re, c_im):
    dt = jnp.exp(log_dt)[:, None]
    tau = jnp.arange(CHUNK + 1, dtype=F32)
    mag = jnp.exp((a_re * dt)[:, :, None] * tau)
    ang = (a_im * dt)[:, :, None] * tau
    pw_re, pw_im = mag * jnp.cos(ang), mag * jnp.sin(ang)
    er, ei = pw_re[:, :, 1] - 1.0, pw_im[:, :, 1]
    den = a_re * a_re + a_im * a_im
    beta_re, beta_im = (er * a_re + ei * a_im) / den, (ei * a_re - er * a_im) / den
    bb_re = (beta_re[:, :, None] * b_re - beta_im[:, :, None] * b_im).transpose(0, 2, 1)
    bb_im = (beta_re[:, :, None] * b_im + beta_im[:, :, None] * b_re).transpose(0, 2, 1)
    return bb_re, bb_im, c_re, c_im, pw_re, pw_im


def _s5_table_factors(bb_re, bb_im, c_re, c_im, pw_re, pw_im):
    pr = pw_re[:, :, CHUNK - 1::-1].transpose(0, 2, 1)
    pi = pw_im[:, :, CHUNK - 1::-1].transpose(0, 2, 1)
    bbp = jnp.concatenate([bb_re, bb_im], axis=-1)
    pwr = jnp.concatenate([pr, pi], axis=-1)
    cnp = jnp.concatenate([c_re, c_im], axis=-1)
    pwf = jnp.concatenate([pw_re.transpose(0, 2, 1), pw_im.transpose(0, 2, 1)], axis=-1)
    return bbp, pwr, cnp, jnp.pad(pwf, ((0, 0), (0, 7), (0, 0)))


def _s5_small_cotangents(dbk, r12, q12, xy, uv, xy0, uv0, da64):
    n = S5_STATE
    fold = lambda a: a[..., :n] + a[..., n:]
    fold_m = lambda a: a[..., n:] - a[..., :n]
    dbb_re = fold(r12[:, :S5_CH]) + dbk[..., :n]
    dbb_im = fold_m(r12[:, S5_CH:]) + dbk[..., n:]
    dpr, dpi = fold(q12[:, :CHUNK]), fold_m(q12[:, CHUNK:])
    dc_re = -fold_m(xy[:, :S5_CH]) - fold_m(xy0[:, :S5_CH])
    dc_im = -fold(xy[:, S5_CH:]) - fold(xy0[:, S5_CH:])
    dp1_re, dp1_im = -fold_m(uv[:, :CHUNK]), -fold(uv[:, CHUNK:])
    dp0_re, dp0_im = -fold_m(uv0[:, :CHUNK]), -fold(uv0[:, CHUNK:])
    zero = jnp.zeros((S5_GROUPS, n, 1), F32)
    dpw_re = (jnp.concatenate([(dpr[:, ::-1] + dp0_re).transpose(0, 2, 1), zero], axis=-1)
              + jnp.concatenate([zero, dp1_re.transpose(0, 2, 1)], axis=-1)).at[:, :, CHUNK].add(da64[:, :n])
    dpw_im = (jnp.concatenate([(dpi[:, ::-1] + dp0_im).transpose(0, 2, 1), zero], axis=-1)
              + jnp.concatenate([zero, dp1_im.transpose(0, 2, 1)], axis=-1)).at[:, :, CHUNK].add(da64[:, n:])
    return dbb_re, dbb_im, dc_re, dc_im, dpw_re, dpw_im


def _s5_scan_powers(a_re, a_im, log_dt, nsteps):
    dt = jnp.exp(log_dt)[:, None]
    steps = (CHUNK * (2.0 ** jnp.arange(8, dtype=F32)))[None, :, None]
    mag = jnp.exp((a_re * dt)[:, None, :] * steps)
    ang = (a_im * dt)[:, None, :] * steps
    re, im = mag * jnp.cos(ang), mag * jnp.sin(ang)
    del nsteps
    return jnp.concatenate([re, re], -1), jnp.concatenate([-im, im], -1)


def _build_toeplitz(kmat, tg_ref):
    lane = lax.broadcasted_iota(jnp.int32, (CHUNK, CHUNK * S5_CH), 1)
    srow = lax.broadcasted_iota(jnp.int32, (CHUNK, CHUNK * S5_CH), 0)
    keep = lane >= S5_CH * srow
    for h in range(S5_CH):
        row = jnp.broadcast_to(kmat[h:h + 1, :], (CHUNK, CHUNK * S5_CH))
        rolled = pltpu.roll(row, 0, 1, stride=S5_CH, stride_axis=0)
        tg_ref[h * CHUNK:(h + 1) * CHUNK, :] = jnp.where(keep, rolled, 0.0).astype(BF)


def _swap_halves(x):
    return pltpu.roll(x, S5_STATE, 1)


def _hi_lo(x):
    hi = x.astype(BF)
    return hi, (x - hi.astype(F32)).astype(BF)


def _dot3(dot, a, b):
    a_hi, a_lo = _hi_lo(a)
    b_hi, b_lo = _hi_lo(b)
    return dot(a_hi, b_hi) + dot(a_hi, b_lo) + dot(a_lo, b_hi)


def _build_state_tables(bbp_ref, pwr_ref, cnp_ref, pwf_ref, wst_ref, wofft_ref, w0_ref):
    lane = lax.broadcasted_iota(jnp.int32, (1, 2 * S5_STATE), 1)
    left = lane < S5_STATE
    pwr = pwr_ref[0]
    pwr_sw = _swap_halves(pwr)
    for hh in range(S5_CH):
        bb = bbp_ref[0, hh:hh + 1, :]
        bb_sw = _swap_halves(bb)
        wst_ref[hh * CHUNK:(hh + 1) * CHUNK, :] = (jnp.where(left, bb, bb_sw) * pwr
                                                   + jnp.where(left, -bb_sw, bb) * pwr_sw).astype(BF)
    cn = cnp_ref[0]
    cn_sw = _swap_halves(cn)
    c_a = jnp.where(left, cn, -cn_sw)
    c_b = jnp.where(left, -cn_sw, -cn)
    p_prev = pwf_ref[0, 0:1, :]
    for t in range(CHUNK):
        p = pwf_ref[0, t + 1:t + 2, :]
        w0_ref[t * S5_CH:(t + 1) * S5_CH, :] = c_a * p_prev + c_b * _swap_halves(p_prev)
        wofft_ref[t * S5_CH:(t + 1) * S5_CH, :] = (c_a * p + c_b * _swap_halves(p)).astype(BF)
        p_prev = p
    return _dot3(_dot_nt, bbp_ref[0], w0_ref[...])


def _reduce_table_cotangent(d_table, cnp_ref, pwf_ref, first_power):
    cn = cnp_ref[0]
    cn_sw = _swap_halves(cn)
    xa = jnp.zeros((S5_CH, 2 * S5_STATE), F32)
    ya = jnp.zeros((S5_CH, 2 * S5_STATE), F32)
    ru_rows, rv_rows = [], []
    for t in range(CHUNK):
        blk = d_table[t * S5_CH:(t + 1) * S5_CH, :]
        p = pwf_ref[0, t + first_power:t + first_power + 1, :]
        xa = xa + blk * p
        ya = ya + blk * _swap_halves(p)
        ru_rows.append(jnp.sum(blk * cn, axis=0, keepdims=True))
        rv_rows.append(jnp.sum(blk * cn_sw, axis=0, keepdims=True))
    return jnp.concatenate([xa, ya], axis=0), jnp.concatenate(ru_rows + rv_rows, axis=0)


def _s5_core_fwd(uflat, bbp, pwr, cnp, pwf, p1, p2):
    G, nc, W = uflat.shape
    nsteps = max(1, (nc - 1).bit_length())

    def body(u_ref, bbp_ref, pwr_ref, cnp_ref, pwf_ref, p1_ref, p2_ref, y_ref, h_ref, tg_ref, wst_ref, wofft_ref, w0_ref):
        _build_toeplitz(_build_state_tables(bbp_ref, pwr_ref, cnp_ref, pwf_ref, wst_ref, wofft_ref, w0_ref), tg_ref)
        u = u_ref[0]
        x = _dot(u, wst_ref[...])
        row = lax.broadcasted_iota(jnp.int32, x.shape, 0)
        d = 1
        for k in range(nsteps):
            sh = jnp.where(row >= d, pltpu.roll(x, d, 0), 0.0)
            x = x + p1_ref[0, k:k + 1, :] * sh + p2_ref[0, k:k + 1, :] * _swap_halves(sh)
            d *= 2
        h = jnp.where(row >= 1, pltpu.roll(x, 1, 0), 0.0)
        h_ref[0] = h
        y = _dot(u, tg_ref[...]) + _dot_nt(h.astype(BF), wofft_ref[...])
        y_ref[0] = y.astype(BF)

    spec_g = lambda a, b: pl.BlockSpec((1, a, b), lambda g: (g, 0, 0))
    return pl.pallas_call(
        body, name="s5_core_fwd", grid=(G,),
        in_specs=[spec_g(nc, W), spec_g(S5_CH, 2 * S5_STATE), spec_g(CHUNK, 2 * S5_STATE),
                  spec_g(S5_CH, 2 * S5_STATE), spec_g(CHUNK + 8, 2 * S5_STATE), spec_g(8, 2 * S5_STATE), spec_g(8, 2 * S5_STATE)],
        out_specs=[spec_g(nc, W), spec_g(nc, 2 * S5_STATE)],
        out_shape=[jax.ShapeDtypeStruct((G, nc, W), BF), jax.ShapeDtypeStruct((G, nc, 2 * S5_STATE), F32)],
        scratch_shapes=[pltpu.VMEM((W, W), BF), pltpu.VMEM((W, 2 * S5_STATE), BF), pltpu.VMEM((W, 2 * S5_STATE), BF),
                        pltpu.VMEM((W, 2 * S5_STATE), F32)],
        compiler_params=_params(("arbitrary",)),
    )(uflat, bbp, pwr, cnp, pwf, p1, p2)


def _s5_core_bwd(uflat, dyflat, hsave, bbp, pwr, cnp, pwf, p1, p2):
    G, nc, W = uflat.shape
    nsteps = max(1, (nc - 1).bit_length())

    def body(u_ref, dy_ref, h_ref, bbp_ref, pwr_ref, cnp_ref, pwf_ref, p1_ref, p2_ref,
             du_ref, dbk_ref, r12_ref, q12_ref, xy_ref, uv_ref, xy0_ref, uv0_ref, da_ref,
             tg_ref, flip_ref, wst_ref, wofft_ref, w0_ref):
        @pl.when(pl.program_id(0) == 0)
        def _():
            r = lax.broadcasted_iota(jnp.int32, (W, W), 0)
            c = lax.broadcasted_iota(jnp.int32, (W, W), 1)
            flip_ref[...] = (((r >> 6) == (c >> 6)) & ((r & (CHUNK - 1)) + (c & (CHUNK - 1)) == CHUNK - 1)).astype(BF)

        _build_toeplitz(_build_state_tables(bbp_ref, pwr_ref, cnp_ref, pwf_ref, wst_ref, wofft_ref, w0_ref), tg_ref)
        u = u_ref[0]
        dy = dy_ref[0]
        h = h_ref[0]
        gh = _dot(dy, wofft_ref[...])
        row = lax.broadcasted_iota(jnp.int32, gh.shape, 0)
        x = jnp.where(row < nc - 1, pltpu.roll(gh, nc - 1, 0), 0.0)
        d = 1
        for k in range(nsteps):
            sh = jnp.where(row < nc - d, pltpu.roll(x, nc - d, 0), 0.0)
            x = x + p1_ref[0, k:k + 1, :] * sh - p2_ref[0, k:k + 1, :] * _swap_halves(sh)
            d *= 2
        gs = x.astype(BF)
        du_ref[0] = (_dot_nt(dy, tg_ref[...]) + _dot_nt(gs, wst_ref[...])).astype(BF)

        dwst = _dot_tn(u, gs)
        pwr = pwr_ref[0]
        pwr_sw = _swap_halves(pwr)
        q1 = jnp.zeros((CHUNK, 2 * S5_STATE), F32)
        q2 = jnp.zeros((CHUNK, 2 * S5_STATE), F32)
        r1_rows, r2_rows = [], []
        for hh in range(S5_CH):
            blk = dwst[hh * CHUNK:(hh + 1) * CHUNK, :]
            r1_rows.append(jnp.sum(blk * pwr, axis=0, keepdims=True))
            r2_rows.append(jnp.sum(blk * pwr_sw, axis=0, keepdims=True))
            bb = bbp_ref[0, hh:hh + 1, :]
            q1 = q1 + blk * bb
            q2 = q2 + blk * _swap_halves(bb)
        r12_ref[0] = jnp.concatenate(r1_rows + r2_rows, axis=0)
        q12_ref[0] = jnp.concatenate([q1, q2], axis=0)

        xy_ref[0], uv_ref[0] = _reduce_table_cotangent(_dot_tn(dy, h.astype(BF)), cnp_ref, pwf_ref, 1)
        r1 = jnp.sum(x * h, axis=0, keepdims=True)
        r2 = jnp.sum(x * _swap_halves(h), axis=0, keepdims=True)
        da_ref[0] = jnp.concatenate([r1, r2, jnp.zeros((6, 2 * S5_STATE), F32)], axis=0)
        lane = lax.broadcasted_iota(jnp.int32, (CHUNK, W), 1)
        srow = lax.broadcasted_iota(jnp.int32, (CHUNK, W), 0)
        keep = lane < S5_CH * (srow + 1)
        ur = _dot(u, flip_ref[...]).astype(BF)
        dk_rows = []
        for hh in range(S5_CH):
            dt_h = _dot_tn(ur[:, hh * CHUNK:(hh + 1) * CHUNK], dy)
            back = pltpu.roll(dt_h, S5_CH, 1, stride=S5_CH, stride_axis=0)
            dk_rows.append(jnp.sum(jnp.where(keep, back, 0.0), axis=0, keepdims=True))
        dk = jnp.concatenate(dk_rows, axis=0)
        dbk_ref[0] = _dot3(_dot, dk, w0_ref[...])
        xy0_ref[0], uv0_ref[0] = _reduce_table_cotangent(_dot3(_dot_tn, dk, bbp_ref[0]), cnp_ref, pwf_ref, 0)

    spec_g = lambda a, b: pl.BlockSpec((1, a, b), lambda g: (g, 0, 0))
    small_c = jax.ShapeDtypeStruct((G, 2 * S5_CH, 2 * S5_STATE), F32)
    small_p = jax.ShapeDtypeStruct((G, 2 * CHUNK, 2 * S5_STATE), F32)
    return pl.pallas_call(
        body, name="s5_core_bwd", grid=(G,),
        in_specs=[spec_g(nc, W), spec_g(nc, W), spec_g(nc, 2 * S5_STATE), spec_g(S5_CH, 2 * S5_STATE),
                  spec_g(CHUNK, 2 * S5_STATE), spec_g(S5_CH, 2 * S5_STATE), spec_g(CHUNK + 8, 2 * S5_STATE),
                  spec_g(8, 2 * S5_STATE), spec_g(8, 2 * S5_STATE)],
        out_specs=[spec_g(nc, W), spec_g(S5_CH, 2 * S5_STATE), spec_g(2 * S5_CH, 2 * S5_STATE), spec_g(2 * CHUNK, 2 * S5_STATE),
                   spec_g(2 * S5_CH, 2 * S5_STATE), spec_g(2 * CHUNK, 2 * S5_STATE),
                   spec_g(2 * S5_CH, 2 * S5_STATE), spec_g(2 * CHUNK, 2 * S5_STATE), spec_g(8, 2 * S5_STATE)],
        out_shape=[jax.ShapeDtypeStruct((G, nc, W), BF), jax.ShapeDtypeStruct((G, S5_CH, 2 * S5_STATE), F32),
                   small_c, small_p, small_c, small_p, small_c, small_p, jax.ShapeDtypeStruct((G, 8, 2 * S5_STATE), F32)],
        scratch_shapes=[pltpu.VMEM((W, W), BF), pltpu.VMEM((W, W), BF), pltpu.VMEM((W, 2 * S5_STATE), BF),
                        pltpu.VMEM((W, 2 * S5_STATE), BF), pltpu.VMEM((W, 2 * S5_STATE), F32)],
        compiler_params=_params(("arbitrary",)),
    )(uflat, dyflat, hsave, bbp, pwr, cnp, pwf, p1, p2)


def _flat_hs(a, nc):
    return a.reshape(nc, CHUNK, S5_GROUPS, S5_CH).transpose(2, 0, 3, 1).reshape(S5_GROUPS, nc, CHUNK * S5_CH)


def _unflat_hs(a, nc):
    return a.reshape(S5_GROUPS, nc, S5_CH, CHUNK).transpose(1, 3, 0, 2).reshape(nc * CHUNK, D_MODEL)


def _flat_tk(a, nc):
    return a.reshape(nc, CHUNK, S5_GROUPS, S5_CH).transpose(2, 0, 1, 3).reshape(S5_GROUPS, nc, CHUNK * S5_CH)


def _unflat_tk(a, nc):
    return a.reshape(S5_GROUPS, nc, CHUNK, S5_CH).transpose(1, 2, 0, 3).reshape(nc * CHUNK, D_MODEL)


def _s5_post_fwd(yssm, u, zs, s5_d, w_glu, b_glu):
    L = yssm.shape[0]
    tl = min(TOKEN_TILE, L)

    def body(ys_ref, u_ref, z_ref, d_ref, wg_ref, bg_ref, o_ref):
        u = u_ref[...].astype(F32)
        a = _gelu(ys_ref[...].astype(F32) + d_ref[...] * u)
        y = a * _sigmoid(_dot(a.astype(BF), wg_ref[...]) + bg_ref[...])
        z = z_ref[...].astype(F32)
        o_ref[...] = (y * z * _sigmoid(z)).astype(BF)

    return pl.pallas_call(
        body, name="s5_post_fwd", grid=(L // tl,),
        in_specs=[_row_spec(tl, D_MODEL), _row_spec(tl, D_MODEL), _row_spec(tl, D_MODEL), _const_spec((1, D_MODEL)),
                  _const_spec((D_MODEL, D_MODEL)), _const_spec((1, D_MODEL))],
        out_specs=_row_spec(tl, D_MODEL),
        out_shape=jax.ShapeDtypeStruct((L, D_MODEL), BF),
        compiler_params=_params(("arbitrary",)),
    )(yssm, u, zs, s5_d, w_glu, b_glu)


def _s5_post_bwd(dys5, yssm, u, zs, s5_d, w_glu, b_glu):
    L = yssm.shape[0]
    tl = min(TOKEN_TILE, L)

    def body(dy_ref, ys_ref, u_ref, z_ref, d_ref, wg_ref, bg_ref, dz_ref, dys_ref, a_ref, dgl_ref, dbg_ref, dd_ref):
        @pl.when(pl.program_id(0) == 0)
        def _():
            dbg_ref[...] = jnp.zeros_like(dbg_ref)
            dd_ref[...] = jnp.zeros_like(dd_ref)

        u = u_ref[...].astype(F32)
        y0 = ys_ref[...].astype(F32) + d_ref[...] * u
        a = _gelu(y0)
        a_bf = a.astype(BF)
        sg = _sigmoid(_dot(a_bf, wg_ref[...]) + bg_ref[...])
        y = a * sg
        z = z_ref[...].astype(F32)
        sz = _sigmoid(z)
        dout = dy_ref[...].astype(F32)
        dz_ref[...] = (dout * y * sz * (1.0 + z * (1.0 - sz))).astype(BF)
        dyv = dout * z * sz
        dgl = dyv * a * sg * (1.0 - sg)
        dgl_bf = dgl.astype(BF)
        da = dyv * sg + _dot_nt(dgl_bf, wg_ref[...])
        dy0 = da * _gelu_grad(y0)
        dbg_ref[...] += jnp.sum(dgl, axis=0, keepdims=True)
        dd_ref[...] += jnp.sum(dy0 * u, axis=0, keepdims=True)
        dys_ref[...] = dy0.astype(BF)
        a_ref[...] = a_bf
        dgl_ref[...] = dgl_bf

    big = jax.ShapeDtypeStruct((L, D_MODEL), BF)
    vec = jax.ShapeDtypeStruct((1, D_MODEL), F32)
    return pl.pallas_call(
        body, name="s5_post_bwd", grid=(L // tl,),
        in_specs=[_row_spec(tl, D_MODEL), _row_spec(tl, D_MODEL), _row_spec(tl, D_MODEL), _row_spec(tl, D_MODEL),
                  _const_spec((1, D_MODEL)), _const_spec((D_MODEL, D_MODEL)), _const_spec((1, D_MODEL))],
        out_specs=[_row_spec(tl, D_MODEL)] * 4 + [_const_spec((1, D_MODEL))] * 2,
        out_shape=[big, big, big, big, vec, vec],
        compiler_params=_params(("arbitrary",)),
    )(dys5, yssm, u, zs, s5_d, w_glu, b_glu)


def _cumsum_rows(a):
    row = lax.broadcasted_iota(jnp.int32, a.shape, 0)
    d = 1
    while d < a.shape[0]:
        a = a + jnp.where(row >= d, pltpu.roll(a, d, 0), 0.0)
        d *= 2
    return a


def _rev_cumsum_rows(a):
    n = a.shape[0]
    row = lax.broadcasted_iota(jnp.int32, a.shape, 0)
    d = 1
    while d < n:
        a = a + jnp.where(row < n - d, pltpu.roll(a, n - d, 0), 0.0)
        d *= 2
    return a


def _ssd_fill_padded(first, xs_ref, bc_ref, hx_ref, hb_ref, xp_ref, tl):
    hal = jnp.concatenate([hx_ref[...], hb_ref[...]], axis=1)
    xp_ref[0:8, :] = jnp.where(first, 0.0, hal)
    xp_ref[8:8 + tl, 0:1024] = xs_ref[...]
    xp_ref[8:8 + tl, 1024:2048] = bc_ref[...]


def _ssd_conv_fwd(first, xs_ref, bc_ref, hx_ref, hb_ref, cw_ref, cb_ref, xp_ref, tl):
    _ssd_fill_padded(first, xs_ref, bc_ref, hx_ref, hb_ref, xp_ref, tl)
    pre = cb_ref[...] + cw_ref[0:1, :] * xp_ref[5:5 + tl, :]
    for k in range(1, 4):
        pre = pre + cw_ref[k:k + 1, :] * xp_ref[5 + k:5 + k + tl, :]
    return pre


def _onehot_lane(h):
    return (lax.broadcasted_iota(jnp.int32, (1, LANES), 1) == h).astype(F32)


def _dot_exact(x, e):
    hi = x.astype(BF)
    r = x - hi.astype(F32)
    mid = r.astype(BF)
    lo = (r - mid.astype(F32)).astype(BF)
    return _dot(hi, e) + _dot(mid, e) + _dot(lo, e)


def _head_expand_matrices():
    e = lax.broadcasted_iota(jnp.int32, (LANES, D_MODEL), 0) == (lax.broadcasted_iota(jnp.int32, (LANES, D_MODEL), 1) >> 6)
    et = (lax.broadcasted_iota(jnp.int32, (D_MODEL, LANES), 0) >> 6) == lax.broadcasted_iota(jnp.int32, (D_MODEL, LANES), 1)
    return e.astype(BF), et.astype(BF)


def _group_masks():
    r64 = lax.broadcasted_iota(jnp.int32, (4 * CHUNK, CHUNK), 0)
    causal4 = (r64 & (CHUNK - 1)) >= lax.broadcasted_iota(jnp.int32, (4 * CHUNK, CHUNK), 1)
    r256 = lax.broadcasted_iota(jnp.int32, (4 * CHUNK, 4 * SSD_HEAD_DIM), 0)
    same = (r256 >> 6) == (lax.broadcasted_iota(jnp.int32, (4 * CHUNK, 4 * SSD_HEAD_DIM), 1) >> 6)
    return causal4, same


def _group_decay(acs, acs_t, j, causal4):
    col = jnp.concatenate([acs[:, 4 * j + hh:4 * j + hh + 1] for hh in range(4)], axis=0)
    rowv = jnp.concatenate([jnp.broadcast_to(acs_t[4 * j + hh:4 * j + hh + 1, :], (CHUNK, CHUNK)) for hh in range(4)], axis=0)
    return jnp.where(causal4, jnp.exp(col - rowv), 0.0)


def _group_last_decay(acs_t, j):
    return jnp.concatenate([jnp.broadcast_to(jnp.exp(acs_t[4 * j + hh:4 * j + hh + 1, CHUNK - 1:CHUNK]), (SSD_HEAD_DIM, 1))
                            for hh in range(4)], axis=0)


def _fold_heads(r):
    return r[0:CHUNK] + r[CHUNK:2 * CHUNK] + r[2 * CHUNK:3 * CHUNK] + r[3 * CHUNK:4 * CHUNK]


def _ssd_specs_in(tl, nt, rev):
    t_of = (lambda i: nt - 1 - i) if rev else (lambda i: i)
    rows = lambda w, col: pl.BlockSpec((tl, w), lambda i: (t_of(i), col))
    halo = lambda col: pl.BlockSpec((8, 1024), lambda i: (jnp.maximum(t_of(i) * (tl // 8) - 1, 0), col))
    return t_of, rows, halo


def _ssd_fwd(proj, pdt, conv_w, conv_b, dt_bias, a_log, ssd_d, norm_w):
    L = proj.shape[0]
    tl = min(TOKEN_TILE, L)
    nt, ncl = L // tl, tl // CHUNK
    _, rows, halo = _ssd_specs_in(tl, nt, False)

    def body(xs_ref, bc_ref, hx_ref, hb_ref, dt_ref, z_ref, cw_ref, cb_ref, dtb_ref, al_ref, dd_ref, nw_ref,
             y_ref, ypre_ref, st_ref, pre_ref, xp_ref, xbc_ref, dts_ref, hst_ref):
        i = pl.program_id(0)

        @pl.when(i == 0)
        def _():
            hst_ref[...] = jnp.zeros_like(hst_ref)

        pre = _ssd_conv_fwd(i == 0, xs_ref, bc_ref, hx_ref, hb_ref, cw_ref, cb_ref, xp_ref, tl)
        pre_ref[...] = pre
        xbc_ref[...] = pre * _sigmoid(pre)
        dts_ref[...] = _softplus(dt_ref[...] + dtb_ref[...])
        a_neg = -jnp.exp(al_ref[...])
        e16, _ = _head_expand_matrices()
        causal4, same = _group_masks()
        dd_x = _dot_exact(jnp.broadcast_to(dd_ref[...], (8, LANES)), e16)[0:1, :]

        def chunk(c, carry):
            r0 = pl.multiple_of(c * CHUNK, CHUNK)
            xbc = xbc_ref[pl.ds(r0, CHUNK), :]
            dtc = dts_ref[pl.ds(r0, CHUNK), :]
            acs = _cumsum_rows(dtc * a_neg)
            acs_t = acs.T
            acs_x = _dot_exact(acs, e16)
            xs = xbc[:, 0:1024]
            xd = xs * _dot_exact(dtc, e16)
            xd_bf = xd.astype(BF)
            xdd = (xd * jnp.exp(acs_x[CHUNK - 1:CHUNK, :] - acs_x)).astype(BF)
            e_x = jnp.exp(acs_x)
            for j in range(SSD_GROUPS):
                sl = slice(256 * j, 256 * (j + 1))
                bj = xbc[:, 1024 + 128 * j:1024 + 128 * (j + 1)].astype(BF)
                cj = xbc[:, 1536 + 128 * j:1536 + 128 * (j + 1)].astype(BF)
                g = _dot_nt(cj, bj)
                hj = hst_ref[sl, :]
                zj = _dot_nt(cj, hj.astype(BF))
                sc = (jnp.concatenate([g] * 4, axis=0) * _group_decay(acs, acs_t, j, causal4)).astype(BF)
                yd = _fold_heads(jnp.where(same, _dot(sc, xd_bf[:, sl]), 0.0))
                ypre_ref[pl.ds(r0, CHUNK), sl] = yd + e_x[:, sl] * zj + dd_x[:, sl] * xs[:, sl]
                st_ref[c, sl, :] = hj
                hst_ref[sl, :] = _group_last_decay(acs_t, j) * hj + _dot_tn(xdd[:, sl], bj)
            return carry

        lax.fori_loop(0, ncl, chunk, 0, unroll=True)
        z = z_ref[...]
        gg = ypre_ref[...] * z * _sigmoid(z)
        for j in range(SSD_GROUPS):
            seg = gg[:, 256 * j:256 * (j + 1)]
            r = lax.rsqrt(jnp.mean(seg * seg, axis=-1, keepdims=True) + EPS)
            y_ref[:, 256 * j:256 * (j + 1)] = (seg * r * nw_ref[:, 256 * j:256 * (j + 1)]).astype(BF)

    nc = L // CHUNK
    return pl.pallas_call(
        body, name="ssd_fwd", grid=(nt,),
        in_specs=[rows(1024, 1), rows(1024, 2), halo(1), halo(2), rows(LANES, 0), rows(1024, 0),
                  _const_spec((4, 2048)), _const_spec((1, 2048)), _const_spec((1, LANES)), _const_spec((1, LANES)),
                  _const_spec((1, LANES)), _const_spec((1, D_MODEL))],
        out_specs=[_row_spec(tl, D_MODEL), _row_spec(tl, D_MODEL), pl.BlockSpec((ncl, 1024, SSD_STATE), lambda i: (i, 0, 0)),
                   _row_spec(tl, 2048)],
        out_shape=[jax.ShapeDtypeStruct((L, D_MODEL), BF), jax.ShapeDtypeStruct((L, D_MODEL), F32),
                   jax.ShapeDtypeStruct((nc, 1024, SSD_STATE), F32), jax.ShapeDtypeStruct((L, 2048), F32)],
        scratch_shapes=[pltpu.VMEM((tl + 8, 2048), F32), pltpu.VMEM((tl, 2048), F32), pltpu.VMEM((tl, LANES), F32),
                        pltpu.VMEM((1024, SSD_STATE), F32)],
        compiler_params=_params(("arbitrary",)),
    )(proj, proj, proj, proj, pdt, proj, conv_w, conv_b, dt_bias, a_log, ssd_d, norm_w)


def _ssd_bwd(dyssd, ypre, proj, pdt, states, pre_act, conv_w, dt_bias, a_log, ssd_d, norm_w):
    L = proj.shape[0]
    tl = min(TOKEN_TILE, L)
    nt, ncl = L // tl, tl // CHUNK
    t_of, rows, halo = _ssd_specs_in(tl, nt, True)

    def body(dy_ref, ypre_ref, z_ref, xs_ref, bc_ref, hx_ref, hb_ref, dt_ref, st_ref, pre_ref, cw_ref, dtb_ref, al_ref,
             dd_ref, nw_ref,
             dxbc_ref, ddt_ref, dz_ref, dcw_ref, dcb_ref, ddtb_ref, dal_ref, ddd_ref, dnw_ref,
             xp_ref, xbc_ref, dts_ref, dyp_ref, dxs_ref, ddts_ref, dp_ref, dh_ref):
        i = pl.program_id(0)

        @pl.when(i == 0)
        def _():
            for r in (dcw_ref, dcb_ref, ddtb_ref, dal_ref, ddd_ref, dnw_ref, dh_ref):
                r[...] = jnp.zeros_like(r)
            dp_ref[tl:tl + 8, :] = jnp.zeros((8, 2048), F32)

        _ssd_fill_padded(t_of(i) == 0, xs_ref, bc_ref, hx_ref, hb_ref, xp_ref, tl)
        pre = pre_ref[...]
        xbc_ref[...] = pre * _sigmoid(pre)
        dts_ref[...] = _softplus(dt_ref[...] + dtb_ref[...])
        a_neg = -jnp.exp(al_ref[...])

        ypre = ypre_ref[...]
        z = z_ref[...]
        sz = _sigmoid(z)
        gg = ypre * z * sz
        dout = dy_ref[...]
        for j in range(SSD_GROUPS):
            sl = slice(256 * j, 256 * (j + 1))
            seg = gg[:, sl]
            r = lax.rsqrt(jnp.mean(seg * seg, axis=-1, keepdims=True) + EPS)
            gh = seg * r
            dnw_ref[:, sl] += jnp.sum(dout[:, sl] * gh, axis=0, keepdims=True)
            gw = dout[:, sl] * nw_ref[:, sl]
            dgg = r * (gw - gh * jnp.mean(gw * gh, axis=-1, keepdims=True))
            dyp_ref[:, sl] = dgg * z[:, sl] * sz[:, sl]
            dz_ref[:, sl] = (dgg * ypre[:, sl] * sz[:, sl] * (1.0 + z[:, sl] * (1.0 - sz[:, sl]))).astype(BF)

        e16, e16t = _head_expand_matrices()
        causal4, same = _group_masks()
        dd_x = _dot_exact(jnp.broadcast_to(dd_ref[...], (8, LANES)), e16)[0:1, :]
        last_row = (lax.broadcasted_iota(jnp.int32, (CHUNK, 1), 0) == CHUNK - 1).astype(F32)
        sel_rows = lax.broadcasted_iota(jnp.int32, (4 * CHUNK, LANES), 0) >> 6
        sel_lanes = lax.broadcasted_iota(jnp.int32, (4 * CHUNK, LANES), 1)

        def chunk(k, carry):
            dal_acc, ddx_acc = carry
            c = ncl - 1 - k
            r0 = pl.multiple_of(c * CHUNK, CHUNK)
            xbc = xbc_ref[pl.ds(r0, CHUNK), :]
            dtc = dts_ref[pl.ds(r0, CHUNK), :]
            dyp = dyp_ref[pl.ds(r0, CHUNK), :]
            acs = _cumsum_rows(dtc * a_neg)
            acs_t = acs.T
            acs_x = _dot_exact(acs, e16)
            dt_x = _dot_exact(dtc, e16)
            xs = xbc[:, 0:1024]
            xd = xs * dt_x
            xd_bf = xd.astype(BF)
            dec_x = jnp.exp(acs_x[CHUNK - 1:CHUNK, :] - acs_x)
            xdd = xd * dec_x
            xdd_bf = xdd.astype(BF)
            dz = dyp * jnp.exp(acs_x)
            dz_bf = dz.astype(BF)
            ddx_acc = ddx_acc + jnp.sum(dyp * xs, axis=0, keepdims=True)
            dacs = jnp.zeros((CHUNK, LANES), F32)
            hsum = jnp.zeros((1, LANES), F32)
            p1_l, p2_l, p3_l = [], [], []
            for j in range(SSD_GROUPS):
                sl = slice(256 * j, 256 * (j + 1))
                bj = xbc[:, 1024 + 128 * j:1024 + 128 * (j + 1)].astype(BF)
                cj = xbc[:, 1536 + 128 * j:1536 + 128 * (j + 1)].astype(BF)
                g = _dot_nt(cj, bj)
                hj = st_ref[c, sl, :]
                hj_bf = hj.astype(BF)
                dhj = dh_ref[sl, :]
                dhj_bf = dhj.astype(BF)
                zj = _dot_nt(cj, hj_bf)
                qj = _dot_nt(bj, dhj_bf)
                lm = _group_decay(acs, acs_t, j, causal4)
                sc = jnp.concatenate([g] * 4, axis=0) * lm
                sc_bf = sc.astype(BF)
                dym = jnp.where(same, jnp.concatenate([dyp[:, sl]] * 4, axis=0), 0.0).astype(BF)
                dsc = _dot_nt(dym, xd_bf[:, sl])
                dxd = _dot_tn(sc_bf, dym) + qj * dec_x[:, sl]
                m = dsc * sc
                dg_bf = _fold_heads(dsc * lm).astype(BF)
                rs = jnp.sum(m, axis=1, keepdims=True)
                e2 = dhj * hj
                for hh in range(4):
                    oh = _onehot_lane(4 * j + hh)
                    dacs = dacs + oh * rs[CHUNK * hh:CHUNK * (hh + 1)]
                    hsum = hsum + oh * jnp.sum(jnp.sum(e2[64 * hh:64 * (hh + 1)], axis=0, keepdims=True), axis=1, keepdims=True)
                sel = (sel_rows + 4 * j == sel_lanes).astype(BF)
                hi = m.astype(BF)
                rem = m - hi.astype(F32)
                mid = rem.astype(BF)
                lo = (rem - mid.astype(F32)).astype(BF)
                dacs = dacs - (_dot_tn(hi, sel) + _dot_tn(mid, sel) + _dot_tn(lo, sel))
                p1_l.append(dz[:, sl] * zj)
                p2_l.append(qj * xdd[:, sl])
                p3_l.append(dxd * xs[:, sl])
                dxs_ref[pl.ds(r0, CHUNK), sl] = dd_x[:, sl] * dyp[:, sl] + dxd * dt_x[:, sl]
                dxs_ref[pl.ds(r0, CHUNK), 1536 + 128 * j:1536 + 128 * (j + 1)] = _dot(dg_bf, bj) + _dot(dz_bf[:, sl], hj_bf)
                dxs_ref[pl.ds(r0, CHUNK), 1024 + 128 * j:1024 + 128 * (j + 1)] = _dot_tn(dg_bf, cj) + _dot(xdd_bf[:, sl], dhj_bf)
                dh_ref[sl, :] = _group_last_decay(acs_t, j) * dhj + _dot_tn(dz_bf[:, sl], cj)
            stacked = jnp.concatenate([jnp.concatenate(p1_l, axis=1), jnp.concatenate(p2_l, axis=1), jnp.concatenate(p3_l, axis=1)], axis=0)
            red = _dot_exact(stacked, e16t)
            r1, r2, ddtc = red[0:CHUNK], red[CHUNK:2 * CHUNK], red[2 * CHUNK:3 * CHUNK]
            tot = jnp.sum(r2, axis=0, keepdims=True) + jnp.exp(acs[CHUNK - 1:CHUNK, :]) * hsum
            da = _rev_cumsum_rows(dacs + r1 - r2 + last_row * tot)
            ddts_ref[pl.ds(r0, CHUNK), :] = ddtc + da * a_neg
            dal_acc = dal_acc + jnp.sum(da * dtc, axis=0, keepdims=True)
            return dal_acc, ddx_acc

        carry = (jnp.zeros((1, LANES), F32), jnp.zeros((1, D_MODEL), F32))
        for k in range(ncl):
            carry = chunk(k, carry)
        dal_acc, ddx_acc = carry
        dal_ref[...] += dal_acc * a_neg
        ddd_ref[...] += _dot_exact(jnp.broadcast_to(ddx_acc, (8, D_MODEL)), e16t)[0:1, :]
        ddt_raw = ddts_ref[...] * _sigmoid(dt_ref[...] + dtb_ref[...])
        ddt_ref[...] = ddt_raw
        ddtb_ref[...] += jnp.sum(ddt_raw, axis=0, keepdims=True)

        pre = pre_ref[...]
        sp = _sigmoid(pre)
        dpre = dxs_ref[...] * sp * (1.0 + pre * (1.0 - sp))
        dp_ref[0:tl, :] = dpre
        dcb_ref[...] += jnp.sum(dpre, axis=0, keepdims=True)
        dx = jnp.zeros((tl, 2048), F32)
        for k in range(4):
            dcw_ref[k:k + 1, :] += jnp.sum(dpre * xp_ref[5 + k:5 + k + tl, :], axis=0, keepdims=True)
            dx = dx + cw_ref[k:k + 1, :] * dp_ref[3 - k:3 - k + tl, :]
        dxbc_ref[...] = dx.astype(BF)
        dp_ref[tl:tl + 8, :] = dp_ref[0:8, :]

    vec = lambda w: jax.ShapeDtypeStruct((1, w), F32)
    rrow = lambda w: pl.BlockSpec((tl, w), lambda i: (t_of(i), 0))
    return pl.pallas_call(
        body, name="ssd_bwd", grid=(nt,),
        in_specs=[rrow(D_MODEL), rrow(D_MODEL), rows(1024, 0), rows(1024, 1), rows(1024, 2), halo(1), halo(2), rows(LANES, 0),
                  pl.BlockSpec((ncl, 1024, SSD_STATE), lambda i: (t_of(i), 0, 0)), rrow(2048),
                  _const_spec((4, 2048)), _const_spec((1, LANES)), _const_spec((1, LANES)),
                  _const_spec((1, LANES)), _const_spec((1, D_MODEL))],
        out_specs=[rrow(2048), rrow(LANES), rrow(D_MODEL), _const_spec((8, 2048)), _const_spec((1, 2048)),
                   _const_spec((1, LANES)), _const_spec((1, LANES)), _const_spec((1, LANES)), _const_spec((1, D_MODEL))],
        out_shape=[jax.ShapeDtypeStruct((L, 2048), BF), jax.ShapeDtypeStruct((L, LANES), F32), jax.ShapeDtypeStruct((L, D_MODEL), BF),
                   jax.ShapeDtypeStruct((8, 2048), F32), vec(2048), vec(LANES), vec(LANES), vec(LANES), vec(D_MODEL)],
        scratch_shapes=[pltpu.VMEM((tl + 8, 2048), F32), pltpu.VMEM((tl, 2048), F32),
                        pltpu.VMEM((tl, LANES), F32), pltpu.VMEM((tl, D_MODEL), F32), pltpu.VMEM((tl, 2048), F32),
                        pltpu.VMEM((tl, LANES), F32), pltpu.VMEM((tl + 8, 2048), F32), pltpu.VMEM((1024, SSD_STATE), F32)],
        compiler_params=_params(("arbitrary",)),
    )(dyssd, ypre, proj, proj, proj, proj, proj, pdt, states, pre_act, conv_w, dt_bias, a_log, ssd_d, norm_w)


def _head_fwd_bwd(x, ys5, yssd, p, target, w_out, w_gate, w_proj, ple_nw, fin_nw):
    L = x.shape[0]
    tl = min(TOKEN_TILE, L)
    inv_d = 1.0 / D_MODEL

    def body(x_ref, ys_ref, yd_ref, p_ref, t_ref, wo_ref, wg_ref, wp_ref, pnw_ref, fnw_ref,
             loss_ref, dys_ref, dyd_ref, dh1_ref, n2_ref, dgl_ref, dpp_ref, dpnw_ref, dfnw_ref):
        @pl.when(pl.program_id(0) == 0)
        def _():
            loss_ref[...] = jnp.zeros_like(loss_ref)
            dpnw_ref[...] = jnp.zeros_like(dpnw_ref)
            dfnw_ref[...] = jnp.zeros_like(dfnw_ref)

        h1 = x_ref[...] + _dot(ys_ref[...], wo_ref[0:1024, :]) + _dot(yd_ref[...], wo_ref[1024:2048, :])
        r1 = lax.rsqrt(jnp.mean(h1 * h1, axis=-1, keepdims=True) + EPS)
        hh1 = h1 * r1
        n2 = (hh1 * pnw_ref[...]).astype(BF)
        gate = _sigmoid(_dot(n2, wg_ref[...]))
        pp = _dot(p_ref[...].astype(BF), wp_ref[...])
        h2 = h1 + pp * gate
        r2 = lax.rsqrt(jnp.mean(h2 * h2, axis=-1, keepdims=True) + EPS)
        hh2 = h2 * r2
        err = hh2 * fnw_ref[...] - t_ref[...]
        loss_ref[...] += 0.5 * inv_d * jnp.sum(err * err)
        dyo = err * inv_d
        dfnw_ref[...] += jnp.sum(dyo * hh2, axis=0, keepdims=True)
        g2 = dyo * fnw_ref[...]
        dh2 = r2 * (g2 - hh2 * jnp.mean(g2 * hh2, axis=-1, keepdims=True))
        dpp_ref[...] = (dh2 * gate).astype(BF)
        dgl = (dh2 * pp * gate * (1.0 - gate)).astype(BF)
        dgl_ref[...] = dgl
        n2_ref[...] = n2
        dn2 = _dot_nt(dgl, wg_ref[...])
        dpnw_ref[...] += jnp.sum(dn2 * hh1, axis=0, keepdims=True)
        g1 = dn2 * pnw_ref[...]
        dh1 = dh2 + r1 * (g1 - hh1 * jnp.mean(g1 * hh1, axis=-1, keepdims=True))
        dh1_ref[...] = dh1
        dh1_bf = dh1.astype(BF)
        dys_ref[...] = _dot_nt(dh1_bf, wo_ref[0:1024, :]).astype(BF)
        dyd_ref[...] = _dot_nt(dh1_bf, wo_ref[1024:2048, :])

    big = jax.ShapeDtypeStruct((L, D_MODEL), BF)
    vec = jax.ShapeDtypeStruct((1, D_MODEL), F32)
    return pl.pallas_call(
        body, name="head_fwd_bwd", grid=(L // tl,),
        in_specs=[_row_spec(tl, D_MODEL), _row_spec(tl, D_MODEL), _row_spec(tl, D_MODEL), _row_spec(tl, 256), _row_spec(tl, D_MODEL),
                  _const_spec((2048, D_MODEL)), _const_spec((D_MODEL, D_MODEL)), _const_spec((256, D_MODEL)),
                  _const_spec((1, D_MODEL)), _const_spec((1, D_MODEL))],
        out_specs=[_const_spec((8, LANES)), _row_spec(tl, D_MODEL), _row_spec(tl, D_MODEL), _row_spec(tl, D_MODEL),
                   _row_spec(tl, D_MODEL), _row_spec(tl, D_MODEL), _row_spec(tl, D_MODEL), _const_spec((1, D_MODEL)), _const_spec((1, D_MODEL))],
        out_shape=[jax.ShapeDtypeStruct((8, LANES), F32), big, jax.ShapeDtypeStruct((L, D_MODEL), F32),
                   jax.ShapeDtypeStruct((L, D_MODEL), F32), big, big, big, vec, vec],
        compiler_params=_params(("arbitrary",)),
    )(x, ys5, yssd, p, target, w_out, w_gate, w_proj, ple_nw, fin_nw)


def _pad_lanes(v):
    return jnp.pad(v.reshape(1, -1), ((0, 0), (0, LANES - v.size)))


def _local_step(x, p, target, w):
    L = x.shape[0]
    nc = L // CHUNK
    nsteps = max(1, (nc - 1).bit_length())
    w_in_t = w["w_in"]
    w_main = w_in_t[:D_MAIN]
    w_dt = jnp.pad(w_in_t[D_MAIN:], ((0, LANES - SSD_HEADS), (0, 0)))
    norm_w = w["norm_w"].reshape(1, -1)
    s5_d = w["s5_D"].reshape(1, -1)
    b_glu = w["s5_b_glu"].reshape(1, -1)
    conv_b = w["conv_b"].reshape(1, -1)
    dtb, alog, ssd_d = _pad_lanes(w["dt_bias"]), _pad_lanes(w["A_log"]), _pad_lanes(w["ssd_D"])
    ssd_nw = w["ssd_norm_w"].reshape(1, -1)
    ple_nw = w["ple_norm_w"].reshape(1, -1)
    fin_nw = w["final_norm_w"].reshape(1, -1)

    s5_args = (w["s5_A_re"], w["s5_A_im"], w["s5_log_dt"], w["s5_B_re"], w["s5_B_im"], w["s5_C_re"], w["s5_C_im"])
    small, small_vjp = jax.vjp(_s5_discretise, *s5_args)
    bbp, pwr, cnp, pwf = _s5_table_factors(*small)
    p1, p2 = _s5_scan_powers(w["s5_A_re"], w["s5_A_im"], w["s5_log_dt"], nsteps)

    hn, u_s5, z_s5, pssd, pdt = _in_proj_fwd(x, norm_w, w_main, w_dt)
    uflat = _flat_hs(u_s5, nc)
    yflat, hsave = _s5_core_fwd(uflat, bbp, pwr, cnp, pwf, p1, p2)
    yssm = _unflat_tk(yflat, nc)
    ys5 = _s5_post_fwd(yssm, u_s5, z_s5, s5_d, w["s5_w_glu"], b_glu)
    yssd, ypre, states, pre_act = _ssd_fwd(pssd, pdt, w["conv_w"], conv_b, dtb, alog, ssd_d, ssd_nw)
    (loss8, dys5, dyssd, dh1, n2, dgl2, dpp, g_ple_nw, g_fin_nw) = _head_fwd_bwd(
        x, ys5, yssd, p, target, w["w_out"], w["w_ple_gate"], w["w_ple_proj"], ple_nw, fin_nw)

    (dxbc, ddt, dzd, g_cw, g_cb, g_dtb, g_alog, g_ssd_d, g_ssd_nw) = _ssd_bwd(
        dyssd, ypre, pssd, pdt, states, pre_act, w["conv_w"], dtb, alog, ssd_d, ssd_nw)
    dzs, dyssm, a_glu, dgl1, g_bglu, g_s5d = _s5_post_bwd(dys5, yssm, u_s5, z_s5, s5_d, w["s5_w_glu"], b_glu)
    duflat, dbk, r12, q12, xy, uv, xy0, uv0, da8 = _s5_core_bwd(uflat, _flat_tk(dyssm, nc), hsave, bbp, pwr, cnp, pwf, p1, p2)
    da64 = jnp.concatenate([da8[:, 0, :S5_STATE] + da8[:, 0, S5_STATE:], da8[:, 1, S5_STATE:] - da8[:, 1, :S5_STATE]], axis=-1)
    g_s5 = small_vjp(_s5_small_cotangents(dbk, r12, q12, xy, uv, xy0, uv0, da64))
    gx, du, g_norm_w = _in_proj_bwd(x, norm_w, dh1, _unflat_hs(duflat, nc), dyssm, s5_d, dzs, dzd, dxbc, ddt, w_main, w_dt)

    g_w_in = jnp.concatenate([
        _matmul_tn(du, hn, "dw_in_u"), _matmul_tn(dzs, hn, "dw_in_zs"), _matmul_tn(dzd, hn, "dw_in_zd"),
        _matmul_tn(dxbc, hn, "dw_in_xbc"), _matmul_tn(ddt, hn, "dw_in_dt")[:SSD_HEADS]], axis=0)
    grads = {
        "norm_w": g_norm_w, "w_in": g_w_in,
        "s5_A_re": g_s5[0], "s5_A_im": g_s5[1], "s5_log_dt": g_s5[2], "s5_B_re": g_s5[3], "s5_B_im": g_s5[4],
        "s5_C_re": g_s5[5], "s5_C_im": g_s5[6], "s5_D": g_s5d, "s5_w_glu": _matmul_tn(a_glu, dgl1, "dw_glu"), "s5_b_glu": g_bglu,
        "conv_w": g_cw[:4], "conv_b": g_cb, "dt_bias": g_dtb[:, :SSD_HEADS], "A_log": g_alog[:, :SSD_HEADS],
        "ssd_D": g_ssd_d[:, :SSD_HEADS], "ssd_norm_w": g_ssd_nw,
        "w_out": jnp.concatenate([_matmul_tn(ys5, dh1, "dw_out_s5"), _matmul_tn(yssd, dh1, "dw_out_ssd")], axis=0),
        "ple_norm_w": g_ple_nw, "w_ple_gate": _matmul_tn(n2, dgl2, "dw_gate"), "w_ple_proj": _matmul_tn(p, dpp, "dw_proj"),
        "final_norm_w": g_fin_nw,
    }
    return loss8[0, 0], gx, grads


WEIGHTS = ("norm_w", "w_in", "s5_A_re", "s5_A_im", "s5_log_dt", "s5_B_re", "s5_B_im", "s5_C_re", "s5_C_im", "s5_D", "s5_w_glu",
           "s5_b_glu", "conv_w", "conv_b", "dt_bias", "A_log", "ssd_D", "ssd_norm_w", "w_out", "ple_norm_w", "w_ple_gate",
           "w_ple_proj", "final_norm_w")
BIG = {"w_in": ((1284, 1024), 0), "s5_w_glu": ((256, 1024), 0), "w_out": ((512, 1024), 0), "w_ple_gate": ((256, 1024), 0),
       "w_ple_proj": ((256, 256), 1)}
SMALL = {"norm_w": (1024,), "s5_A_re": (64, 64), "s5_A_im": (64, 64), "s5_log_dt": (64,), "s5_B_re": (64, 64, 16),
         "s5_B_im": (64, 64, 16), "s5_C_re": (64, 16, 64), "s5_C_im": (64, 16, 64), "s5_D": (1024,), "s5_b_glu": (1024,),
         "conv_w": (4, 2048), "conv_b": (2048,), "dt_bias": (16,), "A_log": (16,), "ssd_D": (16,), "ssd_norm_w": (1024,),
         "ple_norm_w": (1024,), "final_norm_w": (1024,)}
BIG_ROWS = {n: s[0] * s[1] // LANES for n, (s, _) in BIG.items()}
BIG_ROWS_TOTAL = sum(BIG_ROWS.values())
SMALL_TOTAL = sum(math.prod(s) for s in SMALL.values())
SMALL_PIECE_ROWS = -(-SMALL_TOTAL // (N_CHIPS * 16 * LANES)) * 16
HALF_ROWS = (BIG_ROWS_TOTAL + SMALL_PIECE_ROWS) // 2
SMALL_ROW0 = BIG_ROWS_TOTAL - HALF_ROWS


def _mesh_pos():
    return lax.axis_index("x"), lax.axis_index("y"), lax.axis_index("c")


def _other_chips(x, y):
    return [(1 - x, y), (x, 1 - y), (1 - x, 1 - y)]


def _comm_params():
    return pltpu.CompilerParams(has_side_effects=True)


def _all_gather_chips(wpack, cw):
    half = wpack.shape[0] // 2

    def body(w_ref, c_ref, wo_ref, co_ref, send_sems, recv_sems, fwd_send, fwd_recv, loc_sems):
        x, y, c = _mesh_pos()
        me = 2 * x + y
        sib = (x, y, 1 - c)
        mine = pl.ds(c * half, half)
        theirs = pl.ds((1 - c) * half, half)
        others = _other_chips(x, y)
        loc = [pltpu.make_async_copy(c_ref, co_ref.at[me], loc_sems.at[0])]
        for cp in loc:
            cp.start()

        def from_chip(k, chip, dev):
            return pltpu.make_async_remote_copy(w_ref.at[mine], wo_ref.at[chip, mine], send_sems.at[2 * k], recv_sems.at[2 * k],
                                                device_id=dev, device_id_type=MESH)

        def conv_from(k, chip, dev):
            return pltpu.make_async_remote_copy(c_ref, co_ref.at[chip], send_sems.at[2 * k + 1], recv_sems.at[2 * k + 1],
                                                device_id=dev, device_id_type=MESH)

        def passed(k, chip, rows):
            return pltpu.make_async_remote_copy(wo_ref.at[chip, rows], wo_ref.at[chip, rows], fwd_send.at[k], fwd_recv.at[k],
                                                device_id=sib, device_id_type=MESH)

        sends = []
        for k, (px, py) in enumerate(others):
            sends += [from_chip(k, me, (px, py, c)), conv_from(k, me, (px, py, c))]
        for cp in sends:
            cp.start()
        fwds = []
        for k, (px, py) in enumerate(others):
            chip = 2 * px + py
            from_chip(k, chip, (px, py, c)).wait_recv()
            fwds.append(passed(k, chip, mine))
            fwds[-1].start()
        for k, (px, py) in enumerate(others):
            chip = 2 * px + py
            passed(k, chip, theirs).wait_recv()
            conv_from(k, chip, (px, py, c)).wait_recv()
        for cp in sends + fwds:
            cp.wait_send()
        for cp in loc:
            cp.wait()

    return pl.pallas_call(
        body, name="all_gather_weights", in_specs=[ANY, ANY], out_specs=[ANY, ANY],
        out_shape=[jax.ShapeDtypeStruct((N_CHIPS,) + wpack.shape, wpack.dtype), jax.ShapeDtypeStruct((N_CHIPS,) + cw.shape, cw.dtype)],
        scratch_shapes=[pltpu.SemaphoreType.DMA((6,)), pltpu.SemaphoreType.DMA((6,)), pltpu.SemaphoreType.DMA((3,)),
                        pltpu.SemaphoreType.DMA((3,)), pltpu.SemaphoreType.DMA((1,))],
        compiler_params=_comm_params(),
    )(wpack, cw)


def _exchange_pair(gp):
    def body(g_ref, r_ref, send_sems, recv_sems):
        x, y, c = _mesh_pos()
        cps = [pltpu.make_async_remote_copy(g_ref.at[s, 1 - c], r_ref.at[s], send_sems.at[s], recv_sems.at[s],
                                            device_id=(x, y, 1 - c), device_id_type=MESH) for s in range(N_CHIPS)]
        for cp in cps:
            cp.start()
        for cp in cps:
            cp.wait()

    return pl.pallas_call(
        body, name="grad_exchange_pair", in_specs=[ANY], out_specs=ANY,
        out_shape=jax.ShapeDtypeStruct((N_CHIPS,) + gp.shape[2:], gp.dtype),
        scratch_shapes=[pltpu.SemaphoreType.DMA((N_CHIPS,)), pltpu.SemaphoreType.DMA((N_CHIPS,))],
        compiler_params=_comm_params(),
    )(gp)


def _pair_sum(mine, from_sibling):
    def body(a_ref, b_ref, bf_ref, tail_ref):
        s = a_ref[0] + b_ref[0]
        bf_ref[0] = s.astype(BF)
        tail_ref[0] = s[SMALL_ROW0:, :]

    piece = pl.BlockSpec((1, HALF_ROWS, LANES), lambda i: (i, 0, 0))
    return pl.pallas_call(
        body, name="grad_pair_sum", grid=(N_CHIPS,), in_specs=[piece, piece],
        out_specs=[piece, pl.BlockSpec((1, SMALL_PIECE_ROWS, LANES), lambda i: (i, 0, 0))],
        out_shape=[jax.ShapeDtypeStruct((N_CHIPS, HALF_ROWS, LANES), BF), jax.ShapeDtypeStruct((N_CHIPS, SMALL_PIECE_ROWS, LANES), F32)],
        compiler_params=_params(("parallel",)),
    )(mine, from_sibling)


def _exchange_chips(ps_bf, ps_tail):
    def body(p_ref, t_ref, r_ref, rt_ref, send_sems, recv_sems):
        x, y, c = _mesh_pos()
        cps = []
        for k, (px, py) in enumerate(_other_chips(x, y)):
            cps.append(pltpu.make_async_remote_copy(p_ref.at[2 * px + py], r_ref.at[k], send_sems.at[2 * k], recv_sems.at[2 * k],
                                                    device_id=(px, py, c), device_id_type=MESH))
            cps.append(pltpu.make_async_remote_copy(t_ref.at[2 * px + py], rt_ref.at[k], send_sems.at[2 * k + 1],
                                                    recv_sems.at[2 * k + 1], device_id=(px, py, c), device_id_type=MESH))
        for cp in cps:
            cp.start()
        for cp in cps:
            cp.wait()

    return pl.pallas_call(
        body, name="grad_exchange_chips", in_specs=[ANY, ANY], out_specs=[ANY, ANY],
        out_shape=[jax.ShapeDtypeStruct((N_CHIPS - 1,) + ps_bf.shape[1:], ps_bf.dtype),
                   jax.ShapeDtypeStruct((N_CHIPS - 1,) + ps_tail.shape[1:], ps_tail.dtype)],
        scratch_shapes=[pltpu.SemaphoreType.DMA((6,)), pltpu.SemaphoreType.DMA((6,))],
        compiler_params=_comm_params(),
    )(ps_bf, ps_tail)


def _chip_sum(own_bf, own_tail, others_bf, others_tail):
    def body(ob_ref, ot_ref, b_ref, t_ref, o_ref):
        acc = ob_ref[0:SMALL_ROW0, :].astype(F32)
        tail = ot_ref[...]
        for k in range(N_CHIPS - 1):
            acc = acc + b_ref[k, 0:SMALL_ROW0, :].astype(F32)
            tail = tail + t_ref[k]
        o_ref[0:SMALL_ROW0, :] = acc
        o_ref[SMALL_ROW0:, :] = tail

    return pl.pallas_call(
        body, name="grad_chip_sum", out_shape=jax.ShapeDtypeStruct((HALF_ROWS, LANES), F32),
        compiler_params=_params(),
    )(own_bf, own_tail, others_bf, others_tail)


def _swap_reduced_halves(gh):
    def body(g_ref, o_ref, send_sem, recv_sem):
        x, y, c = _mesh_pos()
        cp = pltpu.make_async_remote_copy(g_ref, o_ref, send_sem, recv_sem, device_id=(x, y, 1 - c), device_id_type=MESH)
        cp.start()
        cp.wait()

    return pl.pallas_call(
        body, name="grad_swap_halves", in_specs=[ANY], out_specs=ANY,
        out_shape=jax.ShapeDtypeStruct(gh.shape, gh.dtype),
        scratch_shapes=[pltpu.SemaphoreType.DMA, pltpu.SemaphoreType.DMA],
        compiler_params=_comm_params(),
    )(gh)


def _gather_small(second_half):
    def body(gs_ref, sm_ref, send_sems, recv_sems, loc_sem):
        x, y, c = _mesh_pos()
        me = 2 * x + y
        small = gs_ref.at[pl.ds(SMALL_ROW0, SMALL_PIECE_ROWS)]
        loc = pltpu.make_async_copy(small, sm_ref.at[me], loc_sem)
        loc.start()
        cps = [pltpu.make_async_remote_copy(small, sm_ref.at[me], send_sems.at[k], recv_sems.at[k],
                                            device_id=(px, py, c), device_id_type=MESH)
               for k, (px, py) in enumerate(_other_chips(x, y))]
        for cp in cps:
            cp.start()
        for cp in cps:
            cp.wait()
        loc.wait()

    return pl.pallas_call(
        body, name="grad_gather_small", in_specs=[ANY], out_specs=ANY,
        out_shape=jax.ShapeDtypeStruct((N_CHIPS, SMALL_PIECE_ROWS, LANES), second_half.dtype),
        scratch_shapes=[pltpu.SemaphoreType.DMA((3,)), pltpu.SemaphoreType.DMA((3,)), pltpu.SemaphoreType.DMA],
        compiler_params=_comm_params(),
    )(second_half)


def _pack_grads(grads):
    small = jnp.concatenate([grads[n].reshape(-1) for n in SMALL])
    small = jnp.pad(small, (0, N_CHIPS * SMALL_PIECE_ROWS * LANES - SMALL_TOTAL)).reshape(N_CHIPS, SMALL_PIECE_ROWS, LANES)
    pieces = []
    for s in range(N_CHIPS):
        rows = []
        for n, (shp, axis) in BIG.items():
            g = grads[n]
            blk = g[s * shp[0]:(s + 1) * shp[0], :] if axis == 0 else g[:, s * shp[1]:(s + 1) * shp[1]]
            rows.append(blk.reshape(-1, LANES))
        rows.append(small[s])
        pieces.append(jnp.concatenate(rows, axis=0).reshape(2, HALF_ROWS, LANES))
    return jnp.stack(pieces)


def _unpack_shard(first_half, second_half):
    rows = jnp.concatenate([first_half, second_half], axis=0)
    out, r0 = {}, 0
    for n, (shp, _) in BIG.items():
        out[n] = rows[r0:r0 + BIG_ROWS[n]].reshape(shp)
        r0 += BIG_ROWS[n]
    return out


def _unpack_small(sm):
    flat = sm.reshape(-1)
    out, o = {}, 0
    for n, shp in SMALL.items():
        k = math.prod(shp)
        out[n] = flat[o:o + k].reshape(shp)
        o += k
    return out


def _as_2d(a):
    n = a.size
    if a.ndim >= 2 and a.shape[-1] > 1024:
        return a.reshape(-1, a.shape[-1])
    if n % 1024 == 0:
        return a.reshape(n // 1024, 1024)
    return a.reshape(1, n)


def _adamw(w, g, m, v, name):
    shape = w.shape
    w2, g2, m2, v2 = (_as_2d(a) for a in (w, g, m, v))
    rows, cols = w2.shape
    rb = 256 if rows >= 512 else rows
    by_cols = rows % rb != 0

    def body(w_ref, g_ref, m_ref, v_ref, d_ref, mo_ref, vo_ref):
        gv = g_ref[...]
        mn = ADAM_B1 * m_ref[...] + (1.0 - ADAM_B1) * gv
        vn = ADAM_B2 * v_ref[...] + (1.0 - ADAM_B2) * (gv * gv)
        m_hat = mn / (1.0 - ADAM_B1 ** ADAM_STEP)
        v_hat = vn / (1.0 - ADAM_B2 ** ADAM_STEP)
        d_ref[...] = -ADAM_LR * (m_hat / (jnp.sqrt(v_hat) + ADAM_EPS) + ADAM_WD * w_ref[...])
        mo_ref[...] = mn
        vo_ref[...] = vn

    spec = pl.BlockSpec((rows, 256), lambda i: (0, i)) if by_cols else _row_spec(rb, cols)
    sds = jax.ShapeDtypeStruct((rows, cols), F32)
    d, mo, vo = pl.pallas_call(
        body, name=name, grid=(cols // 256 if by_cols else rows // rb,), in_specs=[spec] * 4, out_specs=[spec] * 3, out_shape=[sds] * 3,
        compiler_params=_params(("parallel",)),
    )(w2, g2, m2, v2)
    return d.reshape(shape), mo.reshape(shape), vo.reshape(shape)


def kernel(x, p, norm_w, w_in, s5_A_re, s5_A_im, s5_log_dt, s5_B_re, s5_B_im, s5_C_re, s5_C_im, s5_D, s5_w_glu, s5_b_glu, conv_w, conv_b, dt_bias, A_log, ssd_D, ssd_norm_w, w_out, ple_norm_w, w_ple_gate, w_ple_proj, final_norm_w, loss_target, m_norm_w, m_w_in, m_s5_A_re, m_s5_A_im, m_s5_log_dt, m_s5_B_re, m_s5_B_im, m_s5_C_re, m_s5_C_im, m_s5_D, m_s5_w_glu, m_s5_b_glu, m_conv_w, m_conv_b, m_dt_bias, m_A_log, m_ssd_D, m_ssd_norm_w, m_w_out, m_ple_norm_w, m_w_ple_gate, m_w_ple_proj, m_final_norm_w, v_norm_w, v_w_in, v_s5_A_re, v_s5_A_im, v_s5_log_dt, v_s5_B_re, v_s5_B_im, v_s5_C_re, v_s5_C_im, v_s5_D, v_s5_w_glu, v_s5_b_glu, v_conv_w, v_conv_b, v_dt_bias, v_A_log, v_ssd_D, v_ssd_norm_w, v_w_out, v_ple_norm_w, v_w_ple_gate, v_w_ple_proj, v_final_norm_w):
    given = (norm_w, w_in, s5_A_re, s5_A_im, s5_log_dt, s5_B_re, s5_B_im, s5_C_re, s5_C_im, s5_D, s5_w_glu, s5_b_glu, conv_w, conv_b,
             dt_bias, A_log, ssd_D, ssd_norm_w, w_out, ple_norm_w, w_ple_gate, w_ple_proj, final_norm_w)
    given_m = (m_norm_w, m_w_in, m_s5_A_re, m_s5_A_im, m_s5_log_dt, m_s5_B_re, m_s5_B_im, m_s5_C_re, m_s5_C_im, m_s5_D, m_s5_w_glu,
               m_s5_b_glu, m_conv_w, m_conv_b, m_dt_bias, m_A_log, m_ssd_D, m_ssd_norm_w, m_w_out, m_ple_norm_w, m_w_ple_gate,
               m_w_ple_proj, m_final_norm_w)
    given_v = (v_norm_w, v_w_in, v_s5_A_re, v_s5_A_im, v_s5_log_dt, v_s5_B_re, v_s5_B_im, v_s5_C_re, v_s5_C_im, v_s5_D, v_s5_w_glu,
               v_s5_b_glu, v_conv_w, v_conv_b, v_dt_bias, v_A_log, v_ssd_D, v_ssd_norm_w, v_w_out, v_ple_norm_w, v_w_ple_gate,
               v_w_ple_proj, v_final_norm_w)
    wts, mom, var = dict(zip(WEIGHTS, given)), dict(zip(WEIGHTS, given_m)), dict(zip(WEIGHTS, given_v))
    drop = lambda n, a: a if n == "final_norm_w" else a[0]

    shard2d = lambda n, a: a[0].T if n == "w_in" else drop(n, a)
    wpack = jnp.concatenate([shard2d(n, wts[n]).astype(BF).reshape(-1, LANES) for n in BIG], axis=0)
    wall, cwall = _all_gather_chips(wpack, drop("conv_w", wts["conv_w"]))
    chip = 2 * lax.axis_index("x") + lax.axis_index("y")
    full, r0 = {}, 0
    for n, (shp, axis) in BIG.items():
        blk = lax.dynamic_update_slice_in_dim(wall[:, r0:r0 + BIG_ROWS[n]].reshape((N_CHIPS,) + shp),
                                              shard2d(n, wts[n]).astype(BF)[None], chip, axis=0)
        full[n] = blk.reshape(N_CHIPS * shp[0], shp[1]) if axis == 0 else blk.transpose(1, 0, 2).reshape(shp[0], N_CHIPS * shp[1])
        r0 += BIG_ROWS[n]
    for n in SMALL:
        full[n] = drop(n, wts[n])
    full["conv_w"] = cwall.transpose(1, 0, 2).reshape(4, 2048)

    loss, gx, grads = _local_step(x[0], p[0, 0], loss_target[0], full)
    loss = lax.psum(loss, MESH_AXES)

    gp = _pack_grads({n: grads[n].reshape(SMALL[n]) if n in SMALL else grads[n] for n in WEIGHTS})
    c = lax.axis_index("c")
    from_sibling = _exchange_pair(gp)
    mine = lax.dynamic_index_in_dim(gp, c, axis=1, keepdims=False)
    ps_bf, ps_tail = _pair_sum(mine, from_sibling)
    others_bf, others_tail = _exchange_chips(ps_bf, ps_tail)
    own_bf = lax.dynamic_index_in_dim(ps_bf, chip, axis=0, keepdims=False)
    own_tail = lax.dynamic_index_in_dim(ps_tail, chip, axis=0, keepdims=False)
    reduced_half = _chip_sum(own_bf, own_tail, others_bf, others_tail)
    sibling_half = _swap_reduced_halves(reduced_half)
    first_half = jnp.where(c == 0, reduced_half, sibling_half)
    second_half = jnp.where(c == 0, sibling_half, reduced_half)
    sm = _gather_small(second_half)
    g_final = {**_unpack_small(sm), **_unpack_shard(first_half, second_half)}
    g_final["conv_w"] = lax.dynamic_slice_in_dim(g_final["conv_w"], chip * 512, 512, axis=1)

    outs_g, outs_d, outs_m, outs_v = [], [], [], []
    for n in WEIGHTS:
        if n == "w_in":
            res = _adamw(wts[n][0].T, g_final[n], mom[n][0].T, var[n][0].T, "adamw_" + n)
            g, d, mo, vo = (a.T[None] for a in (g_final[n],) + res)
        else:
            g = g_final[n].reshape(wts[n].shape)
            d, mo, vo = _adamw(wts[n], g, mom[n], var[n], "adamw_" + n)
        outs_g.append(g)
        outs_d.append(d)
        outs_m.append(mo)
        outs_v.append(vo)
    return (loss, gx[None], *outs_g, *outs_d, *outs_m, *outs_v)
```
